```python
import math
import jax, jax.numpy as jnp
from jax import lax
import numpy as np

D_MODEL = 1024
BATCH = 32
SEQ = 256
DEPTH = 2
DEC_BATCH = 4
DEC_SEQ = 2048
PAST_LEN = 512

GRID_W = 64
HEAD_DIM = 64
BRANCH_W = 256
N_BRANCH = 4
BLK = 128
MLA_H = 4
MLA_Q_LORA = 256
MLA_KV_LORA = 128
MLA_NOPE = 64
MLA_ROPE = 32
MLA_V = 64
MLA_SCALE = (MLA_NOPE + MLA_ROPE) ** -0.5
HY_C = 256
HY_ORDER = 2
HY_BANDS = 8
HY_EMB = 1 + 2 * HY_BANDS
HY_FF = 64
HY_SHORT = 3
HY_MIN_DECAY = 4.605 / 1.5
HY_MAX_DECAY = 4.605 / 0.3
WIN_H = 4
WIN_KV = 2
WIN = 128
NA_H = 4
NA_KH = 8
NA_KW = 16
NA_QCB = 16
NA_KCB = 2 * NA_QCB
N_EXPERTS = 64
TOP_K = 8
N_GROUPS = 8
TOPK_GROUPS = 4
D_EXPERT = 256
D_SHARED = 256
ROUTED_SCALE = 2.5
MOE_CHUNK = 128
ROPE_BASE = 10000.0
LN_EPS = 1e-5
RMS_EPS = 1e-6
NEG = -1e30
DN_ALPHA = (2 * DEPTH) ** 0.25
DN_BETA = (8 * DEPTH) ** -0.25

OFF_MLA_Q = 0
OFF_MLA_KV = OFF_MLA_Q + MLA_Q_LORA
OFF_MLA_KR = OFF_MLA_KV + MLA_KV_LORA
OFF_HY = OFF_MLA_KR + MLA_ROPE
OFF_WIN_Q = OFF_HY + 3 * HY_C
OFF_WIN_K = OFF_WIN_Q + WIN_H * HEAD_DIM
OFF_WIN_V = OFF_WIN_K + WIN_KV * HEAD_DIM
OFF_NA_Q = OFF_WIN_V + WIN_KV * HEAD_DIM
OFF_NA_K = OFF_NA_Q + NA_H * HEAD_DIM
OFF_NA_V = OFF_NA_K + NA_H * HEAD_DIM
OFF_GATE = OFF_NA_V + NA_H * HEAD_DIM
N_IN = OFF_GATE + N_BRANCH * D_MODEL

kernel_name = "hybrid_diffusion_mla_hyena_swa_natten_moe_step"


def layer_norm(x, g, b):
    xf = x.astype(jnp.float32)
    mu = jnp.mean(xf, -1, keepdims=True)
    var = jnp.mean(jnp.square(xf - mu), -1, keepdims=True)
    return ((xf - mu) * lax.rsqrt(var + LN_EPS) * g + b).astype(x.dtype)


def rms_norm(x, g):
    xf = x.astype(jnp.float32)
    return (xf * lax.rsqrt(jnp.mean(xf * xf, -1, keepdims=True) + RMS_EPS) * g).astype(x.dtype)


def rope_1d(x, pos):
    half = x.shape[-1] // 2
    inv = ROPE_BASE ** (-jnp.arange(half, dtype=jnp.float32) / half)
    ang = pos.astype(jnp.float32)[:, None] * inv[None, :]
    cos = jnp.cos(ang)[None, :, None, :]
    sin = jnp.sin(ang)[None, :, None, :]
    xf = x.astype(jnp.float32)
    x1, x2 = xf[..., :half], xf[..., half:]
    return jnp.concatenate([x1 * cos - x2 * sin, x1 * sin + x2 * cos], -1).astype(x.dtype)


def rope_2d(x):
    L = x.shape[1]
    t = jnp.arange(L)
    d = x.shape[-1] // 2
    return jnp.concatenate([rope_1d(x[..., :d], t // GRID_W), rope_1d(x[..., d:], t % GRID_W)], -1)


def dense_attention(q, k, v, scale, sink=None):
    B, Lq, H, dk = q.shape
    nq = Lq // BLK
    qb = q.reshape(B, nq, BLK, H, dk).transpose(1, 0, 2, 3, 4)

    def one_block(qblk):
        s = jnp.einsum('bqhd,bkhd->bhqk', qblk, k).astype(jnp.float32) * scale
        if sink is not None:
            sk = jnp.broadcast_to(sink.astype(jnp.float32)[None, :, None, None], s.shape[:3] + (1,))
            s = jnp.concatenate([s, sk], -1)
        p = jax.nn.softmax(s, axis=-1)
        if sink is not None:
            p = p[..., :-1]
        return jnp.einsum('bhqk,bkhd->bqhd', p.astype(v.dtype), v)

    out = lax.map(one_block, qb)
    return out.transpose(1, 0, 2, 3, 4).reshape(B, Lq, H, v.shape[-1])


def mla_project(proj, lp, latent):
    B, L, _ = proj.shape
    q_lat = rms_norm(proj[..., OFF_MLA_Q:OFF_MLA_KV], lp['mla_q_norm'])
    ckv = rms_norm(proj[..., OFF_MLA_KV:OFF_MLA_KR], lp['mla_kv_norm'])
    kpe = proj[..., OFF_MLA_KR:OFF_HY]
    q = (q_lat @ lp['mla_w_uq']).reshape(B, L, MLA_H, MLA_NOPE + MLA_ROPE)
    q_nope, q_pe = q[..., :MLA_NOPE], q[..., MLA_NOPE:]
    if latent:
        q_pe = rope_2d(q_pe)
        kpe = rope_2d(kpe[:, :, None, :])[:, :, 0]
    return jnp.concatenate([q_nope, q_pe], -1), ckv, kpe


def mla_keys(ckv, kpe, w_ukv):
    B, L, _ = ckv.shape
    kv = (ckv @ w_ukv).reshape(B, L, MLA_H, MLA_NOPE + MLA_V)
    k = jnp.concatenate([kv[..., :MLA_NOPE], jnp.broadcast_to(kpe[:, :, None, :], (B, L, MLA_H, MLA_ROPE))], -1)
    return k, kv[..., MLA_NOPE:]


def short_conv(u, w, b):
    L = u.shape[1]
    pad = HY_SHORT // 2
    up = jnp.pad(u, ((0, 0), (pad, pad), (0, 0)))
    out = up[:, 0:L] * w[0]
    for i in range(1, HY_SHORT):
        out = out + up[:, i:i + L] * w[i]
    return out + b


def hyena_filters(L, lp):
    f32 = jnp.float32
    t = jnp.arange(L, dtype=f32)
    t_norm = t / L
    fw = (2.0 * math.pi / L) * t[:, None] * jnp.arange(HY_BANDS, dtype=f32)[None, :]
    z = jnp.concatenate([t_norm[:, None], jnp.cos(fw), -jnp.sin(fw)], -1)
    freq = lp['hy_sin_freq'].astype(f32)
    a = jnp.sin(freq[0] * (z @ lp['hy_w1'].astype(f32) + lp['hy_b1'].astype(f32)))
    a = jnp.sin(freq[1] * (a @ lp['hy_w2'].astype(f32) + lp['hy_b2'].astype(f32)))
    filt = (a @ lp['hy_w3'].astype(f32)).reshape(L, HY_ORDER, 2, HY_C)
    filt = filt * jnp.exp(-t_norm[:, None, None, None] * jnp.exp(lp['hy_log_decay'].astype(f32))[None])
    fwd, bwd = filt[:, :, 0], filt[:, :, 1]
    two_sided = jnp.concatenate([fwd, jnp.zeros_like(fwd[:1]), bwd[1:][::-1]], 0)
    return jnp.fft.rfft(two_sided, axis=0)


def fft_long_conv(z, gf):
    L = z.shape[1]
    zf = jnp.fft.rfft(z, n=2 * L, axis=1)
    return jnp.fft.irfft(zf * gf[None], n=2 * L, axis=1)[:, :L]


def hyena_mixer(proj, lp):
    L = proj.shape[1]
    u = short_conv(proj[..., OFF_HY:OFF_WIN_Q], lp['hy_conv_w'], lp['hy_conv_b']).astype(jnp.float32)
    v, x1, x2 = jnp.split(u, 3, axis=-1)
    gf = hyena_filters(L, lp)
    skip = lp['hy_skip'].astype(jnp.float32)
    z = v
    for n, gate in enumerate((x1, x2)):
        z = gate * (fft_long_conv(z, gf[:, n]) + skip[n] * z)
    return z.astype(proj.dtype)


def window_qkv(proj):
    B, L, _ = proj.shape
    q = proj[..., OFF_WIN_Q:OFF_WIN_K].reshape(B, L, WIN_H, HEAD_DIM)
    k = proj[..., OFF_WIN_K:OFF_WIN_V].reshape(B, L, WIN_KV, HEAD_DIM)
    v = proj[..., OFF_WIN_V:OFF_NA_Q].reshape(B, L, WIN_KV, HEAD_DIM)
    return q, k, v


def na_qkv(proj):
    B, L, _ = proj.shape
    q = proj[..., OFF_NA_Q:OFF_NA_K].reshape(B, L, NA_H, HEAD_DIM)
    k = proj[..., OFF_NA_K:OFF_NA_V].reshape(B, L, NA_H, HEAD_DIM)
    v = proj[..., OFF_NA_V:OFF_GATE].reshape(B, L, NA_H, HEAD_DIM)
    return q, k, v


def window_attention_latent(q, k, v, k_ctx, v_ctx, sink):
    B, L, H, dh = q.shape
    G = WIN_KV
    R = H // G
    nb = L // BLK
    scale = dh ** -0.5
    qb = q.reshape(B, nb, BLK, G, R, dh)
    pad = ((0, 0), (BLK, BLK), (0, 0), (0, 0))
    kp = jnp.pad(k, pad).reshape(B, nb + 2, BLK, G, dh)
    vp = jnp.pad(v, pad).reshape(B, nb + 2, BLK, G, dh)
    kband = jnp.concatenate([kp[:, :-2], kp[:, 1:-1], kp[:, 2:]], axis=2)
    vband = jnp.concatenate([vp[:, :-2], vp[:, 1:-1], vp[:, 2:]], axis=2)
    s_loc = jnp.einsum('bnqgrd,bnkgd->bngrqk', qb, kband).astype(jnp.float32) * scale
    blk = jnp.arange(nb)[:, None] * BLK
    qpos = blk + jnp.arange(BLK)[None]
    kpos = blk - BLK + jnp.arange(3 * BLK)[None]
    valid = (jnp.abs(qpos[:, :, None] - kpos[:, None, :]) <= WIN) & ((kpos >= 0) & (kpos < L))[:, None, :]
    s_loc = jnp.where(valid[None, :, None, None], s_loc, NEG)
    s_ctx = jnp.einsum('bnqgrd,bcgd->bngrqc', qb, k_ctx).astype(jnp.float32) * scale
    s_sink = jnp.broadcast_to(sink.astype(jnp.float32).reshape(G, R)[None, None, :, :, None, None], s_ctx.shape[:-1] + (1,))
    p = jax.nn.softmax(jnp.concatenate([s_loc, s_ctx, s_sink], -1), axis=-1)
    nk = 3 * BLK
    Lc = k_ctx.shape[1]
    out = (jnp.einsum('bngrqk,bnkgd->bnqgrd', p[..., :nk].astype(v.dtype), vband)
           + jnp.einsum('bngrqc,bcgd->bnqgrd', p[..., nk:nk + Lc].astype(v.dtype), v_ctx))
    return out.reshape(B, L, H, dh)


def neighbourhood_attention_latent(q, k, v, k_ctx, v_ctx, rpb):
    B, L, H, dh = q.shape
    rows = L // GRID_W
    kh = min(NA_KH, rows)
    ncb = GRID_W // NA_QCB
    scale = dh ** -0.5
    r = jnp.arange(rows)
    ridx = jnp.clip(r - kh // 2, 0, rows - kh)[:, None] + jnp.arange(kh)[None]
    j = jnp.arange(ncb)
    cidx = jnp.clip(j * NA_QCB - NA_KW // 2, 0, GRID_W - NA_KCB)[:, None] + jnp.arange(NA_KCB)[None]
    gi = (slice(None), ridx[:, None, :, None], cidx[None, :, None, :])
    kg = k.reshape(B, rows, GRID_W, H, dh)[gi]
    vg = v.reshape(B, rows, GRID_W, H, dh)[gi]
    qg = q.reshape(B, rows, ncb, NA_QCB, H, dh)
    s_loc = jnp.einsum('brjqhd,brjaxhd->brjhqax', qg, kg).astype(jnp.float32) * scale
    qcol = j[:, None] * NA_QCB + jnp.arange(NA_QCB)[None]
    wstart = jnp.clip(qcol - NA_KW // 2, 0, GRID_W - NA_KW)[..., None]
    kc = cidx[:, None, :]
    col_ok = (kc >= wstart) & (kc < wstart + NA_KW)
    dr = ridx - r[:, None] + (NA_KH - 1)
    dc = jnp.clip(kc - qcol[..., None] + (NA_KW - 1), 0, 2 * NA_KW - 2)
    bias = rpb.astype(jnp.float32)[:, dr[:, None, None, :, None], dc[None, :, :, None, :]]
    s_loc = s_loc + jnp.moveaxis(bias, 0, 2)[None]
    s_loc = jnp.where(col_ok[None, None, :, None, :, None, :], s_loc, NEG)
    s_ctx = jnp.einsum('brjqhd,bchd->brjhqc', qg, k_ctx).astype(jnp.float32) * scale
    nloc = kh * NA_KCB
    s = jnp.concatenate([s_loc.reshape(B, rows, ncb, H, NA_QCB, nloc), s_ctx], -1)
    p = jax.nn.softmax(s, axis=-1)
    p_loc = p[..., :nloc].reshape(B, rows, ncb, H, NA_QCB, kh, NA_KCB).astype(v.dtype)
    out = (jnp.einsum('brjhqax,brjaxhd->brjqhd', p_loc, vg)
           + jnp.einsum('brjhqc,bchd->brjqhd', p[..., nloc:].astype(v.dtype), v_ctx))
    return out.reshape(B, L, H, dh)


def merge_branches(proj, outs, lp):
    B, L, _ = proj.shape
    gates = jax.nn.sigmoid(proj[..., OFF_GATE:].reshape(B, L, N_BRANCH, D_MODEL).astype(jnp.float32)).astype(proj.dtype)
    o = jnp.stack([t.reshape(B, L, BRANCH_W) for t in outs], axis=2)
    br = jnp.einsum('blie,ied->blid', o, lp['w_branch'])
    return jnp.sum(gates * br, axis=2) @ lp['w_out']


def context_mixer(h, lp):
    proj = h @ lp['w_in']
    qa, ckv, kpe = mla_project(proj, lp, latent=False)
    ka, va = mla_keys(ckv, kpe, lp['mla_w_ukv'])
    oa = dense_attention(qa, ka, va, MLA_SCALE)
    ob = hyena_mixer(proj, lp)
    qc, kc, vc = window_qkv(proj)
    rep = WIN_H // WIN_KV
    oc = dense_attention(qc, jnp.repeat(kc, rep, axis=2), jnp.repeat(vc, rep, axis=2), HEAD_DIM ** -0.5, lp['win_sink'])
    qd, kd, vd = na_qkv(proj)
    od = dense_attention(qd, kd, vd, HEAD_DIM ** -0.5)
    return merge_branches(proj, (oa, ob, oc, od), lp), (ckv, kpe, kc, vc, kd, vd)


def latent_mixer(h, cache, lp):
    ckv_c, kpe_c, kc_c, vc_c, kd_c, vd_c = cache
    proj = h @ lp['w_in']
    qa, ckv, kpe = mla_project(proj, lp, latent=True)
    ka, va = mla_keys(jnp.concatenate([ckv_c, ckv], 1), jnp.concatenate([kpe_c, kpe], 1), lp['mla_w_ukv'])
    oa = dense_attention(qa, ka, va, MLA_SCALE)
    ob = hyena_mixer(proj, lp)
    qc, kc, vc = window_qkv(proj)
    oc = window_attention_latent(rope_2d(qc), rope_2d(kc), vc, kc_c, vc_c, lp['win_sink'])
    qd, kd, vd = na_qkv(proj)
    od = neighbourhood_attention_latent(qd, kd, vd, kd_c, vd_c, lp['na_rpb'])
    return merge_branches(proj, (oa, ob, oc, od), lp)


def moe_ffn(h, lp):
    B, L, D = h.shape
    t = h.reshape(-1, D)
    nc = t.shape[0] // MOE_CHUNK
    per_group = N_EXPERTS // N_GROUPS

    def routed(tc):
        scores = jax.nn.sigmoid((tc @ lp['moe_router']).astype(jnp.float32))
        sel = scores + lp['moe_bias'].astype(jnp.float32)
        gscore = jnp.sum(lax.top_k(sel.reshape(-1, N_GROUPS, per_group), 2)[0], -1)
        gmask = jnp.sum(jax.nn.one_hot(lax.top_k(gscore, TOPK_GROUPS)[1], N_GROUPS, dtype=jnp.float32), 1)
        sel = jnp.where(jnp.repeat(gmask, per_group, axis=1) > 0, sel, NEG)
        eidx = lax.top_k(sel, TOP_K)[1]
        w = jnp.take_along_axis(scores, eidx, 1)
        w = w / jnp.sum(w, -1, keepdims=True) * ROUTED_SCALE
        gate = jnp.einsum('ck,cke->ce', w, jax.nn.one_hot(eidx, N_EXPERTS, dtype=jnp.float32))
        a = jnp.einsum('cd,edf->cef', tc, lp['moe_w1'])
        b = jnp.einsum('cd,edf->cef', tc, lp['moe_w3'])
        hid = jax.nn.silu(a) * b * gate[..., None].astype(tc.dtype)
        return jnp.einsum('cef,efd->cd', hid, lp['moe_w2'])

    out = lax.map(routed, t.reshape(nc, MOE_CHUNK, D)).reshape(-1, D)
    shared = (jax.nn.silu(t @ lp['sh_w1']) * (t @ lp['sh_w3'])) @ lp['sh_w2']
    return (out + shared).reshape(B, L, D)


def modulation(cvec, lp):
    m = jax.nn.silu(cvec) @ lp['w_ada'] + lp['b_ada']
    return m.reshape(cvec.shape[0], 1, 6, D_MODEL)


def trunk_layer(x, mod, mixer_fn, lp):
    h = x * (1 + mod[:, :, 1]) + mod[:, :, 0]
    mix, extra = mixer_fn(h)
    x = layer_norm(DN_ALPHA * x + mod[:, :, 2] * mix, lp['ln1_g'], lp['ln1_b'])
    h = x * (1 + mod[:, :, 4]) + mod[:, :, 3]
    x = layer_norm(DN_ALPHA * x + mod[:, :, 5] * moe_ffn(h, lp), lp['ln2_g'], lp['ln2_b'])
    return x, extra


def setup_inputs(seed: int = 0) -> dict:
    key = jax.random.key(seed)
    ks = list(jax.random.split(key, 48))
    f32 = jnp.float32

    def nrm(i, shape, scale=1.0):
        return jax.random.normal(ks[i], shape, f32) * scale

    gate_off = jnp.repeat(jnp.array([0., 0., 1., 0., 0., 1.], f32), D_MODEL)
    decay0 = jnp.log(jnp.linspace(HY_MIN_DECAY, HY_MAX_DECAY, HY_C, dtype=f32))
    Dm = D_MODEL
    return {
        'x_prompt': nrm(0, (BATCH, SEQ, Dm)),
        'x_sample': nrm(1, (DEC_BATCH, DEC_SEQ, Dm)),
        'cache_mla_ckv': nrm(2, (DEC_BATCH, DEPTH, PAST_LEN, MLA_KV_LORA)),
        'cache_mla_kpe': nrm(3, (DEC_BATCH, DEPTH, PAST_LEN, MLA_ROPE)),
        'cache_win_k': nrm(4, (DEC_BATCH, DEPTH, PAST_LEN, WIN_KV, HEAD_DIM)),
        'cache_win_v': nrm(5, (DEC_BATCH, DEPTH, PAST_LEN, WIN_KV, HEAD_DIM)),
        'cache_na_k': nrm(6, (DEC_BATCH, DEPTH, PAST_LEN, NA_H, HEAD_DIM)),
        'cache_na_v': nrm(7, (DEC_BATCH, DEPTH, PAST_LEN, NA_H, HEAD_DIM)),
        'c': nrm(8, (DEC_BATCH, Dm)),
        'c_ctx': nrm(9, (Dm,)),
        'w_ada': nrm(10, (DEPTH, Dm, 6 * Dm), 0.1 * Dm ** -0.5),
        'b_ada': nrm(11, (DEPTH, 6 * Dm), 0.02) + gate_off,
        'w_in': nrm(12, (DEPTH, Dm, N_IN), Dm ** -0.5),
        'mla_q_norm': 1.0 + nrm(13, (DEPTH, MLA_Q_LORA), 0.02),
        'mla_kv_norm': 1.0 + nrm(14, (DEPTH, MLA_KV_LORA), 0.02),
        'mla_w_uq': nrm(15, (DEPTH, MLA_Q_LORA, MLA_H * (MLA_NOPE + MLA_ROPE)), MLA_Q_LORA ** -0.5),
        'mla_w_ukv': nrm(16, (DEPTH, MLA_KV_LORA, MLA_H * (MLA_NOPE + MLA_V)), MLA_KV_LORA ** -0.5),
        'hy_conv_w': nrm(17, (DEPTH, HY_SHORT, 3 * HY_C), HY_SHORT ** -0.5),
        'hy_conv_b': nrm(18, (DEPTH, 3 * HY_C), 0.02),
        'hy_w1': nrm(19, (DEPTH, HY_EMB, HY_FF), HY_EMB ** -0.5),
        'hy_b1': nrm(20, (DEPTH, HY_FF), 0.02),
        'hy_w2': nrm(21, (DEPTH, HY_FF, HY_FF), HY_FF ** -0.5),
        'hy_b2': nrm(22, (DEPTH, HY_FF), 0.02),
        'hy_w3': nrm(23, (DEPTH, HY_FF, HY_ORDER * 2 * HY_C), 0.05 * HY_FF ** -0.5),
        'hy_sin_freq': 1.0 + nrm(24, (DEPTH, 2, HY_FF), 0.1),
        'hy_log_decay': decay0 + nrm(25, (DEPTH, HY_ORDER, 2, HY_C), 0.05),
        'hy_skip': nrm(26, (DEPTH, HY_ORDER, HY_C)),
        'win_sink': nrm(27, (DEPTH, WIN_H), 0.5),
        'na_rpb': nrm(28, (DEPTH, NA_H, 2 * NA_KH - 1, 2 * NA_KW - 1), 0.1),
        'w_branch': nrm(29, (DEPTH, N_BRANCH, BRANCH_W, Dm), BRANCH_W ** -0.5),
        'w_out': nrm(30, (DEPTH, Dm, Dm), DN_BETA * Dm ** -0.5),
        'ln1_g': 1.0 + nrm(31, (DEPTH, Dm), 0.02),
        'ln1_b': nrm(32, (DEPTH, Dm), 0.02),
        'ln2_g': 1.0 + nrm(33, (DEPTH, Dm), 0.02),
        'ln2_b': nrm(34, (DEPTH, Dm), 0.02),
        'moe_router': nrm(35, (DEPTH, Dm, N_EXPERTS), Dm ** -0.5),
        'moe_bias': nrm(36, (DEPTH, N_EXPERTS), 0.01),
        'moe_w1': nrm(37, (DEPTH, N_EXPERTS, Dm, D_EXPERT), Dm ** -0.5),
        'moe_w3': nrm(38, (DEPTH, N_EXPERTS, Dm, D_EXPERT), Dm ** -0.5),
        'moe_w2': nrm(39, (DEPTH, N_EXPERTS, D_EXPERT, Dm), DN_BETA * D_EXPERT ** -0.5),
        'sh_w1': nrm(40, (DEPTH, Dm, D_SHARED), Dm ** -0.5),
        'sh_w3': nrm(41, (DEPTH, Dm, D_SHARED), Dm ** -0.5),
        'sh_w2': nrm(42, (DEPTH, D_SHARED, Dm), DN_BETA * D_SHARED ** -0.5),
    }


def reference(x_prompt, x_sample, cache_mla_ckv, cache_mla_kpe, cache_win_k, cache_win_v, cache_na_k, cache_na_v,
              c, c_ctx, w_ada, b_ada, w_in, mla_q_norm, mla_kv_norm, mla_w_uq, mla_w_ukv,
              hy_conv_w, hy_conv_b, hy_w1, hy_b1, hy_w2, hy_b2, hy_w3, hy_sin_freq, hy_log_decay, hy_skip,
              win_sink, na_rpb, w_branch, w_out, ln1_g, ln1_b, ln2_g, ln2_b,
              moe_router, moe_bias, moe_w1, moe_w3, moe_w2, sh_w1, sh_w3, sh_w2):
    xp = x_prompt
    xs = x_sample
    new_ckv, new_kpe, new_wk, new_wv, new_nk, new_nv = [], [], [], [], [], []
    for l in range(DEPTH):
        lp = dict(w_ada=w_ada[l], b_ada=b_ada[l], w_in=w_in[l], mla_q_norm=mla_q_norm[l], mla_kv_norm=mla_kv_norm[l],
                  mla_w_uq=mla_w_uq[l], mla_w_ukv=mla_w_ukv[l], hy_conv_w=hy_conv_w[l], hy_conv_b=hy_conv_b[l],
                  hy_w1=hy_w1[l], hy_b1=hy_b1[l], hy_w2=hy_w2[l], hy_b2=hy_b2[l], hy_w3=hy_w3[l],
                  hy_sin_freq=hy_sin_freq[l], hy_log_decay=hy_log_decay[l], hy_skip=hy_skip[l],
                  win_sink=win_sink[l], na_rpb=na_rpb[l], w_branch=w_branch[l], w_out=w_out[l],
                  ln1_g=ln1_g[l], ln1_b=ln1_b[l], ln2_g=ln2_g[l], ln2_b=ln2_b[l],
                  moe_router=moe_router[l], moe_bias=moe_bias[l], moe_w1=moe_w1[l], moe_w3=moe_w3[l],
                  moe_w2=moe_w2[l], sh_w1=sh_w1[l], sh_w3=sh_w3[l], sh_w2=sh_w2[l])
        mod_ctx = modulation(c_ctx[None, :], lp)
        xp, ctx_t = trunk_layer(xp, mod_ctx, lambda h: context_mixer(h, lp), lp)
        ckv, kpe, kc, vc, kd, vd = ctx_t
        new_ckv.append(ckv)
        new_kpe.append(kpe)
        new_wk.append(kc)
        new_wv.append(vc)
        new_nk.append(kd)
        new_nv.append(vd)
        cache = (cache_mla_ckv[:, l], cache_mla_kpe[:, l], cache_win_k[:, l], cache_win_v[:, l],
                 cache_na_k[:, l], cache_na_v[:, l])
        mod_lat = modulation(c, lp)
        xs, _ = trunk_layer(xs, mod_lat, lambda h: (latent_mixer(h, cache, lp), None), lp)
    return (xp, xs, jnp.stack(new_ckv, 1), jnp.stack(new_kpe, 1), jnp.stack(new_wk, 1), jnp.stack(new_wv, 1),
            jnp.stack(new_nk, 1), jnp.stack(new_nv, 1))
```

```python
import functools
import math

import jax
import jax.numpy as jnp
from jax import lax
from jax.experimental import pallas as pl
from jax.experimental.pallas import tpu as pltpu

F32 = jnp.float32
BF16 = jnp.bfloat16

D = 1024
DEPTH = 2
GRID_W = 64
HEAD_DIM = 64
MLA_SCALE = 96 ** -0.5
ATT_SCALE = HEAD_DIM ** -0.5
HY_C = 256
HY_BANDS = 8
NA_KH = 8
NA_KW = 16
N_EXPERTS = 64
N_GROUPS = 8
TOP_K = 8
TOPK_GROUPS = 4
D_EXPERT = 256
ROUTED_SCALE = 2.5
ROPE_BASE = 10000.0
LN_EPS = 1e-5
RMS_EPS = 1e-6
NEG = -1e30
DN_ALPHA = (2 * DEPTH) ** 0.25

P_QLAT, P_CKV, P_KPE, P_HY = 0, 256, 384, 512
P_WQ, P_WK, P_WV = 1280, 1536, 1664
P_NQ, P_NK, P_NV = 1792, 2048, 2304
P_WQR, P_WKR, P_KPER, P_GATE = 2560, 2816, 2944, 3072
N_PROJ = 7168

VMEM_LIMIT = 56 * 1024 * 1024


def _cp(*sem):
    return pltpu.CompilerParams(dimension_semantics=sem, vmem_limit_bytes=VMEM_LIMIT)


def _sigmoid(x):
    return 1.0 / (1.0 + jnp.exp(-x))


def _dot(a, b):
    return jnp.dot(a, b, preferred_element_type=F32)


def _dot_nt(a, b):
    return lax.dot_general(a, b, (((1,), (1,)), ((), ())), preferred_element_type=F32)


def _dot_hi(a, b):
    return jnp.dot(a, b, preferred_element_type=F32, precision=lax.Precision.HIGHEST)


def _layer_norm(x, g, b):
    mu = jnp.mean(x, -1, keepdims=True)
    xc = x - mu
    var = jnp.mean(xc * xc, -1, keepdims=True)
    return xc * lax.rsqrt(var + LN_EPS) * g + b


def _rms_norm(x, g):
    return x * lax.rsqrt(jnp.mean(x * x, -1, keepdims=True) + RMS_EPS) * g


def _mod_kernel(c_ref, w_ref, b_ref, o_ref):
    c = c_ref[...]
    a = (c * _sigmoid(c)).astype(BF16)
    o_ref[...] = _dot(a, w_ref[...].astype(BF16)) + b_ref[...]


def _modulation(cvec, w_ada, b_ada, l):
    out = pl.pallas_call(
        _mod_kernel,
        grid=(6,),
        in_specs=[pl.BlockSpec((8, D), lambda j: (0, 0)),
                  pl.BlockSpec((None, D, D), lambda j: (l, 0, j)),
                  pl.BlockSpec((None, 1, D), lambda j: (l, 0, j))],
        out_specs=pl.BlockSpec((8, D), lambda j: (0, j)),
        out_shape=jax.ShapeDtypeStruct((8, 6 * D), F32),
        compiler_params=_cp("arbitrary"),
        name="modulation",
    )(cvec, w_ada, b_ada.reshape(DEPTH, 1, 6 * D))
    return out.reshape(8, 6, D)


def _inproj_kernel(x_ref, mod_ref, w_ref, o_ref):
    m = mod_ref[...]
    h = x_ref[...] * (1.0 + m[1:2, :]) + m[0:1, :]
    o_ref[...] = _dot(h.astype(BF16), w_ref[...])


def _in_proj(x, mod, w_p, mod_row, bm):
    T = x.shape[0]
    bn = 1024
    return pl.pallas_call(
        _inproj_kernel,
        grid=(N_PROJ // bn, T // bm),
        in_specs=[pl.BlockSpec((bm, D), lambda j, i: (i, 0)),
                  pl.BlockSpec((None, 6, D), lambda j, i: (mod_row(i), 0, 0)),
                  pl.BlockSpec((D, bn), lambda j, i: (0, j))],
        out_specs=pl.BlockSpec((bm, bn), lambda j, i: (i, j)),
        out_shape=jax.ShapeDtypeStruct((T, N_PROJ), F32),
        compiler_params=_cp("arbitrary", "arbitrary"),
        name="in_proj",
    )(x, mod, w_p)


def _mla_q_kernel(*refs, rope):
    if rope:
        (ql_ref, ckv_ref, kpe_ref, kper_ref, gq_ref, gkv_ref, wc_ref, wr_ref,
         cq_ref, sq_ref, ck_ref, sk_ref, q_ref, ckvn_ref, kpeo_ref) = refs
    else:
        ql_ref, ckv_ref, kpe_ref, gq_ref, gkv_ref, wc_ref, q_ref, ckvn_ref, kpeo_ref = refs
    qn = _rms_norm(ql_ref[...], gq_ref[...]).astype(BF16)
    q = _dot(qn, wc_ref[...])
    if rope:
        q = q * cq_ref[...] + _dot(qn, wr_ref[...]) * sq_ref[...]
        kpeo_ref[...] = kpe_ref[...] * ck_ref[...] + kper_ref[...] * sk_ref[...]
    else:
        kpeo_ref[...] = kpe_ref[...]
    q_ref[...] = (q * MLA_SCALE).astype(BF16)
    ckvn_ref[...] = _rms_norm(ckv_ref[...], gkv_ref[...])


def _mla_q(proj, gq, gkv, wcat, wrot, tabs, Lb, bm):
    T = proj.shape[0]
    rope = tabs is not None
    nl = Lb // bm
    col = lambda c: (lambda i: (i, c))
    fixed = lambda i: (0, 0)
    in_specs = [pl.BlockSpec((bm, 256), col(P_QLAT // 256)),
                pl.BlockSpec((bm, 128), col(P_CKV // 128)),
                pl.BlockSpec((bm, 128), col(P_KPE // 128))]
    args = [proj, proj, proj]
    if rope:
        in_specs.append(pl.BlockSpec((bm, 128), col(P_KPER // 128)))
        args.append(proj)
    in_specs += [pl.BlockSpec((1, 256), fixed), pl.BlockSpec((1, 128), fixed), pl.BlockSpec((256, 512), fixed)]
    args += [gq, gkv, wcat]
    if rope:
        cq, sq, ck, sk = tabs
        pos = lambda i: (i % nl, 0)
        in_specs += [pl.BlockSpec((256, 512), fixed), pl.BlockSpec((bm, 512), pos), pl.BlockSpec((bm, 512), pos),
                     pl.BlockSpec((bm, 128), pos), pl.BlockSpec((bm, 128), pos)]
        args += [wrot, cq, sq, ck, sk]
    return pl.pallas_call(
        functools.partial(_mla_q_kernel, rope=rope),
        grid=(T // bm,),
        in_specs=in_specs,
        out_specs=[pl.BlockSpec((bm, 512), lambda i: (i, 0)),
                   pl.BlockSpec((bm, 128), lambda i: (i, 0)),
                   pl.BlockSpec((bm, 128), lambda i: (i, 0))],
        out_shape=[jax.ShapeDtypeStruct((T, 512), BF16),
                   jax.ShapeDtypeStruct((T, 128), F32),
                   jax.ShapeDtypeStruct((T, 128), F32)],
        compiler_params=_cp("arbitrary"),
        name="mla_q",
    )(*args)


def _mla_kv_kernel(ckv_ref, kpe_ref, wk_ref, wv_ref, k_ref, v_ref):
    c = ckv_ref[...].astype(BF16)
    kpe = kpe_ref[...]
    k_ref[...] = (_dot(c, wk_ref[...]) + jnp.concatenate([kpe] * 4, axis=1)).astype(BF16)
    v_ref[...] = _dot(c, wv_ref[...]).astype(BF16)


def _mla_kv(ckv, kpe, wk, wv, bm):
    Tk = ckv.shape[0]
    return pl.pallas_call(
        _mla_kv_kernel,
        grid=(Tk // bm,),
        in_specs=[pl.BlockSpec((bm, 128), lambda i: (i, 0)),
                  pl.BlockSpec((bm, 128), lambda i: (i, 0)),
                  pl.BlockSpec((128, 512), lambda i: (0, 0)),
                  pl.BlockSpec((128, 256), lambda i: (0, 0))],
        out_specs=[pl.BlockSpec((bm, 512), lambda i: (i, 0)),
                   pl.BlockSpec((bm, 256), lambda i: (i, 0))],
        out_shape=[jax.ShapeDtypeStruct((Tk, 512), BF16),
                   jax.ShapeDtypeStruct((Tk, 256), BF16)],
        compiler_params=_cp("arbitrary"),
        name="mla_kv",
    )(ckv, kpe, wk, wv)


def _attn_core(q, kvs, masks, sink):
    ss = []
    for (k, _), mk in zip(kvs, masks):
        s = _dot_nt(q, k)
        if mk is not None:
            s = s + mk[1] if mk[0] == "add" else jnp.where(mk[1], s, NEG)
        ss.append(s)
    m = ss[0].max(-1, keepdims=True)
    for s in ss[1:]:
        m = jnp.maximum(m, s.max(-1, keepdims=True))
    if sink is not None:
        m = jnp.maximum(m, sink)
    den = None
    acc = None
    for s, (_, v) in zip(ss, kvs):
        p = jnp.exp(s - m)
        d = p.sum(-1, keepdims=True)
        a = _dot(p.astype(BF16), v)
        den = d if den is None else den + d
        acc = a if acc is None else acc + a
    if sink is not None:
        den = den + jnp.exp(sink - m)
    return acc / den


def _ctx_attn_kernel(qm_ref, km_ref, vm_ref, wq_ref, wk_ref, wv_ref, nq_ref, nk_ref, nv_ref, sink_ref,
                     om_ref, ow_ref, on_ref):
    for h in range(4):
        q = qm_ref[:, 128 * h:128 * (h + 1)]
        k = km_ref[:, 128 * h:128 * (h + 1)]
        v = vm_ref[:, 64 * h:64 * (h + 1)]
        om_ref[:, 64 * h:64 * (h + 1)] = _attn_core(q, [(k, v)], [None], None)
    for h in range(4):
        g = h // 2
        q = (wq_ref[:, 64 * h:64 * (h + 1)] * ATT_SCALE).astype(BF16)
        k = wk_ref[:, 64 * g:64 * (g + 1)].astype(BF16)
        v = wv_ref[:, 64 * g:64 * (g + 1)].astype(BF16)
        ow_ref[:, 64 * h:64 * (h + 1)] = _attn_core(q, [(k, v)], [None], sink_ref[h])
    for h in range(4):
        q = (nq_ref[:, 64 * h:64 * (h + 1)] * ATT_SCALE).astype(BF16)
        k = nk_ref[:, 64 * h:64 * (h + 1)].astype(BF16)
        v = nv_ref[:, 64 * h:64 * (h + 1)].astype(BF16)
        on_ref[:, 64 * h:64 * (h + 1)] = _attn_core(q, [(k, v)], [None], None)


def _ctx_attention(proj, q_all, k_all, v_all, sink, NB, Lb):
    T = proj.shape[0]
    pc = lambda w, off: pl.BlockSpec((Lb, w), lambda b: (b, off // w))
    row = lambda w: pl.BlockSpec((Lb, w), lambda b: (b, 0))
    return pl.pallas_call(
        _ctx_attn_kernel,
        grid=(NB,),
        in_specs=[row(512), row(512), row(256),
                  pc(256, P_WQ), pc(128, P_WK), pc(128, P_WV),
                  pc(256, P_NQ), pc(256, P_NK), pc(256, P_NV),
                  pl.BlockSpec(memory_space=pltpu.SMEM)],
        out_specs=[row(256), row(256), row(256)],
        out_shape=[jax.ShapeDtypeStruct((T, 256), F32)] * 3,
        compiler_params=_cp("arbitrary"),
        name="ctx_attention",
    )(q_all, k_all, v_all, proj, proj, proj, proj, proj, proj, sink)


def _lat_mla_kernel(q_ref, k_ref, v_ref, o_ref):
    for h in range(4):
        q = q_ref[:, 128 * h:128 * (h + 1)]
        k = k_ref[:, 128 * h:128 * (h + 1)]
        v = v_ref[:, 64 * h:64 * (h + 1)]
        o_ref[:, 64 * h:64 * (h + 1)] = _attn_core(q, [(k, v)], [None], None)


def _lat_mla_attention(q_all, k_all, v_all, NB, Lb, Lk, tq):
    T = q_all.shape[0]
    nq = Lb // tq
    return pl.pallas_call(
        _lat_mla_kernel,
        grid=(NB, nq),
        in_specs=[pl.BlockSpec((tq, 512), lambda b, i: (b * nq + i, 0)),
                  pl.BlockSpec((Lk, 512), lambda b, i: (b, 0)),
                  pl.BlockSpec((Lk, 256), lambda b, i: (b, 0))],
        out_specs=pl.BlockSpec((tq, 256), lambda b, i: (b * nq + i, 0)),
        out_shape=jax.ShapeDtypeStruct((T, 256), F32),
        compiler_params=_cp("arbitrary", "arbitrary"),
        name="lat_mla_attention",
    )(q_all, k_all, v_all)


def _lat_win_kernel(q_ref, qr_ref, k_ref, kr_ref, v_ref, kc_ref, vc_ref, cq_ref, sq_ref, ck_ref, sk_ref,
                    sink_ref, o_ref, *, Lb):
    i = pl.program_id(1)
    start = pl.multiple_of(jnp.clip((i - 1) * 128, 0, Lb - 384), 128)
    win = pl.ds(start, 384)
    q = (q_ref[...] * cq_ref[...] + qr_ref[...] * sq_ref[...]) * ATT_SCALE
    kk = k_ref[win, :] * ck_ref[win, :] + kr_ref[win, :] * sk_ref[win, :]
    vv = v_ref[win, :]
    qpos = i * 128 + lax.broadcasted_iota(jnp.int32, (128, 384), 0)
    kpos = start + lax.broadcasted_iota(jnp.int32, (128, 384), 1)
    valid = jnp.abs(qpos - kpos) <= 128
    for h in range(4):
        g = h // 2
        sl = slice(64 * g, 64 * (g + 1))
        kvs = [(kk[:, sl].astype(BF16), vv[:, sl].astype(BF16)),
               (kc_ref[:, sl].astype(BF16), vc_ref[:, sl].astype(BF16))]
        qh = q[:, 64 * h:64 * (h + 1)].astype(BF16)
        o_ref[:, 64 * h:64 * (h + 1)] = _attn_core(qh, kvs, [("keep", valid), None], sink_ref[h])


def _lat_win_attention(proj, kc, vc, tabs, sink, NB, Lb):
    T = proj.shape[0]
    nb = Lb // 128
    Lc = kc.shape[1]
    cq, sq, ck, sk = tabs
    qspec = lambda off: pl.BlockSpec((128, 256), lambda b, i: (b * nb + i, off // 256))
    kspec = lambda off: pl.BlockSpec((Lb, 128), lambda b, i: (b, off // 128))
    cspec = pl.BlockSpec((None, Lc, 128), lambda b, i: (b, 0, 0))
    return pl.pallas_call(
        functools.partial(_lat_win_kernel, Lb=Lb),
        grid=(NB, nb),
        in_specs=[qspec(P_WQ), qspec(P_WQR), kspec(P_WK), kspec(P_WKR), kspec(P_WV), cspec, cspec,
                  pl.BlockSpec((128, 256), lambda b, i: (i, 0)), pl.BlockSpec((128, 256), lambda b, i: (i, 0)),
                  pl.BlockSpec((Lb, 128), lambda b, i: (0, 0)), pl.BlockSpec((Lb, 128), lambda b, i: (0, 0)),
                  pl.BlockSpec(memory_space=pltpu.SMEM)],
        out_specs=pl.BlockSpec((128, 256), lambda b, i: (b * nb + i, 0)),
        out_shape=jax.ShapeDtypeStruct((T, 256), F32),
        compiler_params=_cp("arbitrary", "arbitrary"),
        name="lat_win_attention",
    )(proj, proj, proj, proj, proj, kc, vc, cq, sq, ck, sk, sink)


def _na_bias_kernel(rpb_ref, o_ref):
    h = pl.program_id(0)
    qc = lax.broadcasted_iota(jnp.int32, (GRID_W, GRID_W), 0)
    kc = lax.broadcasted_iota(jnp.int32, (GRID_W, GRID_W), 1)
    dc = kc - qc + (NA_KW - 1)
    wstart = jnp.clip(qc - NA_KW // 2, 0, GRID_W - NA_KW)
    ok = (kc >= wstart) & (kc < wstart + NA_KW)
    n_dc = 2 * NA_KW - 1
    n_dr = 2 * NA_KH - 1
    tabs = []
    for dr in range(n_dr):
        t = jnp.zeros((GRID_W, GRID_W), F32)
        for j in range(n_dc):
            t = jnp.where(dc == j, rpb_ref[(h * n_dr + dr) * n_dc + j], t)
        tabs.append(jnp.where(ok, t, NEG))
    for o in range(NA_KH):
        for a in range(NA_KH):
            o_ref[o, :, GRID_W * a:GRID_W * (a + 1)] = tabs[a + NA_KH - 1 - o]


def _na_bias(rpb):
    H = rpb.shape[0]
    return pl.pallas_call(
        _na_bias_kernel,
        grid=(H,),
        in_specs=[pl.BlockSpec(memory_space=pltpu.SMEM)],
        out_specs=pl.BlockSpec((None, NA_KH, GRID_W, NA_KH * GRID_W), lambda h: (h, 0, 0, 0)),
        out_shape=jax.ShapeDtypeStruct((H, NA_KH, GRID_W, NA_KH * GRID_W), F32),
        compiler_params=_cp("arbitrary"),
        name="na_bias",
    )(rpb.reshape(-1))


def _lat_na_kernel(q_ref, k_ref, v_ref, kc_ref, vc_ref, bias_ref, o_ref, *, rows):
    r = pl.program_id(1)
    start = pl.multiple_of(jnp.clip(r - NA_KH // 2, 0, rows - NA_KH) * GRID_W, GRID_W)
    win = pl.ds(start, NA_KH * GRID_W)
    kk = k_ref[win, :]
    vv = v_ref[win, :]
    q = q_ref[...] * ATT_SCALE
    for h in range(4):
        sl = slice(64 * h, 64 * (h + 1))
        kvs = [(kk[:, sl].astype(BF16), vv[:, sl].astype(BF16)),
               (kc_ref[:, sl].astype(BF16), vc_ref[:, sl].astype(BF16))]
        o_ref[:, sl] = _attn_core(q[:, sl].astype(BF16), kvs, [("add", bias_ref[h]), None], None)


def _lat_na_attention(proj, kc, vc, bias, NB, Lb):
    T = proj.shape[0]
    rows = Lb // GRID_W
    Lc = kc.shape[1]
    kspec = lambda off: pl.BlockSpec((Lb, 256), lambda b, r: (b, off // 256))
    cspec = pl.BlockSpec((None, Lc, 256), lambda b, r: (b, 0, 0))
    bspec = pl.BlockSpec((4, None, GRID_W, NA_KH * GRID_W),
                         lambda b, r: (0, r - jnp.clip(r - NA_KH // 2, 0, rows - NA_KH), 0, 0))
    return pl.pallas_call(
        functools.partial(_lat_na_kernel, rows=rows),
        grid=(NB, rows),
        in_specs=[pl.BlockSpec((GRID_W, 256), lambda b, r: (b * rows + r, P_NQ // 256)),
                  kspec(P_NK), kspec(P_NV), cspec, cspec, bspec],
        out_specs=pl.BlockSpec((GRID_W, 256), lambda b, r: (b * rows + r, 0)),
        out_shape=jax.ShapeDtypeStruct((T, 256), F32),
        compiler_params=_cp("arbitrary", "arbitrary"),
        name="lat_na_attention",
    )(proj, proj, proj, kc, vc, bias)


def _short_conv_kernel(a_ref, b_ref, c_ref, w_ref, bias_ref, oa_ref, ob_ref, oc_ref, *, L):
    t = lax.broadcasted_iota(jnp.int32, (L, HY_C), 0)
    for n, (x_ref, o_ref) in enumerate(((a_ref, oa_ref), (b_ref, ob_ref), (c_ref, oc_ref))):
        sl = slice(HY_C * n, HY_C * (n + 1))
        x = x_ref[...]
        prev = jnp.where(t == 0, 0.0, pltpu.roll(x, 1, axis=0))
        nxt = jnp.where(t == L - 1, 0.0, pltpu.roll(x, L - 1, axis=0))
        o_ref[...] = prev * w_ref[0:1, sl] + x * w_ref[1:2, sl] + nxt * w_ref[2:3, sl] + bias_ref[:, sl]


def _short_conv(proj, w, b, NB, Lb):
    T = proj.shape[0]
    spec = lambda c: pl.BlockSpec((Lb, HY_C), lambda i: (i, c))
    return pl.pallas_call(
        functools.partial(_short_conv_kernel, L=Lb),
        grid=(NB,),
        in_specs=[spec(P_HY // HY_C), spec(P_HY // HY_C + 1), spec(P_HY // HY_C + 2),
                  pl.BlockSpec((3, 3 * HY_C), lambda i: (0, 0)), pl.BlockSpec((1, 3 * HY_C), lambda i: (0, 0))],
        out_specs=[spec(0)] * 3,
        out_shape=[jax.ShapeDtypeStruct((T, HY_C), F32)] * 3,
        compiler_params=_cp("arbitrary"),
        name="hyena_short_conv",
    )(proj, proj, proj, w, b)


def _hy_filter_kernel(w1_ref, b1_ref, w2_ref, b2_ref, w3_ref, freq_ref, ld_ref, fs_ref, nyq_ref, *, L):
    ti = lax.broadcasted_iota(jnp.int32, (L, 128), 0)
    t = ti.astype(F32)
    j = lax.broadcasted_iota(jnp.int32, (L, 128), 1)
    band = jnp.where(j <= HY_BANDS, j - 1, j - 1 - HY_BANDS).astype(F32)
    ang = (2.0 * math.pi / L) * t * band
    tn = t / L
    z = jnp.where(j == 0, tn, jnp.where(j <= HY_BANDS, jnp.cos(ang),
                                        jnp.where(j <= 2 * HY_BANDS, -jnp.sin(ang), 0.0)))
    a = jnp.sin(freq_ref[0:1, :] * (_dot_hi(z, w1_ref[...]) + b1_ref[...]))
    a = jnp.sin(freq_ref[1:2, :] * (_dot_hi(a, w2_ref[...]) + b2_ref[...]))
    filt = _dot_hi(a, w3_ref[...])
    tcol = lax.broadcasted_iota(jnp.int32, (L, 4 * HY_C), 0)
    filt = filt * jnp.exp(-(tcol.astype(F32) / L) * jnp.exp(ld_ref[...]))
    t1 = lax.broadcasted_iota(jnp.int32, (L, HY_C), 0)
    sign = jnp.where(t1 % 2 == 0, 1.0, -1.0)
    for n in range(2):
        fwd = filt[:, 2 * HY_C * n:2 * HY_C * n + HY_C]
        bwd = jnp.where(t1 == 0, 0.0, filt[:, 2 * HY_C * n + HY_C:2 * HY_C * (n + 1)])
        tot = fwd + bwd
        fs_ref[:, HY_C * n:HY_C * (n + 1)] = tot
        fs_ref[:, 2 * HY_C + HY_C * n:2 * HY_C + HY_C * (n + 1)] = fwd - bwd
        nyq_ref[:, HY_C * n:HY_C * (n + 1)] = (tot * sign).sum(0, keepdims=True)


def _hy_filter(L, w1p, b1, w2, b2, w3, freq, ld):
    full = lambda s: pl.BlockSpec(s, lambda: tuple(0 for _ in s))
    return pl.pallas_call(
        functools.partial(_hy_filter_kernel, L=L),
        in_specs=[full((128, 64)), full((1, 64)), full((64, 64)), full((1, 64)), full((64, 4 * HY_C)),
                  full((2, 64)), full((1, 4 * HY_C))],
        out_specs=[full((L, 4 * HY_C)), full((1, 2 * HY_C))],
        out_shape=[jax.ShapeDtypeStruct((L, 4 * HY_C), F32), jax.ShapeDtypeStruct((1, 2 * HY_C), F32)],
        compiler_params=pltpu.CompilerParams(vmem_limit_bytes=VMEM_LIMIT),
        name="hyena_filter",
    )(w1p, b1, w2, b2, w3, freq, ld)


def _hy_gdft_kernel(cm_ref, sm_ref, fs_ref, nyq_ref, gr_ref, gi_ref, *, tm):
    m = pl.program_id(0)
    f = fs_ref[...].astype(BF16)
    gr_ref[...] = _dot(cm_ref[...], f[:, :2 * HY_C])
    gi = _dot(sm_ref[...], f[:, 2 * HY_C:])
    row = m * tm + lax.broadcasted_iota(jnp.int32, (tm, 2 * HY_C), 0)
    gi_ref[...] = jnp.where(row == 0, nyq_ref[...], gi)


def _hy_gdft(cm, sm, fs, nyq, L, tm):
    return pl.pallas_call(
        functools.partial(_hy_gdft_kernel, tm=tm),
        grid=(L // tm,),
        in_specs=[pl.BlockSpec((tm, L), lambda m: (m, 0)), pl.BlockSpec((tm, L), lambda m: (m, 0)),
                  pl.BlockSpec((L, 4 * HY_C), lambda m: (0, 0)), pl.BlockSpec((1, 2 * HY_C), lambda m: (0, 0))],
        out_specs=[pl.BlockSpec((tm, 2 * HY_C), lambda m: (m, 0))] * 2,
        out_shape=[jax.ShapeDtypeStruct((L, 2 * HY_C), F32)] * 2,
        compiler_params=_cp("arbitrary"),
        name="hyena_filter_dft",
    )(cm, sm, fs, nyq)


def _hy_fwd_kernel(cm_ref, sm_ref, z_ref, gr_ref, gi_ref, yr_ref, yi_ref, *, L, tm):
    m = pl.program_id(1)
    zb = z_ref[...].astype(BF16)
    zr = _dot(cm_ref[...], zb)
    zi = _dot(sm_ref[...], zb)
    gr = gr_ref[...]
    gi = gi_ref[...]
    row0 = (m * tm + lax.broadcasted_iota(jnp.int32, (tm, HY_C), 0)) == 0
    s = jnp.where(row0, 0.5 / L, 1.0 / L)
    zigi = zi * gi
    yr_ref[...] = ((zr * gr - jnp.where(row0, 0.0, zigi)) * s).astype(BF16)
    yi_ref[...] = (jnp.where(row0, zigi, zr * gi + zi * gr) * s).astype(BF16)


def _hy_fwd(cm, sm, z, gr, gi, n, NB, Lb, tm):
    T = z.shape[0]
    nm = Lb // tm
    return pl.pallas_call(
        functools.partial(_hy_fwd_kernel, L=Lb, tm=tm),
        grid=(NB, nm),
        in_specs=[pl.BlockSpec((tm, Lb), lambda b, m: (m, 0)), pl.BlockSpec((tm, Lb), lambda b, m: (m, 0)),
                  pl.BlockSpec((Lb, HY_C), lambda b, m: (b, 0)),
                  pl.BlockSpec((tm, HY_C), lambda b, m: (m, n)), pl.BlockSpec((tm, HY_C), lambda b, m: (m, n))],
        out_specs=[pl.BlockSpec((tm, HY_C), lambda b, m: (b * nm + m, 0))] * 2,
        out_shape=[jax.ShapeDtypeStruct((T, HY_C), BF16)] * 2,
        compiler_params=_cp("arbitrary", "arbitrary"),
        name="hyena_fwd_dft",
    )(cm, sm, z, gr, gi)


def _hy_inv_kernel(cm_ref, smt_ref, yr_ref, yi_ref, z_ref, g_ref, skip_ref, o_ref):
    conv = _dot(cm_ref[...], yr_ref[...]) + _dot(smt_ref[...], yi_ref[...])
    z = z_ref[...]
    o_ref[...] = g_ref[...] * (conv + skip_ref[...] * z)


def _hy_inv(cm, smt, yr, yi, z, gate, skip, n, NB, Lb, tm):
    T = z.shape[0]
    nm = Lb // tm
    tile = pl.BlockSpec((tm, HY_C), lambda b, m: (b * nm + m, 0))
    seq = pl.BlockSpec((Lb, HY_C), lambda b, m: (b, 0))
    return pl.pallas_call(
        _hy_inv_kernel,
        grid=(NB, nm),
        in_specs=[pl.BlockSpec((tm, Lb), lambda b, m: (m, 0)), pl.BlockSpec((tm, Lb), lambda b, m: (m, 0)),
                  seq, seq, tile, tile, pl.BlockSpec((None, 1, HY_C), lambda b, m: (n, 0, 0))],
        out_specs=tile,
        out_shape=jax.ShapeDtypeStruct((T, HY_C), F32),
        compiler_params=_cp("arbitrary", "arbitrary"),
        name="hyena_inv_dft",
    )(cm, smt, yr, yi, z, gate, skip)


def _dft_mats(L):
    k = jnp.arange(L, dtype=jnp.int32)
    m = (k[:, None] * k[None, :]) % (2 * L)
    ang = m.astype(F32) * (math.pi / L)
    alt = jnp.where(k % 2 == 0, 1.0, -1.0).astype(F32)
    cm = jnp.cos(ang)
    s = -jnp.sin(ang)
    sm = jnp.where(k[:, None] == 0, alt[None, :], s)
    smt = jnp.where(k[None, :] == 0, alt[:, None], s)
    return cm.astype(BF16), sm.astype(BF16), smt.astype(BF16)


def _merge_kernel(oa_ref, ob_ref, oc_ref, od_ref, g0_ref, g1_ref, g2_ref, g3_ref, wb_ref, wo_ref, x_ref, mod_ref,
                  lg_ref, lb_ref, x1_ref, h2_ref):
    acc = None
    for o_ref, g_ref, i in ((oa_ref, g0_ref, 0), (ob_ref, g1_ref, 1), (oc_ref, g2_ref, 2), (od_ref, g3_ref, 3)):
        y = _sigmoid(g_ref[...]) * _dot(o_ref[...].astype(BF16), wb_ref[i])
        acc = y if acc is None else acc + y
    mix = _dot(acc.astype(BF16), wo_ref[...])
    m = mod_ref[...]
    x1 = _layer_norm(DN_ALPHA * x_ref[...] + m[2:3, :] * mix, lg_ref[...], lb_ref[...])
    x1_ref[...] = x1
    h2_ref[...] = x1 * (1.0 + m[4:5, :]) + m[3:4, :]


def _merge(outs, proj, wb, wo, x, mod, lg, lb, mod_row, bm):
    T = x.shape[0]
    row = lambda w: pl.BlockSpec((bm, w), lambda i: (i, 0))
    gspec = lambda n: pl.BlockSpec((bm, D), lambda i: (i, P_GATE // D + n))
    fixed2 = lambda s: pl.BlockSpec(s, lambda i: (0, 0))
    return pl.pallas_call(
        _merge_kernel,
        grid=(T // bm,),
        in_specs=[row(256)] * 4 + [gspec(0), gspec(1), gspec(2), gspec(3),
                                   pl.BlockSpec((4, 256, D), lambda i: (0, 0, 0)), fixed2((D, D)), row(D),
                                   pl.BlockSpec((None, 6, D), lambda i: (mod_row(i), 0, 0)),
                                   fixed2((1, D)), fixed2((1, D))],
        out_specs=[row(D), row(D)],
        out_shape=[jax.ShapeDtypeStruct((T, D), F32)] * 2,
        compiler_params=_cp("arbitrary"),
        name="merge_norm",
    )(*outs, proj, proj, proj, proj, wb, wo, x, mod, lg, lb)


def _router_kernel(h_ref, rt_ref, bias_ref, g_ref, *, tt):
    per = N_EXPERTS // N_GROUPS
    logits = lax.dot_general(rt_ref[...], h_ref[...], (((1,), (1,)), ((), ())), preferred_element_type=F32,
                             precision=lax.Precision.HIGHEST)
    scores = _sigmoid(logits)
    sel = (scores + bias_ref[...]).reshape(N_GROUPS, per, tt)
    gid = lax.broadcasted_iota(jnp.int32, (N_GROUPS, per, tt), 0).astype(F32)
    jid = lax.broadcasted_iota(jnp.int32, (N_GROUPS, per, tt), 1).astype(F32)
    eid = gid * per + jid
    ninf = -jnp.inf
    m1 = sel.max(1, keepdims=True)
    i1 = jnp.where(sel == m1, jid, float(per)).min(1, keepdims=True)
    m2 = jnp.where(jid == i1, ninf, sel).max(1, keepdims=True)
    gs = m1 + m2
    g1 = lax.broadcasted_iota(jnp.int32, (N_GROUPS, 1, tt), 0).astype(F32)
    chosen = jnp.zeros((N_GROUPS, 1, tt), F32)
    for _ in range(TOPK_GROUPS):
        mx = gs.max(0, keepdims=True)
        gi = jnp.where(gs == mx, g1, float(N_GROUPS)).min(0, keepdims=True)
        pick = g1 == gi
        chosen = jnp.where(pick, 1.0, chosen)
        gs = jnp.where(pick, ninf, gs)
    cand = jnp.where(chosen > 0.0, sel, NEG)
    picked = jnp.zeros((N_GROUPS, per, tt), F32)
    for _ in range(TOP_K):
        mx = cand.max(1, keepdims=True).max(0, keepdims=True)
        ei = jnp.where(cand == mx, eid, float(N_EXPERTS)).min(1, keepdims=True).min(0, keepdims=True)
        pick = eid == ei
        picked = jnp.where(pick, 1.0, picked)
        cand = jnp.where(pick, ninf, cand)
    w = scores.reshape(N_GROUPS, per, tt) * picked
    wsum = w.sum(1, keepdims=True).sum(0, keepdims=True)
    g_ref[...] = (w / wsum * ROUTED_SCALE).reshape(N_EXPERTS, tt)


def _router(h2, router_t, bias, tt):
    T = h2.shape[0]
    return pl.pallas_call(
        functools.partial(_router_kernel, tt=tt),
        grid=(T // tt,),
        in_specs=[pl.BlockSpec((tt, D), lambda i: (i, 0)), pl.BlockSpec((N_EXPERTS, D), lambda i: (0, 0)),
                  pl.BlockSpec((N_EXPERTS, 1), lambda i: (0, 0))],
        out_specs=pl.BlockSpec((N_EXPERTS, tt), lambda i: (0, i)),
        out_shape=jax.ShapeDtypeStruct((N_EXPERTS, T), F32),
        compiler_params=_cp("arbitrary"),
        name="moe_router",
    )(h2, router_t, bias)


def _moe_kernel(h_ref, gate_ref, w1_ref, w3_ref, w2_ref, s1_ref, s3_ref, s2_ref, o_ref, xb_ref):
    e = pl.program_id(1)

    @pl.when(e == 0)
    def _():
        xb = h_ref[...].astype(BF16)
        xb_ref[...] = xb
        a = _dot(xb, s1_ref[...].astype(BF16))
        b = _dot(xb, s3_ref[...].astype(BF16))
        o_ref[...] = _dot((a * _sigmoid(a) * b).astype(BF16), s2_ref[...].astype(BF16))

    xb = xb_ref[...]
    a = _dot(xb, w1_ref[...].astype(BF16))
    b = _dot(xb, w3_ref[...].astype(BF16))
    gate = gate_ref[...]
    lane = lax.broadcasted_iota(jnp.int32, gate.shape, 1)
    ge = jnp.where(lane == e, gate, 0.0).sum(-1, keepdims=True)
    hid = a * _sigmoid(a) * b * ge
    o_ref[...] += _dot(hid.astype(BF16), w2_ref[...].astype(BF16))


def _moe(h2, gate, w1, w3, w2, s1, s3, s2, l, tt):
    T = h2.shape[0]
    ds = D_EXPERT
    return pl.pallas_call(
        _moe_kernel,
        grid=(T // tt, N_EXPERTS),
        in_specs=[pl.BlockSpec((tt, D), lambda i, e: (i, 0)),
                  pl.BlockSpec((tt, N_EXPERTS), lambda i, e: (i, 0)),
                  pl.BlockSpec((None, None, D, ds), lambda i, e: (l, e, 0, 0)),
                  pl.BlockSpec((None, None, D, ds), lambda i, e: (l, e, 0, 0)),
                  pl.BlockSpec((None, None, ds, D), lambda i, e: (l, e, 0, 0)),
                  pl.BlockSpec((None, D, ds), lambda i, e: (l, 0, 0)),
                  pl.BlockSpec((None, D, ds), lambda i, e: (l, 0, 0)),
                  pl.BlockSpec((None, ds, D), lambda i, e: (l, 0, 0))],
        out_specs=pl.BlockSpec((tt, D), lambda i, e: (i, 0)),
        out_shape=jax.ShapeDtypeStruct((T, D), F32),
        scratch_shapes=[pltpu.VMEM((tt, D), BF16)],
        compiler_params=_cp("arbitrary", "arbitrary"),
        name="moe_experts",
    )(h2, gate, w1, w3, w2, s1, s3, s2)


def _final_kernel(x_ref, y_ref, mod_ref, lg_ref, lb_ref, o_ref):
    m = mod_ref[...]
    o_ref[...] = _layer_norm(DN_ALPHA * x_ref[...] + m[5:6, :] * y_ref[...], lg_ref[...], lb_ref[...])


def _final_norm(x1, y, mod, lg, lb, mod_row, bm):
    T = x1.shape[0]
    row = pl.BlockSpec((bm, D), lambda i: (i, 0))
    fixed = pl.BlockSpec((1, D), lambda i: (0, 0))
    return pl.pallas_call(
        _final_kernel,
        grid=(T // bm,),
        in_specs=[row, row, pl.BlockSpec((None, 6, D), lambda i: (mod_row(i), 0, 0)), fixed, fixed],
        out_specs=row,
        out_shape=jax.ShapeDtypeStruct((T, D), F32),
        compiler_params=_cp("arbitrary"),
        name="final_norm",
    )(x1, y, mod, lg, lb)


def _rot_cols(w, q):
    a, b, c, d = w[..., :q], w[..., q:2 * q], w[..., 2 * q:3 * q], w[..., 3 * q:]
    return jnp.concatenate([-b, a, -d, c], -1)


def _prep_w_in(w):
    z = lambda n: jnp.zeros((D, n), w.dtype)
    qlat, ckv, kpe, hy = w[:, 0:256], w[:, 256:384], w[:, 384:416], w[:, 416:1184]
    wq, wk, wv = w[:, 1184:1440], w[:, 1440:1568], w[:, 1568:1696]
    nq, nk, nv, gate = w[:, 1696:1952], w[:, 1952:2208], w[:, 2208:2464], w[:, 2464:]
    wq_r = _rot_cols(wq.reshape(D, 4, 64), 16).reshape(D, 256)
    wk_r = _rot_cols(wk.reshape(D, 2, 64), 16).reshape(D, 128)
    kpe_r = _rot_cols(kpe, 8)
    cols = [qlat, ckv, z(64), kpe, z(32), hy, wq, wk, wv, nq, nk, nv, wq_r, wk_r, z(64), kpe_r, z(32), gate]
    return jnp.concatenate(cols, 1).astype(BF16)


def _prep_mla(w_uq, w_ukv):
    uq = w_uq.reshape(256, 4, 96)
    nope, pe = uq[..., :64], uq[..., 64:]
    z32 = jnp.zeros((256, 4, 32), w_uq.dtype)
    z64 = jnp.zeros((256, 4, 64), w_uq.dtype)
    wcat = jnp.concatenate([nope, pe, z32], -1).reshape(256, 512).astype(BF16)
    wrot = jnp.concatenate([z64, _rot_cols(pe, 8), z32], -1).reshape(256, 512).astype(BF16)
    ukv = w_ukv.reshape(128, 4, 128)
    wk = jnp.concatenate([ukv[..., :64], jnp.zeros((128, 4, 64), w_ukv.dtype)], -1).reshape(128, 512).astype(BF16)
    wv = ukv[..., 64:].reshape(128, 256).astype(BF16)
    return wcat, wrot, wk, wv


def _rope_tab(L, q):
    t = jnp.arange(L)
    inv = ROPE_BASE ** (-jnp.arange(q, dtype=F32) / q)
    ar = (t // GRID_W).astype(F32)[:, None] * inv[None, :]
    ac = (t % GRID_W).astype(F32)[:, None] * inv[None, :]
    cos = jnp.concatenate([jnp.cos(ar), jnp.cos(ar), jnp.cos(ac), jnp.cos(ac)], 1)
    sin = jnp.concatenate([jnp.sin(ar), jnp.sin(ar), jnp.sin(ac), jnp.sin(ac)], 1)
    return cos, sin


def _rope_tables(L):
    c8, s8 = _rope_tab(L, 8)
    c16, s16 = _rope_tab(L, 16)
    one, zero = jnp.ones((L, 64), F32), jnp.zeros((L, 64), F32)
    z32 = jnp.zeros((L, 32), F32)
    mla_q = (jnp.tile(jnp.concatenate([one, c8, z32], 1), (1, 4)), jnp.tile(jnp.concatenate([zero, s8, z32], 1), (1, 4)))
    mla_k = (jnp.concatenate([zero, c8, z32], 1), jnp.concatenate([zero, s8, z32], 1))
    win = (jnp.tile(c16, (1, 4)), jnp.tile(s16, (1, 4)), jnp.tile(c16, (1, 2)), jnp.tile(s16, (1, 2)))
    return mla_q + mla_k, win


def _hyena(proj, lp, dft, NB, Lb):
    cm, sm, smt = dft
    tm = min(Lb, 512)
    v, x1, x2 = _short_conv(proj, lp["hy_conv_w"], lp["hy_conv_b"].reshape(1, -1), NB, Lb)
    w1p = jnp.pad(lp["hy_w1"], ((0, 128 - lp["hy_w1"].shape[0]), (0, 0)))
    fs, nyq = _hy_filter(Lb, w1p, lp["hy_b1"].reshape(1, -1), lp["hy_w2"], lp["hy_b2"].reshape(1, -1), lp["hy_w3"],
                         lp["hy_sin_freq"], lp["hy_log_decay"].reshape(1, -1))
    gr, gi = _hy_gdft(cm, sm, fs, nyq, Lb, tm)
    skip = lp["hy_skip"].reshape(2, 1, HY_C)
    z = v
    for n, gate in enumerate((x1, x2)):
        yr, yi = _hy_fwd(cm, sm, z, gr, gi, n, NB, Lb, tm)
        z = _hy_inv(cm, smt, yr, yi, z, gate, skip, n, NB, Lb, tm)
    return z


def _layer(x, mod, lp, l, NB, Lb, mod_row_of_batch, dft, cache=None, tabs=None, na_bias=None):
    T = NB * Lb
    latent = cache is not None
    bm = 256
    mod_row = lambda i: mod_row_of_batch((i * bm) // Lb)
    bmp = min(Lb, 512)
    proj = _in_proj(x, mod, lp["w_in_p"], lambda i: mod_row_of_batch((i * bmp) // Lb), bmp)

    gq, gkv = lp["mla_q_norm"].reshape(1, -1), lp["mla_kv_norm"].reshape(1, -1)
    wcat, wrot, wk, wv = lp["mla_w"]
    q_all, ckv_n, kpe_r = _mla_q(proj, gq, gkv, wcat, wrot, tabs[0] if latent else None, Lb, bm)
    if latent:
        ckv_c, kpe_c, kc_c, vc_c, kd_c, vd_c = cache
        Lc = ckv_c.shape[1]
        kpe_cp = jnp.pad(kpe_c, ((0, 0), (0, 0), (64, 32)))
        ckv_all = jnp.concatenate([ckv_c, ckv_n.reshape(NB, Lb, 128)], 1).reshape(NB * (Lc + Lb), 128)
        kpe_all = jnp.concatenate([kpe_cp, kpe_r.reshape(NB, Lb, 128)], 1).reshape(NB * (Lc + Lb), 128)
        k_all, v_all = _mla_kv(ckv_all, kpe_all, wk, wv, 512)
        oa = _lat_mla_attention(q_all, k_all, v_all, NB, Lb, Lc + Lb, 256)
        oc = _lat_win_attention(proj, kc_c.reshape(NB, Lc, 128), vc_c.reshape(NB, Lc, 128), tabs[1],
                                lp["win_sink"], NB, Lb)
        od = _lat_na_attention(proj, kd_c.reshape(NB, Lc, 256), vd_c.reshape(NB, Lc, 256), na_bias, NB, Lb)
    else:
        k_all, v_all = _mla_kv(ckv_n, kpe_r, wk, wv, 512)
        oa, oc, od = _ctx_attention(proj, q_all, k_all, v_all, lp["win_sink"], NB, Lb)
    ob = _hyena(proj, lp, dft, NB, Lb)

    x1, h2 = _merge((oa, ob, oc, od), proj, lp["w_branch_b"], lp["w_out_b"], x, mod,
                    lp["ln1_g"].reshape(1, -1), lp["ln1_b"].reshape(1, -1), mod_row, bm)
    gate = _router(h2, lp["moe_router"].T, lp["moe_bias"].reshape(-1, 1), 512).T
    tt = min(T, 1024)
    y = _moe(h2, gate, lp["moe_w1"], lp["moe_w3"], lp["moe_w2"], lp["sh_w1"], lp["sh_w3"], lp["sh_w2"], l, tt)
    x2 = _final_norm(x1, y, mod, lp["ln2_g"].reshape(1, -1), lp["ln2_b"].reshape(1, -1), mod_row, bm)
    return x2, (proj, ckv_n)


def kernel(x_prompt, x_sample, cache_mla_ckv, cache_mla_kpe, cache_win_k, cache_win_v, cache_na_k, cache_na_v, c, c_ctx, w_ada, b_ada, w_in, mla_q_norm, mla_kv_norm, mla_w_uq, mla_w_ukv, hy_conv_w, hy_conv_b, hy_w1, hy_b1, hy_w2, hy_b2, hy_w3, hy_sin_freq, hy_log_decay, hy_skip, win_sink, na_rpb, w_branch, w_out, ln1_g, ln1_b, ln2_g, ln2_b, moe_router, moe_bias, moe_w1, moe_w3, moe_w2, sh_w1, sh_w3, sh_w2):
    B, S, _ = x_prompt.shape
    DB, DS, _ = x_sample.shape
    xp = x_prompt.reshape(B * S, D)
    xs = x_sample.reshape(DB * DS, D)
    cvec = jnp.concatenate([c_ctx[None, :], c, jnp.zeros((8 - 1 - DB, D), F32)], 0)
    dft_ctx = _dft_mats(S)
    dft_lat = _dft_mats(DS)
    tabs = _rope_tables(DS)
    new = [[] for _ in range(6)]
    for l in range(DEPTH):
        lp = dict(w_in_p=_prep_w_in(w_in[l]), mla_q_norm=mla_q_norm[l], mla_kv_norm=mla_kv_norm[l],
                  mla_w=_prep_mla(mla_w_uq[l], mla_w_ukv[l]), hy_conv_w=hy_conv_w[l], hy_conv_b=hy_conv_b[l],
                  hy_w1=hy_w1[l], hy_b1=hy_b1[l], hy_w2=hy_w2[l], hy_b2=hy_b2[l], hy_w3=hy_w3[l],
                  hy_sin_freq=hy_sin_freq[l], hy_log_decay=hy_log_decay[l], hy_skip=hy_skip[l],
                  win_sink=win_sink[l], w_branch_b=w_branch[l].astype(BF16), w_out_b=w_out[l].astype(BF16),
                  ln1_g=ln1_g[l], ln1_b=ln1_b[l], ln2_g=ln2_g[l], ln2_b=ln2_b[l],
                  moe_router=moe_router[l], moe_bias=moe_bias[l], moe_w1=moe_w1, moe_w3=moe_w3, moe_w2=moe_w2,
                  sh_w1=sh_w1, sh_w3=sh_w3, sh_w2=sh_w2)
        mod = _modulation(cvec, w_ada, b_ada, l)
        xp, (proj, ckv_n) = _layer(xp, mod, lp, l, B, S, lambda b: 0, dft_ctx)
        new[0].append(ckv_n.reshape(B, S, 128))
        new[1].append(proj[:, P_KPE + 64:P_KPE + 96].reshape(B, S, 32))
        new[2].append(proj[:, P_WK:P_WK + 128].reshape(B, S, 2, 64))
        new[3].append(proj[:, P_WV:P_WV + 128].reshape(B, S, 2, 64))
        new[4].append(proj[:, P_NK:P_NK + 256].reshape(B, S, 4, 64))
        new[5].append(proj[:, P_NV:P_NV + 256].reshape(B, S, 4, 64))
        cache = (cache_mla_ckv[:, l], cache_mla_kpe[:, l], cache_win_k[:, l], cache_win_v[:, l],
                 cache_na_k[:, l], cache_na_v[:, l])
        xs, _ = _layer(xs, mod, lp, l, DB, DS, lambda b: 1 + b, dft_lat, cache=cache, tabs=tabs,
                       na_bias=_na_bias(na_rpb[l]))
    return (xp.reshape(B, S, D), xs.reshape(DB, DS, D)) + tuple(jnp.stack(t, 1) for t in new)
```

```python
import functools
import math

import jax
import jax.numpy as jnp
from jax import lax
from jax.experimental import pallas as pl
from jax.experimental.pallas import tpu as pltpu
from jax.experimental.pallas import tpu_sc as plsc

F32 = jnp.float32
BF16 = jnp.bfloat16

D = 1024
DEPTH = 2
GRID_W = 64
HEAD_DIM = 64
MLA_SCALE = 96 ** -0.5
ATT_SCALE = HEAD_DIM ** -0.5
HY_C = 256
HY_BANDS = 8
NA_KH = 8
NA_KW = 16
N_EXPERTS = 64
N_GROUPS = 8
TOP_K = 8
TOPK_GROUPS = 4
D_EXPERT = 256
ROUTED_SCALE = 2.5
ROPE_BASE = 10000.0
LN_EPS = 1e-5
RMS_EPS = 1e-6
NEG = -1e30
DN_ALPHA = (2 * DEPTH) ** 0.25

P_QLAT, P_CKV, P_KPE, P_HY = 0, 256, 384, 512
P_WQ, P_WK, P_WV = 1280, 1536, 1664
P_NQ, P_NK, P_NV = 1792, 2048, 2304
P_WQR, P_WKR, P_KPER, P_GATE = 2560, 2816, 2944, 3072
N_PROJ = 7168

VMEM_LIMIT = 56 * 1024 * 1024

SC_CORES = 2
SC_SUBCORES = 16
SC_LANES = 16
SC_WORKERS = SC_CORES * SC_SUBCORES
SC_ROWS = 64

MOE_TM = 256


def _cp(*sem):
    return pltpu.CompilerParams(dimension_semantics=sem, vmem_limit_bytes=VMEM_LIMIT)


def _sigmoid(x):
    return 1.0 / (1.0 + jnp.exp(-x))


def _dot(a, b):
    return jnp.dot(a, b, preferred_element_type=F32)


def _dot_nt(a, b):
    return lax.dot_general(a, b, (((1,), (1,)), ((), ())), preferred_element_type=F32)


def _dot_hi(a, b):
    return jnp.dot(a, b, preferred_element_type=F32, precision=lax.Precision.HIGHEST)


def _pack_pairs(x):
    w = x.shape[1] // 2
    hi = lax.bitcast_convert_type(x[:, :w].astype(BF16).astype(F32), jnp.int32)
    lo = lax.bitcast_convert_type(x[:, w:].astype(BF16).astype(F32), jnp.int32)
    return hi | lax.shift_right_logical(lo, 16)


def _unpack_pairs(p):
    hi = lax.bitcast_convert_type(p & jnp.int32(-65536), F32)
    lo = lax.bitcast_convert_type(lax.shift_left(p, 16), F32)
    return hi, lo


def _layer_norm(x, g, b):
    mu = jnp.mean(x, -1, keepdims=True)
    xc = x - mu
    var = jnp.mean(xc * xc, -1, keepdims=True)
    return xc * lax.rsqrt(var + LN_EPS) * g + b


def _rms_norm(x, g):
    return x * lax.rsqrt(jnp.mean(x * x, -1, keepdims=True) + RMS_EPS) * g


def _mod_kernel(c_ref, w_ref, b_ref, o_ref):
    c = c_ref[...]
    a = (c * _sigmoid(c)).astype(BF16)
    o_ref[...] = _dot(a, w_ref[...].astype(BF16)) + b_ref[...]


def _modulation(cvec, w_ada, b_ada, l):
    out = pl.pallas_call(
        _mod_kernel,
        grid=(6,),
        in_specs=[pl.BlockSpec((8, D), lambda j: (0, 0)),
                  pl.BlockSpec((None, D, D), lambda j: (l, 0, j)),
                  pl.BlockSpec((None, 1, D), lambda j: (l, 0, j))],
        out_specs=pl.BlockSpec((8, D), lambda j: (0, j)),
        out_shape=jax.ShapeDtypeStruct((8, 6 * D), F32),
        compiler_params=_cp("arbitrary"),
        name="modulation",
    )(cvec, w_ada, b_ada.reshape(DEPTH, 1, 6 * D))
    return out.reshape(8, 6, D)


def _inproj_kernel(x_ref, mod_ref, w_ref, o_ref):
    m = mod_ref[...]
    h = x_ref[...] * (1.0 + m[1:2, :]) + m[0:1, :]
    o_ref[...] = _dot(h.astype(BF16), w_ref[...])


def _in_proj(x, mod, w_p, mod_row, bm):
    T = x.shape[0]
    bn = 1024
    return pl.pallas_call(
        _inproj_kernel,
        grid=(N_PROJ // bn, T // bm),
        in_specs=[pl.BlockSpec((bm, D), lambda j, i: (i, 0)),
                  pl.BlockSpec((None, 6, D), lambda j, i: (mod_row(i), 0, 0)),
                  pl.BlockSpec((D, bn), lambda j, i: (0, j))],
        out_specs=pl.BlockSpec((bm, bn), lambda j, i: (i, j)),
        out_shape=jax.ShapeDtypeStruct((T, N_PROJ), F32),
        compiler_params=_cp("arbitrary", "arbitrary"),
        name="in_proj",
    )(x, mod, w_p)


def _mla_q_kernel(*refs, rope):
    if rope:
        (ql_ref, ckv_ref, kpe_ref, kper_ref, gq_ref, gkv_ref, wc_ref, wr_ref,
         cq_ref, sq_ref, ck_ref, sk_ref, q_ref, ckvn_ref, kpeo_ref) = refs
    else:
        ql_ref, ckv_ref, kpe_ref, gq_ref, gkv_ref, wc_ref, q_ref, ckvn_ref, kpeo_ref = refs
    qn = _rms_norm(ql_ref[...], gq_ref[...]).astype(BF16)
    q = _dot(qn, wc_ref[...])
    if rope:
        q = q * cq_ref[...] + _dot(qn, wr_ref[...]) * sq_ref[...]
        kpeo_ref[...] = kpe_ref[...] * ck_ref[...] + kper_ref[...] * sk_ref[...]
    else:
        kpeo_ref[...] = kpe_ref[...]
    q_ref[...] = (q * MLA_SCALE).astype(BF16)
    ckvn_ref[...] = _rms_norm(ckv_ref[...], gkv_ref[...])


def _mla_q(proj, gq, gkv, wcat, wrot, tabs, Lb, bm):
    T = proj.shape[0]
    rope = tabs is not None
    nl = Lb // bm
    col = lambda c: (lambda i: (i, c))
    fixed = lambda i: (0, 0)
    in_specs = [pl.BlockSpec((bm, 256), col(P_QLAT // 256)),
                pl.BlockSpec((bm, 128), col(P_CKV // 128)),
                pl.BlockSpec((bm, 128), col(P_KPE // 128))]
    args = [proj, proj, proj]
    if rope:
        in_specs.append(pl.BlockSpec((bm, 128), col(P_KPER // 128)))
        args.append(proj)
    in_specs += [pl.BlockSpec((1, 256), fixed), pl.BlockSpec((1, 128), fixed), pl.BlockSpec((256, 512), fixed)]
    args += [gq, gkv, wcat]
    if rope:
        cq, sq, ck, sk = tabs
        pos = lambda i: (i % nl, 0)
        in_specs += [pl.BlockSpec((256, 512), fixed), pl.BlockSpec((bm, 512), pos), pl.BlockSpec((bm, 512), pos),
                     pl.BlockSpec((bm, 128), pos), pl.BlockSpec((bm, 128), pos)]
        args += [wrot, cq, sq, ck, sk]
    return pl.pallas_call(
        functools.partial(_mla_q_kernel, rope=rope),
        grid=(T // bm,),
        in_specs=in_specs,
        out_specs=[pl.BlockSpec((bm, 512), lambda i: (i, 0)),
                   pl.BlockSpec((bm, 128), lambda i: (i, 0)),
                   pl.BlockSpec((bm, 128), lambda i: (i, 0))],
        out_shape=[jax.ShapeDtypeStruct((T, 512), BF16),
                   jax.ShapeDtypeStruct((T, 128), F32),
                   jax.ShapeDtypeStruct((T, 128), F32)],
        compiler_params=_cp("arbitrary"),
        name="mla_q",
    )(*args)


def _mla_kv_kernel(ckv_ref, kpe_ref, wk_ref, wv_ref, k_ref, v_ref):
    c = ckv_ref[...].astype(BF16)
    kpe = kpe_ref[...]
    k_ref[...] = (_dot(c, wk_ref[...]) + jnp.concatenate([kpe] * 4, axis=1)).astype(BF16)
    v_ref[...] = _dot(c, wv_ref[...]).astype(BF16)


def _mla_kv(ckv, kpe, wk, wv, bm):
    Tk = ckv.shape[0]
    return pl.pallas_call(
        _mla_kv_kernel,
        grid=(Tk // bm,),
        in_specs=[pl.BlockSpec((bm, 128), lambda i: (i, 0)),
                  pl.BlockSpec((bm, 128), lambda i: (i, 0)),
                  pl.BlockSpec((128, 512), lambda i: (0, 0)),
                  pl.BlockSpec((128, 256), lambda i: (0, 0))],
        out_specs=[pl.BlockSpec((bm, 512), lambda i: (i, 0)),
                   pl.BlockSpec((bm, 256), lambda i: (i, 0))],
        out_shape=[jax.ShapeDtypeStruct((Tk, 512), BF16),
                   jax.ShapeDtypeStruct((Tk, 256), BF16)],
        compiler_params=_cp("arbitrary"),
        name="mla_kv",
    )(ckv, kpe, wk, wv)


def _attn_core(q, kvs, masks, sink):
    ss = []
    for (k, _), mk in zip(kvs, masks):
        s = _dot_nt(q, k)
        if mk is not None:
            s = s + mk[1] if mk[0] == "add" else jnp.where(mk[1], s, NEG)
        ss.append(s)
    m = ss[0].max(-1, keepdims=True)
    for s in ss[1:]:
        m = jnp.maximum(m, s.max(-1, keepdims=True))
    if sink is not None:
        m = jnp.maximum(m, sink)
    den = None
    acc = None
    for s, (_, v) in zip(ss, kvs):
        p = jnp.exp(s - m)
        d = p.sum(-1, keepdims=True)
        a = _dot(p.astype(BF16), v)
        den = d if den is None else den + d
        acc = a if acc is None else acc + a
    if sink is not None:
        den = den + jnp.exp(sink - m)
    return acc / den


def _ctx_attn_kernel(qm_ref, km_ref, vm_ref, wq_ref, wk_ref, wv_ref, nq_ref, nk_ref, nv_ref, sink_ref,
                     om_ref, ow_ref, on_ref):
    for h in range(4):
        q = qm_ref[:, 128 * h:128 * (h + 1)]
        k = km_ref[:, 128 * h:128 * (h + 1)]
        v = vm_ref[:, 64 * h:64 * (h + 1)]
        om_ref[:, 64 * h:64 * (h + 1)] = _attn_core(q, [(k, v)], [None], None)
    for h in range(4):
        g = h // 2
        q = (wq_ref[:, 64 * h:64 * (h + 1)] * ATT_SCALE).astype(BF16)
        k = wk_ref[:, 64 * g:64 * (g + 1)].astype(BF16)
        v = wv_ref[:, 64 * g:64 * (g + 1)].astype(BF16)
        ow_ref[:, 64 * h:64 * (h + 1)] = _attn_core(q, [(k, v)], [None], sink_ref[h])
    for h in range(4):
        q = (nq_ref[:, 64 * h:64 * (h + 1)] * ATT_SCALE).astype(BF16)
        k = nk_ref[:, 64 * h:64 * (h + 1)].astype(BF16)
        v = nv_ref[:, 64 * h:64 * (h + 1)].astype(BF16)
        on_ref[:, 64 * h:64 * (h + 1)] = _attn_core(q, [(k, v)], [None], None)


def _ctx_attention(proj, q_all, k_all, v_all, sink, NB, Lb):
    T = proj.shape[0]
    pc = lambda w, off: pl.BlockSpec((Lb, w), lambda b: (b, off // w))
    row = lambda w: pl.BlockSpec((Lb, w), lambda b: (b, 0))
    return pl.pallas_call(
        _ctx_attn_kernel,
        grid=(NB,),
        in_specs=[row(512), row(512), row(256),
                  pc(256, P_WQ), pc(128, P_WK), pc(128, P_WV),
                  pc(256, P_NQ), pc(256, P_NK), pc(256, P_NV),
                  pl.BlockSpec(memory_space=pltpu.SMEM)],
        out_specs=[row(256), row(256), row(256)],
        out_shape=[jax.ShapeDtypeStruct((T, 256), F32)] * 3,
        compiler_params=_cp("arbitrary"),
        name="ctx_attention",
    )(q_all, k_all, v_all, proj, proj, proj, proj, proj, proj, sink)


def _lat_mla_kernel(q_ref, k_ref, v_ref, o_ref):
    for h in range(4):
        q = q_ref[:, 128 * h:128 * (h + 1)]
        k = k_ref[:, 128 * h:128 * (h + 1)]
        v = v_ref[:, 64 * h:64 * (h + 1)]
        o_ref[:, 64 * h:64 * (h + 1)] = _attn_core(q, [(k, v)], [None], None)


def _lat_mla_attention(q_all, k_all, v_all, NB, Lb, Lk, tq):
    T = q_all.shape[0]
    nq = Lb // tq
    return pl.pallas_call(
        _lat_mla_kernel,
        grid=(NB, nq),
        in_specs=[pl.BlockSpec((tq, 512), lambda b, i: (b * nq + i, 0)),
                  pl.BlockSpec((Lk, 512), lambda b, i: (b, 0)),
                  pl.BlockSpec((Lk, 256), lambda b, i: (b, 0))],
        out_specs=pl.BlockSpec((tq, 256), lambda b, i: (b * nq + i, 0)),
        out_shape=jax.ShapeDtypeStruct((T, 256), F32),
        compiler_params=_cp("arbitrary", "arbitrary"),
        name="lat_mla_attention",
    )(q_all, k_all, v_all)


def _lat_win_kernel(q_ref, qr_ref, k_ref, kr_ref, v_ref, kc_ref, vc_ref, cq_ref, sq_ref, ck_ref, sk_ref,
                    sink_ref, o_ref, *, Lb):
    i = pl.program_id(1)
    start = pl.multiple_of(jnp.clip((i - 1) * 128, 0, Lb - 384), 128)
    win = pl.ds(start, 384)
    q = (q_ref[...] * cq_ref[...] + qr_ref[...] * sq_ref[...]) * ATT_SCALE
    kk = k_ref[win, :] * ck_ref[win, :] + kr_ref[win, :] * sk_ref[win, :]
    vv = v_ref[win, :]
    qpos = i * 128 + lax.broadcasted_iota(jnp.int32, (128, 384), 0)
    kpos = start + lax.broadcasted_iota(jnp.int32, (128, 384), 1)
    valid = jnp.abs(qpos - kpos) <= 128
    for h in range(4):
        g = h // 2
        sl = slice(64 * g, 64 * (g + 1))
        kvs = [(kk[:, sl].astype(BF16), vv[:, sl].astype(BF16)),
               (kc_ref[:, sl].astype(BF16), vc_ref[:, sl].astype(BF16))]
        qh = q[:, 64 * h:64 * (h + 1)].astype(BF16)
        o_ref[:, 64 * h:64 * (h + 1)] = _attn_core(qh, kvs, [("keep", valid), None], sink_ref[h])


def _lat_win_attention(proj, kc, vc, tabs, sink, NB, Lb):
    T = proj.shape[0]
    nb = Lb // 128
    Lc = kc.shape[1]
    cq, sq, ck, sk = tabs
    qspec = lambda off: pl.BlockSpec((128, 256), lambda b, i: (b * nb + i, off // 256))
    kspec = lambda off: pl.BlockSpec((Lb, 128), lambda b, i: (b, off // 128))
    cspec = pl.BlockSpec((None, Lc, 128), lambda b, i: (b, 0, 0))
    return pl.pallas_call(
        functools.partial(_lat_win_kernel, Lb=Lb),
        grid=(NB, nb),
        in_specs=[qspec(P_WQ), qspec(P_WQR), kspec(P_WK), kspec(P_WKR), kspec(P_WV), cspec, cspec,
                  pl.BlockSpec((128, 256), lambda b, i: (i, 0)), pl.BlockSpec((128, 256), lambda b, i: (i, 0)),
                  pl.BlockSpec((Lb, 128), lambda b, i: (0, 0)), pl.BlockSpec((Lb, 128), lambda b, i: (0, 0)),
                  pl.BlockSpec(memory_space=pltpu.SMEM)],
        out_specs=pl.BlockSpec((128, 256), lambda b, i: (b * nb + i, 0)),
        out_shape=jax.ShapeDtypeStruct((T, 256), F32),
        compiler_params=_cp("arbitrary", "arbitrary"),
        name="lat_win_attention",
    )(proj, proj, proj, proj, proj, kc, vc, cq, sq, ck, sk, sink)


def _na_bias_kernel(rpb_ref, o_ref):
    h = pl.program_id(0)
    qc = lax.broadcasted_iota(jnp.int32, (GRID_W, GRID_W), 0)
    kc = lax.broadcasted_iota(jnp.int32, (GRID_W, GRID_W), 1)
    dc = kc - qc + (NA_KW - 1)
    wstart = jnp.clip(qc - NA_KW // 2, 0, GRID_W - NA_KW)
    ok = (kc >= wstart) & (kc < wstart + NA_KW)
    n_dc = 2 * NA_KW - 1
    n_dr = 2 * NA_KH - 1
    tabs = []
    for dr in range(n_dr):
        t = jnp.zeros((GRID_W, GRID_W), F32)
        for j in range(n_dc):
            t = jnp.where(dc == j, rpb_ref[(h * n_dr + dr) * n_dc + j], t)
        tabs.append(jnp.where(ok, t, NEG))
    for o in range(NA_KH):
        for a in range(NA_KH):
            o_ref[o, :, GRID_W * a:GRID_W * (a + 1)] = tabs[a + NA_KH - 1 - o]


def _na_bias(rpb):
    H = rpb.shape[0]
    return pl.pallas_call(
        _na_bias_kernel,
        grid=(H,),
        in_specs=[pl.BlockSpec(memory_space=pltpu.SMEM)],
        out_specs=pl.BlockSpec((None, NA_KH, GRID_W, NA_KH * GRID_W), lambda h: (h, 0, 0, 0)),
        out_shape=jax.ShapeDtypeStruct((H, NA_KH, GRID_W, NA_KH * GRID_W), F32),
        compiler_params=_cp("arbitrary"),
        name="na_bias",
    )(rpb.reshape(-1))


def _lat_na_kernel(q_ref, k_ref, v_ref, kc_ref, vc_ref, bias_ref, o_ref, *, rows):
    r = pl.program_id(1)
    start = pl.multiple_of(jnp.clip(r - NA_KH // 2, 0, rows - NA_KH) * GRID_W, GRID_W)
    win = pl.ds(start, NA_KH * GRID_W)
    kk = k_ref[win, :]
    vv = v_ref[win, :]
    q = q_ref[...] * ATT_SCALE
    for h in range(4):
        sl = slice(64 * h, 64 * (h + 1))
        kvs = [(kk[:, sl].astype(BF16), vv[:, sl].astype(BF16)),
               (kc_ref[:, sl].astype(BF16), vc_ref[:, sl].astype(BF16))]
        o_ref[:, sl] = _attn_core(q[:, sl].astype(BF16), kvs, [("add", bias_ref[h]), None], None)


def _lat_na_attention(proj, kc, vc, bias, NB, Lb):
    T = proj.shape[0]
    rows = Lb // GRID_W
    Lc = kc.shape[1]
    kspec = lambda off: pl.BlockSpec((Lb, 256), lambda b, r: (b, off // 256))
    cspec = pl.BlockSpec((None, Lc, 256), lambda b, r: (b, 0, 0))
    bspec = pl.BlockSpec((4, None, GRID_W, NA_KH * GRID_W),
                         lambda b, r: (0, r - jnp.clip(r - NA_KH // 2, 0, rows - NA_KH), 0, 0))
    return pl.pallas_call(
        functools.partial(_lat_na_kernel, rows=rows),
        grid=(NB, rows),
        in_specs=[pl.BlockSpec((GRID_W, 256), lambda b, r: (b * rows + r, P_NQ // 256)),
                  kspec(P_NK), kspec(P_NV), cspec, cspec, bspec],
        out_specs=pl.BlockSpec((GRID_W, 256), lambda b, r: (b * rows + r, 0)),
        out_shape=jax.ShapeDtypeStruct((T, 256), F32),
        compiler_params=_cp("arbitrary", "arbitrary"),
        name="lat_na_attention",
    )(proj, proj, proj, kc, vc, bias)


def _short_conv_kernel(a_ref, b_ref, c_ref, w_ref, bias_ref, oa_ref, ob_ref, oc_ref, *, L):
    t = lax.broadcasted_iota(jnp.int32, (L, HY_C), 0)
    for n, (x_ref, o_ref) in enumerate(((a_ref, oa_ref), (b_ref, ob_ref), (c_ref, oc_ref))):
        sl = slice(HY_C * n, HY_C * (n + 1))
        x = x_ref[...]
        prev = jnp.where(t == 0, 0.0, pltpu.roll(x, 1, axis=0))
        nxt = jnp.where(t == L - 1, 0.0, pltpu.roll(x, L - 1, axis=0))
        o_ref[...] = prev * w_ref[0:1, sl] + x * w_ref[1:2, sl] + nxt * w_ref[2:3, sl] + bias_ref[:, sl]


def _short_conv(proj, w, b, NB, Lb):
    T = proj.shape[0]
    spec = lambda c: pl.BlockSpec((Lb, HY_C), lambda i: (i, c))
    return pl.pallas_call(
        functools.partial(_short_conv_kernel, L=Lb),
        grid=(NB,),
        in_specs=[spec(P_HY // HY_C), spec(P_HY // HY_C + 1), spec(P_HY // HY_C + 2),
                  pl.BlockSpec((3, 3 * HY_C), lambda i: (0, 0)), pl.BlockSpec((1, 3 * HY_C), lambda i: (0, 0))],
        out_specs=[spec(0)] * 3,
        out_shape=[jax.ShapeDtypeStruct((T, HY_C), F32)] * 3,
        compiler_params=_cp("arbitrary"),
        name="hyena_short_conv",
    )(proj, proj, proj, w, b)


def _hy_filter_kernel(w1_ref, b1_ref, w2_ref, b2_ref, w3_ref, freq_ref, ld_ref, fs_ref, nyq_ref, *, L):
    ti = lax.broadcasted_iota(jnp.int32, (L, 128), 0)
    t = ti.astype(F32)
    j = lax.broadcasted_iota(jnp.int32, (L, 128), 1)
    band = jnp.where(j <= HY_BANDS, j - 1, j - 1 - HY_BANDS).astype(F32)
    ang = (2.0 * math.pi / L) * t * band
    tn = t / L
    z = jnp.where(j == 0, tn, jnp.where(j <= HY_BANDS, jnp.cos(ang),
                                        jnp.where(j <= 2 * HY_BANDS, -jnp.sin(ang), 0.0)))
    a = jnp.sin(freq_ref[0:1, :] * (_dot_hi(z, w1_ref[...]) + b1_ref[...]))
    a = jnp.sin(freq_ref[1:2, :] * (_dot_hi(a, w2_ref[...]) + b2_ref[...]))
    filt = _dot_hi(a, w3_ref[...])
    tcol = lax.broadcasted_iota(jnp.int32, (L, 4 * HY_C), 0)
    filt = filt * jnp.exp(-(tcol.astype(F32) / L) * jnp.exp(ld_ref[...]))
    t1 = lax.broadcasted_iota(jnp.int32, (L, HY_C), 0)
    sign = jnp.where(t1 % 2 == 0, 1.0, -1.0)
    for n in range(2):
        fwd = filt[:, 2 * HY_C * n:2 * HY_C * n + HY_C]
        bwd = jnp.where(t1 == 0, 0.0, filt[:, 2 * HY_C * n + HY_C:2 * HY_C * (n + 1)])
        tot = fwd + bwd
        fs_ref[:, HY_C * n:HY_C * (n + 1)] = tot
        fs_ref[:, 2 * HY_C + HY_C * n:2 * HY_C + HY_C * (n + 1)] = fwd - bwd
        nyq_ref[:, HY_C * n:HY_C * (n + 1)] = (tot * sign).sum(0, keepdims=True)


def _hy_filter(L, w1p, b1, w2, b2, w3, freq, ld):
    full = lambda s: pl.BlockSpec(s, lambda: tuple(0 for _ in s))
    return pl.pallas_call(
        functools.partial(_hy_filter_kernel, L=L),
        in_specs=[full((128, 64)), full((1, 64)), full((64, 64)), full((1, 64)), full((64, 4 * HY_C)),
                  full((2, 64)), full((1, 4 * HY_C))],
        out_specs=[full((L, 4 * HY_C)), full((1, 2 * HY_C))],
        out_shape=[jax.ShapeDtypeStruct((L, 4 * HY_C), F32), jax.ShapeDtypeStruct((1, 2 * HY_C), F32)],
        compiler_params=pltpu.CompilerParams(vmem_limit_bytes=VMEM_LIMIT),
        name="hyena_filter",
    )(w1p, b1, w2, b2, w3, freq, ld)


def _hy_gdft_kernel(cm_ref, sm_ref, fs_ref, nyq_ref, gr_ref, gi_ref, *, tm):
    m = pl.program_id(0)
    f = fs_ref[...].astype(BF16)
    gr_ref[...] = _dot(cm_ref[...], f[:, :2 * HY_C])
    gi = _dot(sm_ref[...], f[:, 2 * HY_C:])
    row = m * tm + lax.broadcasted_iota(jnp.int32, (tm, 2 * HY_C), 0)
    gi_ref[...] = jnp.where(row == 0, nyq_ref[...], gi)


def _hy_gdft(cm, sm, fs, nyq, L, tm):
    return pl.pallas_call(
        functools.partial(_hy_gdft_kernel, tm=tm),
        grid=(L // tm,),
        in_specs=[pl.BlockSpec((tm, L), lambda m: (m, 0)), pl.BlockSpec((tm, L), lambda m: (m, 0)),
                  pl.BlockSpec((L, 4 * HY_C), lambda m: (0, 0)), pl.BlockSpec((1, 2 * HY_C), lambda m: (0, 0))],
        out_specs=[pl.BlockSpec((tm, 2 * HY_C), lambda m: (m, 0))] * 2,
        out_shape=[jax.ShapeDtypeStruct((L, 2 * HY_C), F32)] * 2,
        compiler_params=_cp("arbitrary"),
        name="hyena_filter_dft",
    )(cm, sm, fs, nyq)


def _hy_fwd_kernel(cm_ref, sm_ref, z_ref, gr_ref, gi_ref, yr_ref, yi_ref, *, L, tm):
    m = pl.program_id(1)
    zb = z_ref[...].astype(BF16)
    zr = _dot(cm_ref[...], zb)
    zi = _dot(sm_ref[...], zb)
    gr = gr_ref[...]
    gi = gi_ref[...]
    row0 = (m * tm + lax.broadcasted_iota(jnp.int32, (tm, HY_C), 0)) == 0
    s = jnp.where(row0, 0.5 / L, 1.0 / L)
    zigi = zi * gi
    yr_ref[...] = ((zr * gr - jnp.where(row0, 0.0, zigi)) * s).astype(BF16)
    yi_ref[...] = (jnp.where(row0, zigi, zr * gi + zi * gr) * s).astype(BF16)


def _hy_fwd(cm, sm, z, gr, gi, n, NB, Lb, tm):
    T = z.shape[0]
    nm = Lb // tm
    return pl.pallas_call(
        functools.partial(_hy_fwd_kernel, L=Lb, tm=tm),
        grid=(NB, nm),
        in_specs=[pl.BlockSpec((tm, Lb), lambda b, m: (m, 0)), pl.BlockSpec((tm, Lb), lambda b, m: (m, 0)),
                  pl.BlockSpec((Lb, HY_C), lambda b, m: (b, 0)),
                  pl.BlockSpec((tm, HY_C), lambda b, m: (m, n)), pl.BlockSpec((tm, HY_C), lambda b, m: (m, n))],
        out_specs=[pl.BlockSpec((tm, HY_C), lambda b, m: (b * nm + m, 0))] * 2,
        out_shape=[jax.ShapeDtypeStruct((T, HY_C), BF16)] * 2,
        compiler_params=_cp("arbitrary", "arbitrary"),
        name="hyena_fwd_dft",
    )(cm, sm, z, gr, gi)


def _hy_inv_kernel(cm_ref, smt_ref, yr_ref, yi_ref, z_ref, g_ref, skip_ref, o_ref):
    conv = _dot(cm_ref[...], yr_ref[...]) + _dot(smt_ref[...], yi_ref[...])
    z = z_ref[...]
    o_ref[...] = g_ref[...] * (conv + skip_ref[...] * z)


def _hy_inv(cm, smt, yr, yi, z, gate, skip, n, NB, Lb, tm):
    T = z.shape[0]
    nm = Lb // tm
    tile = pl.BlockSpec((tm, HY_C), lambda b, m: (b * nm + m, 0))
    seq = pl.BlockSpec((Lb, HY_C), lambda b, m: (b, 0))
    return pl.pallas_call(
        _hy_inv_kernel,
        grid=(NB, nm),
        in_specs=[pl.BlockSpec((tm, Lb), lambda b, m: (m, 0)), pl.BlockSpec((tm, Lb), lambda b, m: (m, 0)),
                  seq, seq, tile, tile, pl.BlockSpec((None, 1, HY_C), lambda b, m: (n, 0, 0))],
        out_specs=tile,
        out_shape=jax.ShapeDtypeStruct((T, HY_C), F32),
        compiler_params=_cp("arbitrary", "arbitrary"),
        name="hyena_inv_dft",
    )(cm, smt, yr, yi, z, gate, skip)


def _dft_mats(L):
    k = jnp.arange(L, dtype=jnp.int32)
    m = (k[:, None] * k[None, :]) % (2 * L)
    ang = m.astype(F32) * (math.pi / L)
    alt = jnp.where(k % 2 == 0, 1.0, -1.0).astype(F32)
    cm = jnp.cos(ang)
    s = -jnp.sin(ang)
    sm = jnp.where(k[:, None] == 0, alt[None, :], s)
    smt = jnp.where(k[None, :] == 0, alt[:, None], s)
    return cm.astype(BF16), sm.astype(BF16), smt.astype(BF16)


def _merge_kernel(oa_ref, ob_ref, oc_ref, od_ref, g0_ref, g1_ref, g2_ref, g3_ref, wb_ref, wo_ref, x_ref, mod_ref,
                  lg_ref, lb_ref, x1_ref, h2_ref, hp_ref):
    acc = None
    for o_ref, g_ref, i in ((oa_ref, g0_ref, 0), (ob_ref, g1_ref, 1), (oc_ref, g2_ref, 2), (od_ref, g3_ref, 3)):
        y = _sigmoid(g_ref[...]) * _dot(o_ref[...].astype(BF16), wb_ref[i])
        acc = y if acc is None else acc + y
    mix = _dot(acc.astype(BF16), wo_ref[...])
    m = mod_ref[...]
    x1 = _layer_norm(DN_ALPHA * x_ref[...] + m[2:3, :] * mix, lg_ref[...], lb_ref[...])
    x1_ref[...] = x1
    h2 = x1 * (1.0 + m[4:5, :]) + m[3:4, :]
    h2_ref[...] = h2
    hp_ref[...] = _pack_pairs(h2)


def _merge(outs, proj, wb, wo, x, mod, lg, lb, mod_row, bm):
    T = x.shape[0]
    row = lambda w: pl.BlockSpec((bm, w), lambda i: (i, 0))
    gspec = lambda n: pl.BlockSpec((bm, D), lambda i: (i, P_GATE // D + n))
    fixed2 = lambda s: pl.BlockSpec(s, lambda i: (0, 0))
    return pl.pallas_call(
        _merge_kernel,
        grid=(T // bm,),
        in_specs=[row(256)] * 4 + [gspec(0), gspec(1), gspec(2), gspec(3),
                                   pl.BlockSpec((4, 256, D), lambda i: (0, 0, 0)), fixed2((D, D)), row(D),
                                   pl.BlockSpec((None, 6, D), lambda i: (mod_row(i), 0, 0)),
                                   fixed2((1, D)), fixed2((1, D))],
        out_specs=[row(D), row(D), row(D // 2)],
        out_shape=[jax.ShapeDtypeStruct((T, D), F32)] * 2 + [jax.ShapeDtypeStruct((T, D // 2), jnp.int32)],
        compiler_params=_cp("arbitrary"),
        name="merge_norm",
    )(*outs, proj, proj, proj, proj, wb, wo, x, mod, lg, lb)


def _router_kernel(h_ref, rt_ref, bias_ref, g_ref, rank_ref, cnt_ref, *, tt):
    per = N_EXPERTS // N_GROUPS
    logits = lax.dot_general(rt_ref[...], h_ref[...], (((1,), (1,)), ((), ())), preferred_element_type=F32,
                             precision=lax.Precision.HIGHEST)
    scores = _sigmoid(logits)
    sel = (scores + bias_ref[...]).reshape(N_GROUPS, per, tt)
    gid = lax.broadcasted_iota(jnp.int32, (N_GROUPS, per, tt), 0).astype(F32)
    jid = lax.broadcasted_iota(jnp.int32, (N_GROUPS, per, tt), 1).astype(F32)
    eid = gid * per + jid
    ninf = -jnp.inf
    m1 = sel.max(1, keepdims=True)
    i1 = jnp.where(sel == m1, jid, float(per)).min(1, keepdims=True)
    m2 = jnp.where(jid == i1, ninf, sel).max(1, keepdims=True)
    gs = m1 + m2
    g1 = lax.broadcasted_iota(jnp.int32, (N_GROUPS, 1, tt), 0).astype(F32)
    chosen = jnp.zeros((N_GROUPS, 1, tt), F32)
    for _ in range(TOPK_GROUPS):
        mx = gs.max(0, keepdims=True)
        gi = jnp.where(gs == mx, g1, float(N_GROUPS)).min(0, keepdims=True)
        pick = g1 == gi
        chosen = jnp.where(pick, 1.0, chosen)
        gs = jnp.where(pick, ninf, gs)
    cand = jnp.where(chosen > 0.0, sel, NEG)
    picked = jnp.zeros((N_GROUPS, per, tt), F32)
    for _ in range(TOP_K):
        mx = cand.max(1, keepdims=True).max(0, keepdims=True)
        ei = jnp.where(cand == mx, eid, float(N_EXPERTS)).min(1, keepdims=True).min(0, keepdims=True)
        pick = eid == ei
        picked = jnp.where(pick, 1.0, picked)
        cand = jnp.where(pick, ninf, cand)
    w = scores.reshape(N_GROUPS, per, tt) * picked
    wsum = w.sum(1, keepdims=True).sum(0, keepdims=True)
    g_ref[...] = (w / wsum * ROUTED_SCALE).reshape(N_EXPERTS, tt)
    pk = picked.reshape(N_EXPERTS, tt)
    t_in = lax.broadcasted_iota(jnp.int32, (tt, tt), 0)
    t_out = lax.broadcasted_iota(jnp.int32, (tt, tt), 1)
    upper = jnp.where(t_in <= t_out, 1.0, 0.0).astype(BF16)

    @pl.when(pl.program_id(0) == 0)
    def _():
        cnt_ref[...] = jnp.zeros_like(cnt_ref)

    before = cnt_ref[:, 0:1]
    rank_ref[...] = jnp.where(pk > 0.0, before + _dot(pk.astype(BF16), upper) - 1.0, -1.0)
    cnt_ref[...] += pk.sum(-1, keepdims=True)


def _router(h2, router_t, bias, tt):
    T = h2.shape[0]
    tile = pl.BlockSpec((N_EXPERTS, tt), lambda i: (0, i))
    return pl.pallas_call(
        functools.partial(_router_kernel, tt=tt),
        grid=(T // tt,),
        in_specs=[pl.BlockSpec((tt, D), lambda i: (i, 0)), pl.BlockSpec((N_EXPERTS, D), lambda i: (0, 0)),
                  pl.BlockSpec((N_EXPERTS, 1), lambda i: (0, 0))],
        out_specs=[tile, tile, pl.BlockSpec((N_EXPERTS, 128), lambda i: (0, 0))],
        out_shape=[jax.ShapeDtypeStruct((N_EXPERTS, T), F32), jax.ShapeDtypeStruct((N_EXPERTS, T), F32),
                   jax.ShapeDtypeStruct((N_EXPERTS, 128), F32)],
        compiler_params=_cp("arbitrary"),
        name="moe_router",
    )(h2, router_t, bias)


def _route_pos_kernel(gate_ref, rank_ref, cnt_ref, pos_ref, w_ref, te_ref, nt_ref, *, tm, nt_max):
    ei = lax.broadcasted_iota(jnp.int32, (N_EXPERTS, N_EXPERTS), 0)
    ej = lax.broadcasted_iota(jnp.int32, (N_EXPERTS, N_EXPERTS), 1)
    below = jnp.where(ej < ei, 1.0, 0.0)
    padded = jnp.ceil(cnt_ref[...] * (1.0 / tm)) * tm
    offs = _dot_hi(below, padded)
    rank = rank_ref[...]
    routed = rank >= 0.0
    pos = offs[:, 0:1] + rank
    slot = _dot(below.astype(BF16), jnp.where(routed, 1.0, 0.0).astype(BF16))
    gate = gate_ref[...]
    for k in range(TOP_K):
        mine = routed & (slot == float(k))
        pos_ref[k:k + 1, :] = jnp.where(mine, pos, 0.0).sum(0, keepdims=True).astype(jnp.int32)
        w_ref[k:k + 1, :] = jnp.where(mine, gate, 0.0).sum(0, keepdims=True)
    ends = (offs + padded)[:, 0:1]
    first = (lax.broadcasted_iota(jnp.int32, (N_EXPERTS, nt_max), 1) * tm).astype(F32)
    te = jnp.where(ends <= first, 1.0, 0.0).sum(0, keepdims=True)
    te_ref[...] = jnp.minimum(te, N_EXPERTS - 1.0).astype(jnp.int32)
    nt_ref[...] = (padded.sum(0, keepdims=True) * (1.0 / tm)).astype(jnp.int32)


def _route_pos(gate_t, rank, cnt, tt, tm, nt_max):
    T = gate_t.shape[1]
    tile = pl.BlockSpec((N_EXPERTS, tt), lambda i: (0, i))
    out = pl.BlockSpec((TOP_K, tt), lambda i: (0, i))
    return pl.pallas_call(
        functools.partial(_route_pos_kernel, tm=tm, nt_max=nt_max),
        grid=(T // tt,),
        in_specs=[tile, tile, pl.BlockSpec((N_EXPERTS, 128), lambda i: (0, 0))],
        out_specs=[out, out, pl.BlockSpec((1, nt_max), lambda i: (0, 0)), pl.BlockSpec((1, 128), lambda i: (0, 0))],
        out_shape=[jax.ShapeDtypeStruct((TOP_K, T), jnp.int32), jax.ShapeDtypeStruct((TOP_K, T), F32),
                   jax.ShapeDtypeStruct((1, nt_max), jnp.int32), jax.ShapeDtypeStruct((1, 128), jnp.int32)],
        compiler_params=_cp("arbitrary"),
        name="moe_positions",
    )(gate_t, rank, cnt)


def _gmm_kernel(te_ref, nt_ref, xs_ref, w1_ref, w3_ref, w2_ref, ys_ref):
    @pl.when(pl.program_id(0) < nt_ref[0])
    def _():
        xa, xb = _unpack_pairs(xs_ref[...])
        xa, xb = xa.astype(BF16), xb.astype(BF16)
        half = D // 2
        a = _dot(xa, w1_ref[:half, :].astype(BF16)) + _dot(xb, w1_ref[half:, :].astype(BF16))
        b = _dot(xa, w3_ref[:half, :].astype(BF16)) + _dot(xb, w3_ref[half:, :].astype(BF16))
        hid = (a * _sigmoid(a) * b).astype(BF16)
        ys_ref[...] = _pack_pairs(_dot(hid, w2_ref[...].astype(BF16)))


def _gmm(te, nt, xs, w1, w3, w2, l, tm):
    n_slots = xs.shape[0]
    ds = D_EXPERT
    live = lambda j, nt: jnp.minimum(j, nt[0] - 1)
    rows = pl.BlockSpec((tm, D // 2), lambda j, te, nt: (live(j, nt), 0))
    wspec = lambda s: pl.BlockSpec((None, None) + s, lambda j, te, nt: (l, te[live(j, nt)], 0, 0))
    return pl.pallas_call(
        _gmm_kernel,
        grid_spec=pltpu.PrefetchScalarGridSpec(
            num_scalar_prefetch=2,
            grid=(n_slots // tm,),
            in_specs=[rows, wspec((D, ds)), wspec((D, ds)), wspec((ds, D))],
            out_specs=rows),
        out_shape=jax.ShapeDtypeStruct((n_slots, D // 2), jnp.int32),
        compiler_params=_cp("arbitrary"),
        name="moe_grouped_ffn",
    )(te, nt, xs, w1, w3, w2)


def _combine_kernel(yk_ref, w_ref, hp_ref, s1_ref, s3_ref, s2_ref, x_ref, mod_ref, lg_ref, lb_ref, o_ref):
    w = w_ref[...]
    acc_a = acc_b = None
    for k in range(TOP_K):
        ya, yb = _unpack_pairs(yk_ref[k])
        wk = w[:, k:k + 1]
        acc_a = wk * ya if acc_a is None else acc_a + wk * ya
        acc_b = wk * yb if acc_b is None else acc_b + wk * yb
    ha, hb = _unpack_pairs(hp_ref[...])
    ha, hb = ha.astype(BF16), hb.astype(BF16)
    half = D // 2
    a = _dot(ha, s1_ref[:half, :]) + _dot(hb, s1_ref[half:, :])
    b = _dot(ha, s3_ref[:half, :]) + _dot(hb, s3_ref[half:, :])
    y = jnp.concatenate([acc_a, acc_b], axis=1) + _dot((a * _sigmoid(a) * b).astype(BF16), s2_ref[...])
    m = mod_ref[...]
    o_ref[...] = _layer_norm(DN_ALPHA * x_ref[...] + m[5:6, :] * y, lg_ref[...], lb_ref[...])


def _combine(yk, w, hp, s1, s3, s2, x1, mod, lg, lb, mod_row, bm):
    T = x1.shape[0]
    ds = D_EXPERT
    row = lambda n: pl.BlockSpec((bm, n), lambda i: (i, 0))
    fixed = lambda s: pl.BlockSpec(s, lambda i: (0, 0))
    return pl.pallas_call(
        _combine_kernel,
        grid=(T // bm,),
        in_specs=[pl.BlockSpec((TOP_K, bm, D // 2), lambda i: (0, i, 0)), row(TOP_K), row(D // 2),
                  fixed((D, ds)), fixed((D, ds)), fixed((ds, D)), row(D),
                  pl.BlockSpec((None, 6, D), lambda i: (mod_row(i), 0, 0)), fixed((1, D)), fixed((1, D))],
        out_specs=row(D),
        out_shape=jax.ShapeDtypeStruct((T, D), F32),
        compiler_params=_cp("arbitrary"),
        name="moe_combine_norm",
    )(yk, w, hp, s1, s3, s2, x1, mod, lg, lb)


def _sc_worker():
    return lax.axis_index("s") * SC_CORES + lax.axis_index("c")


def _sc_mesh():
    return plsc.VectorSubcoreMesh(core_axis_name="c", subcore_axis_name="s")


def _sc_gather(table, idx):
    N, W = idx.shape[0], table.shape[1]
    per_w = N // SC_WORKERS
    n_chunks = per_w // SC_ROWS

    def body(table_hbm, idx_hbm, out_hbm, idx_v, rows_v, sem):
        base = _sc_worker() * per_w
        pltpu.sync_copy(idx_hbm.at[pl.ds(base, per_w)], idx_v)

        @pl.loop(0, n_chunks)
        def _(c):
            off = pl.multiple_of(c * SC_ROWS, SC_ROWS)
            pltpu.async_copy(table_hbm.at[idx_v.at[pl.ds(off, SC_ROWS)]], rows_v, sem).wait()
            pltpu.sync_copy(rows_v, out_hbm.at[pl.ds(base + off, SC_ROWS)])

    return pl.kernel(
        body, out_type=jax.ShapeDtypeStruct((N, W), table.dtype), mesh=_sc_mesh(),
        scratch_types=[pltpu.VMEM((per_w,), jnp.int32), pltpu.VMEM((SC_ROWS, W), table.dtype),
                       pltpu.SemaphoreType.DMA],
        name="sc_gather",
    )(table, idx)


def _sc_dispatch(pos, table, n_slots):
    NP, (T, W) = pos.shape[0], table.shape
    per_w = n_slots // SC_WORKERS
    n_chunks = per_w // SC_ROWS
    scan = 8192

    def body(pos_hbm, table_hbm, out_hbm, pos_v, src_v, rows_v, sem):
        base = _sc_worker() * per_w
        lane = lax.iota(jnp.int32, SC_LANES)

        @pl.loop(0, per_w // SC_LANES)
        def _(j):
            o = pl.multiple_of(j * SC_LANES, SC_LANES)
            src_v[pl.ds(o, SC_LANES)] = (base + o + lane) & (T - 1)

        @pl.loop(0, NP // scan)
        def _(c):
            pltpu.sync_copy(pos_hbm.at[pl.ds(pl.multiple_of(c * scan, scan), scan)], pos_v)

            @pl.loop(0, scan // SC_LANES)
            def _(v):
                o = pl.multiple_of(v * SC_LANES, SC_LANES)
                p = pos_v[pl.ds(o, SC_LANES)] - base
                mine = (p >= 0) & (p < per_w)
                tok = (c * scan + o + lane) & (T - 1)
                plsc.store_scatter(src_v, [jnp.where(mine, p, 0)], tok, mask=mine)

        @pl.loop(0, n_chunks)
        def _(c):
            off = pl.multiple_of(c * SC_ROWS, SC_ROWS)
            pltpu.async_copy(table_hbm.at[src_v.at[pl.ds(off, SC_ROWS)]], rows_v, sem).wait()
            pltpu.sync_copy(rows_v, out_hbm.at[pl.ds(base + off, SC_ROWS)])

    return pl.kernel(
        body, out_type=jax.ShapeDtypeStruct((n_slots, W), table.dtype), mesh=_sc_mesh(),
        scratch_types=[pltpu.VMEM((scan,), jnp.int32), pltpu.VMEM((per_w,), jnp.int32),
                       pltpu.VMEM((SC_ROWS, W), table.dtype), pltpu.SemaphoreType.DMA],
        compiler_params=pltpu.CompilerParams(needs_layout_passes=False),
        name="sc_dispatch",
    )(pos, table)


def _rot_cols(w, q):
    a, b, c, d = w[..., :q], w[..., q:2 * q], w[..., 2 * q:3 * q], w[..., 3 * q:]
    return jnp.concatenate([-b, a, -d, c], -1)


def _prep_w_in(w):
    z = lambda n: jnp.zeros((D, n), w.dtype)
    qlat, ckv, kpe, hy = w[:, 0:256], w[:, 256:384], w[:, 384:416], w[:, 416:1184]
    wq, wk, wv = w[:, 1184:1440], w[:, 1440:1568], w[:, 1568:1696]
    nq, nk, nv, gate = w[:, 1696:1952], w[:, 1952:2208], w[:, 2208:2464], w[:, 2464:]
    wq_r = _rot_cols(wq.reshape(D, 4, 64), 16).reshape(D, 256)
    wk_r = _rot_cols(wk.reshape(D, 2, 64), 16).reshape(D, 128)
    kpe_r = _rot_cols(kpe, 8)
    cols = [qlat, ckv, z(64), kpe, z(32), hy, wq, wk, wv, nq, nk, nv, wq_r, wk_r, z(64), kpe_r, z(32), gate]
    return jnp.concatenate(cols, 1).astype(BF16)


def _prep_mla(w_uq, w_ukv):
    uq = w_uq.reshape(256, 4, 96)
    nope, pe = uq[..., :64], uq[..., 64:]
    z32 = jnp.zeros((256, 4, 32), w_uq.dtype)
    z64 = jnp.zeros((256, 4, 64), w_uq.dtype)
    wcat = jnp.concatenate([nope, pe, z32], -1).reshape(256, 512).astype(BF16)
    wrot = jnp.concatenate([z64, _rot_cols(pe, 8), z32], -1).reshape(256, 512).astype(BF16)
    ukv = w_ukv.reshape(128, 4, 128)
    wk = jnp.concatenate([ukv[..., :64], jnp.zeros((128, 4, 64), w_ukv.dtype)], -1).reshape(128, 512).astype(BF16)
    wv = ukv[..., 64:].reshape(128, 256).astype(BF16)
    return wcat, wrot, wk, wv


def _rope_tab(L, q):
    t = jnp.arange(L)
    inv = ROPE_BASE ** (-jnp.arange(q, dtype=F32) / q)
    ar = (t // GRID_W).astype(F32)[:, None] * inv[None, :]
    ac = (t % GRID_W).astype(F32)[:, None] * inv[None, :]
    cos = jnp.concatenate([jnp.cos(ar), jnp.cos(ar), jnp.cos(ac), jnp.cos(ac)], 1)
    sin = jnp.concatenate([jnp.sin(ar), jnp.sin(ar), jnp.sin(ac), jnp.sin(ac)], 1)
    return cos, sin


def _rope_tables(L):
    c8, s8 = _rope_tab(L, 8)
    c16, s16 = _rope_tab(L, 16)
    one, zero = jnp.ones((L, 64), F32), jnp.zeros((L, 64), F32)
    z32 = jnp.zeros((L, 32), F32)
    mla_q = (jnp.tile(jnp.concatenate([one, c8, z32], 1), (1, 4)), jnp.tile(jnp.concatenate([zero, s8, z32], 1), (1, 4)))
    mla_k = (jnp.concatenate([zero, c8, z32], 1), jnp.concatenate([zero, s8, z32], 1))
    win = (jnp.tile(c16, (1, 4)), jnp.tile(s16, (1, 4)), jnp.tile(c16, (1, 2)), jnp.tile(s16, (1, 2)))
    return mla_q + mla_k, win


def _hyena(proj, lp, dft, NB, Lb):
    cm, sm, smt = dft
    tm = min(Lb, 512)
    v, x1, x2 = _short_conv(proj, lp["hy_conv_w"], lp["hy_conv_b"].reshape(1, -1), NB, Lb)
    w1p = jnp.pad(lp["hy_w1"], ((0, 128 - lp["hy_w1"].shape[0]), (0, 0)))
    fs, nyq = _hy_filter(Lb, w1p, lp["hy_b1"].reshape(1, -1), lp["hy_w2"], lp["hy_b2"].reshape(1, -1), lp["hy_w3"],
                         lp["hy_sin_freq"], lp["hy_log_decay"].reshape(1, -1))
    gr, gi = _hy_gdft(cm, sm, fs, nyq, Lb, tm)
    skip = lp["hy_skip"].reshape(2, 1, HY_C)
    z = v
    for n, gate in enumerate((x1, x2)):
        yr, yi = _hy_fwd(cm, sm, z, gr, gi, n, NB, Lb, tm)
        z = _hy_inv(cm, smt, yr, yi, z, gate, skip, n, NB, Lb, tm)
    return z


def _layer(x, mod, lp, l, NB, Lb, mod_row_of_batch, dft, cache=None, tabs=None, na_bias=None):
    T = NB * Lb
    latent = cache is not None
    bm = 256
    mod_row = lambda i: mod_row_of_batch((i * bm) // Lb)
    bmp = min(Lb, 512)
    proj = _in_proj(x, mod, lp["w_in_p"], lambda i: mod_row_of_batch((i * bmp) // Lb), bmp)

    gq, gkv = lp["mla_q_norm"].reshape(1, -1), lp["mla_kv_norm"].reshape(1, -1)
    wcat, wrot, wk, wv = lp["mla_w"]
    q_all, ckv_n, kpe_r = _mla_q(proj, gq, gkv, wcat, wrot, tabs[0] if latent else None, Lb, bm)
    if latent:
        ckv_c, kpe_c, kc_c, vc_c, kd_c, vd_c = cache
        Lc = ckv_c.shape[1]
        kpe_cp = jnp.pad(kpe_c, ((0, 0), (0, 0), (64, 32)))
        ckv_all = jnp.concatenate([ckv_c, ckv_n.reshape(NB, Lb, 128)], 1).reshape(NB * (Lc + Lb), 128)
        kpe_all = jnp.concatenate([kpe_cp, kpe_r.reshape(NB, Lb, 128)], 1).reshape(NB * (Lc + Lb), 128)
        k_all, v_all = _mla_kv(ckv_all, kpe_all, wk, wv, 512)
        oa = _lat_mla_attention(q_all, k_all, v_all, NB, Lb, Lc + Lb, 256)
        oc = _lat_win_attention(proj, kc_c.reshape(NB, Lc, 128), vc_c.reshape(NB, Lc, 128), tabs[1],
                                lp["win_sink"], NB, Lb)
        od = _lat_na_attention(proj, kd_c.reshape(NB, Lc, 256), vd_c.reshape(NB, Lc, 256), na_bias, NB, Lb)
    else:
        k_all, v_all = _mla_kv(ckv_n, kpe_r, wk, wv, 512)
        oa, oc, od = _ctx_attention(proj, q_all, k_all, v_all, lp["win_sink"], NB, Lb)
    ob = _hyena(proj, lp, dft, NB, Lb)

    x1, h2, hp = _merge((oa, ob, oc, od), proj, lp["w_branch_b"], lp["w_out_b"], x, mod,
                        lp["ln1_g"].reshape(1, -1), lp["ln1_b"].reshape(1, -1), mod_row, bm)
    n_slots = T * TOP_K + N_EXPERTS * MOE_TM
    gate_t, rank, cnt = _router(h2, lp["moe_router"].T, lp["moe_bias"].reshape(-1, 1), 512)
    pos, w8, te, nt = _route_pos(gate_t, rank, cnt, 512, MOE_TM, n_slots // MOE_TM)
    xs = _sc_dispatch(pos.reshape(-1), hp, n_slots)
    ys = _gmm(te.reshape(-1), nt.reshape(-1)[:1], xs, lp["moe_w1"], lp["moe_w3"], lp["moe_w2"], l, MOE_TM)
    yk = _sc_gather(ys, pos.reshape(-1)).reshape(TOP_K, T, D // 2)
    x2 = _combine(yk, w8.T, hp, lp["sh_w1_b"], lp["sh_w3_b"], lp["sh_w2_b"], x1, mod,
                  lp["ln2_g"].reshape(1, -1), lp["ln2_b"].reshape(1, -1), mod_row, bm)
    return x2, (proj, ckv_n)


def kernel(x_prompt, x_sample, cache_mla_ckv, cache_mla_kpe, cache_win_k, cache_win_v, cache_na_k, cache_na_v, c, c_ctx, w_ada, b_ada, w_in, mla_q_norm, mla_kv_norm, mla_w_uq, mla_w_ukv, hy_conv_w, hy_conv_b, hy_w1, hy_b1, hy_w2, hy_b2, hy_w3, hy_sin_freq, hy_log_decay, hy_skip, win_sink, na_rpb, w_branch, w_out, ln1_g, ln1_b, ln2_g, ln2_b, moe_router, moe_bias, moe_w1, moe_w3, moe_w2, sh_w1, sh_w3, sh_w2):
    B, S, _ = x_prompt.shape
    DB, DS, _ = x_sample.shape
    xp = x_prompt.reshape(B * S, D)
    xs = x_sample.reshape(DB * DS, D)
    cvec = jnp.concatenate([c_ctx[None, :], c, jnp.zeros((8 - 1 - DB, D), F32)], 0)
    dft_ctx = _dft_mats(S)
    dft_lat = _dft_mats(DS)
    tabs = _rope_tables(DS)
    new = [[] for _ in range(6)]
    for l in range(DEPTH):
        lp = dict(w_in_p=_prep_w_in(w_in[l]), mla_q_norm=mla_q_norm[l], mla_kv_norm=mla_kv_norm[l],
                  mla_w=_prep_mla(mla_w_uq[l], mla_w_ukv[l]), hy_conv_w=hy_conv_w[l], hy_conv_b=hy_conv_b[l],
                  hy_w1=hy_w1[l], hy_b1=hy_b1[l], hy_w2=hy_w2[l], hy_b2=hy_b2[l], hy_w3=hy_w3[l],
                  hy_sin_freq=hy_sin_freq[l], hy_log_decay=hy_log_decay[l], hy_skip=hy_skip[l],
                  win_sink=win_sink[l], w_branch_b=w_branch[l].astype(BF16), w_out_b=w_out[l].astype(BF16),
                  ln1_g=ln1_g[l], ln1_b=ln1_b[l], ln2_g=ln2_g[l], ln2_b=ln2_b[l],
                  moe_router=moe_router[l], moe_bias=moe_bias[l], moe_w1=moe_w1, moe_w3=moe_w3, moe_w2=moe_w2,
                  sh_w1_b=sh_w1[l].astype(BF16), sh_w3_b=sh_w3[l].astype(BF16), sh_w2_b=sh_w2[l].astype(BF16))
        mod = _modulation(cvec, w_ada, b_ada, l)
        xp, (proj, ckv_n) = _layer(xp, mod, lp, l, B, S, lambda b: 0, dft_ctx)
        new[0].append(ckv_n.reshape(B, S, 128))
        new[1].append(proj[:, P_KPE + 64:P_KPE + 96].reshape(B, S, 32))
        new[2].append(proj[:, P_WK:P_WK + 128].reshape(B, S, 2, 64))
        new[3].append(proj[:, P_WV:P_WV + 128].reshape(B, S, 2, 64))
        new[4].append(proj[:, P_NK:P_NK + 256].reshape(B, S, 4, 64))
        new[5].append(proj[:, P_NV:P_NV + 256].reshape(B, S, 4, 64))
        cache = (cache_mla_ckv[:, l], cache_mla_kpe[:, l], cache_win_k[:, l], cache_win_v[:, l],
                 cache_na_k[:, l], cache_na_v[:, l])
        xs, _ = _layer(xs, mod, lp, l, DB, DS, lambda b: 1 + b, dft_lat, cache=cache, tabs=tabs,
                       na_bias=_na_bias(na_rpb[l]))
    return (xp.reshape(B, S, D), xs.reshape(DB, DS, D)) + tuple(jnp.stack(t, 1) for t in new)
```

```python
import functools
import math

import jax
import jax.numpy as jnp
from jax import lax
from jax.experimental import pallas as pl
from jax.experimental.pallas import tpu as pltpu
from jax.experimental.pallas import tpu_sc as plsc

F32 = jnp.float32
BF16 = jnp.bfloat16

D = 1024
DEPTH = 2
GRID_W = 64
HEAD_DIM = 64
MLA_SCALE = 96 ** -0.5
ATT_SCALE = HEAD_DIM ** -0.5
HY_C = 256
HY_BANDS = 8
NA_KH = 8
NA_KW = 16
N_EXPERTS = 64
N_GROUPS = 8
TOP_K = 8
TOPK_GROUPS = 4
D_EXPERT = 256
ROUTED_SCALE = 2.5
ROPE_BASE = 10000.0
LN_EPS = 1e-5
RMS_EPS = 1e-6
NEG = -1e30
DN_ALPHA = (2 * DEPTH) ** 0.25

P_QLAT, P_CKV, P_KPE, P_HY = 0, 256, 384, 512
P_WQ, P_WK, P_WV = 1280, 1536, 1664
P_NQ, P_NK, P_NV = 1792, 2048, 2304
P_WQR, P_WKR, P_KPER, P_GATE = 2560, 2816, 2944, 3072
N_PROJ = 7168

VMEM_LIMIT = 56 * 1024 * 1024

SC_CORES = 2
SC_SUBCORES = 16
SC_LANES = 16
SC_WORKERS = SC_CORES * SC_SUBCORES
SC_ROWS = 64

MOE_TM = 512


def _cp(*sem):
    return pltpu.CompilerParams(dimension_semantics=sem, vmem_limit_bytes=VMEM_LIMIT)


def _sigmoid(x):
    return 1.0 / (1.0 + jnp.exp(-x))


def _dot(a, b):
    return jnp.dot(a, b, preferred_element_type=F32)


def _dot_nt(a, b):
    return lax.dot_general(a, b, (((1,), (1,)), ((), ())), preferred_element_type=F32)


def _dot_hi(a, b):
    return jnp.dot(a, b, preferred_element_type=F32, precision=lax.Precision.HIGHEST)


def _pack_pairs(x):
    w = x.shape[1] // 2
    hi = lax.bitcast_convert_type(x[:, :w].astype(BF16).astype(F32), jnp.int32)
    lo = lax.bitcast_convert_type(x[:, w:].astype(BF16).astype(F32), jnp.int32)
    return hi | lax.shift_right_logical(lo, 16)


def _unpack_pairs(p):
    hi = lax.bitcast_convert_type(p & jnp.int32(-65536), F32)
    lo = lax.bitcast_convert_type(lax.shift_left(p, 16), F32)
    return hi, lo


def _layer_norm(x, g, b):
    mu = jnp.mean(x, -1, keepdims=True)
    xc = x - mu
    var = jnp.mean(xc * xc, -1, keepdims=True)
    return xc * lax.rsqrt(var + LN_EPS) * g + b


def _rms_norm(x, g):
    return x * lax.rsqrt(jnp.mean(x * x, -1, keepdims=True) + RMS_EPS) * g


def _mod_kernel(c_ref, w_ref, b_ref, o_ref):
    c = c_ref[...]
    a = (c * _sigmoid(c)).astype(BF16)
    o_ref[...] = _dot(a, w_ref[...].astype(BF16)) + b_ref[...]


def _modulation(cvec, w_ada, b_ada, l):
    out = pl.pallas_call(
        _mod_kernel,
        grid=(6,),
        in_specs=[pl.BlockSpec((8, D), lambda j: (0, 0)),
                  pl.BlockSpec((None, D, D), lambda j: (l, 0, j)),
                  pl.BlockSpec((None, 1, D), lambda j: (l, 0, j))],
        out_specs=pl.BlockSpec((8, D), lambda j: (0, j)),
        out_shape=jax.ShapeDtypeStruct((8, 6 * D), F32),
        compiler_params=_cp("arbitrary"),
        name="modulation",
    )(cvec, w_ada, b_ada.reshape(DEPTH, 1, 6 * D))
    return out.reshape(8, 6, D)


def _inproj_kernel(x_ref, mod_ref, w_ref, o_ref):
    m = mod_ref[...]
    h = x_ref[...] * (1.0 + m[1:2, :]) + m[0:1, :]
    o_ref[...] = _dot(h.astype(BF16), w_ref[...])


def _in_proj(x, mod, w_p, mod_row, bm):
    T = x.shape[0]
    bn = 1024
    return pl.pallas_call(
        _inproj_kernel,
        grid=(N_PROJ // bn, T // bm),
        in_specs=[pl.BlockSpec((bm, D), lambda j, i: (i, 0)),
                  pl.BlockSpec((None, 6, D), lambda j, i: (mod_row(i), 0, 0)),
                  pl.BlockSpec((D, bn), lambda j, i: (0, j))],
        out_specs=pl.BlockSpec((bm, bn), lambda j, i: (i, j)),
        out_shape=jax.ShapeDtypeStruct((T, N_PROJ), F32),
        compiler_params=_cp("arbitrary", "arbitrary"),
        name="in_proj",
    )(x, mod, w_p)


def _mla_q_kernel(*refs, rope):
    if rope:
        (ql_ref, ckv_ref, kpe_ref, kper_ref, gq_ref, gkv_ref, wc_ref, wr_ref,
         cq_ref, sq_ref, ck_ref, sk_ref, q_ref, ckvn_ref, kpeo_ref) = refs
    else:
        ql_ref, ckv_ref, kpe_ref, gq_ref, gkv_ref, wc_ref, q_ref, ckvn_ref, kpeo_ref = refs
    qn = _rms_norm(ql_ref[...], gq_ref[...]).astype(BF16)
    q = _dot(qn, wc_ref[...])
    if rope:
        q = q * cq_ref[...] + _dot(qn, wr_ref[...]) * sq_ref[...]
        kpeo_ref[...] = kpe_ref[...] * ck_ref[...] + kper_ref[...] * sk_ref[...]
    else:
        kpeo_ref[...] = kpe_ref[...]
    q_ref[...] = (q * MLA_SCALE).astype(BF16)
    ckvn_ref[...] = _rms_norm(ckv_ref[...], gkv_ref[...])


def _mla_q(proj, gq, gkv, wcat, wrot, tabs, Lb, bm):
    T = proj.shape[0]
    rope = tabs is not None
    nl = Lb // bm
    col = lambda c: (lambda i: (i, c))
    fixed = lambda i: (0, 0)
    in_specs = [pl.BlockSpec((bm, 256), col(P_QLAT // 256)),
                pl.BlockSpec((bm, 128), col(P_CKV // 128)),
                pl.BlockSpec((bm, 128), col(P_KPE // 128))]
    args = [proj, proj, proj]
    if rope:
        in_specs.append(pl.BlockSpec((bm, 128), col(P_KPER // 128)))
        args.append(proj)
    in_specs += [pl.BlockSpec((1, 256), fixed), pl.BlockSpec((1, 128), fixed), pl.BlockSpec((256, 512), fixed)]
    args += [gq, gkv, wcat]
    if rope:
        cq, sq, ck, sk = tabs
        pos = lambda i: (i % nl, 0)
        in_specs += [pl.BlockSpec((256, 512), fixed), pl.BlockSpec((bm, 512), pos), pl.BlockSpec((bm, 512), pos),
                     pl.BlockSpec((bm, 128), pos), pl.BlockSpec((bm, 128), pos)]
        args += [wrot, cq, sq, ck, sk]
    return pl.pallas_call(
        functools.partial(_mla_q_kernel, rope=rope),
        grid=(T // bm,),
        in_specs=in_specs,
        out_specs=[pl.BlockSpec((bm, 512), lambda i: (i, 0)),
                   pl.BlockSpec((bm, 128), lambda i: (i, 0)),
                   pl.BlockSpec((bm, 128), lambda i: (i, 0))],
        out_shape=[jax.ShapeDtypeStruct((T, 512), BF16),
                   jax.ShapeDtypeStruct((T, 128), F32),
                   jax.ShapeDtypeStruct((T, 128), F32)],
        compiler_params=_cp("arbitrary"),
        name="mla_q",
    )(*args)


def _mla_kv_kernel(ckv_ref, kpe_ref, wk_ref, wv_ref, k_ref, v_ref):
    c = ckv_ref[...].astype(BF16)
    kpe = kpe_ref[...]
    k_ref[...] = (_dot(c, wk_ref[...]) + jnp.concatenate([kpe] * 4, axis=1)).astype(BF16)
    v_ref[...] = _dot(c, wv_ref[...]).astype(BF16)


def _mla_kv(ckv, kpe, wk, wv, bm):
    Tk = ckv.shape[0]
    return pl.pallas_call(
        _mla_kv_kernel,
        grid=(Tk // bm,),
        in_specs=[pl.BlockSpec((bm, 128), lambda i: (i, 0)),
                  pl.BlockSpec((bm, 128), lambda i: (i, 0)),
                  pl.BlockSpec((128, 512), lambda i: (0, 0)),
                  pl.BlockSpec((128, 256), lambda i: (0, 0))],
        out_specs=[pl.BlockSpec((bm, 512), lambda i: (i, 0)),
                   pl.BlockSpec((bm, 256), lambda i: (i, 0))],
        out_shape=[jax.ShapeDtypeStruct((Tk, 512), BF16),
                   jax.ShapeDtypeStruct((Tk, 256), BF16)],
        compiler_params=_cp("arbitrary"),
        name="mla_kv",
    )(ckv, kpe, wk, wv)


def _attn_core(q, kvs, masks, sink):
    ss = []
    for (k, _), mk in zip(kvs, masks):
        s = _dot_nt(q, k)
        if mk is not None:
            s = s + mk[1] if mk[0] == "add" else jnp.where(mk[1], s, NEG)
        ss.append(s)
    m = ss[0].max(-1, keepdims=True)
    for s in ss[1:]:
        m = jnp.maximum(m, s.max(-1, keepdims=True))
    if sink is not None:
        m = jnp.maximum(m, sink)
    den = None
    acc = None
    for s, (_, v) in zip(ss, kvs):
        p = jnp.exp(s - m)
        d = p.sum(-1, keepdims=True)
        a = _dot(p.astype(BF16), v)
        den = d if den is None else den + d
        acc = a if acc is None else acc + a
    if sink is not None:
        den = den + jnp.exp(sink - m)
    return acc / den


def _ctx_attn_kernel(qm_ref, km_ref, vm_ref, wq_ref, wk_ref, wv_ref, nq_ref, nk_ref, nv_ref, sink_ref,
                     om_ref, ow_ref, on_ref):
    for h in range(4):
        q = qm_ref[:, 128 * h:128 * (h + 1)]
        k = km_ref[:, 128 * h:128 * (h + 1)]
        v = vm_ref[:, 64 * h:64 * (h + 1)]
        om_ref[:, 64 * h:64 * (h + 1)] = _attn_core(q, [(k, v)], [None], None)
    for h in range(4):
        g = h // 2
        q = (wq_ref[:, 64 * h:64 * (h + 1)] * ATT_SCALE).astype(BF16)
        k = wk_ref[:, 64 * g:64 * (g + 1)].astype(BF16)
        v = wv_ref[:, 64 * g:64 * (g + 1)].astype(BF16)
        ow_ref[:, 64 * h:64 * (h + 1)] = _attn_core(q, [(k, v)], [None], sink_ref[h])
    for h in range(4):
        q = (nq_ref[:, 64 * h:64 * (h + 1)] * ATT_SCALE).astype(BF16)
        k = nk_ref[:, 64 * h:64 * (h + 1)].astype(BF16)
        v = nv_ref[:, 64 * h:64 * (h + 1)].astype(BF16)
        on_ref[:, 64 * h:64 * (h + 1)] = _attn_core(q, [(k, v)], [None], None)


def _ctx_attention(proj, q_all, k_all, v_all, sink, NB, Lb):
    T = proj.shape[0]
    pc = lambda w, off: pl.BlockSpec((Lb, w), lambda b: (b, off // w))
    row = lambda w: pl.BlockSpec((Lb, w), lambda b: (b, 0))
    return pl.pallas_call(
        _ctx_attn_kernel,
        grid=(NB,),
        in_specs=[row(512), row(512), row(256),
                  pc(256, P_WQ), pc(128, P_WK), pc(128, P_WV),
                  pc(256, P_NQ), pc(256, P_NK), pc(256, P_NV),
                  pl.BlockSpec(memory_space=pltpu.SMEM)],
        out_specs=[row(256), row(256), row(256)],
        out_shape=[jax.ShapeDtypeStruct((T, 256), F32)] * 3,
        compiler_params=_cp("arbitrary"),
        name="ctx_attention",
    )(q_all, k_all, v_all, proj, proj, proj, proj, proj, proj, sink)


def _lat_mla_kernel(q_ref, k_ref, v_ref, o_ref):
    for h in range(4):
        q = q_ref[:, 128 * h:128 * (h + 1)]
        k = k_ref[:, 128 * h:128 * (h + 1)]
        v = v_ref[:, 64 * h:64 * (h + 1)]
        o_ref[:, 64 * h:64 * (h + 1)] = _attn_core(q, [(k, v)], [None], None)


def _lat_mla_attention(q_all, k_all, v_all, NB, Lb, Lk, tq):
    T = q_all.shape[0]
    nq = Lb // tq
    return pl.pallas_call(
        _lat_mla_kernel,
        grid=(NB, nq),
        in_specs=[pl.BlockSpec((tq, 512), lambda b, i: (b * nq + i, 0)),
                  pl.BlockSpec((Lk, 512), lambda b, i: (b, 0)),
                  pl.BlockSpec((Lk, 256), lambda b, i: (b, 0))],
        out_specs=pl.BlockSpec((tq, 256), lambda b, i: (b * nq + i, 0)),
        out_shape=jax.ShapeDtypeStruct((T, 256), F32),
        compiler_params=_cp("arbitrary", "arbitrary"),
        name="lat_mla_attention",
    )(q_all, k_all, v_all)


def _lat_win_kernel(q_ref, qr_ref, k_ref, kr_ref, v_ref, kc_ref, vc_ref, cq_ref, sq_ref, ck_ref, sk_ref,
                    sink_ref, o_ref, *, Lb):
    i = pl.program_id(1)
    start = pl.multiple_of(jnp.clip((i - 1) * 128, 0, Lb - 384), 128)
    win = pl.ds(start, 384)
    q = (q_ref[...] * cq_ref[...] + qr_ref[...] * sq_ref[...]) * ATT_SCALE
    kk = k_ref[win, :] * ck_ref[win, :] + kr_ref[win, :] * sk_ref[win, :]
    vv = v_ref[win, :]
    qpos = i * 128 + lax.broadcasted_iota(jnp.int32, (128, 384), 0)
    kpos = start + lax.broadcasted_iota(jnp.int32, (128, 384), 1)
    valid = jnp.abs(qpos - kpos) <= 128
    for h in range(4):
        g = h // 2
        sl = slice(64 * g, 64 * (g + 1))
        kvs = [(kk[:, sl].astype(BF16), vv[:, sl].astype(BF16)),
               (kc_ref[:, sl].astype(BF16), vc_ref[:, sl].astype(BF16))]
        qh = q[:, 64 * h:64 * (h + 1)].astype(BF16)
        o_ref[:, 64 * h:64 * (h + 1)] = _attn_core(qh, kvs, [("keep", valid), None], sink_ref[h])


def _lat_win_attention(proj, kc, vc, tabs, sink, NB, Lb):
    T = proj.shape[0]
    nb = Lb // 128
    Lc = kc.shape[1]
    cq, sq, ck, sk = tabs
    qspec = lambda off: pl.BlockSpec((128, 256), lambda b, i: (b * nb + i, off // 256))
    kspec = lambda off: pl.BlockSpec((Lb, 128), lambda b, i: (b, off // 128))
    cspec = pl.BlockSpec((None, Lc, 128), lambda b, i: (b, 0, 0))
    return pl.pallas_call(
        functools.partial(_lat_win_kernel, Lb=Lb),
        grid=(NB, nb),
        in_specs=[qspec(P_WQ), qspec(P_WQR), kspec(P_WK), kspec(P_WKR), kspec(P_WV), cspec, cspec,
                  pl.BlockSpec((128, 256), lambda b, i: (i, 0)), pl.BlockSpec((128, 256), lambda b, i: (i, 0)),
                  pl.BlockSpec((Lb, 128), lambda b, i: (0, 0)), pl.BlockSpec((Lb, 128), lambda b, i: (0, 0)),
                  pl.BlockSpec(memory_space=pltpu.SMEM)],
        out_specs=pl.BlockSpec((128, 256), lambda b, i: (b * nb + i, 0)),
        out_shape=jax.ShapeDtypeStruct((T, 256), F32),
        compiler_params=_cp("arbitrary", "arbitrary"),
        name="lat_win_attention",
    )(proj, proj, proj, proj, proj, kc, vc, cq, sq, ck, sk, sink)


def _na_bias_kernel(rpb_ref, o_ref):
    h = pl.program_id(0)
    qc = lax.broadcasted_iota(jnp.int32, (GRID_W, GRID_W), 0)
    kc = lax.broadcasted_iota(jnp.int32, (GRID_W, GRID_W), 1)
    dc = kc - qc + (NA_KW - 1)
    wstart = jnp.clip(qc - NA_KW // 2, 0, GRID_W - NA_KW)
    ok = (kc >= wstart) & (kc < wstart + NA_KW)
    n_dc = 2 * NA_KW - 1
    n_dr = 2 * NA_KH - 1
    tabs = []
    for dr in range(n_dr):
        t = jnp.zeros((GRID_W, GRID_W), F32)
        for j in range(n_dc):
            t = jnp.where(dc == j, rpb_ref[(h * n_dr + dr) * n_dc + j], t)
        tabs.append(jnp.where(ok, t, NEG))
    for o in range(NA_KH):
        for a in range(NA_KH):
            o_ref[o, :, GRID_W * a:GRID_W * (a + 1)] = tabs[a + NA_KH - 1 - o]


def _na_bias(rpb):
    H = rpb.shape[0]
    return pl.pallas_call(
        _na_bias_kernel,
        grid=(H,),
        in_specs=[pl.BlockSpec(memory_space=pltpu.SMEM)],
        out_specs=pl.BlockSpec((None, NA_KH, GRID_W, NA_KH * GRID_W), lambda h: (h, 0, 0, 0)),
        out_shape=jax.ShapeDtypeStruct((H, NA_KH, GRID_W, NA_KH * GRID_W), F32),
        compiler_params=_cp("arbitrary"),
        name="na_bias",
    )(rpb.reshape(-1))


def _lat_na_kernel(q_ref, k_ref, v_ref, kc_ref, vc_ref, bias_ref, o_ref, *, rows):
    r = pl.program_id(1)
    start = pl.multiple_of(jnp.clip(r - NA_KH // 2, 0, rows - NA_KH) * GRID_W, GRID_W)
    win = pl.ds(start, NA_KH * GRID_W)
    kk = k_ref[win, :]
    vv = v_ref[win, :]
    q = q_ref[...] * ATT_SCALE
    for h in range(4):
        sl = slice(64 * h, 64 * (h + 1))
        kvs = [(kk[:, sl].astype(BF16), vv[:, sl].astype(BF16)),
               (kc_ref[:, sl].astype(BF16), vc_ref[:, sl].astype(BF16))]
        o_ref[:, sl] = _attn_core(q[:, sl].astype(BF16), kvs, [("add", bias_ref[h]), None], None)


def _lat_na_attention(proj, kc, vc, bias, NB, Lb):
    T = proj.shape[0]
    rows = Lb // GRID_W
    Lc = kc.shape[1]
    kspec = lambda off: pl.BlockSpec((Lb, 256), lambda b, r: (b, off // 256))
    cspec = pl.BlockSpec((None, Lc, 256), lambda b, r: (b, 0, 0))
    bspec = pl.BlockSpec((4, None, GRID_W, NA_KH * GRID_W),
                         lambda b, r: (0, r - jnp.clip(r - NA_KH // 2, 0, rows - NA_KH), 0, 0))
    return pl.pallas_call(
        functools.partial(_lat_na_kernel, rows=rows),
        grid=(NB, rows),
        in_specs=[pl.BlockSpec((GRID_W, 256), lambda b, r: (b * rows + r, P_NQ // 256)),
                  kspec(P_NK), kspec(P_NV), cspec, cspec, bspec],
        out_specs=pl.BlockSpec((GRID_W, 256), lambda b, r: (b * rows + r, 0)),
        out_shape=jax.ShapeDtypeStruct((T, 256), F32),
        compiler_params=_cp("arbitrary", "arbitrary"),
        name="lat_na_attention",
    )(proj, proj, proj, kc, vc, bias)


def _short_conv_kernel(a_ref, b_ref, c_ref, w_ref, bias_ref, oa_ref, ob_ref, oc_ref, *, L):
    t = lax.broadcasted_iota(jnp.int32, (L, HY_C), 0)
    for n, (x_ref, o_ref) in enumerate(((a_ref, oa_ref), (b_ref, ob_ref), (c_ref, oc_ref))):
        sl = slice(HY_C * n, HY_C * (n + 1))
        x = x_ref[...]
        prev = jnp.where(t == 0, 0.0, pltpu.roll(x, 1, axis=0))
        nxt = jnp.where(t == L - 1, 0.0, pltpu.roll(x, L - 1, axis=0))
        o_ref[...] = prev * w_ref[0:1, sl] + x * w_ref[1:2, sl] + nxt * w_ref[2:3, sl] + bias_ref[:, sl]


def _short_conv(proj, w, b, NB, Lb):
    T = proj.shape[0]
    spec = lambda c: pl.BlockSpec((Lb, HY_C), lambda i: (i, c))
    return pl.pallas_call(
        functools.partial(_short_conv_kernel, L=Lb),
        grid=(NB,),
        in_specs=[spec(P_HY // HY_C), spec(P_HY // HY_C + 1), spec(P_HY // HY_C + 2),
                  pl.BlockSpec((3, 3 * HY_C), lambda i: (0, 0)), pl.BlockSpec((1, 3 * HY_C), lambda i: (0, 0))],
        out_specs=[spec(0)] * 3,
        out_shape=[jax.ShapeDtypeStruct((T, HY_C), F32)] * 3,
        compiler_params=_cp("arbitrary"),
        name="hyena_short_conv",
    )(proj, proj, proj, w, b)


def _hy_filter_kernel(w1_ref, b1_ref, w2_ref, b2_ref, w3_ref, freq_ref, ld_ref, fs_ref, nyq_ref, *, L):
    ti = lax.broadcasted_iota(jnp.int32, (L, 128), 0)
    t = ti.astype(F32)
    j = lax.broadcasted_iota(jnp.int32, (L, 128), 1)
    band = jnp.where(j <= HY_BANDS, j - 1, j - 1 - HY_BANDS).astype(F32)
    ang = (2.0 * math.pi / L) * t * band
    tn = t / L
    z = jnp.where(j == 0, tn, jnp.where(j <= HY_BANDS, jnp.cos(ang),
                                        jnp.where(j <= 2 * HY_BANDS, -jnp.sin(ang), 0.0)))
    a = jnp.sin(freq_ref[0:1, :] * (_dot_hi(z, w1_ref[...]) + b1_ref[...]))
    a = jnp.sin(freq_ref[1:2, :] * (_dot_hi(a, w2_ref[...]) + b2_ref[...]))
    filt = _dot_hi(a, w3_ref[...])
    tcol = lax.broadcasted_iota(jnp.int32, (L, 4 * HY_C), 0)
    filt = filt * jnp.exp(-(tcol.astype(F32) / L) * jnp.exp(ld_ref[...]))
    t1 = lax.broadcasted_iota(jnp.int32, (L, HY_C), 0)
    sign = jnp.where(t1 % 2 == 0, 1.0, -1.0)
    for n in range(2):
        fwd = filt[:, 2 * HY_C * n:2 * HY_C * n + HY_C]
        bwd = jnp.where(t1 == 0, 0.0, filt[:, 2 * HY_C * n + HY_C:2 * HY_C * (n + 1)])
        tot = fwd + bwd
        fs_ref[:, HY_C * n:HY_C * (n + 1)] = tot
        fs_ref[:, 2 * HY_C + HY_C * n:2 * HY_C + HY_C * (n + 1)] = fwd - bwd
        nyq_ref[:, HY_C * n:HY_C * (n + 1)] = (tot * sign).sum(0, keepdims=True)


def _hy_filter(L, w1p, b1, w2, b2, w3, freq, ld):
    full = lambda s: pl.BlockSpec(s, lambda: tuple(0 for _ in s))
    return pl.pallas_call(
        functools.partial(_hy_filter_kernel, L=L),
        in_specs=[full((128, 64)), full((1, 64)), full((64, 64)), full((1, 64)), full((64, 4 * HY_C)),
                  full((2, 64)), full((1, 4 * HY_C))],
        out_specs=[full((L, 4 * HY_C)), full((1, 2 * HY_C))],
        out_shape=[jax.ShapeDtypeStruct((L, 4 * HY_C), F32), jax.ShapeDtypeStruct((1, 2 * HY_C), F32)],
        compiler_params=pltpu.CompilerParams(vmem_limit_bytes=VMEM_LIMIT),
        name="hyena_filter",
    )(w1p, b1, w2, b2, w3, freq, ld)


def _hy_gdft_kernel(cm_ref, sm_ref, fs_ref, nyq_ref, gr_ref, gi_ref, *, tm):
    m = pl.program_id(0)
    f = fs_ref[...].astype(BF16)
    gr_ref[...] = _dot(cm_ref[...], f[:, :2 * HY_C])
    gi = _dot(sm_ref[...], f[:, 2 * HY_C:])
    row = m * tm + lax.broadcasted_iota(jnp.int32, (tm, 2 * HY_C), 0)
    gi_ref[...] = jnp.where(row == 0, nyq_ref[...], gi)


def _hy_gdft(cm, sm, fs, nyq, L, tm):
    return pl.pallas_call(
        functools.partial(_hy_gdft_kernel, tm=tm),
        grid=(L // tm,),
        in_specs=[pl.BlockSpec((tm, L), lambda m: (m, 0)), pl.BlockSpec((tm, L), lambda m: (m, 0)),
                  pl.BlockSpec((L, 4 * HY_C), lambda m: (0, 0)), pl.BlockSpec((1, 2 * HY_C), lambda m: (0, 0))],
        out_specs=[pl.BlockSpec((tm, 2 * HY_C), lambda m: (m, 0))] * 2,
        out_shape=[jax.ShapeDtypeStruct((L, 2 * HY_C), F32)] * 2,
        compiler_params=_cp("arbitrary"),
        name="hyena_filter_dft",
    )(cm, sm, fs, nyq)


def _hy_fwd_kernel(cm_ref, sm_ref, z_ref, gr_ref, gi_ref, yr_ref, yi_ref, *, L, tm):
    m = pl.program_id(1)
    zb = z_ref[...].astype(BF16)
    zr = _dot(cm_ref[...], zb)
    zi = _dot(sm_ref[...], zb)
    gr = gr_ref[...]
    gi = gi_ref[...]
    row0 = (m * tm + lax.broadcasted_iota(jnp.int32, (tm, HY_C), 0)) == 0
    s = jnp.where(row0, 0.5 / L, 1.0 / L)
    zigi = zi * gi
    yr_ref[...] = ((zr * gr - jnp.where(row0, 0.0, zigi)) * s).astype(BF16)
    yi_ref[...] = (jnp.where(row0, zigi, zr * gi + zi * gr) * s).astype(BF16)


def _hy_fwd(cm, sm, z, gr, gi, n, NB, Lb, tm):
    T = z.shape[0]
    nm = Lb // tm
    return pl.pallas_call(
        functools.partial(_hy_fwd_kernel, L=Lb, tm=tm),
        grid=(NB, nm),
        in_specs=[pl.BlockSpec((tm, Lb), lambda b, m: (m, 0)), pl.BlockSpec((tm, Lb), lambda b, m: (m, 0)),
                  pl.BlockSpec((Lb, HY_C), lambda b, m: (b, 0)),
                  pl.BlockSpec((tm, HY_C), lambda b, m: (m, n)), pl.BlockSpec((tm, HY_C), lambda b, m: (m, n))],
        out_specs=[pl.BlockSpec((tm, HY_C), lambda b, m: (b * nm + m, 0))] * 2,
        out_shape=[jax.ShapeDtypeStruct((T, HY_C), BF16)] * 2,
        compiler_params=_cp("arbitrary", "arbitrary"),
        name="hyena_fwd_dft",
    )(cm, sm, z, gr, gi)


def _hy_inv_kernel(cm_ref, smt_ref, yr_ref, yi_ref, z_ref, g_ref, skip_ref, o_ref):
    conv = _dot(cm_ref[...], yr_ref[...]) + _dot(smt_ref[...], yi_ref[...])
    z = z_ref[...]
    o_ref[...] = g_ref[...] * (conv + skip_ref[...] * z)


def _hy_inv(cm, smt, yr, yi, z, gate, skip, n, NB, Lb, tm):
    T = z.shape[0]
    nm = Lb // tm
    tile = pl.BlockSpec((tm, HY_C), lambda b, m: (b * nm + m, 0))
    seq = pl.BlockSpec((Lb, HY_C), lambda b, m: (b, 0))
    return pl.pallas_call(
        _hy_inv_kernel,
        grid=(NB, nm),
        in_specs=[pl.BlockSpec((tm, Lb), lambda b, m: (m, 0)), pl.BlockSpec((tm, Lb), lambda b, m: (m, 0)),
                  seq, seq, tile, tile, pl.BlockSpec((None, 1, HY_C), lambda b, m: (n, 0, 0))],
        out_specs=tile,
        out_shape=jax.ShapeDtypeStruct((T, HY_C), F32),
        compiler_params=_cp("arbitrary", "arbitrary"),
        name="hyena_inv_dft",
    )(cm, smt, yr, yi, z, gate, skip)


def _dft_mats(L):
    k = jnp.arange(L, dtype=jnp.int32)
    blk = 64

    def trig(mult):
        ang = ((mult[:, None] * k[None, :]) % (2 * L)).astype(F32) * (math.pi / L)
        return jnp.cos(ang), jnp.sin(ang)

    ca, sa = trig(jnp.arange(L // blk, dtype=jnp.int32) * blk)
    cb, sb = trig(jnp.arange(blk, dtype=jnp.int32))
    cm = (ca[:, None, :] * cb[None] - sa[:, None, :] * sb[None]).reshape(L, L)
    s = -(sa[:, None, :] * cb[None] + ca[:, None, :] * sb[None]).reshape(L, L)
    alt = jnp.where(k % 2 == 0, 1.0, -1.0).astype(F32)
    sm = jnp.where(k[:, None] == 0, alt[None, :], s)
    smt = jnp.where(k[None, :] == 0, alt[:, None], s)
    return cm.astype(BF16), sm.astype(BF16), smt.astype(BF16)


def _merge_kernel(oa_ref, ob_ref, oc_ref, od_ref, g0_ref, g1_ref, g2_ref, g3_ref, wb_ref, wo_ref, x_ref, mod_ref,
                  lg_ref, lb_ref, x1_ref, h2_ref, hp_ref):
    acc = None
    for o_ref, g_ref, i in ((oa_ref, g0_ref, 0), (ob_ref, g1_ref, 1), (oc_ref, g2_ref, 2), (od_ref, g3_ref, 3)):
        y = _sigmoid(g_ref[...]) * _dot(o_ref[...].astype(BF16), wb_ref[i])
        acc = y if acc is None else acc + y
    mix = _dot(acc.astype(BF16), wo_ref[...])
    m = mod_ref[...]
    x1 = _layer_norm(DN_ALPHA * x_ref[...] + m[2:3, :] * mix, lg_ref[...], lb_ref[...])
    x1_ref[...] = x1
    h2 = x1 * (1.0 + m[4:5, :]) + m[3:4, :]
    h2_ref[...] = h2
    hp_ref[...] = _pack_pairs(h2)


def _merge(outs, proj, wb, wo, x, mod, lg, lb, mod_row, bm):
    T = x.shape[0]
    row = lambda w: pl.BlockSpec((bm, w), lambda i: (i, 0))
    gspec = lambda n: pl.BlockSpec((bm, D), lambda i: (i, P_GATE // D + n))
    fixed2 = lambda s: pl.BlockSpec(s, lambda i: (0, 0))
    return pl.pallas_call(
        _merge_kernel,
        grid=(T // bm,),
        in_specs=[row(256)] * 4 + [gspec(0), gspec(1), gspec(2), gspec(3),
                                   pl.BlockSpec((4, 256, D), lambda i: (0, 0, 0)), fixed2((D, D)), row(D),
                                   pl.BlockSpec((None, 6, D), lambda i: (mod_row(i), 0, 0)),
                                   fixed2((1, D)), fixed2((1, D))],
        out_specs=[row(D), row(D), row(D // 2)],
        out_shape=[jax.ShapeDtypeStruct((T, D), F32)] * 2 + [jax.ShapeDtypeStruct((T, D // 2), jnp.int32)],
        compiler_params=_cp("arbitrary"),
        name="merge_norm",
    )(*outs, proj, proj, proj, proj, wb, wo, x, mod, lg, lb)


def _router_kernel(h_ref, rt_ref, bias_ref, g_ref, rank_ref, cnt_ref, *, tt):
    per = N_EXPERTS // N_GROUPS
    logits = lax.dot_general(rt_ref[...], h_ref[...], (((1,), (1,)), ((), ())), preferred_element_type=F32,
                             precision=lax.Precision.HIGHEST)
    scores = _sigmoid(logits)
    sel = (scores + bias_ref[...]).reshape(N_GROUPS, per, tt)
    gid = lax.broadcasted_iota(jnp.int32, (N_GROUPS, per, tt), 0).astype(F32)
    jid = lax.broadcasted_iota(jnp.int32, (N_GROUPS, per, tt), 1).astype(F32)
    eid = gid * per + jid
    ninf = -jnp.inf
    m1 = sel.max(1, keepdims=True)
    i1 = jnp.where(sel == m1, jid, float(per)).min(1, keepdims=True)
    m2 = jnp.where(jid == i1, ninf, sel).max(1, keepdims=True)
    gs = m1 + m2
    g1 = lax.broadcasted_iota(jnp.int32, (N_GROUPS, 1, tt), 0).astype(F32)
    chosen = jnp.zeros((N_GROUPS, 1, tt), F32)
    for _ in range(TOPK_GROUPS):
        mx = gs.max(0, keepdims=True)
        gi = jnp.where(gs == mx, g1, float(N_GROUPS)).min(0, keepdims=True)
        pick = g1 == gi
        chosen = jnp.where(pick, 1.0, chosen)
        gs = jnp.where(pick, ninf, gs)
    cand = jnp.where(chosen > 0.0, sel, NEG)
    picked = jnp.zeros((N_GROUPS, per, tt), F32)
    for _ in range(TOP_K):
        mx = cand.max(1, keepdims=True).max(0, keepdims=True)
        ei = jnp.where(cand == mx, eid, float(N_EXPERTS)).min(1, keepdims=True).min(0, keepdims=True)
        pick = eid == ei
        picked = jnp.where(pick, 1.0, picked)
        cand = jnp.where(pick, ninf, cand)
    w = scores.reshape(N_GROUPS, per, tt) * picked
    wsum = w.sum(1, keepdims=True).sum(0, keepdims=True)
    g_ref[...] = (w / wsum * ROUTED_SCALE).reshape(N_EXPERTS, tt)
    pk = picked.reshape(N_EXPERTS, tt)
    t_in = lax.broadcasted_iota(jnp.int32, (tt, tt), 0)
    t_out = lax.broadcasted_iota(jnp.int32, (tt, tt), 1)
    upper = jnp.where(t_in <= t_out, 1.0, 0.0).astype(BF16)

    @pl.when(pl.program_id(0) == 0)
    def _():
        cnt_ref[...] = jnp.zeros_like(cnt_ref)

    before = cnt_ref[:, 0:1]
    rank_ref[...] = jnp.where(pk > 0.0, before + _dot(pk.astype(BF16), upper) - 1.0, -1.0)
    cnt_ref[...] += pk.sum(-1, keepdims=True)


def _router(h2, router_t, bias, tt):
    T = h2.shape[0]
    tile = pl.BlockSpec((N_EXPERTS, tt), lambda i: (0, i))
    return pl.pallas_call(
        functools.partial(_router_kernel, tt=tt),
        grid=(T // tt,),
        in_specs=[pl.BlockSpec((tt, D), lambda i: (i, 0)), pl.BlockSpec((N_EXPERTS, D), lambda i: (0, 0)),
                  pl.BlockSpec((N_EXPERTS, 1), lambda i: (0, 0))],
        out_specs=[tile, tile, pl.BlockSpec((N_EXPERTS, 128), lambda i: (0, 0))],
        out_shape=[jax.ShapeDtypeStruct((N_EXPERTS, T), F32), jax.ShapeDtypeStruct((N_EXPERTS, T), F32),
                   jax.ShapeDtypeStruct((N_EXPERTS, 128), F32)],
        compiler_params=_cp("arbitrary"),
        name="moe_router",
    )(h2, router_t, bias)


def _route_pos_kernel(gate_ref, rank_ref, cnt_ref, pos_ref, w_ref, te_ref, nt_ref, *, tm, nt_max):
    ei = lax.broadcasted_iota(jnp.int32, (N_EXPERTS, N_EXPERTS), 0)
    ej = lax.broadcasted_iota(jnp.int32, (N_EXPERTS, N_EXPERTS), 1)
    below = jnp.where(ej < ei, 1.0, 0.0)
    padded = jnp.ceil(cnt_ref[...] * (1.0 / tm)) * tm
    offs = _dot_hi(below, padded)
    rank = rank_ref[...]
    routed = rank >= 0.0
    pos = offs[:, 0:1] + rank
    slot = _dot(below.astype(BF16), jnp.where(routed, 1.0, 0.0).astype(BF16))
    gate = gate_ref[...]
    for k in range(TOP_K):
        mine = routed & (slot == float(k))
        pos_ref[k:k + 1, :] = jnp.where(mine, pos, 0.0).sum(0, keepdims=True).astype(jnp.int32)
        w_ref[k:k + 1, :] = jnp.where(mine, gate, 0.0).sum(0, keepdims=True)
    ends = (offs + padded)[:, 0:1]
    first = (lax.broadcasted_iota(jnp.int32, (N_EXPERTS, nt_max), 1) * tm).astype(F32)
    te = jnp.where(ends <= first, 1.0, 0.0).sum(0, keepdims=True)
    te_ref[...] = jnp.minimum(te, N_EXPERTS - 1.0).astype(jnp.int32)
    nt_ref[...] = (padded.sum(0, keepdims=True) * (1.0 / tm)).astype(jnp.int32)


def _route_pos(gate_t, rank, cnt, tt, tm, nt_max):
    T = gate_t.shape[1]
    tile = pl.BlockSpec((N_EXPERTS, tt), lambda i: (0, i))
    out = pl.BlockSpec((TOP_K, tt), lambda i: (0, i))
    return pl.pallas_call(
        functools.partial(_route_pos_kernel, tm=tm, nt_max=nt_max),
        grid=(T // tt,),
        in_specs=[tile, tile, pl.BlockSpec((N_EXPERTS, 128), lambda i: (0, 0))],
        out_specs=[out, out, pl.BlockSpec((1, nt_max), lambda i: (0, 0)), pl.BlockSpec((1, 128), lambda i: (0, 0))],
        out_shape=[jax.ShapeDtypeStruct((TOP_K, T), jnp.int32), jax.ShapeDtypeStruct((TOP_K, T), F32),
                   jax.ShapeDtypeStruct((1, nt_max), jnp.int32), jax.ShapeDtypeStruct((1, 128), jnp.int32)],
        compiler_params=_cp("arbitrary"),
        name="moe_positions",
    )(gate_t, rank, cnt)


def _gmm_kernel(te_ref, nt_ref, xs_ref, w1_ref, w3_ref, w2_ref, ys_ref, b1_ref, b3_ref, b2_ref):
    j = pl.program_id(0)
    live = j < nt_ref[0]
    new_expert = (j == 0) | (te_ref[j] != te_ref[jnp.maximum(j - 1, 0)])

    @pl.when(live & new_expert)
    def _():
        b1_ref[...] = w1_ref[...].astype(BF16)
        b3_ref[...] = w3_ref[...].astype(BF16)
        b2_ref[...] = w2_ref[...].astype(BF16)

    @pl.when(live)
    def _():
        xa, xb = _unpack_pairs(xs_ref[...])
        xa, xb = xa.astype(BF16), xb.astype(BF16)
        half = D // 2
        a = _dot(xa, b1_ref[:half, :]) + _dot(xb, b1_ref[half:, :])
        b = _dot(xa, b3_ref[:half, :]) + _dot(xb, b3_ref[half:, :])
        hid = (a * _sigmoid(a) * b).astype(BF16)
        ys_ref[...] = _pack_pairs(_dot(hid, b2_ref[...]))


def _gmm(te, nt, xs, w1, w3, w2, l, tm):
    n_slots = xs.shape[0]
    ds = D_EXPERT
    live = lambda j, nt: jnp.minimum(j, nt[0] - 1)
    rows = pl.BlockSpec((tm, D // 2), lambda j, te, nt: (live(j, nt), 0))
    wspec = lambda s: pl.BlockSpec((None, None) + s, lambda j, te, nt: (l, te[live(j, nt)], 0, 0))
    return pl.pallas_call(
        _gmm_kernel,
        grid_spec=pltpu.PrefetchScalarGridSpec(
            num_scalar_prefetch=2,
            grid=(n_slots // tm,),
            in_specs=[rows, wspec((D, ds)), wspec((D, ds)), wspec((ds, D))],
            out_specs=rows,
            scratch_shapes=[pltpu.VMEM((D, ds), BF16), pltpu.VMEM((D, ds), BF16), pltpu.VMEM((ds, D), BF16)]),
        out_shape=jax.ShapeDtypeStruct((n_slots, D // 2), jnp.int32),
        compiler_params=_cp("arbitrary"),
        name="moe_grouped_ffn",
    )(te, nt, xs, w1, w3, w2)


def _combine_kernel(yk_ref, w_ref, hp_ref, s1_ref, s3_ref, s2_ref, x_ref, mod_ref, lg_ref, lb_ref, o_ref):
    w = w_ref[...]
    acc_a = acc_b = None
    for k in range(TOP_K):
        ya, yb = _unpack_pairs(yk_ref[k])
        wk = w[:, k:k + 1]
        acc_a = wk * ya if acc_a is None else acc_a + wk * ya
        acc_b = wk * yb if acc_b is None else acc_b + wk * yb
    ha, hb = _unpack_pairs(hp_ref[...])
    ha, hb = ha.astype(BF16), hb.astype(BF16)
    half = D // 2
    a = _dot(ha, s1_ref[:half, :]) + _dot(hb, s1_ref[half:, :])
    b = _dot(ha, s3_ref[:half, :]) + _dot(hb, s3_ref[half:, :])
    y = jnp.concatenate([acc_a, acc_b], axis=1) + _dot((a * _sigmoid(a) * b).astype(BF16), s2_ref[...])
    m = mod_ref[...]
    o_ref[...] = _layer_norm(DN_ALPHA * x_ref[...] + m[5:6, :] * y, lg_ref[...], lb_ref[...])


def _combine(yk, w, hp, s1, s3, s2, x1, mod, lg, lb, mod_row, bm):
    T = x1.shape[0]
    ds = D_EXPERT
    row = lambda n: pl.BlockSpec((bm, n), lambda i: (i, 0))
    fixed = lambda s: pl.BlockSpec(s, lambda i: (0, 0))
    return pl.pallas_call(
        _combine_kernel,
        grid=(T // bm,),
        in_specs=[pl.BlockSpec((TOP_K, bm, D // 2), lambda i: (0, i, 0)), row(TOP_K), row(D // 2),
                  fixed((D, ds)), fixed((D, ds)), fixed((ds, D)), row(D),
                  pl.BlockSpec((None, 6, D), lambda i: (mod_row(i), 0, 0)), fixed((1, D)), fixed((1, D))],
        out_specs=row(D),
        out_shape=jax.ShapeDtypeStruct((T, D), F32),
        compiler_params=_cp("arbitrary"),
        name="moe_combine_norm",
    )(yk, w, hp, s1, s3, s2, x1, mod, lg, lb)


def _sc_worker():
    return lax.axis_index("s") * SC_CORES + lax.axis_index("c")


def _sc_mesh():
    return plsc.VectorSubcoreMesh(core_axis_name="c", subcore_axis_name="s")


def _sc_gather(table, idx):
    N, W = idx.shape[0], table.shape[1]
    per_w = N // SC_WORKERS
    n_chunks = per_w // SC_ROWS

    def body(table_hbm, idx_hbm, out_hbm, idx_v, rows_v, sem):
        base = _sc_worker() * per_w
        pltpu.sync_copy(idx_hbm.at[pl.ds(base, per_w)], idx_v)

        @pl.loop(0, n_chunks)
        def _(c):
            off = pl.multiple_of(c * SC_ROWS, SC_ROWS)
            pltpu.async_copy(table_hbm.at[idx_v.at[pl.ds(off, SC_ROWS)]], rows_v, sem).wait()
            pltpu.sync_copy(rows_v, out_hbm.at[pl.ds(base + off, SC_ROWS)])

    return pl.kernel(
        body, out_type=jax.ShapeDtypeStruct((N, W), table.dtype), mesh=_sc_mesh(),
        scratch_types=[pltpu.VMEM((per_w,), jnp.int32), pltpu.VMEM((SC_ROWS, W), table.dtype),
                       pltpu.SemaphoreType.DMA],
        name="sc_gather",
    )(table, idx)


def _sc_dispatch(pos, table, n_slots):
    NP, (T, W) = pos.shape[0], table.shape
    per_w = n_slots // SC_WORKERS
    n_chunks = per_w // SC_ROWS
    scan = 8192

    def body(pos_hbm, table_hbm, out_hbm, pos_v, src_v, rows_v, sem):
        base = _sc_worker() * per_w
        lane = lax.iota(jnp.int32, SC_LANES)

        @pl.loop(0, per_w // SC_LANES)
        def _(j):
            o = pl.multiple_of(j * SC_LANES, SC_LANES)
            src_v[pl.ds(o, SC_LANES)] = (base + o + lane) & (T - 1)

        @pl.loop(0, NP // scan)
        def _(c):
            pltpu.sync_copy(pos_hbm.at[pl.ds(pl.multiple_of(c * scan, scan), scan)], pos_v)

            @pl.loop(0, scan // SC_LANES)
            def _(v):
                o = pl.multiple_of(v * SC_LANES, SC_LANES)
                p = pos_v[pl.ds(o, SC_LANES)] - base
                mine = (p >= 0) & (p < per_w)
                tok = (c * scan + o + lane) & (T - 1)
                plsc.store_scatter(src_v, [jnp.where(mine, p, 0)], tok, mask=mine)

        @pl.loop(0, n_chunks)
        def _(c):
            off = pl.multiple_of(c * SC_ROWS, SC_ROWS)
            pltpu.async_copy(table_hbm.at[src_v.at[pl.ds(off, SC_ROWS)]], rows_v, sem).wait()
            pltpu.sync_copy(rows_v, out_hbm.at[pl.ds(base + off, SC_ROWS)])

    return pl.kernel(
        body, out_type=jax.ShapeDtypeStruct((n_slots, W), table.dtype), mesh=_sc_mesh(),
        scratch_types=[pltpu.VMEM((scan,), jnp.int32), pltpu.VMEM((per_w,), jnp.int32),
                       pltpu.VMEM((SC_ROWS, W), table.dtype), pltpu.SemaphoreType.DMA],
        compiler_params=pltpu.CompilerParams(needs_layout_passes=False),
        name="sc_dispatch",
    )(pos, table)


def _rot_cols(w, q):
    a, b, c, d = w[..., :q], w[..., q:2 * q], w[..., 2 * q:3 * q], w[..., 3 * q:]
    return jnp.concatenate([-b, a, -d, c], -1)


def _prep_w_in(w):
    z = lambda n: jnp.zeros((D, n), w.dtype)
    qlat, ckv, kpe, hy = w[:, 0:256], w[:, 256:384], w[:, 384:416], w[:, 416:1184]
    wq, wk, wv = w[:, 1184:1440], w[:, 1440:1568], w[:, 1568:1696]
    nq, nk, nv, gate = w[:, 1696:1952], w[:, 1952:2208], w[:, 2208:2464], w[:, 2464:]
    wq_r = _rot_cols(wq.reshape(D, 4, 64), 16).reshape(D, 256)
    wk_r = _rot_cols(wk.reshape(D, 2, 64), 16).reshape(D, 128)
    kpe_r = _rot_cols(kpe, 8)
    cols = [qlat, ckv, z(64), kpe, z(32), hy, wq, wk, wv, nq, nk, nv, wq_r, wk_r, z(64), kpe_r, z(32), gate]
    return jnp.concatenate(cols, 1).astype(BF16)


def _prep_mla(w_uq, w_ukv):
    uq = w_uq.reshape(256, 4, 96)
    nope, pe = uq[..., :64], uq[..., 64:]
    z32 = jnp.zeros((256, 4, 32), w_uq.dtype)
    z64 = jnp.zeros((256, 4, 64), w_uq.dtype)
    wcat = jnp.concatenate([nope, pe, z32], -1).reshape(256, 512).astype(BF16)
    wrot = jnp.concatenate([z64, _rot_cols(pe, 8), z32], -1).reshape(256, 512).astype(BF16)
    ukv = w_ukv.reshape(128, 4, 128)
    wk = jnp.concatenate([ukv[..., :64], jnp.zeros((128, 4, 64), w_ukv.dtype)], -1).reshape(128, 512).astype(BF16)
    wv = ukv[..., 64:].reshape(128, 256).astype(BF16)
    return wcat, wrot, wk, wv


def _rope_tab(L, q):
    t = jnp.arange(L)
    inv = ROPE_BASE ** (-jnp.arange(q, dtype=F32) / q)
    ar = (t // GRID_W).astype(F32)[:, None] * inv[None, :]
    ac = (t % GRID_W).astype(F32)[:, None] * inv[None, :]
    cos = jnp.concatenate([jnp.cos(ar), jnp.cos(ar), jnp.cos(ac), jnp.cos(ac)], 1)
    sin = jnp.concatenate([jnp.sin(ar), jnp.sin(ar), jnp.sin(ac), jnp.sin(ac)], 1)
    return cos, sin


def _rope_tables(L):
    c8, s8 = _rope_tab(L, 8)
    c16, s16 = _rope_tab(L, 16)
    one, zero = jnp.ones((L, 64), F32), jnp.zeros((L, 64), F32)
    z32 = jnp.zeros((L, 32), F32)
    mla_q = (jnp.tile(jnp.concatenate([one, c8, z32], 1), (1, 4)), jnp.tile(jnp.concatenate([zero, s8, z32], 1), (1, 4)))
    mla_k = (jnp.concatenate([zero, c8, z32], 1), jnp.concatenate([zero, s8, z32], 1))
    win = (jnp.tile(c16, (1, 4)), jnp.tile(s16, (1, 4)), jnp.tile(c16, (1, 2)), jnp.tile(s16, (1, 2)))
    return mla_q + mla_k, win


def _hyena(proj, lp, dft, NB, Lb):
    cm, sm, smt = dft
    tm = min(Lb, 512)
    v, x1, x2 = _short_conv(proj, lp["hy_conv_w"], lp["hy_conv_b"].reshape(1, -1), NB, Lb)
    w1p = jnp.pad(lp["hy_w1"], ((0, 128 - lp["hy_w1"].shape[0]), (0, 0)))
    fs, nyq = _hy_filter(Lb, w1p, lp["hy_b1"].reshape(1, -1), lp["hy_w2"], lp["hy_b2"].reshape(1, -1), lp["hy_w3"],
                         lp["hy_sin_freq"], lp["hy_log_decay"].reshape(1, -1))
    gr, gi = _hy_gdft(cm, sm, fs, nyq, Lb, tm)
    skip = lp["hy_skip"].reshape(2, 1, HY_C)
    z = v
    for n, gate in enumerate((x1, x2)):
        yr, yi = _hy_fwd(cm, sm, z, gr, gi, n, NB, Lb, tm)
        z = _hy_inv(cm, smt, yr, yi, z, gate, skip, n, NB, Lb, tm)
    return z


def _layer(x, mod, lp, l, NB, Lb, mod_row_of_batch, dft, cache=None, tabs=None, na_bias=None):
    T = NB * Lb
    latent = cache is not None
    bm = 256
    rows_of = lambda n: (lambda i: mod_row_of_batch((i * n) // Lb))
    mod_row = rows_of(bm)
    span = Lb if latent else T
    bmp = min(span, 1024)
    proj = _in_proj(x, mod, lp["w_in_p"], rows_of(bmp), bmp)

    gq, gkv = lp["mla_q_norm"].reshape(1, -1), lp["mla_kv_norm"].reshape(1, -1)
    wcat, wrot, wk, wv = lp["mla_w"]
    q_all, ckv_n, kpe_r = _mla_q(proj, gq, gkv, wcat, wrot, tabs[0] if latent else None, Lb, bm)
    if latent:
        ckv_c, kpe_c, kc_c, vc_c, kd_c, vd_c = cache
        Lc = ckv_c.shape[1]
        kpe_cp = jnp.pad(kpe_c, ((0, 0), (0, 0), (64, 32)))
        ckv_all = jnp.concatenate([ckv_c, ckv_n.reshape(NB, Lb, 128)], 1).reshape(NB * (Lc + Lb), 128)
        kpe_all = jnp.concatenate([kpe_cp, kpe_r.reshape(NB, Lb, 128)], 1).reshape(NB * (Lc + Lb), 128)
        k_all, v_all = _mla_kv(ckv_all, kpe_all, wk, wv, 512)
        oa = _lat_mla_attention(q_all, k_all, v_all, NB, Lb, Lc + Lb, 256)
        oc = _lat_win_attention(proj, kc_c.reshape(NB, Lc, 128), vc_c.reshape(NB, Lc, 128), tabs[1],
                                lp["win_sink"], NB, Lb)
        od = _lat_na_attention(proj, kd_c.reshape(NB, Lc, 256), vd_c.reshape(NB, Lc, 256), na_bias, NB, Lb)
    else:
        k_all, v_all = _mla_kv(ckv_n, kpe_r, wk, wv, 512)
        oa, oc, od = _ctx_attention(proj, q_all, k_all, v_all, lp["win_sink"], NB, Lb)
    ob = _hyena(proj, lp, dft, NB, Lb)

    bmm = min(span, 512)
    x1, h2, hp = _merge((oa, ob, oc, od), proj, lp["w_branch_b"], lp["w_out_b"], x, mod,
                        lp["ln1_g"].reshape(1, -1), lp["ln1_b"].reshape(1, -1), rows_of(bmm), bmm)
    n_slots = T * TOP_K + N_EXPERTS * MOE_TM
    gate_t, rank, cnt = _router(h2, lp["moe_router"].T, lp["moe_bias"].reshape(-1, 1), 512)
    pos, w8, te, nt = _route_pos(gate_t, rank, cnt, 512, MOE_TM, n_slots // MOE_TM)
    xs = _sc_dispatch(pos.reshape(-1), hp, n_slots)
    ys = _gmm(te.reshape(-1), nt.reshape(-1)[:1], xs, lp["moe_w1"], lp["moe_w3"], lp["moe_w2"], l, MOE_TM)
    yk = _sc_gather(ys, pos.reshape(-1)).reshape(TOP_K, T, D // 2)
    x2 = _combine(yk, w8.T, hp, lp["sh_w1_b"], lp["sh_w3_b"], lp["sh_w2_b"], x1, mod,
                  lp["ln2_g"].reshape(1, -1), lp["ln2_b"].reshape(1, -1), mod_row, bm)
    return x2, (proj, ckv_n)


def kernel(x_prompt, x_sample, cache_mla_ckv, cache_mla_kpe, cache_win_k, cache_win_v, cache_na_k, cache_na_v, c, c_ctx, w_ada, b_ada, w_in, mla_q_norm, mla_kv_norm, mla_w_uq, mla_w_ukv, hy_conv_w, hy_conv_b, hy_w1, hy_b1, hy_w2, hy_b2, hy_w3, hy_sin_freq, hy_log_decay, hy_skip, win_sink, na_rpb, w_branch, w_out, ln1_g, ln1_b, ln2_g, ln2_b, moe_router, moe_bias, moe_w1, moe_w3, moe_w2, sh_w1, sh_w3, sh_w2):
    B, S, _ = x_prompt.shape
    DB, DS, _ = x_sample.shape
    xp = x_prompt.reshape(B * S, D)
    xs = x_sample.reshape(DB * DS, D)
    cvec = jnp.concatenate([c_ctx[None, :], c, jnp.zeros((8 - 1 - DB, D), F32)], 0)
    dft_ctx = _dft_mats(S)
    dft_lat = _dft_mats(DS)
    tabs = _rope_tables(DS)
    new = [[] for _ in range(6)]
    for l in range(DEPTH):
        lp = dict(w_in_p=_prep_w_in(w_in[l]), mla_q_norm=mla_q_norm[l], mla_kv_norm=mla_kv_norm[l],
                  mla_w=_prep_mla(mla_w_uq[l], mla_w_ukv[l]), hy_conv_w=hy_conv_w[l], hy_conv_b=hy_conv_b[l],
                  hy_w1=hy_w1[l], hy_b1=hy_b1[l], hy_w2=hy_w2[l], hy_b2=hy_b2[l], hy_w3=hy_w3[l],
                  hy_sin_freq=hy_sin_freq[l], hy_log_decay=hy_log_decay[l], hy_skip=hy_skip[l],
                  win_sink=win_sink[l], w_branch_b=w_branch[l].astype(BF16), w_out_b=w_out[l].astype(BF16),
                  ln1_g=ln1_g[l], ln1_b=ln1_b[l], ln2_g=ln2_g[l], ln2_b=ln2_b[l],
                  moe_router=moe_router[l], moe_bias=moe_bias[l], moe_w1=moe_w1, moe_w3=moe_w3, moe_w2=moe_w2,
                  sh_w1_b=sh_w1[l].astype(BF16), sh_w3_b=sh_w3[l].astype(BF16), sh_w2_b=sh_w2[l].astype(BF16))
        mod = _modulation(cvec, w_ada, b_ada, l)
        xp, (proj, ckv_n) = _layer(xp, mod, lp, l, B, S, lambda b: 0, dft_ctx)
        new[0].append(ckv_n.reshape(B, S, 128))
        new[1].append(proj[:, P_KPE + 64:P_KPE + 96].reshape(B, S, 32))
        new[2].append(proj[:, P_WK:P_WK + 128].reshape(B, S, 2, 64))
        new[3].append(proj[:, P_WV:P_WV + 128].reshape(B, S, 2, 64))
        new[4].append(proj[:, P_NK:P_NK + 256].reshape(B, S, 4, 64))
        new[5].append(proj[:, P_NV:P_NV + 256].reshape(B, S, 4, 64))
        cache = (cache_mla_ckv[:, l], cache_mla_kpe[:, l], cache_win_k[:, l], cache_win_v[:, l],
                 cache_na_k[:, l], cache_na_v[:, l])
        xs, _ = _layer(xs, mod, lp, l, DB, DS, lambda b: 1 + b, dft_lat, cache=cache, tabs=tabs,
                       na_bias=_na_bias(na_rpb[l]))
    return (xp.reshape(B, S, D), xs.reshape(DB, DS, D)) + tuple(jnp.stack(t, 1) for t in new)
```

```python
import functools
import math

import jax
import jax.numpy as jnp
from jax import lax
from jax.experimental import pallas as pl
from jax.experimental.pallas import tpu as pltpu
from jax.experimental.pallas import tpu_sc as plsc

F32 = jnp.float32
BF16 = jnp.bfloat16

D = 1024
DEPTH = 2
GRID_W = 64
HEAD_DIM = 64
MLA_SCALE = 96 ** -0.5
ATT_SCALE = HEAD_DIM ** -0.5
HY_C = 256
HY_BANDS = 8
NA_KH = 8
NA_KW = 16
N_EXPERTS = 64
N_GROUPS = 8
TOP_K = 8
TOPK_GROUPS = 4
D_EXPERT = 256
ROUTED_SCALE = 2.5
ROPE_BASE = 10000.0
LN_EPS = 1e-5
RMS_EPS = 1e-6
NEG = -1e30
DN_ALPHA = (2 * DEPTH) ** 0.25

P_QLAT, P_CKV, P_KPE, P_HY = 0, 256, 384, 512
P_WQ, P_WK, P_WV = 1280, 1536, 1664
P_NQ, P_NK, P_NV = 1792, 2048, 2304
P_WQR, P_WKR, P_KPER, P_GATE = 2560, 2816, 2944, 3072
N_PROJ = 7168

VMEM_LIMIT = 56 * 1024 * 1024

SC_CORES = 2
SC_SUBCORES = 16
SC_LANES = 16
SC_WORKERS = SC_CORES * SC_SUBCORES
SC_ROWS = 64

MOE_TM = 512

def _cp(*sem):
    return pltpu.CompilerParams(dimension_semantics=sem, vmem_limit_bytes=VMEM_LIMIT)


def _sigmoid(x):
    return 1.0 / (1.0 + jnp.exp(-x))


def _dot(a, b):
    return jnp.dot(a, b, preferred_element_type=F32)


def _dot_nt(a, b):
    return lax.dot_general(a, b, (((1,), (1,)), ((), ())), preferred_element_type=F32)


def _dot_hi(a, b):
    return jnp.dot(a, b, preferred_element_type=F32, precision=lax.Precision.HIGHEST)


def _pack_pairs(x):
    w = x.shape[1] // 2
    hi = lax.bitcast_convert_type(x[:, :w].astype(BF16).astype(F32), jnp.int32)
    lo = lax.bitcast_convert_type(x[:, w:].astype(BF16).astype(F32), jnp.int32)
    return hi | lax.shift_right_logical(lo, 16)


def _unpack_pairs(p):
    hi = lax.bitcast_convert_type(p & jnp.int32(-65536), F32)
    lo = lax.bitcast_convert_type(lax.shift_left(p, 16), F32)
    return hi, lo


def _layer_norm(x, g, b):
    mu = jnp.mean(x, -1, keepdims=True)
    xc = x - mu
    var = jnp.mean(xc * xc, -1, keepdims=True)
    return xc * lax.rsqrt(var + LN_EPS) * g + b


def _rms_norm(x, g):
    return x * lax.rsqrt(jnp.mean(x * x, -1, keepdims=True) + RMS_EPS) * g


def _mod_kernel(c_ref, w_ref, b_ref, o_ref):
    c = c_ref[...]
    a = (c * _sigmoid(c)).astype(BF16)
    o_ref[...] = _dot(a, w_ref[...].astype(BF16)) + b_ref[...]


def _modulation(cvec, w_ada, b_ada, l):
    out = pl.pallas_call(
        _mod_kernel,
        grid=(6,),
        in_specs=[pl.BlockSpec((8, D), lambda j: (0, 0)),
                  pl.BlockSpec((None, D, D), lambda j: (l, 0, j)),
                  pl.BlockSpec((None, 1, D), lambda j: (l, 0, j))],
        out_specs=pl.BlockSpec((8, D), lambda j: (0, j)),
        out_shape=jax.ShapeDtypeStruct((8, 6 * D), F32),
        compiler_params=_cp("arbitrary"),
        name="modulation",
    )(cvec, w_ada, b_ada.reshape(DEPTH, 1, 6 * D))
    return out.reshape(8, 6, D)


def _inproj_kernel(x_ref, mod_ref, w_ref, o_ref):
    m = mod_ref[...]
    h = x_ref[...] * (1.0 + m[1:2, :]) + m[0:1, :]
    o_ref[...] = _dot(h.astype(BF16), w_ref[...])


def _in_proj(x, mod, w_p, mod_row, bm):
    T = x.shape[0]
    bn = 1024
    return pl.pallas_call(
        _inproj_kernel,
        grid=(N_PROJ // bn, T // bm),
        in_specs=[pl.BlockSpec((bm, D), lambda j, i: (i, 0)),
                  pl.BlockSpec((None, 6, D), lambda j, i: (mod_row(i), 0, 0)),
                  pl.BlockSpec((D, bn), lambda j, i: (0, j))],
        out_specs=pl.BlockSpec((bm, bn), lambda j, i: (i, j)),
        out_shape=jax.ShapeDtypeStruct((T, N_PROJ), F32),
        compiler_params=_cp("arbitrary", "arbitrary"),
        name="in_proj",
    )(x, mod, w_p)


def _mla_q_kernel(*refs, rope):
    if rope:
        (ql_ref, ckv_ref, kpe_ref, kper_ref, gq_ref, gkv_ref, wc_ref, wr_ref,
         cq_ref, sq_ref, ck_ref, sk_ref, q_ref, ckvn_ref, kpeo_ref) = refs
    else:
        ql_ref, ckv_ref, kpe_ref, gq_ref, gkv_ref, wc_ref, q_ref, ckvn_ref, kpeo_ref = refs
    qn = _rms_norm(ql_ref[...], gq_ref[...]).astype(BF16)
    q = _dot(qn, wc_ref[...])
    if rope:
        q = q * cq_ref[...] + _dot(qn, wr_ref[...]) * sq_ref[...]
        kpeo_ref[...] = kpe_ref[...] * ck_ref[...] + kper_ref[...] * sk_ref[...]
    else:
        kpeo_ref[...] = kpe_ref[...]
    q_ref[...] = (q * MLA_SCALE).astype(BF16)
    ckvn_ref[...] = _rms_norm(ckv_ref[...], gkv_ref[...])


def _mla_q(proj, gq, gkv, wcat, wrot, tabs, Lb, bm):
    T = proj.shape[0]
    rope = tabs is not None
    nl = Lb // bm
    col = lambda c: (lambda i: (i, c))
    fixed = lambda i: (0, 0)
    in_specs = [pl.BlockSpec((bm, 256), col(P_QLAT // 256)),
                pl.BlockSpec((bm, 128), col(P_CKV // 128)),
                pl.BlockSpec((bm, 128), col(P_KPE // 128))]
    args = [proj, proj, proj]
    if rope:
        in_specs.append(pl.BlockSpec((bm, 128), col(P_KPER // 128)))
        args.append(proj)
    in_specs += [pl.BlockSpec((1, 256), fixed), pl.BlockSpec((1, 128), fixed), pl.BlockSpec((256, 512), fixed)]
    args += [gq, gkv, wcat]
    if rope:
        cq, sq, ck, sk = tabs
        pos = lambda i: (i % nl, 0)
        in_specs += [pl.BlockSpec((256, 512), fixed), pl.BlockSpec((bm, 512), pos), pl.BlockSpec((bm, 512), pos),
                     pl.BlockSpec((bm, 128), pos), pl.BlockSpec((bm, 128), pos)]
        args += [wrot, cq, sq, ck, sk]
    return pl.pallas_call(
        functools.partial(_mla_q_kernel, rope=rope),
        grid=(T // bm,),
        in_specs=in_specs,
        out_specs=[pl.BlockSpec((bm, 512), lambda i: (i, 0)),
                   pl.BlockSpec((bm, 128), lambda i: (i, 0)),
                   pl.BlockSpec((bm, 128), lambda i: (i, 0))],
        out_shape=[jax.ShapeDtypeStruct((T, 512), BF16),
                   jax.ShapeDtypeStruct((T, 128), F32),
                   jax.ShapeDtypeStruct((T, 128), F32)],
        compiler_params=_cp("arbitrary"),
        name="mla_q",
    )(*args)


def _mla_kv_kernel(ckv_ref, kpe_ref, wk_ref, wv_ref, k_ref, v_ref):
    c = ckv_ref[...].astype(BF16)
    kpe = kpe_ref[...]
    k_ref[...] = (_dot(c, wk_ref[...]) + jnp.concatenate([kpe] * 4, axis=1)).astype(BF16)
    v_ref[...] = _dot(c, wv_ref[...]).astype(BF16)


def _mla_kv(ckv, kpe, wk, wv, bm):
    Tk = ckv.shape[0]
    return pl.pallas_call(
        _mla_kv_kernel,
        grid=(Tk // bm,),
        in_specs=[pl.BlockSpec((bm, 128), lambda i: (i, 0)),
                  pl.BlockSpec((bm, 128), lambda i: (i, 0)),
                  pl.BlockSpec((128, 512), lambda i: (0, 0)),
                  pl.BlockSpec((128, 256), lambda i: (0, 0))],
        out_specs=[pl.BlockSpec((bm, 512), lambda i: (i, 0)),
                   pl.BlockSpec((bm, 256), lambda i: (i, 0))],
        out_shape=[jax.ShapeDtypeStruct((Tk, 512), BF16),
                   jax.ShapeDtypeStruct((Tk, 256), BF16)],
        compiler_params=_cp("arbitrary"),
        name="mla_kv",
    )(ckv, kpe, wk, wv)


def _attn_core(q, kvs, masks, sink):
    ss = []
    for (k, _), mk in zip(kvs, masks):
        s = _dot_nt(q, k)
        if mk is not None:
            s = s + mk[1] if mk[0] == "add" else jnp.where(mk[1], s, NEG)
        ss.append(s)
    m = ss[0].max(-1, keepdims=True)
    for s in ss[1:]:
        m = jnp.maximum(m, s.max(-1, keepdims=True))
    if sink is not None:
        m = jnp.maximum(m, sink)
    den = None
    acc = None
    for s, (_, v) in zip(ss, kvs):
        p = jnp.exp(s - m)
        d = p.sum(-1, keepdims=True)
        a = _dot(p.astype(BF16), v)
        den = d if den is None else den + d
        acc = a if acc is None else acc + a
    if sink is not None:
        den = den + jnp.exp(sink - m)
    return acc / den


def _ctx_attn_kernel(qm_ref, km_ref, vm_ref, wq_ref, wk_ref, wv_ref, nq_ref, nk_ref, nv_ref, sink_ref,
                     om_ref, ow_ref, on_ref):
    for h in range(4):
        q = qm_ref[:, 128 * h:128 * (h + 1)]
        k = km_ref[:, 128 * h:128 * (h + 1)]
        v = vm_ref[:, 64 * h:64 * (h + 1)]
        om_ref[:, 64 * h:64 * (h + 1)] = _attn_core(q, [(k, v)], [None], None)
    for h in range(4):
        g = h // 2
        q = (wq_ref[:, 64 * h:64 * (h + 1)] * ATT_SCALE).astype(BF16)
        k = wk_ref[:, 64 * g:64 * (g + 1)].astype(BF16)
        v = wv_ref[:, 64 * g:64 * (g + 1)].astype(BF16)
        ow_ref[:, 64 * h:64 * (h + 1)] = _attn_core(q, [(k, v)], [None], sink_ref[h])
    for h in range(4):
        q = (nq_ref[:, 64 * h:64 * (h + 1)] * ATT_SCALE).astype(BF16)
        k = nk_ref[:, 64 * h:64 * (h + 1)].astype(BF16)
        v = nv_ref[:, 64 * h:64 * (h + 1)].astype(BF16)
        on_ref[:, 64 * h:64 * (h + 1)] = _attn_core(q, [(k, v)], [None], None)


def _ctx_attention(proj, q_all, k_all, v_all, sink, NB, Lb):
    T = proj.shape[0]
    pc = lambda w, off: pl.BlockSpec((Lb, w), lambda b: (b, off // w))
    row = lambda w: pl.BlockSpec((Lb, w), lambda b: (b, 0))
    return pl.pallas_call(
        _ctx_attn_kernel,
        grid=(NB,),
        in_specs=[row(512), row(512), row(256),
                  pc(256, P_WQ), pc(128, P_WK), pc(128, P_WV),
                  pc(256, P_NQ), pc(256, P_NK), pc(256, P_NV),
                  pl.BlockSpec(memory_space=pltpu.SMEM)],
        out_specs=[row(256), row(256), row(256)],
        out_shape=[jax.ShapeDtypeStruct((T, 256), F32)] * 3,
        compiler_params=_cp("arbitrary"),
        name="ctx_attention",
    )(q_all, k_all, v_all, proj, proj, proj, proj, proj, proj, sink)


def _lat_mla_kernel(q_ref, k_ref, v_ref, o_ref):
    for h in range(4):
        q = q_ref[:, 128 * h:128 * (h + 1)]
        k = k_ref[:, 128 * h:128 * (h + 1)]
        v = v_ref[:, 64 * h:64 * (h + 1)]
        o_ref[:, 64 * h:64 * (h + 1)] = _attn_core(q, [(k, v)], [None], None)


def _lat_mla_attention(q_all, k_all, v_all, NB, Lb, Lk, tq):
    T = q_all.shape[0]
    nq = Lb // tq
    return pl.pallas_call(
        _lat_mla_kernel,
        grid=(NB, nq),
        in_specs=[pl.BlockSpec((tq, 512), lambda b, i: (b * nq + i, 0)),
                  pl.BlockSpec((Lk, 512), lambda b, i: (b, 0)),
                  pl.BlockSpec((Lk, 256), lambda b, i: (b, 0))],
        out_specs=pl.BlockSpec((tq, 256), lambda b, i: (b * nq + i, 0)),
        out_shape=jax.ShapeDtypeStruct((T, 256), F32),
        compiler_params=_cp("arbitrary", "arbitrary"),
        name="lat_mla_attention",
    )(q_all, k_all, v_all)


def _lat_win_kernel(q_ref, qr_ref, k_ref, kr_ref, v_ref, kc_ref, vc_ref, cq_ref, sq_ref, ck_ref, sk_ref,
                    sink_ref, o_ref, *, Lb):
    i = pl.program_id(1)
    start = pl.multiple_of(jnp.clip((i - 1) * 128, 0, Lb - 384), 128)
    win = pl.ds(start, 384)
    q = (q_ref[...] * cq_ref[...] + qr_ref[...] * sq_ref[...]) * ATT_SCALE
    kk = k_ref[win, :] * ck_ref[win, :] + kr_ref[win, :] * sk_ref[win, :]
    vv = v_ref[win, :]
    qpos = i * 128 + lax.broadcasted_iota(jnp.int32, (128, 384), 0)
    kpos = start + lax.broadcasted_iota(jnp.int32, (128, 384), 1)
    valid = jnp.abs(qpos - kpos) <= 128
    for h in range(4):
        g = h // 2
        sl = slice(64 * g, 64 * (g + 1))
        kvs = [(kk[:, sl].astype(BF16), vv[:, sl].astype(BF16)),
               (kc_ref[:, sl].astype(BF16), vc_ref[:, sl].astype(BF16))]
        qh = q[:, 64 * h:64 * (h + 1)].astype(BF16)
        o_ref[:, 64 * h:64 * (h + 1)] = _attn_core(qh, kvs, [("keep", valid), None], sink_ref[h])


def _lat_win_attention(proj, kc, vc, tabs, sink, NB, Lb):
    T = proj.shape[0]
    nb = Lb // 128
    Lc = kc.shape[1]
    cq, sq, ck, sk = tabs
    qspec = lambda off: pl.BlockSpec((128, 256), lambda b, i: (b * nb + i, off // 256))
    kspec = lambda off: pl.BlockSpec((Lb, 128), lambda b, i: (b, off // 128))
    cspec = pl.BlockSpec((None, Lc, 128), lambda b, i: (b, 0, 0))
    return pl.pallas_call(
        functools.partial(_lat_win_kernel, Lb=Lb),
        grid=(NB, nb),
        in_specs=[qspec(P_WQ), qspec(P_WQR), kspec(P_WK), kspec(P_WKR), kspec(P_WV), cspec, cspec,
                  pl.BlockSpec((128, 256), lambda b, i: (i, 0)), pl.BlockSpec((128, 256), lambda b, i: (i, 0)),
                  pl.BlockSpec((Lb, 128), lambda b, i: (0, 0)), pl.BlockSpec((Lb, 128), lambda b, i: (0, 0)),
                  pl.BlockSpec(memory_space=pltpu.SMEM)],
        out_specs=pl.BlockSpec((128, 256), lambda b, i: (b * nb + i, 0)),
        out_shape=jax.ShapeDtypeStruct((T, 256), F32),
        compiler_params=_cp("arbitrary", "arbitrary"),
        name="lat_win_attention",
    )(proj, proj, proj, proj, proj, kc, vc, cq, sq, ck, sk, sink)


def _na_bias_kernel(rpb_ref, o_ref):
    h = pl.program_id(0)
    qc = lax.broadcasted_iota(jnp.int32, (GRID_W, GRID_W), 0)
    kc = lax.broadcasted_iota(jnp.int32, (GRID_W, GRID_W), 1)
    dc = kc - qc + (NA_KW - 1)
    wstart = jnp.clip(qc - NA_KW // 2, 0, GRID_W - NA_KW)
    ok = (kc >= wstart) & (kc < wstart + NA_KW)
    n_dc = 2 * NA_KW - 1
    n_dr = 2 * NA_KH - 1
    tabs = []
    for dr in range(n_dr):
        t = jnp.zeros((GRID_W, GRID_W), F32)
        for j in range(n_dc):
            t = jnp.where(dc == j, rpb_ref[(h * n_dr + dr) * n_dc + j], t)
        tabs.append(jnp.where(ok, t, NEG))
    for o in range(NA_KH):
        for a in range(NA_KH):
            o_ref[o, :, GRID_W * a:GRID_W * (a + 1)] = tabs[a + NA_KH - 1 - o]


def _na_bias(rpb):
    H = rpb.shape[0]
    return pl.pallas_call(
        _na_bias_kernel,
        grid=(H,),
        in_specs=[pl.BlockSpec(memory_space=pltpu.SMEM)],
        out_specs=pl.BlockSpec((None, NA_KH, GRID_W, NA_KH * GRID_W), lambda h: (h, 0, 0, 0)),
        out_shape=jax.ShapeDtypeStruct((H, NA_KH, GRID_W, NA_KH * GRID_W), F32),
        compiler_params=_cp("arbitrary"),
        name="na_bias",
    )(rpb.reshape(-1))


def _lat_na_kernel(q_ref, k_ref, v_ref, kc_ref, vc_ref, bias_ref, o_ref, *, rows):
    r = pl.program_id(1)
    start = pl.multiple_of(jnp.clip(r - NA_KH // 2, 0, rows - NA_KH) * GRID_W, GRID_W)
    win = pl.ds(start, NA_KH * GRID_W)
    kk = k_ref[win, :]
    vv = v_ref[win, :]
    q = q_ref[...] * ATT_SCALE
    for h in range(4):
        sl = slice(64 * h, 64 * (h + 1))
        kvs = [(kk[:, sl].astype(BF16), vv[:, sl].astype(BF16)),
               (kc_ref[:, sl].astype(BF16), vc_ref[:, sl].astype(BF16))]
        o_ref[:, sl] = _attn_core(q[:, sl].astype(BF16), kvs, [("add", bias_ref[h]), None], None)


def _lat_na_attention(proj, kc, vc, bias, NB, Lb):
    T = proj.shape[0]
    rows = Lb // GRID_W
    Lc = kc.shape[1]
    kspec = lambda off: pl.BlockSpec((Lb, 256), lambda b, r: (b, off // 256))
    cspec = pl.BlockSpec((None, Lc, 256), lambda b, r: (b, 0, 0))
    bspec = pl.BlockSpec((4, None, GRID_W, NA_KH * GRID_W),
                         lambda b, r: (0, r - jnp.clip(r - NA_KH // 2, 0, rows - NA_KH), 0, 0))
    return pl.pallas_call(
        functools.partial(_lat_na_kernel, rows=rows),
        grid=(NB, rows),
        in_specs=[pl.BlockSpec((GRID_W, 256), lambda b, r: (b * rows + r, P_NQ // 256)),
                  kspec(P_NK), kspec(P_NV), cspec, cspec, bspec],
        out_specs=pl.BlockSpec((GRID_W, 256), lambda b, r: (b * rows + r, 0)),
        out_shape=jax.ShapeDtypeStruct((T, 256), F32),
        compiler_params=_cp("arbitrary", "arbitrary"),
        name="lat_na_attention",
    )(proj, proj, proj, kc, vc, bias)


def _short_conv_kernel(a_ref, b_ref, c_ref, w_ref, bias_ref, oa_ref, ob_ref, oc_ref, *, L):
    t = lax.broadcasted_iota(jnp.int32, (L, HY_C), 0)
    for n, (x_ref, o_ref) in enumerate(((a_ref, oa_ref), (b_ref, ob_ref), (c_ref, oc_ref))):
        sl = slice(HY_C * n, HY_C * (n + 1))
        x = x_ref[...]
        prev = jnp.where(t == 0, 0.0, pltpu.roll(x, 1, axis=0))
        nxt = jnp.where(t == L - 1, 0.0, pltpu.roll(x, L - 1, axis=0))
        o_ref[...] = prev * w_ref[0:1, sl] + x * w_ref[1:2, sl] + nxt * w_ref[2:3, sl] + bias_ref[:, sl]


def _short_conv(proj, w, b, NB, Lb):
    T = proj.shape[0]
    spec = lambda c: pl.BlockSpec((Lb, HY_C), lambda i: (i, c))
    return pl.pallas_call(
        functools.partial(_short_conv_kernel, L=Lb),
        grid=(NB,),
        in_specs=[spec(P_HY // HY_C), spec(P_HY // HY_C + 1), spec(P_HY // HY_C + 2),
                  pl.BlockSpec((3, 3 * HY_C), lambda i: (0, 0)), pl.BlockSpec((1, 3 * HY_C), lambda i: (0, 0))],
        out_specs=[spec(0)] * 3,
        out_shape=[jax.ShapeDtypeStruct((T, HY_C), F32)] * 3,
        compiler_params=_cp("arbitrary"),
        name="hyena_short_conv",
    )(proj, proj, proj, w, b)


def _hy_filter_kernel(w1_ref, b1_ref, w2_ref, b2_ref, w3_ref, freq_ref, ld_ref, fs_ref, nyq_ref, *, L):
    ti = lax.broadcasted_iota(jnp.int32, (L, 128), 0)
    t = ti.astype(F32)
    j = lax.broadcasted_iota(jnp.int32, (L, 128), 1)
    band = jnp.where(j <= HY_BANDS, j - 1, j - 1 - HY_BANDS).astype(F32)
    ang = (2.0 * math.pi / L) * t * band
    tn = t / L
    z = jnp.where(j == 0, tn, jnp.where(j <= HY_BANDS, jnp.cos(ang),
                                        jnp.where(j <= 2 * HY_BANDS, -jnp.sin(ang), 0.0)))
    a = jnp.sin(freq_ref[0:1, :] * (_dot_hi(z, w1_ref[...]) + b1_ref[...]))
    a = jnp.sin(freq_ref[1:2, :] * (_dot_hi(a, w2_ref[...]) + b2_ref[...]))
    filt = _dot_hi(a, w3_ref[...])
    tcol = lax.broadcasted_iota(jnp.int32, (L, 4 * HY_C), 0)
    filt = filt * jnp.exp(-(tcol.astype(F32) / L) * jnp.exp(ld_ref[...]))
    t1 = lax.broadcasted_iota(jnp.int32, (L, HY_C), 0)
    sign = jnp.where(t1 % 2 == 0, 1.0, -1.0)
    for n in range(2):
        fwd = filt[:, 2 * HY_C * n:2 * HY_C * n + HY_C]
        bwd = jnp.where(t1 == 0, 0.0, filt[:, 2 * HY_C * n + HY_C:2 * HY_C * (n + 1)])
        tot = fwd + bwd
        fs_ref[:, HY_C * n:HY_C * (n + 1)] = tot
        fs_ref[:, 2 * HY_C + HY_C * n:2 * HY_C + HY_C * (n + 1)] = fwd - bwd
        nyq_ref[:, HY_C * n:HY_C * (n + 1)] = (tot * sign).sum(0, keepdims=True)


def _hy_filter(L, w1p, b1, w2, b2, w3, freq, ld):
    full = lambda s: pl.BlockSpec(s, lambda: tuple(0 for _ in s))
    return pl.pallas_call(
        functools.partial(_hy_filter_kernel, L=L),
        in_specs=[full((128, 64)), full((1, 64)), full((64, 64)), full((1, 64)), full((64, 4 * HY_C)),
                  full((2, 64)), full((1, 4 * HY_C))],
        out_specs=[full((L, 4 * HY_C)), full((1, 2 * HY_C))],
        out_shape=[jax.ShapeDtypeStruct((L, 4 * HY_C), F32), jax.ShapeDtypeStruct((1, 2 * HY_C), F32)],
        compiler_params=pltpu.CompilerParams(vmem_limit_bytes=VMEM_LIMIT),
        name="hyena_filter",
    )(w1p, b1, w2, b2, w3, freq, ld)


def _hy_gdft_kernel(cm_ref, sm_ref, fs_ref, nyq_ref, gr_ref, gi_ref, *, tm):
    m = pl.program_id(0)
    f = fs_ref[...].astype(BF16)
    gr_ref[...] = _dot(cm_ref[...], f[:, :2 * HY_C])
    gi = _dot(sm_ref[...], f[:, 2 * HY_C:])
    row = m * tm + lax.broadcasted_iota(jnp.int32, (tm, 2 * HY_C), 0)
    gi_ref[...] = jnp.where(row == 0, nyq_ref[...], gi)


def _hy_gdft(cm, sm, fs, nyq, L, tm):
    return pl.pallas_call(
        functools.partial(_hy_gdft_kernel, tm=tm),
        grid=(L // tm,),
        in_specs=[pl.BlockSpec((tm, L), lambda m: (m, 0)), pl.BlockSpec((tm, L), lambda m: (m, 0)),
                  pl.BlockSpec((L, 4 * HY_C), lambda m: (0, 0)), pl.BlockSpec((1, 2 * HY_C), lambda m: (0, 0))],
        out_specs=[pl.BlockSpec((tm, 2 * HY_C), lambda m: (m, 0))] * 2,
        out_shape=[jax.ShapeDtypeStruct((L, 2 * HY_C), F32)] * 2,
        compiler_params=_cp("arbitrary"),
        name="hyena_filter_dft",
    )(cm, sm, fs, nyq)


def _hy_fwd_kernel(cm_ref, sm_ref, z_ref, gr_ref, gi_ref, yr_ref, yi_ref, *, L, tm, ns):
    m = pl.program_id(1)
    gr = gr_ref[...]
    gi = gi_ref[...]
    row0 = (m * tm + lax.broadcasted_iota(jnp.int32, (tm, HY_C), 0)) == 0
    s = jnp.where(row0, 0.5 / L, 1.0 / L)
    for g in range(ns):
        zb = z_ref[g * L:(g + 1) * L, :].astype(BF16)
        zr = _dot(cm_ref[...], zb)
        zi = _dot(sm_ref[...], zb)
        zigi = zi * gi
        yr_ref[g * tm:(g + 1) * tm, :] = ((zr * gr - jnp.where(row0, 0.0, zigi)) * s).astype(BF16)
        yi_ref[g * tm:(g + 1) * tm, :] = (jnp.where(row0, zigi, zr * gi + zi * gr) * s).astype(BF16)


def _hy_seqs_per_step(NB, Lb, tm):
    ns = max(1, 2048 // Lb) if tm == Lb else 1
    while NB % ns:
        ns //= 2
    return ns


def _hy_fwd(cm, sm, z, gr, gi, n, NB, Lb, tm):
    T = z.shape[0]
    nm = Lb // tm
    ns = _hy_seqs_per_step(NB, Lb, tm)
    return pl.pallas_call(
        functools.partial(_hy_fwd_kernel, L=Lb, tm=tm, ns=ns),
        grid=(NB // ns, nm),
        in_specs=[pl.BlockSpec((tm, Lb), lambda b, m: (m, 0)), pl.BlockSpec((tm, Lb), lambda b, m: (m, 0)),
                  pl.BlockSpec((ns * Lb, HY_C), lambda b, m: (b, 0)),
                  pl.BlockSpec((tm, HY_C), lambda b, m: (m, n)), pl.BlockSpec((tm, HY_C), lambda b, m: (m, n))],
        out_specs=[pl.BlockSpec((ns * tm, HY_C), lambda b, m: (b * nm + m, 0))] * 2,
        out_shape=[jax.ShapeDtypeStruct((T, HY_C), BF16)] * 2,
        compiler_params=_cp("arbitrary", "arbitrary"),
        name="hyena_fwd_dft",
    )(cm, sm, z, gr, gi)


def _hy_inv_kernel(cm_ref, smt_ref, yr_ref, yi_ref, z_ref, g_ref, skip_ref, o_ref, *, L, tm, ns):
    for g in range(ns):
        seq = slice(g * L, (g + 1) * L)
        out = slice(g * tm, (g + 1) * tm)
        conv = _dot(cm_ref[...], yr_ref[seq, :]) + _dot(smt_ref[...], yi_ref[seq, :])
        o_ref[out, :] = g_ref[out, :] * (conv + skip_ref[...] * z_ref[out, :])


def _hy_inv(cm, smt, yr, yi, z, gate, skip, n, NB, Lb, tm):
    T = z.shape[0]
    nm = Lb // tm
    ns = _hy_seqs_per_step(NB, Lb, tm)
    tile = pl.BlockSpec((ns * tm, HY_C), lambda b, m: (b * nm + m, 0))
    seq = pl.BlockSpec((ns * Lb, HY_C), lambda b, m: (b, 0))
    return pl.pallas_call(
        functools.partial(_hy_inv_kernel, L=Lb, tm=tm, ns=ns),
        grid=(NB // ns, nm),
        in_specs=[pl.BlockSpec((tm, Lb), lambda b, m: (m, 0)), pl.BlockSpec((tm, Lb), lambda b, m: (m, 0)),
                  seq, seq, tile, tile, pl.BlockSpec((None, 1, HY_C), lambda b, m: (n, 0, 0))],
        out_specs=tile,
        out_shape=jax.ShapeDtypeStruct((T, HY_C), F32),
        compiler_params=_cp("arbitrary", "arbitrary"),
        name="hyena_inv_dft",
    )(cm, smt, yr, yi, z, gate, skip)


def _dft_mats(L):
    k = jnp.arange(L, dtype=jnp.int32)
    blk = 64

    def trig(mult):
        ang = ((mult[:, None] * k[None, :]) % (2 * L)).astype(F32) * (math.pi / L)
        return jnp.cos(ang), jnp.sin(ang)

    ca, sa = trig(jnp.arange(L // blk, dtype=jnp.int32) * blk)
    cb, sb = trig(jnp.arange(blk, dtype=jnp.int32))
    cm = (ca[:, None, :] * cb[None] - sa[:, None, :] * sb[None]).reshape(L, L)
    s = -(sa[:, None, :] * cb[None] + ca[:, None, :] * sb[None]).reshape(L, L)
    alt = jnp.where(k % 2 == 0, 1.0, -1.0).astype(F32)
    sm = jnp.where(k[:, None] == 0, alt[None, :], s)
    smt = jnp.where(k[None, :] == 0, alt[:, None], s)
    return cm.astype(BF16), sm.astype(BF16), smt.astype(BF16)


def _merge_kernel(oa_ref, ob_ref, oc_ref, od_ref, g0_ref, g1_ref, g2_ref, g3_ref, wb_ref, wo_ref, x_ref, mod_ref,
                  lg_ref, lb_ref, x1_ref, h2_ref, hp_ref):
    acc = None
    for o_ref, g_ref, i in ((oa_ref, g0_ref, 0), (ob_ref, g1_ref, 1), (oc_ref, g2_ref, 2), (od_ref, g3_ref, 3)):
        y = _sigmoid(g_ref[...]) * _dot(o_ref[...].astype(BF16), wb_ref[i])
        acc = y if acc is None else acc + y
    mix = _dot(acc.astype(BF16), wo_ref[...])
    m = mod_ref[...]
    x1 = _layer_norm(DN_ALPHA * x_ref[...] + m[2:3, :] * mix, lg_ref[...], lb_ref[...])
    x1_ref[...] = x1
    h2 = x1 * (1.0 + m[4:5, :]) + m[3:4, :]
    h2_ref[...] = h2
    hp_ref[...] = _pack_pairs(h2)


def _merge(outs, proj, wb, wo, x, mod, lg, lb, mod_row, bm):
    T = x.shape[0]
    row = lambda w: pl.BlockSpec((bm, w), lambda i: (i, 0))
    gspec = lambda n: pl.BlockSpec((bm, D), lambda i: (i, P_GATE // D + n))
    fixed2 = lambda s: pl.BlockSpec(s, lambda i: (0, 0))
    return pl.pallas_call(
        _merge_kernel,
        grid=(T // bm,),
        in_specs=[row(256)] * 4 + [gspec(0), gspec(1), gspec(2), gspec(3),
                                   pl.BlockSpec((4, 256, D), lambda i: (0, 0, 0)), fixed2((D, D)), row(D),
                                   pl.BlockSpec((None, 6, D), lambda i: (mod_row(i), 0, 0)),
                                   fixed2((1, D)), fixed2((1, D))],
        out_specs=[row(D), row(D), row(D // 2)],
        out_shape=[jax.ShapeDtypeStruct((T, D), F32)] * 2 + [jax.ShapeDtypeStruct((T, D // 2), jnp.int32)],
        compiler_params=_cp("arbitrary"),
        name="merge_norm",
    )(*outs, proj, proj, proj, proj, wb, wo, x, mod, lg, lb)


def _router_kernel(h_ref, rt_ref, bias_ref, g_ref, rank_ref, cnt_ref, *, tt):
    per = N_EXPERTS // N_GROUPS
    logits = lax.dot_general(rt_ref[...], h_ref[...], (((1,), (1,)), ((), ())), preferred_element_type=F32,
                             precision=lax.Precision.HIGHEST)
    scores = _sigmoid(logits)
    sel = (scores + bias_ref[...]).reshape(N_GROUPS, per, tt)
    gid = lax.broadcasted_iota(jnp.int32, (N_GROUPS, per, tt), 0).astype(F32)
    jid = lax.broadcasted_iota(jnp.int32, (N_GROUPS, per, tt), 1).astype(F32)
    eid = gid * per + jid
    ninf = -jnp.inf
    m1 = sel.max(1, keepdims=True)
    i1 = jnp.where(sel == m1, jid, float(per)).min(1, keepdims=True)
    m2 = jnp.where(jid == i1, ninf, sel).max(1, keepdims=True)
    gs = m1 + m2
    g1 = lax.broadcasted_iota(jnp.int32, (N_GROUPS, 1, tt), 0).astype(F32)
    chosen = jnp.zeros((N_GROUPS, 1, tt), F32)
    for _ in range(TOPK_GROUPS):
        mx = gs.max(0, keepdims=True)
        gi = jnp.where(gs == mx, g1, float(N_GROUPS)).min(0, keepdims=True)
        pick = g1 == gi
        chosen = jnp.where(pick, 1.0, chosen)
        gs = jnp.where(pick, ninf, gs)
    cand = jnp.where(chosen > 0.0, sel, NEG)
    picked = jnp.zeros((N_GROUPS, per, tt), F32)
    for _ in range(TOP_K):
        mx = cand.max(1, keepdims=True).max(0, keepdims=True)
        ei = jnp.where(cand == mx, eid, float(N_EXPERTS)).min(1, keepdims=True).min(0, keepdims=True)
        pick = eid == ei
        picked = jnp.where(pick, 1.0, picked)
        cand = jnp.where(pick, ninf, cand)
    w = scores.reshape(N_GROUPS, per, tt) * picked
    wsum = w.sum(1, keepdims=True).sum(0, keepdims=True)
    g_ref[...] = (w / wsum * ROUTED_SCALE).reshape(N_EXPERTS, tt)
    pk = picked.reshape(N_EXPERTS, tt)
    t_in = lax.broadcasted_iota(jnp.int32, (tt, tt), 0)
    t_out = lax.broadcasted_iota(jnp.int32, (tt, tt), 1)
    upper = jnp.where(t_in <= t_out, 1.0, 0.0).astype(BF16)

    @pl.when(pl.program_id(0) == 0)
    def _():
        cnt_ref[...] = jnp.zeros_like(cnt_ref)

    before = cnt_ref[:, 0:1]
    rank_ref[...] = jnp.where(pk > 0.0, before + _dot(pk.astype(BF16), upper) - 1.0, -1.0)
    cnt_ref[...] += pk.sum(-1, keepdims=True)


def _router(h2, router_t, bias, tt):
    T = h2.shape[0]
    tile = pl.BlockSpec((N_EXPERTS, tt), lambda i: (0, i))
    return pl.pallas_call(
        functools.partial(_router_kernel, tt=tt),
        grid=(T // tt,),
        in_specs=[pl.BlockSpec((tt, D), lambda i: (i, 0)), pl.BlockSpec((N_EXPERTS, D), lambda i: (0, 0)),
                  pl.BlockSpec((N_EXPERTS, 1), lambda i: (0, 0))],
        out_specs=[tile, tile, pl.BlockSpec((N_EXPERTS, 128), lambda i: (0, 0))],
        out_shape=[jax.ShapeDtypeStruct((N_EXPERTS, T), F32), jax.ShapeDtypeStruct((N_EXPERTS, T), F32),
                   jax.ShapeDtypeStruct((N_EXPERTS, 128), F32)],
        compiler_params=_cp("arbitrary"),
        name="moe_router",
    )(h2, router_t, bias)


def _route_pos_kernel(gate_ref, rank_ref, cnt_ref, pos_ref, w_ref, te_ref, nx_ref, nt_ref, *, tm, nt_max):
    ei = lax.broadcasted_iota(jnp.int32, (N_EXPERTS, N_EXPERTS), 0)
    ej = lax.broadcasted_iota(jnp.int32, (N_EXPERTS, N_EXPERTS), 1)
    below = jnp.where(ej < ei, 1.0, 0.0)
    padded = jnp.ceil(cnt_ref[...] * (1.0 / tm)) * tm
    offs = _dot_hi(below, padded)
    rank = rank_ref[...]
    routed = rank >= 0.0
    pos = offs[:, 0:1] + rank
    slot = _dot(below.astype(BF16), jnp.where(routed, 1.0, 0.0).astype(BF16))
    gate = gate_ref[...]
    for k in range(TOP_K):
        mine = routed & (slot == float(k))
        pos_ref[k:k + 1, :] = jnp.where(mine, pos, 0.0).sum(0, keepdims=True).astype(jnp.int32)
        w_ref[k:k + 1, :] = jnp.where(mine, gate, 0.0).sum(0, keepdims=True)
    ends = (offs + padded)[:, 0:1]
    first = (lax.broadcasted_iota(jnp.int32, (N_EXPERTS, nt_max), 1) * tm).astype(F32)
    te = jnp.minimum(jnp.where(ends <= first, 1.0, 0.0).sum(0, keepdims=True), N_EXPERTS - 1.0)
    te_ref[...] = te.astype(jnp.int32)
    eid = lax.broadcasted_iota(jnp.int32, (N_EXPERTS, nt_max), 0).astype(F32)
    nx_ref[...] = (jnp.where(eid == te, ends, 0.0).sum(0, keepdims=True) * (1.0 / tm)).astype(jnp.int32)
    nt_ref[...] = (padded.sum(0, keepdims=True) * (1.0 / tm)).astype(jnp.int32)


def _route_pos(gate_t, rank, cnt, tt, tm, nt_max):
    T = gate_t.shape[1]
    tile = pl.BlockSpec((N_EXPERTS, tt), lambda i: (0, i))
    out = pl.BlockSpec((TOP_K, tt), lambda i: (0, i))
    return pl.pallas_call(
        functools.partial(_route_pos_kernel, tm=tm, nt_max=nt_max),
        grid=(T // tt,),
        in_specs=[tile, tile, pl.BlockSpec((N_EXPERTS, 128), lambda i: (0, 0))],
        out_specs=[out, out, pl.BlockSpec((1, nt_max), lambda i: (0, 0)), pl.BlockSpec((1, nt_max), lambda i: (0, 0)),
                   pl.BlockSpec((1, 128), lambda i: (0, 0))],
        out_shape=[jax.ShapeDtypeStruct((TOP_K, T), jnp.int32), jax.ShapeDtypeStruct((TOP_K, T), F32),
                   jax.ShapeDtypeStruct((1, nt_max), jnp.int32), jax.ShapeDtypeStruct((1, nt_max), jnp.int32),
                   jax.ShapeDtypeStruct((1, 128), jnp.int32)],
        compiler_params=_cp("arbitrary"),
        name="moe_positions",
    )(gate_t, rank, cnt)


def _gmm_kernel(te_ref, nx_ref, nt_ref, xs_ref, w1_hbm, w3_hbm, w2_hbm, ys_ref, b1_ref, b3_ref, b2_ref,
                f1_ref, f3_ref, f2_ref, seg_ref, sem, *, l):
    j = pl.program_id(0)
    live = j < nt_ref[0]
    new_expert = (j == 0) | (te_ref[j] != te_ref[jnp.maximum(j - 1, 0)])

    def fetch(e, slot):
        return [pltpu.make_async_copy(w_hbm.at[l, e], f_ref.at[slot], sem.at[i, slot])
                for i, (w_hbm, f_ref) in enumerate(((w1_hbm, f1_ref), (w3_hbm, f3_ref), (w2_hbm, f2_ref)))]

    @pl.when(live & new_expert)
    def _():
        @pl.when(j == 0)
        def _():
            seg_ref[0] = 0
            for c in fetch(te_ref[0], 0):
                c.start()

        slot = lax.rem(seg_ref[0], 2)
        for c in fetch(te_ref[j], slot):
            c.wait()
        b1_ref[...] = f1_ref[slot].astype(BF16)
        b3_ref[...] = f3_ref[slot].astype(BF16)
        b2_ref[...] = f2_ref[slot].astype(BF16)
        nxt = nx_ref[j]

        @pl.when(nxt < nt_ref[0])
        def _():
            for c in fetch(te_ref[nxt], 1 - slot):
                c.start()

        seg_ref[0] = seg_ref[0] + 1

    @pl.when(live)
    def _():
        xa, xb = _unpack_pairs(xs_ref[...])
        xa, xb = xa.astype(BF16), xb.astype(BF16)
        half = D // 2
        a = _dot(xa, b1_ref[:half, :]) + _dot(xb, b1_ref[half:, :])
        b = _dot(xa, b3_ref[:half, :]) + _dot(xb, b3_ref[half:, :])
        hid = (a * _sigmoid(a) * b).astype(BF16)
        ys_ref[...] = _pack_pairs(_dot(hid, b2_ref[...]))


def _gmm(te, nx, nt, xs, w1, w3, w2, l, tm):
    n_slots = xs.shape[0]
    ds = D_EXPERT
    rows = pl.BlockSpec((tm, D // 2), lambda j, te, nx, nt: (jnp.minimum(j, nt[0] - 1), 0))
    hbm = pl.BlockSpec(memory_space=pl.ANY)
    return pl.pallas_call(
        functools.partial(_gmm_kernel, l=l),
        grid_spec=pltpu.PrefetchScalarGridSpec(
            num_scalar_prefetch=3,
            grid=(n_slots // tm,),
            in_specs=[rows, hbm, hbm, hbm],
            out_specs=rows,
            scratch_shapes=[pltpu.VMEM((D, ds), BF16), pltpu.VMEM((D, ds), BF16), pltpu.VMEM((ds, D), BF16),
                            pltpu.VMEM((2, D, ds), F32), pltpu.VMEM((2, D, ds), F32), pltpu.VMEM((2, ds, D), F32),
                            pltpu.SMEM((1,), jnp.int32), pltpu.SemaphoreType.DMA((3, 2))]),
        out_shape=jax.ShapeDtypeStruct((n_slots, D // 2), jnp.int32),
        compiler_params=_cp("arbitrary"),
        name="moe_grouped_ffn",
    )(te, nx, nt, xs, w1, w3, w2)


def _combine_kernel(yk_ref, w_ref, hp_ref, s1_ref, s3_ref, s2_ref, x_ref, mod_ref, lg_ref, lb_ref, o_ref):
    w = w_ref[...]
    acc_a = acc_b = None
    for k in range(TOP_K):
        ya, yb = _unpack_pairs(yk_ref[k])
        wk = w[:, k:k + 1]
        acc_a = wk * ya if acc_a is None else acc_a + wk * ya
        acc_b = wk * yb if acc_b is None else acc_b + wk * yb
    ha, hb = _unpack_pairs(hp_ref[...])
    ha, hb = ha.astype(BF16), hb.astype(BF16)
    half = D // 2
    a = _dot(ha, s1_ref[:half, :]) + _dot(hb, s1_ref[half:, :])
    b = _dot(ha, s3_ref[:half, :]) + _dot(hb, s3_ref[half:, :])
    y = jnp.concatenate([acc_a, acc_b], axis=1) + _dot((a * _sigmoid(a) * b).astype(BF16), s2_ref[...])
    m = mod_ref[...]
    o_ref[...] = _layer_norm(DN_ALPHA * x_ref[...] + m[5:6, :] * y, lg_ref[...], lb_ref[...])


def _combine(yk, w, hp, s1, s3, s2, x1, mod, lg, lb, mod_row, bm):
    T = x1.shape[0]
    ds = D_EXPERT
    row = lambda n: pl.BlockSpec((bm, n), lambda i: (i, 0))
    fixed = lambda s: pl.BlockSpec(s, lambda i: (0, 0))
    return pl.pallas_call(
        _combine_kernel,
        grid=(T // bm,),
        in_specs=[pl.BlockSpec((TOP_K, bm, D // 2), lambda i: (0, i, 0)), row(TOP_K), row(D // 2),
                  fixed((D, ds)), fixed((D, ds)), fixed((ds, D)), row(D),
                  pl.BlockSpec((None, 6, D), lambda i: (mod_row(i), 0, 0)), fixed((1, D)), fixed((1, D))],
        out_specs=row(D),
        out_shape=jax.ShapeDtypeStruct((T, D), F32),
        compiler_params=_cp("arbitrary"),
        name="moe_combine_norm",
    )(yk, w, hp, s1, s3, s2, x1, mod, lg, lb)


def _sc_worker():
    return lax.axis_index("s") * SC_CORES + lax.axis_index("c")


def _sc_mesh():
    return plsc.VectorSubcoreMesh(core_axis_name="c", subcore_axis_name="s")


def _sc_gather(table, idx):
    N, W = idx.shape[0], table.shape[1]
    per_w = N // SC_WORKERS
    n_chunks = per_w // SC_ROWS

    def body(table_hbm, idx_hbm, out_hbm, idx_v, rows_v, sem):
        base = _sc_worker() * per_w
        pltpu.sync_copy(idx_hbm.at[pl.ds(base, per_w)], idx_v)

        @pl.loop(0, n_chunks)
        def _(c):
            off = pl.multiple_of(c * SC_ROWS, SC_ROWS)
            pltpu.async_copy(table_hbm.at[idx_v.at[pl.ds(off, SC_ROWS)]], rows_v, sem).wait()
            pltpu.sync_copy(rows_v, out_hbm.at[pl.ds(base + off, SC_ROWS)])

    return pl.kernel(
        body, out_type=jax.ShapeDtypeStruct((N, W), table.dtype), mesh=_sc_mesh(),
        scratch_types=[pltpu.VMEM((per_w,), jnp.int32), pltpu.VMEM((SC_ROWS, W), table.dtype),
                       pltpu.SemaphoreType.DMA],
        name="sc_gather",
    )(table, idx)


def _sc_dispatch(pos, table, n_slots):
    NP, (T, W) = pos.shape[0], table.shape
    per_w = n_slots // SC_WORKERS
    n_chunks = per_w // SC_ROWS
    scan = 8192

    def body(pos_hbm, table_hbm, out_hbm, pos_v, src_v, rows_v, sem):
        base = _sc_worker() * per_w
        lane = lax.iota(jnp.int32, SC_LANES)

        @pl.loop(0, per_w // SC_LANES)
        def _(j):
            o = pl.multiple_of(j * SC_LANES, SC_LANES)
            src_v[pl.ds(o, SC_LANES)] = (base + o + lane) & (T - 1)

        @pl.loop(0, NP // scan)
        def _(c):
            pltpu.sync_copy(pos_hbm.at[pl.ds(pl.multiple_of(c * scan, scan), scan)], pos_v)

            @pl.loop(0, scan // SC_LANES)
            def _(v):
                o = pl.multiple_of(v * SC_LANES, SC_LANES)
                p = pos_v[pl.ds(o, SC_LANES)] - base
                mine = (p >= 0) & (p < per_w)
                tok = (c * scan + o + lane) & (T - 1)
                plsc.store_scatter(src_v, [jnp.where(mine, p, 0)], tok, mask=mine)

        @pl.loop(0, n_chunks)
        def _(c):
            off = pl.multiple_of(c * SC_ROWS, SC_ROWS)
            pltpu.async_copy(table_hbm.at[src_v.at[pl.ds(off, SC_ROWS)]], rows_v, sem).wait()
            pltpu.sync_copy(rows_v, out_hbm.at[pl.ds(base + off, SC_ROWS)])

    return pl.kernel(
        body, out_type=jax.ShapeDtypeStruct((n_slots, W), table.dtype), mesh=_sc_mesh(),
        scratch_types=[pltpu.VMEM((scan,), jnp.int32), pltpu.VMEM((per_w,), jnp.int32),
                       pltpu.VMEM((SC_ROWS, W), table.dtype), pltpu.SemaphoreType.DMA],
        compiler_params=pltpu.CompilerParams(needs_layout_passes=False),
        name="sc_dispatch",
    )(pos, table)


def _rot_cols(w, q):
    a, b, c, d = w[..., :q], w[..., q:2 * q], w[..., 2 * q:3 * q], w[..., 3 * q:]
    return jnp.concatenate([-b, a, -d, c], -1)


def _prep_w_in(w):
    z = lambda n: jnp.zeros((D, n), w.dtype)
    qlat, ckv, kpe, hy = w[:, 0:256], w[:, 256:384], w[:, 384:416], w[:, 416:1184]
    wq, wk, wv = w[:, 1184:1440], w[:, 1440:1568], w[:, 1568:1696]
    nq, nk, nv, gate = w[:, 1696:1952], w[:, 1952:2208], w[:, 2208:2464], w[:, 2464:]
    wq_r = _rot_cols(wq.reshape(D, 4, 64), 16).reshape(D, 256)
    wk_r = _rot_cols(wk.reshape(D, 2, 64), 16).reshape(D, 128)
    kpe_r = _rot_cols(kpe, 8)
    cols = [qlat, ckv, z(64), kpe, z(32), hy, wq, wk, wv, nq, nk, nv, wq_r, wk_r, z(64), kpe_r, z(32), gate]
    return jnp.concatenate(cols, 1).astype(BF16)


def _prep_mla(w_uq, w_ukv):
    uq = w_uq.reshape(256, 4, 96)
    nope, pe = uq[..., :64], uq[..., 64:]
    z32 = jnp.zeros((256, 4, 32), w_uq.dtype)
    z64 = jnp.zeros((256, 4, 64), w_uq.dtype)
    wcat = jnp.concatenate([nope, pe, z32], -1).reshape(256, 512).astype(BF16)
    wrot = jnp.concatenate([z64, _rot_cols(pe, 8), z32], -1).reshape(256, 512).astype(BF16)
    ukv = w_ukv.reshape(128, 4, 128)
    wk = jnp.concatenate([ukv[..., :64], jnp.zeros((128, 4, 64), w_ukv.dtype)], -1).reshape(128, 512).astype(BF16)
    wv = ukv[..., 64:].reshape(128, 256).astype(BF16)
    return wcat, wrot, wk, wv


def _rope_tab(L, q):
    t = jnp.arange(L)
    inv = ROPE_BASE ** (-jnp.arange(q, dtype=F32) / q)
    ar = (t // GRID_W).astype(F32)[:, None] * inv[None, :]
    ac = (t % GRID_W).astype(F32)[:, None] * inv[None, :]
    cos = jnp.concatenate([jnp.cos(ar), jnp.cos(ar), jnp.cos(ac), jnp.cos(ac)], 1)
    sin = jnp.concatenate([jnp.sin(ar), jnp.sin(ar), jnp.sin(ac), jnp.sin(ac)], 1)
    return cos, sin


def _rope_tables(L):
    c8, s8 = _rope_tab(L, 8)
    c16, s16 = _rope_tab(L, 16)
    one, zero = jnp.ones((L, 64), F32), jnp.zeros((L, 64), F32)
    z32 = jnp.zeros((L, 32), F32)
    mla_q = (jnp.tile(jnp.concatenate([one, c8, z32], 1), (1, 4)), jnp.tile(jnp.concatenate([zero, s8, z32], 1), (1, 4)))
    mla_k = (jnp.concatenate([zero, c8, z32], 1), jnp.concatenate([zero, s8, z32], 1))
    win = (jnp.tile(c16, (1, 4)), jnp.tile(s16, (1, 4)), jnp.tile(c16, (1, 2)), jnp.tile(s16, (1, 2)))
    return mla_q + mla_k, win


def _hyena(proj, lp, dft, NB, Lb):
    cm, sm, smt = dft
    tm = min(Lb, 512)
    v, x1, x2 = _short_conv(proj, lp["hy_conv_w"], lp["hy_conv_b"].reshape(1, -1), NB, Lb)
    w1p = jnp.pad(lp["hy_w1"], ((0, 128 - lp["hy_w1"].shape[0]), (0, 0)))
    fs, nyq = _hy_filter(Lb, w1p, lp["hy_b1"].reshape(1, -1), lp["hy_w2"], lp["hy_b2"].reshape(1, -1), lp["hy_w3"],
                         lp["hy_sin_freq"], lp["hy_log_decay"].reshape(1, -1))
    gr, gi = _hy_gdft(cm, sm, fs, nyq, Lb, tm)
    skip = lp["hy_skip"].reshape(2, 1, HY_C)
    z = v
    for n, gate in enumerate((x1, x2)):
        yr, yi = _hy_fwd(cm, sm, z, gr, gi, n, NB, Lb, tm)
        z = _hy_inv(cm, smt, yr, yi, z, gate, skip, n, NB, Lb, tm)
    return z


def _layer(x, mod, lp, l, NB, Lb, mod_row_of_batch, dft, cache=None, tabs=None, na_bias=None):
    T = NB * Lb
    latent = cache is not None
    bm = 256
    rows_of = lambda n: (lambda i: mod_row_of_batch((i * n) // Lb))
    mod_row = rows_of(bm)
    span = Lb if latent else T
    bmp = min(span, 1024)
    proj = _in_proj(x, mod, lp["w_in_p"], rows_of(bmp), bmp)

    gq, gkv = lp["mla_q_norm"].reshape(1, -1), lp["mla_kv_norm"].reshape(1, -1)
    wcat, wrot, wk, wv = lp["mla_w"]
    q_all, ckv_n, kpe_r = _mla_q(proj, gq, gkv, wcat, wrot, tabs[0] if latent else None, Lb, bm)
    if latent:
        ckv_c, kpe_c, kc_c, vc_c, kd_c, vd_c = cache
        Lc = ckv_c.shape[1]
        kpe_cp = jnp.pad(kpe_c, ((0, 0), (0, 0), (64, 32)))
        ckv_all = jnp.concatenate([ckv_c, ckv_n.reshape(NB, Lb, 128)], 1).reshape(NB * (Lc + Lb), 128)
        kpe_all = jnp.concatenate([kpe_cp, kpe_r.reshape(NB, Lb, 128)], 1).reshape(NB * (Lc + Lb), 128)
        k_all, v_all = _mla_kv(ckv_all, kpe_all, wk, wv, 512)
        oa = _lat_mla_attention(q_all, k_all, v_all, NB, Lb, Lc + Lb, 256)
        oc = _lat_win_attention(proj, kc_c.reshape(NB, Lc, 128), vc_c.reshape(NB, Lc, 128), tabs[1],
                                lp["win_sink"], NB, Lb)
        od = _lat_na_attention(proj, kd_c.reshape(NB, Lc, 256), vd_c.reshape(NB, Lc, 256), na_bias, NB, Lb)
    else:
        k_all, v_all = _mla_kv(ckv_n, kpe_r, wk, wv, 512)
        oa, oc, od = _ctx_attention(proj, q_all, k_all, v_all, lp["win_sink"], NB, Lb)
    ob = _hyena(proj, lp, dft, NB, Lb)

    bmm = min(span, 512)
    x1, h2, hp = _merge((oa, ob, oc, od), proj, lp["w_branch_b"], lp["w_out_b"], x, mod,
                        lp["ln1_g"].reshape(1, -1), lp["ln1_b"].reshape(1, -1), rows_of(bmm), bmm)
    n_slots = T * TOP_K + N_EXPERTS * MOE_TM
    gate_t, rank, cnt = _router(h2, lp["moe_router"].T, lp["moe_bias"].reshape(-1, 1), 512)
    pos, w8, te, nx, nt = _route_pos(gate_t, rank, cnt, 512, MOE_TM, n_slots // MOE_TM)
    xs = _sc_dispatch(pos.reshape(-1), hp, n_slots)
    ys = _gmm(te.reshape(-1), nx.reshape(-1), nt.reshape(-1)[:1], xs, lp["moe_w1"], lp["moe_w3"], lp["moe_w2"], l,
              MOE_TM)
    yk = _sc_gather(ys, pos.reshape(-1)).reshape(TOP_K, T, D // 2)
    x2 = _combine(yk, w8.T, hp, lp["sh_w1_b"], lp["sh_w3_b"], lp["sh_w2_b"], x1, mod,
                  lp["ln2_g"].reshape(1, -1), lp["ln2_b"].reshape(1, -1), mod_row, bm)
    return x2, (proj, ckv_n)


def kernel(x_prompt, x_sample, cache_mla_ckv, cache_mla_kpe, cache_win_k, cache_win_v, cache_na_k, cache_na_v, c, c_ctx, w_ada, b_ada, w_in, mla_q_norm, mla_kv_norm, mla_w_uq, mla_w_ukv, hy_conv_w, hy_conv_b, hy_w1, hy_b1, hy_w2, hy_b2, hy_w3, hy_sin_freq, hy_log_decay, hy_skip, win_sink, na_rpb, w_branch, w_out, ln1_g, ln1_b, ln2_g, ln2_b, moe_router, moe_bias, moe_w1, moe_w3, moe_w2, sh_w1, sh_w3, sh_w2):
    B, S, _ = x_prompt.shape
    DB, DS, _ = x_sample.shape
    xp = x_prompt.reshape(B * S, D)
    xs = x_sample.reshape(DB * DS, D)
    cvec = jnp.concatenate([c_ctx[None, :], c, jnp.zeros((8 - 1 - DB, D), F32)], 0)
    dft_ctx = _dft_mats(S)
    dft_lat = _dft_mats(DS)
    tabs = _rope_tables(DS)
    new = [[] for _ in range(6)]
    for l in range(DEPTH):
        lp = dict(w_in_p=_prep_w_in(w_in[l]), mla_q_norm=mla_q_norm[l], mla_kv_norm=mla_kv_norm[l],
                  mla_w=_prep_mla(mla_w_uq[l], mla_w_ukv[l]), hy_conv_w=hy_conv_w[l], hy_conv_b=hy_conv_b[l],
                  hy_w1=hy_w1[l], hy_b1=hy_b1[l], hy_w2=hy_w2[l], hy_b2=hy_b2[l], hy_w3=hy_w3[l],
                  hy_sin_freq=hy_sin_freq[l], hy_log_decay=hy_log_decay[l], hy_skip=hy_skip[l],
                  win_sink=win_sink[l], w_branch_b=w_branch[l].astype(BF16), w_out_b=w_out[l].astype(BF16),
                  ln1_g=ln1_g[l], ln1_b=ln1_b[l], ln2_g=ln2_g[l], ln2_b=ln2_b[l],
                  moe_router=moe_router[l], moe_bias=moe_bias[l], moe_w1=moe_w1, moe_w3=moe_w3, moe_w2=moe_w2,
                  sh_w1_b=sh_w1[l].astype(BF16), sh_w3_b=sh_w3[l].astype(BF16), sh_w2_b=sh_w2[l].astype(BF16))
        mod = _modulation(cvec, w_ada, b_ada, l)
        xp, (proj, ckv_n) = _layer(xp, mod, lp, l, B, S, lambda b: 0, dft_ctx)
        new[0].append(ckv_n.reshape(B, S, 128))
        new[1].append(proj[:, P_KPE + 64:P_KPE + 96].reshape(B, S, 32))
        new[2].append(proj[:, P_WK:P_WK + 128].reshape(B, S, 2, 64))
        new[3].append(proj[:, P_WV:P_WV + 128].reshape(B, S, 2, 64))
        new[4].append(proj[:, P_NK:P_NK + 256].reshape(B, S, 4, 64))
        new[5].append(proj[:, P_NV:P_NV + 256].reshape(B, S, 4, 64))
        cache = (cache_mla_ckv[:, l], cache_mla_kpe[:, l], cache_win_k[:, l], cache_win_v[:, l],
                 cache_na_k[:, l], cache_na_v[:, l])
        xs, _ = _layer(xs, mod, lp, l, DB, DS, lambda b: 1 + b, dft_lat, cache=cache, tabs=tabs,
                       na_bias=_na_bias(na_rpb[l]))
    return (xp.reshape(B, S, D), xs.reshape(DB, DS, D)) + tuple(jnp.stack(t, 1) for t in new)
```

```python
import functools
import math

import jax
import jax.numpy as jnp
from jax import lax
from jax.experimental import pallas as pl
from jax.experimental.pallas import tpu as pltpu
from jax.experimental.pallas import tpu_sc as plsc

F32 = jnp.float32
BF16 = jnp.bfloat16

D = 1024
DEPTH = 2
GRID_W = 64
HEAD_DIM = 64
MLA_SCALE = 96 ** -0.5
ATT_SCALE = HEAD_DIM ** -0.5
HY_C = 256
HY_BANDS = 8
NA_KH = 8
NA_KW = 16
N_EXPERTS = 64
N_GROUPS = 8
TOP_K = 8
TOPK_GROUPS = 4
D_EXPERT = 256
ROUTED_SCALE = 2.5
ROPE_BASE = 10000.0
LN_EPS = 1e-5
RMS_EPS = 1e-6
NEG = -1e30
DN_ALPHA = (2 * DEPTH) ** 0.25

P_QLAT, P_CKV, P_KPE, P_HY = 0, 256, 384, 512
P_WQ, P_WK, P_WV = 1280, 1536, 1664
P_NQ, P_NK, P_NV = 1792, 2048, 2304
P_WQR, P_WKR, P_KPER, P_GATE = 2560, 2816, 2944, 3072
N_PROJ = 7168

VMEM_LIMIT = 56 * 1024 * 1024

SC_CORES = 2
SC_SUBCORES = 16
SC_LANES = 16
SC_WORKERS = SC_CORES * SC_SUBCORES
SC_ROWS = 64

MOE_TM = 512

def _cp(*sem):
    return pltpu.CompilerParams(dimension_semantics=sem, vmem_limit_bytes=VMEM_LIMIT)


def _sigmoid(x):
    return 1.0 / (1.0 + jnp.exp(-x))


def _dot(a, b):
    return jnp.dot(a, b, preferred_element_type=F32)


def _dot_nt(a, b):
    return lax.dot_general(a, b, (((1,), (1,)), ((), ())), preferred_element_type=F32)


def _dot_hi(a, b):
    return jnp.dot(a, b, preferred_element_type=F32, precision=lax.Precision.HIGHEST)


def _pack_pairs(x):
    w = x.shape[1] // 2
    hi = lax.bitcast_convert_type(x[:, :w].astype(BF16).astype(F32), jnp.int32)
    lo = lax.bitcast_convert_type(x[:, w:].astype(BF16).astype(F32), jnp.int32)
    return hi | lax.shift_right_logical(lo, 16)


def _unpack_pairs(p):
    hi = lax.bitcast_convert_type(p & jnp.int32(-65536), F32)
    lo = lax.bitcast_convert_type(lax.shift_left(p, 16), F32)
    return hi, lo


def _layer_norm(x, g, b):
    mu = jnp.mean(x, -1, keepdims=True)
    xc = x - mu
    var = jnp.mean(xc * xc, -1, keepdims=True)
    return xc * lax.rsqrt(var + LN_EPS) * g + b


def _rms_norm(x, g):
    return x * lax.rsqrt(jnp.mean(x * x, -1, keepdims=True) + RMS_EPS) * g


def _mod_kernel(c_ref, w_ref, b_ref, o_ref):
    c = c_ref[...]
    a = (c * _sigmoid(c)).astype(BF16)
    o_ref[...] = _dot(a, w_ref[...].astype(BF16)) + b_ref[...]


def _modulation(cvec, w_ada, b_ada, l):
    out = pl.pallas_call(
        _mod_kernel,
        grid=(6,),
        in_specs=[pl.BlockSpec((8, D), lambda j: (0, 0)),
                  pl.BlockSpec((None, D, D), lambda j: (l, 0, j)),
                  pl.BlockSpec((None, 1, D), lambda j: (l, 0, j))],
        out_specs=pl.BlockSpec((8, D), lambda j: (0, j)),
        out_shape=jax.ShapeDtypeStruct((8, 6 * D), F32),
        compiler_params=_cp("arbitrary"),
        name="modulation",
    )(cvec, w_ada, b_ada.reshape(DEPTH, 1, 6 * D))
    return out.reshape(8, 6, D)


def _inproj_kernel(x_ref, mod_ref, w_ref, o_ref):
    m = mod_ref[...]
    h = x_ref[...] * (1.0 + m[1:2, :]) + m[0:1, :]
    o_ref[...] = _dot(h.astype(BF16), w_ref[...]).astype(o_ref.dtype)


def _in_proj(x, mod, w_p, mod_row, bm, first, n_cols, out_dtype):
    T = x.shape[0]
    bn = 1024
    return pl.pallas_call(
        _inproj_kernel,
        grid=(n_cols // bn, T // bm),
        in_specs=[pl.BlockSpec((bm, D), lambda j, i: (i, 0)),
                  pl.BlockSpec((None, 6, D), lambda j, i: (mod_row(i), 0, 0)),
                  pl.BlockSpec((D, bn), lambda j, i: (0, first // bn + j))],
        out_specs=pl.BlockSpec((bm, bn), lambda j, i: (i, j)),
        out_shape=jax.ShapeDtypeStruct((T, n_cols), out_dtype),
        compiler_params=_cp("arbitrary", "arbitrary"),
        name="in_proj",
    )(x, mod, w_p)


def _mla_q_kernel(*refs, rope):
    if rope:
        (ql_ref, ckv_ref, kpe_ref, kper_ref, gq_ref, gkv_ref, wc_ref, wr_ref,
         cq_ref, sq_ref, ck_ref, sk_ref, q_ref, ckvn_ref, kpeo_ref) = refs
    else:
        ql_ref, ckv_ref, kpe_ref, gq_ref, gkv_ref, wc_ref, q_ref, ckvn_ref, kpeo_ref = refs
    qn = _rms_norm(ql_ref[...], gq_ref[...]).astype(BF16)
    q = _dot(qn, wc_ref[...])
    if rope:
        q = q * cq_ref[...] + _dot(qn, wr_ref[...]) * sq_ref[...]
        kpeo_ref[...] = kpe_ref[...] * ck_ref[...] + kper_ref[...] * sk_ref[...]
    else:
        kpeo_ref[...] = kpe_ref[...]
    q_ref[...] = (q * MLA_SCALE).astype(BF16)
    ckvn_ref[...] = _rms_norm(ckv_ref[...], gkv_ref[...])


def _mla_q(proj, gq, gkv, wcat, wrot, tabs, Lb, bm):
    T = proj.shape[0]
    rope = tabs is not None
    nl = Lb // bm
    col = lambda c: (lambda i: (i, c))
    fixed = lambda i: (0, 0)
    in_specs = [pl.BlockSpec((bm, 256), col(P_QLAT // 256)),
                pl.BlockSpec((bm, 128), col(P_CKV // 128)),
                pl.BlockSpec((bm, 128), col(P_KPE // 128))]
    args = [proj, proj, proj]
    if rope:
        in_specs.append(pl.BlockSpec((bm, 128), col(P_KPER // 128)))
        args.append(proj)
    in_specs += [pl.BlockSpec((1, 256), fixed), pl.BlockSpec((1, 128), fixed), pl.BlockSpec((256, 512), fixed)]
    args += [gq, gkv, wcat]
    if rope:
        cq, sq, ck, sk = tabs
        pos = lambda i: (i % nl, 0)
        in_specs += [pl.BlockSpec((256, 512), fixed), pl.BlockSpec((bm, 512), pos), pl.BlockSpec((bm, 512), pos),
                     pl.BlockSpec((bm, 128), pos), pl.BlockSpec((bm, 128), pos)]
        args += [wrot, cq, sq, ck, sk]
    return pl.pallas_call(
        functools.partial(_mla_q_kernel, rope=rope),
        grid=(T // bm,),
        in_specs=in_specs,
        out_specs=[pl.BlockSpec((bm, 512), lambda i: (i, 0)),
                   pl.BlockSpec((bm, 128), lambda i: (i, 0)),
                   pl.BlockSpec((bm, 128), lambda i: (i, 0))],
        out_shape=[jax.ShapeDtypeStruct((T, 512), BF16),
                   jax.ShapeDtypeStruct((T, 128), F32),
                   jax.ShapeDtypeStruct((T, 128), F32)],
        compiler_params=_cp("arbitrary"),
        name="mla_q",
    )(*args)


def _mla_kv_kernel(ckv_ref, kpe_ref, wk_ref, wv_ref, k_ref, v_ref):
    c = ckv_ref[...].astype(BF16)
    kpe = kpe_ref[...]
    k_ref[...] = (_dot(c, wk_ref[...]) + jnp.concatenate([kpe] * 4, axis=1)).astype(BF16)
    v_ref[...] = _dot(c, wv_ref[...]).astype(BF16)


def _mla_kv(ckv, kpe, wk, wv, bm):
    Tk = ckv.shape[0]
    return pl.pallas_call(
        _mla_kv_kernel,
        grid=(Tk // bm,),
        in_specs=[pl.BlockSpec((bm, 128), lambda i: (i, 0)),
                  pl.BlockSpec((bm, 128), lambda i: (i, 0)),
                  pl.BlockSpec((128, 512), lambda i: (0, 0)),
                  pl.BlockSpec((128, 256), lambda i: (0, 0))],
        out_specs=[pl.BlockSpec((bm, 512), lambda i: (i, 0)),
                   pl.BlockSpec((bm, 256), lambda i: (i, 0))],
        out_shape=[jax.ShapeDtypeStruct((Tk, 512), BF16),
                   jax.ShapeDtypeStruct((Tk, 256), BF16)],
        compiler_params=_cp("arbitrary"),
        name="mla_kv",
    )(ckv, kpe, wk, wv)


def _attn_core(q, kvs, masks, sink):
    ss = []
    for (k, _), mk in zip(kvs, masks):
        s = _dot_nt(q, k)
        if mk is not None:
            s = s + mk[1] if mk[0] == "add" else jnp.where(mk[1], s, NEG)
        ss.append(s)
    m = ss[0].max(-1, keepdims=True)
    for s in ss[1:]:
        m = jnp.maximum(m, s.max(-1, keepdims=True))
    if sink is not None:
        m = jnp.maximum(m, sink)
    den = None
    acc = None
    for s, (_, v) in zip(ss, kvs):
        p = jnp.exp(s - m)
        d = p.sum(-1, keepdims=True)
        a = _dot(p.astype(BF16), v)
        den = d if den is None else den + d
        acc = a if acc is None else acc + a
    if sink is not None:
        den = den + jnp.exp(sink - m)
    return acc / den


def _ctx_attn_kernel(qm_ref, km_ref, vm_ref, wq_ref, wk_ref, wv_ref, nq_ref, nk_ref, nv_ref, sink_ref,
                     om_ref, ow_ref, on_ref):
    for h in range(4):
        q = qm_ref[:, 128 * h:128 * (h + 1)]
        k = km_ref[:, 128 * h:128 * (h + 1)]
        v = vm_ref[:, 64 * h:64 * (h + 1)]
        om_ref[:, 64 * h:64 * (h + 1)] = _attn_core(q, [(k, v)], [None], None)
    for h in range(4):
        g = h // 2
        q = (wq_ref[:, 64 * h:64 * (h + 1)] * ATT_SCALE).astype(BF16)
        k = wk_ref[:, 64 * g:64 * (g + 1)].astype(BF16)
        v = wv_ref[:, 64 * g:64 * (g + 1)].astype(BF16)
        ow_ref[:, 64 * h:64 * (h + 1)] = _attn_core(q, [(k, v)], [None], sink_ref[h])
    for h in range(4):
        q = (nq_ref[:, 64 * h:64 * (h + 1)] * ATT_SCALE).astype(BF16)
        k = nk_ref[:, 64 * h:64 * (h + 1)].astype(BF16)
        v = nv_ref[:, 64 * h:64 * (h + 1)].astype(BF16)
        on_ref[:, 64 * h:64 * (h + 1)] = _attn_core(q, [(k, v)], [None], None)


def _ctx_attention(proj, q_all, k_all, v_all, sink, NB, Lb):
    T = proj.shape[0]
    pc = lambda w, off: pl.BlockSpec((Lb, w), lambda b: (b, off // w))
    row = lambda w: pl.BlockSpec((Lb, w), lambda b: (b, 0))
    return pl.pallas_call(
        _ctx_attn_kernel,
        grid=(NB,),
        in_specs=[row(512), row(512), row(256),
                  pc(256, P_WQ), pc(128, P_WK), pc(128, P_WV),
                  pc(256, P_NQ), pc(256, P_NK), pc(256, P_NV),
                  pl.BlockSpec(memory_space=pltpu.SMEM)],
        out_specs=[row(256), row(256), row(256)],
        out_shape=[jax.ShapeDtypeStruct((T, 256), F32)] * 3,
        compiler_params=_cp("arbitrary"),
        name="ctx_attention",
    )(q_all, k_all, v_all, proj, proj, proj, proj, proj, proj, sink)


def _lat_mla_kernel(q_ref, k_ref, v_ref, o_ref):
    for h in range(4):
        q = q_ref[:, 128 * h:128 * (h + 1)]
        k = k_ref[:, 128 * h:128 * (h + 1)]
        v = v_ref[:, 64 * h:64 * (h + 1)]
        o_ref[:, 64 * h:64 * (h + 1)] = _attn_core(q, [(k, v)], [None], None)


def _lat_mla_attention(q_all, k_all, v_all, NB, Lb, Lk, tq):
    T = q_all.shape[0]
    nq = Lb // tq
    return pl.pallas_call(
        _lat_mla_kernel,
        grid=(NB, nq),
        in_specs=[pl.BlockSpec((tq, 512), lambda b, i: (b * nq + i, 0)),
                  pl.BlockSpec((Lk, 512), lambda b, i: (b, 0)),
                  pl.BlockSpec((Lk, 256), lambda b, i: (b, 0))],
        out_specs=pl.BlockSpec((tq, 256), lambda b, i: (b * nq + i, 0)),
        out_shape=jax.ShapeDtypeStruct((T, 256), F32),
        compiler_params=_cp("arbitrary", "arbitrary"),
        name="lat_mla_attention",
    )(q_all, k_all, v_all)


def _attn_local_ctx(q, locs, kc, vc, sink):
    s_ctx = _dot_nt(q, kc)
    m_ctx = s_ctx.max(-1, keepdims=True)
    if sink is not None:
        m_ctx = jnp.maximum(m_ctx, sink)
    ms, dens, accs = [], [], []
    for rs, k, v, mk in locs:
        s = _dot_nt(q[rs], k)
        s = s + mk[1] if mk[0] == "add" else jnp.where(mk[1], s, NEG)
        m = jnp.maximum(s.max(-1, keepdims=True), m_ctx[rs])
        p = jnp.exp(s - m)
        ms.append(m)
        dens.append(p.sum(-1, keepdims=True))
        accs.append(_dot(p.astype(BF16), v))
    m = jnp.concatenate(ms, axis=0)
    p = jnp.exp(s_ctx - m)
    den = jnp.concatenate(dens, axis=0) + p.sum(-1, keepdims=True)
    if sink is not None:
        den = den + jnp.exp(sink - m)
    return (jnp.concatenate(accs, axis=0) + _dot(p.astype(BF16), vc)) / den


def _lat_win_kernel(q_ref, qr_ref, k_ref, kr_ref, v_ref, kc_ref, vc_ref, cq_ref, sq_ref, ck_ref, sk_ref,
                    sink_ref, o_ref, *, Lb, bpt):
    t = pl.program_id(1)
    q = (q_ref[...] * cq_ref[...] + qr_ref[...] * sq_ref[...]) * ATT_SCALE
    kc = kc_ref[...].astype(BF16)
    vc = vc_ref[...].astype(BF16)
    blocks = []
    for bb in range(bpt):
        i = t * bpt + bb
        start = pl.multiple_of(jnp.clip((i - 1) * 128, 0, Lb - 384), 128)
        win = pl.ds(start, 384)
        kk = (k_ref[win, :] * ck_ref[win, :] + kr_ref[win, :] * sk_ref[win, :]).astype(BF16)
        qpos = i * 128 + lax.broadcasted_iota(jnp.int32, (128, 384), 0)
        kpos = start + lax.broadcasted_iota(jnp.int32, (128, 384), 1)
        blocks.append((kk, v_ref[win, :].astype(BF16), jnp.abs(qpos - kpos) <= 128))
    for h in range(4):
        g = h // 2
        sl = slice(64 * g, 64 * (g + 1))
        locs = [(slice(128 * bb, 128 * (bb + 1)), kk[:, sl], vv[:, sl], ("keep", valid))
                for bb, (kk, vv, valid) in enumerate(blocks)]
        qh = q[:, 64 * h:64 * (h + 1)].astype(BF16)
        o_ref[:, 64 * h:64 * (h + 1)] = _attn_local_ctx(qh, locs, kc[:, sl], vc[:, sl], sink_ref[h])


def _lat_win_attention(proj, kc, vc, tabs, sink, NB, Lb):
    T = proj.shape[0]
    bpt = 1
    tq = 128 * bpt
    nt = Lb // tq
    Lc = kc.shape[1]
    cq, sq, ck, sk = tabs
    qspec = lambda off: pl.BlockSpec((tq, 256), lambda b, i: (b * nt + i, off // 256))
    kspec = lambda off: pl.BlockSpec((Lb, 128), lambda b, i: (b, off // 128))
    cspec = pl.BlockSpec((None, Lc, 128), lambda b, i: (b, 0, 0))
    return pl.pallas_call(
        functools.partial(_lat_win_kernel, Lb=Lb, bpt=bpt),
        grid=(NB, nt),
        in_specs=[qspec(P_WQ), qspec(P_WQR), kspec(P_WK), kspec(P_WKR), kspec(P_WV), cspec, cspec,
                  pl.BlockSpec((tq, 256), lambda b, i: (i, 0)), pl.BlockSpec((tq, 256), lambda b, i: (i, 0)),
                  pl.BlockSpec((Lb, 128), lambda b, i: (0, 0)), pl.BlockSpec((Lb, 128), lambda b, i: (0, 0)),
                  pl.BlockSpec(memory_space=pltpu.SMEM)],
        out_specs=pl.BlockSpec((tq, 256), lambda b, i: (b * nt + i, 0)),
        out_shape=jax.ShapeDtypeStruct((T, 256), F32),
        compiler_params=_cp("arbitrary", "arbitrary"),
        name="lat_win_attention",
    )(proj, proj, proj, proj, proj, kc, vc, cq, sq, ck, sk, sink)


def _na_bias_kernel(rpb_ref, o_ref):
    h = pl.program_id(0)
    qc = lax.broadcasted_iota(jnp.int32, (GRID_W, GRID_W), 0)
    kc = lax.broadcasted_iota(jnp.int32, (GRID_W, GRID_W), 1)
    dc = kc - qc + (NA_KW - 1)
    wstart = jnp.clip(qc - NA_KW // 2, 0, GRID_W - NA_KW)
    ok = (kc >= wstart) & (kc < wstart + NA_KW)
    n_dc = 2 * NA_KW - 1
    n_dr = 2 * NA_KH - 1
    tabs = []
    for dr in range(n_dr):
        t = jnp.zeros((GRID_W, GRID_W), F32)
        for j in range(n_dc):
            t = jnp.where(dc == j, rpb_ref[(h * n_dr + dr) * n_dc + j], t)
        tabs.append(jnp.where(ok, t, NEG))
    for o in range(NA_KH):
        for a in range(NA_KH):
            o_ref[o, :, GRID_W * a:GRID_W * (a + 1)] = tabs[a + NA_KH - 1 - o]


def _na_bias(rpb):
    H = rpb.shape[0]
    return pl.pallas_call(
        _na_bias_kernel,
        grid=(H,),
        in_specs=[pl.BlockSpec(memory_space=pltpu.SMEM)],
        out_specs=pl.BlockSpec((None, NA_KH, GRID_W, NA_KH * GRID_W), lambda h: (h, 0, 0, 0)),
        out_shape=jax.ShapeDtypeStruct((H, NA_KH, GRID_W, NA_KH * GRID_W), F32),
        compiler_params=_cp("arbitrary"),
        name="na_bias",
    )(rpb.reshape(-1))


def _lat_na_kernel(q_ref, k_ref, v_ref, kc_ref, vc_ref, bias_ref, o_ref, *, rows, rpt):
    t = pl.program_id(1)
    q = q_ref[...] * ATT_SCALE
    kc = kc_ref[...].astype(BF16)
    vc = vc_ref[...].astype(BF16)
    bands = []
    for rr in range(rpt):
        r = t * rpt + rr
        first = jnp.clip(r - NA_KH // 2, 0, rows - NA_KH)
        win = pl.ds(pl.multiple_of(first * GRID_W, GRID_W), NA_KH * GRID_W)
        bands.append((k_ref[win, :].astype(BF16), v_ref[win, :].astype(BF16), r - first))
    for h in range(4):
        sl = slice(64 * h, 64 * (h + 1))
        locs = [(slice(GRID_W * rr, GRID_W * (rr + 1)), kk[:, sl], vv[:, sl], ("add", bias_ref[h, off]))
                for rr, (kk, vv, off) in enumerate(bands)]
        o_ref[:, sl] = _attn_local_ctx(q[:, sl].astype(BF16), locs, kc[:, sl], vc[:, sl], None)


def _lat_na_attention(proj, kc, vc, bias, NB, Lb):
    T = proj.shape[0]
    rows = Lb // GRID_W
    rpt = 8
    tq = GRID_W * rpt
    nt = rows // rpt
    Lc = kc.shape[1]
    kspec = lambda off: pl.BlockSpec((Lb, 256), lambda b, t: (b, off // 256))
    cspec = pl.BlockSpec((None, Lc, 256), lambda b, t: (b, 0, 0))
    return pl.pallas_call(
        functools.partial(_lat_na_kernel, rows=rows, rpt=rpt),
        grid=(NB, nt),
        in_specs=[pl.BlockSpec((tq, 256), lambda b, t: (b * nt + t, P_NQ // 256)),
                  kspec(P_NK), kspec(P_NV), cspec, cspec,
                  pl.BlockSpec((4, NA_KH, GRID_W, NA_KH * GRID_W), lambda b, t: (0, 0, 0, 0))],
        out_specs=pl.BlockSpec((tq, 256), lambda b, t: (b * nt + t, 0)),
        out_shape=jax.ShapeDtypeStruct((T, 256), F32),
        compiler_params=_cp("arbitrary", "arbitrary"),
        name="lat_na_attention",
    )(proj, proj, proj, kc, vc, bias)


def _short_conv_kernel(a_ref, b_ref, c_ref, w_ref, bias_ref, oa_ref, ob_ref, oc_ref, *, L):
    t = lax.broadcasted_iota(jnp.int32, (L, HY_C), 0)
    for n, (x_ref, o_ref) in enumerate(((a_ref, oa_ref), (b_ref, ob_ref), (c_ref, oc_ref))):
        sl = slice(HY_C * n, HY_C * (n + 1))
        x = x_ref[...]
        prev = jnp.where(t == 0, 0.0, pltpu.roll(x, 1, axis=0))
        nxt = jnp.where(t == L - 1, 0.0, pltpu.roll(x, L - 1, axis=0))
        o_ref[...] = prev * w_ref[0:1, sl] + x * w_ref[1:2, sl] + nxt * w_ref[2:3, sl] + bias_ref[:, sl]


def _short_conv(proj, w, b, NB, Lb):
    T = proj.shape[0]
    spec = lambda c: pl.BlockSpec((Lb, HY_C), lambda i: (i, c))
    return pl.pallas_call(
        functools.partial(_short_conv_kernel, L=Lb),
        grid=(NB,),
        in_specs=[spec(P_HY // HY_C), spec(P_HY // HY_C + 1), spec(P_HY // HY_C + 2),
                  pl.BlockSpec((3, 3 * HY_C), lambda i: (0, 0)), pl.BlockSpec((1, 3 * HY_C), lambda i: (0, 0))],
        out_specs=[spec(0)] * 3,
        out_shape=[jax.ShapeDtypeStruct((T, HY_C), F32)] * 3,
        compiler_params=_cp("arbitrary"),
        name="hyena_short_conv",
    )(proj, proj, proj, w, b)


def _hy_filter_kernel(w1_ref, b1_ref, w2_ref, b2_ref, w3_ref, freq_ref, ld_ref, fs_ref, nyq_ref, *, L):
    ti = lax.broadcasted_iota(jnp.int32, (L, 128), 0)
    t = ti.astype(F32)
    j = lax.broadcasted_iota(jnp.int32, (L, 128), 1)
    band = jnp.where(j <= HY_BANDS, j - 1, j - 1 - HY_BANDS).astype(F32)
    ang = (2.0 * math.pi / L) * t * band
    tn = t / L
    z = jnp.where(j == 0, tn, jnp.where(j <= HY_BANDS, jnp.cos(ang),
                                        jnp.where(j <= 2 * HY_BANDS, -jnp.sin(ang), 0.0)))
    a = jnp.sin(freq_ref[0:1, :] * (_dot_hi(z, w1_ref[...]) + b1_ref[...]))
    a = jnp.sin(freq_ref[1:2, :] * (_dot_hi(a, w2_ref[...]) + b2_ref[...]))
    filt = _dot_hi(a, w3_ref[...])
    tcol = lax.broadcasted_iota(jnp.int32, (L, 4 * HY_C), 0)
    filt = filt * jnp.exp(-(tcol.astype(F32) / L) * jnp.exp(ld_ref[...]))
    t1 = lax.broadcasted_iota(jnp.int32, (L, HY_C), 0)
    sign = jnp.where(t1 % 2 == 0, 1.0, -1.0)
    for n in range(2):
        fwd = filt[:, 2 * HY_C * n:2 * HY_C * n + HY_C]
        bwd = jnp.where(t1 == 0, 0.0, filt[:, 2 * HY_C * n + HY_C:2 * HY_C * (n + 1)])
        tot = fwd + bwd
        fs_ref[:, HY_C * n:HY_C * (n + 1)] = tot
        fs_ref[:, 2 * HY_C + HY_C * n:2 * HY_C + HY_C * (n + 1)] = fwd - bwd
        nyq_ref[:, HY_C * n:HY_C * (n + 1)] = (tot * sign).sum(0, keepdims=True)


def _hy_filter(L, w1p, b1, w2, b2, w3, freq, ld):
    full = lambda s: pl.BlockSpec(s, lambda: tuple(0 for _ in s))
    return pl.pallas_call(
        functools.partial(_hy_filter_kernel, L=L),
        in_specs=[full((128, 64)), full((1, 64)), full((64, 64)), full((1, 64)), full((64, 4 * HY_C)),
                  full((2, 64)), full((1, 4 * HY_C))],
        out_specs=[full((L, 4 * HY_C)), full((1, 2 * HY_C))],
        out_shape=[jax.ShapeDtypeStruct((L, 4 * HY_C), F32), jax.ShapeDtypeStruct((1, 2 * HY_C), F32)],
        compiler_params=pltpu.CompilerParams(vmem_limit_bytes=VMEM_LIMIT),
        name="hyena_filter",
    )(w1p, b1, w2, b2, w3, freq, ld)


def _hy_gdft_kernel(cm_ref, sm_ref, fs_ref, nyq_ref, gr_ref, gi_ref, *, tm):
    m = pl.program_id(0)
    f = fs_ref[...].astype(BF16)
    gr_ref[...] = _dot(cm_ref[...], f[:, :2 * HY_C])
    gi = _dot(sm_ref[...], f[:, 2 * HY_C:])
    row = m * tm + lax.broadcasted_iota(jnp.int32, (tm, 2 * HY_C), 0)
    gi_ref[...] = jnp.where(row == 0, nyq_ref[...], gi)


def _hy_gdft(cm, sm, fs, nyq, L, tm):
    return pl.pallas_call(
        functools.partial(_hy_gdft_kernel, tm=tm),
        grid=(L // tm,),
        in_specs=[pl.BlockSpec((tm, L), lambda m: (m, 0)), pl.BlockSpec((tm, L), lambda m: (m, 0)),
                  pl.BlockSpec((L, 4 * HY_C), lambda m: (0, 0)), pl.BlockSpec((1, 2 * HY_C), lambda m: (0, 0))],
        out_specs=[pl.BlockSpec((tm, 2 * HY_C), lambda m: (m, 0))] * 2,
        out_shape=[jax.ShapeDtypeStruct((L, 2 * HY_C), F32)] * 2,
        compiler_params=_cp("arbitrary"),
        name="hyena_filter_dft",
    )(cm, sm, fs, nyq)


def _hy_fwd_kernel(cm_ref, sm_ref, z_ref, gr_ref, gi_ref, yr_ref, yi_ref, *, L, tm, ns):
    m = pl.program_id(1)
    gr = gr_ref[...]
    gi = gi_ref[...]
    row0 = (m * tm + lax.broadcasted_iota(jnp.int32, (tm, HY_C), 0)) == 0
    s = jnp.where(row0, 0.5 / L, 1.0 / L)
    for g in range(ns):
        zb = z_ref[g * L:(g + 1) * L, :].astype(BF16)
        zr = _dot(cm_ref[...], zb)
        zi = _dot(sm_ref[...], zb)
        zigi = zi * gi
        yr_ref[g * tm:(g + 1) * tm, :] = ((zr * gr - jnp.where(row0, 0.0, zigi)) * s).astype(BF16)
        yi_ref[g * tm:(g + 1) * tm, :] = (jnp.where(row0, zigi, zr * gi + zi * gr) * s).astype(BF16)


def _hy_seqs_per_step(NB, Lb, tm):
    ns = max(1, 2048 // Lb) if tm == Lb else 1
    while NB % ns:
        ns //= 2
    return ns


def _hy_fwd(cm, sm, z, gr, gi, n, NB, Lb, tm):
    T = z.shape[0]
    nm = Lb // tm
    ns = _hy_seqs_per_step(NB, Lb, tm)
    return pl.pallas_call(
        functools.partial(_hy_fwd_kernel, L=Lb, tm=tm, ns=ns),
        grid=(NB // ns, nm),
        in_specs=[pl.BlockSpec((tm, Lb), lambda b, m: (m, 0)), pl.BlockSpec((tm, Lb), lambda b, m: (m, 0)),
                  pl.BlockSpec((ns * Lb, HY_C), lambda b, m: (b, 0)),
                  pl.BlockSpec((tm, HY_C), lambda b, m: (m, n)), pl.BlockSpec((tm, HY_C), lambda b, m: (m, n))],
        out_specs=[pl.BlockSpec((ns * tm, HY_C), lambda b, m: (b * nm + m, 0))] * 2,
        out_shape=[jax.ShapeDtypeStruct((T, HY_C), BF16)] * 2,
        compiler_params=_cp("arbitrary", "arbitrary"),
        name="hyena_fwd_dft",
    )(cm, sm, z, gr, gi)


def _hy_inv_kernel(cm_ref, smt_ref, yr_ref, yi_ref, z_ref, g_ref, skip_ref, o_ref, *, L, tm, ns):
    for g in range(ns):
        seq = slice(g * L, (g + 1) * L)
        out = slice(g * tm, (g + 1) * tm)
        conv = _dot(cm_ref[...], yr_ref[seq, :]) + _dot(smt_ref[...], yi_ref[seq, :])
        o_ref[out, :] = g_ref[out, :] * (conv + skip_ref[...] * z_ref[out, :])


def _hy_inv(cm, smt, yr, yi, z, gate, skip, n, NB, Lb, tm):
    T = z.shape[0]
    nm = Lb // tm
    ns = _hy_seqs_per_step(NB, Lb, tm)
    tile = pl.BlockSpec((ns * tm, HY_C), lambda b, m: (b * nm + m, 0))
    seq = pl.BlockSpec((ns * Lb, HY_C), lambda b, m: (b, 0))
    return pl.pallas_call(
        functools.partial(_hy_inv_kernel, L=Lb, tm=tm, ns=ns),
        grid=(NB // ns, nm),
        in_specs=[pl.BlockSpec((tm, Lb), lambda b, m: (m, 0)), pl.BlockSpec((tm, Lb), lambda b, m: (m, 0)),
                  seq, seq, tile, tile, pl.BlockSpec((None, 1, HY_C), lambda b, m: (n, 0, 0))],
        out_specs=tile,
        out_shape=jax.ShapeDtypeStruct((T, HY_C), F32),
        compiler_params=_cp("arbitrary", "arbitrary"),
        name="hyena_inv_dft",
    )(cm, smt, yr, yi, z, gate, skip)


def _dft_mats(L):
    k = jnp.arange(L, dtype=jnp.int32)
    blk = 64

    def trig(mult):
        ang = ((mult[:, None] * k[None, :]) % (2 * L)).astype(F32) * (math.pi / L)
        return jnp.cos(ang), jnp.sin(ang)

    ca, sa = trig(jnp.arange(L // blk, dtype=jnp.int32) * blk)
    cb, sb = trig(jnp.arange(blk, dtype=jnp.int32))
    cm = (ca[:, None, :] * cb[None] - sa[:, None, :] * sb[None]).reshape(L, L)
    s = -(sa[:, None, :] * cb[None] + ca[:, None, :] * sb[None]).reshape(L, L)
    alt = jnp.where(k % 2 == 0, 1.0, -1.0).astype(F32)
    sm = jnp.where(k[:, None] == 0, alt[None, :], s)
    smt = jnp.where(k[None, :] == 0, alt[:, None], s)
    return cm.astype(BF16), sm.astype(BF16), smt.astype(BF16)


def _merge_kernel(oa_ref, ob_ref, oc_ref, od_ref, g0_ref, g1_ref, g2_ref, g3_ref, wb_ref, wo_ref, x_ref, mod_ref,
                  lg_ref, lb_ref, x1_ref, h2_ref, hp_ref):
    acc = None
    for o_ref, g_ref, i in ((oa_ref, g0_ref, 0), (ob_ref, g1_ref, 1), (oc_ref, g2_ref, 2), (od_ref, g3_ref, 3)):
        y = _sigmoid(g_ref[...].astype(F32)) * _dot(o_ref[...].astype(BF16), wb_ref[i])
        acc = y if acc is None else acc + y
    mix = _dot(acc.astype(BF16), wo_ref[...])
    m = mod_ref[...]
    x1 = _layer_norm(DN_ALPHA * x_ref[...] + m[2:3, :] * mix, lg_ref[...], lb_ref[...])
    x1_ref[...] = x1
    h2 = x1 * (1.0 + m[4:5, :]) + m[3:4, :]
    h2_ref[...] = h2
    hp_ref[...] = _pack_pairs(h2)


def _merge(outs, gates, wb, wo, x, mod, lg, lb, mod_row, bm):
    T = x.shape[0]
    row = lambda w: pl.BlockSpec((bm, w), lambda i: (i, 0))
    gspec = lambda n: pl.BlockSpec((bm, D), lambda i: (i, n))
    fixed2 = lambda s: pl.BlockSpec(s, lambda i: (0, 0))
    return pl.pallas_call(
        _merge_kernel,
        grid=(T // bm,),
        in_specs=[row(256)] * 4 + [gspec(0), gspec(1), gspec(2), gspec(3),
                                   pl.BlockSpec((4, 256, D), lambda i: (0, 0, 0)), fixed2((D, D)), row(D),
                                   pl.BlockSpec((None, 6, D), lambda i: (mod_row(i), 0, 0)),
                                   fixed2((1, D)), fixed2((1, D))],
        out_specs=[row(D), row(D), row(D // 2)],
        out_shape=[jax.ShapeDtypeStruct((T, D), F32)] * 2 + [jax.ShapeDtypeStruct((T, D // 2), jnp.int32)],
        compiler_params=_cp("arbitrary"),
        name="merge_norm",
    )(*outs, gates, gates, gates, gates, wb, wo, x, mod, lg, lb)


def _router_kernel(h_ref, rt_ref, bias_ref, g_ref, rank_ref, cnt_ref, *, tt):
    per = N_EXPERTS // N_GROUPS
    logits = lax.dot_general(rt_ref[...], h_ref[...], (((1,), (1,)), ((), ())), preferred_element_type=F32,
                             precision=lax.Precision.HIGHEST)
    scores = _sigmoid(logits)
    sel = (scores + bias_ref[...]).reshape(N_GROUPS, per, tt)
    gid = lax.broadcasted_iota(jnp.int32, (N_GROUPS, per, tt), 0).astype(F32)
    jid = lax.broadcasted_iota(jnp.int32, (N_GROUPS, per, tt), 1).astype(F32)
    eid = gid * per + jid
    ninf = -jnp.inf
    m1 = sel.max(1, keepdims=True)
    i1 = jnp.where(sel == m1, jid, float(per)).min(1, keepdims=True)
    m2 = jnp.where(jid == i1, ninf, sel).max(1, keepdims=True)
    gs = m1 + m2
    g1 = lax.broadcasted_iota(jnp.int32, (N_GROUPS, 1, tt), 0).astype(F32)
    chosen = jnp.zeros((N_GROUPS, 1, tt), F32)
    for _ in range(TOPK_GROUPS):
        mx = gs.max(0, keepdims=True)
        gi = jnp.where(gs == mx, g1, float(N_GROUPS)).min(0, keepdims=True)
        pick = g1 == gi
        chosen = jnp.where(pick, 1.0, chosen)
        gs = jnp.where(pick, ninf, gs)
    cand = jnp.where(chosen > 0.0, sel, NEG)
    picked = jnp.zeros((N_GROUPS, per, tt), F32)
    for _ in range(TOP_K):
        mx = cand.max(1, keepdims=True).max(0, keepdims=True)
        ei = jnp.where(cand == mx, eid, float(N_EXPERTS)).min(1, keepdims=True).min(0, keepdims=True)
        pick = eid == ei
        picked = jnp.where(pick, 1.0, picked)
        cand = jnp.where(pick, ninf, cand)
    w = scores.reshape(N_GROUPS, per, tt) * picked
    wsum = w.sum(1, keepdims=True).sum(0, keepdims=True)
    g_ref[...] = (w / wsum * ROUTED_SCALE).reshape(N_EXPERTS, tt)
    pk = picked.reshape(N_EXPERTS, tt)
    t_in = lax.broadcasted_iota(jnp.int32, (tt, tt), 0)
    t_out = lax.broadcasted_iota(jnp.int32, (tt, tt), 1)
    upper = jnp.where(t_in <= t_out, 1.0, 0.0).astype(BF16)

    @pl.when(pl.program_id(0) == 0)
    def _():
        cnt_ref[...] = jnp.zeros_like(cnt_ref)

    before = cnt_ref[:, 0:1]
    rank_ref[...] = jnp.where(pk > 0.0, before + _dot(pk.astype(BF16), upper) - 1.0, -1.0)
    cnt_ref[...] += pk.sum(-1, keepdims=True)


def _router(h2, router_t, bias, tt):
    T = h2.shape[0]
    tile = pl.BlockSpec((N_EXPERTS, tt), lambda i: (0, i))
    return pl.pallas_call(
        functools.partial(_router_kernel, tt=tt),
        grid=(T // tt,),
        in_specs=[pl.BlockSpec((tt, D), lambda i: (i, 0)), pl.BlockSpec((N_EXPERTS, D), lambda i: (0, 0)),
                  pl.BlockSpec((N_EXPERTS, 1), lambda i: (0, 0))],
        out_specs=[tile, tile, pl.BlockSpec((N_EXPERTS, 128), lambda i: (0, 0))],
        out_shape=[jax.ShapeDtypeStruct((N_EXPERTS, T), F32), jax.ShapeDtypeStruct((N_EXPERTS, T), F32),
                   jax.ShapeDtypeStruct((N_EXPERTS, 128), F32)],
        compiler_params=_cp("arbitrary"),
        name="moe_router",
    )(h2, router_t, bias)


def _route_pos_kernel(gate_ref, rank_ref, cnt_ref, pos_ref, w_ref, te_ref, nx_ref, nt_ref, *, tm, nt_max):
    ei = lax.broadcasted_iota(jnp.int32, (N_EXPERTS, N_EXPERTS), 0)
    ej = lax.broadcasted_iota(jnp.int32, (N_EXPERTS, N_EXPERTS), 1)
    below = jnp.where(ej < ei, 1.0, 0.0)
    padded = jnp.ceil(cnt_ref[...] * (1.0 / tm)) * tm
    offs = _dot_hi(below, padded)
    rank = rank_ref[...]
    routed = rank >= 0.0
    pos = offs[:, 0:1] + rank
    slot = _dot(below.astype(BF16), jnp.where(routed, 1.0, 0.0).astype(BF16))
    gate = gate_ref[...]
    for k in range(TOP_K):
        mine = routed & (slot == float(k))
        pos_ref[k:k + 1, :] = jnp.where(mine, pos, 0.0).sum(0, keepdims=True).astype(jnp.int32)
        w_ref[k:k + 1, :] = jnp.where(mine, gate, 0.0).sum(0, keepdims=True)
    ends = (offs + padded)[:, 0:1]
    first = (lax.broadcasted_iota(jnp.int32, (N_EXPERTS, nt_max), 1) * tm).astype(F32)
    te = jnp.minimum(jnp.where(ends <= first, 1.0, 0.0).sum(0, keepdims=True), N_EXPERTS - 1.0)
    te_ref[...] = te.astype(jnp.int32)
    eid = lax.broadcasted_iota(jnp.int32, (N_EXPERTS, nt_max), 0).astype(F32)
    nx_ref[...] = (jnp.where(eid == te, ends, 0.0).sum(0, keepdims=True) * (1.0 / tm)).astype(jnp.int32)
    nt_ref[...] = (padded.sum(0, keepdims=True) * (1.0 / tm)).astype(jnp.int32)


def _route_pos(gate_t, rank, cnt, tt, tm, nt_max):
    T = gate_t.shape[1]
    tile = pl.BlockSpec((N_EXPERTS, tt), lambda i: (0, i))
    out = pl.BlockSpec((TOP_K, tt), lambda i: (0, i))
    return pl.pallas_call(
        functools.partial(_route_pos_kernel, tm=tm, nt_max=nt_max),
        grid=(T // tt,),
        in_specs=[tile, tile, pl.BlockSpec((N_EXPERTS, 128), lambda i: (0, 0))],
        out_specs=[out, out, pl.BlockSpec((1, nt_max), lambda i: (0, 0)), pl.BlockSpec((1, nt_max), lambda i: (0, 0)),
                   pl.BlockSpec((1, 128), lambda i: (0, 0))],
        out_shape=[jax.ShapeDtypeStruct((TOP_K, T), jnp.int32), jax.ShapeDtypeStruct((TOP_K, T), F32),
                   jax.ShapeDtypeStruct((1, nt_max), jnp.int32), jax.ShapeDtypeStruct((1, nt_max), jnp.int32),
                   jax.ShapeDtypeStruct((1, 128), jnp.int32)],
        compiler_params=_cp("arbitrary"),
        name="moe_positions",
    )(gate_t, rank, cnt)


def _gmm_kernel(te_ref, nx_ref, nt_ref, xs_ref, w1_hbm, w3_hbm, w2_hbm, ys_ref, b1_ref, b3_ref, b2_ref,
                f1_ref, f3_ref, f2_ref, seg_ref, sem, *, l):
    j = pl.program_id(0)
    live = j < nt_ref[0]
    new_expert = (j == 0) | (te_ref[j] != te_ref[jnp.maximum(j - 1, 0)])

    def fetch(e, slot):
        return [pltpu.make_async_copy(w_hbm.at[l, e], f_ref.at[slot], sem.at[i, slot])
                for i, (w_hbm, f_ref) in enumerate(((w1_hbm, f1_ref), (w3_hbm, f3_ref), (w2_hbm, f2_ref)))]

    @pl.when(live & new_expert)
    def _():
        @pl.when(j == 0)
        def _():
            seg_ref[0] = 0
            for c in fetch(te_ref[0], 0):
                c.start()

        slot = lax.rem(seg_ref[0], 2)
        for c in fetch(te_ref[j], slot):
            c.wait()
        b1_ref[...] = f1_ref[slot].astype(BF16)
        b3_ref[...] = f3_ref[slot].astype(BF16)
        b2_ref[...] = f2_ref[slot].astype(BF16)
        nxt = nx_ref[j]

        @pl.when(nxt < nt_ref[0])
        def _():
            for c in fetch(te_ref[nxt], 1 - slot):
                c.start()

        seg_ref[0] = seg_ref[0] + 1

    @pl.when(live)
    def _():
        xa, xb = _unpack_pairs(xs_ref[...])
        xa, xb = xa.astype(BF16), xb.astype(BF16)
        half = D // 2
        a = _dot(xa, b1_ref[:half, :]) + _dot(xb, b1_ref[half:, :])
        b = _dot(xa, b3_ref[:half, :]) + _dot(xb, b3_ref[half:, :])
        hid = (a * _sigmoid(a) * b).astype(BF16)
        ys_ref[...] = _pack_pairs(_dot(hid, b2_ref[...]))


def _gmm(te, nx, nt, xs, w1, w3, w2, l, tm):
    n_slots = xs.shape[0]
    ds = D_EXPERT
    rows = pl.BlockSpec((tm, D // 2), lambda j, te, nx, nt: (jnp.minimum(j, nt[0] - 1), 0))
    hbm = pl.BlockSpec(memory_space=pl.ANY)
    return pl.pallas_call(
        functools.partial(_gmm_kernel, l=l),
        grid_spec=pltpu.PrefetchScalarGridSpec(
            num_scalar_prefetch=3,
            grid=(n_slots // tm,),
            in_specs=[rows, hbm, hbm, hbm],
            out_specs=rows,
            scratch_shapes=[pltpu.VMEM((D, ds), BF16), pltpu.VMEM((D, ds), BF16), pltpu.VMEM((ds, D), BF16),
                            pltpu.VMEM((2, D, ds), F32), pltpu.VMEM((2, D, ds), F32), pltpu.VMEM((2, ds, D), F32),
                            pltpu.SMEM((1,), jnp.int32), pltpu.SemaphoreType.DMA((3, 2))]),
        out_shape=jax.ShapeDtypeStruct((n_slots, D // 2), jnp.int32),
        compiler_params=_cp("arbitrary"),
        name="moe_grouped_ffn",
    )(te, nx, nt, xs, w1, w3, w2)


def _combine_kernel(yk_ref, w_ref, hp_ref, s1_ref, s3_ref, s2_ref, x_ref, mod_ref, lg_ref, lb_ref, o_ref):
    w = w_ref[...]
    acc_a = acc_b = None
    for k in range(TOP_K):
        ya, yb = _unpack_pairs(yk_ref[k])
        wk = w[:, k:k + 1]
        acc_a = wk * ya if acc_a is None else acc_a + wk * ya
        acc_b = wk * yb if acc_b is None else acc_b + wk * yb
    ha, hb = _unpack_pairs(hp_ref[...])
    ha, hb = ha.astype(BF16), hb.astype(BF16)
    half = D // 2
    a = _dot(ha, s1_ref[:half, :]) + _dot(hb, s1_ref[half:, :])
    b = _dot(ha, s3_ref[:half, :]) + _dot(hb, s3_ref[half:, :])
    y = jnp.concatenate([acc_a, acc_b], axis=1) + _dot((a * _sigmoid(a) * b).astype(BF16), s2_ref[...])
    m = mod_ref[...]
    o_ref[...] = _layer_norm(DN_ALPHA * x_ref[...] + m[5:6, :] * y, lg_ref[...], lb_ref[...])


def _combine(yk, w, hp, s1, s3, s2, x1, mod, lg, lb, mod_row, bm):
    T = x1.shape[0]
    ds = D_EXPERT
    row = lambda n: pl.BlockSpec((bm, n), lambda i: (i, 0))
    fixed = lambda s: pl.BlockSpec(s, lambda i: (0, 0))
    return pl.pallas_call(
        _combine_kernel,
        grid=(T // bm,),
        in_specs=[pl.BlockSpec((TOP_K, bm, D // 2), lambda i: (0, i, 0)), row(TOP_K), row(D // 2),
                  fixed((D, ds)), fixed((D, ds)), fixed((ds, D)), row(D),
                  pl.BlockSpec((None, 6, D), lambda i: (mod_row(i), 0, 0)), fixed((1, D)), fixed((1, D))],
        out_specs=row(D),
        out_shape=jax.ShapeDtypeStruct((T, D), F32),
        compiler_params=_cp("arbitrary"),
        name="moe_combine_norm",
    )(yk, w, hp, s1, s3, s2, x1, mod, lg, lb)


def _sc_worker():
    return lax.axis_index("s") * SC_CORES + lax.axis_index("c")


def _sc_mesh():
    return plsc.VectorSubcoreMesh(core_axis_name="c", subcore_axis_name="s")


def _sc_gather(table, idx):
    N, W = idx.shape[0], table.shape[1]
    per_w = N // SC_WORKERS
    n_chunks = per_w // SC_ROWS

    def body(table_hbm, idx_hbm, out_hbm, idx_v, rows_v, sem):
        base = _sc_worker() * per_w
        pltpu.sync_copy(idx_hbm.at[pl.ds(base, per_w)], idx_v)

        @pl.loop(0, n_chunks)
        def _(c):
            off = pl.multiple_of(c * SC_ROWS, SC_ROWS)
            pltpu.async_copy(table_hbm.at[idx_v.at[pl.ds(off, SC_ROWS)]], rows_v, sem).wait()
            pltpu.sync_copy(rows_v, out_hbm.at[pl.ds(base + off, SC_ROWS)])

    return pl.kernel(
        body, out_type=jax.ShapeDtypeStruct((N, W), table.dtype), mesh=_sc_mesh(),
        scratch_types=[pltpu.VMEM((per_w,), jnp.int32), pltpu.VMEM((SC_ROWS, W), table.dtype),
                       pltpu.SemaphoreType.DMA],
        name="sc_gather",
    )(table, idx)


def _sc_dispatch(pos, table, n_slots):
    NP, (T, W) = pos.shape[0], table.shape
    per_w = n_slots // SC_WORKERS
    n_chunks = per_w // SC_ROWS
    scan = 8192

    def body(pos_hbm, table_hbm, out_hbm, pos_v, src_v, rows_v, sem):
        base = _sc_worker() * per_w
        lane = lax.iota(jnp.int32, SC_LANES)

        @pl.loop(0, per_w // SC_LANES)
        def _(j):
            o = pl.multiple_of(j * SC_LANES, SC_LANES)
            src_v[pl.ds(o, SC_LANES)] = (base + o + lane) & (T - 1)

        @pl.loop(0, NP // scan)
        def _(c):
            pltpu.sync_copy(pos_hbm.at[pl.ds(pl.multiple_of(c * scan, scan), scan)], pos_v)

            @pl.loop(0, scan // SC_LANES)
            def _(v):
                o = pl.multiple_of(v * SC_LANES, SC_LANES)
                p = pos_v[pl.ds(o, SC_LANES)] - base
                mine = (p >= 0) & (p < per_w)
                tok = (c * scan + o + lane) & (T - 1)
                plsc.store_scatter(src_v, [jnp.where(mine, p, 0)], tok, mask=mine)

        @pl.loop(0, n_chunks)
        def _(c):
            off = pl.multiple_of(c * SC_ROWS, SC_ROWS)
            pltpu.async_copy(table_hbm.at[src_v.at[pl.ds(off, SC_ROWS)]], rows_v, sem).wait()
            pltpu.sync_copy(rows_v, out_hbm.at[pl.ds(base + off, SC_ROWS)])

    return pl.kernel(
        body, out_type=jax.ShapeDtypeStruct((n_slots, W), table.dtype), mesh=_sc_mesh(),
        scratch_types=[pltpu.VMEM((scan,), jnp.int32), pltpu.VMEM((per_w,), jnp.int32),
                       pltpu.VMEM((SC_ROWS, W), table.dtype), pltpu.SemaphoreType.DMA],
        compiler_params=pltpu.CompilerParams(needs_layout_passes=False),
        name="sc_dispatch",
    )(pos, table)


def _caches_kernel(*refs):
    n_in = 6 * DEPTH
    outs = refs[n_in:]
    l = pl.program_id(0)
    for a in range(DEPTH):
        @pl.when(l == a)
        def _(a=a):
            ckv, kpe, wk, wv, nk, nv = refs[6 * a:6 * (a + 1)]
            outs[0][...] = ckv[...]
            outs[1][...] = kpe[:, 64:96]
            outs[2][...] = wk[...]
            outs[3][...] = wv[...]
            outs[4][...] = nk[...]
            outs[5][...] = nv[...]


def _emit_caches(projs, ckvs, B, S):
    def layer_specs(a):
        row = lambda l, b: jnp.where(l == a, b, 0)
        col = lambda w, off: pl.BlockSpec((S, w), lambda l, b: (row(l, b), off // w))
        return [pl.BlockSpec((S, 128), lambda l, b: (row(l, b), 0)), col(128, P_KPE), col(128, P_WK), col(128, P_WV),
                col(256, P_NK), col(256, P_NV)]

    in_specs, args = [], []
    for a in range(DEPTH):
        in_specs += layer_specs(a)
        args += [ckvs[a]] + [projs[a]] * 5
    widths = (128, 32, 128, 128, 256, 256)
    return pl.pallas_call(
        _caches_kernel,
        grid=(DEPTH, B),
        in_specs=in_specs,
        out_specs=[pl.BlockSpec((None, None, S, w), lambda l, b: (b, l, 0, 0)) for w in widths],
        out_shape=[jax.ShapeDtypeStruct((B, DEPTH, S, w), F32) for w in widths],
        compiler_params=_cp("arbitrary", "arbitrary"),
        name="context_tensors",
    )(*args)


def _rot_cols(w, q):
    a, b, c, d = w[..., :q], w[..., q:2 * q], w[..., 2 * q:3 * q], w[..., 3 * q:]
    return jnp.concatenate([-b, a, -d, c], -1)


def _prep_w_in(w):
    z = lambda n: jnp.zeros((D, n), w.dtype)
    qlat, ckv, kpe, hy = w[:, 0:256], w[:, 256:384], w[:, 384:416], w[:, 416:1184]
    wq, wk, wv = w[:, 1184:1440], w[:, 1440:1568], w[:, 1568:1696]
    nq, nk, nv, gate = w[:, 1696:1952], w[:, 1952:2208], w[:, 2208:2464], w[:, 2464:]
    wq_r = _rot_cols(wq.reshape(D, 4, 64), 16).reshape(D, 256)
    wk_r = _rot_cols(wk.reshape(D, 2, 64), 16).reshape(D, 128)
    kpe_r = _rot_cols(kpe, 8)
    cols = [qlat, ckv, z(64), kpe, z(32), hy, wq, wk, wv, nq, nk, nv, wq_r, wk_r, z(64), kpe_r, z(32), gate]
    return jnp.concatenate(cols, 1).astype(BF16)


def _prep_mla(w_uq, w_ukv):
    uq = w_uq.reshape(256, 4, 96)
    nope, pe = uq[..., :64], uq[..., 64:]
    z32 = jnp.zeros((256, 4, 32), w_uq.dtype)
    z64 = jnp.zeros((256, 4, 64), w_uq.dtype)
    wcat = jnp.concatenate([nope, pe, z32], -1).reshape(256, 512).astype(BF16)
    wrot = jnp.concatenate([z64, _rot_cols(pe, 8), z32], -1).reshape(256, 512).astype(BF16)
    ukv = w_ukv.reshape(128, 4, 128)
    wk = jnp.concatenate([ukv[..., :64], jnp.zeros((128, 4, 64), w_ukv.dtype)], -1).reshape(128, 512).astype(BF16)
    wv = ukv[..., 64:].reshape(128, 256).astype(BF16)
    return wcat, wrot, wk, wv


def _rope_tab(L, q):
    t = jnp.arange(L)
    inv = ROPE_BASE ** (-jnp.arange(q, dtype=F32) / q)
    ar = (t // GRID_W).astype(F32)[:, None] * inv[None, :]
    ac = (t % GRID_W).astype(F32)[:, None] * inv[None, :]
    cos = jnp.concatenate([jnp.cos(ar), jnp.cos(ar), jnp.cos(ac), jnp.cos(ac)], 1)
    sin = jnp.concatenate([jnp.sin(ar), jnp.sin(ar), jnp.sin(ac), jnp.sin(ac)], 1)
    return cos, sin


def _rope_tables(L):
    c8, s8 = _rope_tab(L, 8)
    c16, s16 = _rope_tab(L, 16)
    one, zero = jnp.ones((L, 64), F32), jnp.zeros((L, 64), F32)
    z32 = jnp.zeros((L, 32), F32)
    mla_q = (jnp.tile(jnp.concatenate([one, c8, z32], 1), (1, 4)), jnp.tile(jnp.concatenate([zero, s8, z32], 1), (1, 4)))
    mla_k = (jnp.concatenate([zero, c8, z32], 1), jnp.concatenate([zero, s8, z32], 1))
    win = (jnp.tile(c16, (1, 4)), jnp.tile(s16, (1, 4)), jnp.tile(c16, (1, 2)), jnp.tile(s16, (1, 2)))
    return mla_q + mla_k, win


def _hyena(proj, lp, dft, NB, Lb):
    cm, sm, smt = dft
    tm = min(Lb, 512)
    v, x1, x2 = _short_conv(proj, lp["hy_conv_w"], lp["hy_conv_b"].reshape(1, -1), NB, Lb)
    w1p = jnp.pad(lp["hy_w1"], ((0, 128 - lp["hy_w1"].shape[0]), (0, 0)))
    fs, nyq = _hy_filter(Lb, w1p, lp["hy_b1"].reshape(1, -1), lp["hy_w2"], lp["hy_b2"].reshape(1, -1), lp["hy_w3"],
                         lp["hy_sin_freq"], lp["hy_log_decay"].reshape(1, -1))
    gr, gi = _hy_gdft(cm, sm, fs, nyq, Lb, tm)
    skip = lp["hy_skip"].reshape(2, 1, HY_C)
    z = v
    for n, gate in enumerate((x1, x2)):
        yr, yi = _hy_fwd(cm, sm, z, gr, gi, n, NB, Lb, tm)
        z = _hy_inv(cm, smt, yr, yi, z, gate, skip, n, NB, Lb, tm)
    return z


def _layer(x, mod, lp, l, NB, Lb, mod_row_of_batch, dft, cache=None, tabs=None, na_bias=None):
    T = NB * Lb
    latent = cache is not None
    bm = 256
    rows_of = lambda n: (lambda i: mod_row_of_batch((i * n) // Lb))
    mod_row = rows_of(bm)
    span = Lb if latent else T
    bmp = min(span, 1024)
    proj = _in_proj(x, mod, lp["w_in_p"], rows_of(bmp), bmp, 0, P_GATE, F32)
    gates = _in_proj(x, mod, lp["w_in_p"], rows_of(bmp), bmp, P_GATE, N_PROJ - P_GATE, BF16)

    gq, gkv = lp["mla_q_norm"].reshape(1, -1), lp["mla_kv_norm"].reshape(1, -1)
    wcat, wrot, wk, wv = lp["mla_w"]
    q_all, ckv_n, kpe_r = _mla_q(proj, gq, gkv, wcat, wrot, tabs[0] if latent else None, Lb, bm)
    if latent:
        ckv_c, kpe_c, kc_c, vc_c, kd_c, vd_c = cache
        Lc = ckv_c.shape[1]
        kpe_cp = jnp.pad(kpe_c, ((0, 0), (0, 0), (64, 32)))
        ckv_all = jnp.concatenate([ckv_c, ckv_n.reshape(NB, Lb, 128)], 1).reshape(NB * (Lc + Lb), 128)
        kpe_all = jnp.concatenate([kpe_cp, kpe_r.reshape(NB, Lb, 128)], 1).reshape(NB * (Lc + Lb), 128)
        k_all, v_all = _mla_kv(ckv_all, kpe_all, wk, wv, 512)
        oa = _lat_mla_attention(q_all, k_all, v_all, NB, Lb, Lc + Lb, 256)
        oc = _lat_win_attention(proj, kc_c.reshape(NB, Lc, 128), vc_c.reshape(NB, Lc, 128), tabs[1],
                                lp["win_sink"], NB, Lb)
        od = _lat_na_attention(proj, kd_c.reshape(NB, Lc, 256), vd_c.reshape(NB, Lc, 256), na_bias, NB, Lb)
    else:
        k_all, v_all = _mla_kv(ckv_n, kpe_r, wk, wv, 512)
        oa, oc, od = _ctx_attention(proj, q_all, k_all, v_all, lp["win_sink"], NB, Lb)
    ob = _hyena(proj, lp, dft, NB, Lb)

    bmm = min(span, 512)
    x1, h2, hp = _merge((oa, ob, oc, od), gates, lp["w_branch_b"], lp["w_out_b"], x, mod,
                        lp["ln1_g"].reshape(1, -1), lp["ln1_b"].reshape(1, -1), rows_of(bmm), bmm)
    n_slots = T * TOP_K + N_EXPERTS * MOE_TM
    gate_t, rank, cnt = _router(h2, lp["moe_router"].T, lp["moe_bias"].reshape(-1, 1), 512)
    pos, w8, te, nx, nt = _route_pos(gate_t, rank, cnt, 512, MOE_TM, n_slots // MOE_TM)
    xs = _sc_dispatch(pos.reshape(-1), hp, n_slots)
    ys = _gmm(te.reshape(-1), nx.reshape(-1), nt.reshape(-1)[:1], xs, lp["moe_w1"], lp["moe_w3"], lp["moe_w2"], l,
              MOE_TM)
    yk = _sc_gather(ys, pos.reshape(-1)).reshape(TOP_K, T, D // 2)
    x2 = _combine(yk, w8.T, hp, lp["sh_w1_b"], lp["sh_w3_b"], lp["sh_w2_b"], x1, mod,
                  lp["ln2_g"].reshape(1, -1), lp["ln2_b"].reshape(1, -1), mod_row, bm)
    return x2, (proj, ckv_n)


def kernel(x_prompt, x_sample, cache_mla_ckv, cache_mla_kpe, cache_win_k, cache_win_v, cache_na_k, cache_na_v, c, c_ctx, w_ada, b_ada, w_in, mla_q_norm, mla_kv_norm, mla_w_uq, mla_w_ukv, hy_conv_w, hy_conv_b, hy_w1, hy_b1, hy_w2, hy_b2, hy_w3, hy_sin_freq, hy_log_decay, hy_skip, win_sink, na_rpb, w_branch, w_out, ln1_g, ln1_b, ln2_g, ln2_b, moe_router, moe_bias, moe_w1, moe_w3, moe_w2, sh_w1, sh_w3, sh_w2):
    B, S, _ = x_prompt.shape
    DB, DS, _ = x_sample.shape
    xp = x_prompt.reshape(B * S, D)
    xs = x_sample.reshape(DB * DS, D)
    cvec = jnp.concatenate([c_ctx[None, :], c, jnp.zeros((8 - 1 - DB, D), F32)], 0)
    dft_ctx = _dft_mats(S)
    dft_lat = _dft_mats(DS)
    tabs = _rope_tables(DS)
    projs, ckvs = [], []
    for l in range(DEPTH):
        lp = dict(w_in_p=_prep_w_in(w_in[l]), mla_q_norm=mla_q_norm[l], mla_kv_norm=mla_kv_norm[l],
                  mla_w=_prep_mla(mla_w_uq[l], mla_w_ukv[l]), hy_conv_w=hy_conv_w[l], hy_conv_b=hy_conv_b[l],
                  hy_w1=hy_w1[l], hy_b1=hy_b1[l], hy_w2=hy_w2[l], hy_b2=hy_b2[l], hy_w3=hy_w3[l],
                  hy_sin_freq=hy_sin_freq[l], hy_log_decay=hy_log_decay[l], hy_skip=hy_skip[l],
                  win_sink=win_sink[l], w_branch_b=w_branch[l].astype(BF16), w_out_b=w_out[l].astype(BF16),
                  ln1_g=ln1_g[l], ln1_b=ln1_b[l], ln2_g=ln2_g[l], ln2_b=ln2_b[l],
                  moe_router=moe_router[l], moe_bias=moe_bias[l], moe_w1=moe_w1, moe_w3=moe_w3, moe_w2=moe_w2,
                  sh_w1_b=sh_w1[l].astype(BF16), sh_w3_b=sh_w3[l].astype(BF16), sh_w2_b=sh_w2[l].astype(BF16))
        mod = _modulation(cvec, w_ada, b_ada, l)
        xp, (proj, ckv_n) = _layer(xp, mod, lp, l, B, S, lambda b: 0, dft_ctx)
        projs.append(proj)
        ckvs.append(ckv_n)
        cache = (cache_mla_ckv[:, l], cache_mla_kpe[:, l], cache_win_k[:, l], cache_win_v[:, l],
                 cache_na_k[:, l], cache_na_v[:, l])
        xs, _ = _layer(xs, mod, lp, l, DB, DS, lambda b: 1 + b, dft_lat, cache=cache, tabs=tabs,
                       na_bias=_na_bias(na_rpb[l]))
    ckv, kpe, wk, wv, nk, nv = _emit_caches(projs, ckvs, B, S)
    heads = lambda t, h: t.reshape(B, DEPTH, S, h, HEAD_DIM)
    return (xp.reshape(B, S, D), xs.reshape(DB, DS, D), ckv, kpe, heads(wk, 2), heads(wv, 2), heads(nk, 4),
            heads(nv, 4))
```

```python
import functools
import math

import jax
import jax.numpy as jnp
from jax import lax
from jax.experimental import pallas as pl
from jax.experimental.pallas import tpu as pltpu
from jax.experimental.pallas import tpu_sc as plsc

F32 = jnp.float32
BF16 = jnp.bfloat16

D = 1024
DEPTH = 2
GRID_W = 64
HEAD_DIM = 64
MLA_SCALE = 96 ** -0.5
ATT_SCALE = HEAD_DIM ** -0.5
HY_C = 256
HY_BANDS = 8
NA_KH = 8
NA_KW = 16
N_EXPERTS = 64
N_GROUPS = 8
TOP_K = 8
TOPK_GROUPS = 4
D_EXPERT = 256
ROUTED_SCALE = 2.5
ROPE_BASE = 10000.0
LN_EPS = 1e-5
RMS_EPS = 1e-6
NEG = -1e30
DN_ALPHA = (2 * DEPTH) ** 0.25

P_QLAT, P_CKV, P_KPE, P_HY = 0, 256, 384, 512
P_WQ, P_WK, P_WV = 1280, 1536, 1664
P_NQ, P_NK, P_NV = 1792, 2048, 2304
P_WQR, P_WKR, P_KPER, P_GATE = 2560, 2816, 2944, 3072
N_PROJ = 7168

VMEM_LIMIT = 56 * 1024 * 1024

SC_CORES = 2
SC_SUBCORES = 16
SC_LANES = 16
SC_WORKERS = SC_CORES * SC_SUBCORES
SC_ROWS = 64

MOE_TM = 512

def _cp(*sem):
    return pltpu.CompilerParams(dimension_semantics=sem, vmem_limit_bytes=VMEM_LIMIT)


def _sigmoid(x):
    return 1.0 / (1.0 + jnp.exp(-x))


def _dot(a, b):
    return jnp.dot(a, b, preferred_element_type=F32)


def _dot_nt(a, b):
    return lax.dot_general(a, b, (((1,), (1,)), ((), ())), preferred_element_type=F32)


def _dot_hi(a, b):
    return jnp.dot(a, b, preferred_element_type=F32, precision=lax.Precision.HIGHEST)


def _pack_pairs(x):
    w = x.shape[1] // 2
    hi = lax.bitcast_convert_type(x[:, :w].astype(BF16).astype(F32), jnp.int32)
    lo = lax.bitcast_convert_type(x[:, w:].astype(BF16).astype(F32), jnp.int32)
    return hi | lax.shift_right_logical(lo, 16)


def _unpack_pairs(p):
    hi = lax.bitcast_convert_type(p & jnp.int32(-65536), F32)
    lo = lax.bitcast_convert_type(lax.shift_left(p, 16), F32)
    return hi, lo


def _layer_norm(x, g, b):
    mu = jnp.mean(x, -1, keepdims=True)
    xc = x - mu
    var = jnp.mean(xc * xc, -1, keepdims=True)
    return xc * lax.rsqrt(var + LN_EPS) * g + b


def _rms_norm(x, g):
    return x * lax.rsqrt(jnp.mean(x * x, -1, keepdims=True) + RMS_EPS) * g


def _mod_kernel(c_ref, w_ref, b_ref, o_ref):
    c = c_ref[...]
    a = (c * _sigmoid(c)).astype(BF16)
    o_ref[...] = _dot(a, w_ref[...].astype(BF16)) + b_ref[...]


def _modulation(cvec, w_ada, b_ada, l):
    out = pl.pallas_call(
        _mod_kernel,
        grid=(6,),
        in_specs=[pl.BlockSpec((8, D), lambda j: (0, 0)),
                  pl.BlockSpec((None, D, D), lambda j: (l, 0, j)),
                  pl.BlockSpec((None, 1, D), lambda j: (l, 0, j))],
        out_specs=pl.BlockSpec((8, D), lambda j: (0, j)),
        out_shape=jax.ShapeDtypeStruct((8, 6 * D), F32),
        compiler_params=_cp("arbitrary"),
        name="modulation",
    )(cvec, w_ada, b_ada.reshape(DEPTH, 1, 6 * D))
    return out.reshape(8, 6, D)


def _inproj_kernel(x_ref, mod_ref, w_ref, o_ref, g_ref, h_ref, *, n_main):
    j = pl.program_id(1)

    @pl.when(j == 0)
    def _():
        m = mod_ref[...]
        h_ref[...] = (x_ref[...] * (1.0 + m[1:2, :]) + m[0:1, :]).astype(BF16)

    y = _dot(h_ref[...], w_ref[...])

    @pl.when(j < n_main)
    def _():
        o_ref[...] = y

    @pl.when(j >= n_main)
    def _():
        g_ref[...] = y.astype(BF16)


def _in_proj(x, mod, w_p, mod_row, bm):
    T = x.shape[0]
    bn = 1024
    n_main = P_GATE // bn
    return pl.pallas_call(
        functools.partial(_inproj_kernel, n_main=n_main),
        grid=(T // bm, N_PROJ // bn),
        in_specs=[pl.BlockSpec((bm, D), lambda i, j: (i, 0)),
                  pl.BlockSpec((None, 6, D), lambda i, j: (mod_row(i), 0, 0)),
                  pl.BlockSpec((D, bn), lambda i, j: (0, j))],
        out_specs=[pl.BlockSpec((bm, bn), lambda i, j: (i, jnp.minimum(j, n_main - 1))),
                   pl.BlockSpec((bm, bn), lambda i, j: (i, jnp.maximum(j - n_main, 0)))],
        out_shape=[jax.ShapeDtypeStruct((T, P_GATE), F32), jax.ShapeDtypeStruct((T, N_PROJ - P_GATE), BF16)],
        scratch_shapes=[pltpu.VMEM((bm, D), BF16)],
        compiler_params=_cp("arbitrary", "arbitrary"),
        name="in_proj",
    )(x, mod, w_p)


def _mla_q_kernel(*refs, rope):
    if rope:
        (ql_ref, ckv_ref, kpe_ref, kper_ref, gq_ref, gkv_ref, wc_ref, wr_ref,
         cq_ref, sq_ref, ck_ref, sk_ref, q_ref, ckvn_ref, kpeo_ref) = refs
    else:
        ql_ref, ckv_ref, kpe_ref, gq_ref, gkv_ref, wc_ref, q_ref, ckvn_ref, kpeo_ref = refs
    qn = _rms_norm(ql_ref[...], gq_ref[...]).astype(BF16)
    q = _dot(qn, wc_ref[...])
    if rope:
        q = q * cq_ref[...] + _dot(qn, wr_ref[...]) * sq_ref[...]
        kpeo_ref[...] = kpe_ref[...] * ck_ref[...] + kper_ref[...] * sk_ref[...]
    else:
        kpeo_ref[...] = kpe_ref[...]
    q_ref[...] = (q * MLA_SCALE).astype(BF16)
    ckvn_ref[...] = _rms_norm(ckv_ref[...], gkv_ref[...])


def _mla_q(proj, gq, gkv, wcat, wrot, tabs, Lb, bm):
    T = proj.shape[0]
    rope = tabs is not None
    nl = Lb // bm
    col = lambda c: (lambda i: (i, c))
    fixed = lambda i: (0, 0)
    in_specs = [pl.BlockSpec((bm, 256), col(P_QLAT // 256)),
                pl.BlockSpec((bm, 128), col(P_CKV // 128)),
                pl.BlockSpec((bm, 128), col(P_KPE // 128))]
    args = [proj, proj, proj]
    if rope:
        in_specs.append(pl.BlockSpec((bm, 128), col(P_KPER // 128)))
        args.append(proj)
    in_specs += [pl.BlockSpec((1, 256), fixed), pl.BlockSpec((1, 128), fixed), pl.BlockSpec((256, 512), fixed)]
    args += [gq, gkv, wcat]
    if rope:
        cq, sq, ck, sk = tabs
        pos = lambda i: (i % nl, 0)
        in_specs += [pl.BlockSpec((256, 512), fixed), pl.BlockSpec((bm, 512), pos), pl.BlockSpec((bm, 512), pos),
                     pl.BlockSpec((bm, 128), pos), pl.BlockSpec((bm, 128), pos)]
        args += [wrot, cq, sq, ck, sk]
    return pl.pallas_call(
        functools.partial(_mla_q_kernel, rope=rope),
        grid=(T // bm,),
        in_specs=in_specs,
        out_specs=[pl.BlockSpec((bm, 512), lambda i: (i, 0)),
                   pl.BlockSpec((bm, 128), lambda i: (i, 0)),
                   pl.BlockSpec((bm, 128), lambda i: (i, 0))],
        out_shape=[jax.ShapeDtypeStruct((T, 512), BF16),
                   jax.ShapeDtypeStruct((T, 128), F32),
                   jax.ShapeDtypeStruct((T, 128), F32)],
        compiler_params=_cp("arbitrary"),
        name="mla_q",
    )(*args)


def _mla_kv_kernel(ckv_ref, kpe_ref, wk_ref, wv_ref, k_ref, v_ref):
    c = ckv_ref[...].astype(BF16)
    kpe = kpe_ref[...]
    k_ref[...] = (_dot(c, wk_ref[...]) + jnp.concatenate([kpe] * 4, axis=1)).astype(BF16)
    v_ref[...] = _dot(c, wv_ref[...]).astype(BF16)


def _mla_kv(ckv, kpe, wk, wv, bm):
    Tk = ckv.shape[0]
    return pl.pallas_call(
        _mla_kv_kernel,
        grid=(Tk // bm,),
        in_specs=[pl.BlockSpec((bm, 128), lambda i: (i, 0)),
                  pl.BlockSpec((bm, 128), lambda i: (i, 0)),
                  pl.BlockSpec((128, 512), lambda i: (0, 0)),
                  pl.BlockSpec((128, 256), lambda i: (0, 0))],
        out_specs=[pl.BlockSpec((bm, 512), lambda i: (i, 0)),
                   pl.BlockSpec((bm, 256), lambda i: (i, 0))],
        out_shape=[jax.ShapeDtypeStruct((Tk, 512), BF16),
                   jax.ShapeDtypeStruct((Tk, 256), BF16)],
        compiler_params=_cp("arbitrary"),
        name="mla_kv",
    )(ckv, kpe, wk, wv)


def _attn_core(q, kvs, masks, sink):
    ss = []
    for (k, _), mk in zip(kvs, masks):
        s = _dot_nt(q, k)
        if mk is not None:
            s = s + mk[1] if mk[0] == "add" else jnp.where(mk[1], s, NEG)
        ss.append(s)
    m = ss[0].max(-1, keepdims=True)
    for s in ss[1:]:
        m = jnp.maximum(m, s.max(-1, keepdims=True))
    if sink is not None:
        m = jnp.maximum(m, sink)
    den = None
    acc = None
    for s, (_, v) in zip(ss, kvs):
        p = jnp.exp(s - m)
        d = p.sum(-1, keepdims=True)
        a = _dot(p.astype(BF16), v)
        den = d if den is None else den + d
        acc = a if acc is None else acc + a
    if sink is not None:
        den = den + jnp.exp(sink - m)
    return acc / den


def _ctx_attn_kernel(qm_ref, km_ref, vm_ref, wq_ref, wk_ref, wv_ref, nq_ref, nk_ref, nv_ref, sink_ref,
                     om_ref, ow_ref, on_ref):
    for h in range(4):
        q = qm_ref[:, 128 * h:128 * (h + 1)]
        k = km_ref[:, 128 * h:128 * (h + 1)]
        v = vm_ref[:, 64 * h:64 * (h + 1)]
        om_ref[:, 64 * h:64 * (h + 1)] = _attn_core(q, [(k, v)], [None], None)
    for h in range(4):
        g = h // 2
        q = (wq_ref[:, 64 * h:64 * (h + 1)] * ATT_SCALE).astype(BF16)
        k = wk_ref[:, 64 * g:64 * (g + 1)].astype(BF16)
        v = wv_ref[:, 64 * g:64 * (g + 1)].astype(BF16)
        ow_ref[:, 64 * h:64 * (h + 1)] = _attn_core(q, [(k, v)], [None], sink_ref[h])
    for h in range(4):
        q = (nq_ref[:, 64 * h:64 * (h + 1)] * ATT_SCALE).astype(BF16)
        k = nk_ref[:, 64 * h:64 * (h + 1)].astype(BF16)
        v = nv_ref[:, 64 * h:64 * (h + 1)].astype(BF16)
        on_ref[:, 64 * h:64 * (h + 1)] = _attn_core(q, [(k, v)], [None], None)


def _ctx_attention(proj, q_all, k_all, v_all, sink, NB, Lb):
    T = proj.shape[0]
    pc = lambda w, off: pl.BlockSpec((Lb, w), lambda b: (b, off // w))
    row = lambda w: pl.BlockSpec((Lb, w), lambda b: (b, 0))
    return pl.pallas_call(
        _ctx_attn_kernel,
        grid=(NB,),
        in_specs=[row(512), row(512), row(256),
                  pc(256, P_WQ), pc(128, P_WK), pc(128, P_WV),
                  pc(256, P_NQ), pc(256, P_NK), pc(256, P_NV),
                  pl.BlockSpec(memory_space=pltpu.SMEM)],
        out_specs=[row(256), row(256), row(256)],
        out_shape=[jax.ShapeDtypeStruct((T, 256), F32)] * 3,
        compiler_params=_cp("arbitrary"),
        name="ctx_attention",
    )(q_all, k_all, v_all, proj, proj, proj, proj, proj, proj, sink)


def _lat_mla_kernel(q_ref, k_ref, v_ref, o_ref):
    for h in range(4):
        q = q_ref[:, 128 * h:128 * (h + 1)]
        k = k_ref[:, 128 * h:128 * (h + 1)]
        v = v_ref[:, 64 * h:64 * (h + 1)]
        o_ref[:, 64 * h:64 * (h + 1)] = _attn_core(q, [(k, v)], [None], None)


def _lat_mla_attention(q_all, k_all, v_all, NB, Lb, Lk, tq):
    T = q_all.shape[0]
    nq = Lb // tq
    return pl.pallas_call(
        _lat_mla_kernel,
        grid=(NB, nq),
        in_specs=[pl.BlockSpec((tq, 512), lambda b, i: (b * nq + i, 0)),
                  pl.BlockSpec((Lk, 512), lambda b, i: (b, 0)),
                  pl.BlockSpec((Lk, 256), lambda b, i: (b, 0))],
        out_specs=pl.BlockSpec((tq, 256), lambda b, i: (b * nq + i, 0)),
        out_shape=jax.ShapeDtypeStruct((T, 256), F32),
        compiler_params=_cp("arbitrary", "arbitrary"),
        name="lat_mla_attention",
    )(q_all, k_all, v_all)


def _attn_local_ctx(q, locs, kc, vc, sink):
    s_ctx = _dot_nt(q, kc)
    m_ctx = s_ctx.max(-1, keepdims=True)
    if sink is not None:
        m_ctx = jnp.maximum(m_ctx, sink)
    ms, dens, accs = [], [], []
    for rs, k, v, mk in locs:
        s = _dot_nt(q[rs], k)
        s = s + mk[1] if mk[0] == "add" else jnp.where(mk[1], s, NEG)
        m = jnp.maximum(s.max(-1, keepdims=True), m_ctx[rs])
        p = jnp.exp(s - m)
        ms.append(m)
        dens.append(p.sum(-1, keepdims=True))
        accs.append(_dot(p.astype(BF16), v))
    m = jnp.concatenate(ms, axis=0)
    p = jnp.exp(s_ctx - m)
    den = jnp.concatenate(dens, axis=0) + p.sum(-1, keepdims=True)
    if sink is not None:
        den = den + jnp.exp(sink - m)
    return (jnp.concatenate(accs, axis=0) + _dot(p.astype(BF16), vc)) / den


def _lat_win_kernel(q_ref, qr_ref, k_ref, kr_ref, v_ref, kc_ref, vc_ref, cq_ref, sq_ref, ck_ref, sk_ref,
                    sink_ref, o_ref, *, Lb, bpt):
    t = pl.program_id(1)
    q = (q_ref[...] * cq_ref[...] + qr_ref[...] * sq_ref[...]) * ATT_SCALE
    kc = kc_ref[...].astype(BF16)
    vc = vc_ref[...].astype(BF16)
    blocks = []
    for bb in range(bpt):
        i = t * bpt + bb
        start = pl.multiple_of(jnp.clip((i - 1) * 128, 0, Lb - 384), 128)
        win = pl.ds(start, 384)
        kk = (k_ref[win, :] * ck_ref[win, :] + kr_ref[win, :] * sk_ref[win, :]).astype(BF16)
        qpos = i * 128 + lax.broadcasted_iota(jnp.int32, (128, 384), 0)
        kpos = start + lax.broadcasted_iota(jnp.int32, (128, 384), 1)
        blocks.append((kk, v_ref[win, :].astype(BF16), jnp.abs(qpos - kpos) <= 128))
    for h in range(4):
        g = h // 2
        sl = slice(64 * g, 64 * (g + 1))
        locs = [(slice(128 * bb, 128 * (bb + 1)), kk[:, sl], vv[:, sl], ("keep", valid))
                for bb, (kk, vv, valid) in enumerate(blocks)]
        qh = q[:, 64 * h:64 * (h + 1)].astype(BF16)
        o_ref[:, 64 * h:64 * (h + 1)] = _attn_local_ctx(qh, locs, kc[:, sl], vc[:, sl], sink_ref[h])


def _lat_win_attention(proj, kc, vc, tabs, sink, NB, Lb):
    T = proj.shape[0]
    bpt = 1
    tq = 128 * bpt
    nt = Lb // tq
    Lc = kc.shape[1]
    cq, sq, ck, sk = tabs
    qspec = lambda off: pl.BlockSpec((tq, 256), lambda b, i: (b * nt + i, off // 256))
    kspec = lambda off: pl.BlockSpec((Lb, 128), lambda b, i: (b, off // 128))
    cspec = pl.BlockSpec((None, Lc, 128), lambda b, i: (b, 0, 0))
    return pl.pallas_call(
        functools.partial(_lat_win_kernel, Lb=Lb, bpt=bpt),
        grid=(NB, nt),
        in_specs=[qspec(P_WQ), qspec(P_WQR), kspec(P_WK), kspec(P_WKR), kspec(P_WV), cspec, cspec,
                  pl.BlockSpec((tq, 256), lambda b, i: (i, 0)), pl.BlockSpec((tq, 256), lambda b, i: (i, 0)),
                  pl.BlockSpec((Lb, 128), lambda b, i: (0, 0)), pl.BlockSpec((Lb, 128), lambda b, i: (0, 0)),
                  pl.BlockSpec(memory_space=pltpu.SMEM)],
        out_specs=pl.BlockSpec((tq, 256), lambda b, i: (b * nt + i, 0)),
        out_shape=jax.ShapeDtypeStruct((T, 256), F32),
        compiler_params=_cp("arbitrary", "arbitrary"),
        name="lat_win_attention",
    )(proj, proj, proj, proj, proj, kc, vc, cq, sq, ck, sk, sink)


def _na_bias_kernel(rpb_ref, o_ref):
    h = pl.program_id(0)
    qc = lax.broadcasted_iota(jnp.int32, (GRID_W, GRID_W), 0)
    kc = lax.broadcasted_iota(jnp.int32, (GRID_W, GRID_W), 1)
    dc = kc - qc + (NA_KW - 1)
    wstart = jnp.clip(qc - NA_KW // 2, 0, GRID_W - NA_KW)
    ok = (kc >= wstart) & (kc < wstart + NA_KW)
    n_dc = 2 * NA_KW - 1
    n_dr = 2 * NA_KH - 1
    tabs = []
    for dr in range(n_dr):
        t = jnp.zeros((GRID_W, GRID_W), F32)
        for j in range(n_dc):
            t = jnp.where(dc == j, rpb_ref[(h * n_dr + dr) * n_dc + j], t)
        tabs.append(jnp.where(ok, t, NEG))
    for o in range(NA_KH):
        for a in range(NA_KH):
            o_ref[o, :, GRID_W * a:GRID_W * (a + 1)] = tabs[a + NA_KH - 1 - o]


def _na_bias(rpb):
    H = rpb.shape[0]
    return pl.pallas_call(
        _na_bias_kernel,
        grid=(H,),
        in_specs=[pl.BlockSpec(memory_space=pltpu.SMEM)],
        out_specs=pl.BlockSpec((None, NA_KH, GRID_W, NA_KH * GRID_W), lambda h: (h, 0, 0, 0)),
        out_shape=jax.ShapeDtypeStruct((H, NA_KH, GRID_W, NA_KH * GRID_W), F32),
        compiler_params=_cp("arbitrary"),
        name="na_bias",
    )(rpb.reshape(-1))


def _lat_na_kernel(q_ref, k_ref, v_ref, kc_ref, vc_ref, bias_ref, o_ref, *, rows, rpt):
    t = pl.program_id(1)
    q = q_ref[...] * ATT_SCALE
    kc = kc_ref[...].astype(BF16)
    vc = vc_ref[...].astype(BF16)
    bands = []
    for rr in range(rpt):
        r = t * rpt + rr
        first = jnp.clip(r - NA_KH // 2, 0, rows - NA_KH)
        win = pl.ds(pl.multiple_of(first * GRID_W, GRID_W), NA_KH * GRID_W)
        bands.append((k_ref[win, :].astype(BF16), v_ref[win, :].astype(BF16), r - first))
    for h in range(4):
        sl = slice(64 * h, 64 * (h + 1))
        locs = [(slice(GRID_W * rr, GRID_W * (rr + 1)), kk[:, sl], vv[:, sl], ("add", bias_ref[h, off]))
                for rr, (kk, vv, off) in enumerate(bands)]
        o_ref[:, sl] = _attn_local_ctx(q[:, sl].astype(BF16), locs, kc[:, sl], vc[:, sl], None)


def _lat_na_attention(proj, kc, vc, bias, NB, Lb):
    T = proj.shape[0]
    rows = Lb // GRID_W
    rpt = 8
    tq = GRID_W * rpt
    nt = rows // rpt
    Lc = kc.shape[1]
    kspec = lambda off: pl.BlockSpec((Lb, 256), lambda b, t: (b, off // 256))
    cspec = pl.BlockSpec((None, Lc, 256), lambda b, t: (b, 0, 0))
    return pl.pallas_call(
        functools.partial(_lat_na_kernel, rows=rows, rpt=rpt),
        grid=(NB, nt),
        in_specs=[pl.BlockSpec((tq, 256), lambda b, t: (b * nt + t, P_NQ // 256)),
                  kspec(P_NK), kspec(P_NV), cspec, cspec,
                  pl.BlockSpec((4, NA_KH, GRID_W, NA_KH * GRID_W), lambda b, t: (0, 0, 0, 0))],
        out_specs=pl.BlockSpec((tq, 256), lambda b, t: (b * nt + t, 0)),
        out_shape=jax.ShapeDtypeStruct((T, 256), F32),
        compiler_params=_cp("arbitrary", "arbitrary"),
        name="lat_na_attention",
    )(proj, proj, proj, kc, vc, bias)


def _short_conv_kernel(a_ref, b_ref, c_ref, w_ref, bias_ref, oa_ref, ob_ref, oc_ref, *, L):
    t = lax.broadcasted_iota(jnp.int32, (L, HY_C), 0)
    for n, (x_ref, o_ref) in enumerate(((a_ref, oa_ref), (b_ref, ob_ref), (c_ref, oc_ref))):
        sl = slice(HY_C * n, HY_C * (n + 1))
        x = x_ref[...]
        prev = jnp.where(t == 0, 0.0, pltpu.roll(x, 1, axis=0))
        nxt = jnp.where(t == L - 1, 0.0, pltpu.roll(x, L - 1, axis=0))
        o_ref[...] = prev * w_ref[0:1, sl] + x * w_ref[1:2, sl] + nxt * w_ref[2:3, sl] + bias_ref[:, sl]


def _short_conv(proj, w, b, NB, Lb):
    T = proj.shape[0]
    spec = lambda c: pl.BlockSpec((Lb, HY_C), lambda i: (i, c))
    return pl.pallas_call(
        functools.partial(_short_conv_kernel, L=Lb),
        grid=(NB,),
        in_specs=[spec(P_HY // HY_C), spec(P_HY // HY_C + 1), spec(P_HY // HY_C + 2),
                  pl.BlockSpec((3, 3 * HY_C), lambda i: (0, 0)), pl.BlockSpec((1, 3 * HY_C), lambda i: (0, 0))],
        out_specs=[spec(0)] * 3,
        out_shape=[jax.ShapeDtypeStruct((T, HY_C), F32)] * 3,
        compiler_params=_cp("arbitrary"),
        name="hyena_short_conv",
    )(proj, proj, proj, w, b)


def _hy_filter_kernel(w1_ref, b1_ref, w2_ref, b2_ref, w3_ref, freq_ref, ld_ref, fs_ref, nyq_ref, *, L):
    ti = lax.broadcasted_iota(jnp.int32, (L, 128), 0)
    t = ti.astype(F32)
    j = lax.broadcasted_iota(jnp.int32, (L, 128), 1)
    band = jnp.where(j <= HY_BANDS, j - 1, j - 1 - HY_BANDS).astype(F32)
    ang = (2.0 * math.pi / L) * t * band
    tn = t / L
    z = jnp.where(j == 0, tn, jnp.where(j <= HY_BANDS, jnp.cos(ang),
                                        jnp.where(j <= 2 * HY_BANDS, -jnp.sin(ang), 0.0)))
    a = jnp.sin(freq_ref[0:1, :] * (_dot_hi(z, w1_ref[...]) + b1_ref[...]))
    a = jnp.sin(freq_ref[1:2, :] * (_dot_hi(a, w2_ref[...]) + b2_ref[...]))
    filt = _dot_hi(a, w3_ref[...])
    tcol = lax.broadcasted_iota(jnp.int32, (L, 4 * HY_C), 0)
    filt = filt * jnp.exp(-(tcol.astype(F32) / L) * jnp.exp(ld_ref[...]))
    t1 = lax.broadcasted_iota(jnp.int32, (L, HY_C), 0)
    sign = jnp.where(t1 % 2 == 0, 1.0, -1.0)
    for n in range(2):
        fwd = filt[:, 2 * HY_C * n:2 * HY_C * n + HY_C]
        bwd = jnp.where(t1 == 0, 0.0, filt[:, 2 * HY_C * n + HY_C:2 * HY_C * (n + 1)])
        tot = fwd + bwd
        fs_ref[:, HY_C * n:HY_C * (n + 1)] = tot
        fs_ref[:, 2 * HY_C + HY_C * n:2 * HY_C + HY_C * (n + 1)] = fwd - bwd
        nyq_ref[:, HY_C * n:HY_C * (n + 1)] = (tot * sign).sum(0, keepdims=True)


def _hy_filter(L, w1p, b1, w2, b2, w3, freq, ld):
    full = lambda s: pl.BlockSpec(s, lambda: tuple(0 for _ in s))
    return pl.pallas_call(
        functools.partial(_hy_filter_kernel, L=L),
        in_specs=[full((128, 64)), full((1, 64)), full((64, 64)), full((1, 64)), full((64, 4 * HY_C)),
                  full((2, 64)), full((1, 4 * HY_C))],
        out_specs=[full((L, 4 * HY_C)), full((1, 2 * HY_C))],
        out_shape=[jax.ShapeDtypeStruct((L, 4 * HY_C), F32), jax.ShapeDtypeStruct((1, 2 * HY_C), F32)],
        compiler_params=pltpu.CompilerParams(vmem_limit_bytes=VMEM_LIMIT),
        name="hyena_filter",
    )(w1p, b1, w2, b2, w3, freq, ld)


def _hy_gdft_kernel(cm_ref, sm_ref, fs_ref, nyq_ref, gr_ref, gi_ref, *, tm):
    m = pl.program_id(0)
    f = fs_ref[...].astype(BF16)
    gr_ref[...] = _dot(cm_ref[...], f[:, :2 * HY_C])
    gi = _dot(sm_ref[...], f[:, 2 * HY_C:])
    row = m * tm + lax.broadcasted_iota(jnp.int32, (tm, 2 * HY_C), 0)
    gi_ref[...] = jnp.where(row == 0, nyq_ref[...], gi)


def _hy_gdft(cm, sm, fs, nyq, L, tm):
    return pl.pallas_call(
        functools.partial(_hy_gdft_kernel, tm=tm),
        grid=(L // tm,),
        in_specs=[pl.BlockSpec((tm, L), lambda m: (m, 0)), pl.BlockSpec((tm, L), lambda m: (m, 0)),
                  pl.BlockSpec((L, 4 * HY_C), lambda m: (0, 0)), pl.BlockSpec((1, 2 * HY_C), lambda m: (0, 0))],
        out_specs=[pl.BlockSpec((tm, 2 * HY_C), lambda m: (m, 0))] * 2,
        out_shape=[jax.ShapeDtypeStruct((L, 2 * HY_C), F32)] * 2,
        compiler_params=_cp("arbitrary"),
        name="hyena_filter_dft",
    )(cm, sm, fs, nyq)


def _hy_fwd_kernel(cm_ref, sm_ref, z_ref, gr_ref, gi_ref, yr_ref, yi_ref, *, L, tm, ns):
    m = pl.program_id(1)
    gr = gr_ref[...]
    gi = gi_ref[...]
    row0 = (m * tm + lax.broadcasted_iota(jnp.int32, (tm, HY_C), 0)) == 0
    s = jnp.where(row0, 0.5 / L, 1.0 / L)
    for g in range(ns):
        zb = z_ref[g * L:(g + 1) * L, :].astype(BF16)
        zr = _dot(cm_ref[...], zb)
        zi = _dot(sm_ref[...], zb)
        zigi = zi * gi
        yr_ref[g * tm:(g + 1) * tm, :] = ((zr * gr - jnp.where(row0, 0.0, zigi)) * s).astype(BF16)
        yi_ref[g * tm:(g + 1) * tm, :] = (jnp.where(row0, zigi, zr * gi + zi * gr) * s).astype(BF16)


def _hy_seqs_per_step(NB, Lb, tm):
    ns = max(1, 2048 // Lb) if tm == Lb else 1
    while NB % ns:
        ns //= 2
    return ns


def _hy_fwd(cm, sm, z, gr, gi, n, NB, Lb, tm):
    T = z.shape[0]
    nm = Lb // tm
    ns = _hy_seqs_per_step(NB, Lb, tm)
    return pl.pallas_call(
        functools.partial(_hy_fwd_kernel, L=Lb, tm=tm, ns=ns),
        grid=(NB // ns, nm),
        in_specs=[pl.BlockSpec((tm, Lb), lambda b, m: (m, 0)), pl.BlockSpec((tm, Lb), lambda b, m: (m, 0)),
                  pl.BlockSpec((ns * Lb, HY_C), lambda b, m: (b, 0)),
                  pl.BlockSpec((tm, HY_C), lambda b, m: (m, n)), pl.BlockSpec((tm, HY_C), lambda b, m: (m, n))],
        out_specs=[pl.BlockSpec((ns * tm, HY_C), lambda b, m: (b * nm + m, 0))] * 2,
        out_shape=[jax.ShapeDtypeStruct((T, HY_C), BF16)] * 2,
        compiler_params=_cp("arbitrary", "arbitrary"),
        name="hyena_fwd_dft",
    )(cm, sm, z, gr, gi)


def _hy_inv_kernel(cm_ref, smt_ref, yr_ref, yi_ref, z_ref, g_ref, skip_ref, o_ref, *, L, tm, ns):
    for g in range(ns):
        seq = slice(g * L, (g + 1) * L)
        out = slice(g * tm, (g + 1) * tm)
        conv = _dot(cm_ref[...], yr_ref[seq, :]) + _dot(smt_ref[...], yi_ref[seq, :])
        o_ref[out, :] = g_ref[out, :] * (conv + skip_ref[...] * z_ref[out, :])


def _hy_inv(cm, smt, yr, yi, z, gate, skip, n, NB, Lb, tm):
    T = z.shape[0]
    nm = Lb // tm
    ns = _hy_seqs_per_step(NB, Lb, tm)
    tile = pl.BlockSpec((ns * tm, HY_C), lambda b, m: (b * nm + m, 0))
    seq = pl.BlockSpec((ns * Lb, HY_C), lambda b, m: (b, 0))
    return pl.pallas_call(
        functools.partial(_hy_inv_kernel, L=Lb, tm=tm, ns=ns),
        grid=(NB // ns, nm),
        in_specs=[pl.BlockSpec((tm, Lb), lambda b, m: (m, 0)), pl.BlockSpec((tm, Lb), lambda b, m: (m, 0)),
                  seq, seq, tile, tile, pl.BlockSpec((None, 1, HY_C), lambda b, m: (n, 0, 0))],
        out_specs=tile,
        out_shape=jax.ShapeDtypeStruct((T, HY_C), F32),
        compiler_params=_cp("arbitrary", "arbitrary"),
        name="hyena_inv_dft",
    )(cm, smt, yr, yi, z, gate, skip)


def _dft_mats(L):
    k = jnp.arange(L, dtype=jnp.int32)
    blk = 64

    def trig(mult):
        ang = ((mult[:, None] * k[None, :]) % (2 * L)).astype(F32) * (math.pi / L)
        return jnp.cos(ang), jnp.sin(ang)

    ca, sa = trig(jnp.arange(L // blk, dtype=jnp.int32) * blk)
    cb, sb = trig(jnp.arange(blk, dtype=jnp.int32))
    cm = (ca[:, None, :] * cb[None] - sa[:, None, :] * sb[None]).reshape(L, L)
    s = -(sa[:, None, :] * cb[None] + ca[:, None, :] * sb[None]).reshape(L, L)
    alt = jnp.where(k % 2 == 0, 1.0, -1.0).astype(F32)
    sm = jnp.where(k[:, None] == 0, alt[None, :], s)
    smt = jnp.where(k[None, :] == 0, alt[:, None], s)
    return cm.astype(BF16), sm.astype(BF16), smt.astype(BF16)


def _merge_kernel(oa_ref, ob_ref, oc_ref, od_ref, g0_ref, g1_ref, g2_ref, g3_ref, wb_ref, wo_ref, x_ref, mod_ref,
                  lg_ref, lb_ref, x1_ref, h2_ref, hp_ref):
    acc = None
    for o_ref, g_ref, i in ((oa_ref, g0_ref, 0), (ob_ref, g1_ref, 1), (oc_ref, g2_ref, 2), (od_ref, g3_ref, 3)):
        y = _sigmoid(g_ref[...].astype(F32)) * _dot(o_ref[...].astype(BF16), wb_ref[i])
        acc = y if acc is None else acc + y
    mix = _dot(acc.astype(BF16), wo_ref[...])
    m = mod_ref[...]
    x1 = _layer_norm(DN_ALPHA * x_ref[...] + m[2:3, :] * mix, lg_ref[...], lb_ref[...])
    x1_ref[...] = x1
    h2 = x1 * (1.0 + m[4:5, :]) + m[3:4, :]
    h2_ref[...] = h2
    hp_ref[...] = _pack_pairs(h2)


def _merge(outs, gates, wb, wo, x, mod, lg, lb, mod_row, bm):
    T = x.shape[0]
    row = lambda w: pl.BlockSpec((bm, w), lambda i: (i, 0))
    gspec = lambda n: pl.BlockSpec((bm, D), lambda i: (i, n))
    fixed2 = lambda s: pl.BlockSpec(s, lambda i: (0, 0))
    return pl.pallas_call(
        _merge_kernel,
        grid=(T // bm,),
        in_specs=[row(256)] * 4 + [gspec(0), gspec(1), gspec(2), gspec(3),
                                   pl.BlockSpec((4, 256, D), lambda i: (0, 0, 0)), fixed2((D, D)), row(D),
                                   pl.BlockSpec((None, 6, D), lambda i: (mod_row(i), 0, 0)),
                                   fixed2((1, D)), fixed2((1, D))],
        out_specs=[row(D), row(D), row(D // 2)],
        out_shape=[jax.ShapeDtypeStruct((T, D), F32)] * 2 + [jax.ShapeDtypeStruct((T, D // 2), jnp.int32)],
        compiler_params=_cp("arbitrary"),
        name="merge_norm",
    )(*outs, gates, gates, gates, gates, wb, wo, x, mod, lg, lb)


def _router_kernel(h_ref, rt_ref, bias_ref, g_ref, rank_ref, cnt_ref, *, tt):
    per = N_EXPERTS // N_GROUPS
    logits = lax.dot_general(rt_ref[...], h_ref[...], (((1,), (1,)), ((), ())), preferred_element_type=F32,
                             precision=lax.Precision.HIGHEST)
    scores = _sigmoid(logits)
    sel = (scores + bias_ref[...]).reshape(N_GROUPS, per, tt)
    gid = lax.broadcasted_iota(jnp.int32, (N_GROUPS, per, tt), 0).astype(F32)
    jid = lax.broadcasted_iota(jnp.int32, (N_GROUPS, per, tt), 1).astype(F32)
    eid = gid * per + jid
    ninf = -jnp.inf
    m1 = sel.max(1, keepdims=True)
    i1 = jnp.where(sel == m1, jid, float(per)).min(1, keepdims=True)
    m2 = jnp.where(jid == i1, ninf, sel).max(1, keepdims=True)
    gs = m1 + m2
    g1 = lax.broadcasted_iota(jnp.int32, (N_GROUPS, 1, tt), 0).astype(F32)
    chosen = jnp.zeros((N_GROUPS, 1, tt), F32)
    for _ in range(TOPK_GROUPS):
        mx = gs.max(0, keepdims=True)
        gi = jnp.where(gs == mx, g1, float(N_GROUPS)).min(0, keepdims=True)
        pick = g1 == gi
        chosen = jnp.where(pick, 1.0, chosen)
        gs = jnp.where(pick, ninf, gs)
    cand = jnp.where(chosen > 0.0, sel, NEG)
    picked = jnp.zeros((N_GROUPS, per, tt), F32)
    for _ in range(TOP_K):
        mx = cand.max(1, keepdims=True).max(0, keepdims=True)
        ei = jnp.where(cand == mx, eid, float(N_EXPERTS)).min(1, keepdims=True).min(0, keepdims=True)
        pick = eid == ei
        picked = jnp.where(pick, 1.0, picked)
        cand = jnp.where(pick, ninf, cand)
    w = scores.reshape(N_GROUPS, per, tt) * picked
    wsum = w.sum(1, keepdims=True).sum(0, keepdims=True)
    g_ref[...] = (w / wsum * ROUTED_SCALE).reshape(N_EXPERTS, tt)
    pk = picked.reshape(N_EXPERTS, tt)
    t_in = lax.broadcasted_iota(jnp.int32, (tt, tt), 0)
    t_out = lax.broadcasted_iota(jnp.int32, (tt, tt), 1)
    upper = jnp.where(t_in <= t_out, 1.0, 0.0).astype(BF16)

    @pl.when(pl.program_id(0) == 0)
    def _():
        cnt_ref[...] = jnp.zeros_like(cnt_ref)

    before = cnt_ref[:, 0:1]
    rank_ref[...] = jnp.where(pk > 0.0, before + _dot(pk.astype(BF16), upper) - 1.0, -1.0)
    cnt_ref[...] += pk.sum(-1, keepdims=True)


def _router(h2, router_t, bias, tt):
    T = h2.shape[0]
    tile = pl.BlockSpec((N_EXPERTS, tt), lambda i: (0, i))
    return pl.pallas_call(
        functools.partial(_router_kernel, tt=tt),
        grid=(T // tt,),
        in_specs=[pl.BlockSpec((tt, D), lambda i: (i, 0)), pl.BlockSpec((N_EXPERTS, D), lambda i: (0, 0)),
                  pl.BlockSpec((N_EXPERTS, 1), lambda i: (0, 0))],
        out_specs=[tile, tile, pl.BlockSpec((N_EXPERTS, 128), lambda i: (0, 0))],
        out_shape=[jax.ShapeDtypeStruct((N_EXPERTS, T), F32), jax.ShapeDtypeStruct((N_EXPERTS, T), F32),
                   jax.ShapeDtypeStruct((N_EXPERTS, 128), F32)],
        compiler_params=_cp("arbitrary"),
        name="moe_router",
    )(h2, router_t, bias)


def _route_pos_kernel(gate_ref, rank_ref, cnt_ref, pos_ref, w_ref, te_ref, nx_ref, nt_ref, *, tm, nt_max):
    ei = lax.broadcasted_iota(jnp.int32, (N_EXPERTS, N_EXPERTS), 0)
    ej = lax.broadcasted_iota(jnp.int32, (N_EXPERTS, N_EXPERTS), 1)
    below = jnp.where(ej < ei, 1.0, 0.0)
    padded = jnp.ceil(cnt_ref[...] * (1.0 / tm)) * tm
    offs = _dot_hi(below, padded)
    rank = rank_ref[...]
    routed = rank >= 0.0
    pos = offs[:, 0:1] + rank
    slot = _dot(below.astype(BF16), jnp.where(routed, 1.0, 0.0).astype(BF16))
    gate = gate_ref[...]
    for k in range(TOP_K):
        mine = routed & (slot == float(k))
        pos_ref[k:k + 1, :] = jnp.where(mine, pos, 0.0).sum(0, keepdims=True).astype(jnp.int32)
        w_ref[k:k + 1, :] = jnp.where(mine, gate, 0.0).sum(0, keepdims=True)
    ends = (offs + padded)[:, 0:1]
    first = (lax.broadcasted_iota(jnp.int32, (N_EXPERTS, nt_max), 1) * tm).astype(F32)
    te = jnp.minimum(jnp.where(ends <= first, 1.0, 0.0).sum(0, keepdims=True), N_EXPERTS - 1.0)
    te_ref[...] = te.astype(jnp.int32)
    eid = lax.broadcasted_iota(jnp.int32, (N_EXPERTS, nt_max), 0).astype(F32)
    nx_ref[...] = (jnp.where(eid == te, ends, 0.0).sum(0, keepdims=True) * (1.0 / tm)).astype(jnp.int32)
    nt_ref[...] = (padded.sum(0, keepdims=True) * (1.0 / tm)).astype(jnp.int32)


def _route_pos(gate_t, rank, cnt, tt, tm, nt_max):
    T = gate_t.shape[1]
    tile = pl.BlockSpec((N_EXPERTS, tt), lambda i: (0, i))
    out = pl.BlockSpec((TOP_K, tt), lambda i: (0, i))
    return pl.pallas_call(
        functools.partial(_route_pos_kernel, tm=tm, nt_max=nt_max),
        grid=(T // tt,),
        in_specs=[tile, tile, pl.BlockSpec((N_EXPERTS, 128), lambda i: (0, 0))],
        out_specs=[out, out, pl.BlockSpec((1, nt_max), lambda i: (0, 0)), pl.BlockSpec((1, nt_max), lambda i: (0, 0)),
                   pl.BlockSpec((1, 128), lambda i: (0, 0))],
        out_shape=[jax.ShapeDtypeStruct((TOP_K, T), jnp.int32), jax.ShapeDtypeStruct((TOP_K, T), F32),
                   jax.ShapeDtypeStruct((1, nt_max), jnp.int32), jax.ShapeDtypeStruct((1, nt_max), jnp.int32),
                   jax.ShapeDtypeStruct((1, 128), jnp.int32)],
        compiler_params=_cp("arbitrary"),
        name="moe_positions",
    )(gate_t, rank, cnt)


def _gmm_kernel(te_ref, nx_ref, nt_ref, xs_ref, w1_hbm, w3_hbm, w2_hbm, ys_ref, b1_ref, b3_ref, b2_ref,
                f1_ref, f3_ref, f2_ref, seg_ref, sem, *, l):
    j = pl.program_id(0)
    live = j < nt_ref[0]
    new_expert = (j == 0) | (te_ref[j] != te_ref[jnp.maximum(j - 1, 0)])

    def fetch(e, slot):
        return [pltpu.make_async_copy(w_hbm.at[l, e], f_ref.at[slot], sem.at[i, slot])
                for i, (w_hbm, f_ref) in enumerate(((w1_hbm, f1_ref), (w3_hbm, f3_ref), (w2_hbm, f2_ref)))]

    @pl.when(live & new_expert)
    def _():
        @pl.when(j == 0)
        def _():
            seg_ref[0] = 0
            for c in fetch(te_ref[0], 0):
                c.start()

        slot = lax.rem(seg_ref[0], 2)
        for c in fetch(te_ref[j], slot):
            c.wait()
        b1_ref[...] = f1_ref[slot].astype(BF16)
        b3_ref[...] = f3_ref[slot].astype(BF16)
        b2_ref[...] = f2_ref[slot].astype(BF16)
        nxt = nx_ref[j]

        @pl.when(nxt < nt_ref[0])
        def _():
            for c in fetch(te_ref[nxt], 1 - slot):
                c.start()

        seg_ref[0] = seg_ref[0] + 1

    @pl.when(live)
    def _():
        xa, xb = _unpack_pairs(xs_ref[...])
        xa, xb = xa.astype(BF16), xb.astype(BF16)
        half = D // 2
        a = _dot(xa, b1_ref[:half, :]) + _dot(xb, b1_ref[half:, :])
        b = _dot(xa, b3_ref[:half, :]) + _dot(xb, b3_ref[half:, :])
        hid = (a * _sigmoid(a) * b).astype(BF16)
        ys_ref[...] = _pack_pairs(_dot(hid, b2_ref[...]))


def _gmm(te, nx, nt, xs, w1, w3, w2, l, tm):
    n_slots = xs.shape[0]
    ds = D_EXPERT
    rows = pl.BlockSpec((tm, D // 2), lambda j, te, nx, nt: (jnp.minimum(j, nt[0] - 1), 0))
    hbm = pl.BlockSpec(memory_space=pl.ANY)
    return pl.pallas_call(
        functools.partial(_gmm_kernel, l=l),
        grid_spec=pltpu.PrefetchScalarGridSpec(
            num_scalar_prefetch=3,
            grid=(n_slots // tm,),
            in_specs=[rows, hbm, hbm, hbm],
            out_specs=rows,
            scratch_shapes=[pltpu.VMEM((D, ds), BF16), pltpu.VMEM((D, ds), BF16), pltpu.VMEM((ds, D), BF16),
                            pltpu.VMEM((2, D, ds), F32), pltpu.VMEM((2, D, ds), F32), pltpu.VMEM((2, ds, D), F32),
                            pltpu.SMEM((1,), jnp.int32), pltpu.SemaphoreType.DMA((3, 2))]),
        out_shape=jax.ShapeDtypeStruct((n_slots, D // 2), jnp.int32),
        compiler_params=_cp("arbitrary"),
        name="moe_grouped_ffn",
    )(te, nx, nt, xs, w1, w3, w2)


def _combine_kernel(yk_ref, w_ref, hp_ref, s1_ref, s3_ref, s2_ref, x_ref, mod_ref, lg_ref, lb_ref, o_ref):
    w = w_ref[...]
    acc_a = acc_b = None
    for k in range(TOP_K):
        ya, yb = _unpack_pairs(yk_ref[k])
        wk = w[:, k:k + 1]
        acc_a = wk * ya if acc_a is None else acc_a + wk * ya
        acc_b = wk * yb if acc_b is None else acc_b + wk * yb
    ha, hb = _unpack_pairs(hp_ref[...])
    ha, hb = ha.astype(BF16), hb.astype(BF16)
    half = D // 2
    a = _dot(ha, s1_ref[:half, :]) + _dot(hb, s1_ref[half:, :])
    b = _dot(ha, s3_ref[:half, :]) + _dot(hb, s3_ref[half:, :])
    y = jnp.concatenate([acc_a, acc_b], axis=1) + _dot((a * _sigmoid(a) * b).astype(BF16), s2_ref[...])
    m = mod_ref[...]
    o_ref[...] = _layer_norm(DN_ALPHA * x_ref[...] + m[5:6, :] * y, lg_ref[...], lb_ref[...])


def _combine(yk, w, hp, s1, s3, s2, x1, mod, lg, lb, mod_row, bm):
    T = x1.shape[0]
    ds = D_EXPERT
    row = lambda n: pl.BlockSpec((bm, n), lambda i: (i, 0))
    fixed = lambda s: pl.BlockSpec(s, lambda i: (0, 0))
    return pl.pallas_call(
        _combine_kernel,
        grid=(T // bm,),
        in_specs=[pl.BlockSpec((TOP_K, bm, D // 2), lambda i: (0, i, 0)), row(TOP_K), row(D // 2),
                  fixed((D, ds)), fixed((D, ds)), fixed((ds, D)), row(D),
                  pl.BlockSpec((None, 6, D), lambda i: (mod_row(i), 0, 0)), fixed((1, D)), fixed((1, D))],
        out_specs=row(D),
        out_shape=jax.ShapeDtypeStruct((T, D), F32),
        compiler_params=_cp("arbitrary"),
        name="moe_combine_norm",
    )(yk, w, hp, s1, s3, s2, x1, mod, lg, lb)


def _sc_worker():
    return lax.axis_index("s") * SC_CORES + lax.axis_index("c")


def _sc_mesh():
    return plsc.VectorSubcoreMesh(core_axis_name="c", subcore_axis_name="s")


def _sc_gather(table, idx):
    N, W = idx.shape[0], table.shape[1]
    per_w = N // SC_WORKERS
    n_chunks = per_w // SC_ROWS

    def body(table_hbm, idx_hbm, out_hbm, idx_v, rows_v, sem):
        base = _sc_worker() * per_w
        pltpu.sync_copy(idx_hbm.at[pl.ds(base, per_w)], idx_v)

        @pl.loop(0, n_chunks)
        def _(c):
            off = pl.multiple_of(c * SC_ROWS, SC_ROWS)
            pltpu.async_copy(table_hbm.at[idx_v.at[pl.ds(off, SC_ROWS)]], rows_v, sem).wait()
            pltpu.sync_copy(rows_v, out_hbm.at[pl.ds(base + off, SC_ROWS)])

    return pl.kernel(
        body, out_type=jax.ShapeDtypeStruct((N, W), table.dtype), mesh=_sc_mesh(),
        scratch_types=[pltpu.VMEM((per_w,), jnp.int32), pltpu.VMEM((SC_ROWS, W), table.dtype),
                       pltpu.SemaphoreType.DMA],
        name="sc_gather",
    )(table, idx)


def _sc_dispatch(pos, table, n_slots):
    NP, (T, W) = pos.shape[0], table.shape
    per_w = n_slots // SC_WORKERS
    n_chunks = per_w // SC_ROWS
    scan = 8192

    def body(pos_hbm, table_hbm, out_hbm, pos_v, src_v, rows_v, sem):
        base = _sc_worker() * per_w
        lane = lax.iota(jnp.int32, SC_LANES)

        @pl.loop(0, per_w // SC_LANES)
        def _(j):
            o = pl.multiple_of(j * SC_LANES, SC_LANES)
            src_v[pl.ds(o, SC_LANES)] = (base + o + lane) & (T - 1)

        @pl.loop(0, NP // scan)
        def _(c):
            pltpu.sync_copy(pos_hbm.at[pl.ds(pl.multiple_of(c * scan, scan), scan)], pos_v)

            @pl.loop(0, scan // SC_LANES)
            def _(v):
                o = pl.multiple_of(v * SC_LANES, SC_LANES)
                p = pos_v[pl.ds(o, SC_LANES)] - base
                mine = (p >= 0) & (p < per_w)
                tok = (c * scan + o + lane) & (T - 1)
                plsc.store_scatter(src_v, [jnp.where(mine, p, 0)], tok, mask=mine)

        @pl.loop(0, n_chunks)
        def _(c):
            off = pl.multiple_of(c * SC_ROWS, SC_ROWS)
            pltpu.async_copy(table_hbm.at[src_v.at[pl.ds(off, SC_ROWS)]], rows_v, sem).wait()
            pltpu.sync_copy(rows_v, out_hbm.at[pl.ds(base + off, SC_ROWS)])

    return pl.kernel(
        body, out_type=jax.ShapeDtypeStruct((n_slots, W), table.dtype), mesh=_sc_mesh(),
        scratch_types=[pltpu.VMEM((scan,), jnp.int32), pltpu.VMEM((per_w,), jnp.int32),
                       pltpu.VMEM((SC_ROWS, W), table.dtype), pltpu.SemaphoreType.DMA],
        compiler_params=pltpu.CompilerParams(needs_layout_passes=False),
        name="sc_dispatch",
    )(pos, table)


def _caches_kernel(*refs, nb, S):
    n_in = 6 * DEPTH
    outs = refs[n_in:]
    l = pl.program_id(0)
    for a in range(DEPTH):
        @pl.when(l == a)
        def _(a=a):
            ckv, kpe, wk, wv, nk, nv = refs[6 * a:6 * (a + 1)]
            for g in range(nb):
                rows = slice(g * S, (g + 1) * S)
                outs[0][g] = ckv[rows, :]
                outs[1][g] = kpe[rows, 64:96]
                outs[2][g] = wk[rows, :]
                outs[3][g] = wv[rows, :]
                outs[4][g] = nk[rows, :]
                outs[5][g] = nv[rows, :]


def _emit_caches(projs, ckvs, B, S):
    nb = 4
    while B % nb:
        nb //= 2

    def layer_specs(a):
        row = lambda l, b: jnp.where(l == a, b, 0)
        col = lambda w, off: pl.BlockSpec((nb * S, w), lambda l, b: (row(l, b), off // w))
        return [pl.BlockSpec((nb * S, 128), lambda l, b: (row(l, b), 0)), col(128, P_KPE), col(128, P_WK),
                col(128, P_WV), col(256, P_NK), col(256, P_NV)]

    in_specs, args = [], []
    for a in range(DEPTH):
        in_specs += layer_specs(a)
        args += [ckvs[a]] + [projs[a]] * 5
    widths = (128, 32, 128, 128, 256, 256)
    return pl.pallas_call(
        functools.partial(_caches_kernel, nb=nb, S=S),
        grid=(DEPTH, B // nb),
        in_specs=in_specs,
        out_specs=[pl.BlockSpec((nb, None, S, w), lambda l, b: (b, l, 0, 0)) for w in widths],
        out_shape=[jax.ShapeDtypeStruct((B, DEPTH, S, w), F32) for w in widths],
        compiler_params=_cp("arbitrary", "arbitrary"),
        name="context_tensors",
    )(*args)


def _rot_cols(w, q):
    a, b, c, d = w[..., :q], w[..., q:2 * q], w[..., 2 * q:3 * q], w[..., 3 * q:]
    return jnp.concatenate([-b, a, -d, c], -1)


def _prep_w_in(w):
    z = lambda n: jnp.zeros((D, n), w.dtype)
    qlat, ckv, kpe, hy = w[:, 0:256], w[:, 256:384], w[:, 384:416], w[:, 416:1184]
    wq, wk, wv = w[:, 1184:1440], w[:, 1440:1568], w[:, 1568:1696]
    nq, nk, nv, gate = w[:, 1696:1952], w[:, 1952:2208], w[:, 2208:2464], w[:, 2464:]
    wq_r = _rot_cols(wq.reshape(D, 4, 64), 16).reshape(D, 256)
    wk_r = _rot_cols(wk.reshape(D, 2, 64), 16).reshape(D, 128)
    kpe_r = _rot_cols(kpe, 8)
    cols = [qlat, ckv, z(64), kpe, z(32), hy, wq, wk, wv, nq, nk, nv, wq_r, wk_r, z(64), kpe_r, z(32), gate]
    return jnp.concatenate(cols, 1).astype(BF16)


def _prep_mla(w_uq, w_ukv):
    uq = w_uq.reshape(256, 4, 96)
    nope, pe = uq[..., :64], uq[..., 64:]
    z32 = jnp.zeros((256, 4, 32), w_uq.dtype)
    z64 = jnp.zeros((256, 4, 64), w_uq.dtype)
    wcat = jnp.concatenate([nope, pe, z32], -1).reshape(256, 512).astype(BF16)
    wrot = jnp.concatenate([z64, _rot_cols(pe, 8), z32], -1).reshape(256, 512).astype(BF16)
    ukv = w_ukv.reshape(128, 4, 128)
    wk = jnp.concatenate([ukv[..., :64], jnp.zeros((128, 4, 64), w_ukv.dtype)], -1).reshape(128, 512).astype(BF16)
    wv = ukv[..., 64:].reshape(128, 256).astype(BF16)
    return wcat, wrot, wk, wv


def _rope_tab(L, q):
    t = jnp.arange(L)
    inv = ROPE_BASE ** (-jnp.arange(q, dtype=F32) / q)
    ar = (t // GRID_W).astype(F32)[:, None] * inv[None, :]
    ac = (t % GRID_W).astype(F32)[:, None] * inv[None, :]
    cos = jnp.concatenate([jnp.cos(ar), jnp.cos(ar), jnp.cos(ac), jnp.cos(ac)], 1)
    sin = jnp.concatenate([jnp.sin(ar), jnp.sin(ar), jnp.sin(ac), jnp.sin(ac)], 1)
    return cos, sin


def _rope_tables(L):
    c8, s8 = _rope_tab(L, 8)
    c16, s16 = _rope_tab(L, 16)
    one, zero = jnp.ones((L, 64), F32), jnp.zeros((L, 64), F32)
    z32 = jnp.zeros((L, 32), F32)
    mla_q = (jnp.tile(jnp.concatenate([one, c8, z32], 1), (1, 4)), jnp.tile(jnp.concatenate([zero, s8, z32], 1), (1, 4)))
    mla_k = (jnp.concatenate([zero, c8, z32], 1), jnp.concatenate([zero, s8, z32], 1))
    win = (jnp.tile(c16, (1, 4)), jnp.tile(s16, (1, 4)), jnp.tile(c16, (1, 2)), jnp.tile(s16, (1, 2)))
    return mla_q + mla_k, win


def _hyena(proj, lp, dft, NB, Lb):
    cm, sm, smt = dft
    tm = min(Lb, 512)
    v, x1, x2 = _short_conv(proj, lp["hy_conv_w"], lp["hy_conv_b"].reshape(1, -1), NB, Lb)
    w1p = jnp.pad(lp["hy_w1"], ((0, 128 - lp["hy_w1"].shape[0]), (0, 0)))
    fs, nyq = _hy_filter(Lb, w1p, lp["hy_b1"].reshape(1, -1), lp["hy_w2"], lp["hy_b2"].reshape(1, -1), lp["hy_w3"],
                         lp["hy_sin_freq"], lp["hy_log_decay"].reshape(1, -1))
    gr, gi = _hy_gdft(cm, sm, fs, nyq, Lb, tm)
    skip = lp["hy_skip"].reshape(2, 1, HY_C)
    z = v
    for n, gate in enumerate((x1, x2)):
        yr, yi = _hy_fwd(cm, sm, z, gr, gi, n, NB, Lb, tm)
        z = _hy_inv(cm, smt, yr, yi, z, gate, skip, n, NB, Lb, tm)
    return z


def _layer(x, mod, lp, l, NB, Lb, mod_row_of_batch, dft, cache=None, tabs=None, na_bias=None):
    T = NB * Lb
    latent = cache is not None
    bm = 256
    rows_of = lambda n: (lambda i: mod_row_of_batch((i * n) // Lb))
    mod_row = rows_of(bm)
    span = Lb if latent else T
    bmp = min(span, 1024)
    proj, gates = _in_proj(x, mod, lp["w_in_p"], rows_of(bmp), bmp)

    gq, gkv = lp["mla_q_norm"].reshape(1, -1), lp["mla_kv_norm"].reshape(1, -1)
    wcat, wrot, wk, wv = lp["mla_w"]
    q_all, ckv_n, kpe_r = _mla_q(proj, gq, gkv, wcat, wrot, tabs[0] if latent else None, Lb, min(span, 512))
    if latent:
        ckv_c, kpe_c, kc_c, vc_c, kd_c, vd_c = cache
        Lc = ckv_c.shape[1]
        kpe_cp = jnp.pad(kpe_c, ((0, 0), (0, 0), (64, 32)))
        ckv_all = jnp.concatenate([ckv_c, ckv_n.reshape(NB, Lb, 128)], 1).reshape(NB * (Lc + Lb), 128)
        kpe_all = jnp.concatenate([kpe_cp, kpe_r.reshape(NB, Lb, 128)], 1).reshape(NB * (Lc + Lb), 128)
        k_all, v_all = _mla_kv(ckv_all, kpe_all, wk, wv, 512)
        oa = _lat_mla_attention(q_all, k_all, v_all, NB, Lb, Lc + Lb, 256)
        oc = _lat_win_attention(proj, kc_c.reshape(NB, Lc, 128), vc_c.reshape(NB, Lc, 128), tabs[1],
                                lp["win_sink"], NB, Lb)
        od = _lat_na_attention(proj, kd_c.reshape(NB, Lc, 256), vd_c.reshape(NB, Lc, 256), na_bias, NB, Lb)
    else:
        k_all, v_all = _mla_kv(ckv_n, kpe_r, wk, wv, 512)
        oa, oc, od = _ctx_attention(proj, q_all, k_all, v_all, lp["win_sink"], NB, Lb)
    ob = _hyena(proj, lp, dft, NB, Lb)

    bmm = min(span, 512)
    x1, h2, hp = _merge((oa, ob, oc, od), gates, lp["w_branch_b"], lp["w_out_b"], x, mod,
                        lp["ln1_g"].reshape(1, -1), lp["ln1_b"].reshape(1, -1), rows_of(bmm), bmm)
    n_slots = T * TOP_K + N_EXPERTS * MOE_TM
    gate_t, rank, cnt = _router(h2, lp["moe_router"].T, lp["moe_bias"].reshape(-1, 1), 512)
    pos, w8, te, nx, nt = _route_pos(gate_t, rank, cnt, 512, MOE_TM, n_slots // MOE_TM)
    xs = _sc_dispatch(pos.reshape(-1), hp, n_slots)
    ys = _gmm(te.reshape(-1), nx.reshape(-1), nt.reshape(-1)[:1], xs, lp["moe_w1"], lp["moe_w3"], lp["moe_w2"], l,
              MOE_TM)
    yk = _sc_gather(ys, pos.reshape(-1)).reshape(TOP_K, T, D // 2)
    x2 = _combine(yk, w8.T, hp, lp["sh_w1_b"], lp["sh_w3_b"], lp["sh_w2_b"], x1, mod,
                  lp["ln2_g"].reshape(1, -1), lp["ln2_b"].reshape(1, -1), mod_row, bm)
    return x2, (proj, ckv_n)


def kernel(x_prompt, x_sample, cache_mla_ckv, cache_mla_kpe, cache_win_k, cache_win_v, cache_na_k, cache_na_v, c, c_ctx, w_ada, b_ada, w_in, mla_q_norm, mla_kv_norm, mla_w_uq, mla_w_ukv, hy_conv_w, hy_conv_b, hy_w1, hy_b1, hy_w2, hy_b2, hy_w3, hy_sin_freq, hy_log_decay, hy_skip, win_sink, na_rpb, w_branch, w_out, ln1_g, ln1_b, ln2_g, ln2_b, moe_router, moe_bias, moe_w1, moe_w3, moe_w2, sh_w1, sh_w3, sh_w2):
    B, S, _ = x_prompt.shape
    DB, DS, _ = x_sample.shape
    xp = x_prompt.reshape(B * S, D)
    xs = x_sample.reshape(DB * DS, D)
    cvec = jnp.concatenate([c_ctx[None, :], c, jnp.zeros((8 - 1 - DB, D), F32)], 0)
    dft_ctx = _dft_mats(S)
    dft_lat = _dft_mats(DS)
    tabs = _rope_tables(DS)
    projs, ckvs = [], []
    for l in range(DEPTH):
        lp = dict(w_in_p=_prep_w_in(w_in[l]), mla_q_norm=mla_q_norm[l], mla_kv_norm=mla_kv_norm[l],
                  mla_w=_prep_mla(mla_w_uq[l], mla_w_ukv[l]), hy_conv_w=hy_conv_w[l], hy_conv_b=hy_conv_b[l],
                  hy_w1=hy_w1[l], hy_b1=hy_b1[l], hy_w2=hy_w2[l], hy_b2=hy_b2[l], hy_w3=hy_w3[l],
                  hy_sin_freq=hy_sin_freq[l], hy_log_decay=hy_log_decay[l], hy_skip=hy_skip[l],
                  win_sink=win_sink[l], w_branch_b=w_branch[l].astype(BF16), w_out_b=w_out[l].astype(BF16),
                  ln1_g=ln1_g[l], ln1_b=ln1_b[l], ln2_g=ln2_g[l], ln2_b=ln2_b[l],
                  moe_router=moe_router[l], moe_bias=moe_bias[l], moe_w1=moe_w1, moe_w3=moe_w3, moe_w2=moe_w2,
                  sh_w1_b=sh_w1[l].astype(BF16), sh_w3_b=sh_w3[l].astype(BF16), sh_w2_b=sh_w2[l].astype(BF16))
        mod = _modulation(cvec, w_ada, b_ada, l)
        xp, (proj, ckv_n) = _layer(xp, mod, lp, l, B, S, lambda b: 0, dft_ctx)
        projs.append(proj)
        ckvs.append(ckv_n)
        cache = (cache_mla_ckv[:, l], cache_mla_kpe[:, l], cache_win_k[:, l], cache_win_v[:, l],
                 cache_na_k[:, l], cache_na_v[:, l])
        xs, _ = _layer(xs, mod, lp, l, DB, DS, lambda b: 1 + b, dft_lat, cache=cache, tabs=tabs,
                       na_bias=_na_bias(na_rpb[l]))
    ckv, kpe, wk, wv, nk, nv = _emit_caches(projs, ckvs, B, S)
    heads = lambda t, h: t.reshape(B, DEPTH, S, h, HEAD_DIM)
    return (xp.reshape(B, S, D), xs.reshape(DB, DS, D), ckv, kpe, heads(wk, 2), heads(wv, 2), heads(nk, 4),
            heads(nv, 4))
```

```python
import functools
import math

import jax
import jax.numpy as jnp
from jax import lax
from jax.experimental import pallas as pl
from jax.experimental.pallas import tpu as pltpu
from jax.experimental.pallas import tpu_sc as plsc

F32 = jnp.float32
BF16 = jnp.bfloat16

D = 1024
DEPTH = 2
GRID_W = 64
HEAD_DIM = 64
MLA_SCALE = 96 ** -0.5
ATT_SCALE = HEAD_DIM ** -0.5
HY_C = 256
HY_BANDS = 8
NA_KH = 8
NA_KW = 16
N_EXPERTS = 64
N_GROUPS = 8
TOP_K = 8
TOPK_GROUPS = 4
D_EXPERT = 256
ROUTED_SCALE = 2.5
ROPE_BASE = 10000.0
LN_EPS = 1e-5
RMS_EPS = 1e-6
NEG = -1e30
DN_ALPHA = (2 * DEPTH) ** 0.25

P_QLAT, P_CKV, P_KPE, P_HY = 0, 256, 384, 512
P_WQ, P_WK, P_WV = 1280, 1536, 1664
P_NQ, P_NK, P_NV = 1792, 2048, 2304
P_WQR, P_WKR, P_KPER, P_GATE = 2560, 2816, 2944, 3072
N_PROJ = 7168

VMEM_LIMIT = 56 * 1024 * 1024

SC_CORES = 2
SC_SUBCORES = 16
SC_LANES = 16
SC_WORKERS = SC_CORES * SC_SUBCORES
SC_ROWS = 64

MOE_TM = 512

def _cp(*sem):
    return pltpu.CompilerParams(dimension_semantics=sem, vmem_limit_bytes=VMEM_LIMIT)


def _sigmoid(x):
    return 1.0 / (1.0 + jnp.exp(-x))


def _dot(a, b):
    return jnp.dot(a, b, preferred_element_type=F32)


def _dot_nt(a, b):
    return lax.dot_general(a, b, (((1,), (1,)), ((), ())), preferred_element_type=F32)


def _dot_hi(a, b):
    return jnp.dot(a, b, preferred_element_type=F32, precision=lax.Precision.HIGHEST)


def _pack_pairs(x):
    w = x.shape[1] // 2
    hi = lax.bitcast_convert_type(x[:, :w].astype(BF16).astype(F32), jnp.int32)
    lo = lax.bitcast_convert_type(x[:, w:].astype(BF16).astype(F32), jnp.int32)
    return hi | lax.shift_right_logical(lo, 16)


def _unpack_pairs(p):
    hi = lax.bitcast_convert_type(p & jnp.int32(-65536), F32)
    lo = lax.bitcast_convert_type(lax.shift_left(p, 16), F32)
    return hi, lo


def _layer_norm(x, g, b):
    mu = jnp.mean(x, -1, keepdims=True)
    xc = x - mu
    var = jnp.mean(xc * xc, -1, keepdims=True)
    return xc * lax.rsqrt(var + LN_EPS) * g + b


def _rms_norm(x, g):
    return x * lax.rsqrt(jnp.mean(x * x, -1, keepdims=True) + RMS_EPS) * g


def _mod_kernel(c_ref, w_ref, b_ref, o_ref):
    c = c_ref[...]
    a = (c * _sigmoid(c)).astype(BF16)
    o_ref[...] = _dot(a, w_ref[...].astype(BF16)) + b_ref[...]


def _modulation(cvec, w_ada, b_ada, l):
    out = pl.pallas_call(
        _mod_kernel,
        grid=(6,),
        in_specs=[pl.BlockSpec((8, D), lambda j: (0, 0)),
                  pl.BlockSpec((None, D, D), lambda j: (l, 0, j)),
                  pl.BlockSpec((None, 1, D), lambda j: (l, 0, j))],
        out_specs=pl.BlockSpec((8, D), lambda j: (0, j)),
        out_shape=jax.ShapeDtypeStruct((8, 6 * D), F32),
        compiler_params=_cp("arbitrary"),
        name="modulation",
    )(cvec, w_ada, b_ada.reshape(DEPTH, 1, 6 * D))
    return out.reshape(8, 6, D)


def _inproj_kernel(x_ref, mod_ref, w_ref, o_ref, g_ref, h_ref, *, n_main):
    j = pl.program_id(1)

    @pl.when(j == 0)
    def _():
        m = mod_ref[...]
        h_ref[...] = (x_ref[...] * (1.0 + m[1:2, :]) + m[0:1, :]).astype(BF16)

    y = _dot(h_ref[...], w_ref[...])

    @pl.when(j < n_main)
    def _():
        o_ref[...] = y

    @pl.when(j >= n_main)
    def _():
        g_ref[...] = y.astype(BF16)


def _in_proj(x, mod, w_p, mod_row, bm, bn=1024):
    T = x.shape[0]
    n_main = P_GATE // bn
    return pl.pallas_call(
        functools.partial(_inproj_kernel, n_main=n_main),
        grid=(T // bm, N_PROJ // bn),
        in_specs=[pl.BlockSpec((bm, D), lambda i, j: (i, 0)),
                  pl.BlockSpec((None, 6, D), lambda i, j: (mod_row(i), 0, 0)),
                  pl.BlockSpec((D, bn), lambda i, j: (0, j))],
        out_specs=[pl.BlockSpec((bm, bn), lambda i, j: (i, jnp.minimum(j, n_main - 1))),
                   pl.BlockSpec((bm, bn), lambda i, j: (i, jnp.maximum(j - n_main, 0)))],
        out_shape=[jax.ShapeDtypeStruct((T, P_GATE), F32), jax.ShapeDtypeStruct((T, N_PROJ - P_GATE), BF16)],
        scratch_shapes=[pltpu.VMEM((bm, D), BF16)],
        compiler_params=_cp("arbitrary", "arbitrary"),
        name="in_proj",
    )(x, mod, w_p)


def _mla_q_kernel(*refs, rope):
    if rope:
        (ql_ref, ckv_ref, kpe_ref, kper_ref, gq_ref, gkv_ref, wc_ref, wr_ref,
         cq_ref, sq_ref, ck_ref, sk_ref, q_ref, ckvn_ref, kpeo_ref) = refs
    else:
        ql_ref, ckv_ref, kpe_ref, gq_ref, gkv_ref, wc_ref, q_ref, ckvn_ref, kpeo_ref = refs
    qn = _rms_norm(ql_ref[...], gq_ref[...]).astype(BF16)
    q = _dot(qn, wc_ref[...])
    if rope:
        q = q * cq_ref[...] + _dot(qn, wr_ref[...]) * sq_ref[...]
        kpeo_ref[...] = kpe_ref[...] * ck_ref[...] + kper_ref[...] * sk_ref[...]
    else:
        kpeo_ref[...] = kpe_ref[...]
    q_ref[...] = (q * MLA_SCALE).astype(BF16)
    ckvn_ref[...] = _rms_norm(ckv_ref[...], gkv_ref[...])


def _mla_q(proj, gq, gkv, wcat, wrot, tabs, Lb, bm):
    T = proj.shape[0]
    rope = tabs is not None
    nl = Lb // bm
    col = lambda c: (lambda i: (i, c))
    fixed = lambda i: (0, 0)
    in_specs = [pl.BlockSpec((bm, 256), col(P_QLAT // 256)),
                pl.BlockSpec((bm, 128), col(P_CKV // 128)),
                pl.BlockSpec((bm, 128), col(P_KPE // 128))]
    args = [proj, proj, proj]
    if rope:
        in_specs.append(pl.BlockSpec((bm, 128), col(P_KPER // 128)))
        args.append(proj)
    in_specs += [pl.BlockSpec((1, 256), fixed), pl.BlockSpec((1, 128), fixed), pl.BlockSpec((256, 512), fixed)]
    args += [gq, gkv, wcat]
    if rope:
        cq, sq, ck, sk = tabs
        pos = lambda i: (i % nl, 0)
        in_specs += [pl.BlockSpec((256, 512), fixed), pl.BlockSpec((bm, 512), pos), pl.BlockSpec((bm, 512), pos),
                     pl.BlockSpec((bm, 128), pos), pl.BlockSpec((bm, 128), pos)]
        args += [wrot, cq, sq, ck, sk]
    return pl.pallas_call(
        functools.partial(_mla_q_kernel, rope=rope),
        grid=(T // bm,),
        in_specs=in_specs,
        out_specs=[pl.BlockSpec((bm, 512), lambda i: (i, 0)),
                   pl.BlockSpec((bm, 128), lambda i: (i, 0)),
                   pl.BlockSpec((bm, 128), lambda i: (i, 0))],
        out_shape=[jax.ShapeDtypeStruct((T, 512), BF16),
                   jax.ShapeDtypeStruct((T, 128), F32),
                   jax.ShapeDtypeStruct((T, 128), F32)],
        compiler_params=_cp("arbitrary"),
        name="mla_q",
    )(*args)


def _mla_kv_kernel(ckv_ref, kpe_ref, wk_ref, wv_ref, k_ref, v_ref):
    c = ckv_ref[...].astype(BF16)
    kpe = kpe_ref[...]
    k_ref[...] = (_dot(c, wk_ref[...]) + jnp.concatenate([kpe] * 4, axis=1)).astype(BF16)
    v_ref[...] = _dot(c, wv_ref[...]).astype(BF16)


def _mla_kv(ckv, kpe, wk, wv, bm):
    Tk = ckv.shape[0]
    return pl.pallas_call(
        _mla_kv_kernel,
        grid=(Tk // bm,),
        in_specs=[pl.BlockSpec((bm, 128), lambda i: (i, 0)),
                  pl.BlockSpec((bm, 128), lambda i: (i, 0)),
                  pl.BlockSpec((128, 512), lambda i: (0, 0)),
                  pl.BlockSpec((128, 256), lambda i: (0, 0))],
        out_specs=[pl.BlockSpec((bm, 512), lambda i: (i, 0)),
                   pl.BlockSpec((bm, 256), lambda i: (i, 0))],
        out_shape=[jax.ShapeDtypeStruct((Tk, 512), BF16),
                   jax.ShapeDtypeStruct((Tk, 256), BF16)],
        compiler_params=_cp("arbitrary"),
        name="mla_kv",
    )(ckv, kpe, wk, wv)


def _attn_core(q, kvs, masks, sink):
    ss = []
    for (k, _), mk in zip(kvs, masks):
        s = _dot_nt(q, k)
        if mk is not None:
            s = s + mk[1] if mk[0] == "add" else jnp.where(mk[1], s, NEG)
        ss.append(s)
    m = ss[0].max(-1, keepdims=True)
    for s in ss[1:]:
        m = jnp.maximum(m, s.max(-1, keepdims=True))
    if sink is not None:
        m = jnp.maximum(m, sink)
    den = None
    acc = None
    for s, (_, v) in zip(ss, kvs):
        p = jnp.exp(s - m)
        d = p.sum(-1, keepdims=True)
        a = _dot(p.astype(BF16), v)
        den = d if den is None else den + d
        acc = a if acc is None else acc + a
    if sink is not None:
        den = den + jnp.exp(sink - m)
    return acc / den


def _ctx_attn_kernel(qm_ref, km_ref, vm_ref, wq_ref, wk_ref, wv_ref, nq_ref, nk_ref, nv_ref, sink_ref, *rest):
    om_ref, ow_ref, on_ref = rest[-3:]
    for h in range(4):
        q = qm_ref[:, 128 * h:128 * (h + 1)]
        k = km_ref[:, 128 * h:128 * (h + 1)]
        v = vm_ref[:, 64 * h:64 * (h + 1)]
        om_ref[:, 64 * h:64 * (h + 1)] = _attn_core(q, [(k, v)], [None], None)
    for h in range(4):
        g = h // 2
        q = (wq_ref[:, 64 * h:64 * (h + 1)] * ATT_SCALE).astype(BF16)
        k = wk_ref[:, 64 * g:64 * (g + 1)].astype(BF16)
        v = wv_ref[:, 64 * g:64 * (g + 1)].astype(BF16)
        ow_ref[:, 64 * h:64 * (h + 1)] = _attn_core(q, [(k, v)], [None], sink_ref[h])
    for h in range(4):
        q = (nq_ref[:, 64 * h:64 * (h + 1)] * ATT_SCALE).astype(BF16)
        k = nk_ref[:, 64 * h:64 * (h + 1)].astype(BF16)
        v = nv_ref[:, 64 * h:64 * (h + 1)].astype(BF16)
        on_ref[:, 64 * h:64 * (h + 1)] = _attn_core(q, [(k, v)], [None], None)


def _ctx_attention(proj, q_all, k_all, v_all, sink, NB, Lb, after=None):
    T = proj.shape[0]
    pc = lambda w, off: pl.BlockSpec((Lb, w), lambda b: (b, off // w))
    row = lambda w: pl.BlockSpec((Lb, w), lambda b: (b, 0))
    dep_specs, dep_args = _after(after)
    return pl.pallas_call(
        _ctx_attn_kernel,
        grid=(NB,),
        in_specs=[row(512), row(512), row(256),
                  pc(256, P_WQ), pc(128, P_WK), pc(128, P_WV),
                  pc(256, P_NQ), pc(256, P_NK), pc(256, P_NV),
                  pl.BlockSpec(memory_space=pltpu.SMEM)] + dep_specs,
        out_specs=[row(256), row(256), row(256)],
        out_shape=[jax.ShapeDtypeStruct((T, 256), F32)] * 3,
        compiler_params=_cp("arbitrary"),
        name="ctx_attention",
    )(q_all, k_all, v_all, proj, proj, proj, proj, proj, proj, sink, *dep_args)


def _lat_mla_kernel(q_ref, k_ref, v_ref, *rest):
    o_ref = rest[-1]
    for h in range(4):
        q = q_ref[:, 128 * h:128 * (h + 1)]
        k = k_ref[:, 128 * h:128 * (h + 1)]
        v = v_ref[:, 64 * h:64 * (h + 1)]
        o_ref[:, 64 * h:64 * (h + 1)] = _attn_core(q, [(k, v)], [None], None)


def _after(after):
    return ([], []) if after is None else ([pl.BlockSpec(memory_space=pl.ANY)], [after])


def _lat_mla_attention(q_all, k_all, v_all, NB, Lb, Lk, tq, after=None):
    T = q_all.shape[0]
    nq = Lb // tq
    dep_specs, dep_args = _after(after)
    return pl.pallas_call(
        _lat_mla_kernel,
        grid=(NB, nq),
        in_specs=[pl.BlockSpec((tq, 512), lambda b, i: (b * nq + i, 0)),
                  pl.BlockSpec((Lk, 512), lambda b, i: (b, 0)),
                  pl.BlockSpec((Lk, 256), lambda b, i: (b, 0))] + dep_specs,
        out_specs=pl.BlockSpec((tq, 256), lambda b, i: (b * nq + i, 0)),
        out_shape=jax.ShapeDtypeStruct((T, 256), F32),
        compiler_params=_cp("arbitrary", "arbitrary"),
        name="lat_mla_attention",
    )(q_all, k_all, v_all, *dep_args)


def _attn_local_ctx(q, locs, kc, vc, sink):
    s_ctx = _dot_nt(q, kc)
    m_ctx = s_ctx.max(-1, keepdims=True)
    if sink is not None:
        m_ctx = jnp.maximum(m_ctx, sink)
    ms, dens, accs = [], [], []
    for rs, k, v, mk in locs:
        s = _dot_nt(q[rs], k)
        s = s + mk[1] if mk[0] == "add" else jnp.where(mk[1], s, NEG)
        m = jnp.maximum(s.max(-1, keepdims=True), m_ctx[rs])
        p = jnp.exp(s - m)
        ms.append(m)
        dens.append(p.sum(-1, keepdims=True))
        accs.append(_dot(p.astype(BF16), v))
    m = jnp.concatenate(ms, axis=0)
    p = jnp.exp(s_ctx - m)
    den = jnp.concatenate(dens, axis=0) + p.sum(-1, keepdims=True)
    if sink is not None:
        den = den + jnp.exp(sink - m)
    return (jnp.concatenate(accs, axis=0) + _dot(p.astype(BF16), vc)) / den


def _lat_win_kernel(q_ref, qr_ref, k_ref, kr_ref, v_ref, kc_ref, vc_ref, cq_ref, sq_ref, ck_ref, sk_ref,
                    sink_ref, o_ref, *, Lb, bpt):
    t = pl.program_id(1)
    q = (q_ref[...] * cq_ref[...] + qr_ref[...] * sq_ref[...]) * ATT_SCALE
    kc = kc_ref[...].astype(BF16)
    vc = vc_ref[...].astype(BF16)
    blocks = []
    for bb in range(bpt):
        i = t * bpt + bb
        start = pl.multiple_of(jnp.clip((i - 1) * 128, 0, Lb - 384), 128)
        win = pl.ds(start, 384)
        kk = (k_ref[win, :] * ck_ref[win, :] + kr_ref[win, :] * sk_ref[win, :]).astype(BF16)
        qpos = i * 128 + lax.broadcasted_iota(jnp.int32, (128, 384), 0)
        kpos = start + lax.broadcasted_iota(jnp.int32, (128, 384), 1)
        blocks.append((kk, v_ref[win, :].astype(BF16), jnp.abs(qpos - kpos) <= 128))
    for h in range(4):
        g = h // 2
        sl = slice(64 * g, 64 * (g + 1))
        locs = [(slice(128 * bb, 128 * (bb + 1)), kk[:, sl], vv[:, sl], ("keep", valid))
                for bb, (kk, vv, valid) in enumerate(blocks)]
        qh = q[:, 64 * h:64 * (h + 1)].astype(BF16)
        o_ref[:, 64 * h:64 * (h + 1)] = _attn_local_ctx(qh, locs, kc[:, sl], vc[:, sl], sink_ref[h])


def _lat_win_attention(proj, kc, vc, tabs, sink, NB, Lb):
    T = proj.shape[0]
    bpt = 1
    tq = 128 * bpt
    nt = Lb // tq
    Lc = kc.shape[1]
    cq, sq, ck, sk = tabs
    qspec = lambda off: pl.BlockSpec((tq, 256), lambda b, i: (b * nt + i, off // 256))
    kspec = lambda off: pl.BlockSpec((Lb, 128), lambda b, i: (b, off // 128))
    cspec = pl.BlockSpec((None, Lc, 128), lambda b, i: (b, 0, 0))
    return pl.pallas_call(
        functools.partial(_lat_win_kernel, Lb=Lb, bpt=bpt),
        grid=(NB, nt),
        in_specs=[qspec(P_WQ), qspec(P_WQR), kspec(P_WK), kspec(P_WKR), kspec(P_WV), cspec, cspec,
                  pl.BlockSpec((tq, 256), lambda b, i: (i, 0)), pl.BlockSpec((tq, 256), lambda b, i: (i, 0)),
                  pl.BlockSpec((Lb, 128), lambda b, i: (0, 0)), pl.BlockSpec((Lb, 128), lambda b, i: (0, 0)),
                  pl.BlockSpec(memory_space=pltpu.SMEM)],
        out_specs=pl.BlockSpec((tq, 256), lambda b, i: (b * nt + i, 0)),
        out_shape=jax.ShapeDtypeStruct((T, 256), F32),
        compiler_params=_cp("arbitrary", "arbitrary"),
        name="lat_win_attention",
    )(proj, proj, proj, proj, proj, kc, vc, cq, sq, ck, sk, sink)


def _na_bias_kernel(rpb_ref, o_ref):
    h = pl.program_id(0)
    qc = lax.broadcasted_iota(jnp.int32, (GRID_W, GRID_W), 0)
    kc = lax.broadcasted_iota(jnp.int32, (GRID_W, GRID_W), 1)
    dc = kc - qc + (NA_KW - 1)
    wstart = jnp.clip(qc - NA_KW // 2, 0, GRID_W - NA_KW)
    ok = (kc >= wstart) & (kc < wstart + NA_KW)
    n_dc = 2 * NA_KW - 1
    n_dr = 2 * NA_KH - 1
    tabs = []
    for dr in range(n_dr):
        t = jnp.zeros((GRID_W, GRID_W), F32)
        for j in range(n_dc):
            t = jnp.where(dc == j, rpb_ref[(h * n_dr + dr) * n_dc + j], t)
        tabs.append(jnp.where(ok, t, NEG))
    for o in range(NA_KH):
        for a in range(NA_KH):
            o_ref[o, :, GRID_W * a:GRID_W * (a + 1)] = tabs[a + NA_KH - 1 - o]


def _na_bias(rpb):
    H = rpb.shape[0]
    return pl.pallas_call(
        _na_bias_kernel,
        grid=(H,),
        in_specs=[pl.BlockSpec(memory_space=pltpu.SMEM)],
        out_specs=pl.BlockSpec((None, NA_KH, GRID_W, NA_KH * GRID_W), lambda h: (h, 0, 0, 0)),
        out_shape=jax.ShapeDtypeStruct((H, NA_KH, GRID_W, NA_KH * GRID_W), F32),
        compiler_params=_cp("arbitrary"),
        name="na_bias",
    )(rpb.reshape(-1))


def _lat_na_kernel(q_ref, k_ref, v_ref, kc_ref, vc_ref, bias_ref, o_ref, *, rows, rpt):
    t = pl.program_id(1)
    q = q_ref[...] * ATT_SCALE
    kc = kc_ref[...].astype(BF16)
    vc = vc_ref[...].astype(BF16)
    bands = []
    for rr in range(rpt):
        r = t * rpt + rr
        first = jnp.clip(r - NA_KH // 2, 0, rows - NA_KH)
        win = pl.ds(pl.multiple_of(first * GRID_W, GRID_W), NA_KH * GRID_W)
        bands.append((k_ref[win, :].astype(BF16), v_ref[win, :].astype(BF16), r - first))
    for h in range(4):
        sl = slice(64 * h, 64 * (h + 1))
        locs = [(slice(GRID_W * rr, GRID_W * (rr + 1)), kk[:, sl], vv[:, sl], ("add", bias_ref[h, off]))
                for rr, (kk, vv, off) in enumerate(bands)]
        o_ref[:, sl] = _attn_local_ctx(q[:, sl].astype(BF16), locs, kc[:, sl], vc[:, sl], None)


def _lat_na_attention(proj, kc, vc, bias, NB, Lb):
    T = proj.shape[0]
    rows = Lb // GRID_W
    rpt = 8
    tq = GRID_W * rpt
    nt = rows // rpt
    Lc = kc.shape[1]
    kspec = lambda off: pl.BlockSpec((Lb, 256), lambda b, t: (b, off // 256))
    cspec = pl.BlockSpec((None, Lc, 256), lambda b, t: (b, 0, 0))
    return pl.pallas_call(
        functools.partial(_lat_na_kernel, rows=rows, rpt=rpt),
        grid=(NB, nt),
        in_specs=[pl.BlockSpec((tq, 256), lambda b, t: (b * nt + t, P_NQ // 256)),
                  kspec(P_NK), kspec(P_NV), cspec, cspec,
                  pl.BlockSpec((4, NA_KH, GRID_W, NA_KH * GRID_W), lambda b, t: (0, 0, 0, 0))],
        out_specs=pl.BlockSpec((tq, 256), lambda b, t: (b * nt + t, 0)),
        out_shape=jax.ShapeDtypeStruct((T, 256), F32),
        compiler_params=_cp("arbitrary", "arbitrary"),
        name="lat_na_attention",
    )(proj, proj, proj, kc, vc, bias)


def _short_conv_kernel(a_ref, b_ref, c_ref, w_ref, bias_ref, oa_ref, ob_ref, oc_ref, *, L):
    t = lax.broadcasted_iota(jnp.int32, (L, HY_C), 0)
    for n, (x_ref, o_ref) in enumerate(((a_ref, oa_ref), (b_ref, ob_ref), (c_ref, oc_ref))):
        sl = slice(HY_C * n, HY_C * (n + 1))
        x = x_ref[...]
        prev = jnp.where(t == 0, 0.0, pltpu.roll(x, 1, axis=0))
        nxt = jnp.where(t == L - 1, 0.0, pltpu.roll(x, L - 1, axis=0))
        o_ref[...] = prev * w_ref[0:1, sl] + x * w_ref[1:2, sl] + nxt * w_ref[2:3, sl] + bias_ref[:, sl]


def _short_conv(proj, w, b, NB, Lb):
    T = proj.shape[0]
    spec = lambda c: pl.BlockSpec((Lb, HY_C), lambda i: (i, c))
    return pl.pallas_call(
        functools.partial(_short_conv_kernel, L=Lb),
        grid=(NB,),
        in_specs=[spec(P_HY // HY_C), spec(P_HY // HY_C + 1), spec(P_HY // HY_C + 2),
                  pl.BlockSpec((3, 3 * HY_C), lambda i: (0, 0)), pl.BlockSpec((1, 3 * HY_C), lambda i: (0, 0))],
        out_specs=[spec(0)] * 3,
        out_shape=[jax.ShapeDtypeStruct((T, HY_C), F32)] * 3,
        compiler_params=_cp("arbitrary"),
        name="hyena_short_conv",
    )(proj, proj, proj, w, b)


def _hy_filter_kernel(w1_ref, b1_ref, w2_ref, b2_ref, w3_ref, freq_ref, ld_ref, fs_ref, nyq_ref, *, L):
    ti = lax.broadcasted_iota(jnp.int32, (L, 128), 0)
    t = ti.astype(F32)
    j = lax.broadcasted_iota(jnp.int32, (L, 128), 1)
    band = jnp.where(j <= HY_BANDS, j - 1, j - 1 - HY_BANDS).astype(F32)
    ang = (2.0 * math.pi / L) * t * band
    tn = t / L
    z = jnp.where(j == 0, tn, jnp.where(j <= HY_BANDS, jnp.cos(ang),
                                        jnp.where(j <= 2 * HY_BANDS, -jnp.sin(ang), 0.0)))
    a = jnp.sin(freq_ref[0:1, :] * (_dot_hi(z, w1_ref[...]) + b1_ref[...]))
    a = jnp.sin(freq_ref[1:2, :] * (_dot_hi(a, w2_ref[...]) + b2_ref[...]))
    filt = _dot_hi(a, w3_ref[...])
    tcol = lax.broadcasted_iota(jnp.int32, (L, 4 * HY_C), 0)
    filt = filt * jnp.exp(-(tcol.astype(F32) / L) * jnp.exp(ld_ref[...]))
    t1 = lax.broadcasted_iota(jnp.int32, (L, HY_C), 0)
    sign = jnp.where(t1 % 2 == 0, 1.0, -1.0)
    for n in range(2):
        fwd = filt[:, 2 * HY_C * n:2 * HY_C * n + HY_C]
        bwd = jnp.where(t1 == 0, 0.0, filt[:, 2 * HY_C * n + HY_C:2 * HY_C * (n + 1)])
        tot = fwd + bwd
        fs_ref[:, HY_C * n:HY_C * (n + 1)] = tot
        fs_ref[:, 2 * HY_C + HY_C * n:2 * HY_C + HY_C * (n + 1)] = fwd - bwd
        nyq_ref[:, HY_C * n:HY_C * (n + 1)] = (tot * sign).sum(0, keepdims=True)


def _hy_filter(L, w1p, b1, w2, b2, w3, freq, ld):
    full = lambda s: pl.BlockSpec(s, lambda: tuple(0 for _ in s))
    return pl.pallas_call(
        functools.partial(_hy_filter_kernel, L=L),
        in_specs=[full((128, 64)), full((1, 64)), full((64, 64)), full((1, 64)), full((64, 4 * HY_C)),
                  full((2, 64)), full((1, 4 * HY_C))],
        out_specs=[full((L, 4 * HY_C)), full((1, 2 * HY_C))],
        out_shape=[jax.ShapeDtypeStruct((L, 4 * HY_C), F32), jax.ShapeDtypeStruct((1, 2 * HY_C), F32)],
        compiler_params=pltpu.CompilerParams(vmem_limit_bytes=VMEM_LIMIT),
        name="hyena_filter",
    )(w1p, b1, w2, b2, w3, freq, ld)


def _hy_gdft_kernel(cm_ref, sm_ref, fs_ref, nyq_ref, gr_ref, gi_ref, *, tm):
    m = pl.program_id(0)
    f = fs_ref[...].astype(BF16)
    gr_ref[...] = _dot(cm_ref[...], f[:, :2 * HY_C])
    gi = _dot(sm_ref[...], f[:, 2 * HY_C:])
    row = m * tm + lax.broadcasted_iota(jnp.int32, (tm, 2 * HY_C), 0)
    gi_ref[...] = jnp.where(row == 0, nyq_ref[...], gi)


def _hy_gdft(cm, sm, fs, nyq, L, tm):
    return pl.pallas_call(
        functools.partial(_hy_gdft_kernel, tm=tm),
        grid=(L // tm,),
        in_specs=[pl.BlockSpec((tm, L), lambda m: (m, 0)), pl.BlockSpec((tm, L), lambda m: (m, 0)),
                  pl.BlockSpec((L, 4 * HY_C), lambda m: (0, 0)), pl.BlockSpec((1, 2 * HY_C), lambda m: (0, 0))],
        out_specs=[pl.BlockSpec((tm, 2 * HY_C), lambda m: (m, 0))] * 2,
        out_shape=[jax.ShapeDtypeStruct((L, 2 * HY_C), F32)] * 2,
        compiler_params=_cp("arbitrary"),
        name="hyena_filter_dft",
    )(cm, sm, fs, nyq)


def _hy_fwd_kernel(cm_ref, sm_ref, z_ref, gr_ref, gi_ref, yr_ref, yi_ref, *, L, tm, ns):
    m = pl.program_id(1)
    gr = gr_ref[...]
    gi = gi_ref[...]
    row0 = (m * tm + lax.broadcasted_iota(jnp.int32, (tm, HY_C), 0)) == 0
    s = jnp.where(row0, 0.5 / L, 1.0 / L)
    for g in range(ns):
        zb = z_ref[g * L:(g + 1) * L, :].astype(BF16)
        zr = _dot(cm_ref[...], zb)
        zi = _dot(sm_ref[...], zb)
        zigi = zi * gi
        yr_ref[g * tm:(g + 1) * tm, :] = ((zr * gr - jnp.where(row0, 0.0, zigi)) * s).astype(BF16)
        yi_ref[g * tm:(g + 1) * tm, :] = (jnp.where(row0, zigi, zr * gi + zi * gr) * s).astype(BF16)


def _hy_seqs_per_step(NB, Lb, tm):
    ns = max(1, 2048 // Lb) if tm == Lb else 1
    while NB % ns:
        ns //= 2
    return ns


def _hy_fwd(cm, sm, z, gr, gi, n, NB, Lb, tm):
    T = z.shape[0]
    nm = Lb // tm
    ns = _hy_seqs_per_step(NB, Lb, tm)
    return pl.pallas_call(
        functools.partial(_hy_fwd_kernel, L=Lb, tm=tm, ns=ns),
        grid=(NB // ns, nm),
        in_specs=[pl.BlockSpec((tm, Lb), lambda b, m: (m, 0)), pl.BlockSpec((tm, Lb), lambda b, m: (m, 0)),
                  pl.BlockSpec((ns * Lb, HY_C), lambda b, m: (b, 0)),
                  pl.BlockSpec((tm, HY_C), lambda b, m: (m, n)), pl.BlockSpec((tm, HY_C), lambda b, m: (m, n))],
        out_specs=[pl.BlockSpec((ns * tm, HY_C), lambda b, m: (b * nm + m, 0))] * 2,
        out_shape=[jax.ShapeDtypeStruct((T, HY_C), BF16)] * 2,
        compiler_params=_cp("arbitrary", "arbitrary"),
        name="hyena_fwd_dft",
    )(cm, sm, z, gr, gi)


def _hy_inv_kernel(cm_ref, smt_ref, yr_ref, yi_ref, z_ref, g_ref, skip_ref, o_ref, *, L, tm, ns):
    for g in range(ns):
        seq = slice(g * L, (g + 1) * L)
        out = slice(g * tm, (g + 1) * tm)
        conv = _dot(cm_ref[...], yr_ref[seq, :]) + _dot(smt_ref[...], yi_ref[seq, :])
        o_ref[out, :] = g_ref[out, :] * (conv + skip_ref[...] * z_ref[out, :])


def _hy_inv(cm, smt, yr, yi, z, gate, skip, n, NB, Lb, tm):
    T = z.shape[0]
    nm = Lb // tm
    ns = _hy_seqs_per_step(NB, Lb, tm)
    tile = pl.BlockSpec((ns * tm, HY_C), lambda b, m: (b * nm + m, 0))
    seq = pl.BlockSpec((ns * Lb, HY_C), lambda b, m: (b, 0))
    return pl.pallas_call(
        functools.partial(_hy_inv_kernel, L=Lb, tm=tm, ns=ns),
        grid=(NB // ns, nm),
        in_specs=[pl.BlockSpec((tm, Lb), lambda b, m: (m, 0)), pl.BlockSpec((tm, Lb), lambda b, m: (m, 0)),
                  seq, seq, tile, tile, pl.BlockSpec((None, 1, HY_C), lambda b, m: (n, 0, 0))],
        out_specs=tile,
        out_shape=jax.ShapeDtypeStruct((T, HY_C), F32),
        compiler_params=_cp("arbitrary", "arbitrary"),
        name="hyena_inv_dft",
    )(cm, smt, yr, yi, z, gate, skip)


def _dft_mats(L):
    k = jnp.arange(L, dtype=jnp.int32)
    blk = 64

    def trig(mult):
        ang = ((mult[:, None] * k[None, :]) % (2 * L)).astype(F32) * (math.pi / L)
        return jnp.cos(ang), jnp.sin(ang)

    ca, sa = trig(jnp.arange(L // blk, dtype=jnp.int32) * blk)
    cb, sb = trig(jnp.arange(blk, dtype=jnp.int32))
    cm = (ca[:, None, :] * cb[None] - sa[:, None, :] * sb[None]).reshape(L, L)
    s = -(sa[:, None, :] * cb[None] + ca[:, None, :] * sb[None]).reshape(L, L)
    alt = jnp.where(k % 2 == 0, 1.0, -1.0).astype(F32)
    sm = jnp.where(k[:, None] == 0, alt[None, :], s)
    smt = jnp.where(k[None, :] == 0, alt[:, None], s)
    return cm.astype(BF16), sm.astype(BF16), smt.astype(BF16)


def _merge_kernel(oa_ref, ob_ref, oc_ref, od_ref, g0_ref, g1_ref, g2_ref, g3_ref, wb_ref, wo_ref, x_ref, mod_ref,
                  lg_ref, lb_ref, x1_ref, h2_ref, hp_ref):
    acc = None
    for o_ref, g_ref, i in ((oa_ref, g0_ref, 0), (ob_ref, g1_ref, 1), (oc_ref, g2_ref, 2), (od_ref, g3_ref, 3)):
        y = _sigmoid(g_ref[...].astype(F32)) * _dot(o_ref[...].astype(BF16), wb_ref[i])
        acc = y if acc is None else acc + y
    mix = _dot(acc.astype(BF16), wo_ref[...])
    m = mod_ref[...]
    x1 = _layer_norm(DN_ALPHA * x_ref[...] + m[2:3, :] * mix, lg_ref[...], lb_ref[...])
    x1_ref[...] = x1
    h2 = x1 * (1.0 + m[4:5, :]) + m[3:4, :]
    h2_ref[...] = h2
    hp_ref[...] = _pack_pairs(h2)


def _merge(outs, gates, wb, wo, x, mod, lg, lb, mod_row, bm):
    T = x.shape[0]
    row = lambda w: pl.BlockSpec((bm, w), lambda i: (i, 0))
    gspec = lambda n: pl.BlockSpec((bm, D), lambda i: (i, n))
    fixed2 = lambda s: pl.BlockSpec(s, lambda i: (0, 0))
    return pl.pallas_call(
        _merge_kernel,
        grid=(T // bm,),
        in_specs=[row(256)] * 4 + [gspec(0), gspec(1), gspec(2), gspec(3),
                                   pl.BlockSpec((4, 256, D), lambda i: (0, 0, 0)), fixed2((D, D)), row(D),
                                   pl.BlockSpec((None, 6, D), lambda i: (mod_row(i), 0, 0)),
                                   fixed2((1, D)), fixed2((1, D))],
        out_specs=[row(D), row(D), row(D // 2)],
        out_shape=[jax.ShapeDtypeStruct((T, D), F32)] * 2 + [jax.ShapeDtypeStruct((T, D // 2), jnp.int32)],
        compiler_params=_cp("arbitrary"),
        name="merge_norm",
    )(*outs, gates, gates, gates, gates, wb, wo, x, mod, lg, lb)


def _router_kernel(h_ref, rt_ref, bias_ref, g_ref, rank_ref, cnt_ref, *, tt):
    per = N_EXPERTS // N_GROUPS
    logits = lax.dot_general(rt_ref[...], h_ref[...], (((1,), (1,)), ((), ())), preferred_element_type=F32,
                             precision=lax.Precision.HIGHEST)
    scores = _sigmoid(logits)
    sel = (scores + bias_ref[...]).reshape(N_GROUPS, per, tt)
    gid = lax.broadcasted_iota(jnp.int32, (N_GROUPS, per, tt), 0).astype(F32)
    jid = lax.broadcasted_iota(jnp.int32, (N_GROUPS, per, tt), 1).astype(F32)
    eid = gid * per + jid
    ninf = -jnp.inf
    m1 = sel.max(1, keepdims=True)
    i1 = jnp.where(sel == m1, jid, float(per)).min(1, keepdims=True)
    m2 = jnp.where(jid == i1, ninf, sel).max(1, keepdims=True)
    gs = m1 + m2
    g1 = lax.broadcasted_iota(jnp.int32, (N_GROUPS, 1, tt), 0).astype(F32)
    chosen = jnp.zeros((N_GROUPS, 1, tt), F32)
    for _ in range(TOPK_GROUPS):
        mx = gs.max(0, keepdims=True)
        gi = jnp.where(gs == mx, g1, float(N_GROUPS)).min(0, keepdims=True)
        pick = g1 == gi
        chosen = jnp.where(pick, 1.0, chosen)
        gs = jnp.where(pick, ninf, gs)
    cand = jnp.where(chosen > 0.0, sel, NEG)
    picked = jnp.zeros((N_GROUPS, per, tt), F32)
    for _ in range(TOP_K):
        mx = cand.max(1, keepdims=True).max(0, keepdims=True)
        ei = jnp.where(cand == mx, eid, float(N_EXPERTS)).min(1, keepdims=True).min(0, keepdims=True)
        pick = eid == ei
        picked = jnp.where(pick, 1.0, picked)
        cand = jnp.where(pick, ninf, cand)
    w = scores.reshape(N_GROUPS, per, tt) * picked
    wsum = w.sum(1, keepdims=True).sum(0, keepdims=True)
    g_ref[...] = (w / wsum * ROUTED_SCALE).reshape(N_EXPERTS, tt)
    pk = picked.reshape(N_EXPERTS, tt)
    t_in = lax.broadcasted_iota(jnp.int32, (tt, tt), 0)
    t_out = lax.broadcasted_iota(jnp.int32, (tt, tt), 1)
    upper = jnp.where(t_in <= t_out, 1.0, 0.0).astype(BF16)

    @pl.when(pl.program_id(0) == 0)
    def _():
        cnt_ref[...] = jnp.zeros_like(cnt_ref)

    before = cnt_ref[:, 0:1]
    rank_ref[...] = jnp.where(pk > 0.0, before + _dot(pk.astype(BF16), upper) - 1.0, -1.0)
    cnt_ref[...] += pk.sum(-1, keepdims=True)


def _router(h2, router_t, bias, tt):
    T = h2.shape[0]
    tile = pl.BlockSpec((N_EXPERTS, tt), lambda i: (0, i))
    return pl.pallas_call(
        functools.partial(_router_kernel, tt=tt),
        grid=(T // tt,),
        in_specs=[pl.BlockSpec((tt, D), lambda i: (i, 0)), pl.BlockSpec((N_EXPERTS, D), lambda i: (0, 0)),
                  pl.BlockSpec((N_EXPERTS, 1), lambda i: (0, 0))],
        out_specs=[tile, tile, pl.BlockSpec((N_EXPERTS, 128), lambda i: (0, 0))],
        out_shape=[jax.ShapeDtypeStruct((N_EXPERTS, T), F32), jax.ShapeDtypeStruct((N_EXPERTS, T), F32),
                   jax.ShapeDtypeStruct((N_EXPERTS, 128), F32)],
        compiler_params=_cp("arbitrary"),
        name="moe_router",
    )(h2, router_t, bias)


def _route_pos_kernel(gate_ref, rank_ref, cnt_ref, pos_ref, w_ref, te_ref, nx_ref, nt_ref, *, tm, nt_max):
    ei = lax.broadcasted_iota(jnp.int32, (N_EXPERTS, N_EXPERTS), 0)
    ej = lax.broadcasted_iota(jnp.int32, (N_EXPERTS, N_EXPERTS), 1)
    below = jnp.where(ej < ei, 1.0, 0.0)
    padded = jnp.ceil(cnt_ref[...] * (1.0 / tm)) * tm
    offs = _dot_hi(below, padded)
    rank = rank_ref[...]
    routed = rank >= 0.0
    pos = offs[:, 0:1] + rank
    slot = _dot(below.astype(BF16), jnp.where(routed, 1.0, 0.0).astype(BF16))
    gate = gate_ref[...]
    for k in range(TOP_K):
        mine = routed & (slot == float(k))
        pos_ref[k:k + 1, :] = jnp.where(mine, pos, 0.0).sum(0, keepdims=True).astype(jnp.int32)
        w_ref[k:k + 1, :] = jnp.where(mine, gate, 0.0).sum(0, keepdims=True)
    ends = (offs + padded)[:, 0:1]
    first = (lax.broadcasted_iota(jnp.int32, (N_EXPERTS, nt_max), 1) * tm).astype(F32)
    te = jnp.minimum(jnp.where(ends <= first, 1.0, 0.0).sum(0, keepdims=True), N_EXPERTS - 1.0)
    te_ref[...] = te.astype(jnp.int32)
    eid = lax.broadcasted_iota(jnp.int32, (N_EXPERTS, nt_max), 0).astype(F32)
    nx_ref[...] = (jnp.where(eid == te, ends, 0.0).sum(0, keepdims=True) * (1.0 / tm)).astype(jnp.int32)
    nt_ref[...] = (padded.sum(0, keepdims=True) * (1.0 / tm)).astype(jnp.int32)


def _route_pos(gate_t, rank, cnt, tt, tm, nt_max):
    T = gate_t.shape[1]
    tile = pl.BlockSpec((N_EXPERTS, tt), lambda i: (0, i))
    out = pl.BlockSpec((TOP_K, tt), lambda i: (0, i))
    return pl.pallas_call(
        functools.partial(_route_pos_kernel, tm=tm, nt_max=nt_max),
        grid=(T // tt,),
        in_specs=[tile, tile, pl.BlockSpec((N_EXPERTS, 128), lambda i: (0, 0))],
        out_specs=[out, out, pl.BlockSpec((1, nt_max), lambda i: (0, 0)), pl.BlockSpec((1, nt_max), lambda i: (0, 0)),
                   pl.BlockSpec((1, 128), lambda i: (0, 0))],
        out_shape=[jax.ShapeDtypeStruct((TOP_K, T), jnp.int32), jax.ShapeDtypeStruct((TOP_K, T), F32),
                   jax.ShapeDtypeStruct((1, nt_max), jnp.int32), jax.ShapeDtypeStruct((1, nt_max), jnp.int32),
                   jax.ShapeDtypeStruct((1, 128), jnp.int32)],
        compiler_params=_cp("arbitrary"),
        name="moe_positions",
    )(gate_t, rank, cnt)


def _gmm_kernel(te_ref, nx_ref, nt_ref, xs_ref, w1_hbm, w3_hbm, w2_hbm, ys_ref, b1_ref, b3_ref, b2_ref,
                f1_ref, f3_ref, f2_ref, seg_ref, sem, *, l):
    j = pl.program_id(0)
    live = j < nt_ref[0]
    new_expert = (j == 0) | (te_ref[j] != te_ref[jnp.maximum(j - 1, 0)])

    def fetch(e, slot):
        return [pltpu.make_async_copy(w_hbm.at[l, e], f_ref.at[slot], sem.at[i, slot])
                for i, (w_hbm, f_ref) in enumerate(((w1_hbm, f1_ref), (w3_hbm, f3_ref), (w2_hbm, f2_ref)))]

    @pl.when(live & new_expert)
    def _():
        @pl.when(j == 0)
        def _():
            seg_ref[0] = 0
            for c in fetch(te_ref[0], 0):
                c.start()

        slot = lax.rem(seg_ref[0], 2)
        for c in fetch(te_ref[j], slot):
            c.wait()
        b1_ref[...] = f1_ref[slot].astype(BF16)
        b3_ref[...] = f3_ref[slot].astype(BF16)
        b2_ref[...] = f2_ref[slot].astype(BF16)
        nxt = nx_ref[j]

        @pl.when(nxt < nt_ref[0])
        def _():
            for c in fetch(te_ref[nxt], 1 - slot):
                c.start()

        seg_ref[0] = seg_ref[0] + 1

    @pl.when(live)
    def _():
        xa, xb = _unpack_pairs(xs_ref[...])
        xa, xb = xa.astype(BF16), xb.astype(BF16)
        half = D // 2
        a = _dot(xa, b1_ref[:half, :]) + _dot(xb, b1_ref[half:, :])
        b = _dot(xa, b3_ref[:half, :]) + _dot(xb, b3_ref[half:, :])
        hid = (a * _sigmoid(a) * b).astype(BF16)
        ys_ref[...] = _pack_pairs(_dot(hid, b2_ref[...]))


def _gmm(te, nx, nt, xs, w1, w3, w2, l, tm):
    n_slots = xs.shape[0]
    ds = D_EXPERT
    rows = pl.BlockSpec((tm, D // 2), lambda j, te, nx, nt: (jnp.minimum(j, nt[0] - 1), 0))
    hbm = pl.BlockSpec(memory_space=pl.ANY)
    return pl.pallas_call(
        functools.partial(_gmm_kernel, l=l),
        grid_spec=pltpu.PrefetchScalarGridSpec(
            num_scalar_prefetch=3,
            grid=(n_slots // tm,),
            in_specs=[rows, hbm, hbm, hbm],
            out_specs=rows,
            scratch_shapes=[pltpu.VMEM((D, ds), BF16), pltpu.VMEM((D, ds), BF16), pltpu.VMEM((ds, D), BF16),
                            pltpu.VMEM((2, D, ds), F32), pltpu.VMEM((2, D, ds), F32), pltpu.VMEM((2, ds, D), F32),
                            pltpu.SMEM((1,), jnp.int32), pltpu.SemaphoreType.DMA((3, 2))]),
        out_shape=jax.ShapeDtypeStruct((n_slots, D // 2), jnp.int32),
        compiler_params=_cp("arbitrary"),
        name="moe_grouped_ffn",
    )(te, nx, nt, xs, w1, w3, w2)


def _combine_kernel(yk_ref, w_ref, hp_ref, s1_ref, s3_ref, s2_ref, x_ref, mod_ref, lg_ref, lb_ref, o_ref):
    w = w_ref[...]
    acc_a = acc_b = None
    for k in range(TOP_K):
        ya, yb = _unpack_pairs(yk_ref[k])
        wk = w[:, k:k + 1]
        acc_a = wk * ya if acc_a is None else acc_a + wk * ya
        acc_b = wk * yb if acc_b is None else acc_b + wk * yb
    ha, hb = _unpack_pairs(hp_ref[...])
    ha, hb = ha.astype(BF16), hb.astype(BF16)
    half = D // 2
    a = _dot(ha, s1_ref[:half, :]) + _dot(hb, s1_ref[half:, :])
    b = _dot(ha, s3_ref[:half, :]) + _dot(hb, s3_ref[half:, :])
    y = jnp.concatenate([acc_a, acc_b], axis=1) + _dot((a * _sigmoid(a) * b).astype(BF16), s2_ref[...])
    m = mod_ref[...]
    o_ref[...] = _layer_norm(DN_ALPHA * x_ref[...] + m[5:6, :] * y, lg_ref[...], lb_ref[...])


def _combine(yk, w, hp, s1, s3, s2, x1, mod, lg, lb, mod_row, bm):
    T = x1.shape[0]
    ds = D_EXPERT
    row = lambda n: pl.BlockSpec((bm, n), lambda i: (i, 0))
    fixed = lambda s: pl.BlockSpec(s, lambda i: (0, 0))
    return pl.pallas_call(
        _combine_kernel,
        grid=(T // bm,),
        in_specs=[pl.BlockSpec((TOP_K, bm, D // 2), lambda i: (0, i, 0)), row(TOP_K), row(D // 2),
                  fixed((D, ds)), fixed((D, ds)), fixed((ds, D)), row(D),
                  pl.BlockSpec((None, 6, D), lambda i: (mod_row(i), 0, 0)), fixed((1, D)), fixed((1, D))],
        out_specs=row(D),
        out_shape=jax.ShapeDtypeStruct((T, D), F32),
        compiler_params=_cp("arbitrary"),
        name="moe_combine_norm",
    )(yk, w, hp, s1, s3, s2, x1, mod, lg, lb)


def _sc_worker():
    return lax.axis_index("s") * SC_CORES + lax.axis_index("c")


def _sc_mesh():
    return plsc.VectorSubcoreMesh(core_axis_name="c", subcore_axis_name="s")


def _sc_gather(table, idx):
    N, W = idx.shape[0], table.shape[1]
    per_w = N // SC_WORKERS
    n_chunks = per_w // SC_ROWS

    def body(table_hbm, idx_hbm, out_hbm, idx_v, rows_v, sem):
        base = _sc_worker() * per_w
        pltpu.sync_copy(idx_hbm.at[pl.ds(base, per_w)], idx_v)

        @pl.loop(0, n_chunks)
        def _(c):
            off = pl.multiple_of(c * SC_ROWS, SC_ROWS)
            pltpu.async_copy(table_hbm.at[idx_v.at[pl.ds(off, SC_ROWS)]], rows_v, sem).wait()
            pltpu.sync_copy(rows_v, out_hbm.at[pl.ds(base + off, SC_ROWS)])

    return pl.kernel(
        body, out_type=jax.ShapeDtypeStruct((N, W), table.dtype), mesh=_sc_mesh(),
        scratch_types=[pltpu.VMEM((per_w,), jnp.int32), pltpu.VMEM((SC_ROWS, W), table.dtype),
                       pltpu.SemaphoreType.DMA],
        name="sc_gather",
    )(table, idx)


def _sc_dispatch(pos, table, n_slots):
    NP, (T, W) = pos.shape[0], table.shape
    per_w = n_slots // SC_WORKERS
    n_chunks = per_w // SC_ROWS
    scan = 8192

    def body(pos_hbm, table_hbm, out_hbm, pos_v, src_v, rows_v, sem):
        base = _sc_worker() * per_w
        lane = lax.iota(jnp.int32, SC_LANES)

        @pl.loop(0, per_w // SC_LANES)
        def _(j):
            o = pl.multiple_of(j * SC_LANES, SC_LANES)
            src_v[pl.ds(o, SC_LANES)] = (base + o + lane) & (T - 1)

        @pl.loop(0, NP // scan)
        def _(c):
            pltpu.sync_copy(pos_hbm.at[pl.ds(pl.multiple_of(c * scan, scan), scan)], pos_v)

            @pl.loop(0, scan // SC_LANES)
            def _(v):
                o = pl.multiple_of(v * SC_LANES, SC_LANES)
                p = pos_v[pl.ds(o, SC_LANES)] - base
                mine = (p >= 0) & (p < per_w)
                tok = (c * scan + o + lane) & (T - 1)
                plsc.store_scatter(src_v, [jnp.where(mine, p, 0)], tok, mask=mine)

        @pl.loop(0, n_chunks)
        def _(c):
            off = pl.multiple_of(c * SC_ROWS, SC_ROWS)
            pltpu.async_copy(table_hbm.at[src_v.at[pl.ds(off, SC_ROWS)]], rows_v, sem).wait()
            pltpu.sync_copy(rows_v, out_hbm.at[pl.ds(base + off, SC_ROWS)])

    return pl.kernel(
        body, out_type=jax.ShapeDtypeStruct((n_slots, W), table.dtype), mesh=_sc_mesh(),
        scratch_types=[pltpu.VMEM((scan,), jnp.int32), pltpu.VMEM((per_w,), jnp.int32),
                       pltpu.VMEM((SC_ROWS, W), table.dtype), pltpu.SemaphoreType.DMA],
        compiler_params=pltpu.CompilerParams(needs_layout_passes=False),
        name="sc_dispatch",
    )(pos, table)


def _caches_kernel(*refs, nb, S):
    n_in = 6 * DEPTH
    outs = refs[n_in:]
    l = pl.program_id(0)
    for a in range(DEPTH):
        @pl.when(l == a)
        def _(a=a):
            ckv, kpe, wk, wv, nk, nv = refs[6 * a:6 * (a + 1)]
            for g in range(nb):
                rows = slice(g * S, (g + 1) * S)
                outs[0][g] = ckv[rows, :]
                outs[1][g] = kpe[rows, 64:96]
                outs[2][g] = wk[rows, :]
                outs[3][g] = wv[rows, :]
                outs[4][g] = nk[rows, :]
                outs[5][g] = nv[rows, :]


def _emit_caches(projs, ckvs, B, S):
    nb = 4
    while B % nb:
        nb //= 2

    def layer_specs(a):
        row = lambda l, b: jnp.where(l == a, b, 0)
        col = lambda w, off: pl.BlockSpec((nb * S, w), lambda l, b: (row(l, b), off // w))
        return [pl.BlockSpec((nb * S, 128), lambda l, b: (row(l, b), 0)), col(128, P_KPE), col(128, P_WK),
                col(128, P_WV), col(256, P_NK), col(256, P_NV)]

    in_specs, args = [], []
    for a in range(DEPTH):
        in_specs += layer_specs(a)
        args += [ckvs[a]] + [projs[a]] * 5
    widths = (128, 32, 128, 128, 256, 256)
    return pl.pallas_call(
        functools.partial(_caches_kernel, nb=nb, S=S),
        grid=(DEPTH, B // nb),
        in_specs=in_specs,
        out_specs=[pl.BlockSpec((nb, None, S, w), lambda l, b: (b, l, 0, 0)) for w in widths],
        out_shape=[jax.ShapeDtypeStruct((B, DEPTH, S, w), F32) for w in widths],
        compiler_params=_cp("arbitrary", "arbitrary"),
        name="context_tensors",
    )(*args)


def _rot_cols(w, q):
    a, b, c, d = w[..., :q], w[..., q:2 * q], w[..., 2 * q:3 * q], w[..., 3 * q:]
    return jnp.concatenate([-b, a, -d, c], -1)


def _prep_w_in(w):
    z = lambda n: jnp.zeros((D, n), w.dtype)
    qlat, ckv, kpe, hy = w[:, 0:256], w[:, 256:384], w[:, 384:416], w[:, 416:1184]
    wq, wk, wv = w[:, 1184:1440], w[:, 1440:1568], w[:, 1568:1696]
    nq, nk, nv, gate = w[:, 1696:1952], w[:, 1952:2208], w[:, 2208:2464], w[:, 2464:]
    wq_r = _rot_cols(wq.reshape(D, 4, 64), 16).reshape(D, 256)
    wk_r = _rot_cols(wk.reshape(D, 2, 64), 16).reshape(D, 128)
    kpe_r = _rot_cols(kpe, 8)
    cols = [qlat, ckv, z(64), kpe, z(32), hy, wq, wk, wv, nq, nk, nv, wq_r, wk_r, z(64), kpe_r, z(32), gate]
    return jnp.concatenate(cols, 1).astype(BF16)


def _prep_mla(w_uq, w_ukv):
    uq = w_uq.reshape(256, 4, 96)
    nope, pe = uq[..., :64], uq[..., 64:]
    z32 = jnp.zeros((256, 4, 32), w_uq.dtype)
    z64 = jnp.zeros((256, 4, 64), w_uq.dtype)
    wcat = jnp.concatenate([nope, pe, z32], -1).reshape(256, 512).astype(BF16)
    wrot = jnp.concatenate([z64, _rot_cols(pe, 8), z32], -1).reshape(256, 512).astype(BF16)
    ukv = w_ukv.reshape(128, 4, 128)
    wk = jnp.concatenate([ukv[..., :64], jnp.zeros((128, 4, 64), w_ukv.dtype)], -1).reshape(128, 512).astype(BF16)
    wv = ukv[..., 64:].reshape(128, 256).astype(BF16)
    return wcat, wrot, wk, wv


def _rope_tab(L, q):
    t = jnp.arange(L)
    inv = ROPE_BASE ** (-jnp.arange(q, dtype=F32) / q)
    ar = (t // GRID_W).astype(F32)[:, None] * inv[None, :]
    ac = (t % GRID_W).astype(F32)[:, None] * inv[None, :]
    cos = jnp.concatenate([jnp.cos(ar), jnp.cos(ar), jnp.cos(ac), jnp.cos(ac)], 1)
    sin = jnp.concatenate([jnp.sin(ar), jnp.sin(ar), jnp.sin(ac), jnp.sin(ac)], 1)
    return cos, sin


def _rope_tables(L):
    c8, s8 = _rope_tab(L, 8)
    c16, s16 = _rope_tab(L, 16)
    one, zero = jnp.ones((L, 64), F32), jnp.zeros((L, 64), F32)
    z32 = jnp.zeros((L, 32), F32)
    mla_q = (jnp.tile(jnp.concatenate([one, c8, z32], 1), (1, 4)), jnp.tile(jnp.concatenate([zero, s8, z32], 1), (1, 4)))
    mla_k = (jnp.concatenate([zero, c8, z32], 1), jnp.concatenate([zero, s8, z32], 1))
    win = (jnp.tile(c16, (1, 4)), jnp.tile(s16, (1, 4)), jnp.tile(c16, (1, 2)), jnp.tile(s16, (1, 2)))
    return mla_q + mla_k, win


def _hyena(proj, lp, dft, NB, Lb):
    cm, sm, smt = dft
    tm = min(Lb, 512)
    v, x1, x2 = _short_conv(proj, lp["hy_conv_w"], lp["hy_conv_b"].reshape(1, -1), NB, Lb)
    w1p = jnp.pad(lp["hy_w1"], ((0, 128 - lp["hy_w1"].shape[0]), (0, 0)))
    fs, nyq = _hy_filter(Lb, w1p, lp["hy_b1"].reshape(1, -1), lp["hy_w2"], lp["hy_b2"].reshape(1, -1), lp["hy_w3"],
                         lp["hy_sin_freq"], lp["hy_log_decay"].reshape(1, -1))
    gr, gi = _hy_gdft(cm, sm, fs, nyq, Lb, tm)
    skip = lp["hy_skip"].reshape(2, 1, HY_C)
    z = v
    for n, gate in enumerate((x1, x2)):
        yr, yi = _hy_fwd(cm, sm, z, gr, gi, n, NB, Lb, tm)
        z = _hy_inv(cm, smt, yr, yi, z, gate, skip, n, NB, Lb, tm)
    return z


def _layer(x, mod, lp, l, NB, Lb, mod_row_of_batch, dft, cache=None, tabs=None, na_bias=None, after=None):
    T = NB * Lb
    latent = cache is not None
    bm = 256
    rows_of = lambda n: (lambda i: mod_row_of_batch((i * n) // Lb))
    mod_row = rows_of(bm)
    span = Lb if latent else T
    bmp = min(span, 1024)
    proj, gates = _in_proj(x, mod, lp["w_in_p"], rows_of(bmp), bmp)

    gq, gkv = lp["mla_q_norm"].reshape(1, -1), lp["mla_kv_norm"].reshape(1, -1)
    wcat, wrot, wk, wv = lp["mla_w"]
    q_all, ckv_n, kpe_r = _mla_q(proj, gq, gkv, wcat, wrot, tabs[0] if latent else None, Lb, min(span, 512))
    if latent:
        ckv_c, kpe_c, kc_c, vc_c, kd_c, vd_c = cache
        Lc = ckv_c.shape[1]
        kpe_cp = jnp.pad(kpe_c, ((0, 0), (0, 0), (64, 32)))
        ckv_all = jnp.concatenate([ckv_c, ckv_n.reshape(NB, Lb, 128)], 1).reshape(NB * (Lc + Lb), 128)
        kpe_all = jnp.concatenate([kpe_cp, kpe_r.reshape(NB, Lb, 128)], 1).reshape(NB * (Lc + Lb), 128)
        k_all, v_all = _mla_kv(ckv_all, kpe_all, wk, wv, 512)
        oa = _lat_mla_attention(q_all, k_all, v_all, NB, Lb, Lc + Lb, 256, after=after)
        oc = _lat_win_attention(proj, kc_c.reshape(NB, Lc, 128), vc_c.reshape(NB, Lc, 128), tabs[1],
                                lp["win_sink"], NB, Lb)
        od = _lat_na_attention(proj, kd_c.reshape(NB, Lc, 256), vd_c.reshape(NB, Lc, 256), na_bias, NB, Lb)
    else:
        k_all, v_all = _mla_kv(ckv_n, kpe_r, wk, wv, 512)
        oa, oc, od = _ctx_attention(proj, q_all, k_all, v_all, lp["win_sink"], NB, Lb, after=after)
    ob = _hyena(proj, lp, dft, NB, Lb)

    bmm = min(span, 512)
    x1, h2, hp = _merge((oa, ob, oc, od), gates, lp["w_branch_b"], lp["w_out_b"], x, mod,
                        lp["ln1_g"].reshape(1, -1), lp["ln1_b"].reshape(1, -1), rows_of(bmm), bmm)
    n_slots = T * TOP_K + N_EXPERTS * MOE_TM
    gate_t, rank, cnt = _router(h2, lp["moe_router"].T, lp["moe_bias"].reshape(-1, 1), 512)
    pos, w8, te, nx, nt = _route_pos(gate_t, rank, cnt, 512, MOE_TM, n_slots // MOE_TM)
    xs = _sc_dispatch(pos.reshape(-1), hp, n_slots)
    ys = _gmm(te.reshape(-1), nx.reshape(-1), nt.reshape(-1)[:1], xs, lp["moe_w1"], lp["moe_w3"], lp["moe_w2"], l,
              MOE_TM)
    yk = _sc_gather(ys, pos.reshape(-1)).reshape(TOP_K, T, D // 2)
    x2 = _combine(yk, w8.T, hp, lp["sh_w1_b"], lp["sh_w3_b"], lp["sh_w2_b"], x1, mod,
                  lp["ln2_g"].reshape(1, -1), lp["ln2_b"].reshape(1, -1), mod_row, bm)
    return x2, (proj, ckv_n), ys


def kernel(x_prompt, x_sample, cache_mla_ckv, cache_mla_kpe, cache_win_k, cache_win_v, cache_na_k, cache_na_v, c, c_ctx, w_ada, b_ada, w_in, mla_q_norm, mla_kv_norm, mla_w_uq, mla_w_ukv, hy_conv_w, hy_conv_b, hy_w1, hy_b1, hy_w2, hy_b2, hy_w3, hy_sin_freq, hy_log_decay, hy_skip, win_sink, na_rpb, w_branch, w_out, ln1_g, ln1_b, ln2_g, ln2_b, moe_router, moe_bias, moe_w1, moe_w3, moe_w2, sh_w1, sh_w3, sh_w2):
    B, S, _ = x_prompt.shape
    DB, DS, _ = x_sample.shape
    xp = x_prompt.reshape(B * S, D)
    xs = x_sample.reshape(DB * DS, D)
    cvec = jnp.concatenate([c_ctx[None, :], c, jnp.zeros((8 - 1 - DB, D), F32)], 0)
    dft_ctx = _dft_mats(S)
    dft_lat = _dft_mats(DS)
    tabs = _rope_tables(DS)
    projs, ckvs = [], []
    ys_lat = None
    for l in range(DEPTH):
        lp = dict(w_in_p=_prep_w_in(w_in[l]), mla_q_norm=mla_q_norm[l], mla_kv_norm=mla_kv_norm[l],
                  mla_w=_prep_mla(mla_w_uq[l], mla_w_ukv[l]), hy_conv_w=hy_conv_w[l], hy_conv_b=hy_conv_b[l],
                  hy_w1=hy_w1[l], hy_b1=hy_b1[l], hy_w2=hy_w2[l], hy_b2=hy_b2[l], hy_w3=hy_w3[l],
                  hy_sin_freq=hy_sin_freq[l], hy_log_decay=hy_log_decay[l], hy_skip=hy_skip[l],
                  win_sink=win_sink[l], w_branch_b=w_branch[l].astype(BF16), w_out_b=w_out[l].astype(BF16),
                  ln1_g=ln1_g[l], ln1_b=ln1_b[l], ln2_g=ln2_g[l], ln2_b=ln2_b[l],
                  moe_router=moe_router[l], moe_bias=moe_bias[l], moe_w1=moe_w1, moe_w3=moe_w3, moe_w2=moe_w2,
                  sh_w1_b=sh_w1[l].astype(BF16), sh_w3_b=sh_w3[l].astype(BF16), sh_w2_b=sh_w2[l].astype(BF16))
        mod = _modulation(cvec, w_ada, b_ada, l)
        xp, (proj, ckv_n), ys_ctx = _layer(xp, mod, lp, l, B, S, lambda b: 0, dft_ctx, after=ys_lat)
        projs.append(proj)
        ckvs.append(ckv_n)
        cache = (cache_mla_ckv[:, l], cache_mla_kpe[:, l], cache_win_k[:, l], cache_win_v[:, l],
                 cache_na_k[:, l], cache_na_v[:, l])
        xs, _, ys_lat = _layer(xs, mod, lp, l, DB, DS, lambda b: 1 + b, dft_lat, cache=cache, tabs=tabs,
                               na_bias=_na_bias(na_rpb[l]), after=ys_ctx)
    ckv, kpe, wk, wv, nk, nv = _emit_caches(projs, ckvs, B, S)
    heads = lambda t, h: t.reshape(B, DEPTH, S, h, HEAD_DIM)
    return (xp.reshape(B, S, D), xs.reshape(DB, DS, D), ckv, kpe, heads(wk, 2), heads(wv, 2), heads(nk, 4),
            heads(nv, 4))
```

```python
import functools
import math

import jax
import jax.numpy as jnp
from jax import lax
from jax.experimental import pallas as pl
from jax.experimental.pallas import tpu as pltpu
from jax.experimental.pallas import tpu_sc as plsc

F32 = jnp.float32
BF16 = jnp.bfloat16

D = 1024
DEPTH = 2
GRID_W = 64
HEAD_DIM = 64
MLA_SCALE = 96 ** -0.5
ATT_SCALE = HEAD_DIM ** -0.5
HY_C = 256
HY_BANDS = 8
NA_KH = 8
NA_KW = 16
N_EXPERTS = 64
N_GROUPS = 8
TOP_K = 8
TOPK_GROUPS = 4
D_EXPERT = 256
ROUTED_SCALE = 2.5
ROPE_BASE = 10000.0
LN_EPS = 1e-5
RMS_EPS = 1e-6
NEG = -1e30
DN_ALPHA = (2 * DEPTH) ** 0.25

P_QLAT, P_CKV, P_KPE, P_HY = 0, 256, 384, 512
P_WQ, P_WK, P_WV = 1280, 1536, 1664
P_NQ, P_NK, P_NV = 1792, 2048, 2304
P_WQR, P_WKR, P_KPER, P_GATE = 2560, 2816, 2944, 3072
N_PROJ = 7168

VMEM_LIMIT = 56 * 1024 * 1024

SC_CORES = 2
SC_SUBCORES = 16
SC_LANES = 16
SC_WORKERS = SC_CORES * SC_SUBCORES
SC_ROWS = 64

MOE_TM = 512

def _cp(*sem):
    return pltpu.CompilerParams(dimension_semantics=sem, vmem_limit_bytes=VMEM_LIMIT)


def _sigmoid(x):
    return 1.0 / (1.0 + jnp.exp(-x))


def _dot(a, b):
    return jnp.dot(a, b, preferred_element_type=F32)


def _dot_nt(a, b):
    return lax.dot_general(a, b, (((1,), (1,)), ((), ())), preferred_element_type=F32)


def _dot_hi(a, b):
    return jnp.dot(a, b, preferred_element_type=F32, precision=lax.Precision.HIGHEST)


def _pack_pairs(x):
    w = x.shape[1] // 2
    hi = lax.bitcast_convert_type(x[:, :w].astype(BF16).astype(F32), jnp.int32)
    lo = lax.bitcast_convert_type(x[:, w:].astype(BF16).astype(F32), jnp.int32)
    return hi | lax.shift_right_logical(lo, 16)


def _unpack_pairs(p):
    hi = lax.bitcast_convert_type(p & jnp.int32(-65536), F32)
    lo = lax.bitcast_convert_type(lax.shift_left(p, 16), F32)
    return hi, lo


def _layer_norm(x, g, b):
    mu = jnp.mean(x, -1, keepdims=True)
    xc = x - mu
    var = jnp.mean(xc * xc, -1, keepdims=True)
    return xc * lax.rsqrt(var + LN_EPS) * g + b


def _rms_norm(x, g):
    return x * lax.rsqrt(jnp.mean(x * x, -1, keepdims=True) + RMS_EPS) * g


def _mod_kernel(c_ref, w_ref, b_ref, o_ref):
    c = c_ref[...]
    a = (c * _sigmoid(c)).astype(BF16)
    o_ref[...] = _dot(a, w_ref[...].astype(BF16)) + b_ref[...]


def _modulation(cvec, w_ada, b_ada, l):
    out = pl.pallas_call(
        _mod_kernel,
        grid=(6,),
        in_specs=[pl.BlockSpec((8, D), lambda j: (0, 0)),
                  pl.BlockSpec((None, D, D), lambda j: (l, 0, j)),
                  pl.BlockSpec((None, 1, D), lambda j: (l, 0, j))],
        out_specs=pl.BlockSpec((8, D), lambda j: (0, j)),
        out_shape=jax.ShapeDtypeStruct((8, 6 * D), F32),
        compiler_params=_cp("arbitrary"),
        name="modulation",
    )(cvec, w_ada, b_ada.reshape(DEPTH, 1, 6 * D))
    return out.reshape(8, 6, D)


def _inproj_kernel(x_ref, mod_ref, w_ref, o_ref, g_ref, h_ref, *, n_main):
    j = pl.program_id(1)

    @pl.when(j == 0)
    def _():
        m = mod_ref[...]
        h_ref[...] = (x_ref[...] * (1.0 + m[1:2, :]) + m[0:1, :]).astype(BF16)

    y = _dot(h_ref[...], w_ref[...])

    @pl.when(j < n_main)
    def _():
        o_ref[...] = y

    @pl.when(j >= n_main)
    def _():
        g_ref[...] = y.astype(BF16)


def _in_proj(x, mod, w_p, mod_row, bm, bn=1024):
    T = x.shape[0]
    n_main = P_GATE // bn
    return pl.pallas_call(
        functools.partial(_inproj_kernel, n_main=n_main),
        grid=(T // bm, N_PROJ // bn),
        in_specs=[pl.BlockSpec((bm, D), lambda i, j: (i, 0)),
                  pl.BlockSpec((None, 6, D), lambda i, j: (mod_row(i), 0, 0)),
                  pl.BlockSpec((D, bn), lambda i, j: (0, j))],
        out_specs=[pl.BlockSpec((bm, bn), lambda i, j: (i, jnp.minimum(j, n_main - 1))),
                   pl.BlockSpec((bm, bn), lambda i, j: (i, jnp.maximum(j - n_main, 0)))],
        out_shape=[jax.ShapeDtypeStruct((T, P_GATE), F32), jax.ShapeDtypeStruct((T, N_PROJ - P_GATE), BF16)],
        scratch_shapes=[pltpu.VMEM((bm, D), BF16)],
        compiler_params=_cp("arbitrary", "arbitrary"),
        name="in_proj",
    )(x, mod, w_p)


def _mla_q_kernel(*refs, rope):
    if rope:
        (ql_ref, ckv_ref, kpe_ref, kper_ref, gq_ref, gkv_ref, wc_ref, wr_ref,
         cq_ref, sq_ref, ck_ref, sk_ref, q_ref, ckvn_ref, kpeo_ref) = refs
    else:
        ql_ref, ckv_ref, kpe_ref, gq_ref, gkv_ref, wc_ref, q_ref, ckvn_ref, kpeo_ref = refs
    qn = _rms_norm(ql_ref[...], gq_ref[...]).astype(BF16)
    q = _dot(qn, wc_ref[...])
    if rope:
        q = q * cq_ref[...] + _dot(qn, wr_ref[...]) * sq_ref[...]
        kpeo_ref[...] = kpe_ref[...] * ck_ref[...] + kper_ref[...] * sk_ref[...]
    else:
        kpeo_ref[...] = kpe_ref[...]
    q_ref[...] = (q * MLA_SCALE).astype(BF16)
    ckvn_ref[...] = _rms_norm(ckv_ref[...], gkv_ref[...])


def _mla_q(proj, gq, gkv, wcat, wrot, tabs, Lb, bm):
    T = proj.shape[0]
    rope = tabs is not None
    nl = Lb // bm
    col = lambda c: (lambda i: (i, c))
    fixed = lambda i: (0, 0)
    in_specs = [pl.BlockSpec((bm, 256), col(P_QLAT // 256)),
                pl.BlockSpec((bm, 128), col(P_CKV // 128)),
                pl.BlockSpec((bm, 128), col(P_KPE // 128))]
    args = [proj, proj, proj]
    if rope:
        in_specs.append(pl.BlockSpec((bm, 128), col(P_KPER // 128)))
        args.append(proj)
    in_specs += [pl.BlockSpec((1, 256), fixed), pl.BlockSpec((1, 128), fixed), pl.BlockSpec((256, 512), fixed)]
    args += [gq, gkv, wcat]
    if rope:
        cq, sq, ck, sk = tabs
        pos = lambda i: (i % nl, 0)
        in_specs += [pl.BlockSpec((256, 512), fixed), pl.BlockSpec((bm, 512), pos), pl.BlockSpec((bm, 512), pos),
                     pl.BlockSpec((bm, 128), pos), pl.BlockSpec((bm, 128), pos)]
        args += [wrot, cq, sq, ck, sk]
    return pl.pallas_call(
        functools.partial(_mla_q_kernel, rope=rope),
        grid=(T // bm,),
        in_specs=in_specs,
        out_specs=[pl.BlockSpec((bm, 512), lambda i: (i, 0)),
                   pl.BlockSpec((bm, 128), lambda i: (i, 0)),
                   pl.BlockSpec((bm, 128), lambda i: (i, 0))],
        out_shape=[jax.ShapeDtypeStruct((T, 512), BF16),
                   jax.ShapeDtypeStruct((T, 128), F32),
                   jax.ShapeDtypeStruct((T, 128), F32)],
        compiler_params=_cp("arbitrary"),
        name="mla_q",
    )(*args)


def _mla_kv_kernel(ckv_ref, kpe_ref, wk_ref, wv_ref, k_ref, v_ref):
    c = ckv_ref[...].astype(BF16)
    kpe = kpe_ref[...]
    k_ref[...] = (_dot(c, wk_ref[...]) + jnp.concatenate([kpe] * 4, axis=1)).astype(BF16)
    v_ref[...] = _dot(c, wv_ref[...]).astype(BF16)


def _mla_kv(ckv, kpe, wk, wv, bm):
    Tk = ckv.shape[0]
    return pl.pallas_call(
        _mla_kv_kernel,
        grid=(Tk // bm,),
        in_specs=[pl.BlockSpec((bm, 128), lambda i: (i, 0)),
                  pl.BlockSpec((bm, 128), lambda i: (i, 0)),
                  pl.BlockSpec((128, 512), lambda i: (0, 0)),
                  pl.BlockSpec((128, 256), lambda i: (0, 0))],
        out_specs=[pl.BlockSpec((bm, 512), lambda i: (i, 0)),
                   pl.BlockSpec((bm, 256), lambda i: (i, 0))],
        out_shape=[jax.ShapeDtypeStruct((Tk, 512), BF16),
                   jax.ShapeDtypeStruct((Tk, 256), BF16)],
        compiler_params=_cp("arbitrary"),
        name="mla_kv",
    )(ckv, kpe, wk, wv)


def _attn_core(q, kvs, masks, sink):
    ss = []
    for (k, _), mk in zip(kvs, masks):
        s = _dot_nt(q, k)
        if mk is not None:
            s = s + mk[1] if mk[0] == "add" else jnp.where(mk[1], s, NEG)
        ss.append(s)
    m = ss[0].max(-1, keepdims=True)
    for s in ss[1:]:
        m = jnp.maximum(m, s.max(-1, keepdims=True))
    if sink is not None:
        m = jnp.maximum(m, sink)
    den = None
    acc = None
    for s, (_, v) in zip(ss, kvs):
        p = jnp.exp(s - m)
        d = p.sum(-1, keepdims=True)
        a = _dot(p.astype(BF16), v)
        den = d if den is None else den + d
        acc = a if acc is None else acc + a
    if sink is not None:
        den = den + jnp.exp(sink - m)
    return acc / den


def _ctx_attn_kernel(qm_ref, km_ref, vm_ref, wq_ref, wk_ref, wv_ref, nq_ref, nk_ref, nv_ref, sink_ref, *rest):
    om_ref, ow_ref, on_ref = rest[-3:]
    for h in range(4):
        q = qm_ref[:, 128 * h:128 * (h + 1)]
        k = km_ref[:, 128 * h:128 * (h + 1)]
        v = vm_ref[:, 64 * h:64 * (h + 1)]
        om_ref[:, 64 * h:64 * (h + 1)] = _attn_core(q, [(k, v)], [None], None)
    for h in range(4):
        g = h // 2
        q = (wq_ref[:, 64 * h:64 * (h + 1)] * ATT_SCALE).astype(BF16)
        k = wk_ref[:, 64 * g:64 * (g + 1)].astype(BF16)
        v = wv_ref[:, 64 * g:64 * (g + 1)].astype(BF16)
        ow_ref[:, 64 * h:64 * (h + 1)] = _attn_core(q, [(k, v)], [None], sink_ref[h])
    for h in range(4):
        q = (nq_ref[:, 64 * h:64 * (h + 1)] * ATT_SCALE).astype(BF16)
        k = nk_ref[:, 64 * h:64 * (h + 1)].astype(BF16)
        v = nv_ref[:, 64 * h:64 * (h + 1)].astype(BF16)
        on_ref[:, 64 * h:64 * (h + 1)] = _attn_core(q, [(k, v)], [None], None)


def _ctx_attention(proj, q_all, k_all, v_all, sink, NB, Lb, after=None):
    T = proj.shape[0]
    pc = lambda w, off: pl.BlockSpec((Lb, w), lambda b: (b, off // w))
    row = lambda w: pl.BlockSpec((Lb, w), lambda b: (b, 0))
    dep_specs, dep_args = _after(after)
    return pl.pallas_call(
        _ctx_attn_kernel,
        grid=(NB,),
        in_specs=[row(512), row(512), row(256),
                  pc(256, P_WQ), pc(128, P_WK), pc(128, P_WV),
                  pc(256, P_NQ), pc(256, P_NK), pc(256, P_NV),
                  pl.BlockSpec(memory_space=pltpu.SMEM)] + dep_specs,
        out_specs=[row(256), row(256), row(256)],
        out_shape=[jax.ShapeDtypeStruct((T, 256), F32)] * 3,
        compiler_params=_cp("arbitrary"),
        name="ctx_attention",
    )(q_all, k_all, v_all, proj, proj, proj, proj, proj, proj, sink, *dep_args)


def _lat_mla_kernel(q_ref, k_ref, v_ref, *rest):
    o_ref = rest[-1]
    for h in range(4):
        q = q_ref[:, 128 * h:128 * (h + 1)]
        k = k_ref[:, 128 * h:128 * (h + 1)]
        v = v_ref[:, 64 * h:64 * (h + 1)]
        o_ref[:, 64 * h:64 * (h + 1)] = _attn_core(q, [(k, v)], [None], None)


def _after(after):
    return ([], []) if after is None else ([pl.BlockSpec(memory_space=pl.ANY)], [after])


def _lat_mla_attention(q_all, k_all, v_all, NB, Lb, Lk, tq, after=None):
    T = q_all.shape[0]
    nq = Lb // tq
    dep_specs, dep_args = _after(after)
    return pl.pallas_call(
        _lat_mla_kernel,
        grid=(NB, nq),
        in_specs=[pl.BlockSpec((tq, 512), lambda b, i: (b * nq + i, 0)),
                  pl.BlockSpec((Lk, 512), lambda b, i: (b, 0)),
                  pl.BlockSpec((Lk, 256), lambda b, i: (b, 0))] + dep_specs,
        out_specs=pl.BlockSpec((tq, 256), lambda b, i: (b * nq + i, 0)),
        out_shape=jax.ShapeDtypeStruct((T, 256), F32),
        compiler_params=_cp("arbitrary", "arbitrary"),
        name="lat_mla_attention",
    )(q_all, k_all, v_all, *dep_args)


def _attn_local_ctx(q, locs, kc, vc, sink):
    s_ctx = _dot_nt(q, kc)
    m_ctx = s_ctx.max(-1, keepdims=True)
    if sink is not None:
        m_ctx = jnp.maximum(m_ctx, sink)
    ms, dens, accs = [], [], []
    for rs, k, v, mk in locs:
        s = _dot_nt(q[rs], k)
        s = s + mk[1] if mk[0] == "add" else jnp.where(mk[1], s, NEG)
        m = jnp.maximum(s.max(-1, keepdims=True), m_ctx[rs])
        p = jnp.exp(s - m)
        ms.append(m)
        dens.append(p.sum(-1, keepdims=True))
        accs.append(_dot(p.astype(BF16), v))
    m = jnp.concatenate(ms, axis=0)
    p = jnp.exp(s_ctx - m)
    den = jnp.concatenate(dens, axis=0) + p.sum(-1, keepdims=True)
    if sink is not None:
        den = den + jnp.exp(sink - m)
    return (jnp.concatenate(accs, axis=0) + _dot(p.astype(BF16), vc)) / den


def _lat_win_kernel(q_ref, qr_ref, k_ref, kr_ref, v_ref, kc_ref, vc_ref, cq_ref, sq_ref, ck_ref, sk_ref,
                    sink_ref, o_ref, *, Lb, bpt):
    t = pl.program_id(1)
    q = (q_ref[...] * cq_ref[...] + qr_ref[...] * sq_ref[...]) * ATT_SCALE
    kc = kc_ref[...].astype(BF16)
    vc = vc_ref[...].astype(BF16)
    blocks = []
    for bb in range(bpt):
        i = t * bpt + bb
        start = pl.multiple_of(jnp.clip((i - 1) * 128, 0, Lb - 384), 128)
        win = pl.ds(start, 384)
        kk = (k_ref[win, :] * ck_ref[win, :] + kr_ref[win, :] * sk_ref[win, :]).astype(BF16)
        qpos = i * 128 + lax.broadcasted_iota(jnp.int32, (128, 384), 0)
        kpos = start + lax.broadcasted_iota(jnp.int32, (128, 384), 1)
        blocks.append((kk, v_ref[win, :].astype(BF16), jnp.abs(qpos - kpos) <= 128))
    for h in range(4):
        g = h // 2
        sl = slice(64 * g, 64 * (g + 1))
        locs = [(slice(128 * bb, 128 * (bb + 1)), kk[:, sl], vv[:, sl], ("keep", valid))
                for bb, (kk, vv, valid) in enumerate(blocks)]
        qh = q[:, 64 * h:64 * (h + 1)].astype(BF16)
        o_ref[:, 64 * h:64 * (h + 1)] = _attn_local_ctx(qh, locs, kc[:, sl], vc[:, sl], sink_ref[h])


def _lat_win_attention(proj, kc, vc, tabs, sink, NB, Lb):
    T = proj.shape[0]
    bpt = 1
    tq = 128 * bpt
    nt = Lb // tq
    Lc = kc.shape[1]
    cq, sq, ck, sk = tabs
    qspec = lambda off: pl.BlockSpec((tq, 256), lambda b, i: (b * nt + i, off // 256))
    kspec = lambda off: pl.BlockSpec((Lb, 128), lambda b, i: (b, off // 128))
    cspec = pl.BlockSpec((None, Lc, 128), lambda b, i: (b, 0, 0))
    return pl.pallas_call(
        functools.partial(_lat_win_kernel, Lb=Lb, bpt=bpt),
        grid=(NB, nt),
        in_specs=[qspec(P_WQ), qspec(P_WQR), kspec(P_WK), kspec(P_WKR), kspec(P_WV), cspec, cspec,
                  pl.BlockSpec((tq, 256), lambda b, i: (i, 0)), pl.BlockSpec((tq, 256), lambda b, i: (i, 0)),
                  pl.BlockSpec((Lb, 128), lambda b, i: (0, 0)), pl.BlockSpec((Lb, 128), lambda b, i: (0, 0)),
                  pl.BlockSpec(memory_space=pltpu.SMEM)],
        out_specs=pl.BlockSpec((tq, 256), lambda b, i: (b * nt + i, 0)),
        out_shape=jax.ShapeDtypeStruct((T, 256), F32),
        compiler_params=_cp("arbitrary", "arbitrary"),
        name="lat_win_attention",
    )(proj, proj, proj, proj, proj, kc, vc, cq, sq, ck, sk, sink)


def _na_bias_kernel(rpb_ref, o_ref):
    h = pl.program_id(0)
    qc = lax.broadcasted_iota(jnp.int32, (GRID_W, GRID_W), 0)
    kc = lax.broadcasted_iota(jnp.int32, (GRID_W, GRID_W), 1)
    dc = kc - qc + (NA_KW - 1)
    wstart = jnp.clip(qc - NA_KW // 2, 0, GRID_W - NA_KW)
    ok = (kc >= wstart) & (kc < wstart + NA_KW)
    n_dc = 2 * NA_KW - 1
    n_dr = 2 * NA_KH - 1
    tabs = []
    for dr in range(n_dr):
        t = jnp.zeros((GRID_W, GRID_W), F32)
        for j in range(n_dc):
            t = jnp.where(dc == j, rpb_ref[(h * n_dr + dr) * n_dc + j], t)
        tabs.append(jnp.where(ok, t, NEG))
    for o in range(NA_KH):
        for a in range(NA_KH):
            o_ref[o, :, GRID_W * a:GRID_W * (a + 1)] = tabs[a + NA_KH - 1 - o]


def _na_bias(rpb):
    H = rpb.shape[0]
    return pl.pallas_call(
        _na_bias_kernel,
        grid=(H,),
        in_specs=[pl.BlockSpec(memory_space=pltpu.SMEM)],
        out_specs=pl.BlockSpec((None, NA_KH, GRID_W, NA_KH * GRID_W), lambda h: (h, 0, 0, 0)),
        out_shape=jax.ShapeDtypeStruct((H, NA_KH, GRID_W, NA_KH * GRID_W), F32),
        compiler_params=_cp("arbitrary"),
        name="na_bias",
    )(rpb.reshape(-1))


def _lat_na_kernel(q_ref, k_ref, v_ref, kc_ref, vc_ref, bias_ref, o_ref, *, rows, rpt):
    t = pl.program_id(1)
    q = q_ref[...] * ATT_SCALE
    kc = kc_ref[...].astype(BF16)
    vc = vc_ref[...].astype(BF16)
    bands = []
    for rr in range(rpt):
        r = t * rpt + rr
        first = jnp.clip(r - NA_KH // 2, 0, rows - NA_KH)
        win = pl.ds(pl.multiple_of(first * GRID_W, GRID_W), NA_KH * GRID_W)
        bands.append((k_ref[win, :].astype(BF16), v_ref[win, :].astype(BF16), r - first))
    for h in range(4):
        sl = slice(64 * h, 64 * (h + 1))
        locs = [(slice(GRID_W * rr, GRID_W * (rr + 1)), kk[:, sl], vv[:, sl], ("add", bias_ref[h, off]))
                for rr, (kk, vv, off) in enumerate(bands)]
        o_ref[:, sl] = _attn_local_ctx(q[:, sl].astype(BF16), locs, kc[:, sl], vc[:, sl], None)


def _lat_na_attention(proj, kc, vc, bias, NB, Lb):
    T = proj.shape[0]
    rows = Lb // GRID_W
    rpt = 8
    tq = GRID_W * rpt
    nt = rows // rpt
    Lc = kc.shape[1]
    kspec = lambda off: pl.BlockSpec((Lb, 256), lambda b, t: (b, off // 256))
    cspec = pl.BlockSpec((None, Lc, 256), lambda b, t: (b, 0, 0))
    return pl.pallas_call(
        functools.partial(_lat_na_kernel, rows=rows, rpt=rpt),
        grid=(NB, nt),
        in_specs=[pl.BlockSpec((tq, 256), lambda b, t: (b * nt + t, P_NQ // 256)),
                  kspec(P_NK), kspec(P_NV), cspec, cspec,
                  pl.BlockSpec((4, NA_KH, GRID_W, NA_KH * GRID_W), lambda b, t: (0, 0, 0, 0))],
        out_specs=pl.BlockSpec((tq, 256), lambda b, t: (b * nt + t, 0)),
        out_shape=jax.ShapeDtypeStruct((T, 256), F32),
        compiler_params=_cp("arbitrary", "arbitrary"),
        name="lat_na_attention",
    )(proj, proj, proj, kc, vc, bias)


def _short_conv_kernel(a_ref, b_ref, c_ref, w_ref, bias_ref, oa_ref, ob_ref, oc_ref, *, L):
    t = lax.broadcasted_iota(jnp.int32, (L, HY_C), 0)
    for n, (x_ref, o_ref) in enumerate(((a_ref, oa_ref), (b_ref, ob_ref), (c_ref, oc_ref))):
        sl = slice(HY_C * n, HY_C * (n + 1))
        x = x_ref[...]
        prev = jnp.where(t == 0, 0.0, pltpu.roll(x, 1, axis=0))
        nxt = jnp.where(t == L - 1, 0.0, pltpu.roll(x, L - 1, axis=0))
        o_ref[...] = prev * w_ref[0:1, sl] + x * w_ref[1:2, sl] + nxt * w_ref[2:3, sl] + bias_ref[:, sl]


def _short_conv(proj, w, b, NB, Lb):
    T = proj.shape[0]
    spec = lambda c: pl.BlockSpec((Lb, HY_C), lambda i: (i, c))
    return pl.pallas_call(
        functools.partial(_short_conv_kernel, L=Lb),
        grid=(NB,),
        in_specs=[spec(P_HY // HY_C), spec(P_HY // HY_C + 1), spec(P_HY // HY_C + 2),
                  pl.BlockSpec((3, 3 * HY_C), lambda i: (0, 0)), pl.BlockSpec((1, 3 * HY_C), lambda i: (0, 0))],
        out_specs=[spec(0)] * 3,
        out_shape=[jax.ShapeDtypeStruct((T, HY_C), F32)] * 3,
        compiler_params=_cp("arbitrary"),
        name="hyena_short_conv",
    )(proj, proj, proj, w, b)


def _hy_filter_kernel(w1_ref, b1_ref, w2_ref, b2_ref, w3_ref, freq_ref, ld_ref, fs_ref, nyq_ref, *, L):
    ti = lax.broadcasted_iota(jnp.int32, (L, 128), 0)
    t = ti.astype(F32)
    j = lax.broadcasted_iota(jnp.int32, (L, 128), 1)
    band = jnp.where(j <= HY_BANDS, j - 1, j - 1 - HY_BANDS).astype(F32)
    ang = (2.0 * math.pi / L) * t * band
    tn = t / L
    z = jnp.where(j == 0, tn, jnp.where(j <= HY_BANDS, jnp.cos(ang),
                                        jnp.where(j <= 2 * HY_BANDS, -jnp.sin(ang), 0.0)))
    a = jnp.sin(freq_ref[0:1, :] * (_dot_hi(z, w1_ref[...]) + b1_ref[...]))
    a = jnp.sin(freq_ref[1:2, :] * (_dot_hi(a, w2_ref[...]) + b2_ref[...]))
    filt = _dot_hi(a, w3_ref[...])
    tcol = lax.broadcasted_iota(jnp.int32, (L, 4 * HY_C), 0)
    filt = filt * jnp.exp(-(tcol.astype(F32) / L) * jnp.exp(ld_ref[...]))
    t1 = lax.broadcasted_iota(jnp.int32, (L, HY_C), 0)
    sign = jnp.where(t1 % 2 == 0, 1.0, -1.0)
    for n in range(2):
        fwd = filt[:, 2 * HY_C * n:2 * HY_C * n + HY_C]
        bwd = jnp.where(t1 == 0, 0.0, filt[:, 2 * HY_C * n + HY_C:2 * HY_C * (n + 1)])
        tot = fwd + bwd
        fs_ref[:, HY_C * n:HY_C * (n + 1)] = tot
        fs_ref[:, 2 * HY_C + HY_C * n:2 * HY_C + HY_C * (n + 1)] = fwd - bwd
        nyq_ref[:, HY_C * n:HY_C * (n + 1)] = (tot * sign).sum(0, keepdims=True)


def _hy_filter(L, w1p, b1, w2, b2, w3, freq, ld):
    full = lambda s: pl.BlockSpec(s, lambda: tuple(0 for _ in s))
    return pl.pallas_call(
        functools.partial(_hy_filter_kernel, L=L),
        in_specs=[full((128, 64)), full((1, 64)), full((64, 64)), full((1, 64)), full((64, 4 * HY_C)),
                  full((2, 64)), full((1, 4 * HY_C))],
        out_specs=[full((L, 4 * HY_C)), full((1, 2 * HY_C))],
        out_shape=[jax.ShapeDtypeStruct((L, 4 * HY_C), F32), jax.ShapeDtypeStruct((1, 2 * HY_C), F32)],
        compiler_params=pltpu.CompilerParams(vmem_limit_bytes=VMEM_LIMIT),
        name="hyena_filter",
    )(w1p, b1, w2, b2, w3, freq, ld)


def _hy_gdft_kernel(cm_ref, sm_ref, fs_ref, nyq_ref, gr_ref, gi_ref, *, tm):
    m = pl.program_id(0)
    f = fs_ref[...].astype(BF16)
    gr_ref[...] = _dot(cm_ref[...], f[:, :2 * HY_C])
    gi = _dot(sm_ref[...], f[:, 2 * HY_C:])
    row = m * tm + lax.broadcasted_iota(jnp.int32, (tm, 2 * HY_C), 0)
    gi_ref[...] = jnp.where(row == 0, nyq_ref[...], gi)


def _hy_gdft(cm, sm, fs, nyq, L, tm):
    return pl.pallas_call(
        functools.partial(_hy_gdft_kernel, tm=tm),
        grid=(L // tm,),
        in_specs=[pl.BlockSpec((tm, L), lambda m: (m, 0)), pl.BlockSpec((tm, L), lambda m: (m, 0)),
                  pl.BlockSpec((L, 4 * HY_C), lambda m: (0, 0)), pl.BlockSpec((1, 2 * HY_C), lambda m: (0, 0))],
        out_specs=[pl.BlockSpec((tm, 2 * HY_C), lambda m: (m, 0))] * 2,
        out_shape=[jax.ShapeDtypeStruct((L, 2 * HY_C), F32)] * 2,
        compiler_params=_cp("arbitrary"),
        name="hyena_filter_dft",
    )(cm, sm, fs, nyq)


def _hy_fwd_kernel(cm_ref, sm_ref, z_ref, gr_ref, gi_ref, yr_ref, yi_ref, *, L, tm, ns):
    m = pl.program_id(1)
    gr = gr_ref[...]
    gi = gi_ref[...]
    row0 = (m * tm + lax.broadcasted_iota(jnp.int32, (tm, HY_C), 0)) == 0
    s = jnp.where(row0, 0.5 / L, 1.0 / L)
    for g in range(ns):
        zb = z_ref[g * L:(g + 1) * L, :].astype(BF16)
        zr = _dot(cm_ref[...], zb)
        zi = _dot(sm_ref[...], zb)
        zigi = zi * gi
        yr_ref[g * tm:(g + 1) * tm, :] = ((zr * gr - jnp.where(row0, 0.0, zigi)) * s).astype(BF16)
        yi_ref[g * tm:(g + 1) * tm, :] = (jnp.where(row0, zigi, zr * gi + zi * gr) * s).astype(BF16)


def _hy_seqs_per_step(NB, Lb, tm):
    ns = max(1, 2048 // Lb) if tm == Lb else 1
    while NB % ns:
        ns //= 2
    return ns


def _hy_fwd(cm, sm, z, gr, gi, n, NB, Lb, tm):
    T = z.shape[0]
    nm = Lb // tm
    ns = _hy_seqs_per_step(NB, Lb, tm)
    return pl.pallas_call(
        functools.partial(_hy_fwd_kernel, L=Lb, tm=tm, ns=ns),
        grid=(NB // ns, nm),
        in_specs=[pl.BlockSpec((tm, Lb), lambda b, m: (m, 0)), pl.BlockSpec((tm, Lb), lambda b, m: (m, 0)),
                  pl.BlockSpec((ns * Lb, HY_C), lambda b, m: (b, 0)),
                  pl.BlockSpec((tm, HY_C), lambda b, m: (m, n)), pl.BlockSpec((tm, HY_C), lambda b, m: (m, n))],
        out_specs=[pl.BlockSpec((ns * tm, HY_C), lambda b, m: (b * nm + m, 0))] * 2,
        out_shape=[jax.ShapeDtypeStruct((T, HY_C), BF16)] * 2,
        compiler_params=_cp("arbitrary", "arbitrary"),
        name="hyena_fwd_dft",
    )(cm, sm, z, gr, gi)


def _hy_inv_kernel(cm_ref, smt_ref, yr_ref, yi_ref, z_ref, g_ref, skip_ref, o_ref, *, L, tm, ns):
    for g in range(ns):
        seq = slice(g * L, (g + 1) * L)
        out = slice(g * tm, (g + 1) * tm)
        conv = _dot(cm_ref[...], yr_ref[seq, :]) + _dot(smt_ref[...], yi_ref[seq, :])
        o_ref[out, :] = g_ref[out, :] * (conv + skip_ref[...] * z_ref[out, :])


def _hy_inv(cm, smt, yr, yi, z, gate, skip, n, NB, Lb, tm):
    T = z.shape[0]
    nm = Lb // tm
    ns = _hy_seqs_per_step(NB, Lb, tm)
    tile = pl.BlockSpec((ns * tm, HY_C), lambda b, m: (b * nm + m, 0))
    seq = pl.BlockSpec((ns * Lb, HY_C), lambda b, m: (b, 0))
    return pl.pallas_call(
        functools.partial(_hy_inv_kernel, L=Lb, tm=tm, ns=ns),
        grid=(NB // ns, nm),
        in_specs=[pl.BlockSpec((tm, Lb), lambda b, m: (m, 0)), pl.BlockSpec((tm, Lb), lambda b, m: (m, 0)),
                  seq, seq, tile, tile, pl.BlockSpec((None, 1, HY_C), lambda b, m: (n, 0, 0))],
        out_specs=tile,
        out_shape=jax.ShapeDtypeStruct((T, HY_C), F32),
        compiler_params=_cp("arbitrary", "arbitrary"),
        name="hyena_inv_dft",
    )(cm, smt, yr, yi, z, gate, skip)


def _dft_mats(L):
    k = jnp.arange(L, dtype=jnp.int32)
    blk = 64

    def trig(mult):
        ang = ((mult[:, None] * k[None, :]) % (2 * L)).astype(F32) * (math.pi / L)
        return jnp.cos(ang), jnp.sin(ang)

    ca, sa = trig(jnp.arange(L // blk, dtype=jnp.int32) * blk)
    cb, sb = trig(jnp.arange(blk, dtype=jnp.int32))
    cm = (ca[:, None, :] * cb[None] - sa[:, None, :] * sb[None]).reshape(L, L)
    s = -(sa[:, None, :] * cb[None] + ca[:, None, :] * sb[None]).reshape(L, L)
    alt = jnp.where(k % 2 == 0, 1.0, -1.0).astype(F32)
    sm = jnp.where(k[:, None] == 0, alt[None, :], s)
    smt = jnp.where(k[None, :] == 0, alt[:, None], s)
    return cm.astype(BF16), sm.astype(BF16), smt.astype(BF16)


def _merge_kernel(oa_ref, ob_ref, oc_ref, od_ref, g0_ref, g1_ref, g2_ref, g3_ref, wb_ref, wo_ref, x_ref, mod_ref,
                  lg_ref, lb_ref, *rest):
    x1_ref, h2_ref, hp_ref = rest[-3:]
    acc = None
    for o_ref, g_ref, i in ((oa_ref, g0_ref, 0), (ob_ref, g1_ref, 1), (oc_ref, g2_ref, 2), (od_ref, g3_ref, 3)):
        y = _sigmoid(g_ref[...].astype(F32)) * _dot(o_ref[...].astype(BF16), wb_ref[i])
        acc = y if acc is None else acc + y
    mix = _dot(acc.astype(BF16), wo_ref[...])
    m = mod_ref[...]
    x1 = _layer_norm(DN_ALPHA * x_ref[...] + m[2:3, :] * mix, lg_ref[...], lb_ref[...])
    x1_ref[...] = x1
    h2 = x1 * (1.0 + m[4:5, :]) + m[3:4, :]
    h2_ref[...] = h2
    hp_ref[...] = _pack_pairs(h2)


def _merge(outs, gates, wb, wo, x, mod, lg, lb, mod_row, bm, after=None):
    T = x.shape[0]
    row = lambda w: pl.BlockSpec((bm, w), lambda i: (i, 0))
    gspec = lambda n: pl.BlockSpec((bm, D), lambda i: (i, n))
    fixed2 = lambda s: pl.BlockSpec(s, lambda i: (0, 0))
    dep_specs, dep_args = _after(after)
    return pl.pallas_call(
        _merge_kernel,
        grid=(T // bm,),
        in_specs=[row(256)] * 4 + [gspec(0), gspec(1), gspec(2), gspec(3),
                                   pl.BlockSpec((4, 256, D), lambda i: (0, 0, 0)), fixed2((D, D)), row(D),
                                   pl.BlockSpec((None, 6, D), lambda i: (mod_row(i), 0, 0)),
                                   fixed2((1, D)), fixed2((1, D))] + dep_specs,
        out_specs=[row(D), row(D), row(D // 2)],
        out_shape=[jax.ShapeDtypeStruct((T, D), F32)] * 2 + [jax.ShapeDtypeStruct((T, D // 2), jnp.int32)],
        compiler_params=_cp("arbitrary"),
        name="merge_norm",
    )(*outs, gates, gates, gates, gates, wb, wo, x, mod, lg, lb, *dep_args)


def _router_kernel(h_ref, rt_ref, bias_ref, g_ref, rank_ref, cnt_ref, *, tt):
    per = N_EXPERTS // N_GROUPS
    logits = lax.dot_general(rt_ref[...], h_ref[...], (((1,), (1,)), ((), ())), preferred_element_type=F32,
                             precision=lax.Precision.HIGHEST)
    scores = _sigmoid(logits)
    sel = (scores + bias_ref[...]).reshape(N_GROUPS, per, tt)
    gid = lax.broadcasted_iota(jnp.int32, (N_GROUPS, per, tt), 0).astype(F32)
    jid = lax.broadcasted_iota(jnp.int32, (N_GROUPS, per, tt), 1).astype(F32)
    eid = gid * per + jid
    ninf = -jnp.inf
    m1 = sel.max(1, keepdims=True)
    i1 = jnp.where(sel == m1, jid, float(per)).min(1, keepdims=True)
    m2 = jnp.where(jid == i1, ninf, sel).max(1, keepdims=True)
    gs = m1 + m2
    g1 = lax.broadcasted_iota(jnp.int32, (N_GROUPS, 1, tt), 0).astype(F32)
    chosen = jnp.zeros((N_GROUPS, 1, tt), F32)
    for _ in range(TOPK_GROUPS):
        mx = gs.max(0, keepdims=True)
        gi = jnp.where(gs == mx, g1, float(N_GROUPS)).min(0, keepdims=True)
        pick = g1 == gi
        chosen = jnp.where(pick, 1.0, chosen)
        gs = jnp.where(pick, ninf, gs)
    cand = jnp.where(chosen > 0.0, sel, NEG)
    picked = jnp.zeros((N_GROUPS, per, tt), F32)
    for _ in range(TOP_K):
        mx = cand.max(1, keepdims=True).max(0, keepdims=True)
        ei = jnp.where(cand == mx, eid, float(N_EXPERTS)).min(1, keepdims=True).min(0, keepdims=True)
        pick = eid == ei
        picked = jnp.where(pick, 1.0, picked)
        cand = jnp.where(pick, ninf, cand)
    w = scores.reshape(N_GROUPS, per, tt) * picked
    wsum = w.sum(1, keepdims=True).sum(0, keepdims=True)
    g_ref[...] = (w / wsum * ROUTED_SCALE).reshape(N_EXPERTS, tt)
    pk = picked.reshape(N_EXPERTS, tt)
    t_in = lax.broadcasted_iota(jnp.int32, (tt, tt), 0)
    t_out = lax.broadcasted_iota(jnp.int32, (tt, tt), 1)
    upper = jnp.where(t_in <= t_out, 1.0, 0.0).astype(BF16)

    @pl.when(pl.program_id(0) == 0)
    def _():
        cnt_ref[...] = jnp.zeros_like(cnt_ref)

    before = cnt_ref[:, 0:1]
    rank_ref[...] = jnp.where(pk > 0.0, before + _dot(pk.astype(BF16), upper) - 1.0, -1.0)
    cnt_ref[...] += pk.sum(-1, keepdims=True)


def _router(h2, router_t, bias, tt):
    T = h2.shape[0]
    tile = pl.BlockSpec((N_EXPERTS, tt), lambda i: (0, i))
    return pl.pallas_call(
        functools.partial(_router_kernel, tt=tt),
        grid=(T // tt,),
        in_specs=[pl.BlockSpec((tt, D), lambda i: (i, 0)), pl.BlockSpec((N_EXPERTS, D), lambda i: (0, 0)),
                  pl.BlockSpec((N_EXPERTS, 1), lambda i: (0, 0))],
        out_specs=[tile, tile, pl.BlockSpec((N_EXPERTS, 128), lambda i: (0, 0))],
        out_shape=[jax.ShapeDtypeStruct((N_EXPERTS, T), F32), jax.ShapeDtypeStruct((N_EXPERTS, T), F32),
                   jax.ShapeDtypeStruct((N_EXPERTS, 128), F32)],
        compiler_params=_cp("arbitrary"),
        name="moe_router",
    )(h2, router_t, bias)


def _route_pos_kernel(gate_ref, rank_ref, cnt_ref, pos_ref, w_ref, te_ref, nx_ref, nt_ref, *, tm, nt_max):
    ei = lax.broadcasted_iota(jnp.int32, (N_EXPERTS, N_EXPERTS), 0)
    ej = lax.broadcasted_iota(jnp.int32, (N_EXPERTS, N_EXPERTS), 1)
    below = jnp.where(ej < ei, 1.0, 0.0)
    padded = jnp.ceil(cnt_ref[...] * (1.0 / tm)) * tm
    offs = _dot_hi(below, padded)
    rank = rank_ref[...]
    routed = rank >= 0.0
    pos = offs[:, 0:1] + rank
    slot = _dot(below.astype(BF16), jnp.where(routed, 1.0, 0.0).astype(BF16))
    gate = gate_ref[...]
    for k in range(TOP_K):
        mine = routed & (slot == float(k))
        pos_ref[k:k + 1, :] = jnp.where(mine, pos, 0.0).sum(0, keepdims=True).astype(jnp.int32)
        w_ref[k:k + 1, :] = jnp.where(mine, gate, 0.0).sum(0, keepdims=True)
    ends = (offs + padded)[:, 0:1]
    first = (lax.broadcasted_iota(jnp.int32, (N_EXPERTS, nt_max), 1) * tm).astype(F32)
    te = jnp.minimum(jnp.where(ends <= first, 1.0, 0.0).sum(0, keepdims=True), N_EXPERTS - 1.0)
    te_ref[...] = te.astype(jnp.int32)
    eid = lax.broadcasted_iota(jnp.int32, (N_EXPERTS, nt_max), 0).astype(F32)
    nx_ref[...] = (jnp.where(eid == te, ends, 0.0).sum(0, keepdims=True) * (1.0 / tm)).astype(jnp.int32)
    nt_ref[...] = (padded.sum(0, keepdims=True) * (1.0 / tm)).astype(jnp.int32)


def _route_pos(gate_t, rank, cnt, tt, tm, nt_max):
    T = gate_t.shape[1]
    tile = pl.BlockSpec((N_EXPERTS, tt), lambda i: (0, i))
    out = pl.BlockSpec((TOP_K, tt), lambda i: (0, i))
    return pl.pallas_call(
        functools.partial(_route_pos_kernel, tm=tm, nt_max=nt_max),
        grid=(T // tt,),
        in_specs=[tile, tile, pl.BlockSpec((N_EXPERTS, 128), lambda i: (0, 0))],
        out_specs=[out, out, pl.BlockSpec((1, nt_max), lambda i: (0, 0)), pl.BlockSpec((1, nt_max), lambda i: (0, 0)),
                   pl.BlockSpec((1, 128), lambda i: (0, 0))],
        out_shape=[jax.ShapeDtypeStruct((TOP_K, T), jnp.int32), jax.ShapeDtypeStruct((TOP_K, T), F32),
                   jax.ShapeDtypeStruct((1, nt_max), jnp.int32), jax.ShapeDtypeStruct((1, nt_max), jnp.int32),
                   jax.ShapeDtypeStruct((1, 128), jnp.int32)],
        compiler_params=_cp("arbitrary"),
        name="moe_positions",
    )(gate_t, rank, cnt)


def _gmm_kernel(te_ref, nx_ref, nt_ref, xs_ref, w1_hbm, w3_hbm, w2_hbm, ys_ref, b1_ref, b3_ref, b2_ref,
                f1_ref, f3_ref, f2_ref, seg_ref, sem, *, l):
    j = pl.program_id(0)
    live = j < nt_ref[0]
    new_expert = (j == 0) | (te_ref[j] != te_ref[jnp.maximum(j - 1, 0)])

    def fetch(e, slot):
        return [pltpu.make_async_copy(w_hbm.at[l, e], f_ref.at[slot], sem.at[i, slot])
                for i, (w_hbm, f_ref) in enumerate(((w1_hbm, f1_ref), (w3_hbm, f3_ref), (w2_hbm, f2_ref)))]

    @pl.when(live & new_expert)
    def _():
        @pl.when(j == 0)
        def _():
            seg_ref[0] = 0
            for c in fetch(te_ref[0], 0):
                c.start()

        slot = lax.rem(seg_ref[0], 2)
        for c in fetch(te_ref[j], slot):
            c.wait()
        b1_ref[...] = f1_ref[slot].astype(BF16)
        b3_ref[...] = f3_ref[slot].astype(BF16)
        b2_ref[...] = f2_ref[slot].astype(BF16)
        nxt = nx_ref[j]

        @pl.when(nxt < nt_ref[0])
        def _():
            for c in fetch(te_ref[nxt], 1 - slot):
                c.start()

        seg_ref[0] = seg_ref[0] + 1

    @pl.when(live)
    def _():
        xa, xb = _unpack_pairs(xs_ref[...])
        xa, xb = xa.astype(BF16), xb.astype(BF16)
        half = D // 2
        a = _dot(xa, b1_ref[:half, :]) + _dot(xb, b1_ref[half:, :])
        b = _dot(xa, b3_ref[:half, :]) + _dot(xb, b3_ref[half:, :])
        hid = (a * _sigmoid(a) * b).astype(BF16)
        ys_ref[...] = _pack_pairs(_dot(hid, b2_ref[...]))


def _gmm(te, nx, nt, xs, w1, w3, w2, l, tm):
    n_slots = xs.shape[0]
    ds = D_EXPERT
    rows = pl.BlockSpec((tm, D // 2), lambda j, te, nx, nt: (jnp.minimum(j, nt[0] - 1), 0))
    hbm = pl.BlockSpec(memory_space=pl.ANY)
    return pl.pallas_call(
        functools.partial(_gmm_kernel, l=l),
        grid_spec=pltpu.PrefetchScalarGridSpec(
            num_scalar_prefetch=3,
            grid=(n_slots // tm,),
            in_specs=[rows, hbm, hbm, hbm],
            out_specs=rows,
            scratch_shapes=[pltpu.VMEM((D, ds), BF16), pltpu.VMEM((D, ds), BF16), pltpu.VMEM((ds, D), BF16),
                            pltpu.VMEM((2, D, ds), F32), pltpu.VMEM((2, D, ds), F32), pltpu.VMEM((2, ds, D), F32),
                            pltpu.SMEM((1,), jnp.int32), pltpu.SemaphoreType.DMA((3, 2))]),
        out_shape=jax.ShapeDtypeStruct((n_slots, D // 2), jnp.int32),
        compiler_params=_cp("arbitrary"),
        name="moe_grouped_ffn",
    )(te, nx, nt, xs, w1, w3, w2)


def _combine_kernel(yk_ref, w_ref, hp_ref, s1_ref, s3_ref, s2_ref, x_ref, mod_ref, lg_ref, lb_ref, o_ref):
    w = w_ref[...]
    acc_a = acc_b = None
    for k in range(TOP_K):
        ya, yb = _unpack_pairs(yk_ref[k])
        wk = w[:, k:k + 1]
        acc_a = wk * ya if acc_a is None else acc_a + wk * ya
        acc_b = wk * yb if acc_b is None else acc_b + wk * yb
    ha, hb = _unpack_pairs(hp_ref[...])
    ha, hb = ha.astype(BF16), hb.astype(BF16)
    half = D // 2
    a = _dot(ha, s1_ref[:half, :]) + _dot(hb, s1_ref[half:, :])
    b = _dot(ha, s3_ref[:half, :]) + _dot(hb, s3_ref[half:, :])
    y = jnp.concatenate([acc_a, acc_b], axis=1) + _dot((a * _sigmoid(a) * b).astype(BF16), s2_ref[...])
    m = mod_ref[...]
    o_ref[...] = _layer_norm(DN_ALPHA * x_ref[...] + m[5:6, :] * y, lg_ref[...], lb_ref[...])


def _combine(yk, w, hp, s1, s3, s2, x1, mod, lg, lb, mod_row, bm):
    T = x1.shape[0]
    ds = D_EXPERT
    row = lambda n: pl.BlockSpec((bm, n), lambda i: (i, 0))
    fixed = lambda s: pl.BlockSpec(s, lambda i: (0, 0))
    return pl.pallas_call(
        _combine_kernel,
        grid=(T // bm,),
        in_specs=[pl.BlockSpec((TOP_K, bm, D // 2), lambda i: (0, i, 0)), row(TOP_K), row(D // 2),
                  fixed((D, ds)), fixed((D, ds)), fixed((ds, D)), row(D),
                  pl.BlockSpec((None, 6, D), lambda i: (mod_row(i), 0, 0)), fixed((1, D)), fixed((1, D))],
        out_specs=row(D),
        out_shape=jax.ShapeDtypeStruct((T, D), F32),
        compiler_params=_cp("arbitrary"),
        name="moe_combine_norm",
    )(yk, w, hp, s1, s3, s2, x1, mod, lg, lb)


def _sc_worker():
    return lax.axis_index("s") * SC_CORES + lax.axis_index("c")


def _sc_mesh():
    return plsc.VectorSubcoreMesh(core_axis_name="c", subcore_axis_name="s")


def _sc_gather(table, idx):
    N, W = idx.shape[0], table.shape[1]
    per_w = N // SC_WORKERS
    n_chunks = per_w // SC_ROWS

    def body(table_hbm, idx_hbm, out_hbm, idx_v, rows_v, sem):
        base = _sc_worker() * per_w
        pltpu.sync_copy(idx_hbm.at[pl.ds(base, per_w)], idx_v)

        @pl.loop(0, n_chunks)
        def _(c):
            off = pl.multiple_of(c * SC_ROWS, SC_ROWS)
            pltpu.async_copy(table_hbm.at[idx_v.at[pl.ds(off, SC_ROWS)]], rows_v, sem).wait()
            pltpu.sync_copy(rows_v, out_hbm.at[pl.ds(base + off, SC_ROWS)])

    return pl.kernel(
        body, out_type=jax.ShapeDtypeStruct((N, W), table.dtype), mesh=_sc_mesh(),
        scratch_types=[pltpu.VMEM((per_w,), jnp.int32), pltpu.VMEM((SC_ROWS, W), table.dtype),
                       pltpu.SemaphoreType.DMA],
        name="sc_gather",
    )(table, idx)


def _sc_dispatch(pos, table, n_slots):
    NP, (T, W) = pos.shape[0], table.shape
    per_w = n_slots // SC_WORKERS
    n_chunks = per_w // SC_ROWS
    scan = 8192

    def body(pos_hbm, table_hbm, out_hbm, pos_v, src_v, rows_v, sem):
        base = _sc_worker() * per_w
        lane = lax.iota(jnp.int32, SC_LANES)

        @pl.loop(0, per_w // SC_LANES)
        def _(j):
            o = pl.multiple_of(j * SC_LANES, SC_LANES)
            src_v[pl.ds(o, SC_LANES)] = (base + o + lane) & (T - 1)

        @pl.loop(0, NP // scan)
        def _(c):
            pltpu.sync_copy(pos_hbm.at[pl.ds(pl.multiple_of(c * scan, scan), scan)], pos_v)

            @pl.loop(0, scan // SC_LANES)
            def _(v):
                o = pl.multiple_of(v * SC_LANES, SC_LANES)
                p = pos_v[pl.ds(o, SC_LANES)] - base
                mine = (p >= 0) & (p < per_w)
                tok = (c * scan + o + lane) & (T - 1)
                plsc.store_scatter(src_v, [jnp.where(mine, p, 0)], tok, mask=mine)

        @pl.loop(0, n_chunks)
        def _(c):
            off = pl.multiple_of(c * SC_ROWS, SC_ROWS)
            pltpu.async_copy(table_hbm.at[src_v.at[pl.ds(off, SC_ROWS)]], rows_v, sem).wait()
            pltpu.sync_copy(rows_v, out_hbm.at[pl.ds(base + off, SC_ROWS)])

    return pl.kernel(
        body, out_type=jax.ShapeDtypeStruct((n_slots, W), table.dtype), mesh=_sc_mesh(),
        scratch_types=[pltpu.VMEM((scan,), jnp.int32), pltpu.VMEM((per_w,), jnp.int32),
                       pltpu.VMEM((SC_ROWS, W), table.dtype), pltpu.SemaphoreType.DMA],
        compiler_params=pltpu.CompilerParams(needs_layout_passes=False),
        name="sc_dispatch",
    )(pos, table)


def _caches_kernel(*refs, nb, S):
    n_in = 6 * DEPTH
    outs = refs[n_in:]
    l = pl.program_id(0)
    for a in range(DEPTH):
        @pl.when(l == a)
        def _(a=a):
            ckv, kpe, wk, wv, nk, nv = refs[6 * a:6 * (a + 1)]
            for g in range(nb):
                rows = slice(g * S, (g + 1) * S)
                outs[0][g] = ckv[rows, :]
                outs[1][g] = kpe[rows, 64:96]
                outs[2][g] = wk[rows, :]
                outs[3][g] = wv[rows, :]
                outs[4][g] = nk[rows, :]
                outs[5][g] = nv[rows, :]


def _emit_caches(projs, ckvs, B, S):
    nb = 4
    while B % nb:
        nb //= 2

    def layer_specs(a):
        row = lambda l, b: jnp.where(l == a, b, 0)
        col = lambda w, off: pl.BlockSpec((nb * S, w), lambda l, b: (row(l, b), off // w))
        return [pl.BlockSpec((nb * S, 128), lambda l, b: (row(l, b), 0)), col(128, P_KPE), col(128, P_WK),
                col(128, P_WV), col(256, P_NK), col(256, P_NV)]

    in_specs, args = [], []
    for a in range(DEPTH):
        in_specs += layer_specs(a)
        args += [ckvs[a]] + [projs[a]] * 5
    widths = (128, 32, 128, 128, 256, 256)
    return pl.pallas_call(
        functools.partial(_caches_kernel, nb=nb, S=S),
        grid=(DEPTH, B // nb),
        in_specs=in_specs,
        out_specs=[pl.BlockSpec((nb, None, S, w), lambda l, b: (b, l, 0, 0)) for w in widths],
        out_shape=[jax.ShapeDtypeStruct((B, DEPTH, S, w), F32) for w in widths],
        compiler_params=_cp("arbitrary", "arbitrary"),
        name="context_tensors",
    )(*args)


def _rot_cols(w, q):
    a, b, c, d = w[..., :q], w[..., q:2 * q], w[..., 2 * q:3 * q], w[..., 3 * q:]
    return jnp.concatenate([-b, a, -d, c], -1)


def _prep_w_in(w):
    z = lambda n: jnp.zeros((D, n), w.dtype)
    qlat, ckv, kpe, hy = w[:, 0:256], w[:, 256:384], w[:, 384:416], w[:, 416:1184]
    wq, wk, wv = w[:, 1184:1440], w[:, 1440:1568], w[:, 1568:1696]
    nq, nk, nv, gate = w[:, 1696:1952], w[:, 1952:2208], w[:, 2208:2464], w[:, 2464:]
    wq_r = _rot_cols(wq.reshape(D, 4, 64), 16).reshape(D, 256)
    wk_r = _rot_cols(wk.reshape(D, 2, 64), 16).reshape(D, 128)
    kpe_r = _rot_cols(kpe, 8)
    cols = [qlat, ckv, z(64), kpe, z(32), hy, wq, wk, wv, nq, nk, nv, wq_r, wk_r, z(64), kpe_r, z(32), gate]
    return jnp.concatenate(cols, 1).astype(BF16)


def _prep_mla(w_uq, w_ukv):
    uq = w_uq.reshape(256, 4, 96)
    nope, pe = uq[..., :64], uq[..., 64:]
    z32 = jnp.zeros((256, 4, 32), w_uq.dtype)
    z64 = jnp.zeros((256, 4, 64), w_uq.dtype)
    wcat = jnp.concatenate([nope, pe, z32], -1).reshape(256, 512).astype(BF16)
    wrot = jnp.concatenate([z64, _rot_cols(pe, 8), z32], -1).reshape(256, 512).astype(BF16)
    ukv = w_ukv.reshape(128, 4, 128)
    wk = jnp.concatenate([ukv[..., :64], jnp.zeros((128, 4, 64), w_ukv.dtype)], -1).reshape(128, 512).astype(BF16)
    wv = ukv[..., 64:].reshape(128, 256).astype(BF16)
    return wcat, wrot, wk, wv


def _rope_tab(L, q):
    t = jnp.arange(L)
    inv = ROPE_BASE ** (-jnp.arange(q, dtype=F32) / q)
    ar = (t // GRID_W).astype(F32)[:, None] * inv[None, :]
    ac = (t % GRID_W).astype(F32)[:, None] * inv[None, :]
    cos = jnp.concatenate([jnp.cos(ar), jnp.cos(ar), jnp.cos(ac), jnp.cos(ac)], 1)
    sin = jnp.concatenate([jnp.sin(ar), jnp.sin(ar), jnp.sin(ac), jnp.sin(ac)], 1)
    return cos, sin


def _rope_tables(L):
    c8, s8 = _rope_tab(L, 8)
    c16, s16 = _rope_tab(L, 16)
    one, zero = jnp.ones((L, 64), F32), jnp.zeros((L, 64), F32)
    z32 = jnp.zeros((L, 32), F32)
    mla_q = (jnp.tile(jnp.concatenate([one, c8, z32], 1), (1, 4)), jnp.tile(jnp.concatenate([zero, s8, z32], 1), (1, 4)))
    mla_k = (jnp.concatenate([zero, c8, z32], 1), jnp.concatenate([zero, s8, z32], 1))
    win = (jnp.tile(c16, (1, 4)), jnp.tile(s16, (1, 4)), jnp.tile(c16, (1, 2)), jnp.tile(s16, (1, 2)))
    return mla_q + mla_k, win


def _hyena(proj, lp, dft, NB, Lb):
    cm, sm, smt = dft
    tm = min(Lb, 512)
    v, x1, x2 = _short_conv(proj, lp["hy_conv_w"], lp["hy_conv_b"].reshape(1, -1), NB, Lb)
    w1p = jnp.pad(lp["hy_w1"], ((0, 128 - lp["hy_w1"].shape[0]), (0, 0)))
    fs, nyq = _hy_filter(Lb, w1p, lp["hy_b1"].reshape(1, -1), lp["hy_w2"], lp["hy_b2"].reshape(1, -1), lp["hy_w3"],
                         lp["hy_sin_freq"], lp["hy_log_decay"].reshape(1, -1))
    gr, gi = _hy_gdft(cm, sm, fs, nyq, Lb, tm)
    skip = lp["hy_skip"].reshape(2, 1, HY_C)
    z = v
    for n, gate in enumerate((x1, x2)):
        yr, yi = _hy_fwd(cm, sm, z, gr, gi, n, NB, Lb, tm)
        z = _hy_inv(cm, smt, yr, yi, z, gate, skip, n, NB, Lb, tm)
    return z


def _layer(x, mod, lp, l, NB, Lb, mod_row_of_batch, dft, cache=None, tabs=None, na_bias=None, after=None):
    T = NB * Lb
    latent = cache is not None
    bm = 256
    rows_of = lambda n: (lambda i: mod_row_of_batch((i * n) // Lb))
    mod_row = rows_of(bm)
    span = Lb if latent else T
    bmp = min(span, 1024)
    proj, gates = _in_proj(x, mod, lp["w_in_p"], rows_of(bmp), bmp)

    gq, gkv = lp["mla_q_norm"].reshape(1, -1), lp["mla_kv_norm"].reshape(1, -1)
    wcat, wrot, wk, wv = lp["mla_w"]
    q_all, ckv_n, kpe_r = _mla_q(proj, gq, gkv, wcat, wrot, tabs[0] if latent else None, Lb, min(span, 512))
    if latent:
        ckv_c, kpe_c, kc_c, vc_c, kd_c, vd_c = cache
        Lc = ckv_c.shape[1]
        kpe_cp = jnp.pad(kpe_c, ((0, 0), (0, 0), (64, 32)))
        ckv_all = jnp.concatenate([ckv_c, ckv_n.reshape(NB, Lb, 128)], 1).reshape(NB * (Lc + Lb), 128)
        kpe_all = jnp.concatenate([kpe_cp, kpe_r.reshape(NB, Lb, 128)], 1).reshape(NB * (Lc + Lb), 128)
        k_all, v_all = _mla_kv(ckv_all, kpe_all, wk, wv, 512)
        oa = _lat_mla_attention(q_all, k_all, v_all, NB, Lb, Lc + Lb, 256, after=after)
        oc = _lat_win_attention(proj, kc_c.reshape(NB, Lc, 128), vc_c.reshape(NB, Lc, 128), tabs[1],
                                lp["win_sink"], NB, Lb)
        od = _lat_na_attention(proj, kd_c.reshape(NB, Lc, 256), vd_c.reshape(NB, Lc, 256), na_bias, NB, Lb)
    else:
        k_all, v_all = _mla_kv(ckv_n, kpe_r, wk, wv, 512)
        oa, oc, od = _ctx_attention(proj, q_all, k_all, v_all, lp["win_sink"], NB, Lb)
    ob = _hyena(proj, lp, dft, NB, Lb)

    bmm = min(span, 512)
    x1, h2, hp = _merge((oa, ob, oc, od), gates, lp["w_branch_b"], lp["w_out_b"], x, mod,
                        lp["ln1_g"].reshape(1, -1), lp["ln1_b"].reshape(1, -1), rows_of(bmm), bmm,
                        after=None if latent else after)
    n_slots = T * TOP_K + N_EXPERTS * MOE_TM
    gate_t, rank, cnt = _router(h2, lp["moe_router"].T, lp["moe_bias"].reshape(-1, 1), 512)
    pos, w8, te, nx, nt = _route_pos(gate_t, rank, cnt, 512, MOE_TM, n_slots // MOE_TM)
    xs = _sc_dispatch(pos.reshape(-1), hp, n_slots)
    ys = _gmm(te.reshape(-1), nx.reshape(-1), nt.reshape(-1)[:1], xs, lp["moe_w1"], lp["moe_w3"], lp["moe_w2"], l,
              MOE_TM)
    yk = _sc_gather(ys, pos.reshape(-1)).reshape(TOP_K, T, D // 2)
    x2 = _combine(yk, w8.T, hp, lp["sh_w1_b"], lp["sh_w3_b"], lp["sh_w2_b"], x1, mod,
                  lp["ln2_g"].reshape(1, -1), lp["ln2_b"].reshape(1, -1), mod_row, bm)
    return x2, (proj, ckv_n), ys


def kernel(x_prompt, x_sample, cache_mla_ckv, cache_mla_kpe, cache_win_k, cache_win_v, cache_na_k, cache_na_v, c, c_ctx, w_ada, b_ada, w_in, mla_q_norm, mla_kv_norm, mla_w_uq, mla_w_ukv, hy_conv_w, hy_conv_b, hy_w1, hy_b1, hy_w2, hy_b2, hy_w3, hy_sin_freq, hy_log_decay, hy_skip, win_sink, na_rpb, w_branch, w_out, ln1_g, ln1_b, ln2_g, ln2_b, moe_router, moe_bias, moe_w1, moe_w3, moe_w2, sh_w1, sh_w3, sh_w2):
    B, S, _ = x_prompt.shape
    DB, DS, _ = x_sample.shape
    xp = x_prompt.reshape(B * S, D)
    xs = x_sample.reshape(DB * DS, D)
    cvec = jnp.concatenate([c_ctx[None, :], c, jnp.zeros((8 - 1 - DB, D), F32)], 0)
    dft_ctx = _dft_mats(S)
    dft_lat = _dft_mats(DS)
    tabs = _rope_tables(DS)
    projs, ckvs = [], []
    ys_lat = None
    for l in range(DEPTH):
        lp = dict(w_in_p=_prep_w_in(w_in[l]), mla_q_norm=mla_q_norm[l], mla_kv_norm=mla_kv_norm[l],
                  mla_w=_prep_mla(mla_w_uq[l], mla_w_ukv[l]), hy_conv_w=hy_conv_w[l], hy_conv_b=hy_conv_b[l],
                  hy_w1=hy_w1[l], hy_b1=hy_b1[l], hy_w2=hy_w2[l], hy_b2=hy_b2[l], hy_w3=hy_w3[l],
                  hy_sin_freq=hy_sin_freq[l], hy_log_decay=hy_log_decay[l], hy_skip=hy_skip[l],
                  win_sink=win_sink[l], w_branch_b=w_branch[l].astype(BF16), w_out_b=w_out[l].astype(BF16),
                  ln1_g=ln1_g[l], ln1_b=ln1_b[l], ln2_g=ln2_g[l], ln2_b=ln2_b[l],
                  moe_router=moe_router[l], moe_bias=moe_bias[l], moe_w1=moe_w1, moe_w3=moe_w3, moe_w2=moe_w2,
                  sh_w1_b=sh_w1[l].astype(BF16), sh_w3_b=sh_w3[l].astype(BF16), sh_w2_b=sh_w2[l].astype(BF16))
        mod = _modulation(cvec, w_ada, b_ada, l)
        xp, (proj, ckv_n), ys_ctx = _layer(xp, mod, lp, l, B, S, lambda b: 0, dft_ctx, after=ys_lat)
        projs.append(proj)
        ckvs.append(ckv_n)
        cache = (cache_mla_ckv[:, l], cache_mla_kpe[:, l], cache_win_k[:, l], cache_win_v[:, l],
                 cache_na_k[:, l], cache_na_v[:, l])
        xs, _, ys_lat = _layer(xs, mod, lp, l, DB, DS, lambda b: 1 + b, dft_lat, cache=cache, tabs=tabs,
                               na_bias=_na_bias(na_rpb[l]), after=ys_ctx)
    ckv, kpe, wk, wv, nk, nv = _emit_caches(projs, ckvs, B, S)
    heads = lambda t, h: t.reshape(B, DEPTH, S, h, HEAD_DIM)
    return (xp.reshape(B, S, D), xs.reshape(DB, DS, D), ckv, kpe, heads(wk, 2), heads(wv, 2), heads(nk, 4),
            heads(nv, 4))
```

```python
import functools
import math

import jax
import jax.numpy as jnp
from jax import lax
from jax.experimental import pallas as pl
from jax.experimental.pallas import tpu as pltpu
from jax.experimental.pallas import tpu_sc as plsc

F32 = jnp.float32
BF16 = jnp.bfloat16

D = 1024
DEPTH = 2
GRID_W = 64
HEAD_DIM = 64
MLA_SCALE = 96 ** -0.5
ATT_SCALE = HEAD_DIM ** -0.5
HY_C = 256
HY_BANDS = 8
NA_KH = 8
NA_KW = 16
N_EXPERTS = 64
N_GROUPS = 8
TOP_K = 8
TOPK_GROUPS = 4
D_EXPERT = 256
ROUTED_SCALE = 2.5
ROPE_BASE = 10000.0
LN_EPS = 1e-5
RMS_EPS = 1e-6
NEG = -1e30
DN_ALPHA = (2 * DEPTH) ** 0.25

P_QLAT, P_CKV, P_KPE, P_HY = 0, 256, 384, 512
P_WQ, P_WK, P_WV = 1280, 1536, 1664
P_NQ, P_NK, P_NV = 1792, 2048, 2304
P_WQR, P_WKR, P_KPER, P_GATE = 2560, 2816, 2944, 3072
N_PROJ = 7168

VMEM_LIMIT = 56 * 1024 * 1024

SC_CORES = 2
SC_SUBCORES = 16
SC_LANES = 16
SC_WORKERS = SC_CORES * SC_SUBCORES
SC_ROWS = 64

MOE_TM = 512

def _cp(*sem):
    return pltpu.CompilerParams(dimension_semantics=sem, vmem_limit_bytes=VMEM_LIMIT)


def _sigmoid(x):
    return 1.0 / (1.0 + jnp.exp(-x))


def _dot(a, b):
    return jnp.dot(a, b, preferred_element_type=F32)


def _dot_nt(a, b):
    return lax.dot_general(a, b, (((1,), (1,)), ((), ())), preferred_element_type=F32)


def _dot_hi(a, b):
    return jnp.dot(a, b, preferred_element_type=F32, precision=lax.Precision.HIGHEST)


def _pack_pairs(x):
    w = x.shape[1] // 2
    hi = lax.bitcast_convert_type(x[:, :w].astype(BF16).astype(F32), jnp.int32)
    lo = lax.bitcast_convert_type(x[:, w:].astype(BF16).astype(F32), jnp.int32)
    return hi | lax.shift_right_logical(lo, 16)


def _unpack_pairs(p):
    hi = lax.bitcast_convert_type(p & jnp.int32(-65536), F32)
    lo = lax.bitcast_convert_type(lax.shift_left(p, 16), F32)
    return hi, lo


def _layer_norm(x, g, b):
    mu = jnp.mean(x, -1, keepdims=True)
    xc = x - mu
    var = jnp.mean(xc * xc, -1, keepdims=True)
    return xc * lax.rsqrt(var + LN_EPS) * g + b


def _rms_norm(x, g):
    return x * lax.rsqrt(jnp.mean(x * x, -1, keepdims=True) + RMS_EPS) * g


def _mod_kernel(c_ref, w_ref, b_ref, o_ref):
    c = c_ref[...]
    a = (c * _sigmoid(c)).astype(BF16)
    o_ref[...] = _dot(a, w_ref[...].astype(BF16)) + b_ref[...]


def _modulation(cvec, w_ada, b_ada, l):
    out = pl.pallas_call(
        _mod_kernel,
        grid=(6,),
        in_specs=[pl.BlockSpec((8, D), lambda j: (0, 0)),
                  pl.BlockSpec((None, D, D), lambda j: (l, 0, j)),
                  pl.BlockSpec((None, 1, D), lambda j: (l, 0, j))],
        out_specs=pl.BlockSpec((8, D), lambda j: (0, j)),
        out_shape=jax.ShapeDtypeStruct((8, 6 * D), F32),
        compiler_params=_cp("arbitrary"),
        name="modulation",
    )(cvec, w_ada, b_ada.reshape(DEPTH, 1, 6 * D))
    return out.reshape(8, 6, D)


def _inproj_kernel(x_ref, mod_ref, w_ref, o_ref, g_ref, h_ref, *, n_main):
    j = pl.program_id(1)

    @pl.when(j == 0)
    def _():
        m = mod_ref[...]
        h_ref[...] = (x_ref[...] * (1.0 + m[1:2, :]) + m[0:1, :]).astype(BF16)

    y = _dot(h_ref[...], w_ref[...])

    @pl.when(j < n_main)
    def _():
        o_ref[...] = y

    @pl.when(j >= n_main)
    def _():
        g_ref[...] = y.astype(BF16)


def _in_proj(x, mod, w_p, mod_row, bm, bn=1024):
    T = x.shape[0]
    n_main = P_GATE // bn
    return pl.pallas_call(
        functools.partial(_inproj_kernel, n_main=n_main),
        grid=(T // bm, N_PROJ // bn),
        in_specs=[pl.BlockSpec((bm, D), lambda i, j: (i, 0)),
                  pl.BlockSpec((None, 6, D), lambda i, j: (mod_row(i), 0, 0)),
                  pl.BlockSpec((D, bn), lambda i, j: (0, j))],
        out_specs=[pl.BlockSpec((bm, bn), lambda i, j: (i, jnp.minimum(j, n_main - 1))),
                   pl.BlockSpec((bm, bn), lambda i, j: (i, jnp.maximum(j - n_main, 0)))],
        out_shape=[jax.ShapeDtypeStruct((T, P_GATE), F32), jax.ShapeDtypeStruct((T, N_PROJ - P_GATE), BF16)],
        scratch_shapes=[pltpu.VMEM((bm, D), BF16)],
        compiler_params=_cp("arbitrary", "arbitrary"),
        name="in_proj",
    )(x, mod, w_p)


def _mla_q_kernel(*refs, rope):
    if rope:
        (ql_ref, ckv_ref, kpe_ref, kper_ref, gq_ref, gkv_ref, wc_ref, wr_ref,
         cq_ref, sq_ref, ck_ref, sk_ref, q_ref, ckvn_ref, kpeo_ref) = refs
    else:
        ql_ref, ckv_ref, kpe_ref, gq_ref, gkv_ref, wc_ref, q_ref, ckvn_ref, kpeo_ref = refs
    qn = _rms_norm(ql_ref[...], gq_ref[...]).astype(BF16)
    q = _dot(qn, wc_ref[...])
    if rope:
        q = q * cq_ref[...] + _dot(qn, wr_ref[...]) * sq_ref[...]
        kpeo_ref[...] = kpe_ref[...] * ck_ref[...] + kper_ref[...] * sk_ref[...]
    else:
        kpeo_ref[...] = kpe_ref[...]
    q_ref[...] = (q * MLA_SCALE).astype(BF16)
    ckvn_ref[...] = _rms_norm(ckv_ref[...], gkv_ref[...])


def _mla_q(proj, gq, gkv, wcat, wrot, tabs, Lb, bm):
    T = proj.shape[0]
    rope = tabs is not None
    nl = Lb // bm
    col = lambda c: (lambda i: (i, c))
    fixed = lambda i: (0, 0)
    in_specs = [pl.BlockSpec((bm, 256), col(P_QLAT // 256)),
                pl.BlockSpec((bm, 128), col(P_CKV // 128)),
                pl.BlockSpec((bm, 128), col(P_KPE // 128))]
    args = [proj, proj, proj]
    if rope:
        in_specs.append(pl.BlockSpec((bm, 128), col(P_KPER // 128)))
        args.append(proj)
    in_specs += [pl.BlockSpec((1, 256), fixed), pl.BlockSpec((1, 128), fixed), pl.BlockSpec((256, 512), fixed)]
    args += [gq, gkv, wcat]
    if rope:
        cq, sq, ck, sk = tabs
        pos = lambda i: (i % nl, 0)
        in_specs += [pl.BlockSpec((256, 512), fixed), pl.BlockSpec((bm, 512), pos), pl.BlockSpec((bm, 512), pos),
                     pl.BlockSpec((bm, 128), pos), pl.BlockSpec((bm, 128), pos)]
        args += [wrot, cq, sq, ck, sk]
    return pl.pallas_call(
        functools.partial(_mla_q_kernel, rope=rope),
        grid=(T // bm,),
        in_specs=in_specs,
        out_specs=[pl.BlockSpec((bm, 512), lambda i: (i, 0)),
                   pl.BlockSpec((bm, 128), lambda i: (i, 0)),
                   pl.BlockSpec((bm, 128), lambda i: (i, 0))],
        out_shape=[jax.ShapeDtypeStruct((T, 512), BF16),
                   jax.ShapeDtypeStruct((T, 128), F32),
                   jax.ShapeDtypeStruct((T, 128), F32)],
        compiler_params=_cp("arbitrary"),
        name="mla_q",
    )(*args)


def _mla_kv_kernel(ckv_ref, kpe_ref, wk_ref, wv_ref, k_ref, v_ref):
    c = ckv_ref[...].astype(BF16)
    kpe = kpe_ref[...]
    k_ref[...] = (_dot(c, wk_ref[...]) + jnp.concatenate([kpe] * 4, axis=1)).astype(BF16)
    v_ref[...] = _dot(c, wv_ref[...]).astype(BF16)


def _mla_kv(ckv, kpe, wk, wv, bm):
    Tk = ckv.shape[0]
    return pl.pallas_call(
        _mla_kv_kernel,
        grid=(Tk // bm,),
        in_specs=[pl.BlockSpec((bm, 128), lambda i: (i, 0)),
                  pl.BlockSpec((bm, 128), lambda i: (i, 0)),
                  pl.BlockSpec((128, 512), lambda i: (0, 0)),
                  pl.BlockSpec((128, 256), lambda i: (0, 0))],
        out_specs=[pl.BlockSpec((bm, 512), lambda i: (i, 0)),
                   pl.BlockSpec((bm, 256), lambda i: (i, 0))],
        out_shape=[jax.ShapeDtypeStruct((Tk, 512), BF16),
                   jax.ShapeDtypeStruct((Tk, 256), BF16)],
        compiler_params=_cp("arbitrary"),
        name="mla_kv",
    )(ckv, kpe, wk, wv)


def _attn_core(q, kvs, masks, sink):
    ss = []
    for (k, _), mk in zip(kvs, masks):
        s = _dot_nt(q, k)
        if mk is not None:
            s = s + mk[1] if mk[0] == "add" else jnp.where(mk[1], s, NEG)
        ss.append(s)
    m = ss[0].max(-1, keepdims=True)
    for s in ss[1:]:
        m = jnp.maximum(m, s.max(-1, keepdims=True))
    if sink is not None:
        m = jnp.maximum(m, sink)
    den = None
    acc = None
    for s, (_, v) in zip(ss, kvs):
        p = jnp.exp(s - m)
        d = p.sum(-1, keepdims=True)
        a = _dot(p.astype(BF16), v)
        den = d if den is None else den + d
        acc = a if acc is None else acc + a
    if sink is not None:
        den = den + jnp.exp(sink - m)
    return acc / den


def _ctx_attn_kernel(qm_ref, km_ref, vm_ref, wq_ref, wk_ref, wv_ref, nq_ref, nk_ref, nv_ref, sink_ref, *rest):
    om_ref, ow_ref, on_ref = rest[-3:]
    for h in range(4):
        q = qm_ref[:, 128 * h:128 * (h + 1)]
        k = km_ref[:, 128 * h:128 * (h + 1)]
        v = vm_ref[:, 64 * h:64 * (h + 1)]
        om_ref[:, 64 * h:64 * (h + 1)] = _attn_core(q, [(k, v)], [None], None)
    for h in range(4):
        g = h // 2
        q = (wq_ref[:, 64 * h:64 * (h + 1)] * ATT_SCALE).astype(BF16)
        k = wk_ref[:, 64 * g:64 * (g + 1)].astype(BF16)
        v = wv_ref[:, 64 * g:64 * (g + 1)].astype(BF16)
        ow_ref[:, 64 * h:64 * (h + 1)] = _attn_core(q, [(k, v)], [None], sink_ref[h])
    for h in range(4):
        q = (nq_ref[:, 64 * h:64 * (h + 1)] * ATT_SCALE).astype(BF16)
        k = nk_ref[:, 64 * h:64 * (h + 1)].astype(BF16)
        v = nv_ref[:, 64 * h:64 * (h + 1)].astype(BF16)
        on_ref[:, 64 * h:64 * (h + 1)] = _attn_core(q, [(k, v)], [None], None)


def _ctx_attention(proj, q_all, k_all, v_all, sink, NB, Lb, after=None):
    T = proj.shape[0]
    pc = lambda w, off: pl.BlockSpec((Lb, w), lambda b: (b, off // w))
    row = lambda w: pl.BlockSpec((Lb, w), lambda b: (b, 0))
    dep_specs, dep_args = _after(after)
    return pl.pallas_call(
        _ctx_attn_kernel,
        grid=(NB,),
        in_specs=[row(512), row(512), row(256),
                  pc(256, P_WQ), pc(128, P_WK), pc(128, P_WV),
                  pc(256, P_NQ), pc(256, P_NK), pc(256, P_NV),
                  pl.BlockSpec(memory_space=pltpu.SMEM)] + dep_specs,
        out_specs=[row(256), row(256), row(256)],
        out_shape=[jax.ShapeDtypeStruct((T, 256), F32)] * 3,
        compiler_params=_cp("arbitrary"),
        name="ctx_attention",
    )(q_all, k_all, v_all, proj, proj, proj, proj, proj, proj, sink, *dep_args)


def _lat_mla_kernel(q_ref, k_ref, v_ref, *rest):
    o_ref = rest[-1]
    for h in range(4):
        q = q_ref[:, 128 * h:128 * (h + 1)]
        k = k_ref[:, 128 * h:128 * (h + 1)]
        v = v_ref[:, 64 * h:64 * (h + 1)]
        o_ref[:, 64 * h:64 * (h + 1)] = _attn_core(q, [(k, v)], [None], None)


def _after(after):
    return ([], []) if after is None else ([pl.BlockSpec(memory_space=pl.ANY)], [after])


def _lat_mla_attention(q_all, k_all, v_all, NB, Lb, Lk, tq, after=None):
    T = q_all.shape[0]
    nq = Lb // tq
    dep_specs, dep_args = _after(after)
    return pl.pallas_call(
        _lat_mla_kernel,
        grid=(NB, nq),
        in_specs=[pl.BlockSpec((tq, 512), lambda b, i: (b * nq + i, 0)),
                  pl.BlockSpec((Lk, 512), lambda b, i: (b, 0)),
                  pl.BlockSpec((Lk, 256), lambda b, i: (b, 0))] + dep_specs,
        out_specs=pl.BlockSpec((tq, 256), lambda b, i: (b * nq + i, 0)),
        out_shape=jax.ShapeDtypeStruct((T, 256), F32),
        compiler_params=_cp("arbitrary", "arbitrary"),
        name="lat_mla_attention",
    )(q_all, k_all, v_all, *dep_args)


def _attn_local_ctx(q, locs, kc, vc, sink):
    s_ctx = _dot_nt(q, kc)
    m_ctx = s_ctx.max(-1, keepdims=True)
    if sink is not None:
        m_ctx = jnp.maximum(m_ctx, sink)
    ms, dens, accs = [], [], []
    for rs, k, v, mk in locs:
        s = _dot_nt(q[rs], k)
        s = s + mk[1] if mk[0] == "add" else jnp.where(mk[1], s, NEG)
        m = jnp.maximum(s.max(-1, keepdims=True), m_ctx[rs])
        p = jnp.exp(s - m)
        ms.append(m)
        dens.append(p.sum(-1, keepdims=True))
        accs.append(_dot(p.astype(BF16), v))
    m = jnp.concatenate(ms, axis=0)
    p = jnp.exp(s_ctx - m)
    den = jnp.concatenate(dens, axis=0) + p.sum(-1, keepdims=True)
    if sink is not None:
        den = den + jnp.exp(sink - m)
    return (jnp.concatenate(accs, axis=0) + _dot(p.astype(BF16), vc)) / den


def _lat_win_kernel(q_ref, qr_ref, k_ref, kr_ref, v_ref, kc_ref, vc_ref, cq_ref, sq_ref, ck_ref, sk_ref,
                    sink_ref, o_ref, *, Lb, bpt):
    t = pl.program_id(1)
    q = (q_ref[...] * cq_ref[...] + qr_ref[...] * sq_ref[...]) * ATT_SCALE
    kc = kc_ref[...].astype(BF16)
    vc = vc_ref[...].astype(BF16)
    blocks = []
    for bb in range(bpt):
        i = t * bpt + bb
        start = pl.multiple_of(jnp.clip((i - 1) * 128, 0, Lb - 384), 128)
        win = pl.ds(start, 384)
        kk = (k_ref[win, :] * ck_ref[win, :] + kr_ref[win, :] * sk_ref[win, :]).astype(BF16)
        qpos = i * 128 + lax.broadcasted_iota(jnp.int32, (128, 384), 0)
        kpos = start + lax.broadcasted_iota(jnp.int32, (128, 384), 1)
        blocks.append((kk, v_ref[win, :].astype(BF16), jnp.abs(qpos - kpos) <= 128))
    for h in range(4):
        g = h // 2
        sl = slice(64 * g, 64 * (g + 1))
        locs = [(slice(128 * bb, 128 * (bb + 1)), kk[:, sl], vv[:, sl], ("keep", valid))
                for bb, (kk, vv, valid) in enumerate(blocks)]
        qh = q[:, 64 * h:64 * (h + 1)].astype(BF16)
        o_ref[:, 64 * h:64 * (h + 1)] = _attn_local_ctx(qh, locs, kc[:, sl], vc[:, sl], sink_ref[h])


def _lat_win_attention(proj, kc, vc, tabs, sink, NB, Lb):
    T = proj.shape[0]
    bpt = 1
    tq = 128 * bpt
    nt = Lb // tq
    Lc = kc.shape[1]
    cq, sq, ck, sk = tabs
    qspec = lambda off: pl.BlockSpec((tq, 256), lambda b, i: (b * nt + i, off // 256))
    kspec = lambda off: pl.BlockSpec((Lb, 128), lambda b, i: (b, off // 128))
    cspec = pl.BlockSpec((None, Lc, 128), lambda b, i: (b, 0, 0))
    return pl.pallas_call(
        functools.partial(_lat_win_kernel, Lb=Lb, bpt=bpt),
        grid=(NB, nt),
        in_specs=[qspec(P_WQ), qspec(P_WQR), kspec(P_WK), kspec(P_WKR), kspec(P_WV), cspec, cspec,
                  pl.BlockSpec((tq, 256), lambda b, i: (i, 0)), pl.BlockSpec((tq, 256), lambda b, i: (i, 0)),
                  pl.BlockSpec((Lb, 128), lambda b, i: (0, 0)), pl.BlockSpec((Lb, 128), lambda b, i: (0, 0)),
                  pl.BlockSpec(memory_space=pltpu.SMEM)],
        out_specs=pl.BlockSpec((tq, 256), lambda b, i: (b * nt + i, 0)),
        out_shape=jax.ShapeDtypeStruct((T, 256), F32),
        compiler_params=_cp("arbitrary", "arbitrary"),
        name="lat_win_attention",
    )(proj, proj, proj, proj, proj, kc, vc, cq, sq, ck, sk, sink)


def _na_bias_kernel(rpb_ref, o_ref):
    h = pl.program_id(0)
    qc = lax.broadcasted_iota(jnp.int32, (GRID_W, GRID_W), 0)
    kc = lax.broadcasted_iota(jnp.int32, (GRID_W, GRID_W), 1)
    dc = kc - qc + (NA_KW - 1)
    wstart = jnp.clip(qc - NA_KW // 2, 0, GRID_W - NA_KW)
    ok = (kc >= wstart) & (kc < wstart + NA_KW)
    n_dc = 2 * NA_KW - 1
    n_dr = 2 * NA_KH - 1
    tabs = []
    for dr in range(n_dr):
        t = jnp.zeros((GRID_W, GRID_W), F32)
        for j in range(n_dc):
            t = jnp.where(dc == j, rpb_ref[(h * n_dr + dr) * n_dc + j], t)
        tabs.append(jnp.where(ok, t, NEG))
    for o in range(NA_KH):
        for a in range(NA_KH):
            o_ref[o, :, GRID_W * a:GRID_W * (a + 1)] = tabs[a + NA_KH - 1 - o]


def _na_bias(rpb):
    H = rpb.shape[0]
    return pl.pallas_call(
        _na_bias_kernel,
        grid=(H,),
        in_specs=[pl.BlockSpec(memory_space=pltpu.SMEM)],
        out_specs=pl.BlockSpec((None, NA_KH, GRID_W, NA_KH * GRID_W), lambda h: (h, 0, 0, 0)),
        out_shape=jax.ShapeDtypeStruct((H, NA_KH, GRID_W, NA_KH * GRID_W), F32),
        compiler_params=_cp("arbitrary"),
        name="na_bias",
    )(rpb.reshape(-1))


def _lat_na_kernel(q_ref, k_ref, v_ref, kc_ref, vc_ref, bias_ref, o_ref, *, rows, rpt):
    t = pl.program_id(1)
    q = q_ref[...] * ATT_SCALE
    kc = kc_ref[...].astype(BF16)
    vc = vc_ref[...].astype(BF16)
    bands = []
    for rr in range(rpt):
        r = t * rpt + rr
        first = jnp.clip(r - NA_KH // 2, 0, rows - NA_KH)
        win = pl.ds(pl.multiple_of(first * GRID_W, GRID_W), NA_KH * GRID_W)
        bands.append((k_ref[win, :].astype(BF16), v_ref[win, :].astype(BF16), r - first))
    for h in range(4):
        sl = slice(64 * h, 64 * (h + 1))
        locs = [(slice(GRID_W * rr, GRID_W * (rr + 1)), kk[:, sl], vv[:, sl], ("add", bias_ref[h, off]))
                for rr, (kk, vv, off) in enumerate(bands)]
        o_ref[:, sl] = _attn_local_ctx(q[:, sl].astype(BF16), locs, kc[:, sl], vc[:, sl], None)


def _lat_na_attention(proj, kc, vc, bias, NB, Lb):
    T = proj.shape[0]
    rows = Lb // GRID_W
    rpt = 8
    tq = GRID_W * rpt
    nt = rows // rpt
    Lc = kc.shape[1]
    kspec = lambda off: pl.BlockSpec((Lb, 256), lambda b, t: (b, off // 256))
    cspec = pl.BlockSpec((None, Lc, 256), lambda b, t: (b, 0, 0))
    return pl.pallas_call(
        functools.partial(_lat_na_kernel, rows=rows, rpt=rpt),
        grid=(NB, nt),
        in_specs=[pl.BlockSpec((tq, 256), lambda b, t: (b * nt + t, P_NQ // 256)),
                  kspec(P_NK), kspec(P_NV), cspec, cspec,
                  pl.BlockSpec((4, NA_KH, GRID_W, NA_KH * GRID_W), lambda b, t: (0, 0, 0, 0))],
        out_specs=pl.BlockSpec((tq, 256), lambda b, t: (b * nt + t, 0)),
        out_shape=jax.ShapeDtypeStruct((T, 256), F32),
        compiler_params=_cp("arbitrary", "arbitrary"),
        name="lat_na_attention",
    )(proj, proj, proj, kc, vc, bias)


def _short_conv_kernel(a_ref, b_ref, c_ref, w_ref, bias_ref, oa_ref, ob_ref, oc_ref, *, L):
    t = lax.broadcasted_iota(jnp.int32, (L, HY_C), 0)
    for n, (x_ref, o_ref) in enumerate(((a_ref, oa_ref), (b_ref, ob_ref), (c_ref, oc_ref))):
        sl = slice(HY_C * n, HY_C * (n + 1))
        x = x_ref[...]
        prev = jnp.where(t == 0, 0.0, pltpu.roll(x, 1, axis=0))
        nxt = jnp.where(t == L - 1, 0.0, pltpu.roll(x, L - 1, axis=0))
        o_ref[...] = prev * w_ref[0:1, sl] + x * w_ref[1:2, sl] + nxt * w_ref[2:3, sl] + bias_ref[:, sl]


def _short_conv(proj, w, b, NB, Lb):
    T = proj.shape[0]
    spec = lambda c: pl.BlockSpec((Lb, HY_C), lambda i: (i, c))
    return pl.pallas_call(
        functools.partial(_short_conv_kernel, L=Lb),
        grid=(NB,),
        in_specs=[spec(P_HY // HY_C), spec(P_HY // HY_C + 1), spec(P_HY // HY_C + 2),
                  pl.BlockSpec((3, 3 * HY_C), lambda i: (0, 0)), pl.BlockSpec((1, 3 * HY_C), lambda i: (0, 0))],
        out_specs=[spec(0)] * 3,
        out_shape=[jax.ShapeDtypeStruct((T, HY_C), F32)] * 3,
        compiler_params=_cp("arbitrary"),
        name="hyena_short_conv",
    )(proj, proj, proj, w, b)


def _hy_filter_kernel(w1_ref, b1_ref, w2_ref, b2_ref, w3_ref, freq_ref, ld_ref, fs_ref, nyq_ref, *, L):
    ti = lax.broadcasted_iota(jnp.int32, (L, 128), 0)
    t = ti.astype(F32)
    j = lax.broadcasted_iota(jnp.int32, (L, 128), 1)
    band = jnp.where(j <= HY_BANDS, j - 1, j - 1 - HY_BANDS).astype(F32)
    ang = (2.0 * math.pi / L) * t * band
    tn = t / L
    z = jnp.where(j == 0, tn, jnp.where(j <= HY_BANDS, jnp.cos(ang),
                                        jnp.where(j <= 2 * HY_BANDS, -jnp.sin(ang), 0.0)))
    a = jnp.sin(freq_ref[0:1, :] * (_dot_hi(z, w1_ref[...]) + b1_ref[...]))
    a = jnp.sin(freq_ref[1:2, :] * (_dot_hi(a, w2_ref[...]) + b2_ref[...]))
    filt = _dot_hi(a, w3_ref[...])
    tcol = lax.broadcasted_iota(jnp.int32, (L, 4 * HY_C), 0)
    filt = filt * jnp.exp(-(tcol.astype(F32) / L) * jnp.exp(ld_ref[...]))
    t1 = lax.broadcasted_iota(jnp.int32, (L, HY_C), 0)
    sign = jnp.where(t1 % 2 == 0, 1.0, -1.0)
    for n in range(2):
        fwd = filt[:, 2 * HY_C * n:2 * HY_C * n + HY_C]
        bwd = jnp.where(t1 == 0, 0.0, filt[:, 2 * HY_C * n + HY_C:2 * HY_C * (n + 1)])
        tot = fwd + bwd
        fs_ref[:, HY_C * n:HY_C * (n + 1)] = tot
        fs_ref[:, 2 * HY_C + HY_C * n:2 * HY_C + HY_C * (n + 1)] = fwd - bwd
        nyq_ref[:, HY_C * n:HY_C * (n + 1)] = (tot * sign).sum(0, keepdims=True)


def _hy_filter(L, w1p, b1, w2, b2, w3, freq, ld):
    full = lambda s: pl.BlockSpec(s, lambda: tuple(0 for _ in s))
    return pl.pallas_call(
        functools.partial(_hy_filter_kernel, L=L),
        in_specs=[full((128, 64)), full((1, 64)), full((64, 64)), full((1, 64)), full((64, 4 * HY_C)),
                  full((2, 64)), full((1, 4 * HY_C))],
        out_specs=[full((L, 4 * HY_C)), full((1, 2 * HY_C))],
        out_shape=[jax.ShapeDtypeStruct((L, 4 * HY_C), F32), jax.ShapeDtypeStruct((1, 2 * HY_C), F32)],
        compiler_params=pltpu.CompilerParams(vmem_limit_bytes=VMEM_LIMIT),
        name="hyena_filter",
    )(w1p, b1, w2, b2, w3, freq, ld)


def _hy_gdft_kernel(cm_ref, sm_ref, fs_ref, nyq_ref, gr_ref, gi_ref, *, tm):
    m = pl.program_id(0)
    f = fs_ref[...].astype(BF16)
    gr_ref[...] = _dot(cm_ref[...], f[:, :2 * HY_C])
    gi = _dot(sm_ref[...], f[:, 2 * HY_C:])
    row = m * tm + lax.broadcasted_iota(jnp.int32, (tm, 2 * HY_C), 0)
    gi_ref[...] = jnp.where(row == 0, nyq_ref[...], gi)


def _hy_gdft(cm, sm, fs, nyq, L, tm):
    return pl.pallas_call(
        functools.partial(_hy_gdft_kernel, tm=tm),
        grid=(L // tm,),
        in_specs=[pl.BlockSpec((tm, L), lambda m: (m, 0)), pl.BlockSpec((tm, L), lambda m: (m, 0)),
                  pl.BlockSpec((L, 4 * HY_C), lambda m: (0, 0)), pl.BlockSpec((1, 2 * HY_C), lambda m: (0, 0))],
        out_specs=[pl.BlockSpec((tm, 2 * HY_C), lambda m: (m, 0))] * 2,
        out_shape=[jax.ShapeDtypeStruct((L, 2 * HY_C), F32)] * 2,
        compiler_params=_cp("arbitrary"),
        name="hyena_filter_dft",
    )(cm, sm, fs, nyq)


def _hy_fwd_kernel(cm_ref, sm_ref, z_ref, gr_ref, gi_ref, yr_ref, yi_ref, *, L, tm, ns):
    m = pl.program_id(1)
    gr = gr_ref[...]
    gi = gi_ref[...]
    row0 = (m * tm + lax.broadcasted_iota(jnp.int32, (tm, HY_C), 0)) == 0
    s = jnp.where(row0, 0.5 / L, 1.0 / L)
    for g in range(ns):
        zb = z_ref[g * L:(g + 1) * L, :].astype(BF16)
        zr = _dot(cm_ref[...], zb)
        zi = _dot(sm_ref[...], zb)
        zigi = zi * gi
        yr_ref[g * tm:(g + 1) * tm, :] = ((zr * gr - jnp.where(row0, 0.0, zigi)) * s).astype(BF16)
        yi_ref[g * tm:(g + 1) * tm, :] = (jnp.where(row0, zigi, zr * gi + zi * gr) * s).astype(BF16)


def _hy_seqs_per_step(NB, Lb, tm):
    ns = max(1, 2048 // Lb) if tm == Lb else 1
    while NB % ns:
        ns //= 2
    return ns


def _hy_fwd(cm, sm, z, gr, gi, n, NB, Lb, tm):
    T = z.shape[0]
    nm = Lb // tm
    ns = _hy_seqs_per_step(NB, Lb, tm)
    return pl.pallas_call(
        functools.partial(_hy_fwd_kernel, L=Lb, tm=tm, ns=ns),
        grid=(NB // ns, nm),
        in_specs=[pl.BlockSpec((tm, Lb), lambda b, m: (m, 0)), pl.BlockSpec((tm, Lb), lambda b, m: (m, 0)),
                  pl.BlockSpec((ns * Lb, HY_C), lambda b, m: (b, 0)),
                  pl.BlockSpec((tm, HY_C), lambda b, m: (m, n)), pl.BlockSpec((tm, HY_C), lambda b, m: (m, n))],
        out_specs=[pl.BlockSpec((ns * tm, HY_C), lambda b, m: (b * nm + m, 0))] * 2,
        out_shape=[jax.ShapeDtypeStruct((T, HY_C), BF16)] * 2,
        compiler_params=_cp("arbitrary", "arbitrary"),
        name="hyena_fwd_dft",
    )(cm, sm, z, gr, gi)


def _hy_inv_kernel(cm_ref, smt_ref, yr_ref, yi_ref, z_ref, g_ref, skip_ref, o_ref, *, L, tm, ns):
    for g in range(ns):
        seq = slice(g * L, (g + 1) * L)
        out = slice(g * tm, (g + 1) * tm)
        conv = _dot(cm_ref[...], yr_ref[seq, :]) + _dot(smt_ref[...], yi_ref[seq, :])
        o_ref[out, :] = g_ref[out, :] * (conv + skip_ref[...] * z_ref[out, :])


def _hy_inv(cm, smt, yr, yi, z, gate, skip, n, NB, Lb, tm):
    T = z.shape[0]
    nm = Lb // tm
    ns = _hy_seqs_per_step(NB, Lb, tm)
    tile = pl.BlockSpec((ns * tm, HY_C), lambda b, m: (b * nm + m, 0))
    seq = pl.BlockSpec((ns * Lb, HY_C), lambda b, m: (b, 0))
    return pl.pallas_call(
        functools.partial(_hy_inv_kernel, L=Lb, tm=tm, ns=ns),
        grid=(NB // ns, nm),
        in_specs=[pl.BlockSpec((tm, Lb), lambda b, m: (m, 0)), pl.BlockSpec((tm, Lb), lambda b, m: (m, 0)),
                  seq, seq, tile, tile, pl.BlockSpec((None, 1, HY_C), lambda b, m: (n, 0, 0))],
        out_specs=tile,
        out_shape=jax.ShapeDtypeStruct((T, HY_C), F32),
        compiler_params=_cp("arbitrary", "arbitrary"),
        name="hyena_inv_dft",
    )(cm, smt, yr, yi, z, gate, skip)


def _dft_mats(L):
    k = jnp.arange(L, dtype=jnp.int32)
    blk = 64

    def trig(mult):
        ang = ((mult[:, None] * k[None, :]) % (2 * L)).astype(F32) * (math.pi / L)
        return jnp.cos(ang), jnp.sin(ang)

    ca, sa = trig(jnp.arange(L // blk, dtype=jnp.int32) * blk)
    cb, sb = trig(jnp.arange(blk, dtype=jnp.int32))
    cm = (ca[:, None, :] * cb[None] - sa[:, None, :] * sb[None]).reshape(L, L)
    s = -(sa[:, None, :] * cb[None] + ca[:, None, :] * sb[None]).reshape(L, L)
    alt = jnp.where(k % 2 == 0, 1.0, -1.0).astype(F32)
    sm = jnp.where(k[:, None] == 0, alt[None, :], s)
    smt = jnp.where(k[None, :] == 0, alt[:, None], s)
    return cm.astype(BF16), sm.astype(BF16), smt.astype(BF16)


def _merge_kernel(oa_ref, ob_ref, oc_ref, od_ref, g0_ref, g1_ref, g2_ref, g3_ref, wb_ref, wo_ref, x_ref, mod_ref,
                  lg_ref, lb_ref, *rest):
    x1_ref, h2_ref, hp_ref = rest[-3:]
    acc = None
    for o_ref, g_ref, i in ((oa_ref, g0_ref, 0), (ob_ref, g1_ref, 1), (oc_ref, g2_ref, 2), (od_ref, g3_ref, 3)):
        y = _sigmoid(g_ref[...].astype(F32)) * _dot(o_ref[...].astype(BF16), wb_ref[i])
        acc = y if acc is None else acc + y
    mix = _dot(acc.astype(BF16), wo_ref[...])
    m = mod_ref[...]
    x1 = _layer_norm(DN_ALPHA * x_ref[...] + m[2:3, :] * mix, lg_ref[...], lb_ref[...])
    x1_ref[...] = x1
    h2 = x1 * (1.0 + m[4:5, :]) + m[3:4, :]
    h2_ref[...] = h2
    hp_ref[...] = _pack_pairs(h2)


def _merge(outs, gates, wb, wo, x, mod, lg, lb, mod_row, bm, after=None):
    T = x.shape[0]
    row = lambda w: pl.BlockSpec((bm, w), lambda i: (i, 0))
    gspec = lambda n: pl.BlockSpec((bm, D), lambda i: (i, n))
    fixed2 = lambda s: pl.BlockSpec(s, lambda i: (0, 0))
    dep_specs, dep_args = _after(after)
    return pl.pallas_call(
        _merge_kernel,
        grid=(T // bm,),
        in_specs=[row(256)] * 4 + [gspec(0), gspec(1), gspec(2), gspec(3),
                                   pl.BlockSpec((4, 256, D), lambda i: (0, 0, 0)), fixed2((D, D)), row(D),
                                   pl.BlockSpec((None, 6, D), lambda i: (mod_row(i), 0, 0)),
                                   fixed2((1, D)), fixed2((1, D))] + dep_specs,
        out_specs=[row(D), row(D), row(D // 2)],
        out_shape=[jax.ShapeDtypeStruct((T, D), F32)] * 2 + [jax.ShapeDtypeStruct((T, D // 2), jnp.int32)],
        compiler_params=_cp("arbitrary"),
        name="merge_norm",
    )(*outs, gates, gates, gates, gates, wb, wo, x, mod, lg, lb, *dep_args)


def _router_kernel(h_ref, rt_ref, bias_ref, g_ref, rank_ref, cnt_ref, *, tt):
    per = N_EXPERTS // N_GROUPS
    logits = lax.dot_general(rt_ref[...], h_ref[...], (((1,), (1,)), ((), ())), preferred_element_type=F32,
                             precision=lax.Precision.HIGHEST)
    scores = _sigmoid(logits)
    sel = (scores + bias_ref[...]).reshape(N_GROUPS, per, tt)
    gid = lax.broadcasted_iota(jnp.int32, (N_GROUPS, per, tt), 0).astype(F32)
    jid = lax.broadcasted_iota(jnp.int32, (N_GROUPS, per, tt), 1).astype(F32)
    eid = gid * per + jid
    ninf = -jnp.inf
    m1 = sel.max(1, keepdims=True)
    i1 = jnp.where(sel == m1, jid, float(per)).min(1, keepdims=True)
    m2 = jnp.where(jid == i1, ninf, sel).max(1, keepdims=True)
    gs = m1 + m2
    g1 = lax.broadcasted_iota(jnp.int32, (N_GROUPS, 1, tt), 0).astype(F32)
    chosen = jnp.zeros((N_GROUPS, 1, tt), F32)
    for _ in range(TOPK_GROUPS):
        mx = gs.max(0, keepdims=True)
        gi = jnp.where(gs == mx, g1, float(N_GROUPS)).min(0, keepdims=True)
        pick = g1 == gi
        chosen = jnp.where(pick, 1.0, chosen)
        gs = jnp.where(pick, ninf, gs)
    cand = jnp.where(chosen > 0.0, sel, NEG)
    picked = jnp.zeros((N_GROUPS, per, tt), F32)
    for _ in range(TOP_K):
        mx = cand.max(1, keepdims=True).max(0, keepdims=True)
        ei = jnp.where(cand == mx, eid, float(N_EXPERTS)).min(1, keepdims=True).min(0, keepdims=True)
        pick = eid == ei
        picked = jnp.where(pick, 1.0, picked)
        cand = jnp.where(pick, ninf, cand)
    w = scores.reshape(N_GROUPS, per, tt) * picked
    wsum = w.sum(1, keepdims=True).sum(0, keepdims=True)
    g_ref[...] = (w / wsum * ROUTED_SCALE).reshape(N_EXPERTS, tt)
    pk = picked.reshape(N_EXPERTS, tt)
    t_in = lax.broadcasted_iota(jnp.int32, (tt, tt), 0)
    t_out = lax.broadcasted_iota(jnp.int32, (tt, tt), 1)
    upper = jnp.where(t_in <= t_out, 1.0, 0.0).astype(BF16)

    @pl.when(pl.program_id(0) == 0)
    def _():
        cnt_ref[...] = jnp.zeros_like(cnt_ref)

    before = cnt_ref[:, 0:1]
    rank_ref[...] = jnp.where(pk > 0.0, before + _dot(pk.astype(BF16), upper) - 1.0, -1.0)
    cnt_ref[...] += pk.sum(-1, keepdims=True)


def _router(h2, router_t, bias, tt):
    T = h2.shape[0]
    tile = pl.BlockSpec((N_EXPERTS, tt), lambda i: (0, i))
    return pl.pallas_call(
        functools.partial(_router_kernel, tt=tt),
        grid=(T // tt,),
        in_specs=[pl.BlockSpec((tt, D), lambda i: (i, 0)), pl.BlockSpec((N_EXPERTS, D), lambda i: (0, 0)),
                  pl.BlockSpec((N_EXPERTS, 1), lambda i: (0, 0))],
        out_specs=[tile, tile, pl.BlockSpec((N_EXPERTS, 128), lambda i: (0, 0))],
        out_shape=[jax.ShapeDtypeStruct((N_EXPERTS, T), F32), jax.ShapeDtypeStruct((N_EXPERTS, T), F32),
                   jax.ShapeDtypeStruct((N_EXPERTS, 128), F32)],
        compiler_params=_cp("arbitrary"),
        name="moe_router",
    )(h2, router_t, bias)


def _route_pos_kernel(gate_ref, rank_ref, cnt_ref, pos_ref, w_ref, te_ref, nx_ref, nt_ref, *, tm, nt_max):
    ei = lax.broadcasted_iota(jnp.int32, (N_EXPERTS, N_EXPERTS), 0)
    ej = lax.broadcasted_iota(jnp.int32, (N_EXPERTS, N_EXPERTS), 1)
    below = jnp.where(ej < ei, 1.0, 0.0)
    padded = jnp.ceil(cnt_ref[...] * (1.0 / tm)) * tm
    offs = _dot_hi(below, padded)
    rank = rank_ref[...]
    routed = rank >= 0.0
    pos = offs[:, 0:1] + rank
    slot = _dot(below.astype(BF16), jnp.where(routed, 1.0, 0.0).astype(BF16))
    gate = gate_ref[...]
    for k in range(TOP_K):
        mine = routed & (slot == float(k))
        pos_ref[k:k + 1, :] = jnp.where(mine, pos, 0.0).sum(0, keepdims=True).astype(jnp.int32)
        w_ref[k:k + 1, :] = jnp.where(mine, gate, 0.0).sum(0, keepdims=True)
    ends = (offs + padded)[:, 0:1]
    first = (lax.broadcasted_iota(jnp.int32, (N_EXPERTS, nt_max), 1) * tm).astype(F32)
    te = jnp.minimum(jnp.where(ends <= first, 1.0, 0.0).sum(0, keepdims=True), N_EXPERTS - 1.0)
    te_ref[...] = te.astype(jnp.int32)
    eid = lax.broadcasted_iota(jnp.int32, (N_EXPERTS, nt_max), 0).astype(F32)
    nx_ref[...] = (jnp.where(eid == te, ends, 0.0).sum(0, keepdims=True) * (1.0 / tm)).astype(jnp.int32)
    nt_ref[...] = (padded.sum(0, keepdims=True) * (1.0 / tm)).astype(jnp.int32)


def _route_pos(gate_t, rank, cnt, tt, tm, nt_max):
    T = gate_t.shape[1]
    tile = pl.BlockSpec((N_EXPERTS, tt), lambda i: (0, i))
    out = pl.BlockSpec((TOP_K, tt), lambda i: (0, i))
    return pl.pallas_call(
        functools.partial(_route_pos_kernel, tm=tm, nt_max=nt_max),
        grid=(T // tt,),
        in_specs=[tile, tile, pl.BlockSpec((N_EXPERTS, 128), lambda i: (0, 0))],
        out_specs=[out, out, pl.BlockSpec((1, nt_max), lambda i: (0, 0)), pl.BlockSpec((1, nt_max), lambda i: (0, 0)),
                   pl.BlockSpec((1, 128), lambda i: (0, 0))],
        out_shape=[jax.ShapeDtypeStruct((TOP_K, T), jnp.int32), jax.ShapeDtypeStruct((TOP_K, T), F32),
                   jax.ShapeDtypeStruct((1, nt_max), jnp.int32), jax.ShapeDtypeStruct((1, nt_max), jnp.int32),
                   jax.ShapeDtypeStruct((1, 128), jnp.int32)],
        compiler_params=_cp("arbitrary"),
        name="moe_positions",
    )(gate_t, rank, cnt)


def _gmm_kernel(te_ref, nx_ref, nt_ref, xs_ref, w1_hbm, w3_hbm, w2_hbm, *rest, l):
    ys_ref, b1_ref, b3_ref, b2_ref, f1_ref, f3_ref, f2_ref, seg_ref, sem = rest[-9:]
    j = pl.program_id(0)
    live = j < nt_ref[0]
    new_expert = (j == 0) | (te_ref[j] != te_ref[jnp.maximum(j - 1, 0)])

    def fetch(e, slot):
        return [pltpu.make_async_copy(w_hbm.at[l, e], f_ref.at[slot], sem.at[i, slot])
                for i, (w_hbm, f_ref) in enumerate(((w1_hbm, f1_ref), (w3_hbm, f3_ref), (w2_hbm, f2_ref)))]

    @pl.when(live & new_expert)
    def _():
        @pl.when(j == 0)
        def _():
            seg_ref[0] = 0
            for c in fetch(te_ref[0], 0):
                c.start()

        slot = lax.rem(seg_ref[0], 2)
        for c in fetch(te_ref[j], slot):
            c.wait()
        b1_ref[...] = f1_ref[slot].astype(BF16)
        b3_ref[...] = f3_ref[slot].astype(BF16)
        b2_ref[...] = f2_ref[slot].astype(BF16)
        nxt = nx_ref[j]

        @pl.when(nxt < nt_ref[0])
        def _():
            for c in fetch(te_ref[nxt], 1 - slot):
                c.start()

        seg_ref[0] = seg_ref[0] + 1

    @pl.when(live)
    def _():
        xa, xb = _unpack_pairs(xs_ref[...])
        xa, xb = xa.astype(BF16), xb.astype(BF16)
        half = D // 2
        a = _dot(xa, b1_ref[:half, :]) + _dot(xb, b1_ref[half:, :])
        b = _dot(xa, b3_ref[:half, :]) + _dot(xb, b3_ref[half:, :])
        hid = (a * _sigmoid(a) * b).astype(BF16)
        ys_ref[...] = _pack_pairs(_dot(hid, b2_ref[...]))


def _gmm(te, nx, nt, xs, w1, w3, w2, l, tm, after=None):
    n_slots = xs.shape[0]
    ds = D_EXPERT
    rows = pl.BlockSpec((tm, D // 2), lambda j, te, nx, nt: (jnp.minimum(j, nt[0] - 1), 0))
    hbm = pl.BlockSpec(memory_space=pl.ANY)
    dep_specs, dep_args = _after(after)
    return pl.pallas_call(
        functools.partial(_gmm_kernel, l=l),
        grid_spec=pltpu.PrefetchScalarGridSpec(
            num_scalar_prefetch=3,
            grid=(n_slots // tm,),
            in_specs=[rows, hbm, hbm, hbm] + dep_specs,
            out_specs=rows,
            scratch_shapes=[pltpu.VMEM((D, ds), BF16), pltpu.VMEM((D, ds), BF16), pltpu.VMEM((ds, D), BF16),
                            pltpu.VMEM((2, D, ds), F32), pltpu.VMEM((2, D, ds), F32), pltpu.VMEM((2, ds, D), F32),
                            pltpu.SMEM((1,), jnp.int32), pltpu.SemaphoreType.DMA((3, 2))]),
        out_shape=jax.ShapeDtypeStruct((n_slots, D // 2), jnp.int32),
        compiler_params=_cp("arbitrary"),
        name="moe_grouped_ffn",
    )(te, nx, nt, xs, w1, w3, w2, *dep_args)


def _combine_kernel(yk_ref, w_ref, hp_ref, s1_ref, s3_ref, s2_ref, x_ref, mod_ref, lg_ref, lb_ref, o_ref):
    w = w_ref[...]
    acc_a = acc_b = None
    for k in range(TOP_K):
        ya, yb = _unpack_pairs(yk_ref[k])
        wk = w[:, k:k + 1]
        acc_a = wk * ya if acc_a is None else acc_a + wk * ya
        acc_b = wk * yb if acc_b is None else acc_b + wk * yb
    ha, hb = _unpack_pairs(hp_ref[...])
    ha, hb = ha.astype(BF16), hb.astype(BF16)
    half = D // 2
    a = _dot(ha, s1_ref[:half, :]) + _dot(hb, s1_ref[half:, :])
    b = _dot(ha, s3_ref[:half, :]) + _dot(hb, s3_ref[half:, :])
    y = jnp.concatenate([acc_a, acc_b], axis=1) + _dot((a * _sigmoid(a) * b).astype(BF16), s2_ref[...])
    m = mod_ref[...]
    o_ref[...] = _layer_norm(DN_ALPHA * x_ref[...] + m[5:6, :] * y, lg_ref[...], lb_ref[...])


def _combine(yk, w, hp, s1, s3, s2, x1, mod, lg, lb, mod_row, bm):
    T = x1.shape[0]
    ds = D_EXPERT
    row = lambda n: pl.BlockSpec((bm, n), lambda i: (i, 0))
    fixed = lambda s: pl.BlockSpec(s, lambda i: (0, 0))
    return pl.pallas_call(
        _combine_kernel,
        grid=(T // bm,),
        in_specs=[pl.BlockSpec((TOP_K, bm, D // 2), lambda i: (0, i, 0)), row(TOP_K), row(D // 2),
                  fixed((D, ds)), fixed((D, ds)), fixed((ds, D)), row(D),
                  pl.BlockSpec((None, 6, D), lambda i: (mod_row(i), 0, 0)), fixed((1, D)), fixed((1, D))],
        out_specs=row(D),
        out_shape=jax.ShapeDtypeStruct((T, D), F32),
        compiler_params=_cp("arbitrary"),
        name="moe_combine_norm",
    )(yk, w, hp, s1, s3, s2, x1, mod, lg, lb)


def _sc_worker():
    return lax.axis_index("s") * SC_CORES + lax.axis_index("c")


def _sc_mesh():
    return plsc.VectorSubcoreMesh(core_axis_name="c", subcore_axis_name="s")


def _sc_gather(table, idx):
    N, W = idx.shape[0], table.shape[1]
    per_w = N // SC_WORKERS
    n_chunks = per_w // SC_ROWS

    def body(table_hbm, idx_hbm, out_hbm, idx_v, rows_v, sem):
        base = _sc_worker() * per_w
        pltpu.sync_copy(idx_hbm.at[pl.ds(base, per_w)], idx_v)

        @pl.loop(0, n_chunks)
        def _(c):
            off = pl.multiple_of(c * SC_ROWS, SC_ROWS)
            pltpu.async_copy(table_hbm.at[idx_v.at[pl.ds(off, SC_ROWS)]], rows_v, sem).wait()
            pltpu.sync_copy(rows_v, out_hbm.at[pl.ds(base + off, SC_ROWS)])

    return pl.kernel(
        body, out_type=jax.ShapeDtypeStruct((N, W), table.dtype), mesh=_sc_mesh(),
        scratch_types=[pltpu.VMEM((per_w,), jnp.int32), pltpu.VMEM((SC_ROWS, W), table.dtype),
                       pltpu.SemaphoreType.DMA],
        name="sc_gather",
    )(table, idx)


def _sc_dispatch(pos, table, n_slots):
    NP, (T, W) = pos.shape[0], table.shape
    per_w = n_slots // SC_WORKERS
    n_chunks = per_w // SC_ROWS
    scan = 8192

    def body(pos_hbm, table_hbm, out_hbm, pos_v, src_v, rows_v, sem):
        base = _sc_worker() * per_w
        lane = lax.iota(jnp.int32, SC_LANES)

        @pl.loop(0, per_w // SC_LANES)
        def _(j):
            o = pl.multiple_of(j * SC_LANES, SC_LANES)
            src_v[pl.ds(o, SC_LANES)] = (base + o + lane) & (T - 1)

        @pl.loop(0, NP // scan)
        def _(c):
            pltpu.sync_copy(pos_hbm.at[pl.ds(pl.multiple_of(c * scan, scan), scan)], pos_v)

            @pl.loop(0, scan // SC_LANES)
            def _(v):
                o = pl.multiple_of(v * SC_LANES, SC_LANES)
                p = pos_v[pl.ds(o, SC_LANES)] - base
                mine = (p >= 0) & (p < per_w)
                tok = (c * scan + o + lane) & (T - 1)
                plsc.store_scatter(src_v, [jnp.where(mine, p, 0)], tok, mask=mine)

        @pl.loop(0, n_chunks)
        def _(c):
            off = pl.multiple_of(c * SC_ROWS, SC_ROWS)
            pltpu.async_copy(table_hbm.at[src_v.at[pl.ds(off, SC_ROWS)]], rows_v, sem).wait()
            pltpu.sync_copy(rows_v, out_hbm.at[pl.ds(base + off, SC_ROWS)])

    return pl.kernel(
        body, out_type=jax.ShapeDtypeStruct((n_slots, W), table.dtype), mesh=_sc_mesh(),
        scratch_types=[pltpu.VMEM((scan,), jnp.int32), pltpu.VMEM((per_w,), jnp.int32),
                       pltpu.VMEM((SC_ROWS, W), table.dtype), pltpu.SemaphoreType.DMA],
        compiler_params=pltpu.CompilerParams(needs_layout_passes=False),
        name="sc_dispatch",
    )(pos, table)


def _caches_kernel(*refs, nb, S):
    n_in = 6 * DEPTH
    outs = refs[n_in:]
    l = pl.program_id(0)
    for a in range(DEPTH):
        @pl.when(l == a)
        def _(a=a):
            ckv, kpe, wk, wv, nk, nv = refs[6 * a:6 * (a + 1)]
            for g in range(nb):
                rows = slice(g * S, (g + 1) * S)
                outs[0][g] = ckv[rows, :]
                outs[1][g] = kpe[rows, 64:96]
                outs[2][g] = wk[rows, :]
                outs[3][g] = wv[rows, :]
                outs[4][g] = nk[rows, :]
                outs[5][g] = nv[rows, :]


def _emit_caches(projs, ckvs, B, S):
    nb = 4
    while B % nb:
        nb //= 2

    def layer_specs(a):
        row = lambda l, b: jnp.where(l == a, b, 0)
        col = lambda w, off: pl.BlockSpec((nb * S, w), lambda l, b: (row(l, b), off // w))
        return [pl.BlockSpec((nb * S, 128), lambda l, b: (row(l, b), 0)), col(128, P_KPE), col(128, P_WK),
                col(128, P_WV), col(256, P_NK), col(256, P_NV)]

    in_specs, args = [], []
    for a in range(DEPTH):
        in_specs += layer_specs(a)
        args += [ckvs[a]] + [projs[a]] * 5
    widths = (128, 32, 128, 128, 256, 256)
    return pl.pallas_call(
        functools.partial(_caches_kernel, nb=nb, S=S),
        grid=(DEPTH, B // nb),
        in_specs=in_specs,
        out_specs=[pl.BlockSpec((nb, None, S, w), lambda l, b: (b, l, 0, 0)) for w in widths],
        out_shape=[jax.ShapeDtypeStruct((B, DEPTH, S, w), F32) for w in widths],
        compiler_params=_cp("arbitrary", "arbitrary"),
        name="context_tensors",
    )(*args)


def _rot_cols(w, q):
    a, b, c, d = w[..., :q], w[..., q:2 * q], w[..., 2 * q:3 * q], w[..., 3 * q:]
    return jnp.concatenate([-b, a, -d, c], -1)


def _prep_w_in(w):
    z = lambda n: jnp.zeros((D, n), w.dtype)
    qlat, ckv, kpe, hy = w[:, 0:256], w[:, 256:384], w[:, 384:416], w[:, 416:1184]
    wq, wk, wv = w[:, 1184:1440], w[:, 1440:1568], w[:, 1568:1696]
    nq, nk, nv, gate = w[:, 1696:1952], w[:, 1952:2208], w[:, 2208:2464], w[:, 2464:]
    wq_r = _rot_cols(wq.reshape(D, 4, 64), 16).reshape(D, 256)
    wk_r = _rot_cols(wk.reshape(D, 2, 64), 16).reshape(D, 128)
    kpe_r = _rot_cols(kpe, 8)
    cols = [qlat, ckv, z(64), kpe, z(32), hy, wq, wk, wv, nq, nk, nv, wq_r, wk_r, z(64), kpe_r, z(32), gate]
    return jnp.concatenate(cols, 1).astype(BF16)


def _prep_mla(w_uq, w_ukv):
    uq = w_uq.reshape(256, 4, 96)
    nope, pe = uq[..., :64], uq[..., 64:]
    z32 = jnp.zeros((256, 4, 32), w_uq.dtype)
    z64 = jnp.zeros((256, 4, 64), w_uq.dtype)
    wcat = jnp.concatenate([nope, pe, z32], -1).reshape(256, 512).astype(BF16)
    wrot = jnp.concatenate([z64, _rot_cols(pe, 8), z32], -1).reshape(256, 512).astype(BF16)
    ukv = w_ukv.reshape(128, 4, 128)
    wk = jnp.concatenate([ukv[..., :64], jnp.zeros((128, 4, 64), w_ukv.dtype)], -1).reshape(128, 512).astype(BF16)
    wv = ukv[..., 64:].reshape(128, 256).astype(BF16)
    return wcat, wrot, wk, wv


def _rope_tab(L, q):
    t = jnp.arange(L)
    inv = ROPE_BASE ** (-jnp.arange(q, dtype=F32) / q)
    ar = (t // GRID_W).astype(F32)[:, None] * inv[None, :]
    ac = (t % GRID_W).astype(F32)[:, None] * inv[None, :]
    cos = jnp.concatenate([jnp.cos(ar), jnp.cos(ar), jnp.cos(ac), jnp.cos(ac)], 1)
    sin = jnp.concatenate([jnp.sin(ar), jnp.sin(ar), jnp.sin(ac), jnp.sin(ac)], 1)
    return cos, sin


def _rope_tables(L):
    c8, s8 = _rope_tab(L, 8)
    c16, s16 = _rope_tab(L, 16)
    one, zero = jnp.ones((L, 64), F32), jnp.zeros((L, 64), F32)
    z32 = jnp.zeros((L, 32), F32)
    mla_q = (jnp.tile(jnp.concatenate([one, c8, z32], 1), (1, 4)), jnp.tile(jnp.concatenate([zero, s8, z32], 1), (1, 4)))
    mla_k = (jnp.concatenate([zero, c8, z32], 1), jnp.concatenate([zero, s8, z32], 1))
    win = (jnp.tile(c16, (1, 4)), jnp.tile(s16, (1, 4)), jnp.tile(c16, (1, 2)), jnp.tile(s16, (1, 2)))
    return mla_q + mla_k, win


def _hyena(proj, lp, dft, NB, Lb):
    cm, sm, smt = dft
    tm = min(Lb, 512)
    v, x1, x2 = _short_conv(proj, lp["hy_conv_w"], lp["hy_conv_b"].reshape(1, -1), NB, Lb)
    w1p = jnp.pad(lp["hy_w1"], ((0, 128 - lp["hy_w1"].shape[0]), (0, 0)))
    fs, nyq = _hy_filter(Lb, w1p, lp["hy_b1"].reshape(1, -1), lp["hy_w2"], lp["hy_b2"].reshape(1, -1), lp["hy_w3"],
                         lp["hy_sin_freq"], lp["hy_log_decay"].reshape(1, -1))
    gr, gi = _hy_gdft(cm, sm, fs, nyq, Lb, tm)
    skip = lp["hy_skip"].reshape(2, 1, HY_C)
    z = v
    for n, gate in enumerate((x1, x2)):
        yr, yi = _hy_fwd(cm, sm, z, gr, gi, n, NB, Lb, tm)
        z = _hy_inv(cm, smt, yr, yi, z, gate, skip, n, NB, Lb, tm)
    return z


def _layer_steps(x, mod, lp, l, NB, Lb, mod_row_of_batch, dft, cache=None, tabs=None, na_bias=None):
    T = NB * Lb
    latent = cache is not None
    bm = 256
    rows_of = lambda n: (lambda i: mod_row_of_batch((i * n) // Lb))
    mod_row = rows_of(bm)
    span = Lb if latent else T
    bmp = min(span, 1024)
    proj, gates = _in_proj(x, mod, lp["w_in_p"], rows_of(bmp), bmp)

    gq, gkv = lp["mla_q_norm"].reshape(1, -1), lp["mla_kv_norm"].reshape(1, -1)
    wcat, wrot, wk, wv = lp["mla_w"]
    q_all, ckv_n, kpe_r = _mla_q(proj, gq, gkv, wcat, wrot, tabs[0] if latent else None, Lb, min(span, 512))
    if latent:
        ckv_c, kpe_c, kc_c, vc_c, kd_c, vd_c = cache
        Lc = ckv_c.shape[1]
        kpe_cp = jnp.pad(kpe_c, ((0, 0), (0, 0), (64, 32)))
        ckv_all = jnp.concatenate([ckv_c, ckv_n.reshape(NB, Lb, 128)], 1).reshape(NB * (Lc + Lb), 128)
        kpe_all = jnp.concatenate([kpe_cp, kpe_r.reshape(NB, Lb, 128)], 1).reshape(NB * (Lc + Lb), 128)
        k_all, v_all = _mla_kv(ckv_all, kpe_all, wk, wv, 512)
        oc = _lat_win_attention(proj, kc_c.reshape(NB, Lc, 128), vc_c.reshape(NB, Lc, 128), tabs[1],
                                lp["win_sink"], NB, Lb)
        od = _lat_na_attention(proj, kd_c.reshape(NB, Lc, 256), vd_c.reshape(NB, Lc, 256), na_bias, NB, Lb)
        ob = _hyena(proj, lp, dft, NB, Lb)
        after = yield "projected", od
        oa = _lat_mla_attention(q_all, k_all, v_all, NB, Lb, Lc + Lb, 256, after=after)
        after = None
    else:
        k_all, v_all = _mla_kv(ckv_n, kpe_r, wk, wv, 512)
        oa, oc, od = _ctx_attention(proj, q_all, k_all, v_all, lp["win_sink"], NB, Lb)
        ob = _hyena(proj, lp, dft, NB, Lb)
        after = yield "projected", oa

    bmm = min(span, 512)
    x1, h2, hp = _merge((oa, ob, oc, od), gates, lp["w_branch_b"], lp["w_out_b"], x, mod,
                        lp["ln1_g"].reshape(1, -1), lp["ln1_b"].reshape(1, -1), rows_of(bmm), bmm, after=after)
    n_slots = T * TOP_K + N_EXPERTS * MOE_TM
    gate_t, rank, cnt = _router(h2, lp["moe_router"].T, lp["moe_bias"].reshape(-1, 1), 512)
    pos, w8, te, nx, nt = _route_pos(gate_t, rank, cnt, 512, MOE_TM, n_slots // MOE_TM)
    xs = _sc_dispatch(pos.reshape(-1), hp, n_slots)
    after = yield "dispatched", None
    ys = _gmm(te.reshape(-1), nx.reshape(-1), nt.reshape(-1)[:1], xs, lp["moe_w1"], lp["moe_w3"], lp["moe_w2"], l,
              MOE_TM, after=after)
    yield "ffn", ys
    yk = _sc_gather(ys, pos.reshape(-1)).reshape(TOP_K, T, D // 2)
    x2 = _combine(yk, w8.T, hp, lp["sh_w1_b"], lp["sh_w3_b"], lp["sh_w2_b"], x1, mod,
                  lp["ln2_g"].reshape(1, -1), lp["ln2_b"].reshape(1, -1), mod_row, bm)
    yield "done", (x2, proj, ckv_n)


def kernel(x_prompt, x_sample, cache_mla_ckv, cache_mla_kpe, cache_win_k, cache_win_v, cache_na_k, cache_na_v, c, c_ctx, w_ada, b_ada, w_in, mla_q_norm, mla_kv_norm, mla_w_uq, mla_w_ukv, hy_conv_w, hy_conv_b, hy_w1, hy_b1, hy_w2, hy_b2, hy_w3, hy_sin_freq, hy_log_decay, hy_skip, win_sink, na_rpb, w_branch, w_out, ln1_g, ln1_b, ln2_g, ln2_b, moe_router, moe_bias, moe_w1, moe_w3, moe_w2, sh_w1, sh_w3, sh_w2):
    B, S, _ = x_prompt.shape
    DB, DS, _ = x_sample.shape
    xp = x_prompt.reshape(B * S, D)
    xs = x_sample.reshape(DB * DS, D)
    cvec = jnp.concatenate([c_ctx[None, :], c, jnp.zeros((8 - 1 - DB, D), F32)], 0)
    dft_ctx = _dft_mats(S)
    dft_lat = _dft_mats(DS)
    tabs = _rope_tables(DS)
    projs, ckvs = [], []

    def params(l):
        return dict(w_in_p=_prep_w_in(w_in[l]), mla_q_norm=mla_q_norm[l], mla_kv_norm=mla_kv_norm[l],
                    mla_w=_prep_mla(mla_w_uq[l], mla_w_ukv[l]), hy_conv_w=hy_conv_w[l], hy_conv_b=hy_conv_b[l],
                    hy_w1=hy_w1[l], hy_b1=hy_b1[l], hy_w2=hy_w2[l], hy_b2=hy_b2[l], hy_w3=hy_w3[l],
                    hy_sin_freq=hy_sin_freq[l], hy_log_decay=hy_log_decay[l], hy_skip=hy_skip[l],
                    win_sink=win_sink[l], w_branch_b=w_branch[l].astype(BF16), w_out_b=w_out[l].astype(BF16),
                    ln1_g=ln1_g[l], ln1_b=ln1_b[l], ln2_g=ln2_g[l], ln2_b=ln2_b[l],
                    moe_router=moe_router[l], moe_bias=moe_bias[l], moe_w1=moe_w1, moe_w3=moe_w3, moe_w2=moe_w2,
                    sh_w1_b=sh_w1[l].astype(BF16), sh_w3_b=sh_w3[l].astype(BF16), sh_w2_b=sh_w2[l].astype(BF16))

    lps = [params(l) for l in range(DEPTH)]
    mods = [_modulation(cvec, w_ada, b_ada, l) for l in range(DEPTH)]

    def ctx_layer(l, x):
        return _layer_steps(x, mods[l], lps[l], l, B, S, lambda b: 0, dft_ctx)

    def lat_layer(l, x):
        cache = (cache_mla_ckv[:, l], cache_mla_kpe[:, l], cache_win_k[:, l], cache_win_v[:, l],
                 cache_na_k[:, l], cache_na_v[:, l])
        return _layer_steps(x, mods[l], lps[l], l, DB, DS, lambda b: 1 + b, dft_lat, cache=cache, tabs=tabs,
                            na_bias=_na_bias(na_rpb[l]))

    ctx = ctx_layer(0, xp)
    next(ctx)
    ctx.send(None)
    for l in range(DEPTH):
        lat = lat_layer(l, xs)
        next(lat)
        ys_ctx = ctx.send(None)[1]
        lat.send(ys_ctx)
        xp, proj, ckv_n = ctx.send(None)[1]
        projs.append(proj)
        ckvs.append(ckv_n)
        if l + 1 < DEPTH:
            ctx = ctx_layer(l + 1, xp)
            attended = next(ctx)[1]
            ys_lat = lat.send(attended)[1]
            ctx.send(ys_lat)
        else:
            lat.send(xp)
        xs = lat.send(None)[1][0]
    ckv, kpe, wk, wv, nk, nv = _emit_caches(projs, ckvs, B, S)
    heads = lambda t, h: t.reshape(B, DEPTH, S, h, HEAD_DIM)
    return (xp.reshape(B, S, D), xs.reshape(DB, DS, D), ckv, kpe, heads(wk, 2), heads(wv, 2), heads(nk, 4),
            heads(nv, 4))
```

```python
import functools
import math

import jax
import jax.numpy as jnp
from jax import lax
from jax.experimental import pallas as pl
from jax.experimental.pallas import tpu as pltpu
from jax.experimental.pallas import tpu_sc as plsc

F32 = jnp.float32
BF16 = jnp.bfloat16

D = 1024
DEPTH = 2
GRID_W = 64
HEAD_DIM = 64
MLA_SCALE = 96 ** -0.5
ATT_SCALE = HEAD_DIM ** -0.5
HY_C = 256
HY_BANDS = 8
NA_KH = 8
NA_KW = 16
N_EXPERTS = 64
N_GROUPS = 8
TOP_K = 8
TOPK_GROUPS = 4
D_EXPERT = 256
ROUTED_SCALE = 2.5
ROPE_BASE = 10000.0
LN_EPS = 1e-5
RMS_EPS = 1e-6
NEG = -1e30
DN_ALPHA = (2 * DEPTH) ** 0.25

P_QLAT, P_CKV, P_KPE, P_HY = 0, 256, 384, 512
P_WQ, P_WK, P_WV = 1280, 1536, 1664
P_NQ, P_NK, P_NV = 1792, 2048, 2304
P_WQR, P_WKR, P_KPER, P_GATE = 2560, 2816, 2944, 3072
N_PROJ = 7168

VMEM_LIMIT = 56 * 1024 * 1024

SC_CORES = 2
SC_SUBCORES = 16
SC_LANES = 16
SC_WORKERS = SC_CORES * SC_SUBCORES
SC_ROWS = 64

MOE_TM = 512

def _cp(*sem):
    return pltpu.CompilerParams(dimension_semantics=sem, vmem_limit_bytes=VMEM_LIMIT)


def _sigmoid(x):
    return 1.0 / (1.0 + jnp.exp(-x))


def _dot(a, b):
    return jnp.dot(a, b, preferred_element_type=F32)


def _dot_nt(a, b):
    return lax.dot_general(a, b, (((1,), (1,)), ((), ())), preferred_element_type=F32)


def _dot_hi(a, b):
    return jnp.dot(a, b, preferred_element_type=F32, precision=lax.Precision.HIGHEST)


def _pack_pairs(x):
    w = x.shape[1] // 2
    hi = lax.bitcast_convert_type(x[:, :w].astype(BF16).astype(F32), jnp.int32)
    lo = lax.bitcast_convert_type(x[:, w:].astype(BF16).astype(F32), jnp.int32)
    return hi | lax.shift_right_logical(lo, 16)


def _unpack_pairs(p):
    hi = lax.bitcast_convert_type(p & jnp.int32(-65536), F32)
    lo = lax.bitcast_convert_type(lax.shift_left(p, 16), F32)
    return hi, lo


def _layer_norm(x, g, b):
    mu = jnp.mean(x, -1, keepdims=True)
    xc = x - mu
    var = jnp.mean(xc * xc, -1, keepdims=True)
    return xc * lax.rsqrt(var + LN_EPS) * g + b


def _rms_norm(x, g):
    return x * lax.rsqrt(jnp.mean(x * x, -1, keepdims=True) + RMS_EPS) * g


def _mod_kernel(c_ref, w_ref, b_ref, o_ref):
    c = c_ref[...]
    a = (c * _sigmoid(c)).astype(BF16)
    o_ref[...] = _dot(a, w_ref[...].astype(BF16)) + b_ref[...]


def _modulation(cvec, w_ada, b_ada, l):
    out = pl.pallas_call(
        _mod_kernel,
        grid=(6,),
        in_specs=[pl.BlockSpec((8, D), lambda j: (0, 0)),
                  pl.BlockSpec((None, D, D), lambda j: (l, 0, j)),
                  pl.BlockSpec((None, 1, D), lambda j: (l, 0, j))],
        out_specs=pl.BlockSpec((8, D), lambda j: (0, j)),
        out_shape=jax.ShapeDtypeStruct((8, 6 * D), F32),
        compiler_params=_cp("arbitrary"),
        name="modulation",
    )(cvec, w_ada, b_ada.reshape(DEPTH, 1, 6 * D))
    return out.reshape(8, 6, D)


def _inproj_kernel(x_ref, mod_ref, w_ref, o_ref, g_ref, h_ref, *, n_main):
    j = pl.program_id(1)

    @pl.when(j == 0)
    def _():
        m = mod_ref[...]
        h_ref[...] = (x_ref[...] * (1.0 + m[1:2, :]) + m[0:1, :]).astype(BF16)

    y = _dot(h_ref[...], w_ref[...])

    @pl.when(j < n_main)
    def _():
        o_ref[...] = y

    @pl.when(j >= n_main)
    def _():
        g_ref[...] = y.astype(BF16)


def _in_proj(x, mod, w_p, mod_row, bm, bn=1024):
    T = x.shape[0]
    n_main = P_GATE // bn
    return pl.pallas_call(
        functools.partial(_inproj_kernel, n_main=n_main),
        grid=(T // bm, N_PROJ // bn),
        in_specs=[pl.BlockSpec((bm, D), lambda i, j: (i, 0)),
                  pl.BlockSpec((None, 6, D), lambda i, j: (mod_row(i), 0, 0)),
                  pl.BlockSpec((D, bn), lambda i, j: (0, j))],
        out_specs=[pl.BlockSpec((bm, bn), lambda i, j: (i, jnp.minimum(j, n_main - 1))),
                   pl.BlockSpec((bm, bn), lambda i, j: (i, jnp.maximum(j - n_main, 0)))],
        out_shape=[jax.ShapeDtypeStruct((T, P_GATE), F32), jax.ShapeDtypeStruct((T, N_PROJ - P_GATE), BF16)],
        scratch_shapes=[pltpu.VMEM((bm, D), BF16)],
        compiler_params=_cp("arbitrary", "arbitrary"),
        name="in_proj",
    )(x, mod, w_p)


def _mla_q_kernel(*refs, rope):
    if rope:
        (ql_ref, ckv_ref, kpe_ref, kper_ref, gq_ref, gkv_ref, wc_ref, wr_ref,
         cq_ref, sq_ref, ck_ref, sk_ref, q_ref, ckvn_ref, kpeo_ref) = refs
    else:
        ql_ref, ckv_ref, kpe_ref, gq_ref, gkv_ref, wc_ref, q_ref, ckvn_ref, kpeo_ref = refs
    qn = _rms_norm(ql_ref[...], gq_ref[...]).astype(BF16)
    q = _dot(qn, wc_ref[...])
    if rope:
        q = q * cq_ref[...] + _dot(qn, wr_ref[...]) * sq_ref[...]
        kpeo_ref[...] = kpe_ref[...] * ck_ref[...] + kper_ref[...] * sk_ref[...]
    else:
        kpeo_ref[...] = kpe_ref[...]
    q_ref[...] = (q * MLA_SCALE).astype(BF16)
    ckvn_ref[...] = _rms_norm(ckv_ref[...], gkv_ref[...])


def _mla_q(proj, gq, gkv, wcat, wrot, tabs, Lb, bm):
    T = proj.shape[0]
    rope = tabs is not None
    nl = Lb // bm
    col = lambda c: (lambda i: (i, c))
    fixed = lambda i: (0, 0)
    in_specs = [pl.BlockSpec((bm, 256), col(P_QLAT // 256)),
                pl.BlockSpec((bm, 128), col(P_CKV // 128)),
                pl.BlockSpec((bm, 128), col(P_KPE // 128))]
    args = [proj, proj, proj]
    if rope:
        in_specs.append(pl.BlockSpec((bm, 128), col(P_KPER // 128)))
        args.append(proj)
    in_specs += [pl.BlockSpec((1, 256), fixed), pl.BlockSpec((1, 128), fixed), pl.BlockSpec((256, 512), fixed)]
    args += [gq, gkv, wcat]
    if rope:
        cq, sq, ck, sk = tabs
        pos = lambda i: (i % nl, 0)
        in_specs += [pl.BlockSpec((256, 512), fixed), pl.BlockSpec((bm, 512), pos), pl.BlockSpec((bm, 512), pos),
                     pl.BlockSpec((bm, 128), pos), pl.BlockSpec((bm, 128), pos)]
        args += [wrot, cq, sq, ck, sk]
    return pl.pallas_call(
        functools.partial(_mla_q_kernel, rope=rope),
        grid=(T // bm,),
        in_specs=in_specs,
        out_specs=[pl.BlockSpec((bm, 512), lambda i: (i, 0)),
                   pl.BlockSpec((bm, 128), lambda i: (i, 0)),
                   pl.BlockSpec((bm, 128), lambda i: (i, 0))],
        out_shape=[jax.ShapeDtypeStruct((T, 512), BF16),
                   jax.ShapeDtypeStruct((T, 128), F32),
                   jax.ShapeDtypeStruct((T, 128), F32)],
        compiler_params=_cp("arbitrary"),
        name="mla_q",
    )(*args)


def _mla_kv_kernel(ckv_ref, kpe_ref, wk_ref, wv_ref, k_ref, v_ref):
    c = ckv_ref[...].astype(BF16)
    kpe = kpe_ref[...]
    k_ref[...] = (_dot(c, wk_ref[...]) + jnp.concatenate([kpe] * 4, axis=1)).astype(BF16)
    v_ref[...] = _dot(c, wv_ref[...]).astype(BF16)


def _mla_kv(ckv, kpe, wk, wv, bm):
    Tk = ckv.shape[0]
    return pl.pallas_call(
        _mla_kv_kernel,
        grid=(Tk // bm,),
        in_specs=[pl.BlockSpec((bm, 128), lambda i: (i, 0)),
                  pl.BlockSpec((bm, 128), lambda i: (i, 0)),
                  pl.BlockSpec((128, 512), lambda i: (0, 0)),
                  pl.BlockSpec((128, 256), lambda i: (0, 0))],
        out_specs=[pl.BlockSpec((bm, 512), lambda i: (i, 0)),
                   pl.BlockSpec((bm, 256), lambda i: (i, 0))],
        out_shape=[jax.ShapeDtypeStruct((Tk, 512), BF16),
                   jax.ShapeDtypeStruct((Tk, 256), BF16)],
        compiler_params=_cp("arbitrary"),
        name="mla_kv",
    )(ckv, kpe, wk, wv)


def _attn_core(q, kvs, masks, sink):
    ss = []
    for (k, _), mk in zip(kvs, masks):
        s = _dot_nt(q, k)
        if mk is not None:
            s = s + mk[1] if mk[0] == "add" else jnp.where(mk[1], s, NEG)
        ss.append(s)
    m = ss[0].max(-1, keepdims=True)
    for s in ss[1:]:
        m = jnp.maximum(m, s.max(-1, keepdims=True))
    if sink is not None:
        m = jnp.maximum(m, sink)
    den = None
    acc = None
    for s, (_, v) in zip(ss, kvs):
        p = jnp.exp(s - m)
        d = p.sum(-1, keepdims=True)
        a = _dot(p.astype(BF16), v)
        den = d if den is None else den + d
        acc = a if acc is None else acc + a
    if sink is not None:
        den = den + jnp.exp(sink - m)
    return acc / den


def _ctx_attn_kernel(qm_ref, km_ref, vm_ref, wq_ref, wk_ref, wv_ref, nq_ref, nk_ref, nv_ref, sink_ref, *rest):
    om_ref, ow_ref, on_ref = rest[-3:]
    for h in range(4):
        q = qm_ref[:, 128 * h:128 * (h + 1)]
        k = km_ref[:, 128 * h:128 * (h + 1)]
        v = vm_ref[:, 64 * h:64 * (h + 1)]
        om_ref[:, 64 * h:64 * (h + 1)] = _attn_core(q, [(k, v)], [None], None)
    for h in range(4):
        g = h // 2
        q = (wq_ref[:, 64 * h:64 * (h + 1)] * ATT_SCALE).astype(BF16)
        k = wk_ref[:, 64 * g:64 * (g + 1)].astype(BF16)
        v = wv_ref[:, 64 * g:64 * (g + 1)].astype(BF16)
        ow_ref[:, 64 * h:64 * (h + 1)] = _attn_core(q, [(k, v)], [None], sink_ref[h])
    for h in range(4):
        q = (nq_ref[:, 64 * h:64 * (h + 1)] * ATT_SCALE).astype(BF16)
        k = nk_ref[:, 64 * h:64 * (h + 1)].astype(BF16)
        v = nv_ref[:, 64 * h:64 * (h + 1)].astype(BF16)
        on_ref[:, 64 * h:64 * (h + 1)] = _attn_core(q, [(k, v)], [None], None)


def _ctx_attention(proj, q_all, k_all, v_all, sink, NB, Lb, after=None):
    T = proj.shape[0]
    pc = lambda w, off: pl.BlockSpec((Lb, w), lambda b: (b, off // w))
    row = lambda w: pl.BlockSpec((Lb, w), lambda b: (b, 0))
    dep_specs, dep_args = _after(after)
    return pl.pallas_call(
        _ctx_attn_kernel,
        grid=(NB,),
        in_specs=[row(512), row(512), row(256),
                  pc(256, P_WQ), pc(128, P_WK), pc(128, P_WV),
                  pc(256, P_NQ), pc(256, P_NK), pc(256, P_NV),
                  pl.BlockSpec(memory_space=pltpu.SMEM)] + dep_specs,
        out_specs=[row(256), row(256), row(256)],
        out_shape=[jax.ShapeDtypeStruct((T, 256), F32)] * 3,
        compiler_params=_cp("arbitrary"),
        name="ctx_attention",
    )(q_all, k_all, v_all, proj, proj, proj, proj, proj, proj, sink, *dep_args)


def _lat_mla_kernel(q_ref, k_ref, v_ref, *rest):
    o_ref = rest[-1]
    for h in range(4):
        q = q_ref[:, 128 * h:128 * (h + 1)]
        k = k_ref[:, 128 * h:128 * (h + 1)]
        v = v_ref[:, 64 * h:64 * (h + 1)]
        o_ref[:, 64 * h:64 * (h + 1)] = _attn_core(q, [(k, v)], [None], None)


def _after(after):
    deps = [] if after is None else list(after) if isinstance(after, (tuple, list)) else [after]
    return [pl.BlockSpec(memory_space=pl.ANY)] * len(deps), deps


def _lat_mla_attention(q_all, k_all, v_all, NB, Lb, Lk, tq, after=None):
    T = q_all.shape[0]
    nq = Lb // tq
    dep_specs, dep_args = _after(after)
    return pl.pallas_call(
        _lat_mla_kernel,
        grid=(NB, nq),
        in_specs=[pl.BlockSpec((tq, 512), lambda b, i: (b * nq + i, 0)),
                  pl.BlockSpec((Lk, 512), lambda b, i: (b, 0)),
                  pl.BlockSpec((Lk, 256), lambda b, i: (b, 0))] + dep_specs,
        out_specs=pl.BlockSpec((tq, 256), lambda b, i: (b * nq + i, 0)),
        out_shape=jax.ShapeDtypeStruct((T, 256), F32),
        compiler_params=_cp("arbitrary", "arbitrary"),
        name="lat_mla_attention",
    )(q_all, k_all, v_all, *dep_args)


def _attn_local_ctx(q, locs, kc, vc, sink):
    s_ctx = _dot_nt(q, kc)
    m_ctx = s_ctx.max(-1, keepdims=True)
    if sink is not None:
        m_ctx = jnp.maximum(m_ctx, sink)
    ms, dens, accs = [], [], []
    for rs, k, v, mk in locs:
        s = _dot_nt(q[rs], k)
        s = s + mk[1] if mk[0] == "add" else jnp.where(mk[1], s, NEG)
        m = jnp.maximum(s.max(-1, keepdims=True), m_ctx[rs])
        p = jnp.exp(s - m)
        ms.append(m)
        dens.append(p.sum(-1, keepdims=True))
        accs.append(_dot(p.astype(BF16), v))
    m = jnp.concatenate(ms, axis=0)
    p = jnp.exp(s_ctx - m)
    den = jnp.concatenate(dens, axis=0) + p.sum(-1, keepdims=True)
    if sink is not None:
        den = den + jnp.exp(sink - m)
    return (jnp.concatenate(accs, axis=0) + _dot(p.astype(BF16), vc)) / den


def _lat_win_kernel(q_ref, qr_ref, k_ref, kr_ref, v_ref, kc_ref, vc_ref, cq_ref, sq_ref, ck_ref, sk_ref,
                    sink_ref, o_ref, *, Lb, bpt):
    t = pl.program_id(1)
    q = (q_ref[...] * cq_ref[...] + qr_ref[...] * sq_ref[...]) * ATT_SCALE
    kc = kc_ref[...].astype(BF16)
    vc = vc_ref[...].astype(BF16)
    blocks = []
    for bb in range(bpt):
        i = t * bpt + bb
        start = pl.multiple_of(jnp.clip((i - 1) * 128, 0, Lb - 384), 128)
        win = pl.ds(start, 384)
        kk = (k_ref[win, :] * ck_ref[win, :] + kr_ref[win, :] * sk_ref[win, :]).astype(BF16)
        qpos = i * 128 + lax.broadcasted_iota(jnp.int32, (128, 384), 0)
        kpos = start + lax.broadcasted_iota(jnp.int32, (128, 384), 1)
        blocks.append((kk, v_ref[win, :].astype(BF16), jnp.abs(qpos - kpos) <= 128))
    for h in range(4):
        g = h // 2
        sl = slice(64 * g, 64 * (g + 1))
        locs = [(slice(128 * bb, 128 * (bb + 1)), kk[:, sl], vv[:, sl], ("keep", valid))
                for bb, (kk, vv, valid) in enumerate(blocks)]
        qh = q[:, 64 * h:64 * (h + 1)].astype(BF16)
        o_ref[:, 64 * h:64 * (h + 1)] = _attn_local_ctx(qh, locs, kc[:, sl], vc[:, sl], sink_ref[h])


def _lat_win_attention(proj, kc, vc, tabs, sink, NB, Lb):
    T = proj.shape[0]
    bpt = 1
    tq = 128 * bpt
    nt = Lb // tq
    Lc = kc.shape[1]
    cq, sq, ck, sk = tabs
    qspec = lambda off: pl.BlockSpec((tq, 256), lambda b, i: (b * nt + i, off // 256))
    kspec = lambda off: pl.BlockSpec((Lb, 128), lambda b, i: (b, off // 128))
    cspec = pl.BlockSpec((None, Lc, 128), lambda b, i: (b, 0, 0))
    return pl.pallas_call(
        functools.partial(_lat_win_kernel, Lb=Lb, bpt=bpt),
        grid=(NB, nt),
        in_specs=[qspec(P_WQ), qspec(P_WQR), kspec(P_WK), kspec(P_WKR), kspec(P_WV), cspec, cspec,
                  pl.BlockSpec((tq, 256), lambda b, i: (i, 0)), pl.BlockSpec((tq, 256), lambda b, i: (i, 0)),
                  pl.BlockSpec((Lb, 128), lambda b, i: (0, 0)), pl.BlockSpec((Lb, 128), lambda b, i: (0, 0)),
                  pl.BlockSpec(memory_space=pltpu.SMEM)],
        out_specs=pl.BlockSpec((tq, 256), lambda b, i: (b * nt + i, 0)),
        out_shape=jax.ShapeDtypeStruct((T, 256), F32),
        compiler_params=_cp("arbitrary", "arbitrary"),
        name="lat_win_attention",
    )(proj, proj, proj, proj, proj, kc, vc, cq, sq, ck, sk, sink)


def _na_bias_kernel(rpb_ref, o_ref):
    h = pl.program_id(0)
    qc = lax.broadcasted_iota(jnp.int32, (GRID_W, GRID_W), 0)
    kc = lax.broadcasted_iota(jnp.int32, (GRID_W, GRID_W), 1)
    dc = kc - qc + (NA_KW - 1)
    wstart = jnp.clip(qc - NA_KW // 2, 0, GRID_W - NA_KW)
    ok = (kc >= wstart) & (kc < wstart + NA_KW)
    n_dc = 2 * NA_KW - 1
    n_dr = 2 * NA_KH - 1
    tabs = []
    for dr in range(n_dr):
        t = jnp.zeros((GRID_W, GRID_W), F32)
        for j in range(n_dc):
            t = jnp.where(dc == j, rpb_ref[(h * n_dr + dr) * n_dc + j], t)
        tabs.append(jnp.where(ok, t, NEG))
    for o in range(NA_KH):
        for a in range(NA_KH):
            o_ref[o, :, GRID_W * a:GRID_W * (a + 1)] = tabs[a + NA_KH - 1 - o]


def _na_bias(rpb):
    H = rpb.shape[0]
    return pl.pallas_call(
        _na_bias_kernel,
        grid=(H,),
        in_specs=[pl.BlockSpec(memory_space=pltpu.SMEM)],
        out_specs=pl.BlockSpec((None, NA_KH, GRID_W, NA_KH * GRID_W), lambda h: (h, 0, 0, 0)),
        out_shape=jax.ShapeDtypeStruct((H, NA_KH, GRID_W, NA_KH * GRID_W), F32),
        compiler_params=_cp("arbitrary"),
        name="na_bias",
    )(rpb.reshape(-1))


def _lat_na_kernel(q_ref, k_ref, v_ref, kc_ref, vc_ref, bias_ref, o_ref, *, rows, rpt):
    t = pl.program_id(1)
    q = q_ref[...] * ATT_SCALE
    kc = kc_ref[...].astype(BF16)
    vc = vc_ref[...].astype(BF16)
    bands = []
    for rr in range(rpt):
        r = t * rpt + rr
        first = jnp.clip(r - NA_KH // 2, 0, rows - NA_KH)
        win = pl.ds(pl.multiple_of(first * GRID_W, GRID_W), NA_KH * GRID_W)
        bands.append((k_ref[win, :].astype(BF16), v_ref[win, :].astype(BF16), r - first))
    for h in range(4):
        sl = slice(64 * h, 64 * (h + 1))
        locs = [(slice(GRID_W * rr, GRID_W * (rr + 1)), kk[:, sl], vv[:, sl], ("add", bias_ref[h, off]))
                for rr, (kk, vv, off) in enumerate(bands)]
        o_ref[:, sl] = _attn_local_ctx(q[:, sl].astype(BF16), locs, kc[:, sl], vc[:, sl], None)


def _lat_na_attention(proj, kc, vc, bias, NB, Lb):
    T = proj.shape[0]
    rows = Lb // GRID_W
    rpt = 8
    tq = GRID_W * rpt
    nt = rows // rpt
    Lc = kc.shape[1]
    kspec = lambda off: pl.BlockSpec((Lb, 256), lambda b, t: (b, off // 256))
    cspec = pl.BlockSpec((None, Lc, 256), lambda b, t: (b, 0, 0))
    return pl.pallas_call(
        functools.partial(_lat_na_kernel, rows=rows, rpt=rpt),
        grid=(NB, nt),
        in_specs=[pl.BlockSpec((tq, 256), lambda b, t: (b * nt + t, P_NQ // 256)),
                  kspec(P_NK), kspec(P_NV), cspec, cspec,
                  pl.BlockSpec((4, NA_KH, GRID_W, NA_KH * GRID_W), lambda b, t: (0, 0, 0, 0))],
        out_specs=pl.BlockSpec((tq, 256), lambda b, t: (b * nt + t, 0)),
        out_shape=jax.ShapeDtypeStruct((T, 256), F32),
        compiler_params=_cp("arbitrary", "arbitrary"),
        name="lat_na_attention",
    )(proj, proj, proj, kc, vc, bias)


def _short_conv_kernel(a_ref, b_ref, c_ref, w_ref, bias_ref, oa_ref, ob_ref, oc_ref, *, L):
    t = lax.broadcasted_iota(jnp.int32, (L, HY_C), 0)
    for n, (x_ref, o_ref) in enumerate(((a_ref, oa_ref), (b_ref, ob_ref), (c_ref, oc_ref))):
        sl = slice(HY_C * n, HY_C * (n + 1))
        x = x_ref[...]
        prev = jnp.where(t == 0, 0.0, pltpu.roll(x, 1, axis=0))
        nxt = jnp.where(t == L - 1, 0.0, pltpu.roll(x, L - 1, axis=0))
        o_ref[...] = prev * w_ref[0:1, sl] + x * w_ref[1:2, sl] + nxt * w_ref[2:3, sl] + bias_ref[:, sl]


def _short_conv(proj, w, b, NB, Lb):
    T = proj.shape[0]
    spec = lambda c: pl.BlockSpec((Lb, HY_C), lambda i: (i, c))
    return pl.pallas_call(
        functools.partial(_short_conv_kernel, L=Lb),
        grid=(NB,),
        in_specs=[spec(P_HY // HY_C), spec(P_HY // HY_C + 1), spec(P_HY // HY_C + 2),
                  pl.BlockSpec((3, 3 * HY_C), lambda i: (0, 0)), pl.BlockSpec((1, 3 * HY_C), lambda i: (0, 0))],
        out_specs=[spec(0)] * 3,
        out_shape=[jax.ShapeDtypeStruct((T, HY_C), F32)] * 3,
        compiler_params=_cp("arbitrary"),
        name="hyena_short_conv",
    )(proj, proj, proj, w, b)


def _hy_filter_kernel(w1_ref, b1_ref, w2_ref, b2_ref, w3_ref, freq_ref, ld_ref, fs_ref, nyq_ref, *, L):
    ti = lax.broadcasted_iota(jnp.int32, (L, 128), 0)
    t = ti.astype(F32)
    j = lax.broadcasted_iota(jnp.int32, (L, 128), 1)
    band = jnp.where(j <= HY_BANDS, j - 1, j - 1 - HY_BANDS).astype(F32)
    ang = (2.0 * math.pi / L) * t * band
    tn = t / L
    z = jnp.where(j == 0, tn, jnp.where(j <= HY_BANDS, jnp.cos(ang),
                                        jnp.where(j <= 2 * HY_BANDS, -jnp.sin(ang), 0.0)))
    a = jnp.sin(freq_ref[0:1, :] * (_dot_hi(z, w1_ref[...]) + b1_ref[...]))
    a = jnp.sin(freq_ref[1:2, :] * (_dot_hi(a, w2_ref[...]) + b2_ref[...]))
    filt = _dot_hi(a, w3_ref[...])
    tcol = lax.broadcasted_iota(jnp.int32, (L, 4 * HY_C), 0)
    filt = filt * jnp.exp(-(tcol.astype(F32) / L) * jnp.exp(ld_ref[...]))
    t1 = lax.broadcasted_iota(jnp.int32, (L, HY_C), 0)
    sign = jnp.where(t1 % 2 == 0, 1.0, -1.0)
    for n in range(2):
        fwd = filt[:, 2 * HY_C * n:2 * HY_C * n + HY_C]
        bwd = jnp.where(t1 == 0, 0.0, filt[:, 2 * HY_C * n + HY_C:2 * HY_C * (n + 1)])
        tot = fwd + bwd
        fs_ref[:, HY_C * n:HY_C * (n + 1)] = tot
        fs_ref[:, 2 * HY_C + HY_C * n:2 * HY_C + HY_C * (n + 1)] = fwd - bwd
        nyq_ref[:, HY_C * n:HY_C * (n + 1)] = (tot * sign).sum(0, keepdims=True)


def _hy_filter(L, w1p, b1, w2, b2, w3, freq, ld):
    full = lambda s: pl.BlockSpec(s, lambda: tuple(0 for _ in s))
    return pl.pallas_call(
        functools.partial(_hy_filter_kernel, L=L),
        in_specs=[full((128, 64)), full((1, 64)), full((64, 64)), full((1, 64)), full((64, 4 * HY_C)),
                  full((2, 64)), full((1, 4 * HY_C))],
        out_specs=[full((L, 4 * HY_C)), full((1, 2 * HY_C))],
        out_shape=[jax.ShapeDtypeStruct((L, 4 * HY_C), F32), jax.ShapeDtypeStruct((1, 2 * HY_C), F32)],
        compiler_params=pltpu.CompilerParams(vmem_limit_bytes=VMEM_LIMIT),
        name="hyena_filter",
    )(w1p, b1, w2, b2, w3, freq, ld)


def _hy_gdft_kernel(cm_ref, sm_ref, fs_ref, nyq_ref, gr_ref, gi_ref, *, tm):
    m = pl.program_id(0)
    f = fs_ref[...].astype(BF16)
    gr_ref[...] = _dot(cm_ref[...], f[:, :2 * HY_C])
    gi = _dot(sm_ref[...], f[:, 2 * HY_C:])
    row = m * tm + lax.broadcasted_iota(jnp.int32, (tm, 2 * HY_C), 0)
    gi_ref[...] = jnp.where(row == 0, nyq_ref[...], gi)


def _hy_gdft(cm, sm, fs, nyq, L, tm):
    return pl.pallas_call(
        functools.partial(_hy_gdft_kernel, tm=tm),
        grid=(L // tm,),
        in_specs=[pl.BlockSpec((tm, L), lambda m: (m, 0)), pl.BlockSpec((tm, L), lambda m: (m, 0)),
                  pl.BlockSpec((L, 4 * HY_C), lambda m: (0, 0)), pl.BlockSpec((1, 2 * HY_C), lambda m: (0, 0))],
        out_specs=[pl.BlockSpec((tm, 2 * HY_C), lambda m: (m, 0))] * 2,
        out_shape=[jax.ShapeDtypeStruct((L, 2 * HY_C), F32)] * 2,
        compiler_params=_cp("arbitrary"),
        name="hyena_filter_dft",
    )(cm, sm, fs, nyq)


def _hy_fwd_kernel(cm_ref, sm_ref, z_ref, gr_ref, gi_ref, yr_ref, yi_ref, *, L, tm, ns):
    m = pl.program_id(1)
    gr = gr_ref[...]
    gi = gi_ref[...]
    row0 = (m * tm + lax.broadcasted_iota(jnp.int32, (tm, HY_C), 0)) == 0
    s = jnp.where(row0, 0.5 / L, 1.0 / L)
    for g in range(ns):
        zb = z_ref[g * L:(g + 1) * L, :].astype(BF16)
        zr = _dot(cm_ref[...], zb)
        zi = _dot(sm_ref[...], zb)
        zigi = zi * gi
        yr_ref[g * tm:(g + 1) * tm, :] = ((zr * gr - jnp.where(row0, 0.0, zigi)) * s).astype(BF16)
        yi_ref[g * tm:(g + 1) * tm, :] = (jnp.where(row0, zigi, zr * gi + zi * gr) * s).astype(BF16)


def _hy_seqs_per_step(NB, Lb, tm):
    ns = max(1, 2048 // Lb) if tm == Lb else 1
    while NB % ns:
        ns //= 2
    return ns


def _hy_fwd(cm, sm, z, gr, gi, n, NB, Lb, tm):
    T = z.shape[0]
    nm = Lb // tm
    ns = _hy_seqs_per_step(NB, Lb, tm)
    return pl.pallas_call(
        functools.partial(_hy_fwd_kernel, L=Lb, tm=tm, ns=ns),
        grid=(NB // ns, nm),
        in_specs=[pl.BlockSpec((tm, Lb), lambda b, m: (m, 0)), pl.BlockSpec((tm, Lb), lambda b, m: (m, 0)),
                  pl.BlockSpec((ns * Lb, HY_C), lambda b, m: (b, 0)),
                  pl.BlockSpec((tm, HY_C), lambda b, m: (m, n)), pl.BlockSpec((tm, HY_C), lambda b, m: (m, n))],
        out_specs=[pl.BlockSpec((ns * tm, HY_C), lambda b, m: (b * nm + m, 0))] * 2,
        out_shape=[jax.ShapeDtypeStruct((T, HY_C), BF16)] * 2,
        compiler_params=_cp("arbitrary", "arbitrary"),
        name="hyena_fwd_dft",
    )(cm, sm, z, gr, gi)


def _hy_inv_kernel(cm_ref, smt_ref, yr_ref, yi_ref, z_ref, g_ref, skip_ref, o_ref, *, L, tm, ns):
    for g in range(ns):
        seq = slice(g * L, (g + 1) * L)
        out = slice(g * tm, (g + 1) * tm)
        conv = _dot(cm_ref[...], yr_ref[seq, :]) + _dot(smt_ref[...], yi_ref[seq, :])
        o_ref[out, :] = g_ref[out, :] * (conv + skip_ref[...] * z_ref[out, :])


def _hy_inv(cm, smt, yr, yi, z, gate, skip, n, NB, Lb, tm):
    T = z.shape[0]
    nm = Lb // tm
    ns = _hy_seqs_per_step(NB, Lb, tm)
    tile = pl.BlockSpec((ns * tm, HY_C), lambda b, m: (b * nm + m, 0))
    seq = pl.BlockSpec((ns * Lb, HY_C), lambda b, m: (b, 0))
    return pl.pallas_call(
        functools.partial(_hy_inv_kernel, L=Lb, tm=tm, ns=ns),
        grid=(NB // ns, nm),
        in_specs=[pl.BlockSpec((tm, Lb), lambda b, m: (m, 0)), pl.BlockSpec((tm, Lb), lambda b, m: (m, 0)),
                  seq, seq, tile, tile, pl.BlockSpec((None, 1, HY_C), lambda b, m: (n, 0, 0))],
        out_specs=tile,
        out_shape=jax.ShapeDtypeStruct((T, HY_C), F32),
        compiler_params=_cp("arbitrary", "arbitrary"),
        name="hyena_inv_dft",
    )(cm, smt, yr, yi, z, gate, skip)


def _dft_mats(L):
    k = jnp.arange(L, dtype=jnp.int32)
    blk = 64

    def trig(mult):
        ang = ((mult[:, None] * k[None, :]) % (2 * L)).astype(F32) * (math.pi / L)
        return jnp.cos(ang), jnp.sin(ang)

    ca, sa = trig(jnp.arange(L // blk, dtype=jnp.int32) * blk)
    cb, sb = trig(jnp.arange(blk, dtype=jnp.int32))
    cm = (ca[:, None, :] * cb[None] - sa[:, None, :] * sb[None]).reshape(L, L)
    s = -(sa[:, None, :] * cb[None] + ca[:, None, :] * sb[None]).reshape(L, L)
    alt = jnp.where(k % 2 == 0, 1.0, -1.0).astype(F32)
    sm = jnp.where(k[:, None] == 0, alt[None, :], s)
    smt = jnp.where(k[None, :] == 0, alt[:, None], s)
    return cm.astype(BF16), sm.astype(BF16), smt.astype(BF16)


def _merge_kernel(oa_ref, ob_ref, oc_ref, od_ref, g0_ref, g1_ref, g2_ref, g3_ref, wb_ref, wo_ref, x_ref, mod_ref,
                  lg_ref, lb_ref, *rest):
    x1_ref, h2_ref, hp_ref = rest[-3:]
    acc = None
    for o_ref, g_ref, i in ((oa_ref, g0_ref, 0), (ob_ref, g1_ref, 1), (oc_ref, g2_ref, 2), (od_ref, g3_ref, 3)):
        y = _sigmoid(g_ref[...].astype(F32)) * _dot(o_ref[...].astype(BF16), wb_ref[i])
        acc = y if acc is None else acc + y
    mix = _dot(acc.astype(BF16), wo_ref[...])
    m = mod_ref[...]
    x1 = _layer_norm(DN_ALPHA * x_ref[...] + m[2:3, :] * mix, lg_ref[...], lb_ref[...])
    x1_ref[...] = x1
    h2 = x1 * (1.0 + m[4:5, :]) + m[3:4, :]
    h2_ref[...] = h2
    hp_ref[...] = _pack_pairs(h2)


def _merge(outs, gates, wb, wo, x, mod, lg, lb, mod_row, bm, after=None):
    T = x.shape[0]
    row = lambda w: pl.BlockSpec((bm, w), lambda i: (i, 0))
    gspec = lambda n: pl.BlockSpec((bm, D), lambda i: (i, n))
    fixed2 = lambda s: pl.BlockSpec(s, lambda i: (0, 0))
    dep_specs, dep_args = _after(after)
    return pl.pallas_call(
        _merge_kernel,
        grid=(T // bm,),
        in_specs=[row(256)] * 4 + [gspec(0), gspec(1), gspec(2), gspec(3),
                                   pl.BlockSpec((4, 256, D), lambda i: (0, 0, 0)), fixed2((D, D)), row(D),
                                   pl.BlockSpec((None, 6, D), lambda i: (mod_row(i), 0, 0)),
                                   fixed2((1, D)), fixed2((1, D))] + dep_specs,
        out_specs=[row(D), row(D), row(D // 2)],
        out_shape=[jax.ShapeDtypeStruct((T, D), F32)] * 2 + [jax.ShapeDtypeStruct((T, D // 2), jnp.int32)],
        compiler_params=_cp("arbitrary"),
        name="merge_norm",
    )(*outs, gates, gates, gates, gates, wb, wo, x, mod, lg, lb, *dep_args)


def _router_kernel(h_ref, rt_ref, bias_ref, g_ref, rank_ref, cnt_ref, *, tt):
    per = N_EXPERTS // N_GROUPS
    logits = lax.dot_general(rt_ref[...], h_ref[...], (((1,), (1,)), ((), ())), preferred_element_type=F32,
                             precision=lax.Precision.HIGHEST)
    scores = _sigmoid(logits)
    sel = (scores + bias_ref[...]).reshape(N_GROUPS, per, tt)
    gid = lax.broadcasted_iota(jnp.int32, (N_GROUPS, per, tt), 0).astype(F32)
    jid = lax.broadcasted_iota(jnp.int32, (N_GROUPS, per, tt), 1).astype(F32)
    eid = gid * per + jid
    ninf = -jnp.inf
    m1 = sel.max(1, keepdims=True)
    i1 = jnp.where(sel == m1, jid, float(per)).min(1, keepdims=True)
    m2 = jnp.where(jid == i1, ninf, sel).max(1, keepdims=True)
    gs = m1 + m2
    g1 = lax.broadcasted_iota(jnp.int32, (N_GROUPS, 1, tt), 0).astype(F32)
    chosen = jnp.zeros((N_GROUPS, 1, tt), F32)
    for _ in range(TOPK_GROUPS):
        mx = gs.max(0, keepdims=True)
        gi = jnp.where(gs == mx, g1, float(N_GROUPS)).min(0, keepdims=True)
        pick = g1 == gi
        chosen = jnp.where(pick, 1.0, chosen)
        gs = jnp.where(pick, ninf, gs)
    cand = jnp.where(chosen > 0.0, sel, NEG)
    picked = jnp.zeros((N_GROUPS, per, tt), F32)
    for _ in range(TOP_K):
        mx = cand.max(1, keepdims=True).max(0, keepdims=True)
        ei = jnp.where(cand == mx, eid, float(N_EXPERTS)).min(1, keepdims=True).min(0, keepdims=True)
        pick = eid == ei
        picked = jnp.where(pick, 1.0, picked)
        cand = jnp.where(pick, ninf, cand)
    w = scores.reshape(N_GROUPS, per, tt) * picked
    wsum = w.sum(1, keepdims=True).sum(0, keepdims=True)
    g_ref[...] = (w / wsum * ROUTED_SCALE).reshape(N_EXPERTS, tt)
    pk = picked.reshape(N_EXPERTS, tt)
    t_in = lax.broadcasted_iota(jnp.int32, (tt, tt), 0)
    t_out = lax.broadcasted_iota(jnp.int32, (tt, tt), 1)
    upper = jnp.where(t_in <= t_out, 1.0, 0.0).astype(BF16)

    @pl.when(pl.program_id(0) == 0)
    def _():
        cnt_ref[...] = jnp.zeros_like(cnt_ref)

    before = cnt_ref[:, 0:1]
    rank_ref[...] = jnp.where(pk > 0.0, before + _dot(pk.astype(BF16), upper) - 1.0, -1.0)
    cnt_ref[...] += pk.sum(-1, keepdims=True)


def _router(h2, router_t, bias, tt):
    T = h2.shape[0]
    tile = pl.BlockSpec((N_EXPERTS, tt), lambda i: (0, i))
    return pl.pallas_call(
        functools.partial(_router_kernel, tt=tt),
        grid=(T // tt,),
        in_specs=[pl.BlockSpec((tt, D), lambda i: (i, 0)), pl.BlockSpec((N_EXPERTS, D), lambda i: (0, 0)),
                  pl.BlockSpec((N_EXPERTS, 1), lambda i: (0, 0))],
        out_specs=[tile, tile, pl.BlockSpec((N_EXPERTS, 128), lambda i: (0, 0))],
        out_shape=[jax.ShapeDtypeStruct((N_EXPERTS, T), F32), jax.ShapeDtypeStruct((N_EXPERTS, T), F32),
                   jax.ShapeDtypeStruct((N_EXPERTS, 128), F32)],
        compiler_params=_cp("arbitrary"),
        name="moe_router",
    )(h2, router_t, bias)


def _route_pos_kernel(gate_ref, rank_ref, cnt_ref, pos_ref, w_ref, te_ref, nx_ref, nt_ref, *, tm, nt_max):
    ei = lax.broadcasted_iota(jnp.int32, (N_EXPERTS, N_EXPERTS), 0)
    ej = lax.broadcasted_iota(jnp.int32, (N_EXPERTS, N_EXPERTS), 1)
    below = jnp.where(ej < ei, 1.0, 0.0)
    padded = jnp.ceil(cnt_ref[...] * (1.0 / tm)) * tm
    offs = _dot_hi(below, padded)
    rank = rank_ref[...]
    routed = rank >= 0.0
    pos = offs[:, 0:1] + rank
    slot = _dot(below.astype(BF16), jnp.where(routed, 1.0, 0.0).astype(BF16))
    gate = gate_ref[...]
    for k in range(TOP_K):
        mine = routed & (slot == float(k))
        pos_ref[k:k + 1, :] = jnp.where(mine, pos, 0.0).sum(0, keepdims=True).astype(jnp.int32)
        w_ref[k:k + 1, :] = jnp.where(mine, gate, 0.0).sum(0, keepdims=True)
    ends = (offs + padded)[:, 0:1]
    first = (lax.broadcasted_iota(jnp.int32, (N_EXPERTS, nt_max), 1) * tm).astype(F32)
    te = jnp.minimum(jnp.where(ends <= first, 1.0, 0.0).sum(0, keepdims=True), N_EXPERTS - 1.0)
    te_ref[...] = te.astype(jnp.int32)
    eid = lax.broadcasted_iota(jnp.int32, (N_EXPERTS, nt_max), 0).astype(F32)
    nx_ref[...] = (jnp.where(eid == te, ends, 0.0).sum(0, keepdims=True) * (1.0 / tm)).astype(jnp.int32)
    nt_ref[...] = (padded.sum(0, keepdims=True) * (1.0 / tm)).astype(jnp.int32)


def _route_pos(gate_t, rank, cnt, tt, tm, nt_max):
    T = gate_t.shape[1]
    tile = pl.BlockSpec((N_EXPERTS, tt), lambda i: (0, i))
    out = pl.BlockSpec((TOP_K, tt), lambda i: (0, i))
    return pl.pallas_call(
        functools.partial(_route_pos_kernel, tm=tm, nt_max=nt_max),
        grid=(T // tt,),
        in_specs=[tile, tile, pl.BlockSpec((N_EXPERTS, 128), lambda i: (0, 0))],
        out_specs=[out, out, pl.BlockSpec((1, nt_max), lambda i: (0, 0)), pl.BlockSpec((1, nt_max), lambda i: (0, 0)),
                   pl.BlockSpec((1, 128), lambda i: (0, 0))],
        out_shape=[jax.ShapeDtypeStruct((TOP_K, T), jnp.int32), jax.ShapeDtypeStruct((TOP_K, T), F32),
                   jax.ShapeDtypeStruct((1, nt_max), jnp.int32), jax.ShapeDtypeStruct((1, nt_max), jnp.int32),
                   jax.ShapeDtypeStruct((1, 128), jnp.int32)],
        compiler_params=_cp("arbitrary"),
        name="moe_positions",
    )(gate_t, rank, cnt)


def _gmm_kernel(te_ref, nx_ref, nt_ref, xs_ref, w1_hbm, w3_hbm, w2_hbm, *rest, l):
    ys_ref, b1_ref, b3_ref, b2_ref, f1_ref, f3_ref, f2_ref, seg_ref, sem = rest[-9:]
    j = pl.program_id(0)
    live = j < nt_ref[0]
    new_expert = (j == 0) | (te_ref[j] != te_ref[jnp.maximum(j - 1, 0)])

    def fetch(e, slot):
        return [pltpu.make_async_copy(w_hbm.at[l, e], f_ref.at[slot], sem.at[i, slot])
                for i, (w_hbm, f_ref) in enumerate(((w1_hbm, f1_ref), (w3_hbm, f3_ref), (w2_hbm, f2_ref)))]

    @pl.when(live & new_expert)
    def _():
        @pl.when(j == 0)
        def _():
            seg_ref[0] = 0
            for c in fetch(te_ref[0], 0):
                c.start()

        slot = lax.rem(seg_ref[0], 2)
        for c in fetch(te_ref[j], slot):
            c.wait()
        b1_ref[...] = f1_ref[slot].astype(BF16)
        b3_ref[...] = f3_ref[slot].astype(BF16)
        b2_ref[...] = f2_ref[slot].astype(BF16)
        nxt = nx_ref[j]

        @pl.when(nxt < nt_ref[0])
        def _():
            for c in fetch(te_ref[nxt], 1 - slot):
                c.start()

        seg_ref[0] = seg_ref[0] + 1

    @pl.when(live)
    def _():
        xa, xb = _unpack_pairs(xs_ref[...])
        xa, xb = xa.astype(BF16), xb.astype(BF16)
        half = D // 2
        a = _dot(xa, b1_ref[:half, :]) + _dot(xb, b1_ref[half:, :])
        b = _dot(xa, b3_ref[:half, :]) + _dot(xb, b3_ref[half:, :])
        hid = (a * _sigmoid(a) * b).astype(BF16)
        ys_ref[...] = _pack_pairs(_dot(hid, b2_ref[...]))


def _gmm(te, nx, nt, xs, w1, w3, w2, l, tm, after=None):
    n_slots = xs.shape[0]
    ds = D_EXPERT
    rows = pl.BlockSpec((tm, D // 2), lambda j, te, nx, nt: (jnp.minimum(j, nt[0] - 1), 0))
    hbm = pl.BlockSpec(memory_space=pl.ANY)
    dep_specs, dep_args = _after(after)
    return pl.pallas_call(
        functools.partial(_gmm_kernel, l=l),
        grid_spec=pltpu.PrefetchScalarGridSpec(
            num_scalar_prefetch=3,
            grid=(n_slots // tm,),
            in_specs=[rows, hbm, hbm, hbm] + dep_specs,
            out_specs=rows,
            scratch_shapes=[pltpu.VMEM((D, ds), BF16), pltpu.VMEM((D, ds), BF16), pltpu.VMEM((ds, D), BF16),
                            pltpu.VMEM((2, D, ds), F32), pltpu.VMEM((2, D, ds), F32), pltpu.VMEM((2, ds, D), F32),
                            pltpu.SMEM((1,), jnp.int32), pltpu.SemaphoreType.DMA((3, 2))]),
        out_shape=jax.ShapeDtypeStruct((n_slots, D // 2), jnp.int32),
        compiler_params=_cp("arbitrary"),
        name="moe_grouped_ffn",
    )(te, nx, nt, xs, w1, w3, w2, *dep_args)


def _combine_kernel(yk_ref, w_ref, hp_ref, s1_ref, s3_ref, s2_ref, x_ref, mod_ref, lg_ref, lb_ref, o_ref):
    w = w_ref[...]
    acc_a = acc_b = None
    for k in range(TOP_K):
        ya, yb = _unpack_pairs(yk_ref[k])
        wk = w[:, k:k + 1]
        acc_a = wk * ya if acc_a is None else acc_a + wk * ya
        acc_b = wk * yb if acc_b is None else acc_b + wk * yb
    ha, hb = _unpack_pairs(hp_ref[...])
    ha, hb = ha.astype(BF16), hb.astype(BF16)
    half = D // 2
    a = _dot(ha, s1_ref[:half, :]) + _dot(hb, s1_ref[half:, :])
    b = _dot(ha, s3_ref[:half, :]) + _dot(hb, s3_ref[half:, :])
    y = jnp.concatenate([acc_a, acc_b], axis=1) + _dot((a * _sigmoid(a) * b).astype(BF16), s2_ref[...])
    m = mod_ref[...]
    o_ref[...] = _layer_norm(DN_ALPHA * x_ref[...] + m[5:6, :] * y, lg_ref[...], lb_ref[...])


def _combine(yk, w, hp, s1, s3, s2, x1, mod, lg, lb, mod_row, bm):
    T = x1.shape[0]
    ds = D_EXPERT
    row = lambda n: pl.BlockSpec((bm, n), lambda i: (i, 0))
    fixed = lambda s: pl.BlockSpec(s, lambda i: (0, 0))
    return pl.pallas_call(
        _combine_kernel,
        grid=(T // bm,),
        in_specs=[pl.BlockSpec((TOP_K, bm, D // 2), lambda i: (0, i, 0)), row(TOP_K), row(D // 2),
                  fixed((D, ds)), fixed((D, ds)), fixed((ds, D)), row(D),
                  pl.BlockSpec((None, 6, D), lambda i: (mod_row(i), 0, 0)), fixed((1, D)), fixed((1, D))],
        out_specs=row(D),
        out_shape=jax.ShapeDtypeStruct((T, D), F32),
        compiler_params=_cp("arbitrary"),
        name="moe_combine_norm",
    )(yk, w, hp, s1, s3, s2, x1, mod, lg, lb)


def _sc_worker():
    return lax.axis_index("s") * SC_CORES + lax.axis_index("c")


def _sc_mesh():
    return plsc.VectorSubcoreMesh(core_axis_name="c", subcore_axis_name="s")


def _sc_gather(table, idx):
    N, W = idx.shape[0], table.shape[1]
    per_w = N // SC_WORKERS
    n_chunks = per_w // SC_ROWS

    def body(table_hbm, idx_hbm, out_hbm, idx_v, rows_v, sem):
        base = _sc_worker() * per_w
        pltpu.sync_copy(idx_hbm.at[pl.ds(base, per_w)], idx_v)

        @pl.loop(0, n_chunks)
        def _(c):
            off = pl.multiple_of(c * SC_ROWS, SC_ROWS)
            pltpu.async_copy(table_hbm.at[idx_v.at[pl.ds(off, SC_ROWS)]], rows_v, sem).wait()
            pltpu.sync_copy(rows_v, out_hbm.at[pl.ds(base + off, SC_ROWS)])

    return pl.kernel(
        body, out_type=jax.ShapeDtypeStruct((N, W), table.dtype), mesh=_sc_mesh(),
        scratch_types=[pltpu.VMEM((per_w,), jnp.int32), pltpu.VMEM((SC_ROWS, W), table.dtype),
                       pltpu.SemaphoreType.DMA],
        name="sc_gather",
    )(table, idx)


def _sc_dispatch(pos, table, n_slots):
    NP, (T, W) = pos.shape[0], table.shape
    per_w = n_slots // SC_WORKERS
    n_chunks = per_w // SC_ROWS
    scan = 8192

    def body(pos_hbm, table_hbm, out_hbm, pos_v, src_v, rows_v, sem):
        base = _sc_worker() * per_w
        lane = lax.iota(jnp.int32, SC_LANES)

        @pl.loop(0, per_w // SC_LANES)
        def _(j):
            o = pl.multiple_of(j * SC_LANES, SC_LANES)
            src_v[pl.ds(o, SC_LANES)] = (base + o + lane) & (T - 1)

        @pl.loop(0, NP // scan)
        def _(c):
            pltpu.sync_copy(pos_hbm.at[pl.ds(pl.multiple_of(c * scan, scan), scan)], pos_v)

            @pl.loop(0, scan // SC_LANES)
            def _(v):
                o = pl.multiple_of(v * SC_LANES, SC_LANES)
                p = pos_v[pl.ds(o, SC_LANES)] - base
                mine = (p >= 0) & (p < per_w)
                tok = (c * scan + o + lane) & (T - 1)
                plsc.store_scatter(src_v, [jnp.where(mine, p, 0)], tok, mask=mine)

        @pl.loop(0, n_chunks)
        def _(c):
            off = pl.multiple_of(c * SC_ROWS, SC_ROWS)
            pltpu.async_copy(table_hbm.at[src_v.at[pl.ds(off, SC_ROWS)]], rows_v, sem).wait()
            pltpu.sync_copy(rows_v, out_hbm.at[pl.ds(base + off, SC_ROWS)])

    return pl.kernel(
        body, out_type=jax.ShapeDtypeStruct((n_slots, W), table.dtype), mesh=_sc_mesh(),
        scratch_types=[pltpu.VMEM((scan,), jnp.int32), pltpu.VMEM((per_w,), jnp.int32),
                       pltpu.VMEM((SC_ROWS, W), table.dtype), pltpu.SemaphoreType.DMA],
        compiler_params=pltpu.CompilerParams(needs_layout_passes=False),
        name="sc_dispatch",
    )(pos, table)


def _caches_kernel(*refs, nb, S):
    n_in = 6 * DEPTH
    outs = refs[n_in:]
    l = pl.program_id(0)
    for a in range(DEPTH):
        @pl.when(l == a)
        def _(a=a):
            ckv, kpe, wk, wv, nk, nv = refs[6 * a:6 * (a + 1)]
            for g in range(nb):
                rows = slice(g * S, (g + 1) * S)
                outs[0][g] = ckv[rows, :]
                outs[1][g] = kpe[rows, 64:96]
                outs[2][g] = wk[rows, :]
                outs[3][g] = wv[rows, :]
                outs[4][g] = nk[rows, :]
                outs[5][g] = nv[rows, :]


def _emit_caches(projs, ckvs, B, S):
    nb = 4
    while B % nb:
        nb //= 2

    def layer_specs(a):
        row = lambda l, b: jnp.where(l == a, b, 0)
        col = lambda w, off: pl.BlockSpec((nb * S, w), lambda l, b: (row(l, b), off // w))
        return [pl.BlockSpec((nb * S, 128), lambda l, b: (row(l, b), 0)), col(128, P_KPE), col(128, P_WK),
                col(128, P_WV), col(256, P_NK), col(256, P_NV)]

    in_specs, args = [], []
    for a in range(DEPTH):
        in_specs += layer_specs(a)
        args += [ckvs[a]] + [projs[a]] * 5
    widths = (128, 32, 128, 128, 256, 256)
    return pl.pallas_call(
        functools.partial(_caches_kernel, nb=nb, S=S),
        grid=(DEPTH, B // nb),
        in_specs=in_specs,
        out_specs=[pl.BlockSpec((nb, None, S, w), lambda l, b: (b, l, 0, 0)) for w in widths],
        out_shape=[jax.ShapeDtypeStruct((B, DEPTH, S, w), F32) for w in widths],
        compiler_params=_cp("arbitrary", "arbitrary"),
        name="context_tensors",
    )(*args)


def _rot_cols(w, q):
    a, b, c, d = w[..., :q], w[..., q:2 * q], w[..., 2 * q:3 * q], w[..., 3 * q:]
    return jnp.concatenate([-b, a, -d, c], -1)


def _prep_w_in(w):
    z = lambda n: jnp.zeros((D, n), w.dtype)
    qlat, ckv, kpe, hy = w[:, 0:256], w[:, 256:384], w[:, 384:416], w[:, 416:1184]
    wq, wk, wv = w[:, 1184:1440], w[:, 1440:1568], w[:, 1568:1696]
    nq, nk, nv, gate = w[:, 1696:1952], w[:, 1952:2208], w[:, 2208:2464], w[:, 2464:]
    wq_r = _rot_cols(wq.reshape(D, 4, 64), 16).reshape(D, 256)
    wk_r = _rot_cols(wk.reshape(D, 2, 64), 16).reshape(D, 128)
    kpe_r = _rot_cols(kpe, 8)
    cols = [qlat, ckv, z(64), kpe, z(32), hy, wq, wk, wv, nq, nk, nv, wq_r, wk_r, z(64), kpe_r, z(32), gate]
    return jnp.concatenate(cols, 1).astype(BF16)


def _prep_mla(w_uq, w_ukv):
    uq = w_uq.reshape(256, 4, 96)
    nope, pe = uq[..., :64], uq[..., 64:]
    z32 = jnp.zeros((256, 4, 32), w_uq.dtype)
    z64 = jnp.zeros((256, 4, 64), w_uq.dtype)
    wcat = jnp.concatenate([nope, pe, z32], -1).reshape(256, 512).astype(BF16)
    wrot = jnp.concatenate([z64, _rot_cols(pe, 8), z32], -1).reshape(256, 512).astype(BF16)
    ukv = w_ukv.reshape(128, 4, 128)
    wk = jnp.concatenate([ukv[..., :64], jnp.zeros((128, 4, 64), w_ukv.dtype)], -1).reshape(128, 512).astype(BF16)
    wv = ukv[..., 64:].reshape(128, 256).astype(BF16)
    return wcat, wrot, wk, wv


def _rope_tab(L, q):
    t = jnp.arange(L)
    inv = ROPE_BASE ** (-jnp.arange(q, dtype=F32) / q)
    ar = (t // GRID_W).astype(F32)[:, None] * inv[None, :]
    ac = (t % GRID_W).astype(F32)[:, None] * inv[None, :]
    cos = jnp.concatenate([jnp.cos(ar), jnp.cos(ar), jnp.cos(ac), jnp.cos(ac)], 1)
    sin = jnp.concatenate([jnp.sin(ar), jnp.sin(ar), jnp.sin(ac), jnp.sin(ac)], 1)
    return cos, sin


def _rope_tables(L):
    c8, s8 = _rope_tab(L, 8)
    c16, s16 = _rope_tab(L, 16)
    one, zero = jnp.ones((L, 64), F32), jnp.zeros((L, 64), F32)
    z32 = jnp.zeros((L, 32), F32)
    mla_q = (jnp.tile(jnp.concatenate([one, c8, z32], 1), (1, 4)), jnp.tile(jnp.concatenate([zero, s8, z32], 1), (1, 4)))
    mla_k = (jnp.concatenate([zero, c8, z32], 1), jnp.concatenate([zero, s8, z32], 1))
    win = (jnp.tile(c16, (1, 4)), jnp.tile(s16, (1, 4)), jnp.tile(c16, (1, 2)), jnp.tile(s16, (1, 2)))
    return mla_q + mla_k, win


def _hyena(proj, lp, dft, NB, Lb):
    cm, sm, smt = dft
    tm = min(Lb, 512)
    v, x1, x2 = _short_conv(proj, lp["hy_conv_w"], lp["hy_conv_b"].reshape(1, -1), NB, Lb)
    w1p = jnp.pad(lp["hy_w1"], ((0, 128 - lp["hy_w1"].shape[0]), (0, 0)))
    fs, nyq = _hy_filter(Lb, w1p, lp["hy_b1"].reshape(1, -1), lp["hy_w2"], lp["hy_b2"].reshape(1, -1), lp["hy_w3"],
                         lp["hy_sin_freq"], lp["hy_log_decay"].reshape(1, -1))
    gr, gi = _hy_gdft(cm, sm, fs, nyq, Lb, tm)
    skip = lp["hy_skip"].reshape(2, 1, HY_C)
    z = v
    for n, gate in enumerate((x1, x2)):
        yr, yi = _hy_fwd(cm, sm, z, gr, gi, n, NB, Lb, tm)
        z = _hy_inv(cm, smt, yr, yi, z, gate, skip, n, NB, Lb, tm)
    return z


def _layer_steps(x, mod, lp, l, NB, Lb, mod_row_of_batch, dft, cache=None, tabs=None, na_bias=None):
    T = NB * Lb
    latent = cache is not None
    bm = 256
    rows_of = lambda n: (lambda i: mod_row_of_batch((i * n) // Lb))
    mod_row = rows_of(bm)
    span = Lb if latent else T
    bmp = min(span, 1024)
    proj, gates = _in_proj(x, mod, lp["w_in_p"], rows_of(bmp), bmp)

    gq, gkv = lp["mla_q_norm"].reshape(1, -1), lp["mla_kv_norm"].reshape(1, -1)
    wcat, wrot, wk, wv = lp["mla_w"]
    q_all, ckv_n, kpe_r = _mla_q(proj, gq, gkv, wcat, wrot, tabs[0] if latent else None, Lb, min(span, 512))
    if latent:
        ckv_c, kpe_c, kc_c, vc_c, kd_c, vd_c = cache
        Lc = ckv_c.shape[1]
        kpe_cp = jnp.pad(kpe_c, ((0, 0), (0, 0), (64, 32)))
        ckv_all = jnp.concatenate([ckv_c, ckv_n.reshape(NB, Lb, 128)], 1).reshape(NB * (Lc + Lb), 128)
        kpe_all = jnp.concatenate([kpe_cp, kpe_r.reshape(NB, Lb, 128)], 1).reshape(NB * (Lc + Lb), 128)
        k_all, v_all = _mla_kv(ckv_all, kpe_all, wk, wv, 512)
        oc = _lat_win_attention(proj, kc_c.reshape(NB, Lc, 128), vc_c.reshape(NB, Lc, 128), tabs[1],
                                lp["win_sink"], NB, Lb)
        od = _lat_na_attention(proj, kd_c.reshape(NB, Lc, 256), vd_c.reshape(NB, Lc, 256), na_bias, NB, Lb)
        ob = _hyena(proj, lp, dft, NB, Lb)
        after = yield "projected", od
        oa = _lat_mla_attention(q_all, k_all, v_all, NB, Lb, Lc + Lb, 256, after=after)
        after = None
    else:
        k_all, v_all = _mla_kv(ckv_n, kpe_r, wk, wv, 512)
        oa, oc, od = _ctx_attention(proj, q_all, k_all, v_all, lp["win_sink"], NB, Lb)
        ob = _hyena(proj, lp, dft, NB, Lb)
        after = yield "projected", oa

    bmm = min(span, 512)
    x1, h2, hp = _merge((oa, ob, oc, od), gates, lp["w_branch_b"], lp["w_out_b"], x, mod,
                        lp["ln1_g"].reshape(1, -1), lp["ln1_b"].reshape(1, -1), rows_of(bmm), bmm, after=after)
    n_slots = T * TOP_K + N_EXPERTS * MOE_TM
    gate_t, rank, cnt = _router(h2, lp["moe_router"].T, lp["moe_bias"].reshape(-1, 1), 512)
    pos, w8, te, nx, nt = _route_pos(gate_t, rank, cnt, 512, MOE_TM, n_slots // MOE_TM)
    xs = _sc_dispatch(pos.reshape(-1), hp, n_slots)
    after = yield "dispatched", None
    ys = _gmm(te.reshape(-1), nx.reshape(-1), nt.reshape(-1)[:1], xs, lp["moe_w1"], lp["moe_w3"], lp["moe_w2"], l,
              MOE_TM, after=after)
    yield "ffn", ys
    yk = _sc_gather(ys, pos.reshape(-1)).reshape(TOP_K, T, D // 2)
    x2 = _combine(yk, w8.T, hp, lp["sh_w1_b"], lp["sh_w3_b"], lp["sh_w2_b"], x1, mod,
                  lp["ln2_g"].reshape(1, -1), lp["ln2_b"].reshape(1, -1), mod_row, bm)
    yield "done", (x2, proj, ckv_n)


def kernel(x_prompt, x_sample, cache_mla_ckv, cache_mla_kpe, cache_win_k, cache_win_v, cache_na_k, cache_na_v, c, c_ctx, w_ada, b_ada, w_in, mla_q_norm, mla_kv_norm, mla_w_uq, mla_w_ukv, hy_conv_w, hy_conv_b, hy_w1, hy_b1, hy_w2, hy_b2, hy_w3, hy_sin_freq, hy_log_decay, hy_skip, win_sink, na_rpb, w_branch, w_out, ln1_g, ln1_b, ln2_g, ln2_b, moe_router, moe_bias, moe_w1, moe_w3, moe_w2, sh_w1, sh_w3, sh_w2):
    B, S, _ = x_prompt.shape
    DB, DS, _ = x_sample.shape
    xp = x_prompt.reshape(B * S, D)
    xs = x_sample.reshape(DB * DS, D)
    cvec = jnp.concatenate([c_ctx[None, :], c, jnp.zeros((8 - 1 - DB, D), F32)], 0)
    dft_ctx = _dft_mats(S)
    dft_lat = _dft_mats(DS)
    tabs = _rope_tables(DS)
    projs, ckvs = [], []

    def params(l):
        return dict(w_in_p=_prep_w_in(w_in[l]), mla_q_norm=mla_q_norm[l], mla_kv_norm=mla_kv_norm[l],
                    mla_w=_prep_mla(mla_w_uq[l], mla_w_ukv[l]), hy_conv_w=hy_conv_w[l], hy_conv_b=hy_conv_b[l],
                    hy_w1=hy_w1[l], hy_b1=hy_b1[l], hy_w2=hy_w2[l], hy_b2=hy_b2[l], hy_w3=hy_w3[l],
                    hy_sin_freq=hy_sin_freq[l], hy_log_decay=hy_log_decay[l], hy_skip=hy_skip[l],
                    win_sink=win_sink[l], w_branch_b=w_branch[l].astype(BF16), w_out_b=w_out[l].astype(BF16),
                    ln1_g=ln1_g[l], ln1_b=ln1_b[l], ln2_g=ln2_g[l], ln2_b=ln2_b[l],
                    moe_router=moe_router[l], moe_bias=moe_bias[l], moe_w1=moe_w1, moe_w3=moe_w3, moe_w2=moe_w2,
                    sh_w1_b=sh_w1[l].astype(BF16), sh_w3_b=sh_w3[l].astype(BF16), sh_w2_b=sh_w2[l].astype(BF16))

    lps = [params(l) for l in range(DEPTH)]
    mods = [_modulation(cvec, w_ada, b_ada, l) for l in range(DEPTH)]

    def ctx_layer(l, x):
        return _layer_steps(x, mods[l], lps[l], l, B, S, lambda b: 0, dft_ctx)

    def lat_layer(l, x):
        cache = (cache_mla_ckv[:, l], cache_mla_kpe[:, l], cache_win_k[:, l], cache_win_v[:, l],
                 cache_na_k[:, l], cache_na_v[:, l])
        return _layer_steps(x, mods[l], lps[l], l, DB, DS, lambda b: 1 + b, dft_lat, cache=cache, tabs=tabs,
                            na_bias=_na_bias(na_rpb[l]))

    ctx = ctx_layer(0, xp)
    next(ctx)
    ctx.send(None)
    for l in range(DEPTH):
        lat = lat_layer(l, xs)
        lat_local = next(lat)[1]
        ys_ctx = ctx.send(lat_local)[1]
        lat.send(ys_ctx)
        xp, proj, ckv_n = ctx.send(None)[1]
        projs.append(proj)
        ckvs.append(ckv_n)
        if l + 1 < DEPTH:
            ctx = ctx_layer(l + 1, xp)
            attended = next(ctx)[1]
            ys_lat = lat.send(attended)[1]
            ctx.send(ys_lat)
        else:
            ckv, kpe, wk, wv, nk, nv = _emit_caches(projs, ckvs, B, S)
            lat.send((xp, nv))
        xs = lat.send(None)[1][0]
    heads = lambda t, h: t.reshape(B, DEPTH, S, h, HEAD_DIM)
    return (xp.reshape(B, S, D), xs.reshape(DB, DS, D), ckv, kpe, heads(wk, 2), heads(wv, 2), heads(nk, 4),
            heads(nv, 4))
```

```python
import functools
import math

import jax
import jax.numpy as jnp
from jax import lax
from jax.experimental import pallas as pl
from jax.experimental.pallas import tpu as pltpu
from jax.experimental.pallas import tpu_sc as plsc

F32 = jnp.float32
BF16 = jnp.bfloat16

D = 1024
DEPTH = 2
GRID_W = 64
HEAD_DIM = 64
MLA_SCALE = 96 ** -0.5
ATT_SCALE = HEAD_DIM ** -0.5
HY_C = 256
HY_BANDS = 8
NA_KH = 8
NA_KW = 16
N_EXPERTS = 64
N_GROUPS = 8
TOP_K = 8
TOPK_GROUPS = 4
D_EXPERT = 256
ROUTED_SCALE = 2.5
ROPE_BASE = 10000.0
LN_EPS = 1e-5
RMS_EPS = 1e-6
NEG = -1e30
DN_ALPHA = (2 * DEPTH) ** 0.25

P_QLAT, P_CKV, P_KPE, P_HY = 0, 256, 384, 512
P_WQ, P_WK, P_WV = 1280, 1536, 1664
P_NQ, P_NK, P_NV = 1792, 2048, 2304
P_WQR, P_WKR, P_KPER, P_GATE = 2560, 2816, 2944, 3072
N_PROJ = 7168

VMEM_LIMIT = 56 * 1024 * 1024

SC_CORES = 2
SC_SUBCORES = 16
SC_LANES = 16
SC_WORKERS = SC_CORES * SC_SUBCORES
SC_ROWS = 64

MOE_TM = 512

def _cp(*sem):
    return pltpu.CompilerParams(dimension_semantics=sem, vmem_limit_bytes=VMEM_LIMIT)


def _sigmoid(x):
    return 1.0 / (1.0 + jnp.exp(-x))


def _dot(a, b):
    return jnp.dot(a, b, preferred_element_type=F32)


def _dot_nt(a, b):
    return lax.dot_general(a, b, (((1,), (1,)), ((), ())), preferred_element_type=F32)


def _dot_hi(a, b):
    return jnp.dot(a, b, preferred_element_type=F32, precision=lax.Precision.HIGHEST)


def _pack_pairs(x):
    w = x.shape[1] // 2
    hi = lax.bitcast_convert_type(x[:, :w].astype(BF16).astype(F32), jnp.int32)
    lo = lax.bitcast_convert_type(x[:, w:].astype(BF16).astype(F32), jnp.int32)
    return hi | lax.shift_right_logical(lo, 16)


def _unpack_pairs(p):
    hi = lax.bitcast_convert_type(p & jnp.int32(-65536), F32)
    lo = lax.bitcast_convert_type(lax.shift_left(p, 16), F32)
    return hi, lo


def _layer_norm(x, g, b):
    mu = jnp.mean(x, -1, keepdims=True)
    xc = x - mu
    var = jnp.mean(xc * xc, -1, keepdims=True)
    return xc * lax.rsqrt(var + LN_EPS) * g + b


def _rms_norm(x, g):
    return x * lax.rsqrt(jnp.mean(x * x, -1, keepdims=True) + RMS_EPS) * g


def _mod_kernel(c_ref, w_ref, b_ref, o_ref):
    c = c_ref[...]
    a = (c * _sigmoid(c)).astype(BF16)
    o_ref[...] = _dot(a, w_ref[...].astype(BF16)) + b_ref[...]


def _modulation(cvec, w_ada, b_ada, l):
    out = pl.pallas_call(
        _mod_kernel,
        grid=(6,),
        in_specs=[pl.BlockSpec((8, D), lambda j: (0, 0)),
                  pl.BlockSpec((None, D, D), lambda j: (l, 0, j)),
                  pl.BlockSpec((None, 1, D), lambda j: (l, 0, j))],
        out_specs=pl.BlockSpec((8, D), lambda j: (0, j)),
        out_shape=jax.ShapeDtypeStruct((8, 6 * D), F32),
        compiler_params=_cp("arbitrary"),
        name="modulation",
    )(cvec, w_ada, b_ada.reshape(DEPTH, 1, 6 * D))
    return out.reshape(8, 6, D)


def _inproj_kernel(x_ref, mod_ref, w_ref, o_ref, g_ref, h_ref, *, n_main):
    j = pl.program_id(1)

    @pl.when(j == 0)
    def _():
        m = mod_ref[...]
        h_ref[...] = (x_ref[...] * (1.0 + m[1:2, :]) + m[0:1, :]).astype(BF16)

    y = _dot(h_ref[...], w_ref[...])

    @pl.when(j < n_main)
    def _():
        o_ref[...] = y

    @pl.when(j >= n_main)
    def _():
        g_ref[...] = y.astype(BF16)


def _in_proj(x, mod, w_p, mod_row, bm, bn=1024):
    T = x.shape[0]
    n_main = P_GATE // bn
    return pl.pallas_call(
        functools.partial(_inproj_kernel, n_main=n_main),
        grid=(T // bm, N_PROJ // bn),
        in_specs=[pl.BlockSpec((bm, D), lambda i, j: (i, 0)),
                  pl.BlockSpec((None, 6, D), lambda i, j: (mod_row(i), 0, 0)),
                  pl.BlockSpec((D, bn), lambda i, j: (0, j))],
        out_specs=[pl.BlockSpec((bm, bn), lambda i, j: (i, jnp.minimum(j, n_main - 1))),
                   pl.BlockSpec((bm, bn), lambda i, j: (i, jnp.maximum(j - n_main, 0)))],
        out_shape=[jax.ShapeDtypeStruct((T, P_GATE), F32), jax.ShapeDtypeStruct((T, N_PROJ - P_GATE), BF16)],
        scratch_shapes=[pltpu.VMEM((bm, D), BF16)],
        compiler_params=_cp("arbitrary", "arbitrary"),
        name="in_proj",
    )(x, mod, w_p)


def _mla_q_kernel(*refs, rope):
    if rope:
        (ql_ref, ckv_ref, kpe_ref, kper_ref, gq_ref, gkv_ref, wc_ref, wr_ref,
         cq_ref, sq_ref, ck_ref, sk_ref, q_ref, ckvn_ref, kpeo_ref) = refs
    else:
        ql_ref, ckv_ref, kpe_ref, gq_ref, gkv_ref, wc_ref, q_ref, ckvn_ref, kpeo_ref = refs
    qn = _rms_norm(ql_ref[...], gq_ref[...]).astype(BF16)
    q = _dot(qn, wc_ref[...])
    if rope:
        q = q * cq_ref[...] + _dot(qn, wr_ref[...]) * sq_ref[...]
        kpeo_ref[...] = kpe_ref[...] * ck_ref[...] + kper_ref[...] * sk_ref[...]
    else:
        kpeo_ref[...] = kpe_ref[...]
    q_ref[...] = (q * MLA_SCALE).astype(BF16)
    ckvn_ref[...] = _rms_norm(ckv_ref[...], gkv_ref[...])


def _mla_q(proj, gq, gkv, wcat, wrot, tabs, Lb, bm):
    T = proj.shape[0]
    rope = tabs is not None
    nl = Lb // bm
    col = lambda c: (lambda i: (i, c))
    fixed = lambda i: (0, 0)
    in_specs = [pl.BlockSpec((bm, 256), col(P_QLAT // 256)),
                pl.BlockSpec((bm, 128), col(P_CKV // 128)),
                pl.BlockSpec((bm, 128), col(P_KPE // 128))]
    args = [proj, proj, proj]
    if rope:
        in_specs.append(pl.BlockSpec((bm, 128), col(P_KPER // 128)))
        args.append(proj)
    in_specs += [pl.BlockSpec((1, 256), fixed), pl.BlockSpec((1, 128), fixed), pl.BlockSpec((256, 512), fixed)]
    args += [gq, gkv, wcat]
    if rope:
        cq, sq, ck, sk = tabs
        pos = lambda i: (i % nl, 0)
        in_specs += [pl.BlockSpec((256, 512), fixed), pl.BlockSpec((bm, 512), pos), pl.BlockSpec((bm, 512), pos),
                     pl.BlockSpec((bm, 128), pos), pl.BlockSpec((bm, 128), pos)]
        args += [wrot, cq, sq, ck, sk]
    return pl.pallas_call(
        functools.partial(_mla_q_kernel, rope=rope),
        grid=(T // bm,),
        in_specs=in_specs,
        out_specs=[pl.BlockSpec((bm, 512), lambda i: (i, 0)),
                   pl.BlockSpec((bm, 128), lambda i: (i, 0)),
                   pl.BlockSpec((bm, 128), lambda i: (i, 0))],
        out_shape=[jax.ShapeDtypeStruct((T, 512), BF16),
                   jax.ShapeDtypeStruct((T, 128), F32),
                   jax.ShapeDtypeStruct((T, 128), F32)],
        compiler_params=_cp("arbitrary"),
        name="mla_q",
    )(*args)


def _mla_kv_kernel(ckv_ref, kpe_ref, wk_ref, wv_ref, k_ref, v_ref):
    c = ckv_ref[...].astype(BF16)
    kpe = kpe_ref[...]
    k_ref[...] = (_dot(c, wk_ref[...]) + jnp.concatenate([kpe] * 4, axis=1)).astype(BF16)
    v_ref[...] = _dot(c, wv_ref[...]).astype(BF16)


def _mla_kv(ckv, kpe, wk, wv, bm):
    Tk = ckv.shape[0]
    return pl.pallas_call(
        _mla_kv_kernel,
        grid=(Tk // bm,),
        in_specs=[pl.BlockSpec((bm, 128), lambda i: (i, 0)),
                  pl.BlockSpec((bm, 128), lambda i: (i, 0)),
                  pl.BlockSpec((128, 512), lambda i: (0, 0)),
                  pl.BlockSpec((128, 256), lambda i: (0, 0))],
        out_specs=[pl.BlockSpec((bm, 512), lambda i: (i, 0)),
                   pl.BlockSpec((bm, 256), lambda i: (i, 0))],
        out_shape=[jax.ShapeDtypeStruct((Tk, 512), BF16),
                   jax.ShapeDtypeStruct((Tk, 256), BF16)],
        compiler_params=_cp("arbitrary"),
        name="mla_kv",
    )(ckv, kpe, wk, wv)


def _attn_core(q, kvs, masks, sink):
    ss = []
    for (k, _), mk in zip(kvs, masks):
        s = _dot_nt(q, k)
        if mk is not None:
            s = s + mk[1] if mk[0] == "add" else jnp.where(mk[1], s, NEG)
        ss.append(s)
    m = ss[0].max(-1, keepdims=True)
    for s in ss[1:]:
        m = jnp.maximum(m, s.max(-1, keepdims=True))
    if sink is not None:
        m = jnp.maximum(m, sink)
    den = None
    acc = None
    for s, (_, v) in zip(ss, kvs):
        p = jnp.exp(s - m)
        d = p.sum(-1, keepdims=True)
        a = _dot(p.astype(BF16), v)
        den = d if den is None else den + d
        acc = a if acc is None else acc + a
    if sink is not None:
        den = den + jnp.exp(sink - m)
    return acc / den


def _ctx_attn_kernel(qm_ref, km_ref, vm_ref, wq_ref, wk_ref, wv_ref, nq_ref, nk_ref, nv_ref, sink_ref, *rest):
    om_ref, ow_ref, on_ref = rest[-3:]
    for h in range(4):
        q = qm_ref[:, 128 * h:128 * (h + 1)]
        k = km_ref[:, 128 * h:128 * (h + 1)]
        v = vm_ref[:, 64 * h:64 * (h + 1)]
        om_ref[:, 64 * h:64 * (h + 1)] = _attn_core(q, [(k, v)], [None], None)
    for h in range(4):
        g = h // 2
        q = (wq_ref[:, 64 * h:64 * (h + 1)] * ATT_SCALE).astype(BF16)
        k = wk_ref[:, 64 * g:64 * (g + 1)].astype(BF16)
        v = wv_ref[:, 64 * g:64 * (g + 1)].astype(BF16)
        ow_ref[:, 64 * h:64 * (h + 1)] = _attn_core(q, [(k, v)], [None], sink_ref[h])
    for h in range(4):
        q = (nq_ref[:, 64 * h:64 * (h + 1)] * ATT_SCALE).astype(BF16)
        k = nk_ref[:, 64 * h:64 * (h + 1)].astype(BF16)
        v = nv_ref[:, 64 * h:64 * (h + 1)].astype(BF16)
        on_ref[:, 64 * h:64 * (h + 1)] = _attn_core(q, [(k, v)], [None], None)


def _ctx_attention(proj, q_all, k_all, v_all, sink, NB, Lb, after=None):
    T = proj.shape[0]
    pc = lambda w, off: pl.BlockSpec((Lb, w), lambda b: (b, off // w))
    row = lambda w: pl.BlockSpec((Lb, w), lambda b: (b, 0))
    dep_specs, dep_args = _after(after)
    return pl.pallas_call(
        _ctx_attn_kernel,
        grid=(NB,),
        in_specs=[row(512), row(512), row(256),
                  pc(256, P_WQ), pc(128, P_WK), pc(128, P_WV),
                  pc(256, P_NQ), pc(256, P_NK), pc(256, P_NV),
                  pl.BlockSpec(memory_space=pltpu.SMEM)] + dep_specs,
        out_specs=[row(256), row(256), row(256)],
        out_shape=[jax.ShapeDtypeStruct((T, 256), F32)] * 3,
        compiler_params=_cp("arbitrary"),
        name="ctx_attention",
    )(q_all, k_all, v_all, proj, proj, proj, proj, proj, proj, sink, *dep_args)


def _lat_mla_kernel(q_ref, k_ref, v_ref, *rest):
    o_ref = rest[-1]
    for h in range(4):
        q = q_ref[:, 128 * h:128 * (h + 1)]
        k = k_ref[:, 128 * h:128 * (h + 1)]
        v = v_ref[:, 64 * h:64 * (h + 1)]
        o_ref[:, 64 * h:64 * (h + 1)] = _attn_core(q, [(k, v)], [None], None)


def _after(after):
    deps = [] if after is None else list(after) if isinstance(after, (tuple, list)) else [after]
    return [pl.BlockSpec(memory_space=pl.ANY)] * len(deps), deps


def _lat_mla_attention(q_all, k_all, v_all, NB, Lb, Lk, tq, after=None):
    T = q_all.shape[0]
    nq = Lb // tq
    dep_specs, dep_args = _after(after)
    return pl.pallas_call(
        _lat_mla_kernel,
        grid=(NB, nq),
        in_specs=[pl.BlockSpec((tq, 512), lambda b, i: (b * nq + i, 0)),
                  pl.BlockSpec((Lk, 512), lambda b, i: (b, 0)),
                  pl.BlockSpec((Lk, 256), lambda b, i: (b, 0))] + dep_specs,
        out_specs=pl.BlockSpec((tq, 256), lambda b, i: (b * nq + i, 0)),
        out_shape=jax.ShapeDtypeStruct((T, 256), F32),
        compiler_params=_cp("arbitrary", "arbitrary"),
        name="lat_mla_attention",
    )(q_all, k_all, v_all, *dep_args)


def _attn_local_ctx(q, locs, kc, vc, sink):
    s_ctx = _dot_nt(q, kc)
    m_ctx = s_ctx.max(-1, keepdims=True)
    if sink is not None:
        m_ctx = jnp.maximum(m_ctx, sink)
    ms, dens, accs = [], [], []
    for rs, k, v, mk in locs:
        s = _dot_nt(q[rs], k)
        s = s + mk[1] if mk[0] == "add" else jnp.where(mk[1], s, NEG)
        m = jnp.maximum(s.max(-1, keepdims=True), m_ctx[rs])
        p = jnp.exp(s - m)
        ms.append(m)
        dens.append(p.sum(-1, keepdims=True))
        accs.append(_dot(p.astype(BF16), v))
    m = jnp.concatenate(ms, axis=0)
    p = jnp.exp(s_ctx - m)
    den = jnp.concatenate(dens, axis=0) + p.sum(-1, keepdims=True)
    if sink is not None:
        den = den + jnp.exp(sink - m)
    return (jnp.concatenate(accs, axis=0) + _dot(p.astype(BF16), vc)) / den


def _lat_win_kernel(q_ref, qr_ref, k_ref, kr_ref, v_ref, kc_ref, vc_ref, cq_ref, sq_ref, ck_ref, sk_ref,
                    sink_ref, o_ref, *, Lb, bpt):
    t = pl.program_id(1)
    q = (q_ref[...] * cq_ref[...] + qr_ref[...] * sq_ref[...]) * ATT_SCALE
    kc = kc_ref[...].astype(BF16)
    vc = vc_ref[...].astype(BF16)
    blocks = []
    for bb in range(bpt):
        i = t * bpt + bb
        start = pl.multiple_of(jnp.clip((i - 1) * 128, 0, Lb - 384), 128)
        win = pl.ds(start, 384)
        kk = (k_ref[win, :] * ck_ref[win, :] + kr_ref[win, :] * sk_ref[win, :]).astype(BF16)
        qpos = i * 128 + lax.broadcasted_iota(jnp.int32, (128, 384), 0)
        kpos = start + lax.broadcasted_iota(jnp.int32, (128, 384), 1)
        blocks.append((kk, v_ref[win, :].astype(BF16), jnp.abs(qpos - kpos) <= 128))
    for h in range(4):
        g = h // 2
        sl = slice(64 * g, 64 * (g + 1))
        locs = [(slice(128 * bb, 128 * (bb + 1)), kk[:, sl], vv[:, sl], ("keep", valid))
                for bb, (kk, vv, valid) in enumerate(blocks)]
        qh = q[:, 64 * h:64 * (h + 1)].astype(BF16)
        o_ref[:, 64 * h:64 * (h + 1)] = _attn_local_ctx(qh, locs, kc[:, sl], vc[:, sl], sink_ref[h])


def _lat_win_attention(proj, kc, vc, tabs, sink, NB, Lb):
    T = proj.shape[0]
    bpt = 1
    tq = 128 * bpt
    nt = Lb // tq
    Lc = kc.shape[1]
    cq, sq, ck, sk = tabs
    qspec = lambda off: pl.BlockSpec((tq, 256), lambda b, i: (b * nt + i, off // 256))
    kspec = lambda off: pl.BlockSpec((Lb, 128), lambda b, i: (b, off // 128))
    cspec = pl.BlockSpec((None, Lc, 128), lambda b, i: (b, 0, 0))
    return pl.pallas_call(
        functools.partial(_lat_win_kernel, Lb=Lb, bpt=bpt),
        grid=(NB, nt),
        in_specs=[qspec(P_WQ), qspec(P_WQR), kspec(P_WK), kspec(P_WKR), kspec(P_WV), cspec, cspec,
                  pl.BlockSpec((tq, 256), lambda b, i: (i, 0)), pl.BlockSpec((tq, 256), lambda b, i: (i, 0)),
                  pl.BlockSpec((Lb, 128), lambda b, i: (0, 0)), pl.BlockSpec((Lb, 128), lambda b, i: (0, 0)),
                  pl.BlockSpec(memory_space=pltpu.SMEM)],
        out_specs=pl.BlockSpec((tq, 256), lambda b, i: (b * nt + i, 0)),
        out_shape=jax.ShapeDtypeStruct((T, 256), F32),
        compiler_params=_cp("arbitrary", "arbitrary"),
        name="lat_win_attention",
    )(proj, proj, proj, proj, proj, kc, vc, cq, sq, ck, sk, sink)


def _na_bias_kernel(rpb_ref, o_ref):
    h = pl.program_id(0)
    qc = lax.broadcasted_iota(jnp.int32, (GRID_W, GRID_W), 0)
    kc = lax.broadcasted_iota(jnp.int32, (GRID_W, GRID_W), 1)
    dc = kc - qc + (NA_KW - 1)
    wstart = jnp.clip(qc - NA_KW // 2, 0, GRID_W - NA_KW)
    ok = (kc >= wstart) & (kc < wstart + NA_KW)
    n_dc = 2 * NA_KW - 1
    n_dr = 2 * NA_KH - 1
    tabs = []
    for dr in range(n_dr):
        t = jnp.zeros((GRID_W, GRID_W), F32)
        for j in range(n_dc):
            t = jnp.where(dc == j, rpb_ref[(h * n_dr + dr) * n_dc + j], t)
        tabs.append(jnp.where(ok, t, NEG))
    for o in range(NA_KH):
        for a in range(NA_KH):
            o_ref[o, :, GRID_W * a:GRID_W * (a + 1)] = tabs[a + NA_KH - 1 - o]


def _na_bias(rpb):
    H = rpb.shape[0]
    return pl.pallas_call(
        _na_bias_kernel,
        grid=(H,),
        in_specs=[pl.BlockSpec(memory_space=pltpu.SMEM)],
        out_specs=pl.BlockSpec((None, NA_KH, GRID_W, NA_KH * GRID_W), lambda h: (h, 0, 0, 0)),
        out_shape=jax.ShapeDtypeStruct((H, NA_KH, GRID_W, NA_KH * GRID_W), F32),
        compiler_params=_cp("arbitrary"),
        name="na_bias",
    )(rpb.reshape(-1))


def _lat_na_kernel(q_ref, k_ref, v_ref, kc_ref, vc_ref, bias_ref, o_ref, *, rows, rpt):
    t = pl.program_id(1)
    q = q_ref[...] * ATT_SCALE
    kc = kc_ref[...].astype(BF16)
    vc = vc_ref[...].astype(BF16)
    bands = []
    for rr in range(rpt):
        r = t * rpt + rr
        first = jnp.clip(r - NA_KH // 2, 0, rows - NA_KH)
        win = pl.ds(pl.multiple_of(first * GRID_W, GRID_W), NA_KH * GRID_W)
        bands.append((k_ref[win, :].astype(BF16), v_ref[win, :].astype(BF16), r - first))
    for h in range(4):
        sl = slice(64 * h, 64 * (h + 1))
        locs = [(slice(GRID_W * rr, GRID_W * (rr + 1)), kk[:, sl], vv[:, sl], ("add", bias_ref[h, off]))
                for rr, (kk, vv, off) in enumerate(bands)]
        o_ref[:, sl] = _attn_local_ctx(q[:, sl].astype(BF16), locs, kc[:, sl], vc[:, sl], None)


def _lat_na_attention(proj, kc, vc, bias, NB, Lb):
    T = proj.shape[0]
    rows = Lb // GRID_W
    rpt = 8
    tq = GRID_W * rpt
    nt = rows // rpt
    Lc = kc.shape[1]
    kspec = lambda off: pl.BlockSpec((Lb, 256), lambda b, t: (b, off // 256))
    cspec = pl.BlockSpec((None, Lc, 256), lambda b, t: (b, 0, 0))
    return pl.pallas_call(
        functools.partial(_lat_na_kernel, rows=rows, rpt=rpt),
        grid=(NB, nt),
        in_specs=[pl.BlockSpec((tq, 256), lambda b, t: (b * nt + t, P_NQ // 256)),
                  kspec(P_NK), kspec(P_NV), cspec, cspec,
                  pl.BlockSpec((4, NA_KH, GRID_W, NA_KH * GRID_W), lambda b, t: (0, 0, 0, 0))],
        out_specs=pl.BlockSpec((tq, 256), lambda b, t: (b * nt + t, 0)),
        out_shape=jax.ShapeDtypeStruct((T, 256), F32),
        compiler_params=_cp("arbitrary", "arbitrary"),
        name="lat_na_attention",
    )(proj, proj, proj, kc, vc, bias)


def _short_conv_kernel(a_ref, b_ref, c_ref, w_ref, bias_ref, oa_ref, ob_ref, oc_ref, *, L):
    t = lax.broadcasted_iota(jnp.int32, (L, HY_C), 0)
    for n, (x_ref, o_ref) in enumerate(((a_ref, oa_ref), (b_ref, ob_ref), (c_ref, oc_ref))):
        sl = slice(HY_C * n, HY_C * (n + 1))
        x = x_ref[...]
        prev = jnp.where(t == 0, 0.0, pltpu.roll(x, 1, axis=0))
        nxt = jnp.where(t == L - 1, 0.0, pltpu.roll(x, L - 1, axis=0))
        o_ref[...] = prev * w_ref[0:1, sl] + x * w_ref[1:2, sl] + nxt * w_ref[2:3, sl] + bias_ref[:, sl]


def _short_conv(proj, w, b, NB, Lb):
    T = proj.shape[0]
    spec = lambda c: pl.BlockSpec((Lb, HY_C), lambda i: (i, c))
    return pl.pallas_call(
        functools.partial(_short_conv_kernel, L=Lb),
        grid=(NB,),
        in_specs=[spec(P_HY // HY_C), spec(P_HY // HY_C + 1), spec(P_HY // HY_C + 2),
                  pl.BlockSpec((3, 3 * HY_C), lambda i: (0, 0)), pl.BlockSpec((1, 3 * HY_C), lambda i: (0, 0))],
        out_specs=[spec(0)] * 3,
        out_shape=[jax.ShapeDtypeStruct((T, HY_C), F32)] * 3,
        compiler_params=_cp("arbitrary"),
        name="hyena_short_conv",
    )(proj, proj, proj, w, b)


def _hy_filter_kernel(w1_ref, b1_ref, w2_ref, b2_ref, w3_ref, freq_ref, ld_ref, fs_ref, nyq_ref, *, L):
    ti = lax.broadcasted_iota(jnp.int32, (L, 128), 0)
    t = ti.astype(F32)
    j = lax.broadcasted_iota(jnp.int32, (L, 128), 1)
    band = jnp.where(j <= HY_BANDS, j - 1, j - 1 - HY_BANDS).astype(F32)
    ang = (2.0 * math.pi / L) * t * band
    tn = t / L
    z = jnp.where(j == 0, tn, jnp.where(j <= HY_BANDS, jnp.cos(ang),
                                        jnp.where(j <= 2 * HY_BANDS, -jnp.sin(ang), 0.0)))
    a = jnp.sin(freq_ref[0:1, :] * (_dot_hi(z, w1_ref[...]) + b1_ref[...]))
    a = jnp.sin(freq_ref[1:2, :] * (_dot_hi(a, w2_ref[...]) + b2_ref[...]))
    filt = _dot_hi(a, w3_ref[...])
    tcol = lax.broadcasted_iota(jnp.int32, (L, 4 * HY_C), 0)
    filt = filt * jnp.exp(-(tcol.astype(F32) / L) * jnp.exp(ld_ref[...]))
    t1 = lax.broadcasted_iota(jnp.int32, (L, HY_C), 0)
    sign = jnp.where(t1 % 2 == 0, 1.0, -1.0)
    for n in range(2):
        fwd = filt[:, 2 * HY_C * n:2 * HY_C * n + HY_C]
        bwd = jnp.where(t1 == 0, 0.0, filt[:, 2 * HY_C * n + HY_C:2 * HY_C * (n + 1)])
        tot = fwd + bwd
        fs_ref[:, HY_C * n:HY_C * (n + 1)] = tot
        fs_ref[:, 2 * HY_C + HY_C * n:2 * HY_C + HY_C * (n + 1)] = fwd - bwd
        nyq_ref[:, HY_C * n:HY_C * (n + 1)] = (tot * sign).sum(0, keepdims=True)


def _hy_filter(L, w1p, b1, w2, b2, w3, freq, ld):
    full = lambda s: pl.BlockSpec(s, lambda: tuple(0 for _ in s))
    return pl.pallas_call(
        functools.partial(_hy_filter_kernel, L=L),
        in_specs=[full((128, 64)), full((1, 64)), full((64, 64)), full((1, 64)), full((64, 4 * HY_C)),
                  full((2, 64)), full((1, 4 * HY_C))],
        out_specs=[full((L, 4 * HY_C)), full((1, 2 * HY_C))],
        out_shape=[jax.ShapeDtypeStruct((L, 4 * HY_C), F32), jax.ShapeDtypeStruct((1, 2 * HY_C), F32)],
        compiler_params=pltpu.CompilerParams(vmem_limit_bytes=VMEM_LIMIT),
        name="hyena_filter",
    )(w1p, b1, w2, b2, w3, freq, ld)


def _hy_gdft_kernel(cm_ref, sm_ref, fs_ref, nyq_ref, gr_ref, gi_ref, *, tm):
    m = pl.program_id(0)
    f = fs_ref[...].astype(BF16)
    gr_ref[...] = _dot(cm_ref[...], f[:, :2 * HY_C])
    gi = _dot(sm_ref[...], f[:, 2 * HY_C:])
    row = m * tm + lax.broadcasted_iota(jnp.int32, (tm, 2 * HY_C), 0)
    gi_ref[...] = jnp.where(row == 0, nyq_ref[...], gi)


def _hy_gdft(cm, sm, fs, nyq, L, tm):
    return pl.pallas_call(
        functools.partial(_hy_gdft_kernel, tm=tm),
        grid=(L // tm,),
        in_specs=[pl.BlockSpec((tm, L), lambda m: (m, 0)), pl.BlockSpec((tm, L), lambda m: (m, 0)),
                  pl.BlockSpec((L, 4 * HY_C), lambda m: (0, 0)), pl.BlockSpec((1, 2 * HY_C), lambda m: (0, 0))],
        out_specs=[pl.BlockSpec((tm, 2 * HY_C), lambda m: (m, 0))] * 2,
        out_shape=[jax.ShapeDtypeStruct((L, 2 * HY_C), F32)] * 2,
        compiler_params=_cp("arbitrary"),
        name="hyena_filter_dft",
    )(cm, sm, fs, nyq)


def _hy_fwd_kernel(cm_ref, sm_ref, z_ref, gr_ref, gi_ref, yr_ref, yi_ref, *, L, tm, ns):
    m = pl.program_id(1)
    gr = gr_ref[...]
    gi = gi_ref[...]
    row0 = (m * tm + lax.broadcasted_iota(jnp.int32, (tm, HY_C), 0)) == 0
    s = jnp.where(row0, 0.5 / L, 1.0 / L)
    for g in range(ns):
        zb = z_ref[g * L:(g + 1) * L, :].astype(BF16)
        zr = _dot(cm_ref[...], zb)
        zi = _dot(sm_ref[...], zb)
        zigi = zi * gi
        yr_ref[g * tm:(g + 1) * tm, :] = ((zr * gr - jnp.where(row0, 0.0, zigi)) * s).astype(BF16)
        yi_ref[g * tm:(g + 1) * tm, :] = (jnp.where(row0, zigi, zr * gi + zi * gr) * s).astype(BF16)


def _hy_seqs_per_step(NB, Lb, tm):
    ns = max(1, 2048 // Lb) if tm == Lb else 1
    while NB % ns:
        ns //= 2
    return ns


def _hy_fwd(cm, sm, z, gr, gi, n, NB, Lb, tm):
    T = z.shape[0]
    nm = Lb // tm
    ns = _hy_seqs_per_step(NB, Lb, tm)
    return pl.pallas_call(
        functools.partial(_hy_fwd_kernel, L=Lb, tm=tm, ns=ns),
        grid=(NB // ns, nm),
        in_specs=[pl.BlockSpec((tm, Lb), lambda b, m: (m, 0)), pl.BlockSpec((tm, Lb), lambda b, m: (m, 0)),
                  pl.BlockSpec((ns * Lb, HY_C), lambda b, m: (b, 0)),
                  pl.BlockSpec((tm, HY_C), lambda b, m: (m, n)), pl.BlockSpec((tm, HY_C), lambda b, m: (m, n))],
        out_specs=[pl.BlockSpec((ns * tm, HY_C), lambda b, m: (b * nm + m, 0))] * 2,
        out_shape=[jax.ShapeDtypeStruct((T, HY_C), BF16)] * 2,
        compiler_params=_cp("arbitrary", "arbitrary"),
        name="hyena_fwd_dft",
    )(cm, sm, z, gr, gi)


def _hy_inv_kernel(cm_ref, smt_ref, yr_ref, yi_ref, z_ref, g_ref, skip_ref, o_ref, *, L, tm, ns):
    for g in range(ns):
        seq = slice(g * L, (g + 1) * L)
        out = slice(g * tm, (g + 1) * tm)
        conv = _dot(cm_ref[...], yr_ref[seq, :]) + _dot(smt_ref[...], yi_ref[seq, :])
        o_ref[out, :] = g_ref[out, :] * (conv + skip_ref[...] * z_ref[out, :])


def _hy_inv(cm, smt, yr, yi, z, gate, skip, n, NB, Lb, tm):
    T = z.shape[0]
    nm = Lb // tm
    ns = _hy_seqs_per_step(NB, Lb, tm)
    tile = pl.BlockSpec((ns * tm, HY_C), lambda b, m: (b * nm + m, 0))
    seq = pl.BlockSpec((ns * Lb, HY_C), lambda b, m: (b, 0))
    return pl.pallas_call(
        functools.partial(_hy_inv_kernel, L=Lb, tm=tm, ns=ns),
        grid=(NB // ns, nm),
        in_specs=[pl.BlockSpec((tm, Lb), lambda b, m: (m, 0)), pl.BlockSpec((tm, Lb), lambda b, m: (m, 0)),
                  seq, seq, tile, tile, pl.BlockSpec((None, 1, HY_C), lambda b, m: (n, 0, 0))],
        out_specs=tile,
        out_shape=jax.ShapeDtypeStruct((T, HY_C), F32),
        compiler_params=_cp("arbitrary", "arbitrary"),
        name="hyena_inv_dft",
    )(cm, smt, yr, yi, z, gate, skip)


def _dft_mats(L):
    k = jnp.arange(L, dtype=jnp.int32)
    blk = 64

    def trig(mult):
        ang = ((mult[:, None] * k[None, :]) % (2 * L)).astype(F32) * (math.pi / L)
        return jnp.cos(ang), jnp.sin(ang)

    ca, sa = trig(jnp.arange(L // blk, dtype=jnp.int32) * blk)
    cb, sb = trig(jnp.arange(blk, dtype=jnp.int32))
    cm = (ca[:, None, :] * cb[None] - sa[:, None, :] * sb[None]).reshape(L, L)
    s = -(sa[:, None, :] * cb[None] + ca[:, None, :] * sb[None]).reshape(L, L)
    alt = jnp.where(k % 2 == 0, 1.0, -1.0).astype(F32)
    sm = jnp.where(k[:, None] == 0, alt[None, :], s)
    smt = jnp.where(k[None, :] == 0, alt[:, None], s)
    return cm.astype(BF16), sm.astype(BF16), smt.astype(BF16)


def _merge_kernel(oa_ref, ob_ref, oc_ref, od_ref, g0_ref, g1_ref, g2_ref, g3_ref, wb_ref, wo_ref, x_ref, mod_ref,
                  lg_ref, lb_ref, rt_ref, *rest):
    x1_ref, hp_ref, logit_ref = rest[-3:]
    acc = None
    for o_ref, g_ref, i in ((oa_ref, g0_ref, 0), (ob_ref, g1_ref, 1), (oc_ref, g2_ref, 2), (od_ref, g3_ref, 3)):
        y = _sigmoid(g_ref[...].astype(F32)) * _dot(o_ref[...].astype(BF16), wb_ref[i])
        acc = y if acc is None else acc + y
    mix = _dot(acc.astype(BF16), wo_ref[...])
    m = mod_ref[...]
    x1 = _layer_norm(DN_ALPHA * x_ref[...] + m[2:3, :] * mix, lg_ref[...], lb_ref[...])
    x1_ref[...] = x1
    h2 = x1 * (1.0 + m[4:5, :]) + m[3:4, :]
    hp_ref[...] = _pack_pairs(h2)
    logit_ref[...] = lax.dot_general(rt_ref[...], h2, (((1,), (1,)), ((), ())), preferred_element_type=F32,
                                     precision=lax.Precision.HIGHEST)


def _merge(outs, gates, wb, wo, x, mod, lg, lb, router_t, mod_row, bm, after=None):
    T = x.shape[0]
    row = lambda w: pl.BlockSpec((bm, w), lambda i: (i, 0))
    gspec = lambda n: pl.BlockSpec((bm, D), lambda i: (i, n))
    fixed2 = lambda s: pl.BlockSpec(s, lambda i: (0, 0))
    dep_specs, dep_args = _after(after)
    return pl.pallas_call(
        _merge_kernel,
        grid=(T // bm,),
        in_specs=[row(256)] * 4 + [gspec(0), gspec(1), gspec(2), gspec(3),
                                   pl.BlockSpec((4, 256, D), lambda i: (0, 0, 0)), fixed2((D, D)), row(D),
                                   pl.BlockSpec((None, 6, D), lambda i: (mod_row(i), 0, 0)),
                                   fixed2((1, D)), fixed2((1, D)), fixed2((N_EXPERTS, D))] + dep_specs,
        out_specs=[row(D), row(D // 2), pl.BlockSpec((N_EXPERTS, bm), lambda i: (0, i))],
        out_shape=[jax.ShapeDtypeStruct((T, D), F32), jax.ShapeDtypeStruct((T, D // 2), jnp.int32),
                   jax.ShapeDtypeStruct((N_EXPERTS, T), F32)],
        compiler_params=_cp("arbitrary"),
        name="merge_norm",
    )(*outs, gates, gates, gates, gates, wb, wo, x, mod, lg, lb, router_t, *dep_args)


def _router_kernel(logit_ref, bias_ref, g_ref, rank_ref, cnt_ref, *, tt):
    per = N_EXPERTS // N_GROUPS
    scores = _sigmoid(logit_ref[...])
    sel = (scores + bias_ref[...]).reshape(N_GROUPS, per, tt)
    gid = lax.broadcasted_iota(jnp.int32, (N_GROUPS, per, tt), 0).astype(F32)
    jid = lax.broadcasted_iota(jnp.int32, (N_GROUPS, per, tt), 1).astype(F32)
    eid = gid * per + jid
    ninf = -jnp.inf
    m1 = sel.max(1, keepdims=True)
    i1 = jnp.where(sel == m1, jid, float(per)).min(1, keepdims=True)
    m2 = jnp.where(jid == i1, ninf, sel).max(1, keepdims=True)
    gs = m1 + m2
    g1 = lax.broadcasted_iota(jnp.int32, (N_GROUPS, 1, tt), 0).astype(F32)
    chosen = jnp.zeros((N_GROUPS, 1, tt), F32)
    for _ in range(TOPK_GROUPS):
        mx = gs.max(0, keepdims=True)
        gi = jnp.where(gs == mx, g1, float(N_GROUPS)).min(0, keepdims=True)
        pick = g1 == gi
        chosen = jnp.where(pick, 1.0, chosen)
        gs = jnp.where(pick, ninf, gs)
    cand = jnp.where(chosen > 0.0, sel, NEG)
    picked = jnp.zeros((N_GROUPS, per, tt), F32)
    for _ in range(TOP_K):
        mx = cand.max(1, keepdims=True).max(0, keepdims=True)
        ei = jnp.where(cand == mx, eid, float(N_EXPERTS)).min(1, keepdims=True).min(0, keepdims=True)
        pick = eid == ei
        picked = jnp.where(pick, 1.0, picked)
        cand = jnp.where(pick, ninf, cand)
    w = scores.reshape(N_GROUPS, per, tt) * picked
    wsum = w.sum(1, keepdims=True).sum(0, keepdims=True)
    g_ref[...] = (w / wsum * ROUTED_SCALE).reshape(N_EXPERTS, tt)
    pk = picked.reshape(N_EXPERTS, tt)
    t_in = lax.broadcasted_iota(jnp.int32, (tt, tt), 0)
    t_out = lax.broadcasted_iota(jnp.int32, (tt, tt), 1)
    upper = jnp.where(t_in <= t_out, 1.0, 0.0).astype(BF16)

    @pl.when(pl.program_id(0) == 0)
    def _():
        cnt_ref[...] = jnp.zeros_like(cnt_ref)

    before = cnt_ref[:, 0:1]
    rank_ref[...] = jnp.where(pk > 0.0, before + _dot(pk.astype(BF16), upper) - 1.0, -1.0)
    cnt_ref[...] += pk.sum(-1, keepdims=True)


def _router(logits_t, bias, tt):
    T = logits_t.shape[1]
    tile = pl.BlockSpec((N_EXPERTS, tt), lambda i: (0, i))
    return pl.pallas_call(
        functools.partial(_router_kernel, tt=tt),
        grid=(T // tt,),
        in_specs=[tile, pl.BlockSpec((N_EXPERTS, 1), lambda i: (0, 0))],
        out_specs=[tile, tile, pl.BlockSpec((N_EXPERTS, 128), lambda i: (0, 0))],
        out_shape=[jax.ShapeDtypeStruct((N_EXPERTS, T), F32), jax.ShapeDtypeStruct((N_EXPERTS, T), F32),
                   jax.ShapeDtypeStruct((N_EXPERTS, 128), F32)],
        compiler_params=_cp("arbitrary"),
        name="moe_router",
    )(logits_t, bias)


def _route_pos_kernel(gate_ref, rank_ref, cnt_ref, pos_ref, w_ref, te_ref, nx_ref, nt_ref, *, tm, nt_max):
    ei = lax.broadcasted_iota(jnp.int32, (N_EXPERTS, N_EXPERTS), 0)
    ej = lax.broadcasted_iota(jnp.int32, (N_EXPERTS, N_EXPERTS), 1)
    below = jnp.where(ej < ei, 1.0, 0.0)
    padded = jnp.ceil(cnt_ref[...] * (1.0 / tm)) * tm
    offs = _dot_hi(below, padded)
    rank = rank_ref[...]
    routed = rank >= 0.0
    pos = offs[:, 0:1] + rank
    slot = _dot(below.astype(BF16), jnp.where(routed, 1.0, 0.0).astype(BF16))
    gate = gate_ref[...]
    for k in range(TOP_K):
        mine = routed & (slot == float(k))
        pos_ref[k:k + 1, :] = jnp.where(mine, pos, 0.0).sum(0, keepdims=True).astype(jnp.int32)
        w_ref[k:k + 1, :] = jnp.where(mine, gate, 0.0).sum(0, keepdims=True)
    ends = (offs + padded)[:, 0:1]
    first = (lax.broadcasted_iota(jnp.int32, (N_EXPERTS, nt_max), 1) * tm).astype(F32)
    te = jnp.minimum(jnp.where(ends <= first, 1.0, 0.0).sum(0, keepdims=True), N_EXPERTS - 1.0)
    te_ref[...] = te.astype(jnp.int32)
    eid = lax.broadcasted_iota(jnp.int32, (N_EXPERTS, nt_max), 0).astype(F32)
    nx_ref[...] = (jnp.where(eid == te, ends, 0.0).sum(0, keepdims=True) * (1.0 / tm)).astype(jnp.int32)
    nt_ref[...] = (padded.sum(0, keepdims=True) * (1.0 / tm)).astype(jnp.int32)


def _route_pos(gate_t, rank, cnt, tt, tm, nt_max):
    T = gate_t.shape[1]
    tile = pl.BlockSpec((N_EXPERTS, tt), lambda i: (0, i))
    out = pl.BlockSpec((TOP_K, tt), lambda i: (0, i))
    return pl.pallas_call(
        functools.partial(_route_pos_kernel, tm=tm, nt_max=nt_max),
        grid=(T // tt,),
        in_specs=[tile, tile, pl.BlockSpec((N_EXPERTS, 128), lambda i: (0, 0))],
        out_specs=[out, out, pl.BlockSpec((1, nt_max), lambda i: (0, 0)), pl.BlockSpec((1, nt_max), lambda i: (0, 0)),
                   pl.BlockSpec((1, 128), lambda i: (0, 0))],
        out_shape=[jax.ShapeDtypeStruct((TOP_K, T), jnp.int32), jax.ShapeDtypeStruct((TOP_K, T), F32),
                   jax.ShapeDtypeStruct((1, nt_max), jnp.int32), jax.ShapeDtypeStruct((1, nt_max), jnp.int32),
                   jax.ShapeDtypeStruct((1, 128), jnp.int32)],
        compiler_params=_cp("arbitrary"),
        name="moe_positions",
    )(gate_t, rank, cnt)


def _gmm_kernel(te_ref, nx_ref, nt_ref, xs_ref, w1_hbm, w3_hbm, w2_hbm, *rest, l):
    ys_ref, b1_ref, b3_ref, b2_ref, f1_ref, f3_ref, f2_ref, seg_ref, sem = rest[-9:]
    j = pl.program_id(0)
    live = j < nt_ref[0]
    new_expert = (j == 0) | (te_ref[j] != te_ref[jnp.maximum(j - 1, 0)])

    def fetch(e, slot):
        return [pltpu.make_async_copy(w_hbm.at[l, e], f_ref.at[slot], sem.at[i, slot])
                for i, (w_hbm, f_ref) in enumerate(((w1_hbm, f1_ref), (w3_hbm, f3_ref), (w2_hbm, f2_ref)))]

    @pl.when(live & new_expert)
    def _():
        @pl.when(j == 0)
        def _():
            seg_ref[0] = 0
            for c in fetch(te_ref[0], 0):
                c.start()

        slot = lax.rem(seg_ref[0], 2)
        for c in fetch(te_ref[j], slot):
            c.wait()
        b1_ref[...] = f1_ref[slot].astype(BF16)
        b3_ref[...] = f3_ref[slot].astype(BF16)
        b2_ref[...] = f2_ref[slot].astype(BF16)
        nxt = nx_ref[j]

        @pl.when(nxt < nt_ref[0])
        def _():
            for c in fetch(te_ref[nxt], 1 - slot):
                c.start()

        seg_ref[0] = seg_ref[0] + 1

    @pl.when(live)
    def _():
        xa, xb = _unpack_pairs(xs_ref[...])
        xa, xb = xa.astype(BF16), xb.astype(BF16)
        half = D // 2
        a = _dot(xa, b1_ref[:half, :]) + _dot(xb, b1_ref[half:, :])
        b = _dot(xa, b3_ref[:half, :]) + _dot(xb, b3_ref[half:, :])
        hid = (a * _sigmoid(a) * b).astype(BF16)
        ys_ref[...] = _pack_pairs(_dot(hid, b2_ref[...]))


def _gmm(te, nx, nt, xs, w1, w3, w2, l, tm, after=None):
    n_slots = xs.shape[0]
    ds = D_EXPERT
    rows = pl.BlockSpec((tm, D // 2), lambda j, te, nx, nt: (jnp.minimum(j, nt[0] - 1), 0))
    hbm = pl.BlockSpec(memory_space=pl.ANY)
    dep_specs, dep_args = _after(after)
    return pl.pallas_call(
        functools.partial(_gmm_kernel, l=l),
        grid_spec=pltpu.PrefetchScalarGridSpec(
            num_scalar_prefetch=3,
            grid=(n_slots // tm,),
            in_specs=[rows, hbm, hbm, hbm] + dep_specs,
            out_specs=rows,
            scratch_shapes=[pltpu.VMEM((D, ds), BF16), pltpu.VMEM((D, ds), BF16), pltpu.VMEM((ds, D), BF16),
                            pltpu.VMEM((2, D, ds), F32), pltpu.VMEM((2, D, ds), F32), pltpu.VMEM((2, ds, D), F32),
                            pltpu.SMEM((1,), jnp.int32), pltpu.SemaphoreType.DMA((3, 2))]),
        out_shape=jax.ShapeDtypeStruct((n_slots, D // 2), jnp.int32),
        compiler_params=_cp("arbitrary"),
        name="moe_grouped_ffn",
    )(te, nx, nt, xs, w1, w3, w2, *dep_args)


def _combine_kernel(yk_ref, w_ref, hp_ref, s1_ref, s3_ref, s2_ref, x_ref, mod_ref, lg_ref, lb_ref, o_ref):
    w = w_ref[...]
    acc_a = acc_b = None
    for k in range(TOP_K):
        ya, yb = _unpack_pairs(yk_ref[k])
        wk = w[:, k:k + 1]
        acc_a = wk * ya if acc_a is None else acc_a + wk * ya
        acc_b = wk * yb if acc_b is None else acc_b + wk * yb
    ha, hb = _unpack_pairs(hp_ref[...])
    ha, hb = ha.astype(BF16), hb.astype(BF16)
    half = D // 2
    a = _dot(ha, s1_ref[:half, :]) + _dot(hb, s1_ref[half:, :])
    b = _dot(ha, s3_ref[:half, :]) + _dot(hb, s3_ref[half:, :])
    y = jnp.concatenate([acc_a, acc_b], axis=1) + _dot((a * _sigmoid(a) * b).astype(BF16), s2_ref[...])
    m = mod_ref[...]
    o_ref[...] = _layer_norm(DN_ALPHA * x_ref[...] + m[5:6, :] * y, lg_ref[...], lb_ref[...])


def _combine(yk, w, hp, s1, s3, s2, x1, mod, lg, lb, mod_row, bm):
    T = x1.shape[0]
    ds = D_EXPERT
    row = lambda n: pl.BlockSpec((bm, n), lambda i: (i, 0))
    fixed = lambda s: pl.BlockSpec(s, lambda i: (0, 0))
    return pl.pallas_call(
        _combine_kernel,
        grid=(T // bm,),
        in_specs=[pl.BlockSpec((TOP_K, bm, D // 2), lambda i: (0, i, 0)), row(TOP_K), row(D // 2),
                  fixed((D, ds)), fixed((D, ds)), fixed((ds, D)), row(D),
                  pl.BlockSpec((None, 6, D), lambda i: (mod_row(i), 0, 0)), fixed((1, D)), fixed((1, D))],
        out_specs=row(D),
        out_shape=jax.ShapeDtypeStruct((T, D), F32),
        compiler_params=_cp("arbitrary"),
        name="moe_combine_norm",
    )(yk, w, hp, s1, s3, s2, x1, mod, lg, lb)


def _sc_worker():
    return lax.axis_index("s") * SC_CORES + lax.axis_index("c")


def _sc_mesh():
    return plsc.VectorSubcoreMesh(core_axis_name="c", subcore_axis_name="s")


def _sc_gather(table, idx):
    N, W = idx.shape[0], table.shape[1]
    per_w = N // SC_WORKERS
    n_chunks = per_w // SC_ROWS

    def body(table_hbm, idx_hbm, out_hbm, idx_v, rows_v, sem):
        base = _sc_worker() * per_w
        pltpu.sync_copy(idx_hbm.at[pl.ds(base, per_w)], idx_v)

        @pl.loop(0, n_chunks)
        def _(c):
            off = pl.multiple_of(c * SC_ROWS, SC_ROWS)
            pltpu.async_copy(table_hbm.at[idx_v.at[pl.ds(off, SC_ROWS)]], rows_v, sem).wait()
            pltpu.sync_copy(rows_v, out_hbm.at[pl.ds(base + off, SC_ROWS)])

    return pl.kernel(
        body, out_type=jax.ShapeDtypeStruct((N, W), table.dtype), mesh=_sc_mesh(),
        scratch_types=[pltpu.VMEM((per_w,), jnp.int32), pltpu.VMEM((SC_ROWS, W), table.dtype),
                       pltpu.SemaphoreType.DMA],
        name="sc_gather",
    )(table, idx)


def _sc_dispatch(pos, table, n_slots):
    NP, (T, W) = pos.shape[0], table.shape
    per_w = n_slots // SC_WORKERS
    n_chunks = per_w // SC_ROWS
    scan = 8192

    def body(pos_hbm, table_hbm, out_hbm, pos_v, src_v, rows_v, sem):
        base = _sc_worker() * per_w
        lane = lax.iota(jnp.int32, SC_LANES)

        @pl.loop(0, per_w // SC_LANES)
        def _(j):
            o = pl.multiple_of(j * SC_LANES, SC_LANES)
            src_v[pl.ds(o, SC_LANES)] = (base + o + lane) & (T - 1)

        @pl.loop(0, NP // scan)
        def _(c):
            pltpu.sync_copy(pos_hbm.at[pl.ds(pl.multiple_of(c * scan, scan), scan)], pos_v)

            @pl.loop(0, scan // SC_LANES)
            def _(v):
                o = pl.multiple_of(v * SC_LANES, SC_LANES)
                p = pos_v[pl.ds(o, SC_LANES)] - base
                mine = (p >= 0) & (p < per_w)
                tok = (c * scan + o + lane) & (T - 1)
                plsc.store_scatter(src_v, [jnp.where(mine, p, 0)], tok, mask=mine)

        @pl.loop(0, n_chunks)
        def _(c):
            off = pl.multiple_of(c * SC_ROWS, SC_ROWS)
            pltpu.async_copy(table_hbm.at[src_v.at[pl.ds(off, SC_ROWS)]], rows_v, sem).wait()
            pltpu.sync_copy(rows_v, out_hbm.at[pl.ds(base + off, SC_ROWS)])

    return pl.kernel(
        body, out_type=jax.ShapeDtypeStruct((n_slots, W), table.dtype), mesh=_sc_mesh(),
        scratch_types=[pltpu.VMEM((scan,), jnp.int32), pltpu.VMEM((per_w,), jnp.int32),
                       pltpu.VMEM((SC_ROWS, W), table.dtype), pltpu.SemaphoreType.DMA],
        compiler_params=pltpu.CompilerParams(needs_layout_passes=False),
        name="sc_dispatch",
    )(pos, table)


def _caches_kernel(*refs, nb, S):
    n_in = 6 * DEPTH
    outs = refs[n_in:]
    l = pl.program_id(0)
    for a in range(DEPTH):
        @pl.when(l == a)
        def _(a=a):
            ckv, kpe, wk, wv, nk, nv = refs[6 * a:6 * (a + 1)]
            for g in range(nb):
                rows = slice(g * S, (g + 1) * S)
                outs[0][g] = ckv[rows, :]
                outs[1][g] = kpe[rows, 64:96]
                outs[2][g] = wk[rows, :]
                outs[3][g] = wv[rows, :]
                outs[4][g] = nk[rows, :]
                outs[5][g] = nv[rows, :]


def _emit_caches(projs, ckvs, B, S):
    nb = 4
    while B % nb:
        nb //= 2

    def layer_specs(a):
        row = lambda l, b: jnp.where(l == a, b, 0)
        col = lambda w, off: pl.BlockSpec((nb * S, w), lambda l, b: (row(l, b), off // w))
        return [pl.BlockSpec((nb * S, 128), lambda l, b: (row(l, b), 0)), col(128, P_KPE), col(128, P_WK),
                col(128, P_WV), col(256, P_NK), col(256, P_NV)]

    in_specs, args = [], []
    for a in range(DEPTH):
        in_specs += layer_specs(a)
        args += [ckvs[a]] + [projs[a]] * 5
    widths = (128, 32, 128, 128, 256, 256)
    return pl.pallas_call(
        functools.partial(_caches_kernel, nb=nb, S=S),
        grid=(DEPTH, B // nb),
        in_specs=in_specs,
        out_specs=[pl.BlockSpec((nb, None, S, w), lambda l, b: (b, l, 0, 0)) for w in widths],
        out_shape=[jax.ShapeDtypeStruct((B, DEPTH, S, w), F32) for w in widths],
        compiler_params=_cp("arbitrary", "arbitrary"),
        name="context_tensors",
    )(*args)


def _rot_cols(w, q):
    a, b, c, d = w[..., :q], w[..., q:2 * q], w[..., 2 * q:3 * q], w[..., 3 * q:]
    return jnp.concatenate([-b, a, -d, c], -1)


def _prep_w_in(w):
    z = lambda n: jnp.zeros((D, n), w.dtype)
    qlat, ckv, kpe, hy = w[:, 0:256], w[:, 256:384], w[:, 384:416], w[:, 416:1184]
    wq, wk, wv = w[:, 1184:1440], w[:, 1440:1568], w[:, 1568:1696]
    nq, nk, nv, gate = w[:, 1696:1952], w[:, 1952:2208], w[:, 2208:2464], w[:, 2464:]
    wq_r = _rot_cols(wq.reshape(D, 4, 64), 16).reshape(D, 256)
    wk_r = _rot_cols(wk.reshape(D, 2, 64), 16).reshape(D, 128)
    kpe_r = _rot_cols(kpe, 8)
    cols = [qlat, ckv, z(64), kpe, z(32), hy, wq, wk, wv, nq, nk, nv, wq_r, wk_r, z(64), kpe_r, z(32), gate]
    return jnp.concatenate(cols, 1).astype(BF16)


def _prep_mla(w_uq, w_ukv):
    uq = w_uq.reshape(256, 4, 96)
    nope, pe = uq[..., :64], uq[..., 64:]
    z32 = jnp.zeros((256, 4, 32), w_uq.dtype)
    z64 = jnp.zeros((256, 4, 64), w_uq.dtype)
    wcat = jnp.concatenate([nope, pe, z32], -1).reshape(256, 512).astype(BF16)
    wrot = jnp.concatenate([z64, _rot_cols(pe, 8), z32], -1).reshape(256, 512).astype(BF16)
    ukv = w_ukv.reshape(128, 4, 128)
    wk = jnp.concatenate([ukv[..., :64], jnp.zeros((128, 4, 64), w_ukv.dtype)], -1).reshape(128, 512).astype(BF16)
    wv = ukv[..., 64:].reshape(128, 256).astype(BF16)
    return wcat, wrot, wk, wv


def _rope_tab(L, q):
    t = jnp.arange(L)
    inv = ROPE_BASE ** (-jnp.arange(q, dtype=F32) / q)
    ar = (t // GRID_W).astype(F32)[:, None] * inv[None, :]
    ac = (t % GRID_W).astype(F32)[:, None] * inv[None, :]
    cos = jnp.concatenate([jnp.cos(ar), jnp.cos(ar), jnp.cos(ac), jnp.cos(ac)], 1)
    sin = jnp.concatenate([jnp.sin(ar), jnp.sin(ar), jnp.sin(ac), jnp.sin(ac)], 1)
    return cos, sin


def _rope_tables(L):
    c8, s8 = _rope_tab(L, 8)
    c16, s16 = _rope_tab(L, 16)
    one, zero = jnp.ones((L, 64), F32), jnp.zeros((L, 64), F32)
    z32 = jnp.zeros((L, 32), F32)
    mla_q = (jnp.tile(jnp.concatenate([one, c8, z32], 1), (1, 4)), jnp.tile(jnp.concatenate([zero, s8, z32], 1), (1, 4)))
    mla_k = (jnp.concatenate([zero, c8, z32], 1), jnp.concatenate([zero, s8, z32], 1))
    win = (jnp.tile(c16, (1, 4)), jnp.tile(s16, (1, 4)), jnp.tile(c16, (1, 2)), jnp.tile(s16, (1, 2)))
    return mla_q + mla_k, win


def _hyena(proj, lp, dft, NB, Lb):
    cm, sm, smt = dft
    tm = min(Lb, 512)
    v, x1, x2 = _short_conv(proj, lp["hy_conv_w"], lp["hy_conv_b"].reshape(1, -1), NB, Lb)
    w1p = jnp.pad(lp["hy_w1"], ((0, 128 - lp["hy_w1"].shape[0]), (0, 0)))
    fs, nyq = _hy_filter(Lb, w1p, lp["hy_b1"].reshape(1, -1), lp["hy_w2"], lp["hy_b2"].reshape(1, -1), lp["hy_w3"],
                         lp["hy_sin_freq"], lp["hy_log_decay"].reshape(1, -1))
    gr, gi = _hy_gdft(cm, sm, fs, nyq, Lb, tm)
    skip = lp["hy_skip"].reshape(2, 1, HY_C)
    z = v
    for n, gate in enumerate((x1, x2)):
        yr, yi = _hy_fwd(cm, sm, z, gr, gi, n, NB, Lb, tm)
        z = _hy_inv(cm, smt, yr, yi, z, gate, skip, n, NB, Lb, tm)
    return z


def _layer_steps(x, mod, lp, l, NB, Lb, mod_row_of_batch, dft, cache=None, tabs=None, na_bias=None):
    T = NB * Lb
    latent = cache is not None
    bm = 256
    rows_of = lambda n: (lambda i: mod_row_of_batch((i * n) // Lb))
    mod_row = rows_of(bm)
    span = Lb if latent else T
    bmp = min(span, 1024)
    proj, gates = _in_proj(x, mod, lp["w_in_p"], rows_of(bmp), bmp)

    gq, gkv = lp["mla_q_norm"].reshape(1, -1), lp["mla_kv_norm"].reshape(1, -1)
    wcat, wrot, wk, wv = lp["mla_w"]
    q_all, ckv_n, kpe_r = _mla_q(proj, gq, gkv, wcat, wrot, tabs[0] if latent else None, Lb, min(span, 512))
    if latent:
        ckv_c, kpe_c, kc_c, vc_c, kd_c, vd_c = cache
        Lc = ckv_c.shape[1]
        kpe_cp = jnp.pad(kpe_c, ((0, 0), (0, 0), (64, 32)))
        ckv_all = jnp.concatenate([ckv_c, ckv_n.reshape(NB, Lb, 128)], 1).reshape(NB * (Lc + Lb), 128)
        kpe_all = jnp.concatenate([kpe_cp, kpe_r.reshape(NB, Lb, 128)], 1).reshape(NB * (Lc + Lb), 128)
        k_all, v_all = _mla_kv(ckv_all, kpe_all, wk, wv, 512)
        oc = _lat_win_attention(proj, kc_c.reshape(NB, Lc, 128), vc_c.reshape(NB, Lc, 128), tabs[1],
                                lp["win_sink"], NB, Lb)
        od = _lat_na_attention(proj, kd_c.reshape(NB, Lc, 256), vd_c.reshape(NB, Lc, 256), na_bias, NB, Lb)
        ob = _hyena(proj, lp, dft, NB, Lb)
        after = yield "projected", od
        oa = _lat_mla_attention(q_all, k_all, v_all, NB, Lb, Lc + Lb, 256, after=after)
        after = None
    else:
        k_all, v_all = _mla_kv(ckv_n, kpe_r, wk, wv, 512)
        oa, oc, od = _ctx_attention(proj, q_all, k_all, v_all, lp["win_sink"], NB, Lb)
        ob = _hyena(proj, lp, dft, NB, Lb)
        after = yield "projected", oa

    bmm = min(span, 512)
    x1, hp, logits_t = _merge((oa, ob, oc, od), gates, lp["w_branch_b"], lp["w_out_b"], x, mod,
                              lp["ln1_g"].reshape(1, -1), lp["ln1_b"].reshape(1, -1), lp["moe_router"].T,
                              rows_of(bmm), bmm, after=after)
    n_slots = T * TOP_K + N_EXPERTS * MOE_TM
    gate_t, rank, cnt = _router(logits_t, lp["moe_bias"].reshape(-1, 1), 512)
    pos, w8, te, nx, nt = _route_pos(gate_t, rank, cnt, 512, MOE_TM, n_slots // MOE_TM)
    xs = _sc_dispatch(pos.reshape(-1), hp, n_slots)
    after = yield "dispatched", None
    ys = _gmm(te.reshape(-1), nx.reshape(-1), nt.reshape(-1)[:1], xs, lp["moe_w1"], lp["moe_w3"], lp["moe_w2"], l,
              MOE_TM, after=after)
    yield "ffn", ys
    yk = _sc_gather(ys, pos.reshape(-1)).reshape(TOP_K, T, D // 2)
    x2 = _combine(yk, w8.T, hp, lp["sh_w1_b"], lp["sh_w3_b"], lp["sh_w2_b"], x1, mod,
                  lp["ln2_g"].reshape(1, -1), lp["ln2_b"].reshape(1, -1), mod_row, bm)
    yield "done", (x2, proj, ckv_n)


def kernel(x_prompt, x_sample, cache_mla_ckv, cache_mla_kpe, cache_win_k, cache_win_v, cache_na_k, cache_na_v, c, c_ctx, w_ada, b_ada, w_in, mla_q_norm, mla_kv_norm, mla_w_uq, mla_w_ukv, hy_conv_w, hy_conv_b, hy_w1, hy_b1, hy_w2, hy_b2, hy_w3, hy_sin_freq, hy_log_decay, hy_skip, win_sink, na_rpb, w_branch, w_out, ln1_g, ln1_b, ln2_g, ln2_b, moe_router, moe_bias, moe_w1, moe_w3, moe_w2, sh_w1, sh_w3, sh_w2):
    B, S, _ = x_prompt.shape
    DB, DS, _ = x_sample.shape
    xp = x_prompt.reshape(B * S, D)
    xs = x_sample.reshape(DB * DS, D)
    cvec = jnp.concatenate([c_ctx[None, :], c, jnp.zeros((8 - 1 - DB, D), F32)], 0)
    dft_ctx = _dft_mats(S)
    dft_lat = _dft_mats(DS)
    tabs = _rope_tables(DS)
    projs, ckvs = [], []

    def params(l):
        return dict(w_in_p=_prep_w_in(w_in[l]), mla_q_norm=mla_q_norm[l], mla_kv_norm=mla_kv_norm[l],
                    mla_w=_prep_mla(mla_w_uq[l], mla_w_ukv[l]), hy_conv_w=hy_conv_w[l], hy_conv_b=hy_conv_b[l],
                    hy_w1=hy_w1[l], hy_b1=hy_b1[l], hy_w2=hy_w2[l], hy_b2=hy_b2[l], hy_w3=hy_w3[l],
                    hy_sin_freq=hy_sin_freq[l], hy_log_decay=hy_log_decay[l], hy_skip=hy_skip[l],
                    win_sink=win_sink[l], w_branch_b=w_branch[l].astype(BF16), w_out_b=w_out[l].astype(BF16),
                    ln1_g=ln1_g[l], ln1_b=ln1_b[l], ln2_g=ln2_g[l], ln2_b=ln2_b[l],
                    moe_router=moe_router[l], moe_bias=moe_bias[l], moe_w1=moe_w1, moe_w3=moe_w3, moe_w2=moe_w2,
                    sh_w1_b=sh_w1[l].astype(BF16), sh_w3_b=sh_w3[l].astype(BF16), sh_w2_b=sh_w2[l].astype(BF16))

    lps = [params(l) for l in range(DEPTH)]
    mods = [_modulation(cvec, w_ada, b_ada, l) for l in range(DEPTH)]

    def ctx_layer(l, x):
        return _layer_steps(x, mods[l], lps[l], l, B, S, lambda b: 0, dft_ctx)

    def lat_layer(l, x):
        cache = (cache_mla_ckv[:, l], cache_mla_kpe[:, l], cache_win_k[:, l], cache_win_v[:, l],
                 cache_na_k[:, l], cache_na_v[:, l])
        return _layer_steps(x, mods[l], lps[l], l, DB, DS, lambda b: 1 + b, dft_lat, cache=cache, tabs=tabs,
                            na_bias=_na_bias(na_rpb[l]))

    ctx = ctx_layer(0, xp)
    next(ctx)
    ctx.send(None)
    for l in range(DEPTH):
        lat = lat_layer(l, xs)
        lat_local = next(lat)[1]
        ys_ctx = ctx.send(lat_local)[1]
        lat.send(ys_ctx)
        xp, proj, ckv_n = ctx.send(None)[1]
        projs.append(proj)
        ckvs.append(ckv_n)
        if l + 1 < DEPTH:
            ctx = ctx_layer(l + 1, xp)
            attended = next(ctx)[1]
            ys_lat = lat.send(attended)[1]
            ctx.send(ys_lat)
        else:
            ckv, kpe, wk, wv, nk, nv = _emit_caches(projs, ckvs, B, S)
            lat.send((xp, nv))
        xs = lat.send(None)[1][0]
    heads = lambda t, h: t.reshape(B, DEPTH, S, h, HEAD_DIM)
    return (xp.reshape(B, S, D), xs.reshape(DB, DS, D), ckv, kpe, heads(wk, 2), heads(wv, 2), heads(nk, 4),
            heads(nv, 4))
```

```python
import functools
import math

import jax
import jax.numpy as jnp
from jax import lax
from jax.experimental import pallas as pl
from jax.experimental.pallas import tpu as pltpu
from jax.experimental.pallas import tpu_sc as plsc

F32 = jnp.float32
BF16 = jnp.bfloat16

D = 1024
DEPTH = 2
GRID_W = 64
HEAD_DIM = 64
MLA_SCALE = 96 ** -0.5
ATT_SCALE = HEAD_DIM ** -0.5
HY_C = 256
HY_BANDS = 8
NA_KH = 8
NA_KW = 16
N_EXPERTS = 64
N_GROUPS = 8
TOP_K = 8
TOPK_GROUPS = 4
D_EXPERT = 256
ROUTED_SCALE = 2.5
ROPE_BASE = 10000.0
LN_EPS = 1e-5
RMS_EPS = 1e-6
NEG = -1e30
DN_ALPHA = (2 * DEPTH) ** 0.25

P_QLAT, P_CKV, P_KPE, P_HY = 0, 256, 384, 512
P_WQ, P_WK, P_WV = 1280, 1536, 1664
P_NQ, P_NK, P_NV = 1792, 2048, 2304
P_WQR, P_WKR, P_KPER, P_GATE = 2560, 2816, 2944, 3072
N_PROJ = 7168

VMEM_LIMIT = 56 * 1024 * 1024

SC_CORES = 2
SC_SUBCORES = 16
SC_LANES = 16
SC_WORKERS = SC_CORES * SC_SUBCORES
SC_ROWS = 64

MOE_TM = 512

def _cp(*sem):
    return pltpu.CompilerParams(dimension_semantics=sem, vmem_limit_bytes=VMEM_LIMIT)


def _sigmoid(x):
    return 1.0 / (1.0 + jnp.exp(-x))


def _dot(a, b):
    return jnp.dot(a, b, preferred_element_type=F32)


def _dot_nt(a, b):
    return lax.dot_general(a, b, (((1,), (1,)), ((), ())), preferred_element_type=F32)


def _dot_hi(a, b):
    return jnp.dot(a, b, preferred_element_type=F32, precision=lax.Precision.HIGHEST)


def _pack_pairs(x):
    w = x.shape[1] // 2
    hi = lax.bitcast_convert_type(x[:, :w].astype(BF16).astype(F32), jnp.int32)
    lo = lax.bitcast_convert_type(x[:, w:].astype(BF16).astype(F32), jnp.int32)
    return hi | lax.shift_right_logical(lo, 16)


def _unpack_pairs(p):
    hi = lax.bitcast_convert_type(p & jnp.int32(-65536), F32)
    lo = lax.bitcast_convert_type(lax.shift_left(p, 16), F32)
    return hi, lo


def _layer_norm(x, g, b):
    mu = jnp.mean(x, -1, keepdims=True)
    xc = x - mu
    var = jnp.mean(xc * xc, -1, keepdims=True)
    return xc * lax.rsqrt(var + LN_EPS) * g + b


def _rms_norm(x, g):
    return x * lax.rsqrt(jnp.mean(x * x, -1, keepdims=True) + RMS_EPS) * g


def _mod_kernel(c_ref, w_ref, b_ref, o_ref):
    c = c_ref[...]
    a = (c * _sigmoid(c)).astype(BF16)
    o_ref[...] = _dot(a, w_ref[...].astype(BF16)) + b_ref[...]


def _modulation(cvec, w_ada, b_ada, l):
    out = pl.pallas_call(
        _mod_kernel,
        grid=(6,),
        in_specs=[pl.BlockSpec((8, D), lambda j: (0, 0)),
                  pl.BlockSpec((None, D, D), lambda j: (l, 0, j)),
                  pl.BlockSpec((None, 1, D), lambda j: (l, 0, j))],
        out_specs=pl.BlockSpec((8, D), lambda j: (0, j)),
        out_shape=jax.ShapeDtypeStruct((8, 6 * D), F32),
        compiler_params=_cp("arbitrary"),
        name="modulation",
    )(cvec, w_ada, b_ada.reshape(DEPTH, 1, 6 * D))
    return out.reshape(8, 6, D)


def _inproj_kernel(x_ref, mod_ref, w_ref, o_ref, g_ref, h_ref, *, n_main):
    j = pl.program_id(1)

    @pl.when(j == 0)
    def _():
        m = mod_ref[...]
        h_ref[...] = (x_ref[...] * (1.0 + m[1:2, :]) + m[0:1, :]).astype(BF16)

    y = _dot(h_ref[...], w_ref[...])

    @pl.when(j < n_main)
    def _():
        o_ref[...] = y

    @pl.when(j >= n_main)
    def _():
        g_ref[...] = y.astype(BF16)


def _in_proj(x, mod, w_p, mod_row, bm, bn=1024):
    T = x.shape[0]
    n_main = P_GATE // bn
    return pl.pallas_call(
        functools.partial(_inproj_kernel, n_main=n_main),
        grid=(T // bm, N_PROJ // bn),
        in_specs=[pl.BlockSpec((bm, D), lambda i, j: (i, 0)),
                  pl.BlockSpec((None, 6, D), lambda i, j: (mod_row(i), 0, 0)),
                  pl.BlockSpec((D, bn), lambda i, j: (0, j))],
        out_specs=[pl.BlockSpec((bm, bn), lambda i, j: (i, jnp.minimum(j, n_main - 1))),
                   pl.BlockSpec((bm, bn), lambda i, j: (i, jnp.maximum(j - n_main, 0)))],
        out_shape=[jax.ShapeDtypeStruct((T, P_GATE), F32), jax.ShapeDtypeStruct((T, N_PROJ - P_GATE), BF16)],
        scratch_shapes=[pltpu.VMEM((bm, D), BF16)],
        compiler_params=_cp("arbitrary", "arbitrary"),
        name="in_proj",
    )(x, mod, w_p)


def _mla_q_kernel(*refs, rope):
    if rope:
        (ql_ref, ckv_ref, kpe_ref, kper_ref, gq_ref, gkv_ref, wc_ref, wr_ref,
         cq_ref, sq_ref, ck_ref, sk_ref, q_ref, ckvn_ref, kpeo_ref) = refs
    else:
        ql_ref, ckv_ref, kpe_ref, gq_ref, gkv_ref, wc_ref, q_ref, ckvn_ref, kpeo_ref = refs
    qn = _rms_norm(ql_ref[...], gq_ref[...]).astype(BF16)
    q = _dot(qn, wc_ref[...])
    if rope:
        q = q * cq_ref[...] + _dot(qn, wr_ref[...]) * sq_ref[...]
        kpeo_ref[...] = kpe_ref[...] * ck_ref[...] + kper_ref[...] * sk_ref[...]
    else:
        kpeo_ref[...] = kpe_ref[...]
    q_ref[...] = (q * MLA_SCALE).astype(BF16)
    ckvn_ref[...] = _rms_norm(ckv_ref[...], gkv_ref[...])


def _mla_q(proj, gq, gkv, wcat, wrot, tabs, Lb, bm):
    T = proj.shape[0]
    rope = tabs is not None
    nl = Lb // bm
    col = lambda c: (lambda i: (i, c))
    fixed = lambda i: (0, 0)
    in_specs = [pl.BlockSpec((bm, 256), col(P_QLAT // 256)),
                pl.BlockSpec((bm, 128), col(P_CKV // 128)),
                pl.BlockSpec((bm, 128), col(P_KPE // 128))]
    args = [proj, proj, proj]
    if rope:
        in_specs.append(pl.BlockSpec((bm, 128), col(P_KPER // 128)))
        args.append(proj)
    in_specs += [pl.BlockSpec((1, 256), fixed), pl.BlockSpec((1, 128), fixed), pl.BlockSpec((256, 512), fixed)]
    args += [gq, gkv, wcat]
    if rope:
        cq, sq, ck, sk = tabs
        pos = lambda i: (i % nl, 0)
        in_specs += [pl.BlockSpec((256, 512), fixed), pl.BlockSpec((bm, 512), pos), pl.BlockSpec((bm, 512), pos),
                     pl.BlockSpec((bm, 128), pos), pl.BlockSpec((bm, 128), pos)]
        args += [wrot, cq, sq, ck, sk]
    return pl.pallas_call(
        functools.partial(_mla_q_kernel, rope=rope),
        grid=(T // bm,),
        in_specs=in_specs,
        out_specs=[pl.BlockSpec((bm, 512), lambda i: (i, 0)),
                   pl.BlockSpec((bm, 128), lambda i: (i, 0)),
                   pl.BlockSpec((bm, 128), lambda i: (i, 0))],
        out_shape=[jax.ShapeDtypeStruct((T, 512), BF16),
                   jax.ShapeDtypeStruct((T, 128), F32),
                   jax.ShapeDtypeStruct((T, 128), F32)],
        compiler_params=_cp("arbitrary"),
        name="mla_q",
    )(*args)


def _mla_kv_kernel(ckv_ref, kpe_ref, wk_ref, wv_ref, k_ref, v_ref):
    c = ckv_ref[...].astype(BF16)
    kpe = kpe_ref[...]
    k_ref[...] = (_dot(c, wk_ref[...]) + jnp.concatenate([kpe] * 4, axis=1)).astype(BF16)
    v_ref[...] = _dot(c, wv_ref[...]).astype(BF16)


def _mla_kv(ckv, kpe, wk, wv, bm):
    Tk = ckv.shape[0]
    return pl.pallas_call(
        _mla_kv_kernel,
        grid=(Tk // bm,),
        in_specs=[pl.BlockSpec((bm, 128), lambda i: (i, 0)),
                  pl.BlockSpec((bm, 128), lambda i: (i, 0)),
                  pl.BlockSpec((128, 512), lambda i: (0, 0)),
                  pl.BlockSpec((128, 256), lambda i: (0, 0))],
        out_specs=[pl.BlockSpec((bm, 512), lambda i: (i, 0)),
                   pl.BlockSpec((bm, 256), lambda i: (i, 0))],
        out_shape=[jax.ShapeDtypeStruct((Tk, 512), BF16),
                   jax.ShapeDtypeStruct((Tk, 256), BF16)],
        compiler_params=_cp("arbitrary"),
        name="mla_kv",
    )(ckv, kpe, wk, wv)


def _attn_core(q, kvs, masks, sink):
    ss = []
    for (k, _), mk in zip(kvs, masks):
        s = _dot_nt(q, k)
        if mk is not None:
            s = s + mk[1] if mk[0] == "add" else jnp.where(mk[1], s, NEG)
        ss.append(s)
    m = ss[0].max(-1, keepdims=True)
    for s in ss[1:]:
        m = jnp.maximum(m, s.max(-1, keepdims=True))
    if sink is not None:
        m = jnp.maximum(m, sink)
    den = None
    acc = None
    for s, (_, v) in zip(ss, kvs):
        p = jnp.exp(s - m)
        d = p.sum(-1, keepdims=True)
        a = _dot(p.astype(BF16), v)
        den = d if den is None else den + d
        acc = a if acc is None else acc + a
    if sink is not None:
        den = den + jnp.exp(sink - m)
    return acc / den


def _ctx_attn_kernel(qm_ref, km_ref, vm_ref, wq_ref, wk_ref, wv_ref, nq_ref, nk_ref, nv_ref, sink_ref, *rest):
    om_ref, ow_ref, on_ref = rest[-3:]
    for h in range(4):
        q = qm_ref[:, 128 * h:128 * (h + 1)]
        k = km_ref[:, 128 * h:128 * (h + 1)]
        v = vm_ref[:, 64 * h:64 * (h + 1)]
        om_ref[:, 64 * h:64 * (h + 1)] = _attn_core(q, [(k, v)], [None], None)
    for h in range(4):
        g = h // 2
        q = (wq_ref[:, 64 * h:64 * (h + 1)] * ATT_SCALE).astype(BF16)
        k = wk_ref[:, 64 * g:64 * (g + 1)].astype(BF16)
        v = wv_ref[:, 64 * g:64 * (g + 1)].astype(BF16)
        ow_ref[:, 64 * h:64 * (h + 1)] = _attn_core(q, [(k, v)], [None], sink_ref[h])
    for h in range(4):
        q = (nq_ref[:, 64 * h:64 * (h + 1)] * ATT_SCALE).astype(BF16)
        k = nk_ref[:, 64 * h:64 * (h + 1)].astype(BF16)
        v = nv_ref[:, 64 * h:64 * (h + 1)].astype(BF16)
        on_ref[:, 64 * h:64 * (h + 1)] = _attn_core(q, [(k, v)], [None], None)


def _ctx_attention(proj, q_all, k_all, v_all, sink, NB, Lb, after=None):
    T = proj.shape[0]
    pc = lambda w, off: pl.BlockSpec((Lb, w), lambda b: (b, off // w))
    row = lambda w: pl.BlockSpec((Lb, w), lambda b: (b, 0))
    dep_specs, dep_args = _after(after)
    return pl.pallas_call(
        _ctx_attn_kernel,
        grid=(NB,),
        in_specs=[row(512), row(512), row(256),
                  pc(256, P_WQ), pc(128, P_WK), pc(128, P_WV),
                  pc(256, P_NQ), pc(256, P_NK), pc(256, P_NV),
                  pl.BlockSpec(memory_space=pltpu.SMEM)] + dep_specs,
        out_specs=[row(256), row(256), row(256)],
        out_shape=[jax.ShapeDtypeStruct((T, 256), F32)] * 3,
        compiler_params=_cp("arbitrary"),
        name="ctx_attention",
    )(q_all, k_all, v_all, proj, proj, proj, proj, proj, proj, sink, *dep_args)


def _lat_mla_kernel(q_ref, k_ref, v_ref, *rest):
    o_ref = rest[-1]
    for h in range(4):
        q = q_ref[:, 128 * h:128 * (h + 1)]
        k = k_ref[:, 128 * h:128 * (h + 1)]
        v = v_ref[:, 64 * h:64 * (h + 1)]
        o_ref[:, 64 * h:64 * (h + 1)] = _attn_core(q, [(k, v)], [None], None)


def _after(after):
    deps = [] if after is None else list(after) if isinstance(after, (tuple, list)) else [after]
    return [pl.BlockSpec(memory_space=pl.ANY)] * len(deps), deps


def _lat_mla_attention(q_all, k_all, v_all, NB, Lb, Lk, tq, after=None):
    T = q_all.shape[0]
    nq = Lb // tq
    dep_specs, dep_args = _after(after)
    return pl.pallas_call(
        _lat_mla_kernel,
        grid=(NB, nq),
        in_specs=[pl.BlockSpec((tq, 512), lambda b, i: (b * nq + i, 0)),
                  pl.BlockSpec((Lk, 512), lambda b, i: (b, 0)),
                  pl.BlockSpec((Lk, 256), lambda b, i: (b, 0))] + dep_specs,
        out_specs=pl.BlockSpec((tq, 256), lambda b, i: (b * nq + i, 0)),
        out_shape=jax.ShapeDtypeStruct((T, 256), F32),
        compiler_params=_cp("arbitrary", "arbitrary"),
        name="lat_mla_attention",
    )(q_all, k_all, v_all, *dep_args)


def _attn_local_ctx(q, locs, kc, vc, sink):
    s_ctx = _dot_nt(q, kc)
    m_ctx = s_ctx.max(-1, keepdims=True)
    if sink is not None:
        m_ctx = jnp.maximum(m_ctx, sink)
    ms, dens, accs = [], [], []
    for rs, k, v, mk in locs:
        s = _dot_nt(q[rs], k)
        s = s + mk[1] if mk[0] == "add" else jnp.where(mk[1], s, NEG)
        m = jnp.maximum(s.max(-1, keepdims=True), m_ctx[rs])
        p = jnp.exp(s - m)
        ms.append(m)
        dens.append(p.sum(-1, keepdims=True))
        accs.append(_dot(p.astype(BF16), v))
    m = jnp.concatenate(ms, axis=0)
    p = jnp.exp(s_ctx - m)
    den = jnp.concatenate(dens, axis=0) + p.sum(-1, keepdims=True)
    if sink is not None:
        den = den + jnp.exp(sink - m)
    return (jnp.concatenate(accs, axis=0) + _dot(p.astype(BF16), vc)) / den


def _lat_win_kernel(q_ref, qr_ref, k_ref, kr_ref, v_ref, kc_ref, vc_ref, cq_ref, sq_ref, ck_ref, sk_ref,
                    sink_ref, o_ref, *, Lb, bpt):
    t = pl.program_id(1)
    q = (q_ref[...] * cq_ref[...] + qr_ref[...] * sq_ref[...]) * ATT_SCALE
    kc = kc_ref[...].astype(BF16)
    vc = vc_ref[...].astype(BF16)
    blocks = []
    for bb in range(bpt):
        i = t * bpt + bb
        start = pl.multiple_of(jnp.clip((i - 1) * 128, 0, Lb - 384), 128)
        win = pl.ds(start, 384)
        kk = (k_ref[win, :] * ck_ref[win, :] + kr_ref[win, :] * sk_ref[win, :]).astype(BF16)
        qpos = i * 128 + lax.broadcasted_iota(jnp.int32, (128, 384), 0)
        kpos = start + lax.broadcasted_iota(jnp.int32, (128, 384), 1)
        blocks.append((kk, v_ref[win, :].astype(BF16), jnp.abs(qpos - kpos) <= 128))
    for h in range(4):
        g = h // 2
        sl = slice(64 * g, 64 * (g + 1))
        locs = [(slice(128 * bb, 128 * (bb + 1)), kk[:, sl], vv[:, sl], ("keep", valid))
                for bb, (kk, vv, valid) in enumerate(blocks)]
        qh = q[:, 64 * h:64 * (h + 1)].astype(BF16)
        o_ref[:, 64 * h:64 * (h + 1)] = _attn_local_ctx(qh, locs, kc[:, sl], vc[:, sl], sink_ref[h])


def _lat_win_attention(proj, kc, vc, tabs, sink, NB, Lb):
    T = proj.shape[0]
    bpt = 1
    tq = 128 * bpt
    nt = Lb // tq
    Lc = kc.shape[1]
    cq, sq, ck, sk = tabs
    qspec = lambda off: pl.BlockSpec((tq, 256), lambda b, i: (b * nt + i, off // 256))
    kspec = lambda off: pl.BlockSpec((Lb, 128), lambda b, i: (b, off // 128))
    cspec = pl.BlockSpec((None, Lc, 128), lambda b, i: (b, 0, 0))
    return pl.pallas_call(
        functools.partial(_lat_win_kernel, Lb=Lb, bpt=bpt),
        grid=(NB, nt),
        in_specs=[qspec(P_WQ), qspec(P_WQR), kspec(P_WK), kspec(P_WKR), kspec(P_WV), cspec, cspec,
                  pl.BlockSpec((tq, 256), lambda b, i: (i, 0)), pl.BlockSpec((tq, 256), lambda b, i: (i, 0)),
                  pl.BlockSpec((Lb, 128), lambda b, i: (0, 0)), pl.BlockSpec((Lb, 128), lambda b, i: (0, 0)),
                  pl.BlockSpec(memory_space=pltpu.SMEM)],
        out_specs=pl.BlockSpec((tq, 256), lambda b, i: (b * nt + i, 0)),
        out_shape=jax.ShapeDtypeStruct((T, 256), F32),
        compiler_params=_cp("arbitrary", "arbitrary"),
        name="lat_win_attention",
    )(proj, proj, proj, proj, proj, kc, vc, cq, sq, ck, sk, sink)


def _na_bias_kernel(rpb_ref, o_ref):
    h = pl.program_id(0)
    qc = lax.broadcasted_iota(jnp.int32, (GRID_W, GRID_W), 0)
    kc = lax.broadcasted_iota(jnp.int32, (GRID_W, GRID_W), 1)
    dc = kc - qc + (NA_KW - 1)
    wstart = jnp.clip(qc - NA_KW // 2, 0, GRID_W - NA_KW)
    ok = (kc >= wstart) & (kc < wstart + NA_KW)
    n_dc = 2 * NA_KW - 1
    n_dr = 2 * NA_KH - 1
    tabs = []
    for dr in range(n_dr):
        t = jnp.zeros((GRID_W, GRID_W), F32)
        for j in range(n_dc):
            t = jnp.where(dc == j, rpb_ref[(h * n_dr + dr) * n_dc + j], t)
        tabs.append(jnp.where(ok, t, NEG))
    for o in range(NA_KH):
        for a in range(NA_KH):
            o_ref[o, :, GRID_W * a:GRID_W * (a + 1)] = tabs[a + NA_KH - 1 - o]


def _na_bias(rpb):
    H = rpb.shape[0]
    return pl.pallas_call(
        _na_bias_kernel,
        grid=(H,),
        in_specs=[pl.BlockSpec(memory_space=pltpu.SMEM)],
        out_specs=pl.BlockSpec((None, NA_KH, GRID_W, NA_KH * GRID_W), lambda h: (h, 0, 0, 0)),
        out_shape=jax.ShapeDtypeStruct((H, NA_KH, GRID_W, NA_KH * GRID_W), F32),
        compiler_params=_cp("arbitrary"),
        name="na_bias",
    )(rpb.reshape(-1))


def _lat_na_kernel(q_ref, k_ref, v_ref, kc_ref, vc_ref, bias_ref, o_ref, *, rows, rpt):
    t = pl.program_id(1)
    q = q_ref[...] * ATT_SCALE
    kc = kc_ref[...].astype(BF16)
    vc = vc_ref[...].astype(BF16)
    bands = []
    for rr in range(rpt):
        r = t * rpt + rr
        first = jnp.clip(r - NA_KH // 2, 0, rows - NA_KH)
        win = pl.ds(pl.multiple_of(first * GRID_W, GRID_W), NA_KH * GRID_W)
        bands.append((k_ref[win, :].astype(BF16), v_ref[win, :].astype(BF16), r - first))
    for h in range(4):
        sl = slice(64 * h, 64 * (h + 1))
        locs = [(slice(GRID_W * rr, GRID_W * (rr + 1)), kk[:, sl], vv[:, sl], ("add", bias_ref[h, off]))
                for rr, (kk, vv, off) in enumerate(bands)]
        o_ref[:, sl] = _attn_local_ctx(q[:, sl].astype(BF16), locs, kc[:, sl], vc[:, sl], None)


def _lat_na_attention(proj, kc, vc, bias, NB, Lb):
    T = proj.shape[0]
    rows = Lb // GRID_W
    rpt = 8
    tq = GRID_W * rpt
    nt = rows // rpt
    Lc = kc.shape[1]
    kspec = lambda off: pl.BlockSpec((Lb, 256), lambda b, t: (b, off // 256))
    cspec = pl.BlockSpec((None, Lc, 256), lambda b, t: (b, 0, 0))
    return pl.pallas_call(
        functools.partial(_lat_na_kernel, rows=rows, rpt=rpt),
        grid=(NB, nt),
        in_specs=[pl.BlockSpec((tq, 256), lambda b, t: (b * nt + t, P_NQ // 256)),
                  kspec(P_NK), kspec(P_NV), cspec, cspec,
                  pl.BlockSpec((4, NA_KH, GRID_W, NA_KH * GRID_W), lambda b, t: (0, 0, 0, 0))],
        out_specs=pl.BlockSpec((tq, 256), lambda b, t: (b * nt + t, 0)),
        out_shape=jax.ShapeDtypeStruct((T, 256), F32),
        compiler_params=_cp("arbitrary", "arbitrary"),
        name="lat_na_attention",
    )(proj, proj, proj, kc, vc, bias)


def _short_conv_kernel(a_ref, b_ref, c_ref, w_ref, bias_ref, oa_ref, ob_ref, oc_ref, *, L):
    t = lax.broadcasted_iota(jnp.int32, (L, HY_C), 0)
    for n, (x_ref, o_ref) in enumerate(((a_ref, oa_ref), (b_ref, ob_ref), (c_ref, oc_ref))):
        sl = slice(HY_C * n, HY_C * (n + 1))
        x = x_ref[...]
        prev = jnp.where(t == 0, 0.0, pltpu.roll(x, 1, axis=0))
        nxt = jnp.where(t == L - 1, 0.0, pltpu.roll(x, L - 1, axis=0))
        o_ref[...] = prev * w_ref[0:1, sl] + x * w_ref[1:2, sl] + nxt * w_ref[2:3, sl] + bias_ref[:, sl]


def _short_conv(proj, w, b, NB, Lb):
    T = proj.shape[0]
    spec = lambda c: pl.BlockSpec((Lb, HY_C), lambda i: (i, c))
    return pl.pallas_call(
        functools.partial(_short_conv_kernel, L=Lb),
        grid=(NB,),
        in_specs=[spec(P_HY // HY_C), spec(P_HY // HY_C + 1), spec(P_HY // HY_C + 2),
                  pl.BlockSpec((3, 3 * HY_C), lambda i: (0, 0)), pl.BlockSpec((1, 3 * HY_C), lambda i: (0, 0))],
        out_specs=[spec(0)] * 3,
        out_shape=[jax.ShapeDtypeStruct((T, HY_C), F32)] * 3,
        compiler_params=_cp("arbitrary"),
        name="hyena_short_conv",
    )(proj, proj, proj, w, b)


def _hy_filter_kernel(w1_ref, b1_ref, w2_ref, b2_ref, w3_ref, freq_ref, ld_ref, fs_ref, nyq_ref, *, L):
    ti = lax.broadcasted_iota(jnp.int32, (L, 128), 0)
    t = ti.astype(F32)
    j = lax.broadcasted_iota(jnp.int32, (L, 128), 1)
    band = jnp.where(j <= HY_BANDS, j - 1, j - 1 - HY_BANDS).astype(F32)
    ang = (2.0 * math.pi / L) * t * band
    tn = t / L
    z = jnp.where(j == 0, tn, jnp.where(j <= HY_BANDS, jnp.cos(ang),
                                        jnp.where(j <= 2 * HY_BANDS, -jnp.sin(ang), 0.0)))
    a = jnp.sin(freq_ref[0:1, :] * (_dot_hi(z, w1_ref[...]) + b1_ref[...]))
    a = jnp.sin(freq_ref[1:2, :] * (_dot_hi(a, w2_ref[...]) + b2_ref[...]))
    filt = _dot_hi(a, w3_ref[...])
    tcol = lax.broadcasted_iota(jnp.int32, (L, 4 * HY_C), 0)
    filt = filt * jnp.exp(-(tcol.astype(F32) / L) * jnp.exp(ld_ref[...]))
    t1 = lax.broadcasted_iota(jnp.int32, (L, HY_C), 0)
    sign = jnp.where(t1 % 2 == 0, 1.0, -1.0)
    for n in range(2):
        fwd = filt[:, 2 * HY_C * n:2 * HY_C * n + HY_C]
        bwd = jnp.where(t1 == 0, 0.0, filt[:, 2 * HY_C * n + HY_C:2 * HY_C * (n + 1)])
        tot = fwd + bwd
        fs_ref[:, HY_C * n:HY_C * (n + 1)] = tot
        fs_ref[:, 2 * HY_C + HY_C * n:2 * HY_C + HY_C * (n + 1)] = fwd - bwd
        nyq_ref[:, HY_C * n:HY_C * (n + 1)] = (tot * sign).sum(0, keepdims=True)


def _hy_filter(L, w1p, b1, w2, b2, w3, freq, ld):
    full = lambda s: pl.BlockSpec(s, lambda: tuple(0 for _ in s))
    return pl.pallas_call(
        functools.partial(_hy_filter_kernel, L=L),
        in_specs=[full((128, 64)), full((1, 64)), full((64, 64)), full((1, 64)), full((64, 4 * HY_C)),
                  full((2, 64)), full((1, 4 * HY_C))],
        out_specs=[full((L, 4 * HY_C)), full((1, 2 * HY_C))],
        out_shape=[jax.ShapeDtypeStruct((L, 4 * HY_C), F32), jax.ShapeDtypeStruct((1, 2 * HY_C), F32)],
        compiler_params=pltpu.CompilerParams(vmem_limit_bytes=VMEM_LIMIT),
        name="hyena_filter",
    )(w1p, b1, w2, b2, w3, freq, ld)


def _hy_gdft_kernel(cm_ref, sm_ref, fs_ref, nyq_ref, gr_ref, gi_ref, *, tm):
    m = pl.program_id(0)
    f = fs_ref[...].astype(BF16)
    gr_ref[...] = _dot(cm_ref[...], f[:, :2 * HY_C])
    gi = _dot(sm_ref[...], f[:, 2 * HY_C:])
    row = m * tm + lax.broadcasted_iota(jnp.int32, (tm, 2 * HY_C), 0)
    gi_ref[...] = jnp.where(row == 0, nyq_ref[...], gi)


def _hy_gdft(cm, sm, fs, nyq, L, tm):
    return pl.pallas_call(
        functools.partial(_hy_gdft_kernel, tm=tm),
        grid=(L // tm,),
        in_specs=[pl.BlockSpec((tm, L), lambda m: (m, 0)), pl.BlockSpec((tm, L), lambda m: (m, 0)),
                  pl.BlockSpec((L, 4 * HY_C), lambda m: (0, 0)), pl.BlockSpec((1, 2 * HY_C), lambda m: (0, 0))],
        out_specs=[pl.BlockSpec((tm, 2 * HY_C), lambda m: (m, 0))] * 2,
        out_shape=[jax.ShapeDtypeStruct((L, 2 * HY_C), F32)] * 2,
        compiler_params=_cp("arbitrary"),
        name="hyena_filter_dft",
    )(cm, sm, fs, nyq)


def _hy_fwd_kernel(cm_ref, sm_ref, z_ref, gr_ref, gi_ref, yr_ref, yi_ref, *, L, tm, ns):
    m = pl.program_id(1)
    gr = gr_ref[...]
    gi = gi_ref[...]
    row0 = (m * tm + lax.broadcasted_iota(jnp.int32, (tm, HY_C), 0)) == 0
    s = jnp.where(row0, 0.5 / L, 1.0 / L)
    for g in range(ns):
        zb = z_ref[g * L:(g + 1) * L, :].astype(BF16)
        zr = _dot(cm_ref[...], zb)
        zi = _dot(sm_ref[...], zb)
        zigi = zi * gi
        yr_ref[g * tm:(g + 1) * tm, :] = ((zr * gr - jnp.where(row0, 0.0, zigi)) * s).astype(BF16)
        yi_ref[g * tm:(g + 1) * tm, :] = (jnp.where(row0, zigi, zr * gi + zi * gr) * s).astype(BF16)


def _hy_seqs_per_step(NB, Lb, tm):
    ns = max(1, 2048 // Lb) if tm == Lb else 1
    while NB % ns:
        ns //= 2
    return ns


def _hy_fwd(cm, sm, z, gr, gi, n, NB, Lb, tm):
    T = z.shape[0]
    nm = Lb // tm
    ns = _hy_seqs_per_step(NB, Lb, tm)
    return pl.pallas_call(
        functools.partial(_hy_fwd_kernel, L=Lb, tm=tm, ns=ns),
        grid=(NB // ns, nm),
        in_specs=[pl.BlockSpec((tm, Lb), lambda b, m: (m, 0)), pl.BlockSpec((tm, Lb), lambda b, m: (m, 0)),
                  pl.BlockSpec((ns * Lb, HY_C), lambda b, m: (b, 0)),
                  pl.BlockSpec((tm, HY_C), lambda b, m: (m, n)), pl.BlockSpec((tm, HY_C), lambda b, m: (m, n))],
        out_specs=[pl.BlockSpec((ns * tm, HY_C), lambda b, m: (b * nm + m, 0))] * 2,
        out_shape=[jax.ShapeDtypeStruct((T, HY_C), BF16)] * 2,
        compiler_params=_cp("arbitrary", "arbitrary"),
        name="hyena_fwd_dft",
    )(cm, sm, z, gr, gi)


def _hy_inv_kernel(cm_ref, smt_ref, yr_ref, yi_ref, z_ref, g_ref, skip_ref, o_ref, *, L, tm, ns):
    for g in range(ns):
        seq = slice(g * L, (g + 1) * L)
        out = slice(g * tm, (g + 1) * tm)
        conv = _dot(cm_ref[...], yr_ref[seq, :]) + _dot(smt_ref[...], yi_ref[seq, :])
        o_ref[out, :] = g_ref[out, :] * (conv + skip_ref[...] * z_ref[out, :])


def _hy_inv(cm, smt, yr, yi, z, gate, skip, n, NB, Lb, tm):
    T = z.shape[0]
    nm = Lb // tm
    ns = _hy_seqs_per_step(NB, Lb, tm)
    tile = pl.BlockSpec((ns * tm, HY_C), lambda b, m: (b * nm + m, 0))
    seq = pl.BlockSpec((ns * Lb, HY_C), lambda b, m: (b, 0))
    return pl.pallas_call(
        functools.partial(_hy_inv_kernel, L=Lb, tm=tm, ns=ns),
        grid=(NB // ns, nm),
        in_specs=[pl.BlockSpec((tm, Lb), lambda b, m: (m, 0)), pl.BlockSpec((tm, Lb), lambda b, m: (m, 0)),
                  seq, seq, tile, tile, pl.BlockSpec((None, 1, HY_C), lambda b, m: (n, 0, 0))],
        out_specs=tile,
        out_shape=jax.ShapeDtypeStruct((T, HY_C), F32),
        compiler_params=_cp("arbitrary", "arbitrary"),
        name="hyena_inv_dft",
    )(cm, smt, yr, yi, z, gate, skip)


def _dft_mats(L):
    k = jnp.arange(L, dtype=jnp.int32)
    blk = 64

    def trig(mult):
        ang = ((mult[:, None] * k[None, :]) % (2 * L)).astype(F32) * (math.pi / L)
        return jnp.cos(ang), jnp.sin(ang)

    ca, sa = trig(jnp.arange(L // blk, dtype=jnp.int32) * blk)
    cb, sb = trig(jnp.arange(blk, dtype=jnp.int32))
    cm = (ca[:, None, :] * cb[None] - sa[:, None, :] * sb[None]).reshape(L, L)
    s = -(sa[:, None, :] * cb[None] + ca[:, None, :] * sb[None]).reshape(L, L)
    alt = jnp.where(k % 2 == 0, 1.0, -1.0).astype(F32)
    sm = jnp.where(k[:, None] == 0, alt[None, :], s)
    smt = jnp.where(k[None, :] == 0, alt[:, None], s)
    return cm.astype(BF16), sm.astype(BF16), smt.astype(BF16)


def _merge_kernel(oa_ref, ob_ref, oc_ref, od_ref, g0_ref, g1_ref, g2_ref, g3_ref, wb_ref, wo_ref, x_ref, mod_ref,
                  lg_ref, lb_ref, rt_ref, *rest):
    x1_ref, hp_ref, logit_ref = rest[-3:]
    acc = None
    for o_ref, g_ref, i in ((oa_ref, g0_ref, 0), (ob_ref, g1_ref, 1), (oc_ref, g2_ref, 2), (od_ref, g3_ref, 3)):
        y = _sigmoid(g_ref[...].astype(F32)) * _dot(o_ref[...].astype(BF16), wb_ref[i])
        acc = y if acc is None else acc + y
    mix = _dot(acc.astype(BF16), wo_ref[...])
    m = mod_ref[...]
    x1 = _layer_norm(DN_ALPHA * x_ref[...] + m[2:3, :] * mix, lg_ref[...], lb_ref[...])
    x1_ref[...] = x1
    h2 = x1 * (1.0 + m[4:5, :]) + m[3:4, :]
    hp_ref[...] = _pack_pairs(h2)
    logit_ref[...] = _dot_hi(h2, rt_ref[...])


def _merge(outs, gates, wb, wo, x, mod, lg, lb, router, mod_row, bm, after=None):
    T = x.shape[0]
    row = lambda w: pl.BlockSpec((bm, w), lambda i: (i, 0))
    gspec = lambda n: pl.BlockSpec((bm, D), lambda i: (i, n))
    fixed2 = lambda s: pl.BlockSpec(s, lambda i: (0, 0))
    dep_specs, dep_args = _after(after)
    return pl.pallas_call(
        _merge_kernel,
        grid=(T // bm,),
        in_specs=[row(256)] * 4 + [gspec(0), gspec(1), gspec(2), gspec(3),
                                   pl.BlockSpec((4, 256, D), lambda i: (0, 0, 0)), fixed2((D, D)), row(D),
                                   pl.BlockSpec((None, 6, D), lambda i: (mod_row(i), 0, 0)),
                                   fixed2((1, D)), fixed2((1, D)), fixed2((D, N_EXPERTS))] + dep_specs,
        out_specs=[row(D), row(D // 2), row(N_EXPERTS)],
        out_shape=[jax.ShapeDtypeStruct((T, D), F32), jax.ShapeDtypeStruct((T, D // 2), jnp.int32),
                   jax.ShapeDtypeStruct((T, N_EXPERTS), F32)],
        compiler_params=_cp("arbitrary"),
        name="merge_norm",
    )(*outs, gates, gates, gates, gates, wb, wo, x, mod, lg, lb, router, *dep_args)


def _router_kernel(logit_ref, bias_ref, g_ref, rank_ref, cnt_ref, *, tt):
    per = N_EXPERTS // N_GROUPS
    scores = _sigmoid(logit_ref[...])
    sel = (scores + bias_ref[...]).reshape(N_GROUPS, per, tt)
    gid = lax.broadcasted_iota(jnp.int32, (N_GROUPS, per, tt), 0).astype(F32)
    jid = lax.broadcasted_iota(jnp.int32, (N_GROUPS, per, tt), 1).astype(F32)
    eid = gid * per + jid
    ninf = -jnp.inf
    m1 = sel.max(1, keepdims=True)
    i1 = jnp.where(sel == m1, jid, float(per)).min(1, keepdims=True)
    m2 = jnp.where(jid == i1, ninf, sel).max(1, keepdims=True)
    gs = m1 + m2
    g1 = lax.broadcasted_iota(jnp.int32, (N_GROUPS, 1, tt), 0).astype(F32)
    chosen = jnp.zeros((N_GROUPS, 1, tt), F32)
    for _ in range(TOPK_GROUPS):
        mx = gs.max(0, keepdims=True)
        gi = jnp.where(gs == mx, g1, float(N_GROUPS)).min(0, keepdims=True)
        pick = g1 == gi
        chosen = jnp.where(pick, 1.0, chosen)
        gs = jnp.where(pick, ninf, gs)
    cand = jnp.where(chosen > 0.0, sel, NEG)
    picked = jnp.zeros((N_GROUPS, per, tt), F32)
    for _ in range(TOP_K):
        mx = cand.max(1, keepdims=True).max(0, keepdims=True)
        ei = jnp.where(cand == mx, eid, float(N_EXPERTS)).min(1, keepdims=True).min(0, keepdims=True)
        pick = eid == ei
        picked = jnp.where(pick, 1.0, picked)
        cand = jnp.where(pick, ninf, cand)
    w = scores.reshape(N_GROUPS, per, tt) * picked
    wsum = w.sum(1, keepdims=True).sum(0, keepdims=True)
    g_ref[...] = (w / wsum * ROUTED_SCALE).reshape(N_EXPERTS, tt)
    pk = picked.reshape(N_EXPERTS, tt)
    t_in = lax.broadcasted_iota(jnp.int32, (tt, tt), 0)
    t_out = lax.broadcasted_iota(jnp.int32, (tt, tt), 1)
    upper = jnp.where(t_in <= t_out, 1.0, 0.0).astype(BF16)

    @pl.when(pl.program_id(0) == 0)
    def _():
        cnt_ref[...] = jnp.zeros_like(cnt_ref)

    before = cnt_ref[:, 0:1]
    rank_ref[...] = jnp.where(pk > 0.0, before + _dot(pk.astype(BF16), upper) - 1.0, -1.0)
    cnt_ref[...] += pk.sum(-1, keepdims=True)


def _router(logits_t, bias, tt):
    T = logits_t.shape[1]
    tile = pl.BlockSpec((N_EXPERTS, tt), lambda i: (0, i))
    return pl.pallas_call(
        functools.partial(_router_kernel, tt=tt),
        grid=(T // tt,),
        in_specs=[tile, pl.BlockSpec((N_EXPERTS, 1), lambda i: (0, 0))],
        out_specs=[tile, tile, pl.BlockSpec((N_EXPERTS, 128), lambda i: (0, 0))],
        out_shape=[jax.ShapeDtypeStruct((N_EXPERTS, T), F32), jax.ShapeDtypeStruct((N_EXPERTS, T), F32),
                   jax.ShapeDtypeStruct((N_EXPERTS, 128), F32)],
        compiler_params=_cp("arbitrary"),
        name="moe_router",
    )(logits_t, bias)


def _route_pos_kernel(gate_ref, rank_ref, cnt_ref, pos_ref, w_ref, te_ref, nx_ref, nt_ref, *, tm, nt_max):
    ei = lax.broadcasted_iota(jnp.int32, (N_EXPERTS, N_EXPERTS), 0)
    ej = lax.broadcasted_iota(jnp.int32, (N_EXPERTS, N_EXPERTS), 1)
    below = jnp.where(ej < ei, 1.0, 0.0)
    padded = jnp.ceil(cnt_ref[...] * (1.0 / tm)) * tm
    offs = _dot_hi(below, padded)
    rank = rank_ref[...]
    routed = rank >= 0.0
    pos = offs[:, 0:1] + rank
    slot = _dot(below.astype(BF16), jnp.where(routed, 1.0, 0.0).astype(BF16))
    gate = gate_ref[...]
    for k in range(TOP_K):
        mine = routed & (slot == float(k))
        pos_ref[k:k + 1, :] = jnp.where(mine, pos, 0.0).sum(0, keepdims=True).astype(jnp.int32)
        w_ref[k:k + 1, :] = jnp.where(mine, gate, 0.0).sum(0, keepdims=True)
    ends = (offs + padded)[:, 0:1]
    first = (lax.broadcasted_iota(jnp.int32, (N_EXPERTS, nt_max), 1) * tm).astype(F32)
    te = jnp.minimum(jnp.where(ends <= first, 1.0, 0.0).sum(0, keepdims=True), N_EXPERTS - 1.0)
    te_ref[...] = te.astype(jnp.int32)
    eid = lax.broadcasted_iota(jnp.int32, (N_EXPERTS, nt_max), 0).astype(F32)
    nx_ref[...] = (jnp.where(eid == te, ends, 0.0).sum(0, keepdims=True) * (1.0 / tm)).astype(jnp.int32)
    nt_ref[...] = (padded.sum(0, keepdims=True) * (1.0 / tm)).astype(jnp.int32)


def _route_pos(gate_t, rank, cnt, tt, tm, nt_max):
    T = gate_t.shape[1]
    tile = pl.BlockSpec((N_EXPERTS, tt), lambda i: (0, i))
    out = pl.BlockSpec((TOP_K, tt), lambda i: (0, i))
    return pl.pallas_call(
        functools.partial(_route_pos_kernel, tm=tm, nt_max=nt_max),
        grid=(T // tt,),
        in_specs=[tile, tile, pl.BlockSpec((N_EXPERTS, 128), lambda i: (0, 0))],
        out_specs=[out, out, pl.BlockSpec((1, nt_max), lambda i: (0, 0)), pl.BlockSpec((1, nt_max), lambda i: (0, 0)),
                   pl.BlockSpec((1, 128), lambda i: (0, 0))],
        out_shape=[jax.ShapeDtypeStruct((TOP_K, T), jnp.int32), jax.ShapeDtypeStruct((TOP_K, T), F32),
                   jax.ShapeDtypeStruct((1, nt_max), jnp.int32), jax.ShapeDtypeStruct((1, nt_max), jnp.int32),
                   jax.ShapeDtypeStruct((1, 128), jnp.int32)],
        compiler_params=_cp("arbitrary"),
        name="moe_positions",
    )(gate_t, rank, cnt)


def _gmm_kernel(te_ref, nx_ref, nt_ref, xs_ref, w1_hbm, w3_hbm, w2_hbm, *rest, l):
    ys_ref, b1_ref, b3_ref, b2_ref, f1_ref, f3_ref, f2_ref, seg_ref, sem = rest[-9:]
    j = pl.program_id(0)
    live = j < nt_ref[0]
    new_expert = (j == 0) | (te_ref[j] != te_ref[jnp.maximum(j - 1, 0)])

    def fetch(e, slot):
        return [pltpu.make_async_copy(w_hbm.at[l, e], f_ref.at[slot], sem.at[i, slot])
                for i, (w_hbm, f_ref) in enumerate(((w1_hbm, f1_ref), (w3_hbm, f3_ref), (w2_hbm, f2_ref)))]

    @pl.when(live & new_expert)
    def _():
        @pl.when(j == 0)
        def _():
            seg_ref[0] = 0
            for c in fetch(te_ref[0], 0):
                c.start()

        slot = lax.rem(seg_ref[0], 2)
        for c in fetch(te_ref[j], slot):
            c.wait()
        b1_ref[...] = f1_ref[slot].astype(BF16)
        b3_ref[...] = f3_ref[slot].astype(BF16)
        b2_ref[...] = f2_ref[slot].astype(BF16)
        nxt = nx_ref[j]

        @pl.when(nxt < nt_ref[0])
        def _():
            for c in fetch(te_ref[nxt], 1 - slot):
                c.start()

        seg_ref[0] = seg_ref[0] + 1

    @pl.when(live)
    def _():
        xa, xb = _unpack_pairs(xs_ref[...])
        xa, xb = xa.astype(BF16), xb.astype(BF16)
        half = D // 2
        a = _dot(xa, b1_ref[:half, :]) + _dot(xb, b1_ref[half:, :])
        b = _dot(xa, b3_ref[:half, :]) + _dot(xb, b3_ref[half:, :])
        hid = (a * _sigmoid(a) * b).astype(BF16)
        ys_ref[...] = _pack_pairs(_dot(hid, b2_ref[...]))


def _gmm(te, nx, nt, xs, w1, w3, w2, l, tm, after=None):
    n_slots = xs.shape[0]
    ds = D_EXPERT
    rows = pl.BlockSpec((tm, D // 2), lambda j, te, nx, nt: (jnp.minimum(j, nt[0] - 1), 0))
    hbm = pl.BlockSpec(memory_space=pl.ANY)
    dep_specs, dep_args = _after(after)
    return pl.pallas_call(
        functools.partial(_gmm_kernel, l=l),
        grid_spec=pltpu.PrefetchScalarGridSpec(
            num_scalar_prefetch=3,
            grid=(n_slots // tm,),
            in_specs=[rows, hbm, hbm, hbm] + dep_specs,
            out_specs=rows,
            scratch_shapes=[pltpu.VMEM((D, ds), BF16), pltpu.VMEM((D, ds), BF16), pltpu.VMEM((ds, D), BF16),
                            pltpu.VMEM((2, D, ds), F32), pltpu.VMEM((2, D, ds), F32), pltpu.VMEM((2, ds, D), F32),
                            pltpu.SMEM((1,), jnp.int32), pltpu.SemaphoreType.DMA((3, 2))]),
        out_shape=jax.ShapeDtypeStruct((n_slots, D // 2), jnp.int32),
        compiler_params=_cp("arbitrary"),
        name="moe_grouped_ffn",
    )(te, nx, nt, xs, w1, w3, w2, *dep_args)


def _combine_kernel(yk_ref, w_ref, hp_ref, s1_ref, s3_ref, s2_ref, x_ref, mod_ref, lg_ref, lb_ref, o_ref):
    w = w_ref[...]
    acc_a = acc_b = None
    for k in range(TOP_K):
        ya, yb = _unpack_pairs(yk_ref[k])
        wk = w[:, k:k + 1]
        acc_a = wk * ya if acc_a is None else acc_a + wk * ya
        acc_b = wk * yb if acc_b is None else acc_b + wk * yb
    ha, hb = _unpack_pairs(hp_ref[...])
    ha, hb = ha.astype(BF16), hb.astype(BF16)
    half = D // 2
    a = _dot(ha, s1_ref[:half, :]) + _dot(hb, s1_ref[half:, :])
    b = _dot(ha, s3_ref[:half, :]) + _dot(hb, s3_ref[half:, :])
    y = jnp.concatenate([acc_a, acc_b], axis=1) + _dot((a * _sigmoid(a) * b).astype(BF16), s2_ref[...])
    m = mod_ref[...]
    o_ref[...] = _layer_norm(DN_ALPHA * x_ref[...] + m[5:6, :] * y, lg_ref[...], lb_ref[...])


def _combine(yk, w, hp, s1, s3, s2, x1, mod, lg, lb, mod_row, bm):
    T = x1.shape[0]
    ds = D_EXPERT
    row = lambda n: pl.BlockSpec((bm, n), lambda i: (i, 0))
    fixed = lambda s: pl.BlockSpec(s, lambda i: (0, 0))
    return pl.pallas_call(
        _combine_kernel,
        grid=(T // bm,),
        in_specs=[pl.BlockSpec((TOP_K, bm, D // 2), lambda i: (0, i, 0)), row(TOP_K), row(D // 2),
                  fixed((D, ds)), fixed((D, ds)), fixed((ds, D)), row(D),
                  pl.BlockSpec((None, 6, D), lambda i: (mod_row(i), 0, 0)), fixed((1, D)), fixed((1, D))],
        out_specs=row(D),
        out_shape=jax.ShapeDtypeStruct((T, D), F32),
        compiler_params=_cp("arbitrary"),
        name="moe_combine_norm",
    )(yk, w, hp, s1, s3, s2, x1, mod, lg, lb)


def _sc_worker():
    return lax.axis_index("s") * SC_CORES + lax.axis_index("c")


def _sc_mesh():
    return plsc.VectorSubcoreMesh(core_axis_name="c", subcore_axis_name="s")


def _sc_gather(table, idx):
    N, W = idx.shape[0], table.shape[1]
    per_w = N // SC_WORKERS
    n_chunks = per_w // SC_ROWS

    def body(table_hbm, idx_hbm, out_hbm, idx_v, rows_v, sem):
        base = _sc_worker() * per_w
        pltpu.sync_copy(idx_hbm.at[pl.ds(base, per_w)], idx_v)

        @pl.loop(0, n_chunks)
        def _(c):
            off = pl.multiple_of(c * SC_ROWS, SC_ROWS)
            pltpu.async_copy(table_hbm.at[idx_v.at[pl.ds(off, SC_ROWS)]], rows_v, sem).wait()
            pltpu.sync_copy(rows_v, out_hbm.at[pl.ds(base + off, SC_ROWS)])

    return pl.kernel(
        body, out_type=jax.ShapeDtypeStruct((N, W), table.dtype), mesh=_sc_mesh(),
        scratch_types=[pltpu.VMEM((per_w,), jnp.int32), pltpu.VMEM((SC_ROWS, W), table.dtype),
                       pltpu.SemaphoreType.DMA],
        name="sc_gather",
    )(table, idx)


def _sc_dispatch(pos, table, n_slots):
    NP, (T, W) = pos.shape[0], table.shape
    per_w = n_slots // SC_WORKERS
    n_chunks = per_w // SC_ROWS
    scan = 8192

    def body(pos_hbm, table_hbm, out_hbm, pos_v, src_v, rows_v, sem):
        base = _sc_worker() * per_w
        lane = lax.iota(jnp.int32, SC_LANES)

        @pl.loop(0, per_w // SC_LANES)
        def _(j):
            o = pl.multiple_of(j * SC_LANES, SC_LANES)
            src_v[pl.ds(o, SC_LANES)] = (base + o + lane) & (T - 1)

        @pl.loop(0, NP // scan)
        def _(c):
            pltpu.sync_copy(pos_hbm.at[pl.ds(pl.multiple_of(c * scan, scan), scan)], pos_v)

            @pl.loop(0, scan // SC_LANES)
            def _(v):
                o = pl.multiple_of(v * SC_LANES, SC_LANES)
                p = pos_v[pl.ds(o, SC_LANES)] - base
                mine = (p >= 0) & (p < per_w)
                tok = (c * scan + o + lane) & (T - 1)
                plsc.store_scatter(src_v, [jnp.where(mine, p, 0)], tok, mask=mine)

        @pl.loop(0, n_chunks)
        def _(c):
            off = pl.multiple_of(c * SC_ROWS, SC_ROWS)
            pltpu.async_copy(table_hbm.at[src_v.at[pl.ds(off, SC_ROWS)]], rows_v, sem).wait()
            pltpu.sync_copy(rows_v, out_hbm.at[pl.ds(base + off, SC_ROWS)])

    return pl.kernel(
        body, out_type=jax.ShapeDtypeStruct((n_slots, W), table.dtype), mesh=_sc_mesh(),
        scratch_types=[pltpu.VMEM((scan,), jnp.int32), pltpu.VMEM((per_w,), jnp.int32),
                       pltpu.VMEM((SC_ROWS, W), table.dtype), pltpu.SemaphoreType.DMA],
        compiler_params=pltpu.CompilerParams(needs_layout_passes=False),
        name="sc_dispatch",
    )(pos, table)


def _caches_kernel(*refs, nb, S):
    n_in = 6 * DEPTH
    outs = refs[n_in:]
    l = pl.program_id(0)
    for a in range(DEPTH):
        @pl.when(l == a)
        def _(a=a):
            ckv, kpe, wk, wv, nk, nv = refs[6 * a:6 * (a + 1)]
            for g in range(nb):
                rows = slice(g * S, (g + 1) * S)
                outs[0][g] = ckv[rows, :]
                outs[1][g] = kpe[rows, 64:96]
                outs[2][g] = wk[rows, :]
                outs[3][g] = wv[rows, :]
                outs[4][g] = nk[rows, :]
                outs[5][g] = nv[rows, :]


def _emit_caches(projs, ckvs, B, S):
    nb = 4
    while B % nb:
        nb //= 2

    def layer_specs(a):
        row = lambda l, b: jnp.where(l == a, b, 0)
        col = lambda w, off: pl.BlockSpec((nb * S, w), lambda l, b: (row(l, b), off // w))
        return [pl.BlockSpec((nb * S, 128), lambda l, b: (row(l, b), 0)), col(128, P_KPE), col(128, P_WK),
                col(128, P_WV), col(256, P_NK), col(256, P_NV)]

    in_specs, args = [], []
    for a in range(DEPTH):
        in_specs += layer_specs(a)
        args += [ckvs[a]] + [projs[a]] * 5
    widths = (128, 32, 128, 128, 256, 256)
    return pl.pallas_call(
        functools.partial(_caches_kernel, nb=nb, S=S),
        grid=(DEPTH, B // nb),
        in_specs=in_specs,
        out_specs=[pl.BlockSpec((nb, None, S, w), lambda l, b: (b, l, 0, 0)) for w in widths],
        out_shape=[jax.ShapeDtypeStruct((B, DEPTH, S, w), F32) for w in widths],
        compiler_params=_cp("arbitrary", "arbitrary"),
        name="context_tensors",
    )(*args)


def _rot_cols(w, q):
    a, b, c, d = w[..., :q], w[..., q:2 * q], w[..., 2 * q:3 * q], w[..., 3 * q:]
    return jnp.concatenate([-b, a, -d, c], -1)


def _prep_w_in(w):
    z = lambda n: jnp.zeros((D, n), w.dtype)
    qlat, ckv, kpe, hy = w[:, 0:256], w[:, 256:384], w[:, 384:416], w[:, 416:1184]
    wq, wk, wv = w[:, 1184:1440], w[:, 1440:1568], w[:, 1568:1696]
    nq, nk, nv, gate = w[:, 1696:1952], w[:, 1952:2208], w[:, 2208:2464], w[:, 2464:]
    wq_r = _rot_cols(wq.reshape(D, 4, 64), 16).reshape(D, 256)
    wk_r = _rot_cols(wk.reshape(D, 2, 64), 16).reshape(D, 128)
    kpe_r = _rot_cols(kpe, 8)
    cols = [qlat, ckv, z(64), kpe, z(32), hy, wq, wk, wv, nq, nk, nv, wq_r, wk_r, z(64), kpe_r, z(32), gate]
    return jnp.concatenate(cols, 1).astype(BF16)


def _prep_mla(w_uq, w_ukv):
    uq = w_uq.reshape(256, 4, 96)
    nope, pe = uq[..., :64], uq[..., 64:]
    z32 = jnp.zeros((256, 4, 32), w_uq.dtype)
    z64 = jnp.zeros((256, 4, 64), w_uq.dtype)
    wcat = jnp.concatenate([nope, pe, z32], -1).reshape(256, 512).astype(BF16)
    wrot = jnp.concatenate([z64, _rot_cols(pe, 8), z32], -1).reshape(256, 512).astype(BF16)
    ukv = w_ukv.reshape(128, 4, 128)
    wk = jnp.concatenate([ukv[..., :64], jnp.zeros((128, 4, 64), w_ukv.dtype)], -1).reshape(128, 512).astype(BF16)
    wv = ukv[..., 64:].reshape(128, 256).astype(BF16)
    return wcat, wrot, wk, wv


def _rope_tab(L, q):
    t = jnp.arange(L)
    inv = ROPE_BASE ** (-jnp.arange(q, dtype=F32) / q)
    ar = (t // GRID_W).astype(F32)[:, None] * inv[None, :]
    ac = (t % GRID_W).astype(F32)[:, None] * inv[None, :]
    cos = jnp.concatenate([jnp.cos(ar), jnp.cos(ar), jnp.cos(ac), jnp.cos(ac)], 1)
    sin = jnp.concatenate([jnp.sin(ar), jnp.sin(ar), jnp.sin(ac), jnp.sin(ac)], 1)
    return cos, sin


def _rope_tables(L):
    c8, s8 = _rope_tab(L, 8)
    c16, s16 = _rope_tab(L, 16)
    one, zero = jnp.ones((L, 64), F32), jnp.zeros((L, 64), F32)
    z32 = jnp.zeros((L, 32), F32)
    mla_q = (jnp.tile(jnp.concatenate([one, c8, z32], 1), (1, 4)), jnp.tile(jnp.concatenate([zero, s8, z32], 1), (1, 4)))
    mla_k = (jnp.concatenate([zero, c8, z32], 1), jnp.concatenate([zero, s8, z32], 1))
    win = (jnp.tile(c16, (1, 4)), jnp.tile(s16, (1, 4)), jnp.tile(c16, (1, 2)), jnp.tile(s16, (1, 2)))
    return mla_q + mla_k, win


def _hyena(proj, lp, dft, NB, Lb):
    cm, sm, smt = dft
    tm = min(Lb, 512)
    v, x1, x2 = _short_conv(proj, lp["hy_conv_w"], lp["hy_conv_b"].reshape(1, -1), NB, Lb)
    w1p = jnp.pad(lp["hy_w1"], ((0, 128 - lp["hy_w1"].shape[0]), (0, 0)))
    fs, nyq = _hy_filter(Lb, w1p, lp["hy_b1"].reshape(1, -1), lp["hy_w2"], lp["hy_b2"].reshape(1, -1), lp["hy_w3"],
                         lp["hy_sin_freq"], lp["hy_log_decay"].reshape(1, -1))
    gr, gi = _hy_gdft(cm, sm, fs, nyq, Lb, tm)
    skip = lp["hy_skip"].reshape(2, 1, HY_C)
    z = v
    for n, gate in enumerate((x1, x2)):
        yr, yi = _hy_fwd(cm, sm, z, gr, gi, n, NB, Lb, tm)
        z = _hy_inv(cm, smt, yr, yi, z, gate, skip, n, NB, Lb, tm)
    return z


def _layer_steps(x, mod, lp, l, NB, Lb, mod_row_of_batch, dft, cache=None, tabs=None, na_bias=None):
    T = NB * Lb
    latent = cache is not None
    bm = 256
    rows_of = lambda n: (lambda i: mod_row_of_batch((i * n) // Lb))
    mod_row = rows_of(bm)
    span = Lb if latent else T
    bmp = min(span, 1024)
    proj, gates = _in_proj(x, mod, lp["w_in_p"], rows_of(bmp), bmp)

    gq, gkv = lp["mla_q_norm"].reshape(1, -1), lp["mla_kv_norm"].reshape(1, -1)
    wcat, wrot, wk, wv = lp["mla_w"]
    q_all, ckv_n, kpe_r = _mla_q(proj, gq, gkv, wcat, wrot, tabs[0] if latent else None, Lb, min(span, 512))
    if latent:
        ckv_c, kpe_c, kc_c, vc_c, kd_c, vd_c = cache
        Lc = ckv_c.shape[1]
        kpe_cp = jnp.pad(kpe_c, ((0, 0), (0, 0), (64, 32)))
        ckv_all = jnp.concatenate([ckv_c, ckv_n.reshape(NB, Lb, 128)], 1).reshape(NB * (Lc + Lb), 128)
        kpe_all = jnp.concatenate([kpe_cp, kpe_r.reshape(NB, Lb, 128)], 1).reshape(NB * (Lc + Lb), 128)
        k_all, v_all = _mla_kv(ckv_all, kpe_all, wk, wv, 512)
        oc = _lat_win_attention(proj, kc_c.reshape(NB, Lc, 128), vc_c.reshape(NB, Lc, 128), tabs[1],
                                lp["win_sink"], NB, Lb)
        od = _lat_na_attention(proj, kd_c.reshape(NB, Lc, 256), vd_c.reshape(NB, Lc, 256), na_bias, NB, Lb)
        ob = _hyena(proj, lp, dft, NB, Lb)
        after = yield "projected", od
        oa = _lat_mla_attention(q_all, k_all, v_all, NB, Lb, Lc + Lb, 256, after=after)
        after = None
    else:
        k_all, v_all = _mla_kv(ckv_n, kpe_r, wk, wv, 512)
        oa, oc, od = _ctx_attention(proj, q_all, k_all, v_all, lp["win_sink"], NB, Lb)
        ob = _hyena(proj, lp, dft, NB, Lb)
        after = yield "projected", oa

    bmm = min(span, 512)
    x1, hp, logits = _merge((oa, ob, oc, od), gates, lp["w_branch_b"], lp["w_out_b"], x, mod,
                            lp["ln1_g"].reshape(1, -1), lp["ln1_b"].reshape(1, -1), lp["moe_router"],
                            rows_of(bmm), bmm, after=after)
    n_slots = T * TOP_K + N_EXPERTS * MOE_TM
    gate_t, rank, cnt = _router(logits.T, lp["moe_bias"].reshape(-1, 1), 512)
    pos, w8, te, nx, nt = _route_pos(gate_t, rank, cnt, 512, MOE_TM, n_slots // MOE_TM)
    xs = _sc_dispatch(pos.reshape(-1), hp, n_slots)
    after = yield "dispatched", None
    ys = _gmm(te.reshape(-1), nx.reshape(-1), nt.reshape(-1)[:1], xs, lp["moe_w1"], lp["moe_w3"], lp["moe_w2"], l,
              MOE_TM, after=after)
    yield "ffn", ys
    yk = _sc_gather(ys, pos.reshape(-1)).reshape(TOP_K, T, D // 2)
    x2 = _combine(yk, w8.T, hp, lp["sh_w1_b"], lp["sh_w3_b"], lp["sh_w2_b"], x1, mod,
                  lp["ln2_g"].reshape(1, -1), lp["ln2_b"].reshape(1, -1), mod_row, bm)
    yield "done", (x2, proj, ckv_n)


def kernel(x_prompt, x_sample, cache_mla_ckv, cache_mla_kpe, cache_win_k, cache_win_v, cache_na_k, cache_na_v, c, c_ctx, w_ada, b_ada, w_in, mla_q_norm, mla_kv_norm, mla_w_uq, mla_w_ukv, hy_conv_w, hy_conv_b, hy_w1, hy_b1, hy_w2, hy_b2, hy_w3, hy_sin_freq, hy_log_decay, hy_skip, win_sink, na_rpb, w_branch, w_out, ln1_g, ln1_b, ln2_g, ln2_b, moe_router, moe_bias, moe_w1, moe_w3, moe_w2, sh_w1, sh_w3, sh_w2):
    B, S, _ = x_prompt.shape
    DB, DS, _ = x_sample.shape
    xp = x_prompt.reshape(B * S, D)
    xs = x_sample.reshape(DB * DS, D)
    cvec = jnp.concatenate([c_ctx[None, :], c, jnp.zeros((8 - 1 - DB, D), F32)], 0)
    dft_ctx = _dft_mats(S)
    dft_lat = _dft_mats(DS)
    tabs = _rope_tables(DS)
    projs, ckvs = [], []

    def params(l):
        return dict(w_in_p=_prep_w_in(w_in[l]), mla_q_norm=mla_q_norm[l], mla_kv_norm=mla_kv_norm[l],
                    mla_w=_prep_mla(mla_w_uq[l], mla_w_ukv[l]), hy_conv_w=hy_conv_w[l], hy_conv_b=hy_conv_b[l],
                    hy_w1=hy_w1[l], hy_b1=hy_b1[l], hy_w2=hy_w2[l], hy_b2=hy_b2[l], hy_w3=hy_w3[l],
                    hy_sin_freq=hy_sin_freq[l], hy_log_decay=hy_log_decay[l], hy_skip=hy_skip[l],
                    win_sink=win_sink[l], w_branch_b=w_branch[l].astype(BF16), w_out_b=w_out[l].astype(BF16),
                    ln1_g=ln1_g[l], ln1_b=ln1_b[l], ln2_g=ln2_g[l], ln2_b=ln2_b[l],
                    moe_router=moe_router[l], moe_bias=moe_bias[l], moe_w1=moe_w1, moe_w3=moe_w3, moe_w2=moe_w2,
                    sh_w1_b=sh_w1[l].astype(BF16), sh_w3_b=sh_w3[l].astype(BF16), sh_w2_b=sh_w2[l].astype(BF16))

    lps = [params(l) for l in range(DEPTH)]
    mods = [_modulation(cvec, w_ada, b_ada, l) for l in range(DEPTH)]

    def ctx_layer(l, x):
        return _layer_steps(x, mods[l], lps[l], l, B, S, lambda b: 0, dft_ctx)

    def lat_layer(l, x):
        cache = (cache_mla_ckv[:, l], cache_mla_kpe[:, l], cache_win_k[:, l], cache_win_v[:, l],
                 cache_na_k[:, l], cache_na_v[:, l])
        return _layer_steps(x, mods[l], lps[l], l, DB, DS, lambda b: 1 + b, dft_lat, cache=cache, tabs=tabs,
                            na_bias=_na_bias(na_rpb[l]))

    ctx = ctx_layer(0, xp)
    next(ctx)
    ctx.send(None)
    for l in range(DEPTH):
        lat = lat_layer(l, xs)
        lat_local = next(lat)[1]
        ys_ctx = ctx.send(lat_local)[1]
        lat.send(ys_ctx)
        xp, proj, ckv_n = ctx.send(None)[1]
        projs.append(proj)
        ckvs.append(ckv_n)
        if l + 1 < DEPTH:
            ctx = ctx_layer(l + 1, xp)
            attended = next(ctx)[1]
            ys_lat = lat.send(attended)[1]
            ctx.send(ys_lat)
        else:
            ckv, kpe, wk, wv, nk, nv = _emit_caches(projs, ckvs, B, S)
            lat.send((xp, nv))
        xs = lat.send(None)[1][0]
    heads = lambda t, h: t.reshape(B, DEPTH, S, h, HEAD_DIM)
    return (xp.reshape(B, S, D), xs.reshape(DB, DS, D), ckv, kpe, heads(wk, 2), heads(wv, 2), heads(nk, 4),
            heads(nv, 4))
```

```python
import functools
import math

import jax
import jax.numpy as jnp
from jax import lax
from jax.experimental import pallas as pl
from jax.experimental.pallas import tpu as pltpu
from jax.experimental.pallas import tpu_sc as plsc

F32 = jnp.float32
BF16 = jnp.bfloat16

D = 1024
DEPTH = 2
GRID_W = 64
HEAD_DIM = 64
MLA_SCALE = 96 ** -0.5
ATT_SCALE = HEAD_DIM ** -0.5
HY_C = 256
HY_BANDS = 8
NA_KH = 8
NA_KW = 16
N_EXPERTS = 64
N_GROUPS = 8
TOP_K = 8
TOPK_GROUPS = 4
D_EXPERT = 256
ROUTED_SCALE = 2.5
ROPE_BASE = 10000.0
LN_EPS = 1e-5
RMS_EPS = 1e-6
NEG = -1e30
DN_ALPHA = (2 * DEPTH) ** 0.25

P_QLAT, P_CKV, P_KPE, P_HY = 0, 256, 384, 512
P_WQ, P_WK, P_WV = 1280, 1536, 1664
P_NQ, P_NK, P_NV = 1792, 2048, 2304
P_WQR, P_WKR, P_KPER, P_GATE = 2560, 2816, 2944, 3072
N_PROJ = 7168

VMEM_LIMIT = 56 * 1024 * 1024

SC_CORES = 2
SC_SUBCORES = 16
SC_LANES = 16
SC_WORKERS = SC_CORES * SC_SUBCORES
SC_ROWS = 64

MOE_TM = 512

def _cp(*sem):
    return pltpu.CompilerParams(dimension_semantics=sem, vmem_limit_bytes=VMEM_LIMIT)


def _sigmoid(x):
    return 1.0 / (1.0 + jnp.exp(-x))


def _dot(a, b):
    return jnp.dot(a, b, preferred_element_type=F32)


def _dot_nt(a, b):
    return lax.dot_general(a, b, (((1,), (1,)), ((), ())), preferred_element_type=F32)


def _dot_hi(a, b):
    return jnp.dot(a, b, preferred_element_type=F32, precision=lax.Precision.HIGHEST)


def _pack_pairs(x):
    w = x.shape[1] // 2
    hi = lax.bitcast_convert_type(x[:, :w].astype(BF16).astype(F32), jnp.int32)
    lo = lax.bitcast_convert_type(x[:, w:].astype(BF16).astype(F32), jnp.int32)
    return hi | lax.shift_right_logical(lo, 16)


def _unpack_pairs(p):
    hi = lax.bitcast_convert_type(p & jnp.int32(-65536), F32)
    lo = lax.bitcast_convert_type(lax.shift_left(p, 16), F32)
    return hi, lo


def _layer_norm(x, g, b):
    mu = jnp.mean(x, -1, keepdims=True)
    xc = x - mu
    var = jnp.mean(xc * xc, -1, keepdims=True)
    return xc * lax.rsqrt(var + LN_EPS) * g + b


def _rms_norm(x, g):
    return x * lax.rsqrt(jnp.mean(x * x, -1, keepdims=True) + RMS_EPS) * g


def _mod_kernel(c_ref, w_ref, b_ref, o_ref):
    c = c_ref[...]
    a = (c * _sigmoid(c)).astype(BF16)
    o_ref[...] = _dot(a, w_ref[...].astype(BF16)) + b_ref[...]


def _modulation(cvec, w_ada, b_ada, l):
    out = pl.pallas_call(
        _mod_kernel,
        grid=(6,),
        in_specs=[pl.BlockSpec((8, D), lambda j: (0, 0)),
                  pl.BlockSpec((None, D, D), lambda j: (l, 0, j)),
                  pl.BlockSpec((None, 1, D), lambda j: (l, 0, j))],
        out_specs=pl.BlockSpec((8, D), lambda j: (0, j)),
        out_shape=jax.ShapeDtypeStruct((8, 6 * D), F32),
        compiler_params=_cp("arbitrary"),
        name="modulation",
    )(cvec, w_ada, b_ada.reshape(DEPTH, 1, 6 * D))
    return out.reshape(8, 6, D)


def _inproj_kernel(x_ref, mod_ref, w_ref, o_ref, g_ref, h_ref, *, n_main):
    j = pl.program_id(1)

    @pl.when(j == 0)
    def _():
        m = mod_ref[...]
        h_ref[...] = (x_ref[...] * (1.0 + m[1:2, :]) + m[0:1, :]).astype(BF16)

    y = _dot(h_ref[...], w_ref[...])

    @pl.when(j < n_main)
    def _():
        o_ref[...] = y

    @pl.when(j >= n_main)
    def _():
        g_ref[...] = y.astype(BF16)


def _in_proj(x, mod, w_p, mod_row, bm, bn=1024):
    T = x.shape[0]
    n_main = P_GATE // bn
    return pl.pallas_call(
        functools.partial(_inproj_kernel, n_main=n_main),
        grid=(T // bm, N_PROJ // bn),
        in_specs=[pl.BlockSpec((bm, D), lambda i, j: (i, 0)),
                  pl.BlockSpec((None, 6, D), lambda i, j: (mod_row(i), 0, 0)),
                  pl.BlockSpec((D, bn), lambda i, j: (0, j))],
        out_specs=[pl.BlockSpec((bm, bn), lambda i, j: (i, jnp.minimum(j, n_main - 1))),
                   pl.BlockSpec((bm, bn), lambda i, j: (i, jnp.maximum(j - n_main, 0)))],
        out_shape=[jax.ShapeDtypeStruct((T, P_GATE), F32), jax.ShapeDtypeStruct((T, N_PROJ - P_GATE), BF16)],
        scratch_shapes=[pltpu.VMEM((bm, D), BF16)],
        compiler_params=_cp("arbitrary", "arbitrary"),
        name="in_proj",
    )(x, mod, w_p)


def _mla_q_kernel(*refs, rope):
    if rope:
        (ql_ref, ckv_ref, kpe_ref, kper_ref, gq_ref, gkv_ref, wc_ref, wr_ref,
         cq_ref, sq_ref, ck_ref, sk_ref, q_ref, ckvn_ref, kpeo_ref) = refs
    else:
        ql_ref, ckv_ref, kpe_ref, gq_ref, gkv_ref, wc_ref, q_ref, ckvn_ref, kpeo_ref = refs
    qn = _rms_norm(ql_ref[...], gq_ref[...]).astype(BF16)
    q = _dot(qn, wc_ref[...])
    if rope:
        q = q * cq_ref[...] + _dot(qn, wr_ref[...]) * sq_ref[...]
        kpeo_ref[...] = kpe_ref[...] * ck_ref[...] + kper_ref[...] * sk_ref[...]
    else:
        kpeo_ref[...] = kpe_ref[...]
    q_ref[...] = (q * MLA_SCALE).astype(BF16)
    ckvn_ref[...] = _rms_norm(ckv_ref[...], gkv_ref[...])


def _mla_q(proj, gq, gkv, wcat, wrot, tabs, Lb, bm):
    T = proj.shape[0]
    rope = tabs is not None
    nl = Lb // bm
    col = lambda c: (lambda i: (i, c))
    fixed = lambda i: (0, 0)
    in_specs = [pl.BlockSpec((bm, 256), col(P_QLAT // 256)),
                pl.BlockSpec((bm, 128), col(P_CKV // 128)),
                pl.BlockSpec((bm, 128), col(P_KPE // 128))]
    args = [proj, proj, proj]
    if rope:
        in_specs.append(pl.BlockSpec((bm, 128), col(P_KPER // 128)))
        args.append(proj)
    in_specs += [pl.BlockSpec((1, 256), fixed), pl.BlockSpec((1, 128), fixed), pl.BlockSpec((256, 512), fixed)]
    args += [gq, gkv, wcat]
    if rope:
        cq, sq, ck, sk = tabs
        pos = lambda i: (i % nl, 0)
        in_specs += [pl.BlockSpec((256, 512), fixed), pl.BlockSpec((bm, 512), pos), pl.BlockSpec((bm, 512), pos),
                     pl.BlockSpec((bm, 128), pos), pl.BlockSpec((bm, 128), pos)]
        args += [wrot, cq, sq, ck, sk]
    return pl.pallas_call(
        functools.partial(_mla_q_kernel, rope=rope),
        grid=(T // bm,),
        in_specs=in_specs,
        out_specs=[pl.BlockSpec((bm, 512), lambda i: (i, 0)),
                   pl.BlockSpec((bm, 128), lambda i: (i, 0)),
                   pl.BlockSpec((bm, 128), lambda i: (i, 0))],
        out_shape=[jax.ShapeDtypeStruct((T, 512), BF16),
                   jax.ShapeDtypeStruct((T, 128), F32),
                   jax.ShapeDtypeStruct((T, 128), F32)],
        compiler_params=_cp("arbitrary"),
        name="mla_q",
    )(*args)


def _mla_kv_kernel(ckv_ref, kpe_ref, wk_ref, wv_ref, k_ref, v_ref):
    c = ckv_ref[...].astype(BF16)
    kpe = kpe_ref[...]
    k_ref[...] = (_dot(c, wk_ref[...]) + jnp.concatenate([kpe] * 4, axis=1)).astype(BF16)
    v_ref[...] = _dot(c, wv_ref[...]).astype(BF16)


def _mla_kv(ckv, kpe, wk, wv, bm):
    Tk = ckv.shape[0]
    return pl.pallas_call(
        _mla_kv_kernel,
        grid=(Tk // bm,),
        in_specs=[pl.BlockSpec((bm, 128), lambda i: (i, 0)),
                  pl.BlockSpec((bm, 128), lambda i: (i, 0)),
                  pl.BlockSpec((128, 512), lambda i: (0, 0)),
                  pl.BlockSpec((128, 256), lambda i: (0, 0))],
        out_specs=[pl.BlockSpec((bm, 512), lambda i: (i, 0)),
                   pl.BlockSpec((bm, 256), lambda i: (i, 0))],
        out_shape=[jax.ShapeDtypeStruct((Tk, 512), BF16),
                   jax.ShapeDtypeStruct((Tk, 256), BF16)],
        compiler_params=_cp("arbitrary"),
        name="mla_kv",
    )(ckv, kpe, wk, wv)


def _attn_core(q, kvs, masks, sink):
    ss = []
    for (k, _), mk in zip(kvs, masks):
        s = _dot_nt(q, k)
        if mk is not None:
            s = s + mk[1] if mk[0] == "add" else jnp.where(mk[1], s, NEG)
        ss.append(s)
    m = ss[0].max(-1, keepdims=True)
    for s in ss[1:]:
        m = jnp.maximum(m, s.max(-1, keepdims=True))
    if sink is not None:
        m = jnp.maximum(m, sink)
    den = None
    acc = None
    for s, (_, v) in zip(ss, kvs):
        p = jnp.exp(s - m)
        d = p.sum(-1, keepdims=True)
        a = _dot(p.astype(BF16), v)
        den = d if den is None else den + d
        acc = a if acc is None else acc + a
    if sink is not None:
        den = den + jnp.exp(sink - m)
    return acc / den


def _ctx_attn_kernel(qm_ref, km_ref, vm_ref, wq_ref, wk_ref, wv_ref, nq_ref, nk_ref, nv_ref, sink_ref, *rest):
    om_ref, ow_ref, on_ref = rest[-3:]
    for h in range(4):
        q = qm_ref[:, 128 * h:128 * (h + 1)]
        k = km_ref[:, 128 * h:128 * (h + 1)]
        v = vm_ref[:, 64 * h:64 * (h + 1)]
        om_ref[:, 64 * h:64 * (h + 1)] = _attn_core(q, [(k, v)], [None], None)
    for h in range(4):
        g = h // 2
        q = (wq_ref[:, 64 * h:64 * (h + 1)] * ATT_SCALE).astype(BF16)
        k = wk_ref[:, 64 * g:64 * (g + 1)].astype(BF16)
        v = wv_ref[:, 64 * g:64 * (g + 1)].astype(BF16)
        ow_ref[:, 64 * h:64 * (h + 1)] = _attn_core(q, [(k, v)], [None], sink_ref[h])
    for h in range(4):
        q = (nq_ref[:, 64 * h:64 * (h + 1)] * ATT_SCALE).astype(BF16)
        k = nk_ref[:, 64 * h:64 * (h + 1)].astype(BF16)
        v = nv_ref[:, 64 * h:64 * (h + 1)].astype(BF16)
        on_ref[:, 64 * h:64 * (h + 1)] = _attn_core(q, [(k, v)], [None], None)


def _ctx_attention(proj, q_all, k_all, v_all, sink, NB, Lb, after=None):
    T = proj.shape[0]
    pc = lambda w, off: pl.BlockSpec((Lb, w), lambda b: (b, off // w))
    row = lambda w: pl.BlockSpec((Lb, w), lambda b: (b, 0))
    dep_specs, dep_args = _after(after)
    return pl.pallas_call(
        _ctx_attn_kernel,
        grid=(NB,),
        in_specs=[row(512), row(512), row(256),
                  pc(256, P_WQ), pc(128, P_WK), pc(128, P_WV),
                  pc(256, P_NQ), pc(256, P_NK), pc(256, P_NV),
                  pl.BlockSpec(memory_space=pltpu.SMEM)] + dep_specs,
        out_specs=[row(256), row(256), row(256)],
        out_shape=[jax.ShapeDtypeStruct((T, 256), F32)] * 3,
        compiler_params=_cp("arbitrary"),
        name="ctx_attention",
    )(q_all, k_all, v_all, proj, proj, proj, proj, proj, proj, sink, *dep_args)


def _lat_mla_kernel(q_ref, k_ref, v_ref, *rest):
    o_ref = rest[-1]
    for h in range(4):
        q = q_ref[:, 128 * h:128 * (h + 1)]
        k = k_ref[:, 128 * h:128 * (h + 1)]
        v = v_ref[:, 64 * h:64 * (h + 1)]
        o_ref[:, 64 * h:64 * (h + 1)] = _attn_core(q, [(k, v)], [None], None)


def _after(after):
    deps = [] if after is None else list(after) if isinstance(after, (tuple, list)) else [after]
    return [pl.BlockSpec(memory_space=pl.ANY)] * len(deps), deps


def _lat_mla_attention(q_all, k_all, v_all, NB, Lb, Lk, tq, after=None):
    T = q_all.shape[0]
    nq = Lb // tq
    dep_specs, dep_args = _after(after)
    return pl.pallas_call(
        _lat_mla_kernel,
        grid=(NB, nq),
        in_specs=[pl.BlockSpec((tq, 512), lambda b, i: (b * nq + i, 0)),
                  pl.BlockSpec((Lk, 512), lambda b, i: (b, 0)),
                  pl.BlockSpec((Lk, 256), lambda b, i: (b, 0))] + dep_specs,
        out_specs=pl.BlockSpec((tq, 256), lambda b, i: (b * nq + i, 0)),
        out_shape=jax.ShapeDtypeStruct((T, 256), F32),
        compiler_params=_cp("arbitrary", "arbitrary"),
        name="lat_mla_attention",
    )(q_all, k_all, v_all, *dep_args)


def _attn_local_ctx(q, locs, kc, vc, sink):
    s_ctx = _dot_nt(q, kc)
    m_ctx = s_ctx.max(-1, keepdims=True)
    if sink is not None:
        m_ctx = jnp.maximum(m_ctx, sink)
    ms, dens, accs = [], [], []
    for rs, k, v, mk in locs:
        s = _dot_nt(q[rs], k)
        s = s + mk[1] if mk[0] == "add" else jnp.where(mk[1], s, NEG)
        m = jnp.maximum(s.max(-1, keepdims=True), m_ctx[rs])
        p = jnp.exp(s - m)
        ms.append(m)
        dens.append(p.sum(-1, keepdims=True))
        accs.append(_dot(p.astype(BF16), v))
    m = jnp.concatenate(ms, axis=0)
    p = jnp.exp(s_ctx - m)
    den = jnp.concatenate(dens, axis=0) + p.sum(-1, keepdims=True)
    if sink is not None:
        den = den + jnp.exp(sink - m)
    return (jnp.concatenate(accs, axis=0) + _dot(p.astype(BF16), vc)) / den


def _lat_win_kernel(q_ref, qr_ref, k_ref, kr_ref, v_ref, kc_ref, vc_ref, cq_ref, sq_ref, ck_ref, sk_ref,
                    sink_ref, o_ref, *, Lb, bpt):
    t = pl.program_id(1)
    q = (q_ref[...] * cq_ref[...] + qr_ref[...] * sq_ref[...]) * ATT_SCALE
    kc = kc_ref[...].astype(BF16)
    vc = vc_ref[...].astype(BF16)
    blocks = []
    for bb in range(bpt):
        i = t * bpt + bb
        start = pl.multiple_of(jnp.clip((i - 1) * 128, 0, Lb - 384), 128)
        win = pl.ds(start, 384)
        kk = (k_ref[win, :] * ck_ref[win, :] + kr_ref[win, :] * sk_ref[win, :]).astype(BF16)
        qpos = i * 128 + lax.broadcasted_iota(jnp.int32, (128, 384), 0)
        kpos = start + lax.broadcasted_iota(jnp.int32, (128, 384), 1)
        blocks.append((kk, v_ref[win, :].astype(BF16), jnp.abs(qpos - kpos) <= 128))
    for h in range(4):
        g = h // 2
        sl = slice(64 * g, 64 * (g + 1))
        locs = [(slice(128 * bb, 128 * (bb + 1)), kk[:, sl], vv[:, sl], ("keep", valid))
                for bb, (kk, vv, valid) in enumerate(blocks)]
        qh = q[:, 64 * h:64 * (h + 1)].astype(BF16)
        o_ref[:, 64 * h:64 * (h + 1)] = _attn_local_ctx(qh, locs, kc[:, sl], vc[:, sl], sink_ref[h])


def _lat_win_attention(proj, kc, vc, tabs, sink, NB, Lb):
    T = proj.shape[0]
    bpt = 1
    tq = 128 * bpt
    nt = Lb // tq
    Lc = kc.shape[1]
    cq, sq, ck, sk = tabs
    qspec = lambda off: pl.BlockSpec((tq, 256), lambda b, i: (b * nt + i, off // 256))
    kspec = lambda off: pl.BlockSpec((Lb, 128), lambda b, i: (b, off // 128))
    cspec = pl.BlockSpec((None, Lc, 128), lambda b, i: (b, 0, 0))
    return pl.pallas_call(
        functools.partial(_lat_win_kernel, Lb=Lb, bpt=bpt),
        grid=(NB, nt),
        in_specs=[qspec(P_WQ), qspec(P_WQR), kspec(P_WK), kspec(P_WKR), kspec(P_WV), cspec, cspec,
                  pl.BlockSpec((tq, 256), lambda b, i: (i, 0)), pl.BlockSpec((tq, 256), lambda b, i: (i, 0)),
                  pl.BlockSpec((Lb, 128), lambda b, i: (0, 0)), pl.BlockSpec((Lb, 128), lambda b, i: (0, 0)),
                  pl.BlockSpec(memory_space=pltpu.SMEM)],
        out_specs=pl.BlockSpec((tq, 256), lambda b, i: (b * nt + i, 0)),
        out_shape=jax.ShapeDtypeStruct((T, 256), F32),
        compiler_params=_cp("arbitrary", "arbitrary"),
        name="lat_win_attention",
    )(proj, proj, proj, proj, proj, kc, vc, cq, sq, ck, sk, sink)


def _na_bias_kernel(rpb_ref, o_ref):
    h = pl.program_id(0)
    qc = lax.broadcasted_iota(jnp.int32, (GRID_W, GRID_W), 0)
    kc = lax.broadcasted_iota(jnp.int32, (GRID_W, GRID_W), 1)
    dc = kc - qc + (NA_KW - 1)
    wstart = jnp.clip(qc - NA_KW // 2, 0, GRID_W - NA_KW)
    ok = (kc >= wstart) & (kc < wstart + NA_KW)
    n_dc = 2 * NA_KW - 1
    n_dr = 2 * NA_KH - 1
    tabs = []
    for dr in range(n_dr):
        t = jnp.zeros((GRID_W, GRID_W), F32)
        for j in range(n_dc):
            t = jnp.where(dc == j, rpb_ref[(h * n_dr + dr) * n_dc + j], t)
        tabs.append(jnp.where(ok, t, NEG))
    for o in range(NA_KH):
        for a in range(NA_KH):
            o_ref[o, :, GRID_W * a:GRID_W * (a + 1)] = tabs[a + NA_KH - 1 - o]


def _na_bias(rpb):
    H = rpb.shape[0]
    return pl.pallas_call(
        _na_bias_kernel,
        grid=(H,),
        in_specs=[pl.BlockSpec(memory_space=pltpu.SMEM)],
        out_specs=pl.BlockSpec((None, NA_KH, GRID_W, NA_KH * GRID_W), lambda h: (h, 0, 0, 0)),
        out_shape=jax.ShapeDtypeStruct((H, NA_KH, GRID_W, NA_KH * GRID_W), F32),
        compiler_params=_cp("arbitrary"),
        name="na_bias",
    )(rpb.reshape(-1))


def _lat_na_kernel(q_ref, k_ref, v_ref, kc_ref, vc_ref, bias_ref, o_ref, *, rows, rpt):
    t = pl.program_id(1)
    q = q_ref[...] * ATT_SCALE
    kc = kc_ref[...].astype(BF16)
    vc = vc_ref[...].astype(BF16)
    bands = []
    for rr in range(rpt):
        r = t * rpt + rr
        first = jnp.clip(r - NA_KH // 2, 0, rows - NA_KH)
        win = pl.ds(pl.multiple_of(first * GRID_W, GRID_W), NA_KH * GRID_W)
        bands.append((k_ref[win, :].astype(BF16), v_ref[win, :].astype(BF16), r - first))
    for h in range(4):
        sl = slice(64 * h, 64 * (h + 1))
        locs = [(slice(GRID_W * rr, GRID_W * (rr + 1)), kk[:, sl], vv[:, sl], ("add", bias_ref[h, off]))
                for rr, (kk, vv, off) in enumerate(bands)]
        o_ref[:, sl] = _attn_local_ctx(q[:, sl].astype(BF16), locs, kc[:, sl], vc[:, sl], None)


def _lat_na_attention(proj, kc, vc, bias, NB, Lb):
    T = proj.shape[0]
    rows = Lb // GRID_W
    rpt = 8
    tq = GRID_W * rpt
    nt = rows // rpt
    Lc = kc.shape[1]
    kspec = lambda off: pl.BlockSpec((Lb, 256), lambda b, t: (b, off // 256))
    cspec = pl.BlockSpec((None, Lc, 256), lambda b, t: (b, 0, 0))
    return pl.pallas_call(
        functools.partial(_lat_na_kernel, rows=rows, rpt=rpt),
        grid=(NB, nt),
        in_specs=[pl.BlockSpec((tq, 256), lambda b, t: (b * nt + t, P_NQ // 256)),
                  kspec(P_NK), kspec(P_NV), cspec, cspec,
                  pl.BlockSpec((4, NA_KH, GRID_W, NA_KH * GRID_W), lambda b, t: (0, 0, 0, 0))],
        out_specs=pl.BlockSpec((tq, 256), lambda b, t: (b * nt + t, 0)),
        out_shape=jax.ShapeDtypeStruct((T, 256), F32),
        compiler_params=_cp("arbitrary", "arbitrary"),
        name="lat_na_attention",
    )(proj, proj, proj, kc, vc, bias)


def _short_conv_kernel(a_ref, b_ref, c_ref, w_ref, bias_ref, oa_ref, ob_ref, oc_ref, *, L):
    t = lax.broadcasted_iota(jnp.int32, (L, HY_C), 0)
    for n, (x_ref, o_ref) in enumerate(((a_ref, oa_ref), (b_ref, ob_ref), (c_ref, oc_ref))):
        sl = slice(HY_C * n, HY_C * (n + 1))
        x = x_ref[...]
        prev = jnp.where(t == 0, 0.0, pltpu.roll(x, 1, axis=0))
        nxt = jnp.where(t == L - 1, 0.0, pltpu.roll(x, L - 1, axis=0))
        o_ref[...] = prev * w_ref[0:1, sl] + x * w_ref[1:2, sl] + nxt * w_ref[2:3, sl] + bias_ref[:, sl]


def _short_conv(proj, w, b, NB, Lb):
    T = proj.shape[0]
    spec = lambda c: pl.BlockSpec((Lb, HY_C), lambda i: (i, c))
    return pl.pallas_call(
        functools.partial(_short_conv_kernel, L=Lb),
        grid=(NB,),
        in_specs=[spec(P_HY // HY_C), spec(P_HY // HY_C + 1), spec(P_HY // HY_C + 2),
                  pl.BlockSpec((3, 3 * HY_C), lambda i: (0, 0)), pl.BlockSpec((1, 3 * HY_C), lambda i: (0, 0))],
        out_specs=[spec(0)] * 3,
        out_shape=[jax.ShapeDtypeStruct((T, HY_C), F32)] * 3,
        compiler_params=_cp("arbitrary"),
        name="hyena_short_conv",
    )(proj, proj, proj, w, b)


def _hy_filter_kernel(w1_ref, b1_ref, w2_ref, b2_ref, w3_ref, freq_ref, ld_ref, fs_ref, nyq_ref, *, L):
    ti = lax.broadcasted_iota(jnp.int32, (L, 128), 0)
    t = ti.astype(F32)
    j = lax.broadcasted_iota(jnp.int32, (L, 128), 1)
    band = jnp.where(j <= HY_BANDS, j - 1, j - 1 - HY_BANDS).astype(F32)
    ang = (2.0 * math.pi / L) * t * band
    tn = t / L
    z = jnp.where(j == 0, tn, jnp.where(j <= HY_BANDS, jnp.cos(ang),
                                        jnp.where(j <= 2 * HY_BANDS, -jnp.sin(ang), 0.0)))
    a = jnp.sin(freq_ref[0:1, :] * (_dot_hi(z, w1_ref[...]) + b1_ref[...]))
    a = jnp.sin(freq_ref[1:2, :] * (_dot_hi(a, w2_ref[...]) + b2_ref[...]))
    filt = _dot_hi(a, w3_ref[...])
    tcol = lax.broadcasted_iota(jnp.int32, (L, 4 * HY_C), 0)
    filt = filt * jnp.exp(-(tcol.astype(F32) / L) * jnp.exp(ld_ref[...]))
    t1 = lax.broadcasted_iota(jnp.int32, (L, HY_C), 0)
    sign = jnp.where(t1 % 2 == 0, 1.0, -1.0)
    for n in range(2):
        fwd = filt[:, 2 * HY_C * n:2 * HY_C * n + HY_C]
        bwd = jnp.where(t1 == 0, 0.0, filt[:, 2 * HY_C * n + HY_C:2 * HY_C * (n + 1)])
        tot = fwd + bwd
        fs_ref[:, HY_C * n:HY_C * (n + 1)] = tot
        fs_ref[:, 2 * HY_C + HY_C * n:2 * HY_C + HY_C * (n + 1)] = fwd - bwd
        nyq_ref[:, HY_C * n:HY_C * (n + 1)] = (tot * sign).sum(0, keepdims=True)


def _hy_filter(L, w1p, b1, w2, b2, w3, freq, ld):
    full = lambda s: pl.BlockSpec(s, lambda: tuple(0 for _ in s))
    return pl.pallas_call(
        functools.partial(_hy_filter_kernel, L=L),
        in_specs=[full((128, 64)), full((1, 64)), full((64, 64)), full((1, 64)), full((64, 4 * HY_C)),
                  full((2, 64)), full((1, 4 * HY_C))],
        out_specs=[full((L, 4 * HY_C)), full((1, 2 * HY_C))],
        out_shape=[jax.ShapeDtypeStruct((L, 4 * HY_C), F32), jax.ShapeDtypeStruct((1, 2 * HY_C), F32)],
        compiler_params=pltpu.CompilerParams(vmem_limit_bytes=VMEM_LIMIT),
        name="hyena_filter",
    )(w1p, b1, w2, b2, w3, freq, ld)


def _hy_gdft_kernel(cm_ref, sm_ref, fs_ref, nyq_ref, gr_ref, gi_ref, *, tm):
    m = pl.program_id(0)
    f = fs_ref[...].astype(BF16)
    gr_ref[...] = _dot(cm_ref[...], f[:, :2 * HY_C])
    gi = _dot(sm_ref[...], f[:, 2 * HY_C:])
    row = m * tm + lax.broadcasted_iota(jnp.int32, (tm, 2 * HY_C), 0)
    gi_ref[...] = jnp.where(row == 0, nyq_ref[...], gi)


def _hy_gdft(cm, sm, fs, nyq, L, tm):
    return pl.pallas_call(
        functools.partial(_hy_gdft_kernel, tm=tm),
        grid=(L // tm,),
        in_specs=[pl.BlockSpec((tm, L), lambda m: (m, 0)), pl.BlockSpec((tm, L), lambda m: (m, 0)),
                  pl.BlockSpec((L, 4 * HY_C), lambda m: (0, 0)), pl.BlockSpec((1, 2 * HY_C), lambda m: (0, 0))],
        out_specs=[pl.BlockSpec((tm, 2 * HY_C), lambda m: (m, 0))] * 2,
        out_shape=[jax.ShapeDtypeStruct((L, 2 * HY_C), F32)] * 2,
        compiler_params=_cp("arbitrary"),
        name="hyena_filter_dft",
    )(cm, sm, fs, nyq)


def _hy_fwd_kernel(cm_ref, sm_ref, z_ref, gr_ref, gi_ref, yr_ref, yi_ref, *, L, tm, ns):
    m = pl.program_id(1)
    gr = gr_ref[...]
    gi = gi_ref[...]
    row0 = (m * tm + lax.broadcasted_iota(jnp.int32, (tm, HY_C), 0)) == 0
    s = jnp.where(row0, 0.5 / L, 1.0 / L)
    for g in range(ns):
        zb = z_ref[g * L:(g + 1) * L, :].astype(BF16)
        zr = _dot(cm_ref[...], zb)
        zi = _dot(sm_ref[...], zb)
        zigi = zi * gi
        yr_ref[g * tm:(g + 1) * tm, :] = ((zr * gr - jnp.where(row0, 0.0, zigi)) * s).astype(BF16)
        yi_ref[g * tm:(g + 1) * tm, :] = (jnp.where(row0, zigi, zr * gi + zi * gr) * s).astype(BF16)


def _hy_seqs_per_step(NB, Lb, tm):
    ns = max(1, 2048 // Lb) if tm == Lb else 1
    while NB % ns:
        ns //= 2
    return ns


def _hy_fwd(cm, sm, z, gr, gi, n, NB, Lb, tm):
    T = z.shape[0]
    nm = Lb // tm
    ns = _hy_seqs_per_step(NB, Lb, tm)
    return pl.pallas_call(
        functools.partial(_hy_fwd_kernel, L=Lb, tm=tm, ns=ns),
        grid=(NB // ns, nm),
        in_specs=[pl.BlockSpec((tm, Lb), lambda b, m: (m, 0)), pl.BlockSpec((tm, Lb), lambda b, m: (m, 0)),
                  pl.BlockSpec((ns * Lb, HY_C), lambda b, m: (b, 0)),
                  pl.BlockSpec((tm, HY_C), lambda b, m: (m, n)), pl.BlockSpec((tm, HY_C), lambda b, m: (m, n))],
        out_specs=[pl.BlockSpec((ns * tm, HY_C), lambda b, m: (b * nm + m, 0))] * 2,
        out_shape=[jax.ShapeDtypeStruct((T, HY_C), BF16)] * 2,
        compiler_params=_cp("arbitrary", "arbitrary"),
        name="hyena_fwd_dft",
    )(cm, sm, z, gr, gi)


def _hy_inv_kernel(cm_ref, smt_ref, yr_ref, yi_ref, z_ref, g_ref, skip_ref, o_ref, *, L, tm, ns):
    for g in range(ns):
        seq = slice(g * L, (g + 1) * L)
        out = slice(g * tm, (g + 1) * tm)
        conv = _dot(cm_ref[...], yr_ref[seq, :]) + _dot(smt_ref[...], yi_ref[seq, :])
        o_ref[out, :] = g_ref[out, :] * (conv + skip_ref[...] * z_ref[out, :])


def _hy_inv(cm, smt, yr, yi, z, gate, skip, n, NB, Lb, tm):
    T = z.shape[0]
    nm = Lb // tm
    ns = _hy_seqs_per_step(NB, Lb, tm)
    tile = pl.BlockSpec((ns * tm, HY_C), lambda b, m: (b * nm + m, 0))
    seq = pl.BlockSpec((ns * Lb, HY_C), lambda b, m: (b, 0))
    return pl.pallas_call(
        functools.partial(_hy_inv_kernel, L=Lb, tm=tm, ns=ns),
        grid=(NB // ns, nm),
        in_specs=[pl.BlockSpec((tm, Lb), lambda b, m: (m, 0)), pl.BlockSpec((tm, Lb), lambda b, m: (m, 0)),
                  seq, seq, tile, tile, pl.BlockSpec((None, 1, HY_C), lambda b, m: (n, 0, 0))],
        out_specs=tile,
        out_shape=jax.ShapeDtypeStruct((T, HY_C), F32),
        compiler_params=_cp("arbitrary", "arbitrary"),
        name="hyena_inv_dft",
    )(cm, smt, yr, yi, z, gate, skip)


def _dft_mats(L):
    k = jnp.arange(L, dtype=jnp.int32)
    blk = 64

    def trig(mult):
        ang = ((mult[:, None] * k[None, :]) % (2 * L)).astype(F32) * (math.pi / L)
        return jnp.cos(ang), jnp.sin(ang)

    ca, sa = trig(jnp.arange(L // blk, dtype=jnp.int32) * blk)
    cb, sb = trig(jnp.arange(blk, dtype=jnp.int32))
    cm = (ca[:, None, :] * cb[None] - sa[:, None, :] * sb[None]).reshape(L, L)
    s = -(sa[:, None, :] * cb[None] + ca[:, None, :] * sb[None]).reshape(L, L)
    alt = jnp.where(k % 2 == 0, 1.0, -1.0).astype(F32)
    sm = jnp.where(k[:, None] == 0, alt[None, :], s)
    smt = jnp.where(k[None, :] == 0, alt[:, None], s)
    return cm.astype(BF16), sm.astype(BF16), smt.astype(BF16)


def _merge_kernel(oa_ref, ob_ref, oc_ref, od_ref, g0_ref, g1_ref, g2_ref, g3_ref, wb_ref, wo_ref, x_ref, mod_ref,
                  lg_ref, lb_ref, rt_ref, *rest):
    x1_ref, hp_ref, logit_ref = rest[-3:]
    acc = None
    for o_ref, g_ref, i in ((oa_ref, g0_ref, 0), (ob_ref, g1_ref, 1), (oc_ref, g2_ref, 2), (od_ref, g3_ref, 3)):
        y = _sigmoid(g_ref[...].astype(F32)) * _dot(o_ref[...].astype(BF16), wb_ref[i])
        acc = y if acc is None else acc + y
    mix = _dot(acc.astype(BF16), wo_ref[...])
    m = mod_ref[...]
    x1 = _layer_norm(DN_ALPHA * x_ref[...] + m[2:3, :] * mix, lg_ref[...], lb_ref[...])
    x1_ref[...] = x1
    h2 = x1 * (1.0 + m[4:5, :]) + m[3:4, :]
    hp_ref[...] = _pack_pairs(h2)
    hi = h2.astype(BF16)
    lo = (h2 - hi.astype(F32)).astype(BF16)
    r = rt_ref[...]
    r_hi = r.astype(BF16)
    r_lo = (r - r_hi.astype(F32)).astype(BF16)
    both = _dot_nt(jnp.concatenate([r_hi, r_lo], axis=0), hi)
    logit_ref[...] = both[:N_EXPERTS, :] + both[N_EXPERTS:, :] + _dot_nt(r_hi, lo)


def _merge(outs, gates, wb, wo, x, mod, lg, lb, router, mod_row, bm, after=None):
    T = x.shape[0]
    row = lambda w: pl.BlockSpec((bm, w), lambda i: (i, 0))
    gspec = lambda n: pl.BlockSpec((bm, D), lambda i: (i, n))
    fixed2 = lambda s: pl.BlockSpec(s, lambda i: (0, 0))
    dep_specs, dep_args = _after(after)
    return pl.pallas_call(
        _merge_kernel,
        grid=(T // bm,),
        in_specs=[row(256)] * 4 + [gspec(0), gspec(1), gspec(2), gspec(3),
                                   pl.BlockSpec((4, 256, D), lambda i: (0, 0, 0)), fixed2((D, D)), row(D),
                                   pl.BlockSpec((None, 6, D), lambda i: (mod_row(i), 0, 0)),
                                   fixed2((1, D)), fixed2((1, D)), fixed2((N_EXPERTS, D))] + dep_specs,
        out_specs=[row(D), row(D // 2), pl.BlockSpec((N_EXPERTS, bm), lambda i: (0, i))],
        out_shape=[jax.ShapeDtypeStruct((T, D), F32), jax.ShapeDtypeStruct((T, D // 2), jnp.int32),
                   jax.ShapeDtypeStruct((N_EXPERTS, T), F32)],
        compiler_params=_cp("arbitrary"),
        name="merge_norm",
    )(*outs, gates, gates, gates, gates, wb, wo, x, mod, lg, lb, router, *dep_args)


def _router_kernel(logit_ref, bias_ref, g_ref, rank_ref, cnt_ref, *, tt):
    per = N_EXPERTS // N_GROUPS
    scores = _sigmoid(logit_ref[...])
    sel = (scores + bias_ref[...]).reshape(N_GROUPS, per, tt)
    gid = lax.broadcasted_iota(jnp.int32, (N_GROUPS, per, tt), 0).astype(F32)
    jid = lax.broadcasted_iota(jnp.int32, (N_GROUPS, per, tt), 1).astype(F32)
    eid = gid * per + jid
    ninf = -jnp.inf
    m1 = sel.max(1, keepdims=True)
    i1 = jnp.where(sel == m1, jid, float(per)).min(1, keepdims=True)
    m2 = jnp.where(jid == i1, ninf, sel).max(1, keepdims=True)
    gs = m1 + m2
    g1 = lax.broadcasted_iota(jnp.int32, (N_GROUPS, 1, tt), 0).astype(F32)
    chosen = jnp.zeros((N_GROUPS, 1, tt), F32)
    for _ in range(TOPK_GROUPS):
        mx = gs.max(0, keepdims=True)
        gi = jnp.where(gs == mx, g1, float(N_GROUPS)).min(0, keepdims=True)
        pick = g1 == gi
        chosen = jnp.where(pick, 1.0, chosen)
        gs = jnp.where(pick, ninf, gs)
    cand = jnp.where(chosen > 0.0, sel, NEG)
    picked = jnp.zeros((N_GROUPS, per, tt), F32)
    for _ in range(TOP_K):
        mx = cand.max(1, keepdims=True).max(0, keepdims=True)
        ei = jnp.where(cand == mx, eid, float(N_EXPERTS)).min(1, keepdims=True).min(0, keepdims=True)
        pick = eid == ei
        picked = jnp.where(pick, 1.0, picked)
        cand = jnp.where(pick, ninf, cand)
    w = scores.reshape(N_GROUPS, per, tt) * picked
    wsum = w.sum(1, keepdims=True).sum(0, keepdims=True)
    g_ref[...] = (w / wsum * ROUTED_SCALE).reshape(N_EXPERTS, tt)
    pk = picked.reshape(N_EXPERTS, tt)
    t_in = lax.broadcasted_iota(jnp.int32, (tt, tt), 0)
    t_out = lax.broadcasted_iota(jnp.int32, (tt, tt), 1)
    upper = jnp.where(t_in <= t_out, 1.0, 0.0).astype(BF16)

    @pl.when(pl.program_id(0) == 0)
    def _():
        cnt_ref[...] = jnp.zeros_like(cnt_ref)

    before = cnt_ref[:, 0:1]
    rank_ref[...] = jnp.where(pk > 0.0, before + _dot(pk.astype(BF16), upper) - 1.0, -1.0)
    cnt_ref[...] += pk.sum(-1, keepdims=True)


def _router(logits_t, bias, tt):
    T = logits_t.shape[1]
    tile = pl.BlockSpec((N_EXPERTS, tt), lambda i: (0, i))
    return pl.pallas_call(
        functools.partial(_router_kernel, tt=tt),
        grid=(T // tt,),
        in_specs=[tile, pl.BlockSpec((N_EXPERTS, 1), lambda i: (0, 0))],
        out_specs=[tile, tile, pl.BlockSpec((N_EXPERTS, 128), lambda i: (0, 0))],
        out_shape=[jax.ShapeDtypeStruct((N_EXPERTS, T), F32), jax.ShapeDtypeStruct((N_EXPERTS, T), F32),
                   jax.ShapeDtypeStruct((N_EXPERTS, 128), F32)],
        compiler_params=_cp("arbitrary"),
        name="moe_router",
    )(logits_t, bias)


def _route_pos_kernel(gate_ref, rank_ref, cnt_ref, pos_ref, w_ref, te_ref, nx_ref, nt_ref, *, tm, nt_max):
    ei = lax.broadcasted_iota(jnp.int32, (N_EXPERTS, N_EXPERTS), 0)
    ej = lax.broadcasted_iota(jnp.int32, (N_EXPERTS, N_EXPERTS), 1)
    below = jnp.where(ej < ei, 1.0, 0.0)
    padded = jnp.ceil(cnt_ref[...] * (1.0 / tm)) * tm
    offs = _dot_hi(below, padded)
    rank = rank_ref[...]
    routed = rank >= 0.0
    pos = offs[:, 0:1] + rank
    slot = _dot(below.astype(BF16), jnp.where(routed, 1.0, 0.0).astype(BF16))
    gate = gate_ref[...]
    for k in range(TOP_K):
        mine = routed & (slot == float(k))
        pos_ref[k:k + 1, :] = jnp.where(mine, pos, 0.0).sum(0, keepdims=True).astype(jnp.int32)
        w_ref[k:k + 1, :] = jnp.where(mine, gate, 0.0).sum(0, keepdims=True)
    ends = (offs + padded)[:, 0:1]
    first = (lax.broadcasted_iota(jnp.int32, (N_EXPERTS, nt_max), 1) * tm).astype(F32)
    te = jnp.minimum(jnp.where(ends <= first, 1.0, 0.0).sum(0, keepdims=True), N_EXPERTS - 1.0)
    te_ref[...] = te.astype(jnp.int32)
    eid = lax.broadcasted_iota(jnp.int32, (N_EXPERTS, nt_max), 0).astype(F32)
    nx_ref[...] = (jnp.where(eid == te, ends, 0.0).sum(0, keepdims=True) * (1.0 / tm)).astype(jnp.int32)
    nt_ref[...] = (padded.sum(0, keepdims=True) * (1.0 / tm)).astype(jnp.int32)


def _route_pos(gate_t, rank, cnt, tt, tm, nt_max):
    T = gate_t.shape[1]
    tile = pl.BlockSpec((N_EXPERTS, tt), lambda i: (0, i))
    out = pl.BlockSpec((TOP_K, tt), lambda i: (0, i))
    return pl.pallas_call(
        functools.partial(_route_pos_kernel, tm=tm, nt_max=nt_max),
        grid=(T // tt,),
        in_specs=[tile, tile, pl.BlockSpec((N_EXPERTS, 128), lambda i: (0, 0))],
        out_specs=[out, out, pl.BlockSpec((1, nt_max), lambda i: (0, 0)), pl.BlockSpec((1, nt_max), lambda i: (0, 0)),
                   pl.BlockSpec((1, 128), lambda i: (0, 0))],
        out_shape=[jax.ShapeDtypeStruct((TOP_K, T), jnp.int32), jax.ShapeDtypeStruct((TOP_K, T), F32),
                   jax.ShapeDtypeStruct((1, nt_max), jnp.int32), jax.ShapeDtypeStruct((1, nt_max), jnp.int32),
                   jax.ShapeDtypeStruct((1, 128), jnp.int32)],
        compiler_params=_cp("arbitrary"),
        name="moe_positions",
    )(gate_t, rank, cnt)


def _gmm_kernel(te_ref, nx_ref, nt_ref, xs_ref, w1_hbm, w3_hbm, w2_hbm, *rest, l):
    ys_ref, b1_ref, b3_ref, b2_ref, f1_ref, f3_ref, f2_ref, seg_ref, sem = rest[-9:]
    j = pl.program_id(0)
    live = j < nt_ref[0]
    new_expert = (j == 0) | (te_ref[j] != te_ref[jnp.maximum(j - 1, 0)])

    def fetch(e, slot):
        return [pltpu.make_async_copy(w_hbm.at[l, e], f_ref.at[slot], sem.at[i, slot])
                for i, (w_hbm, f_ref) in enumerate(((w1_hbm, f1_ref), (w3_hbm, f3_ref), (w2_hbm, f2_ref)))]

    @pl.when(live & new_expert)
    def _():
        @pl.when(j == 0)
        def _():
            seg_ref[0] = 0
            for c in fetch(te_ref[0], 0):
                c.start()

        slot = lax.rem(seg_ref[0], 2)
        for c in fetch(te_ref[j], slot):
            c.wait()
        b1_ref[...] = f1_ref[slot].astype(BF16)
        b3_ref[...] = f3_ref[slot].astype(BF16)
        b2_ref[...] = f2_ref[slot].astype(BF16)
        nxt = nx_ref[j]

        @pl.when(nxt < nt_ref[0])
        def _():
            for c in fetch(te_ref[nxt], 1 - slot):
                c.start()

        seg_ref[0] = seg_ref[0] + 1

    @pl.when(live)
    def _():
        xa, xb = _unpack_pairs(xs_ref[...])
        xa, xb = xa.astype(BF16), xb.astype(BF16)
        half = D // 2
        a = _dot(xa, b1_ref[:half, :]) + _dot(xb, b1_ref[half:, :])
        b = _dot(xa, b3_ref[:half, :]) + _dot(xb, b3_ref[half:, :])
        hid = (a * _sigmoid(a) * b).astype(BF16)
        ys_ref[...] = _pack_pairs(_dot(hid, b2_ref[...]))


def _gmm(te, nx, nt, xs, w1, w3, w2, l, tm, after=None):
    n_slots = xs.shape[0]
    ds = D_EXPERT
    rows = pl.BlockSpec((tm, D // 2), lambda j, te, nx, nt: (jnp.minimum(j, nt[0] - 1), 0))
    hbm = pl.BlockSpec(memory_space=pl.ANY)
    dep_specs, dep_args = _after(after)
    return pl.pallas_call(
        functools.partial(_gmm_kernel, l=l),
        grid_spec=pltpu.PrefetchScalarGridSpec(
            num_scalar_prefetch=3,
            grid=(n_slots // tm,),
            in_specs=[rows, hbm, hbm, hbm] + dep_specs,
            out_specs=rows,
            scratch_shapes=[pltpu.VMEM((D, ds), BF16), pltpu.VMEM((D, ds), BF16), pltpu.VMEM((ds, D), BF16),
                            pltpu.VMEM((2, D, ds), F32), pltpu.VMEM((2, D, ds), F32), pltpu.VMEM((2, ds, D), F32),
                            pltpu.SMEM((1,), jnp.int32), pltpu.SemaphoreType.DMA((3, 2))]),
        out_shape=jax.ShapeDtypeStruct((n_slots, D // 2), jnp.int32),
        compiler_params=_cp("arbitrary"),
        name="moe_grouped_ffn",
    )(te, nx, nt, xs, w1, w3, w2, *dep_args)


def _combine_kernel(yk_ref, w_ref, hp_ref, s1_ref, s3_ref, s2_ref, x_ref, mod_ref, lg_ref, lb_ref, o_ref):
    w = w_ref[...]
    acc_a = acc_b = None
    for k in range(TOP_K):
        ya, yb = _unpack_pairs(yk_ref[k])
        wk = w[:, k:k + 1]
        acc_a = wk * ya if acc_a is None else acc_a + wk * ya
        acc_b = wk * yb if acc_b is None else acc_b + wk * yb
    ha, hb = _unpack_pairs(hp_ref[...])
    ha, hb = ha.astype(BF16), hb.astype(BF16)
    half = D // 2
    a = _dot(ha, s1_ref[:half, :]) + _dot(hb, s1_ref[half:, :])
    b = _dot(ha, s3_ref[:half, :]) + _dot(hb, s3_ref[half:, :])
    y = jnp.concatenate([acc_a, acc_b], axis=1) + _dot((a * _sigmoid(a) * b).astype(BF16), s2_ref[...])
    m = mod_ref[...]
    o_ref[...] = _layer_norm(DN_ALPHA * x_ref[...] + m[5:6, :] * y, lg_ref[...], lb_ref[...])


def _combine(yk, w, hp, s1, s3, s2, x1, mod, lg, lb, mod_row, bm):
    T = x1.shape[0]
    ds = D_EXPERT
    row = lambda n: pl.BlockSpec((bm, n), lambda i: (i, 0))
    fixed = lambda s: pl.BlockSpec(s, lambda i: (0, 0))
    return pl.pallas_call(
        _combine_kernel,
        grid=(T // bm,),
        in_specs=[pl.BlockSpec((TOP_K, bm, D // 2), lambda i: (0, i, 0)), row(TOP_K), row(D // 2),
                  fixed((D, ds)), fixed((D, ds)), fixed((ds, D)), row(D),
                  pl.BlockSpec((None, 6, D), lambda i: (mod_row(i), 0, 0)), fixed((1, D)), fixed((1, D))],
        out_specs=row(D),
        out_shape=jax.ShapeDtypeStruct((T, D), F32),
        compiler_params=_cp("arbitrary"),
        name="moe_combine_norm",
    )(yk, w, hp, s1, s3, s2, x1, mod, lg, lb)


def _sc_worker():
    return lax.axis_index("s") * SC_CORES + lax.axis_index("c")


def _sc_mesh():
    return plsc.VectorSubcoreMesh(core_axis_name="c", subcore_axis_name="s")


def _sc_gather(table, idx):
    N, W = idx.shape[0], table.shape[1]
    per_w = N // SC_WORKERS
    n_chunks = per_w // SC_ROWS

    def body(table_hbm, idx_hbm, out_hbm, idx_v, rows_v, sem):
        base = _sc_worker() * per_w
        pltpu.sync_copy(idx_hbm.at[pl.ds(base, per_w)], idx_v)

        @pl.loop(0, n_chunks)
        def _(c):
            off = pl.multiple_of(c * SC_ROWS, SC_ROWS)
            pltpu.async_copy(table_hbm.at[idx_v.at[pl.ds(off, SC_ROWS)]], rows_v, sem).wait()
            pltpu.sync_copy(rows_v, out_hbm.at[pl.ds(base + off, SC_ROWS)])

    return pl.kernel(
        body, out_type=jax.ShapeDtypeStruct((N, W), table.dtype), mesh=_sc_mesh(),
        scratch_types=[pltpu.VMEM((per_w,), jnp.int32), pltpu.VMEM((SC_ROWS, W), table.dtype),
                       pltpu.SemaphoreType.DMA],
        name="sc_gather",
    )(table, idx)


def _sc_dispatch(pos, table, n_slots):
    NP, (T, W) = pos.shape[0], table.shape
    per_w = n_slots // SC_WORKERS
    n_chunks = per_w // SC_ROWS
    scan = 8192

    def body(pos_hbm, table_hbm, out_hbm, pos_v, src_v, rows_v, sem):
        base = _sc_worker() * per_w
        lane = lax.iota(jnp.int32, SC_LANES)

        @pl.loop(0, per_w // SC_LANES)
        def _(j):
            o = pl.multiple_of(j * SC_LANES, SC_LANES)
            src_v[pl.ds(o, SC_LANES)] = (base + o + lane) & (T - 1)

        @pl.loop(0, NP // scan)
        def _(c):
            pltpu.sync_copy(pos_hbm.at[pl.ds(pl.multiple_of(c * scan, scan), scan)], pos_v)

            @pl.loop(0, scan // SC_LANES)
            def _(v):
                o = pl.multiple_of(v * SC_LANES, SC_LANES)
                p = pos_v[pl.ds(o, SC_LANES)] - base
                mine = (p >= 0) & (p < per_w)
                tok = (c * scan + o + lane) & (T - 1)
                plsc.store_scatter(src_v, [jnp.where(mine, p, 0)], tok, mask=mine)

        @pl.loop(0, n_chunks)
        def _(c):
            off = pl.multiple_of(c * SC_ROWS, SC_ROWS)
            pltpu.async_copy(table_hbm.at[src_v.at[pl.ds(off, SC_ROWS)]], rows_v, sem).wait()
            pltpu.sync_copy(rows_v, out_hbm.at[pl.ds(base + off, SC_ROWS)])

    return pl.kernel(
        body, out_type=jax.ShapeDtypeStruct((n_slots, W), table.dtype), mesh=_sc_mesh(),
        scratch_types=[pltpu.VMEM((scan,), jnp.int32), pltpu.VMEM((per_w,), jnp.int32),
                       pltpu.VMEM((SC_ROWS, W), table.dtype), pltpu.SemaphoreType.DMA],
        compiler_params=pltpu.CompilerParams(needs_layout_passes=False),
        name="sc_dispatch",
    )(pos, table)


def _caches_kernel(*refs, nb, S):
    n_in = 6 * DEPTH
    outs = refs[n_in:]
    l = pl.program_id(0)
    for a in range(DEPTH):
        @pl.when(l == a)
        def _(a=a):
            ckv, kpe, wk, wv, nk, nv = refs[6 * a:6 * (a + 1)]
            for g in range(nb):
                rows = slice(g * S, (g + 1) * S)
                outs[0][g] = ckv[rows, :]
                outs[1][g] = kpe[rows, 64:96]
                outs[2][g] = wk[rows, :]
                outs[3][g] = wv[rows, :]
                outs[4][g] = nk[rows, :]
                outs[5][g] = nv[rows, :]


def _emit_caches(projs, ckvs, B, S):
    nb = 4
    while B % nb:
        nb //= 2

    def layer_specs(a):
        row = lambda l, b: jnp.where(l == a, b, 0)
        col = lambda w, off: pl.BlockSpec((nb * S, w), lambda l, b: (row(l, b), off // w))
        return [pl.BlockSpec((nb * S, 128), lambda l, b: (row(l, b), 0)), col(128, P_KPE), col(128, P_WK),
                col(128, P_WV), col(256, P_NK), col(256, P_NV)]

    in_specs, args = [], []
    for a in range(DEPTH):
        in_specs += layer_specs(a)
        args += [ckvs[a]] + [projs[a]] * 5
    widths = (128, 32, 128, 128, 256, 256)
    return pl.pallas_call(
        functools.partial(_caches_kernel, nb=nb, S=S),
        grid=(DEPTH, B // nb),
        in_specs=in_specs,
        out_specs=[pl.BlockSpec((nb, None, S, w), lambda l, b: (b, l, 0, 0)) for w in widths],
        out_shape=[jax.ShapeDtypeStruct((B, DEPTH, S, w), F32) for w in widths],
        compiler_params=_cp("arbitrary", "arbitrary"),
        name="context_tensors",
    )(*args)


def _rot_cols(w, q):
    a, b, c, d = w[..., :q], w[..., q:2 * q], w[..., 2 * q:3 * q], w[..., 3 * q:]
    return jnp.concatenate([-b, a, -d, c], -1)


def _prep_w_in(w):
    z = lambda n: jnp.zeros((D, n), w.dtype)
    qlat, ckv, kpe, hy = w[:, 0:256], w[:, 256:384], w[:, 384:416], w[:, 416:1184]
    wq, wk, wv = w[:, 1184:1440], w[:, 1440:1568], w[:, 1568:1696]
    nq, nk, nv, gate = w[:, 1696:1952], w[:, 1952:2208], w[:, 2208:2464], w[:, 2464:]
    wq_r = _rot_cols(wq.reshape(D, 4, 64), 16).reshape(D, 256)
    wk_r = _rot_cols(wk.reshape(D, 2, 64), 16).reshape(D, 128)
    kpe_r = _rot_cols(kpe, 8)
    cols = [qlat, ckv, z(64), kpe, z(32), hy, wq, wk, wv, nq, nk, nv, wq_r, wk_r, z(64), kpe_r, z(32), gate]
    return jnp.concatenate(cols, 1).astype(BF16)


def _prep_mla(w_uq, w_ukv):
    uq = w_uq.reshape(256, 4, 96)
    nope, pe = uq[..., :64], uq[..., 64:]
    z32 = jnp.zeros((256, 4, 32), w_uq.dtype)
    z64 = jnp.zeros((256, 4, 64), w_uq.dtype)
    wcat = jnp.concatenate([nope, pe, z32], -1).reshape(256, 512).astype(BF16)
    wrot = jnp.concatenate([z64, _rot_cols(pe, 8), z32], -1).reshape(256, 512).astype(BF16)
    ukv = w_ukv.reshape(128, 4, 128)
    wk = jnp.concatenate([ukv[..., :64], jnp.zeros((128, 4, 64), w_ukv.dtype)], -1).reshape(128, 512).astype(BF16)
    wv = ukv[..., 64:].reshape(128, 256).astype(BF16)
    return wcat, wrot, wk, wv


def _rope_tab(L, q):
    t = jnp.arange(L)
    inv = ROPE_BASE ** (-jnp.arange(q, dtype=F32) / q)
    ar = (t // GRID_W).astype(F32)[:, None] * inv[None, :]
    ac = (t % GRID_W).astype(F32)[:, None] * inv[None, :]
    cos = jnp.concatenate([jnp.cos(ar), jnp.cos(ar), jnp.cos(ac), jnp.cos(ac)], 1)
    sin = jnp.concatenate([jnp.sin(ar), jnp.sin(ar), jnp.sin(ac), jnp.sin(ac)], 1)
    return cos, sin


def _rope_tables(L):
    c8, s8 = _rope_tab(L, 8)
    c16, s16 = _rope_tab(L, 16)
    one, zero = jnp.ones((L, 64), F32), jnp.zeros((L, 64), F32)
    z32 = jnp.zeros((L, 32), F32)
    mla_q = (jnp.tile(jnp.concatenate([one, c8, z32], 1), (1, 4)), jnp.tile(jnp.concatenate([zero, s8, z32], 1), (1, 4)))
    mla_k = (jnp.concatenate([zero, c8, z32], 1), jnp.concatenate([zero, s8, z32], 1))
    win = (jnp.tile(c16, (1, 4)), jnp.tile(s16, (1, 4)), jnp.tile(c16, (1, 2)), jnp.tile(s16, (1, 2)))
    return mla_q + mla_k, win


def _hyena(proj, lp, dft, NB, Lb):
    cm, sm, smt = dft
    tm = min(Lb, 512)
    v, x1, x2 = _short_conv(proj, lp["hy_conv_w"], lp["hy_conv_b"].reshape(1, -1), NB, Lb)
    w1p = jnp.pad(lp["hy_w1"], ((0, 128 - lp["hy_w1"].shape[0]), (0, 0)))
    fs, nyq = _hy_filter(Lb, w1p, lp["hy_b1"].reshape(1, -1), lp["hy_w2"], lp["hy_b2"].reshape(1, -1), lp["hy_w3"],
                         lp["hy_sin_freq"], lp["hy_log_decay"].reshape(1, -1))
    gr, gi = _hy_gdft(cm, sm, fs, nyq, Lb, tm)
    skip = lp["hy_skip"].reshape(2, 1, HY_C)
    z = v
    for n, gate in enumerate((x1, x2)):
        yr, yi = _hy_fwd(cm, sm, z, gr, gi, n, NB, Lb, tm)
        z = _hy_inv(cm, smt, yr, yi, z, gate, skip, n, NB, Lb, tm)
    return z


def _layer_steps(x, mod, lp, l, NB, Lb, mod_row_of_batch, dft, cache=None, tabs=None, na_bias=None):
    T = NB * Lb
    latent = cache is not None
    bm = 256
    rows_of = lambda n: (lambda i: mod_row_of_batch((i * n) // Lb))
    mod_row = rows_of(bm)
    span = Lb if latent else T
    bmp = min(span, 1024)
    proj, gates = _in_proj(x, mod, lp["w_in_p"], rows_of(bmp), bmp)

    gq, gkv = lp["mla_q_norm"].reshape(1, -1), lp["mla_kv_norm"].reshape(1, -1)
    wcat, wrot, wk, wv = lp["mla_w"]
    q_all, ckv_n, kpe_r = _mla_q(proj, gq, gkv, wcat, wrot, tabs[0] if latent else None, Lb, min(span, 512))
    if latent:
        ckv_c, kpe_c, kc_c, vc_c, kd_c, vd_c = cache
        Lc = ckv_c.shape[1]
        kpe_cp = jnp.pad(kpe_c, ((0, 0), (0, 0), (64, 32)))
        ckv_all = jnp.concatenate([ckv_c, ckv_n.reshape(NB, Lb, 128)], 1).reshape(NB * (Lc + Lb), 128)
        kpe_all = jnp.concatenate([kpe_cp, kpe_r.reshape(NB, Lb, 128)], 1).reshape(NB * (Lc + Lb), 128)
        k_all, v_all = _mla_kv(ckv_all, kpe_all, wk, wv, 512)
        oc = _lat_win_attention(proj, kc_c.reshape(NB, Lc, 128), vc_c.reshape(NB, Lc, 128), tabs[1],
                                lp["win_sink"], NB, Lb)
        od = _lat_na_attention(proj, kd_c.reshape(NB, Lc, 256), vd_c.reshape(NB, Lc, 256), na_bias, NB, Lb)
        ob = _hyena(proj, lp, dft, NB, Lb)
        after = yield "projected", od
        oa = _lat_mla_attention(q_all, k_all, v_all, NB, Lb, Lc + Lb, 256, after=after)
        after = None
    else:
        k_all, v_all = _mla_kv(ckv_n, kpe_r, wk, wv, 512)
        oa, oc, od = _ctx_attention(proj, q_all, k_all, v_all, lp["win_sink"], NB, Lb)
        ob = _hyena(proj, lp, dft, NB, Lb)
        after = yield "projected", oa

    bmm = min(span, 512)
    x1, hp, logits_t = _merge((oa, ob, oc, od), gates, lp["w_branch_b"], lp["w_out_b"], x, mod,
                              lp["ln1_g"].reshape(1, -1), lp["ln1_b"].reshape(1, -1), lp["moe_router"].T,
                              rows_of(bmm), bmm, after=after)
    n_slots = T * TOP_K + N_EXPERTS * MOE_TM
    gate_t, rank, cnt = _router(logits_t, lp["moe_bias"].reshape(-1, 1), 512)
    pos, w8, te, nx, nt = _route_pos(gate_t, rank, cnt, 512, MOE_TM, n_slots // MOE_TM)
    xs = _sc_dispatch(pos.reshape(-1), hp, n_slots)
    after = yield "dispatched", None
    ys = _gmm(te.reshape(-1), nx.reshape(-1), nt.reshape(-1)[:1], xs, lp["moe_w1"], lp["moe_w3"], lp["moe_w2"], l,
              MOE_TM, after=after)
    yield "ffn", ys
    yk = _sc_gather(ys, pos.reshape(-1)).reshape(TOP_K, T, D // 2)
    x2 = _combine(yk, w8.T, hp, lp["sh_w1_b"], lp["sh_w3_b"], lp["sh_w2_b"], x1, mod,
                  lp["ln2_g"].reshape(1, -1), lp["ln2_b"].reshape(1, -1), mod_row, bm)
    yield "done", (x2, proj, ckv_n)


def kernel(x_prompt, x_sample, cache_mla_ckv, cache_mla_kpe, cache_win_k, cache_win_v, cache_na_k, cache_na_v, c, c_ctx, w_ada, b_ada, w_in, mla_q_norm, mla_kv_norm, mla_w_uq, mla_w_ukv, hy_conv_w, hy_conv_b, hy_w1, hy_b1, hy_w2, hy_b2, hy_w3, hy_sin_freq, hy_log_decay, hy_skip, win_sink, na_rpb, w_branch, w_out, ln1_g, ln1_b, ln2_g, ln2_b, moe_router, moe_bias, moe_w1, moe_w3, moe_w2, sh_w1, sh_w3, sh_w2):
    B, S, _ = x_prompt.shape
    DB, DS, _ = x_sample.shape
    xp = x_prompt.reshape(B * S, D)
    xs = x_sample.reshape(DB * DS, D)
    cvec = jnp.concatenate([c_ctx[None, :], c, jnp.zeros((8 - 1 - DB, D), F32)], 0)
    dft_ctx = _dft_mats(S)
    dft_lat = _dft_mats(DS)
    tabs = _rope_tables(DS)
    projs, ckvs = [], []

    def params(l):
        return dict(w_in_p=_prep_w_in(w_in[l]), mla_q_norm=mla_q_norm[l], mla_kv_norm=mla_kv_norm[l],
                    mla_w=_prep_mla(mla_w_uq[l], mla_w_ukv[l]), hy_conv_w=hy_conv_w[l], hy_conv_b=hy_conv_b[l],
                    hy_w1=hy_w1[l], hy_b1=hy_b1[l], hy_w2=hy_w2[l], hy_b2=hy_b2[l], hy_w3=hy_w3[l],
                    hy_sin_freq=hy_sin_freq[l], hy_log_decay=hy_log_decay[l], hy_skip=hy_skip[l],
                    win_sink=win_sink[l], w_branch_b=w_branch[l].astype(BF16), w_out_b=w_out[l].astype(BF16),
                    ln1_g=ln1_g[l], ln1_b=ln1_b[l], ln2_g=ln2_g[l], ln2_b=ln2_b[l],
                    moe_router=moe_router[l], moe_bias=moe_bias[l], moe_w1=moe_w1, moe_w3=moe_w3, moe_w2=moe_w2,
                    sh_w1_b=sh_w1[l].astype(BF16), sh_w3_b=sh_w3[l].astype(BF16), sh_w2_b=sh_w2[l].astype(BF16))

    lps = [params(l) for l in range(DEPTH)]
    mods = [_modulation(cvec, w_ada, b_ada, l) for l in range(DEPTH)]

    def ctx_layer(l, x):
        return _layer_steps(x, mods[l], lps[l], l, B, S, lambda b: 0, dft_ctx)

    def lat_layer(l, x):
        cache = (cache_mla_ckv[:, l], cache_mla_kpe[:, l], cache_win_k[:, l], cache_win_v[:, l],
                 cache_na_k[:, l], cache_na_v[:, l])
        return _layer_steps(x, mods[l], lps[l], l, DB, DS, lambda b: 1 + b, dft_lat, cache=cache, tabs=tabs,
                            na_bias=_na_bias(na_rpb[l]))

    ctx = ctx_layer(0, xp)
    next(ctx)
    ctx.send(None)
    for l in range(DEPTH):
        lat = lat_layer(l, xs)
        lat_local = next(lat)[1]
        ys_ctx = ctx.send(lat_local)[1]
        lat.send(ys_ctx)
        xp, proj, ckv_n = ctx.send(None)[1]
        projs.append(proj)
        ckvs.append(ckv_n)
        if l + 1 < DEPTH:
            ctx = ctx_layer(l + 1, xp)
            attended = next(ctx)[1]
            ys_lat = lat.send(attended)[1]
            ctx.send(ys_lat)
        else:
            ckv, kpe, wk, wv, nk, nv = _emit_caches(projs, ckvs, B, S)
            lat.send((xp, nv))
        xs = lat.send(None)[1][0]
    heads = lambda t, h: t.reshape(B, DEPTH, S, h, HEAD_DIM)
    return (xp.reshape(B, S, D), xs.reshape(DB, DS, D), ckv, kpe, heads(wk, 2), heads(wv, 2), heads(nk, 4),
            heads(nv, 4))
```

```python
import functools
import math

import jax
import jax.numpy as jnp
from jax import lax
from jax.experimental import pallas as pl
from jax.experimental.pallas import tpu as pltpu
from jax.experimental.pallas import tpu_sc as plsc

F32 = jnp.float32
BF16 = jnp.bfloat16

D = 1024
DEPTH = 2
GRID_W = 64
HEAD_DIM = 64
MLA_SCALE = 96 ** -0.5
ATT_SCALE = HEAD_DIM ** -0.5
HY_C = 256
HY_BANDS = 8
NA_KH = 8
NA_KW = 16
N_EXPERTS = 64
N_GROUPS = 8
TOP_K = 8
TOPK_GROUPS = 4
D_EXPERT = 256
ROUTED_SCALE = 2.5
ROPE_BASE = 10000.0
LN_EPS = 1e-5
RMS_EPS = 1e-6
NEG = -1e30
DN_ALPHA = (2 * DEPTH) ** 0.25

P_QLAT, P_CKV, P_KPE, P_HY = 0, 256, 384, 512
P_WQ, P_WK, P_WV = 1280, 1536, 1664
P_NQ, P_NK, P_NV = 1792, 2048, 2304
P_WQR, P_WKR, P_KPER, P_GATE = 2560, 2816, 2944, 3072
N_PROJ = 7168

VMEM_LIMIT = 56 * 1024 * 1024

SC_CORES = 2
SC_SUBCORES = 16
SC_LANES = 16
SC_WORKERS = SC_CORES * SC_SUBCORES
SC_ROWS = 64

MOE_TM = 512

def _cp(*sem):
    return pltpu.CompilerParams(dimension_semantics=sem, vmem_limit_bytes=VMEM_LIMIT)


def _sigmoid(x):
    return 1.0 / (1.0 + jnp.exp(-x))


def _dot(a, b):
    return jnp.dot(a, b, preferred_element_type=F32)


def _dot_nt(a, b):
    return lax.dot_general(a, b, (((1,), (1,)), ((), ())), preferred_element_type=F32)


def _dot_hi(a, b):
    return jnp.dot(a, b, preferred_element_type=F32, precision=lax.Precision.HIGHEST)


def _pack_pairs(x):
    w = x.shape[1] // 2
    hi = lax.bitcast_convert_type(x[:, :w].astype(BF16).astype(F32), jnp.int32)
    lo = lax.bitcast_convert_type(x[:, w:].astype(BF16).astype(F32), jnp.int32)
    return hi | lax.shift_right_logical(lo, 16)


def _unpack_pairs(p):
    hi = lax.bitcast_convert_type(p & jnp.int32(-65536), F32)
    lo = lax.bitcast_convert_type(lax.shift_left(p, 16), F32)
    return hi, lo


def _layer_norm(x, g, b):
    mu = jnp.mean(x, -1, keepdims=True)
    xc = x - mu
    var = jnp.mean(xc * xc, -1, keepdims=True)
    return xc * lax.rsqrt(var + LN_EPS) * g + b


def _rms_norm(x, g):
    return x * lax.rsqrt(jnp.mean(x * x, -1, keepdims=True) + RMS_EPS) * g


def _mod_kernel(c_ref, w_ref, b_ref, o_ref):
    c = c_ref[...]
    a = (c * _sigmoid(c)).astype(BF16)
    o_ref[...] = _dot(a, w_ref[...].astype(BF16)) + b_ref[...]


def _modulation(cvec, w_ada, b_ada, l):
    out = pl.pallas_call(
        _mod_kernel,
        grid=(6,),
        in_specs=[pl.BlockSpec((8, D), lambda j: (0, 0)),
                  pl.BlockSpec((None, D, D), lambda j: (l, 0, j)),
                  pl.BlockSpec((None, 1, D), lambda j: (l, 0, j))],
        out_specs=pl.BlockSpec((8, D), lambda j: (0, j)),
        out_shape=jax.ShapeDtypeStruct((8, 6 * D), F32),
        compiler_params=_cp("arbitrary"),
        name="modulation",
    )(cvec, w_ada, b_ada.reshape(DEPTH, 1, 6 * D))
    return out.reshape(8, 6, D)


def _inproj_kernel(x_ref, mod_ref, w_ref, o_ref, g_ref, h_ref, *, n_main):
    j = pl.program_id(1)

    @pl.when(j == 0)
    def _():
        m = mod_ref[...]
        h_ref[...] = (x_ref[...] * (1.0 + m[1:2, :]) + m[0:1, :]).astype(BF16)

    y = _dot(h_ref[...], w_ref[...])

    @pl.when(j < n_main)
    def _():
        o_ref[...] = y

    @pl.when(j >= n_main)
    def _():
        g_ref[...] = y.astype(BF16)


def _in_proj(x, mod, w_p, mod_row, bm, bn=1024):
    T = x.shape[0]
    n_main = P_GATE // bn
    return pl.pallas_call(
        functools.partial(_inproj_kernel, n_main=n_main),
        grid=(T // bm, N_PROJ // bn),
        in_specs=[pl.BlockSpec((bm, D), lambda i, j: (i, 0)),
                  pl.BlockSpec((None, 6, D), lambda i, j: (mod_row(i), 0, 0)),
                  pl.BlockSpec((D, bn), lambda i, j: (0, j))],
        out_specs=[pl.BlockSpec((bm, bn), lambda i, j: (i, jnp.minimum(j, n_main - 1))),
                   pl.BlockSpec((bm, bn), lambda i, j: (i, jnp.maximum(j - n_main, 0)))],
        out_shape=[jax.ShapeDtypeStruct((T, P_GATE), F32), jax.ShapeDtypeStruct((T, N_PROJ - P_GATE), BF16)],
        scratch_shapes=[pltpu.VMEM((bm, D), BF16)],
        compiler_params=_cp("arbitrary", "arbitrary"),
        name="in_proj",
    )(x, mod, w_p)


def _mla_q_kernel(*refs, rope):
    if rope:
        (ql_ref, ckv_ref, kpe_ref, kper_ref, gq_ref, gkv_ref, wc_ref, wr_ref,
         cq_ref, sq_ref, ck_ref, sk_ref, q_ref, ckvn_ref, kpeo_ref) = refs
    else:
        ql_ref, ckv_ref, kpe_ref, gq_ref, gkv_ref, wc_ref, q_ref, ckvn_ref, kpeo_ref = refs
    qn = _rms_norm(ql_ref[...], gq_ref[...]).astype(BF16)
    q = _dot(qn, wc_ref[...])
    if rope:
        q = q * cq_ref[...] + _dot(qn, wr_ref[...]) * sq_ref[...]
        kpeo_ref[...] = kpe_ref[...] * ck_ref[...] + kper_ref[...] * sk_ref[...]
    else:
        kpeo_ref[...] = kpe_ref[...]
    q_ref[...] = (q * MLA_SCALE).astype(BF16)
    ckvn_ref[...] = _rms_norm(ckv_ref[...], gkv_ref[...])


def _mla_q(proj, gq, gkv, wcat, wrot, tabs, Lb, bm):
    T = proj.shape[0]
    rope = tabs is not None
    nl = Lb // bm
    col = lambda c: (lambda i: (i, c))
    fixed = lambda i: (0, 0)
    in_specs = [pl.BlockSpec((bm, 256), col(P_QLAT // 256)),
                pl.BlockSpec((bm, 128), col(P_CKV // 128)),
                pl.BlockSpec((bm, 128), col(P_KPE // 128))]
    args = [proj, proj, proj]
    if rope:
        in_specs.append(pl.BlockSpec((bm, 128), col(P_KPER // 128)))
        args.append(proj)
    in_specs += [pl.BlockSpec((1, 256), fixed), pl.BlockSpec((1, 128), fixed), pl.BlockSpec((256, 512), fixed)]
    args += [gq, gkv, wcat]
    if rope:
        cq, sq, ck, sk = tabs
        pos = lambda i: (i % nl, 0)
        in_specs += [pl.BlockSpec((256, 512), fixed), pl.BlockSpec((bm, 512), pos), pl.BlockSpec((bm, 512), pos),
                     pl.BlockSpec((bm, 128), pos), pl.BlockSpec((bm, 128), pos)]
        args += [wrot, cq, sq, ck, sk]
    return pl.pallas_call(
        functools.partial(_mla_q_kernel, rope=rope),
        grid=(T // bm,),
        in_specs=in_specs,
        out_specs=[pl.BlockSpec((bm, 512), lambda i: (i, 0)),
                   pl.BlockSpec((bm, 128), lambda i: (i, 0)),
                   pl.BlockSpec((bm, 128), lambda i: (i, 0))],
        out_shape=[jax.ShapeDtypeStruct((T, 512), BF16),
                   jax.ShapeDtypeStruct((T, 128), F32),
                   jax.ShapeDtypeStruct((T, 128), F32)],
        compiler_params=_cp("arbitrary"),
        name="mla_q",
    )(*args)


def _mla_kv_kernel(ckv_ref, kpe_ref, wk_ref, wv_ref, k_ref, v_ref):
    c = ckv_ref[...].astype(BF16)
    kpe = kpe_ref[...]
    k_ref[...] = (_dot(c, wk_ref[...]) + jnp.concatenate([kpe] * 4, axis=1)).astype(BF16)
    v_ref[...] = _dot(c, wv_ref[...]).astype(BF16)


def _mla_kv(ckv, kpe, wk, wv, bm):
    Tk = ckv.shape[0]
    return pl.pallas_call(
        _mla_kv_kernel,
        grid=(Tk // bm,),
        in_specs=[pl.BlockSpec((bm, 128), lambda i: (i, 0)),
                  pl.BlockSpec((bm, 128), lambda i: (i, 0)),
                  pl.BlockSpec((128, 512), lambda i: (0, 0)),
                  pl.BlockSpec((128, 256), lambda i: (0, 0))],
        out_specs=[pl.BlockSpec((bm, 512), lambda i: (i, 0)),
                   pl.BlockSpec((bm, 256), lambda i: (i, 0))],
        out_shape=[jax.ShapeDtypeStruct((Tk, 512), BF16),
                   jax.ShapeDtypeStruct((Tk, 256), BF16)],
        compiler_params=_cp("arbitrary"),
        name="mla_kv",
    )(ckv, kpe, wk, wv)


def _attn_core(q, kvs, masks, sink):
    ss = []
    for (k, _), mk in zip(kvs, masks):
        s = _dot_nt(q, k)
        if mk is not None:
            s = s + mk[1] if mk[0] == "add" else jnp.where(mk[1], s, NEG)
        ss.append(s)
    m = ss[0].max(-1, keepdims=True)
    for s in ss[1:]:
        m = jnp.maximum(m, s.max(-1, keepdims=True))
    if sink is not None:
        m = jnp.maximum(m, sink)
    den = None
    acc = None
    for s, (_, v) in zip(ss, kvs):
        p = jnp.exp(s - m)
        d = p.sum(-1, keepdims=True)
        a = _dot(p.astype(BF16), v)
        den = d if den is None else den + d
        acc = a if acc is None else acc + a
    if sink is not None:
        den = den + jnp.exp(sink - m)
    return acc / den


def _ctx_attn_kernel(qm_ref, km_ref, vm_ref, wq_ref, wk_ref, wv_ref, nq_ref, nk_ref, nv_ref, sink_ref, *rest):
    om_ref, ow_ref, on_ref = rest[-3:]
    for h in range(4):
        q = qm_ref[:, 128 * h:128 * (h + 1)]
        k = km_ref[:, 128 * h:128 * (h + 1)]
        v = vm_ref[:, 64 * h:64 * (h + 1)]
        om_ref[:, 64 * h:64 * (h + 1)] = _attn_core(q, [(k, v)], [None], None)
    for h in range(4):
        g = h // 2
        q = (wq_ref[:, 64 * h:64 * (h + 1)] * ATT_SCALE).astype(BF16)
        k = wk_ref[:, 64 * g:64 * (g + 1)].astype(BF16)
        v = wv_ref[:, 64 * g:64 * (g + 1)].astype(BF16)
        ow_ref[:, 64 * h:64 * (h + 1)] = _attn_core(q, [(k, v)], [None], sink_ref[h])
    for h in range(4):
        q = (nq_ref[:, 64 * h:64 * (h + 1)] * ATT_SCALE).astype(BF16)
        k = nk_ref[:, 64 * h:64 * (h + 1)].astype(BF16)
        v = nv_ref[:, 64 * h:64 * (h + 1)].astype(BF16)
        on_ref[:, 64 * h:64 * (h + 1)] = _attn_core(q, [(k, v)], [None], None)


def _ctx_attention(proj, q_all, k_all, v_all, sink, NB, Lb, after=None):
    T = proj.shape[0]
    pc = lambda w, off: pl.BlockSpec((Lb, w), lambda b: (b, off // w))
    row = lambda w: pl.BlockSpec((Lb, w), lambda b: (b, 0))
    dep_specs, dep_args = _after(after)
    return pl.pallas_call(
        _ctx_attn_kernel,
        grid=(NB,),
        in_specs=[row(512), row(512), row(256),
                  pc(256, P_WQ), pc(128, P_WK), pc(128, P_WV),
                  pc(256, P_NQ), pc(256, P_NK), pc(256, P_NV),
                  pl.BlockSpec(memory_space=pltpu.SMEM)] + dep_specs,
        out_specs=[row(256), row(256), row(256)],
        out_shape=[jax.ShapeDtypeStruct((T, 256), F32)] * 3,
        compiler_params=_cp("arbitrary"),
        name="ctx_attention",
    )(q_all, k_all, v_all, proj, proj, proj, proj, proj, proj, sink, *dep_args)


def _lat_mla_kernel(q_ref, k_ref, v_ref, *rest):
    o_ref = rest[-1]
    for h in range(4):
        q = q_ref[:, 128 * h:128 * (h + 1)]
        k = k_ref[:, 128 * h:128 * (h + 1)]
        v = v_ref[:, 64 * h:64 * (h + 1)]
        o_ref[:, 64 * h:64 * (h + 1)] = _attn_core(q, [(k, v)], [None], None)


def _after(after):
    deps = [] if after is None else list(after) if isinstance(after, (tuple, list)) else [after]
    return [pl.BlockSpec(memory_space=pl.ANY)] * len(deps), deps


def _lat_mla_attention(q_all, k_all, v_all, NB, Lb, Lk, tq, after=None):
    T = q_all.shape[0]
    nq = Lb // tq
    dep_specs, dep_args = _after(after)
    return pl.pallas_call(
        _lat_mla_kernel,
        grid=(NB, nq),
        in_specs=[pl.BlockSpec((tq, 512), lambda b, i: (b * nq + i, 0)),
                  pl.BlockSpec((Lk, 512), lambda b, i: (b, 0)),
                  pl.BlockSpec((Lk, 256), lambda b, i: (b, 0))] + dep_specs,
        out_specs=pl.BlockSpec((tq, 256), lambda b, i: (b * nq + i, 0)),
        out_shape=jax.ShapeDtypeStruct((T, 256), F32),
        compiler_params=_cp("arbitrary", "arbitrary"),
        name="lat_mla_attention",
    )(q_all, k_all, v_all, *dep_args)


def _attn_local_ctx(q, locs, kc, vc, sink):
    s_ctx = _dot_nt(q, kc)
    m_ctx = s_ctx.max(-1, keepdims=True)
    if sink is not None:
        m_ctx = jnp.maximum(m_ctx, sink)
    ms, dens, accs = [], [], []
    for rs, k, v, mk in locs:
        s = _dot_nt(q[rs], k)
        s = s + mk[1] if mk[0] == "add" else jnp.where(mk[1], s, NEG)
        m = jnp.maximum(s.max(-1, keepdims=True), m_ctx[rs])
        p = jnp.exp(s - m)
        ms.append(m)
        dens.append(p.sum(-1, keepdims=True))
        accs.append(_dot(p.astype(BF16), v))
    m = jnp.concatenate(ms, axis=0)
    p = jnp.exp(s_ctx - m)
    den = jnp.concatenate(dens, axis=0) + p.sum(-1, keepdims=True)
    if sink is not None:
        den = den + jnp.exp(sink - m)
    return (jnp.concatenate(accs, axis=0) + _dot(p.astype(BF16), vc)) / den


def _lat_win_kernel(q_ref, qr_ref, k_ref, kr_ref, v_ref, kc_ref, vc_ref, cq_ref, sq_ref, ck_ref, sk_ref,
                    sink_ref, o_ref, *, Lb, bpt):
    t = pl.program_id(1)
    q = (q_ref[...] * cq_ref[...] + qr_ref[...] * sq_ref[...]) * ATT_SCALE
    kc = kc_ref[...].astype(BF16)
    vc = vc_ref[...].astype(BF16)
    blocks = []
    for bb in range(bpt):
        i = t * bpt + bb
        start = pl.multiple_of(jnp.clip((i - 1) * 128, 0, Lb - 384), 128)
        win = pl.ds(start, 384)
        kk = (k_ref[win, :] * ck_ref[win, :] + kr_ref[win, :] * sk_ref[win, :]).astype(BF16)
        qpos = i * 128 + lax.broadcasted_iota(jnp.int32, (128, 384), 0)
        kpos = start + lax.broadcasted_iota(jnp.int32, (128, 384), 1)
        blocks.append((kk, v_ref[win, :].astype(BF16), jnp.abs(qpos - kpos) <= 128))
    for h in range(4):
        g = h // 2
        sl = slice(64 * g, 64 * (g + 1))
        locs = [(slice(128 * bb, 128 * (bb + 1)), kk[:, sl], vv[:, sl], ("keep", valid))
                for bb, (kk, vv, valid) in enumerate(blocks)]
        qh = q[:, 64 * h:64 * (h + 1)].astype(BF16)
        o_ref[:, 64 * h:64 * (h + 1)] = _attn_local_ctx(qh, locs, kc[:, sl], vc[:, sl], sink_ref[h])


def _lat_win_attention(proj, kc, vc, tabs, sink, NB, Lb):
    T = proj.shape[0]
    bpt = 1
    tq = 128 * bpt
    nt = Lb // tq
    Lc = kc.shape[1]
    cq, sq, ck, sk = tabs
    qspec = lambda off: pl.BlockSpec((tq, 256), lambda b, i: (b * nt + i, off // 256))
    kspec = lambda off: pl.BlockSpec((Lb, 128), lambda b, i: (b, off // 128))
    cspec = pl.BlockSpec((None, Lc, 128), lambda b, i: (b, 0, 0))
    return pl.pallas_call(
        functools.partial(_lat_win_kernel, Lb=Lb, bpt=bpt),
        grid=(NB, nt),
        in_specs=[qspec(P_WQ), qspec(P_WQR), kspec(P_WK), kspec(P_WKR), kspec(P_WV), cspec, cspec,
                  pl.BlockSpec((tq, 256), lambda b, i: (i, 0)), pl.BlockSpec((tq, 256), lambda b, i: (i, 0)),
                  pl.BlockSpec((Lb, 128), lambda b, i: (0, 0)), pl.BlockSpec((Lb, 128), lambda b, i: (0, 0)),
                  pl.BlockSpec(memory_space=pltpu.SMEM)],
        out_specs=pl.BlockSpec((tq, 256), lambda b, i: (b * nt + i, 0)),
        out_shape=jax.ShapeDtypeStruct((T, 256), F32),
        compiler_params=_cp("arbitrary", "arbitrary"),
        name="lat_win_attention",
    )(proj, proj, proj, proj, proj, kc, vc, cq, sq, ck, sk, sink)


def _na_bias_kernel(rpb_ref, o_ref):
    h = pl.program_id(0)
    qc = lax.broadcasted_iota(jnp.int32, (GRID_W, GRID_W), 0)
    kc = lax.broadcasted_iota(jnp.int32, (GRID_W, GRID_W), 1)
    dc = kc - qc + (NA_KW - 1)
    wstart = jnp.clip(qc - NA_KW // 2, 0, GRID_W - NA_KW)
    ok = (kc >= wstart) & (kc < wstart + NA_KW)
    n_dc = 2 * NA_KW - 1
    n_dr = 2 * NA_KH - 1
    tabs = []
    for dr in range(n_dr):
        t = jnp.zeros((GRID_W, GRID_W), F32)
        for j in range(n_dc):
            t = jnp.where(dc == j, rpb_ref[(h * n_dr + dr) * n_dc + j], t)
        tabs.append(jnp.where(ok, t, NEG))
    for o in range(NA_KH):
        for a in range(NA_KH):
            o_ref[o, :, GRID_W * a:GRID_W * (a + 1)] = tabs[a + NA_KH - 1 - o]


def _na_bias(rpb):
    H = rpb.shape[0]
    return pl.pallas_call(
        _na_bias_kernel,
        grid=(H,),
        in_specs=[pl.BlockSpec(memory_space=pltpu.SMEM)],
        out_specs=pl.BlockSpec((None, NA_KH, GRID_W, NA_KH * GRID_W), lambda h: (h, 0, 0, 0)),
        out_shape=jax.ShapeDtypeStruct((H, NA_KH, GRID_W, NA_KH * GRID_W), F32),
        compiler_params=_cp("arbitrary"),
        name="na_bias",
    )(rpb.reshape(-1))


def _lat_na_kernel(q_ref, k_ref, v_ref, kc_ref, vc_ref, bias_ref, o_ref, *, rows, rpt):
    t = pl.program_id(1)
    q = q_ref[...] * ATT_SCALE
    kc = kc_ref[...].astype(BF16)
    vc = vc_ref[...].astype(BF16)
    bands = []
    for rr in range(rpt):
        r = t * rpt + rr
        first = jnp.clip(r - NA_KH // 2, 0, rows - NA_KH)
        win = pl.ds(pl.multiple_of(first * GRID_W, GRID_W), NA_KH * GRID_W)
        bands.append((k_ref[win, :].astype(BF16), v_ref[win, :].astype(BF16), r - first))
    for h in range(4):
        sl = slice(64 * h, 64 * (h + 1))
        locs = [(slice(GRID_W * rr, GRID_W * (rr + 1)), kk[:, sl], vv[:, sl], ("add", bias_ref[h, off]))
                for rr, (kk, vv, off) in enumerate(bands)]
        o_ref[:, sl] = _attn_local_ctx(q[:, sl].astype(BF16), locs, kc[:, sl], vc[:, sl], None)


def _lat_na_attention(proj, kc, vc, bias, NB, Lb):
    T = proj.shape[0]
    rows = Lb // GRID_W
    rpt = 8
    tq = GRID_W * rpt
    nt = rows // rpt
    Lc = kc.shape[1]
    kspec = lambda off: pl.BlockSpec((Lb, 256), lambda b, t: (b, off // 256))
    cspec = pl.BlockSpec((None, Lc, 256), lambda b, t: (b, 0, 0))
    return pl.pallas_call(
        functools.partial(_lat_na_kernel, rows=rows, rpt=rpt),
        grid=(NB, nt),
        in_specs=[pl.BlockSpec((tq, 256), lambda b, t: (b * nt + t, P_NQ // 256)),
                  kspec(P_NK), kspec(P_NV), cspec, cspec,
                  pl.BlockSpec((4, NA_KH, GRID_W, NA_KH * GRID_W), lambda b, t: (0, 0, 0, 0))],
        out_specs=pl.BlockSpec((tq, 256), lambda b, t: (b * nt + t, 0)),
        out_shape=jax.ShapeDtypeStruct((T, 256), F32),
        compiler_params=_cp("arbitrary", "arbitrary"),
        name="lat_na_attention",
    )(proj, proj, proj, kc, vc, bias)


def _short_conv_kernel(a_ref, b_ref, c_ref, w_ref, bias_ref, oa_ref, ob_ref, oc_ref, *, L):
    t = lax.broadcasted_iota(jnp.int32, (L, HY_C), 0)
    for n, (x_ref, o_ref) in enumerate(((a_ref, oa_ref), (b_ref, ob_ref), (c_ref, oc_ref))):
        sl = slice(HY_C * n, HY_C * (n + 1))
        x = x_ref[...]
        prev = jnp.where(t == 0, 0.0, pltpu.roll(x, 1, axis=0))
        nxt = jnp.where(t == L - 1, 0.0, pltpu.roll(x, L - 1, axis=0))
        o_ref[...] = prev * w_ref[0:1, sl] + x * w_ref[1:2, sl] + nxt * w_ref[2:3, sl] + bias_ref[:, sl]


def _short_conv(proj, w, b, NB, Lb):
    T = proj.shape[0]
    spec = lambda c: pl.BlockSpec((Lb, HY_C), lambda i: (i, c))
    return pl.pallas_call(
        functools.partial(_short_conv_kernel, L=Lb),
        grid=(NB,),
        in_specs=[spec(P_HY // HY_C), spec(P_HY // HY_C + 1), spec(P_HY // HY_C + 2),
                  pl.BlockSpec((3, 3 * HY_C), lambda i: (0, 0)), pl.BlockSpec((1, 3 * HY_C), lambda i: (0, 0))],
        out_specs=[spec(0)] * 3,
        out_shape=[jax.ShapeDtypeStruct((T, HY_C), F32)] * 3,
        compiler_params=_cp("arbitrary"),
        name="hyena_short_conv",
    )(proj, proj, proj, w, b)


def _hy_filter_kernel(w1_ref, b1_ref, w2_ref, b2_ref, w3_ref, freq_ref, ld_ref, fs_ref, nyq_ref, *, L):
    ti = lax.broadcasted_iota(jnp.int32, (L, 128), 0)
    t = ti.astype(F32)
    j = lax.broadcasted_iota(jnp.int32, (L, 128), 1)
    band = jnp.where(j <= HY_BANDS, j - 1, j - 1 - HY_BANDS).astype(F32)
    ang = (2.0 * math.pi / L) * t * band
    tn = t / L
    z = jnp.where(j == 0, tn, jnp.where(j <= HY_BANDS, jnp.cos(ang),
                                        jnp.where(j <= 2 * HY_BANDS, -jnp.sin(ang), 0.0)))
    a = jnp.sin(freq_ref[0:1, :] * (_dot_hi(z, w1_ref[...]) + b1_ref[...]))
    a = jnp.sin(freq_ref[1:2, :] * (_dot_hi(a, w2_ref[...]) + b2_ref[...]))
    filt = _dot_hi(a, w3_ref[...])
    tcol = lax.broadcasted_iota(jnp.int32, (L, 4 * HY_C), 0)
    filt = filt * jnp.exp(-(tcol.astype(F32) / L) * jnp.exp(ld_ref[...]))
    t1 = lax.broadcasted_iota(jnp.int32, (L, HY_C), 0)
    sign = jnp.where(t1 % 2 == 0, 1.0, -1.0)
    for n in range(2):
        fwd = filt[:, 2 * HY_C * n:2 * HY_C * n + HY_C]
        bwd = jnp.where(t1 == 0, 0.0, filt[:, 2 * HY_C * n + HY_C:2 * HY_C * (n + 1)])
        tot = fwd + bwd
        fs_ref[:, HY_C * n:HY_C * (n + 1)] = tot
        fs_ref[:, 2 * HY_C + HY_C * n:2 * HY_C + HY_C * (n + 1)] = fwd - bwd
        nyq_ref[:, HY_C * n:HY_C * (n + 1)] = (tot * sign).sum(0, keepdims=True)


def _hy_filter(L, w1p, b1, w2, b2, w3, freq, ld):
    full = lambda s: pl.BlockSpec(s, lambda: tuple(0 for _ in s))
    return pl.pallas_call(
        functools.partial(_hy_filter_kernel, L=L),
        in_specs=[full((128, 64)), full((1, 64)), full((64, 64)), full((1, 64)), full((64, 4 * HY_C)),
                  full((2, 64)), full((1, 4 * HY_C))],
        out_specs=[full((L, 4 * HY_C)), full((1, 2 * HY_C))],
        out_shape=[jax.ShapeDtypeStruct((L, 4 * HY_C), F32), jax.ShapeDtypeStruct((1, 2 * HY_C), F32)],
        compiler_params=pltpu.CompilerParams(vmem_limit_bytes=VMEM_LIMIT),
        name="hyena_filter",
    )(w1p, b1, w2, b2, w3, freq, ld)


def _hy_gdft_kernel(cm_ref, sm_ref, fs_ref, nyq_ref, gr_ref, gi_ref, *, tm):
    m = pl.program_id(0)
    f = fs_ref[...].astype(BF16)
    gr_ref[...] = _dot(cm_ref[...], f[:, :2 * HY_C])
    gi = _dot(sm_ref[...], f[:, 2 * HY_C:])
    row = m * tm + lax.broadcasted_iota(jnp.int32, (tm, 2 * HY_C), 0)
    gi_ref[...] = jnp.where(row == 0, nyq_ref[...], gi)


def _hy_gdft(cm, sm, fs, nyq, L, tm):
    return pl.pallas_call(
        functools.partial(_hy_gdft_kernel, tm=tm),
        grid=(L // tm,),
        in_specs=[pl.BlockSpec((tm, L), lambda m: (m, 0)), pl.BlockSpec((tm, L), lambda m: (m, 0)),
                  pl.BlockSpec((L, 4 * HY_C), lambda m: (0, 0)), pl.BlockSpec((1, 2 * HY_C), lambda m: (0, 0))],
        out_specs=[pl.BlockSpec((tm, 2 * HY_C), lambda m: (m, 0))] * 2,
        out_shape=[jax.ShapeDtypeStruct((L, 2 * HY_C), F32)] * 2,
        compiler_params=_cp("arbitrary"),
        name="hyena_filter_dft",
    )(cm, sm, fs, nyq)


def _hy_fwd_kernel(cm_ref, sm_ref, z_ref, gr_ref, gi_ref, yr_ref, yi_ref, *, L, tm, ns):
    m = pl.program_id(1)
    gr = gr_ref[...]
    gi = gi_ref[...]
    row0 = (m * tm + lax.broadcasted_iota(jnp.int32, (tm, HY_C), 0)) == 0
    s = jnp.where(row0, 0.5 / L, 1.0 / L)
    for g in range(ns):
        zb = z_ref[g * L:(g + 1) * L, :].astype(BF16)
        zr = _dot(cm_ref[...], zb)
        zi = _dot(sm_ref[...], zb)
        zigi = zi * gi
        yr_ref[g * tm:(g + 1) * tm, :] = ((zr * gr - jnp.where(row0, 0.0, zigi)) * s).astype(BF16)
        yi_ref[g * tm:(g + 1) * tm, :] = (jnp.where(row0, zigi, zr * gi + zi * gr) * s).astype(BF16)


def _hy_seqs_per_step(NB, Lb, tm):
    ns = max(1, 2048 // Lb) if tm == Lb else 1
    while NB % ns:
        ns //= 2
    return ns


def _hy_fwd(cm, sm, z, gr, gi, n, NB, Lb, tm):
    T = z.shape[0]
    nm = Lb // tm
    ns = _hy_seqs_per_step(NB, Lb, tm)
    return pl.pallas_call(
        functools.partial(_hy_fwd_kernel, L=Lb, tm=tm, ns=ns),
        grid=(NB // ns, nm),
        in_specs=[pl.BlockSpec((tm, Lb), lambda b, m: (m, 0)), pl.BlockSpec((tm, Lb), lambda b, m: (m, 0)),
                  pl.BlockSpec((ns * Lb, HY_C), lambda b, m: (b, 0)),
                  pl.BlockSpec((tm, HY_C), lambda b, m: (m, n)), pl.BlockSpec((tm, HY_C), lambda b, m: (m, n))],
        out_specs=[pl.BlockSpec((ns * tm, HY_C), lambda b, m: (b * nm + m, 0))] * 2,
        out_shape=[jax.ShapeDtypeStruct((T, HY_C), BF16)] * 2,
        compiler_params=_cp("arbitrary", "arbitrary"),
        name="hyena_fwd_dft",
    )(cm, sm, z, gr, gi)


def _hy_inv_kernel(cm_ref, smt_ref, yr_ref, yi_ref, z_ref, g_ref, skip_ref, o_ref, *, L, tm, ns):
    for g in range(ns):
        seq = slice(g * L, (g + 1) * L)
        out = slice(g * tm, (g + 1) * tm)
        conv = _dot(cm_ref[...], yr_ref[seq, :]) + _dot(smt_ref[...], yi_ref[seq, :])
        o_ref[out, :] = g_ref[out, :] * (conv + skip_ref[...] * z_ref[out, :])


def _hy_inv(cm, smt, yr, yi, z, gate, skip, n, NB, Lb, tm):
    T = z.shape[0]
    nm = Lb // tm
    ns = _hy_seqs_per_step(NB, Lb, tm)
    tile = pl.BlockSpec((ns * tm, HY_C), lambda b, m: (b * nm + m, 0))
    seq = pl.BlockSpec((ns * Lb, HY_C), lambda b, m: (b, 0))
    return pl.pallas_call(
        functools.partial(_hy_inv_kernel, L=Lb, tm=tm, ns=ns),
        grid=(NB // ns, nm),
        in_specs=[pl.BlockSpec((tm, Lb), lambda b, m: (m, 0)), pl.BlockSpec((tm, Lb), lambda b, m: (m, 0)),
                  seq, seq, tile, tile, pl.BlockSpec((None, 1, HY_C), lambda b, m: (n, 0, 0))],
        out_specs=tile,
        out_shape=jax.ShapeDtypeStruct((T, HY_C), F32),
        compiler_params=_cp("arbitrary", "arbitrary"),
        name="hyena_inv_dft",
    )(cm, smt, yr, yi, z, gate, skip)


def _dft_mats(L):
    k = jnp.arange(L, dtype=jnp.int32)
    blk = 64

    def trig(mult):
        ang = ((mult[:, None] * k[None, :]) % (2 * L)).astype(F32) * (math.pi / L)
        return jnp.cos(ang), jnp.sin(ang)

    ca, sa = trig(jnp.arange(L // blk, dtype=jnp.int32) * blk)
    cb, sb = trig(jnp.arange(blk, dtype=jnp.int32))
    cm = (ca[:, None, :] * cb[None] - sa[:, None, :] * sb[None]).reshape(L, L)
    s = -(sa[:, None, :] * cb[None] + ca[:, None, :] * sb[None]).reshape(L, L)
    alt = jnp.where(k % 2 == 0, 1.0, -1.0).astype(F32)
    sm = jnp.where(k[:, None] == 0, alt[None, :], s)
    smt = jnp.where(k[None, :] == 0, alt[:, None], s)
    return cm.astype(BF16), sm.astype(BF16), smt.astype(BF16)


def _merge_kernel(oa_ref, ob_ref, oc_ref, od_ref, g0_ref, g1_ref, g2_ref, g3_ref, wb_ref, wo_ref, x_ref, mod_ref,
                  lg_ref, lb_ref, rt_ref, *rest):
    x1_ref, hp_ref, logit_ref = rest[-3:]
    acc = None
    for o_ref, g_ref, i in ((oa_ref, g0_ref, 0), (ob_ref, g1_ref, 1), (oc_ref, g2_ref, 2), (od_ref, g3_ref, 3)):
        y = _sigmoid(g_ref[...].astype(F32)) * _dot(o_ref[...].astype(BF16), wb_ref[i])
        acc = y if acc is None else acc + y
    mix = _dot(acc.astype(BF16), wo_ref[...])
    m = mod_ref[...]
    x1 = _layer_norm(DN_ALPHA * x_ref[...] + m[2:3, :] * mix, lg_ref[...], lb_ref[...])
    x1_ref[...] = x1
    h2 = x1 * (1.0 + m[4:5, :]) + m[3:4, :]
    hp_ref[...] = _pack_pairs(h2)
    hi = h2.astype(BF16)
    lo = (h2 - hi.astype(F32)).astype(BF16)
    r = rt_ref[...]
    r_hi = r.astype(BF16)
    r_lo = (r - r_hi.astype(F32)).astype(BF16)
    both = _dot_nt(jnp.concatenate([r_hi, r_lo], axis=0), hi)
    logit_ref[...] = both[:N_EXPERTS, :] + both[N_EXPERTS:, :] + _dot_nt(r_hi, lo)


def _merge(outs, gates, wb, wo, x, mod, lg, lb, router, mod_row, bm, after=None):
    T = x.shape[0]
    row = lambda w: pl.BlockSpec((bm, w), lambda i: (i, 0))
    gspec = lambda n: pl.BlockSpec((bm, D), lambda i: (i, n))
    fixed2 = lambda s: pl.BlockSpec(s, lambda i: (0, 0))
    dep_specs, dep_args = _after(after)
    return pl.pallas_call(
        _merge_kernel,
        grid=(T // bm,),
        in_specs=[row(256)] * 4 + [gspec(0), gspec(1), gspec(2), gspec(3),
                                   pl.BlockSpec((4, 256, D), lambda i: (0, 0, 0)), fixed2((D, D)), row(D),
                                   pl.BlockSpec((None, 6, D), lambda i: (mod_row(i), 0, 0)),
                                   fixed2((1, D)), fixed2((1, D)), fixed2((N_EXPERTS, D))] + dep_specs,
        out_specs=[row(D), row(D // 2), pl.BlockSpec((N_EXPERTS, bm), lambda i: (0, i))],
        out_shape=[jax.ShapeDtypeStruct((T, D), F32), jax.ShapeDtypeStruct((T, D // 2), jnp.int32),
                   jax.ShapeDtypeStruct((N_EXPERTS, T), F32)],
        compiler_params=_cp("arbitrary"),
        name="merge_norm",
    )(*outs, gates, gates, gates, gates, wb, wo, x, mod, lg, lb, router, *dep_args)


def _router_kernel(logit_ref, bias_ref, g_ref, rank_ref, cnt_ref, *, tt):
    per = N_EXPERTS // N_GROUPS
    scores = _sigmoid(logit_ref[...])
    sel = (scores + bias_ref[...]).reshape(N_GROUPS, per, tt)
    gid = lax.broadcasted_iota(jnp.int32, (N_GROUPS, per, tt), 0).astype(F32)
    jid = lax.broadcasted_iota(jnp.int32, (N_GROUPS, per, tt), 1).astype(F32)
    eid = gid * per + jid
    ninf = -jnp.inf
    m1 = sel.max(1, keepdims=True)
    i1 = jnp.where(sel == m1, jid, float(per)).min(1, keepdims=True)
    m2 = jnp.where(jid == i1, ninf, sel).max(1, keepdims=True)
    gs = m1 + m2
    g1 = lax.broadcasted_iota(jnp.int32, (N_GROUPS, 1, tt), 0).astype(F32)
    chosen = jnp.zeros((N_GROUPS, 1, tt), F32)
    for _ in range(TOPK_GROUPS):
        mx = gs.max(0, keepdims=True)
        gi = jnp.where(gs == mx, g1, float(N_GROUPS)).min(0, keepdims=True)
        pick = g1 == gi
        chosen = jnp.where(pick, 1.0, chosen)
        gs = jnp.where(pick, ninf, gs)
    cand = jnp.where(chosen > 0.0, sel, NEG)
    picked = jnp.zeros((N_GROUPS, per, tt), F32)
    for _ in range(TOP_K):
        mx = cand.max(1, keepdims=True).max(0, keepdims=True)
        ei = jnp.where(cand == mx, eid, float(N_EXPERTS)).min(1, keepdims=True).min(0, keepdims=True)
        pick = eid == ei
        picked = jnp.where(pick, 1.0, picked)
        cand = jnp.where(pick, ninf, cand)
    w = scores.reshape(N_GROUPS, per, tt) * picked
    wsum = w.sum(1, keepdims=True).sum(0, keepdims=True)
    g_ref[...] = (w / wsum * ROUTED_SCALE).reshape(N_EXPERTS, tt)
    pk = picked.reshape(N_EXPERTS, tt)
    t_in = lax.broadcasted_iota(jnp.int32, (tt, tt), 0)
    t_out = lax.broadcasted_iota(jnp.int32, (tt, tt), 1)
    upper = jnp.where(t_in <= t_out, 1.0, 0.0).astype(BF16)

    @pl.when(pl.program_id(0) == 0)
    def _():
        cnt_ref[...] = jnp.zeros_like(cnt_ref)

    before = cnt_ref[:, 0:1]
    rank_ref[...] = jnp.where(pk > 0.0, before + _dot(pk.astype(BF16), upper) - 1.0, -1.0)
    cnt_ref[...] += pk.sum(-1, keepdims=True)


def _router(logits_t, bias, tt):
    T = logits_t.shape[1]
    tile = pl.BlockSpec((N_EXPERTS, tt), lambda i: (0, i))
    return pl.pallas_call(
        functools.partial(_router_kernel, tt=tt),
        grid=(T // tt,),
        in_specs=[tile, pl.BlockSpec((N_EXPERTS, 1), lambda i: (0, 0))],
        out_specs=[tile, tile, pl.BlockSpec((N_EXPERTS, 128), lambda i: (0, 0))],
        out_shape=[jax.ShapeDtypeStruct((N_EXPERTS, T), F32), jax.ShapeDtypeStruct((N_EXPERTS, T), F32),
                   jax.ShapeDtypeStruct((N_EXPERTS, 128), F32)],
        compiler_params=_cp("arbitrary"),
        name="moe_router",
    )(logits_t, bias)


def _route_pos_kernel(gate_ref, rank_ref, cnt_ref, pos_ref, w_ref, te_ref, nx_ref, nt_ref, *, tm, nt_max):
    ei = lax.broadcasted_iota(jnp.int32, (N_EXPERTS, N_EXPERTS), 0)
    ej = lax.broadcasted_iota(jnp.int32, (N_EXPERTS, N_EXPERTS), 1)
    below = jnp.where(ej < ei, 1.0, 0.0)
    padded = jnp.ceil(cnt_ref[...] * (1.0 / tm)) * tm
    offs = _dot_hi(below, padded)
    rank = rank_ref[...]
    routed = rank >= 0.0
    pos = offs[:, 0:1] + rank
    slot = _dot(below.astype(BF16), jnp.where(routed, 1.0, 0.0).astype(BF16))
    gate = gate_ref[...]
    for k in range(TOP_K):
        mine = routed & (slot == float(k))
        pos_ref[k:k + 1, :] = jnp.where(mine, pos, 0.0).sum(0, keepdims=True).astype(jnp.int32)
        w_ref[k:k + 1, :] = jnp.where(mine, gate, 0.0).sum(0, keepdims=True)
    ends = (offs + padded)[:, 0:1]
    first = (lax.broadcasted_iota(jnp.int32, (N_EXPERTS, nt_max), 1) * tm).astype(F32)
    te = jnp.minimum(jnp.where(ends <= first, 1.0, 0.0).sum(0, keepdims=True), N_EXPERTS - 1.0)
    te_ref[...] = te.astype(jnp.int32)
    eid = lax.broadcasted_iota(jnp.int32, (N_EXPERTS, nt_max), 0).astype(F32)
    nx_ref[...] = (jnp.where(eid == te, ends, 0.0).sum(0, keepdims=True) * (1.0 / tm)).astype(jnp.int32)
    nt_ref[...] = (padded.sum(0, keepdims=True) * (1.0 / tm)).astype(jnp.int32)


def _route_pos(gate_t, rank, cnt, tt, tm, nt_max):
    T = gate_t.shape[1]
    tile = pl.BlockSpec((N_EXPERTS, tt), lambda i: (0, i))
    out = pl.BlockSpec((TOP_K, tt), lambda i: (0, i))
    return pl.pallas_call(
        functools.partial(_route_pos_kernel, tm=tm, nt_max=nt_max),
        grid=(T // tt,),
        in_specs=[tile, tile, pl.BlockSpec((N_EXPERTS, 128), lambda i: (0, 0))],
        out_specs=[out, out, pl.BlockSpec((1, nt_max), lambda i: (0, 0)), pl.BlockSpec((1, nt_max), lambda i: (0, 0)),
                   pl.BlockSpec((1, 128), lambda i: (0, 0))],
        out_shape=[jax.ShapeDtypeStruct((TOP_K, T), jnp.int32), jax.ShapeDtypeStruct((TOP_K, T), F32),
                   jax.ShapeDtypeStruct((1, nt_max), jnp.int32), jax.ShapeDtypeStruct((1, nt_max), jnp.int32),
                   jax.ShapeDtypeStruct((1, 128), jnp.int32)],
        compiler_params=_cp("arbitrary"),
        name="moe_positions",
    )(gate_t, rank, cnt)


def _gmm_kernel(te_ref, nx_ref, nt_ref, xs_ref, w1_hbm, w3_hbm, w2_hbm, *rest, l):
    ys_ref, b1_ref, b3_ref, b2_ref, f1_ref, f3_ref, f2_ref, seg_ref, sem = rest[-9:]
    j = pl.program_id(0)
    live = j < nt_ref[0]
    new_expert = (j == 0) | (te_ref[j] != te_ref[jnp.maximum(j - 1, 0)])

    def fetch(e, slot):
        return [pltpu.make_async_copy(w_hbm.at[l, e], f_ref.at[slot], sem.at[i, slot])
                for i, (w_hbm, f_ref) in enumerate(((w1_hbm, f1_ref), (w3_hbm, f3_ref), (w2_hbm, f2_ref)))]

    @pl.when(live & new_expert)
    def _():
        @pl.when(j == 0)
        def _():
            seg_ref[0] = 0
            for c in fetch(te_ref[0], 0):
                c.start()

        slot = lax.rem(seg_ref[0], 2)
        for c in fetch(te_ref[j], slot):
            c.wait()
        b1_ref[...] = f1_ref[slot].astype(BF16)
        b3_ref[...] = f3_ref[slot].astype(BF16)
        b2_ref[...] = f2_ref[slot].astype(BF16)
        nxt = nx_ref[j]

        @pl.when(nxt < nt_ref[0])
        def _():
            for c in fetch(te_ref[nxt], 1 - slot):
                c.start()

        seg_ref[0] = seg_ref[0] + 1

    @pl.when(live)
    def _():
        xa, xb = _unpack_pairs(xs_ref[...])
        xa, xb = xa.astype(BF16), xb.astype(BF16)
        half = D // 2
        a = _dot(xa, b1_ref[:half, :]) + _dot(xb, b1_ref[half:, :])
        b = _dot(xa, b3_ref[:half, :]) + _dot(xb, b3_ref[half:, :])
        hid = (a * _sigmoid(a) * b).astype(BF16)
        ys_ref[...] = _pack_pairs(_dot(hid, b2_ref[...]))


def _gmm(te, nx, nt, xs, w1, w3, w2, l, tm, after=None):
    n_slots = xs.shape[0]
    ds = D_EXPERT
    rows = pl.BlockSpec((tm, D // 2), lambda j, te, nx, nt: (jnp.minimum(j, nt[0] - 1), 0))
    hbm = pl.BlockSpec(memory_space=pl.ANY)
    dep_specs, dep_args = _after(after)
    return pl.pallas_call(
        functools.partial(_gmm_kernel, l=l),
        grid_spec=pltpu.PrefetchScalarGridSpec(
            num_scalar_prefetch=3,
            grid=(n_slots // tm,),
            in_specs=[rows, hbm, hbm, hbm] + dep_specs,
            out_specs=rows,
            scratch_shapes=[pltpu.VMEM((D, ds), BF16), pltpu.VMEM((D, ds), BF16), pltpu.VMEM((ds, D), BF16),
                            pltpu.VMEM((2, D, ds), F32), pltpu.VMEM((2, D, ds), F32), pltpu.VMEM((2, ds, D), F32),
                            pltpu.SMEM((1,), jnp.int32), pltpu.SemaphoreType.DMA((3, 2))]),
        out_shape=jax.ShapeDtypeStruct((n_slots, D // 2), jnp.int32),
        compiler_params=_cp("arbitrary"),
        name="moe_grouped_ffn",
    )(te, nx, nt, xs, w1, w3, w2, *dep_args)


def _combine_kernel(yk_ref, w_ref, hp_ref, s1_ref, s3_ref, s2_ref, x_ref, mod_ref, lg_ref, lb_ref, o_ref):
    w = w_ref[...]
    acc_a = acc_b = None
    for k in range(TOP_K):
        ya, yb = _unpack_pairs(yk_ref[k])
        wk = w[:, k:k + 1]
        acc_a = wk * ya if acc_a is None else acc_a + wk * ya
        acc_b = wk * yb if acc_b is None else acc_b + wk * yb
    ha, hb = _unpack_pairs(hp_ref[...])
    ha, hb = ha.astype(BF16), hb.astype(BF16)
    half = D // 2
    a = _dot(ha, s1_ref[:half, :]) + _dot(hb, s1_ref[half:, :])
    b = _dot(ha, s3_ref[:half, :]) + _dot(hb, s3_ref[half:, :])
    y = jnp.concatenate([acc_a, acc_b], axis=1) + _dot((a * _sigmoid(a) * b).astype(BF16), s2_ref[...])
    m = mod_ref[...]
    o_ref[...] = _layer_norm(DN_ALPHA * x_ref[...] + m[5:6, :] * y, lg_ref[...], lb_ref[...])


def _combine(yk, w, hp, s1, s3, s2, x1, mod, lg, lb, mod_row, bm):
    T = x1.shape[0]
    ds = D_EXPERT
    row = lambda n: pl.BlockSpec((bm, n), lambda i: (i, 0))
    fixed = lambda s: pl.BlockSpec(s, lambda i: (0, 0))
    return pl.pallas_call(
        _combine_kernel,
        grid=(T // bm,),
        in_specs=[pl.BlockSpec((TOP_K, bm, D // 2), lambda i: (0, i, 0)), row(TOP_K), row(D // 2),
                  fixed((D, ds)), fixed((D, ds)), fixed((ds, D)), row(D),
                  pl.BlockSpec((None, 6, D), lambda i: (mod_row(i), 0, 0)), fixed((1, D)), fixed((1, D))],
        out_specs=row(D),
        out_shape=jax.ShapeDtypeStruct((T, D), F32),
        compiler_params=_cp("arbitrary"),
        name="moe_combine_norm",
    )(yk, w, hp, s1, s3, s2, x1, mod, lg, lb)


def _sc_worker():
    return lax.axis_index("s") * SC_CORES + lax.axis_index("c")


def _sc_mesh():
    return plsc.VectorSubcoreMesh(core_axis_name="c", subcore_axis_name="s")


def _sc_copy_rows(table_hbm, idx_v, out_hbm, base, n_chunks, bufs, gsem, wsem):
    def gather(c, b):
        rows = idx_v.at[pl.ds(pl.multiple_of(c * SC_ROWS, SC_ROWS), SC_ROWS)]
        return pltpu.make_async_copy(table_hbm.at[rows], bufs.at[b], gsem.at[b])

    def write(c, b):
        rows = pl.ds(base + pl.multiple_of(c * SC_ROWS, SC_ROWS), SC_ROWS)
        return pltpu.make_async_copy(bufs.at[b], out_hbm.at[rows], wsem.at[b])

    gather(0, 0).start()

    @pl.loop(0, n_chunks // 2)
    def _(p):
        for b in range(2):
            c = 2 * p + b
            gather(c, b).wait()
            write(c, b).start()

            @pl.when(c >= 1)
            def _():
                write(c - 1, 1 - b).wait()

            @pl.when(c + 1 < n_chunks)
            def _():
                gather(c + 1, 1 - b).start()

    write(n_chunks - 1, 1).wait()


def _sc_gather(table, idx):
    N, W = idx.shape[0], table.shape[1]
    per_w = N // SC_WORKERS
    n_chunks = per_w // SC_ROWS

    def body(table_hbm, idx_hbm, out_hbm, idx_v, bufs, gsem, wsem):
        base = _sc_worker() * per_w
        pltpu.sync_copy(idx_hbm.at[pl.ds(base, per_w)], idx_v)
        _sc_copy_rows(table_hbm, idx_v, out_hbm, base, n_chunks, bufs, gsem, wsem)

    return pl.kernel(
        body, out_type=jax.ShapeDtypeStruct((N, W), table.dtype), mesh=_sc_mesh(),
        scratch_types=[pltpu.VMEM((per_w,), jnp.int32), pltpu.VMEM((2, SC_ROWS, W), table.dtype),
                       pltpu.SemaphoreType.DMA((2,)), pltpu.SemaphoreType.DMA((2,))],
        name="sc_gather",
    )(table, idx)


def _sc_dispatch(pos, table, n_slots):
    NP, (T, W) = pos.shape[0], table.shape
    per_w = n_slots // SC_WORKERS
    n_chunks = per_w // SC_ROWS
    scan = 8192

    def body(pos_hbm, table_hbm, out_hbm, pos_v, src_v, bufs, gsem, wsem):
        base = _sc_worker() * per_w
        lane = lax.iota(jnp.int32, SC_LANES)

        @pl.loop(0, per_w // SC_LANES)
        def _(j):
            o = pl.multiple_of(j * SC_LANES, SC_LANES)
            src_v[pl.ds(o, SC_LANES)] = (base + o + lane) & (T - 1)

        @pl.loop(0, NP // scan)
        def _(c):
            pltpu.sync_copy(pos_hbm.at[pl.ds(pl.multiple_of(c * scan, scan), scan)], pos_v)

            @pl.loop(0, scan // SC_LANES)
            def _(v):
                o = pl.multiple_of(v * SC_LANES, SC_LANES)
                p = pos_v[pl.ds(o, SC_LANES)] - base
                mine = (p >= 0) & (p < per_w)
                tok = (c * scan + o + lane) & (T - 1)
                plsc.store_scatter(src_v, [jnp.where(mine, p, 0)], tok, mask=mine)

        _sc_copy_rows(table_hbm, src_v, out_hbm, base, n_chunks, bufs, gsem, wsem)

    return pl.kernel(
        body, out_type=jax.ShapeDtypeStruct((n_slots, W), table.dtype), mesh=_sc_mesh(),
        scratch_types=[pltpu.VMEM((scan,), jnp.int32), pltpu.VMEM((per_w,), jnp.int32),
                       pltpu.VMEM((2, SC_ROWS, W), table.dtype), pltpu.SemaphoreType.DMA((2,)),
                       pltpu.SemaphoreType.DMA((2,))],
        compiler_params=pltpu.CompilerParams(needs_layout_passes=False),
        name="sc_dispatch",
    )(pos, table)


def _caches_kernel(*refs, nb, S):
    n_in = 6 * DEPTH
    outs = refs[n_in:]
    l = pl.program_id(0)
    for a in range(DEPTH):
        @pl.when(l == a)
        def _(a=a):
            ckv, kpe, wk, wv, nk, nv = refs[6 * a:6 * (a + 1)]
            for g in range(nb):
                rows = slice(g * S, (g + 1) * S)
                outs[0][g] = ckv[rows, :]
                outs[1][g] = kpe[rows, 64:96]
                outs[2][g] = wk[rows, :]
                outs[3][g] = wv[rows, :]
                outs[4][g] = nk[rows, :]
                outs[5][g] = nv[rows, :]


def _emit_caches(projs, ckvs, B, S):
    nb = 4
    while B % nb:
        nb //= 2

    def layer_specs(a):
        row = lambda l, b: jnp.where(l == a, b, 0)
        col = lambda w, off: pl.BlockSpec((nb * S, w), lambda l, b: (row(l, b), off // w))
        return [pl.BlockSpec((nb * S, 128), lambda l, b: (row(l, b), 0)), col(128, P_KPE), col(128, P_WK),
                col(128, P_WV), col(256, P_NK), col(256, P_NV)]

    in_specs, args = [], []
    for a in range(DEPTH):
        in_specs += layer_specs(a)
        args += [ckvs[a]] + [projs[a]] * 5
    widths = (128, 32, 128, 128, 256, 256)
    return pl.pallas_call(
        functools.partial(_caches_kernel, nb=nb, S=S),
        grid=(DEPTH, B // nb),
        in_specs=in_specs,
        out_specs=[pl.BlockSpec((nb, None, S, w), lambda l, b: (b, l, 0, 0)) for w in widths],
        out_shape=[jax.ShapeDtypeStruct((B, DEPTH, S, w), F32) for w in widths],
        compiler_params=_cp("arbitrary", "arbitrary"),
        name="context_tensors",
    )(*args)


def _rot_cols(w, q):
    a, b, c, d = w[..., :q], w[..., q:2 * q], w[..., 2 * q:3 * q], w[..., 3 * q:]
    return jnp.concatenate([-b, a, -d, c], -1)


def _prep_w_in(w):
    z = lambda n: jnp.zeros((D, n), w.dtype)
    qlat, ckv, kpe, hy = w[:, 0:256], w[:, 256:384], w[:, 384:416], w[:, 416:1184]
    wq, wk, wv = w[:, 1184:1440], w[:, 1440:1568], w[:, 1568:1696]
    nq, nk, nv, gate = w[:, 1696:1952], w[:, 1952:2208], w[:, 2208:2464], w[:, 2464:]
    wq_r = _rot_cols(wq.reshape(D, 4, 64), 16).reshape(D, 256)
    wk_r = _rot_cols(wk.reshape(D, 2, 64), 16).reshape(D, 128)
    kpe_r = _rot_cols(kpe, 8)
    cols = [qlat, ckv, z(64), kpe, z(32), hy, wq, wk, wv, nq, nk, nv, wq_r, wk_r, z(64), kpe_r, z(32), gate]
    return jnp.concatenate(cols, 1).astype(BF16)


def _prep_mla(w_uq, w_ukv):
    uq = w_uq.reshape(256, 4, 96)
    nope, pe = uq[..., :64], uq[..., 64:]
    z32 = jnp.zeros((256, 4, 32), w_uq.dtype)
    z64 = jnp.zeros((256, 4, 64), w_uq.dtype)
    wcat = jnp.concatenate([nope, pe, z32], -1).reshape(256, 512).astype(BF16)
    wrot = jnp.concatenate([z64, _rot_cols(pe, 8), z32], -1).reshape(256, 512).astype(BF16)
    ukv = w_ukv.reshape(128, 4, 128)
    wk = jnp.concatenate([ukv[..., :64], jnp.zeros((128, 4, 64), w_ukv.dtype)], -1).reshape(128, 512).astype(BF16)
    wv = ukv[..., 64:].reshape(128, 256).astype(BF16)
    return wcat, wrot, wk, wv


def _rope_tab(L, q):
    t = jnp.arange(L)
    inv = ROPE_BASE ** (-jnp.arange(q, dtype=F32) / q)
    ar = (t // GRID_W).astype(F32)[:, None] * inv[None, :]
    ac = (t % GRID_W).astype(F32)[:, None] * inv[None, :]
    cos = jnp.concatenate([jnp.cos(ar), jnp.cos(ar), jnp.cos(ac), jnp.cos(ac)], 1)
    sin = jnp.concatenate([jnp.sin(ar), jnp.sin(ar), jnp.sin(ac), jnp.sin(ac)], 1)
    return cos, sin


def _rope_tables(L):
    c8, s8 = _rope_tab(L, 8)
    c16, s16 = _rope_tab(L, 16)
    one, zero = jnp.ones((L, 64), F32), jnp.zeros((L, 64), F32)
    z32 = jnp.zeros((L, 32), F32)
    mla_q = (jnp.tile(jnp.concatenate([one, c8, z32], 1), (1, 4)), jnp.tile(jnp.concatenate([zero, s8, z32], 1), (1, 4)))
    mla_k = (jnp.concatenate([zero, c8, z32], 1), jnp.concatenate([zero, s8, z32], 1))
    win = (jnp.tile(c16, (1, 4)), jnp.tile(s16, (1, 4)), jnp.tile(c16, (1, 2)), jnp.tile(s16, (1, 2)))
    return mla_q + mla_k, win


def _hyena(proj, lp, dft, NB, Lb):
    cm, sm, smt = dft
    tm = min(Lb, 512)
    v, x1, x2 = _short_conv(proj, lp["hy_conv_w"], lp["hy_conv_b"].reshape(1, -1), NB, Lb)
    w1p = jnp.pad(lp["hy_w1"], ((0, 128 - lp["hy_w1"].shape[0]), (0, 0)))
    fs, nyq = _hy_filter(Lb, w1p, lp["hy_b1"].reshape(1, -1), lp["hy_w2"], lp["hy_b2"].reshape(1, -1), lp["hy_w3"],
                         lp["hy_sin_freq"], lp["hy_log_decay"].reshape(1, -1))
    gr, gi = _hy_gdft(cm, sm, fs, nyq, Lb, tm)
    skip = lp["hy_skip"].reshape(2, 1, HY_C)
    z = v
    for n, gate in enumerate((x1, x2)):
        yr, yi = _hy_fwd(cm, sm, z, gr, gi, n, NB, Lb, tm)
        z = _hy_inv(cm, smt, yr, yi, z, gate, skip, n, NB, Lb, tm)
    return z


def _layer_steps(x, mod, lp, l, NB, Lb, mod_row_of_batch, dft, cache=None, tabs=None, na_bias=None):
    T = NB * Lb
    latent = cache is not None
    bm = 256
    rows_of = lambda n: (lambda i: mod_row_of_batch((i * n) // Lb))
    mod_row = rows_of(bm)
    span = Lb if latent else T
    bmp = min(span, 1024)
    proj, gates = _in_proj(x, mod, lp["w_in_p"], rows_of(bmp), bmp)

    gq, gkv = lp["mla_q_norm"].reshape(1, -1), lp["mla_kv_norm"].reshape(1, -1)
    wcat, wrot, wk, wv = lp["mla_w"]
    q_all, ckv_n, kpe_r = _mla_q(proj, gq, gkv, wcat, wrot, tabs[0] if latent else None, Lb, min(span, 512))
    if latent:
        ckv_c, kpe_c, kc_c, vc_c, kd_c, vd_c = cache
        Lc = ckv_c.shape[1]
        kpe_cp = jnp.pad(kpe_c, ((0, 0), (0, 0), (64, 32)))
        ckv_all = jnp.concatenate([ckv_c, ckv_n.reshape(NB, Lb, 128)], 1).reshape(NB * (Lc + Lb), 128)
        kpe_all = jnp.concatenate([kpe_cp, kpe_r.reshape(NB, Lb, 128)], 1).reshape(NB * (Lc + Lb), 128)
        k_all, v_all = _mla_kv(ckv_all, kpe_all, wk, wv, 512)
        oc = _lat_win_attention(proj, kc_c.reshape(NB, Lc, 128), vc_c.reshape(NB, Lc, 128), tabs[1],
                                lp["win_sink"], NB, Lb)
        od = _lat_na_attention(proj, kd_c.reshape(NB, Lc, 256), vd_c.reshape(NB, Lc, 256), na_bias, NB, Lb)
        ob = _hyena(proj, lp, dft, NB, Lb)
        after = yield "projected", od
        oa = _lat_mla_attention(q_all, k_all, v_all, NB, Lb, Lc + Lb, 256, after=after)
        after = None
    else:
        k_all, v_all = _mla_kv(ckv_n, kpe_r, wk, wv, 512)
        oa, oc, od = _ctx_attention(proj, q_all, k_all, v_all, lp["win_sink"], NB, Lb)
        ob = _hyena(proj, lp, dft, NB, Lb)
        after = yield "projected", oa

    bmm = min(span, 512)
    x1, hp, logits_t = _merge((oa, ob, oc, od), gates, lp["w_branch_b"], lp["w_out_b"], x, mod,
                              lp["ln1_g"].reshape(1, -1), lp["ln1_b"].reshape(1, -1), lp["moe_router"].T,
                              rows_of(bmm), bmm, after=after)
    n_slots = T * TOP_K + N_EXPERTS * MOE_TM
    gate_t, rank, cnt = _router(logits_t, lp["moe_bias"].reshape(-1, 1), 512)
    pos, w8, te, nx, nt = _route_pos(gate_t, rank, cnt, 512, MOE_TM, n_slots // MOE_TM)
    xs = _sc_dispatch(pos.reshape(-1), hp, n_slots)
    after = yield "dispatched", None
    ys = _gmm(te.reshape(-1), nx.reshape(-1), nt.reshape(-1)[:1], xs, lp["moe_w1"], lp["moe_w3"], lp["moe_w2"], l,
              MOE_TM, after=after)
    yield "ffn", ys
    yk = _sc_gather(ys, pos.reshape(-1)).reshape(TOP_K, T, D // 2)
    x2 = _combine(yk, w8.T, hp, lp["sh_w1_b"], lp["sh_w3_b"], lp["sh_w2_b"], x1, mod,
                  lp["ln2_g"].reshape(1, -1), lp["ln2_b"].reshape(1, -1), mod_row, bm)
    yield "done", (x2, proj, ckv_n)


def kernel(x_prompt, x_sample, cache_mla_ckv, cache_mla_kpe, cache_win_k, cache_win_v, cache_na_k, cache_na_v, c, c_ctx, w_ada, b_ada, w_in, mla_q_norm, mla_kv_norm, mla_w_uq, mla_w_ukv, hy_conv_w, hy_conv_b, hy_w1, hy_b1, hy_w2, hy_b2, hy_w3, hy_sin_freq, hy_log_decay, hy_skip, win_sink, na_rpb, w_branch, w_out, ln1_g, ln1_b, ln2_g, ln2_b, moe_router, moe_bias, moe_w1, moe_w3, moe_w2, sh_w1, sh_w3, sh_w2):
    B, S, _ = x_prompt.shape
    DB, DS, _ = x_sample.shape
    xp = x_prompt.reshape(B * S, D)
    xs = x_sample.reshape(DB * DS, D)
    cvec = jnp.concatenate([c_ctx[None, :], c, jnp.zeros((8 - 1 - DB, D), F32)], 0)
    dft_ctx = _dft_mats(S)
    dft_lat = _dft_mats(DS)
    tabs = _rope_tables(DS)
    projs, ckvs = [], []

    def params(l):
        return dict(w_in_p=_prep_w_in(w_in[l]), mla_q_norm=mla_q_norm[l], mla_kv_norm=mla_kv_norm[l],
                    mla_w=_prep_mla(mla_w_uq[l], mla_w_ukv[l]), hy_conv_w=hy_conv_w[l], hy_conv_b=hy_conv_b[l],
                    hy_w1=hy_w1[l], hy_b1=hy_b1[l], hy_w2=hy_w2[l], hy_b2=hy_b2[l], hy_w3=hy_w3[l],
                    hy_sin_freq=hy_sin_freq[l], hy_log_decay=hy_log_decay[l], hy_skip=hy_skip[l],
                    win_sink=win_sink[l], w_branch_b=w_branch[l].astype(BF16), w_out_b=w_out[l].astype(BF16),
                    ln1_g=ln1_g[l], ln1_b=ln1_b[l], ln2_g=ln2_g[l], ln2_b=ln2_b[l],
                    moe_router=moe_router[l], moe_bias=moe_bias[l], moe_w1=moe_w1, moe_w3=moe_w3, moe_w2=moe_w2,
                    sh_w1_b=sh_w1[l].astype(BF16), sh_w3_b=sh_w3[l].astype(BF16), sh_w2_b=sh_w2[l].astype(BF16))

    lps = [params(l) for l in range(DEPTH)]
    mods = [_modulation(cvec, w_ada, b_ada, l) for l in range(DEPTH)]

    def ctx_layer(l, x):
        return _layer_steps(x, mods[l], lps[l], l, B, S, lambda b: 0, dft_ctx)

    def lat_layer(l, x):
        cache = (cache_mla_ckv[:, l], cache_mla_kpe[:, l], cache_win_k[:, l], cache_win_v[:, l],
                 cache_na_k[:, l], cache_na_v[:, l])
        return _layer_steps(x, mods[l], lps[l], l, DB, DS, lambda b: 1 + b, dft_lat, cache=cache, tabs=tabs,
                            na_bias=_na_bias(na_rpb[l]))

    ctx = ctx_layer(0, xp)
    next(ctx)
    ctx.send(None)
    for l in range(DEPTH):
        lat = lat_layer(l, xs)
        lat_local = next(lat)[1]
        ys_ctx = ctx.send(lat_local)[1]
        lat.send(ys_ctx)
        xp, proj, ckv_n = ctx.send(None)[1]
        projs.append(proj)
        ckvs.append(ckv_n)
        if l + 1 < DEPTH:
            ctx = ctx_layer(l + 1, xp)
            attended = next(ctx)[1]
            ys_lat = lat.send(attended)[1]
            ctx.send(ys_lat)
        else:
            ckv, kpe, wk, wv, nk, nv = _emit_caches(projs, ckvs, B, S)
            lat.send((xp, nv))
        xs = lat.send(None)[1][0]
    heads = lambda t, h: t.reshape(B, DEPTH, S, h, HEAD_DIM)
    return (xp.reshape(B, S, D), xs.reshape(DB, DS, D), ckv, kpe, heads(wk, 2), heads(wv, 2), heads(nk, 4),
            heads(nv, 4))
```

```python
import functools
import math

import jax
import jax.numpy as jnp
from jax import lax
from jax.experimental import pallas as pl
from jax.experimental.pallas import tpu as pltpu
from jax.experimental.pallas import tpu_sc as plsc

F32 = jnp.float32
BF16 = jnp.bfloat16

D = 1024
DEPTH = 2
GRID_W = 64
HEAD_DIM = 64
LOG2E = math.log2(math.e)
MLA_SCALE = 96 ** -0.5 * LOG2E
ATT_SCALE = HEAD_DIM ** -0.5 * LOG2E
HY_C = 256
HY_BANDS = 8
NA_KH = 8
NA_KW = 16
N_EXPERTS = 64
N_GROUPS = 8
TOP_K = 8
TOPK_GROUPS = 4
D_EXPERT = 256
ROUTED_SCALE = 2.5
ROPE_BASE = 10000.0
LN_EPS = 1e-5
RMS_EPS = 1e-6
NEG = -1e30
DN_ALPHA = (2 * DEPTH) ** 0.25

P_QLAT, P_CKV, P_KPE, P_HY = 0, 256, 384, 512
P_WQ, P_WK, P_WV = 1280, 1536, 1664
P_NQ, P_NK, P_NV = 1792, 2048, 2304
P_WQR, P_WKR, P_KPER, P_GATE = 2560, 2816, 2944, 3072
N_PROJ = 7168

VMEM_LIMIT = 56 * 1024 * 1024

SC_CORES = 2
SC_SUBCORES = 16
SC_LANES = 16
SC_WORKERS = SC_CORES * SC_SUBCORES
SC_ROWS = 64

MOE_TM = 512

def _cp(*sem):
    return pltpu.CompilerParams(dimension_semantics=sem, vmem_limit_bytes=VMEM_LIMIT)


def _sigmoid(x):
    return 1.0 / (1.0 + jnp.exp(-x))


def _dot(a, b):
    return jnp.dot(a, b, preferred_element_type=F32)


def _dot_nt(a, b):
    return lax.dot_general(a, b, (((1,), (1,)), ((), ())), preferred_element_type=F32)


def _dot_hi(a, b):
    return jnp.dot(a, b, preferred_element_type=F32, precision=lax.Precision.HIGHEST)


def _pack_pairs(x):
    w = x.shape[1] // 2
    hi = lax.bitcast_convert_type(x[:, :w].astype(BF16).astype(F32), jnp.int32)
    lo = lax.bitcast_convert_type(x[:, w:].astype(BF16).astype(F32), jnp.int32)
    return hi | lax.shift_right_logical(lo, 16)


def _unpack_pairs(p):
    hi = lax.bitcast_convert_type(p & jnp.int32(-65536), F32)
    lo = lax.bitcast_convert_type(lax.shift_left(p, 16), F32)
    return hi, lo


def _layer_norm(x, g, b):
    mu = jnp.mean(x, -1, keepdims=True)
    xc = x - mu
    var = jnp.mean(xc * xc, -1, keepdims=True)
    return xc * lax.rsqrt(var + LN_EPS) * g + b


def _rms_norm(x, g):
    return x * lax.rsqrt(jnp.mean(x * x, -1, keepdims=True) + RMS_EPS) * g


def _mod_kernel(c_ref, w_ref, b_ref, o_ref):
    c = c_ref[...]
    a = (c * _sigmoid(c)).astype(BF16)
    o_ref[...] = _dot(a, w_ref[...].astype(BF16)) + b_ref[...]


def _modulation(cvec, w_ada, b_ada, l):
    out = pl.pallas_call(
        _mod_kernel,
        grid=(6,),
        in_specs=[pl.BlockSpec((8, D), lambda j: (0, 0)),
                  pl.BlockSpec((None, D, D), lambda j: (l, 0, j)),
                  pl.BlockSpec((None, 1, D), lambda j: (l, 0, j))],
        out_specs=pl.BlockSpec((8, D), lambda j: (0, j)),
        out_shape=jax.ShapeDtypeStruct((8, 6 * D), F32),
        compiler_params=_cp("arbitrary"),
        name="modulation",
    )(cvec, w_ada, b_ada.reshape(DEPTH, 1, 6 * D))
    return out.reshape(8, 6, D)


def _inproj_kernel(x_ref, mod_ref, w_ref, o_ref, g_ref, h_ref, *, n_main):
    j = pl.program_id(1)

    @pl.when(j == 0)
    def _():
        m = mod_ref[...]
        h_ref[...] = (x_ref[...] * (1.0 + m[1:2, :]) + m[0:1, :]).astype(BF16)

    y = _dot(h_ref[...], w_ref[...])

    @pl.when(j < n_main)
    def _():
        o_ref[...] = y

    @pl.when(j >= n_main)
    def _():
        g_ref[...] = y.astype(BF16)


def _in_proj(x, mod, w_p, mod_row, bm, bn=1024):
    T = x.shape[0]
    n_main = P_GATE // bn
    return pl.pallas_call(
        functools.partial(_inproj_kernel, n_main=n_main),
        grid=(T // bm, N_PROJ // bn),
        in_specs=[pl.BlockSpec((bm, D), lambda i, j: (i, 0)),
                  pl.BlockSpec((None, 6, D), lambda i, j: (mod_row(i), 0, 0)),
                  pl.BlockSpec((D, bn), lambda i, j: (0, j))],
        out_specs=[pl.BlockSpec((bm, bn), lambda i, j: (i, jnp.minimum(j, n_main - 1))),
                   pl.BlockSpec((bm, bn), lambda i, j: (i, jnp.maximum(j - n_main, 0)))],
        out_shape=[jax.ShapeDtypeStruct((T, P_GATE), F32), jax.ShapeDtypeStruct((T, N_PROJ - P_GATE), BF16)],
        scratch_shapes=[pltpu.VMEM((bm, D), BF16)],
        compiler_params=_cp("arbitrary", "arbitrary"),
        name="in_proj",
    )(x, mod, w_p)


def _mla_q_kernel(*refs, rope):
    if rope:
        (ql_ref, ckv_ref, kpe_ref, kper_ref, gq_ref, gkv_ref, wc_ref, wr_ref,
         cq_ref, sq_ref, ck_ref, sk_ref, q_ref, ckvn_ref, kpeo_ref) = refs
    else:
        ql_ref, ckv_ref, kpe_ref, gq_ref, gkv_ref, wc_ref, q_ref, ckvn_ref, kpeo_ref = refs
    qn = _rms_norm(ql_ref[...], gq_ref[...]).astype(BF16)
    q = _dot(qn, wc_ref[...])
    if rope:
        q = q * cq_ref[...] + _dot(qn, wr_ref[...]) * sq_ref[...]
        kpeo_ref[...] = kpe_ref[...] * ck_ref[...] + kper_ref[...] * sk_ref[...]
    else:
        kpeo_ref[...] = kpe_ref[...]
    q_ref[...] = (q * MLA_SCALE).astype(BF16)
    ckvn_ref[...] = _rms_norm(ckv_ref[...], gkv_ref[...])


def _mla_q(proj, gq, gkv, wcat, wrot, tabs, Lb, bm):
    T = proj.shape[0]
    rope = tabs is not None
    nl = Lb // bm
    col = lambda c: (lambda i: (i, c))
    fixed = lambda i: (0, 0)
    in_specs = [pl.BlockSpec((bm, 256), col(P_QLAT // 256)),
                pl.BlockSpec((bm, 128), col(P_CKV // 128)),
                pl.BlockSpec((bm, 128), col(P_KPE // 128))]
    args = [proj, proj, proj]
    if rope:
        in_specs.append(pl.BlockSpec((bm, 128), col(P_KPER // 128)))
        args.append(proj)
    in_specs += [pl.BlockSpec((1, 256), fixed), pl.BlockSpec((1, 128), fixed), pl.BlockSpec((256, 512), fixed)]
    args += [gq, gkv, wcat]
    if rope:
        cq, sq, ck, sk = tabs
        pos = lambda i: (i % nl, 0)
        in_specs += [pl.BlockSpec((256, 512), fixed), pl.BlockSpec((bm, 512), pos), pl.BlockSpec((bm, 512), pos),
                     pl.BlockSpec((bm, 128), pos), pl.BlockSpec((bm, 128), pos)]
        args += [wrot, cq, sq, ck, sk]
    return pl.pallas_call(
        functools.partial(_mla_q_kernel, rope=rope),
        grid=(T // bm,),
        in_specs=in_specs,
        out_specs=[pl.BlockSpec((bm, 512), lambda i: (i, 0)),
                   pl.BlockSpec((bm, 128), lambda i: (i, 0)),
                   pl.BlockSpec((bm, 128), lambda i: (i, 0))],
        out_shape=[jax.ShapeDtypeStruct((T, 512), BF16),
                   jax.ShapeDtypeStruct((T, 128), F32),
                   jax.ShapeDtypeStruct((T, 128), F32)],
        compiler_params=_cp("arbitrary"),
        name="mla_q",
    )(*args)


def _mla_kv_kernel(ckv_ref, kpe_ref, wk_ref, wv_ref, k_ref, v_ref):
    c = ckv_ref[...].astype(BF16)
    kpe = kpe_ref[...]
    k_ref[...] = (_dot(c, wk_ref[...]) + jnp.concatenate([kpe] * 4, axis=1)).astype(BF16)
    v_ref[...] = _dot(c, wv_ref[...]).astype(BF16)


def _mla_kv(ckv, kpe, wk, wv, bm):
    Tk = ckv.shape[0]
    return pl.pallas_call(
        _mla_kv_kernel,
        grid=(Tk // bm,),
        in_specs=[pl.BlockSpec((bm, 128), lambda i: (i, 0)),
                  pl.BlockSpec((bm, 128), lambda i: (i, 0)),
                  pl.BlockSpec((128, 512), lambda i: (0, 0)),
                  pl.BlockSpec((128, 256), lambda i: (0, 0))],
        out_specs=[pl.BlockSpec((bm, 512), lambda i: (i, 0)),
                   pl.BlockSpec((bm, 256), lambda i: (i, 0))],
        out_shape=[jax.ShapeDtypeStruct((Tk, 512), BF16),
                   jax.ShapeDtypeStruct((Tk, 256), BF16)],
        compiler_params=_cp("arbitrary"),
        name="mla_kv",
    )(ckv, kpe, wk, wv)


def _attn_core(q, kvs, masks, sink):
    ss = []
    for (k, _), mk in zip(kvs, masks):
        s = _dot_nt(q, k)
        if mk is not None:
            s = s + mk[1] if mk[0] == "add" else jnp.where(mk[1], s, NEG)
        ss.append(s)
    m = ss[0].max(-1, keepdims=True)
    for s in ss[1:]:
        m = jnp.maximum(m, s.max(-1, keepdims=True))
    if sink is not None:
        m = jnp.maximum(m, sink)
    den = None
    acc = None
    for s, (_, v) in zip(ss, kvs):
        p = jnp.exp2(s - m)
        d = p.sum(-1, keepdims=True)
        a = _dot(p.astype(BF16), v)
        den = d if den is None else den + d
        acc = a if acc is None else acc + a
    if sink is not None:
        den = den + jnp.exp2(sink - m)
    return acc / den


def _ctx_attn_kernel(qm_ref, km_ref, vm_ref, wq_ref, wk_ref, wv_ref, nq_ref, nk_ref, nv_ref, sink_ref, *rest):
    om_ref, ow_ref, on_ref = rest[-3:]
    for h in range(4):
        q = qm_ref[:, 128 * h:128 * (h + 1)]
        k = km_ref[:, 128 * h:128 * (h + 1)]
        v = vm_ref[:, 64 * h:64 * (h + 1)]
        om_ref[:, 64 * h:64 * (h + 1)] = _attn_core(q, [(k, v)], [None], None)
    for h in range(4):
        g = h // 2
        q = (wq_ref[:, 64 * h:64 * (h + 1)] * ATT_SCALE).astype(BF16)
        k = wk_ref[:, 64 * g:64 * (g + 1)].astype(BF16)
        v = wv_ref[:, 64 * g:64 * (g + 1)].astype(BF16)
        ow_ref[:, 64 * h:64 * (h + 1)] = _attn_core(q, [(k, v)], [None], sink_ref[h] * LOG2E)
    for h in range(4):
        q = (nq_ref[:, 64 * h:64 * (h + 1)] * ATT_SCALE).astype(BF16)
        k = nk_ref[:, 64 * h:64 * (h + 1)].astype(BF16)
        v = nv_ref[:, 64 * h:64 * (h + 1)].astype(BF16)
        on_ref[:, 64 * h:64 * (h + 1)] = _attn_core(q, [(k, v)], [None], None)


def _ctx_attention(proj, q_all, k_all, v_all, sink, NB, Lb, after=None):
    T = proj.shape[0]
    pc = lambda w, off: pl.BlockSpec((Lb, w), lambda b: (b, off // w))
    row = lambda w: pl.BlockSpec((Lb, w), lambda b: (b, 0))
    dep_specs, dep_args = _after(after)
    return pl.pallas_call(
        _ctx_attn_kernel,
        grid=(NB,),
        in_specs=[row(512), row(512), row(256),
                  pc(256, P_WQ), pc(128, P_WK), pc(128, P_WV),
                  pc(256, P_NQ), pc(256, P_NK), pc(256, P_NV),
                  pl.BlockSpec(memory_space=pltpu.SMEM)] + dep_specs,
        out_specs=[row(256), row(256), row(256)],
        out_shape=[jax.ShapeDtypeStruct((T, 256), F32)] * 3,
        compiler_params=_cp("arbitrary"),
        name="ctx_attention",
    )(q_all, k_all, v_all, proj, proj, proj, proj, proj, proj, sink, *dep_args)


def _lat_mla_kernel(q_ref, k_ref, v_ref, *rest):
    o_ref = rest[-1]
    for h in range(4):
        q = q_ref[:, 128 * h:128 * (h + 1)]
        k = k_ref[:, 128 * h:128 * (h + 1)]
        v = v_ref[:, 64 * h:64 * (h + 1)]
        o_ref[:, 64 * h:64 * (h + 1)] = _attn_core(q, [(k, v)], [None], None)


def _after(after):
    deps = [] if after is None else list(after) if isinstance(after, (tuple, list)) else [after]
    return [pl.BlockSpec(memory_space=pl.ANY)] * len(deps), deps


def _lat_mla_attention(q_all, k_all, v_all, NB, Lb, Lk, tq, after=None):
    T = q_all.shape[0]
    nq = Lb // tq
    dep_specs, dep_args = _after(after)
    return pl.pallas_call(
        _lat_mla_kernel,
        grid=(NB, nq),
        in_specs=[pl.BlockSpec((tq, 512), lambda b, i: (b * nq + i, 0)),
                  pl.BlockSpec((Lk, 512), lambda b, i: (b, 0)),
                  pl.BlockSpec((Lk, 256), lambda b, i: (b, 0))] + dep_specs,
        out_specs=pl.BlockSpec((tq, 256), lambda b, i: (b * nq + i, 0)),
        out_shape=jax.ShapeDtypeStruct((T, 256), F32),
        compiler_params=_cp("arbitrary", "arbitrary"),
        name="lat_mla_attention",
    )(q_all, k_all, v_all, *dep_args)


def _attn_local_ctx(q, locs, kc, vc, sink):
    s_ctx = _dot_nt(q, kc)
    m_ctx = s_ctx.max(-1, keepdims=True)
    if sink is not None:
        m_ctx = jnp.maximum(m_ctx, sink)
    ms, dens, accs = [], [], []
    for rs, k, v, mk in locs:
        s = _dot_nt(q[rs], k)
        s = s + mk[1] if mk[0] == "add" else jnp.where(mk[1], s, NEG)
        m = jnp.maximum(s.max(-1, keepdims=True), m_ctx[rs])
        p = jnp.exp2(s - m)
        ms.append(m)
        dens.append(p.sum(-1, keepdims=True))
        accs.append(_dot(p.astype(BF16), v))
    m = jnp.concatenate(ms, axis=0)
    p = jnp.exp2(s_ctx - m)
    den = jnp.concatenate(dens, axis=0) + p.sum(-1, keepdims=True)
    if sink is not None:
        den = den + jnp.exp2(sink - m)
    return (jnp.concatenate(accs, axis=0) + _dot(p.astype(BF16), vc)) / den


def _lat_win_kernel(q_ref, qr_ref, k_ref, kr_ref, v_ref, kc_ref, vc_ref, cq_ref, sq_ref, ck_ref, sk_ref,
                    sink_ref, o_ref, *, Lb, bpt):
    t = pl.program_id(1)
    q = (q_ref[...] * cq_ref[...] + qr_ref[...] * sq_ref[...]) * ATT_SCALE
    kc = kc_ref[...].astype(BF16)
    vc = vc_ref[...].astype(BF16)
    blocks = []
    for bb in range(bpt):
        i = t * bpt + bb
        start = pl.multiple_of(jnp.clip((i - 1) * 128, 0, Lb - 384), 128)
        win = pl.ds(start, 384)
        kk = (k_ref[win, :] * ck_ref[win, :] + kr_ref[win, :] * sk_ref[win, :]).astype(BF16)
        qpos = i * 128 + lax.broadcasted_iota(jnp.int32, (128, 384), 0)
        kpos = start + lax.broadcasted_iota(jnp.int32, (128, 384), 1)
        blocks.append((kk, v_ref[win, :].astype(BF16), jnp.abs(qpos - kpos) <= 128))
    for h in range(4):
        g = h // 2
        sl = slice(64 * g, 64 * (g + 1))
        locs = [(slice(128 * bb, 128 * (bb + 1)), kk[:, sl], vv[:, sl], ("keep", valid))
                for bb, (kk, vv, valid) in enumerate(blocks)]
        qh = q[:, 64 * h:64 * (h + 1)].astype(BF16)
        o_ref[:, 64 * h:64 * (h + 1)] = _attn_local_ctx(qh, locs, kc[:, sl], vc[:, sl], sink_ref[h] * LOG2E)


def _lat_win_attention(proj, kc, vc, tabs, sink, NB, Lb):
    T = proj.shape[0]
    bpt = 1
    tq = 128 * bpt
    nt = Lb // tq
    Lc = kc.shape[1]
    cq, sq, ck, sk = tabs
    qspec = lambda off: pl.BlockSpec((tq, 256), lambda b, i: (b * nt + i, off // 256))
    kspec = lambda off: pl.BlockSpec((Lb, 128), lambda b, i: (b, off // 128))
    cspec = pl.BlockSpec((None, Lc, 128), lambda b, i: (b, 0, 0))
    return pl.pallas_call(
        functools.partial(_lat_win_kernel, Lb=Lb, bpt=bpt),
        grid=(NB, nt),
        in_specs=[qspec(P_WQ), qspec(P_WQR), kspec(P_WK), kspec(P_WKR), kspec(P_WV), cspec, cspec,
                  pl.BlockSpec((tq, 256), lambda b, i: (i, 0)), pl.BlockSpec((tq, 256), lambda b, i: (i, 0)),
                  pl.BlockSpec((Lb, 128), lambda b, i: (0, 0)), pl.BlockSpec((Lb, 128), lambda b, i: (0, 0)),
                  pl.BlockSpec(memory_space=pltpu.SMEM)],
        out_specs=pl.BlockSpec((tq, 256), lambda b, i: (b * nt + i, 0)),
        out_shape=jax.ShapeDtypeStruct((T, 256), F32),
        compiler_params=_cp("arbitrary", "arbitrary"),
        name="lat_win_attention",
    )(proj, proj, proj, proj, proj, kc, vc, cq, sq, ck, sk, sink)


def _na_bias_kernel(rpb_ref, o_ref):
    h = pl.program_id(0)
    qc = lax.broadcasted_iota(jnp.int32, (GRID_W, GRID_W), 0)
    kc = lax.broadcasted_iota(jnp.int32, (GRID_W, GRID_W), 1)
    dc = kc - qc + (NA_KW - 1)
    wstart = jnp.clip(qc - NA_KW // 2, 0, GRID_W - NA_KW)
    ok = (kc >= wstart) & (kc < wstart + NA_KW)
    n_dc = 2 * NA_KW - 1
    n_dr = 2 * NA_KH - 1
    tabs = []
    for dr in range(n_dr):
        t = jnp.zeros((GRID_W, GRID_W), F32)
        for j in range(n_dc):
            t = jnp.where(dc == j, rpb_ref[(h * n_dr + dr) * n_dc + j], t)
        tabs.append(jnp.where(ok, t * LOG2E, NEG))
    for o in range(NA_KH):
        for a in range(NA_KH):
            o_ref[o, :, GRID_W * a:GRID_W * (a + 1)] = tabs[a + NA_KH - 1 - o]


def _na_bias(rpb):
    H = rpb.shape[0]
    return pl.pallas_call(
        _na_bias_kernel,
        grid=(H,),
        in_specs=[pl.BlockSpec(memory_space=pltpu.SMEM)],
        out_specs=pl.BlockSpec((None, NA_KH, GRID_W, NA_KH * GRID_W), lambda h: (h, 0, 0, 0)),
        out_shape=jax.ShapeDtypeStruct((H, NA_KH, GRID_W, NA_KH * GRID_W), F32),
        compiler_params=_cp("arbitrary"),
        name="na_bias",
    )(rpb.reshape(-1))


def _lat_na_kernel(q_ref, k_ref, v_ref, kc_ref, vc_ref, bias_ref, o_ref, *, rows, rpt):
    t = pl.program_id(1)
    q = q_ref[...] * ATT_SCALE
    kc = kc_ref[...].astype(BF16)
    vc = vc_ref[...].astype(BF16)
    bands = []
    for rr in range(rpt):
        r = t * rpt + rr
        first = jnp.clip(r - NA_KH // 2, 0, rows - NA_KH)
        win = pl.ds(pl.multiple_of(first * GRID_W, GRID_W), NA_KH * GRID_W)
        bands.append((k_ref[win, :].astype(BF16), v_ref[win, :].astype(BF16), r - first))
    for h in range(4):
        sl = slice(64 * h, 64 * (h + 1))
        locs = [(slice(GRID_W * rr, GRID_W * (rr + 1)), kk[:, sl], vv[:, sl], ("add", bias_ref[h, off]))
                for rr, (kk, vv, off) in enumerate(bands)]
        o_ref[:, sl] = _attn_local_ctx(q[:, sl].astype(BF16), locs, kc[:, sl], vc[:, sl], None)


def _lat_na_attention(proj, kc, vc, bias, NB, Lb):
    T = proj.shape[0]
    rows = Lb // GRID_W
    rpt = 8
    tq = GRID_W * rpt
    nt = rows // rpt
    Lc = kc.shape[1]
    kspec = lambda off: pl.BlockSpec((Lb, 256), lambda b, t: (b, off // 256))
    cspec = pl.BlockSpec((None, Lc, 256), lambda b, t: (b, 0, 0))
    return pl.pallas_call(
        functools.partial(_lat_na_kernel, rows=rows, rpt=rpt),
        grid=(NB, nt),
        in_specs=[pl.BlockSpec((tq, 256), lambda b, t: (b * nt + t, P_NQ // 256)),
                  kspec(P_NK), kspec(P_NV), cspec, cspec,
                  pl.BlockSpec((4, NA_KH, GRID_W, NA_KH * GRID_W), lambda b, t: (0, 0, 0, 0))],
        out_specs=pl.BlockSpec((tq, 256), lambda b, t: (b * nt + t, 0)),
        out_shape=jax.ShapeDtypeStruct((T, 256), F32),
        compiler_params=_cp("arbitrary", "arbitrary"),
        name="lat_na_attention",
    )(proj, proj, proj, kc, vc, bias)


def _short_conv_kernel(a_ref, b_ref, c_ref, w_ref, bias_ref, oa_ref, ob_ref, oc_ref, *, L):
    t = lax.broadcasted_iota(jnp.int32, (L, HY_C), 0)
    for n, (x_ref, o_ref) in enumerate(((a_ref, oa_ref), (b_ref, ob_ref), (c_ref, oc_ref))):
        sl = slice(HY_C * n, HY_C * (n + 1))
        x = x_ref[...]
        prev = jnp.where(t == 0, 0.0, pltpu.roll(x, 1, axis=0))
        nxt = jnp.where(t == L - 1, 0.0, pltpu.roll(x, L - 1, axis=0))
        o_ref[...] = prev * w_ref[0:1, sl] + x * w_ref[1:2, sl] + nxt * w_ref[2:3, sl] + bias_ref[:, sl]


def _short_conv(proj, w, b, NB, Lb):
    T = proj.shape[0]
    spec = lambda c: pl.BlockSpec((Lb, HY_C), lambda i: (i, c))
    return pl.pallas_call(
        functools.partial(_short_conv_kernel, L=Lb),
        grid=(NB,),
        in_specs=[spec(P_HY // HY_C), spec(P_HY // HY_C + 1), spec(P_HY // HY_C + 2),
                  pl.BlockSpec((3, 3 * HY_C), lambda i: (0, 0)), pl.BlockSpec((1, 3 * HY_C), lambda i: (0, 0))],
        out_specs=[spec(0)] * 3,
        out_shape=[jax.ShapeDtypeStruct((T, HY_C), F32)] * 3,
        compiler_params=_cp("arbitrary"),
        name="hyena_short_conv",
    )(proj, proj, proj, w, b)


def _hy_filter_kernel(w1_ref, b1_ref, w2_ref, b2_ref, w3_ref, freq_ref, ld_ref, fs_ref, nyq_ref, *, L):
    ti = lax.broadcasted_iota(jnp.int32, (L, 128), 0)
    t = ti.astype(F32)
    j = lax.broadcasted_iota(jnp.int32, (L, 128), 1)
    band = jnp.where(j <= HY_BANDS, j - 1, j - 1 - HY_BANDS).astype(F32)
    ang = (2.0 * math.pi / L) * t * band
    tn = t / L
    z = jnp.where(j == 0, tn, jnp.where(j <= HY_BANDS, jnp.cos(ang),
                                        jnp.where(j <= 2 * HY_BANDS, -jnp.sin(ang), 0.0)))
    a = jnp.sin(freq_ref[0:1, :] * (_dot_hi(z, w1_ref[...]) + b1_ref[...]))
    a = jnp.sin(freq_ref[1:2, :] * (_dot_hi(a, w2_ref[...]) + b2_ref[...]))
    filt = _dot_hi(a, w3_ref[...])
    tcol = lax.broadcasted_iota(jnp.int32, (L, 4 * HY_C), 0)
    filt = filt * jnp.exp(-(tcol.astype(F32) / L) * jnp.exp(ld_ref[...]))
    t1 = lax.broadcasted_iota(jnp.int32, (L, HY_C), 0)
    sign = jnp.where(t1 % 2 == 0, 1.0, -1.0)
    for n in range(2):
        fwd = filt[:, 2 * HY_C * n:2 * HY_C * n + HY_C]
        bwd = jnp.where(t1 == 0, 0.0, filt[:, 2 * HY_C * n + HY_C:2 * HY_C * (n + 1)])
        tot = fwd + bwd
        fs_ref[:, HY_C * n:HY_C * (n + 1)] = tot
        fs_ref[:, 2 * HY_C + HY_C * n:2 * HY_C + HY_C * (n + 1)] = fwd - bwd
        nyq_ref[:, HY_C * n:HY_C * (n + 1)] = (tot * sign).sum(0, keepdims=True)


def _hy_filter(L, w1p, b1, w2, b2, w3, freq, ld):
    full = lambda s: pl.BlockSpec(s, lambda: tuple(0 for _ in s))
    return pl.pallas_call(
        functools.partial(_hy_filter_kernel, L=L),
        in_specs=[full((128, 64)), full((1, 64)), full((64, 64)), full((1, 64)), full((64, 4 * HY_C)),
                  full((2, 64)), full((1, 4 * HY_C))],
        out_specs=[full((L, 4 * HY_C)), full((1, 2 * HY_C))],
        out_shape=[jax.ShapeDtypeStruct((L, 4 * HY_C), F32), jax.ShapeDtypeStruct((1, 2 * HY_C), F32)],
        compiler_params=pltpu.CompilerParams(vmem_limit_bytes=VMEM_LIMIT),
        name="hyena_filter",
    )(w1p, b1, w2, b2, w3, freq, ld)


def _hy_gdft_kernel(cm_ref, sm_ref, fs_ref, nyq_ref, gr_ref, gi_ref, *, tm):
    m = pl.program_id(0)
    f = fs_ref[...].astype(BF16)
    gr_ref[...] = _dot(cm_ref[...], f[:, :2 * HY_C])
    gi = _dot(sm_ref[...], f[:, 2 * HY_C:])
    row = m * tm + lax.broadcasted_iota(jnp.int32, (tm, 2 * HY_C), 0)
    gi_ref[...] = jnp.where(row == 0, nyq_ref[...], gi)


def _hy_gdft(cm, sm, fs, nyq, L, tm):
    return pl.pallas_call(
        functools.partial(_hy_gdft_kernel, tm=tm),
        grid=(L // tm,),
        in_specs=[pl.BlockSpec((tm, L), lambda m: (m, 0)), pl.BlockSpec((tm, L), lambda m: (m, 0)),
                  pl.BlockSpec((L, 4 * HY_C), lambda m: (0, 0)), pl.BlockSpec((1, 2 * HY_C), lambda m: (0, 0))],
        out_specs=[pl.BlockSpec((tm, 2 * HY_C), lambda m: (m, 0))] * 2,
        out_shape=[jax.ShapeDtypeStruct((L, 2 * HY_C), F32)] * 2,
        compiler_params=_cp("arbitrary"),
        name="hyena_filter_dft",
    )(cm, sm, fs, nyq)


def _hy_fwd_kernel(cm_ref, sm_ref, z_ref, gr_ref, gi_ref, yr_ref, yi_ref, *, L, tm, ns):
    m = pl.program_id(1)
    gr = gr_ref[...]
    gi = gi_ref[...]
    row0 = (m * tm + lax.broadcasted_iota(jnp.int32, (tm, HY_C), 0)) == 0
    s = jnp.where(row0, 0.5 / L, 1.0 / L)
    for g in range(ns):
        zb = z_ref[g * L:(g + 1) * L, :].astype(BF16)
        zr = _dot(cm_ref[...], zb)
        zi = _dot(sm_ref[...], zb)
        zigi = zi * gi
        yr_ref[g * tm:(g + 1) * tm, :] = ((zr * gr - jnp.where(row0, 0.0, zigi)) * s).astype(BF16)
        yi_ref[g * tm:(g + 1) * tm, :] = (jnp.where(row0, zigi, zr * gi + zi * gr) * s).astype(BF16)


def _hy_seqs_per_step(NB, Lb, tm):
    ns = max(1, 2048 // Lb) if tm == Lb else 1
    while NB % ns:
        ns //= 2
    return ns


def _hy_fwd(cm, sm, z, gr, gi, n, NB, Lb, tm):
    T = z.shape[0]
    nm = Lb // tm
    ns = _hy_seqs_per_step(NB, Lb, tm)
    return pl.pallas_call(
        functools.partial(_hy_fwd_kernel, L=Lb, tm=tm, ns=ns),
        grid=(NB // ns, nm),
        in_specs=[pl.BlockSpec((tm, Lb), lambda b, m: (m, 0)), pl.BlockSpec((tm, Lb), lambda b, m: (m, 0)),
                  pl.BlockSpec((ns * Lb, HY_C), lambda b, m: (b, 0)),
                  pl.BlockSpec((tm, HY_C), lambda b, m: (m, n)), pl.BlockSpec((tm, HY_C), lambda b, m: (m, n))],
        out_specs=[pl.BlockSpec((ns * tm, HY_C), lambda b, m: (b * nm + m, 0))] * 2,
        out_shape=[jax.ShapeDtypeStruct((T, HY_C), BF16)] * 2,
        compiler_params=_cp("arbitrary", "arbitrary"),
        name="hyena_fwd_dft",
    )(cm, sm, z, gr, gi)


def _hy_inv_kernel(cm_ref, smt_ref, yr_ref, yi_ref, z_ref, g_ref, skip_ref, o_ref, *, L, tm, ns):
    for g in range(ns):
        seq = slice(g * L, (g + 1) * L)
        out = slice(g * tm, (g + 1) * tm)
        conv = _dot(cm_ref[...], yr_ref[seq, :]) + _dot(smt_ref[...], yi_ref[seq, :])
        o_ref[out, :] = g_ref[out, :] * (conv + skip_ref[...] * z_ref[out, :])


def _hy_inv(cm, smt, yr, yi, z, gate, skip, n, NB, Lb, tm):
    T = z.shape[0]
    nm = Lb // tm
    ns = _hy_seqs_per_step(NB, Lb, tm)
    tile = pl.BlockSpec((ns * tm, HY_C), lambda b, m: (b * nm + m, 0))
    seq = pl.BlockSpec((ns * Lb, HY_C), lambda b, m: (b, 0))
    return pl.pallas_call(
        functools.partial(_hy_inv_kernel, L=Lb, tm=tm, ns=ns),
        grid=(NB // ns, nm),
        in_specs=[pl.BlockSpec((tm, Lb), lambda b, m: (m, 0)), pl.BlockSpec((tm, Lb), lambda b, m: (m, 0)),
                  seq, seq, tile, tile, pl.BlockSpec((None, 1, HY_C), lambda b, m: (n, 0, 0))],
        out_specs=tile,
        out_shape=jax.ShapeDtypeStruct((T, HY_C), F32),
        compiler_params=_cp("arbitrary", "arbitrary"),
        name="hyena_inv_dft",
    )(cm, smt, yr, yi, z, gate, skip)


def _dft_mats(L):
    k = jnp.arange(L, dtype=jnp.int32)
    blk = 64

    def trig(mult):
        ang = ((mult[:, None] * k[None, :]) % (2 * L)).astype(F32) * (math.pi / L)
        return jnp.cos(ang), jnp.sin(ang)

    ca, sa = trig(jnp.arange(L // blk, dtype=jnp.int32) * blk)
    cb, sb = trig(jnp.arange(blk, dtype=jnp.int32))
    cm = (ca[:, None, :] * cb[None] - sa[:, None, :] * sb[None]).reshape(L, L)
    s = -(sa[:, None, :] * cb[None] + ca[:, None, :] * sb[None]).reshape(L, L)
    alt = jnp.where(k % 2 == 0, 1.0, -1.0).astype(F32)
    sm = jnp.where(k[:, None] == 0, alt[None, :], s)
    smt = jnp.where(k[None, :] == 0, alt[:, None], s)
    return cm.astype(BF16), sm.astype(BF16), smt.astype(BF16)


def _merge_kernel(oa_ref, ob_ref, oc_ref, od_ref, g0_ref, g1_ref, g2_ref, g3_ref, wb_ref, wo_ref, x_ref, mod_ref,
                  lg_ref, lb_ref, rt_ref, *rest):
    x1_ref, hp_ref, logit_ref = rest[-3:]
    acc = None
    for o_ref, g_ref, i in ((oa_ref, g0_ref, 0), (ob_ref, g1_ref, 1), (oc_ref, g2_ref, 2), (od_ref, g3_ref, 3)):
        y = _sigmoid(g_ref[...].astype(F32)) * _dot(o_ref[...].astype(BF16), wb_ref[i])
        acc = y if acc is None else acc + y
    mix = _dot(acc.astype(BF16), wo_ref[...])
    m = mod_ref[...]
    x1 = _layer_norm(DN_ALPHA * x_ref[...] + m[2:3, :] * mix, lg_ref[...], lb_ref[...])
    x1_ref[...] = x1
    h2 = x1 * (1.0 + m[4:5, :]) + m[3:4, :]
    hp_ref[...] = _pack_pairs(h2)
    hi = h2.astype(BF16)
    lo = (h2 - hi.astype(F32)).astype(BF16)
    r = rt_ref[...]
    r_hi = r.astype(BF16)
    r_lo = (r - r_hi.astype(F32)).astype(BF16)
    both = _dot_nt(jnp.concatenate([r_hi, r_lo], axis=0), hi)
    logit_ref[...] = both[:N_EXPERTS, :] + both[N_EXPERTS:, :] + _dot_nt(r_hi, lo)


def _merge(outs, gates, wb, wo, x, mod, lg, lb, router, mod_row, bm, after=None):
    T = x.shape[0]
    row = lambda w: pl.BlockSpec((bm, w), lambda i: (i, 0))
    gspec = lambda n: pl.BlockSpec((bm, D), lambda i: (i, n))
    fixed2 = lambda s: pl.BlockSpec(s, lambda i: (0, 0))
    dep_specs, dep_args = _after(after)
    return pl.pallas_call(
        _merge_kernel,
        grid=(T // bm,),
        in_specs=[row(256)] * 4 + [gspec(0), gspec(1), gspec(2), gspec(3),
                                   pl.BlockSpec((4, 256, D), lambda i: (0, 0, 0)), fixed2((D, D)), row(D),
                                   pl.BlockSpec((None, 6, D), lambda i: (mod_row(i), 0, 0)),
                                   fixed2((1, D)), fixed2((1, D)), fixed2((N_EXPERTS, D))] + dep_specs,
        out_specs=[row(D), row(D // 2), pl.BlockSpec((N_EXPERTS, bm), lambda i: (0, i))],
        out_shape=[jax.ShapeDtypeStruct((T, D), F32), jax.ShapeDtypeStruct((T, D // 2), jnp.int32),
                   jax.ShapeDtypeStruct((N_EXPERTS, T), F32)],
        compiler_params=_cp("arbitrary"),
        name="merge_norm",
    )(*outs, gates, gates, gates, gates, wb, wo, x, mod, lg, lb, router, *dep_args)


def _router_kernel(logit_ref, bias_ref, g_ref, rank_ref, cnt_ref, *, tt):
    per = N_EXPERTS // N_GROUPS
    scores = _sigmoid(logit_ref[...])
    sel = (scores + bias_ref[...]).reshape(N_GROUPS, per, tt)
    gid = lax.broadcasted_iota(jnp.int32, (N_GROUPS, per, tt), 0).astype(F32)
    jid = lax.broadcasted_iota(jnp.int32, (N_GROUPS, per, tt), 1).astype(F32)
    eid = gid * per + jid
    ninf = -jnp.inf
    m1 = sel.max(1, keepdims=True)
    i1 = jnp.where(sel == m1, jid, float(per)).min(1, keepdims=True)
    m2 = jnp.where(jid == i1, ninf, sel).max(1, keepdims=True)
    gs = m1 + m2
    g1 = lax.broadcasted_iota(jnp.int32, (N_GROUPS, 1, tt), 0).astype(F32)
    chosen = jnp.zeros((N_GROUPS, 1, tt), F32)
    for _ in range(TOPK_GROUPS):
        mx = gs.max(0, keepdims=True)
        gi = jnp.where(gs == mx, g1, float(N_GROUPS)).min(0, keepdims=True)
        pick = g1 == gi
        chosen = jnp.where(pick, 1.0, chosen)
        gs = jnp.where(pick, ninf, gs)
    cand = jnp.where(chosen > 0.0, sel, NEG)
    picked = jnp.zeros((N_GROUPS, per, tt), F32)
    for _ in range(TOP_K):
        mx = cand.max(1, keepdims=True).max(0, keepdims=True)
        ei = jnp.where(cand == mx, eid, float(N_EXPERTS)).min(1, keepdims=True).min(0, keepdims=True)
        pick = eid == ei
        picked = jnp.where(pick, 1.0, picked)
        cand = jnp.where(pick, ninf, cand)
    w = scores.reshape(N_GROUPS, per, tt) * picked
    wsum = w.sum(1, keepdims=True).sum(0, keepdims=True)
    g_ref[...] = (w / wsum * ROUTED_SCALE).reshape(N_EXPERTS, tt)
    pk = picked.reshape(N_EXPERTS, tt)
    t_in = lax.broadcasted_iota(jnp.int32, (tt, tt), 0)
    t_out = lax.broadcasted_iota(jnp.int32, (tt, tt), 1)
    upper = jnp.where(t_in <= t_out, 1.0, 0.0).astype(BF16)

    @pl.when(pl.program_id(0) == 0)
    def _():
        cnt_ref[...] = jnp.zeros_like(cnt_ref)

    before = cnt_ref[:, 0:1]
    rank_ref[...] = jnp.where(pk > 0.0, before + _dot(pk.astype(BF16), upper) - 1.0, -1.0)
    cnt_ref[...] += pk.sum(-1, keepdims=True)


def _router(logits_t, bias, tt):
    T = logits_t.shape[1]
    tile = pl.BlockSpec((N_EXPERTS, tt), lambda i: (0, i))
    return pl.pallas_call(
        functools.partial(_router_kernel, tt=tt),
        grid=(T // tt,),
        in_specs=[tile, pl.BlockSpec((N_EXPERTS, 1), lambda i: (0, 0))],
        out_specs=[tile, tile, pl.BlockSpec((N_EXPERTS, 128), lambda i: (0, 0))],
        out_shape=[jax.ShapeDtypeStruct((N_EXPERTS, T), F32), jax.ShapeDtypeStruct((N_EXPERTS, T), F32),
                   jax.ShapeDtypeStruct((N_EXPERTS, 128), F32)],
        compiler_params=_cp("arbitrary"),
        name="moe_router",
    )(logits_t, bias)


def _route_pos_kernel(gate_ref, rank_ref, cnt_ref, pos_ref, w_ref, te_ref, nx_ref, nt_ref, *, tm, nt_max):
    ei = lax.broadcasted_iota(jnp.int32, (N_EXPERTS, N_EXPERTS), 0)
    ej = lax.broadcasted_iota(jnp.int32, (N_EXPERTS, N_EXPERTS), 1)
    below = jnp.where(ej < ei, 1.0, 0.0)
    padded = jnp.ceil(cnt_ref[...] * (1.0 / tm)) * tm
    offs = _dot_hi(below, padded)
    rank = rank_ref[...]
    routed = rank >= 0.0
    pos = offs[:, 0:1] + rank
    slot = _dot(below.astype(BF16), jnp.where(routed, 1.0, 0.0).astype(BF16))
    gate = gate_ref[...]
    for k in range(TOP_K):
        mine = routed & (slot == float(k))
        pos_ref[k:k + 1, :] = jnp.where(mine, pos, 0.0).sum(0, keepdims=True).astype(jnp.int32)
        w_ref[k:k + 1, :] = jnp.where(mine, gate, 0.0).sum(0, keepdims=True)
    ends = (offs + padded)[:, 0:1]
    first = (lax.broadcasted_iota(jnp.int32, (N_EXPERTS, nt_max), 1) * tm).astype(F32)
    te = jnp.minimum(jnp.where(ends <= first, 1.0, 0.0).sum(0, keepdims=True), N_EXPERTS - 1.0)
    te_ref[...] = te.astype(jnp.int32)
    eid = lax.broadcasted_iota(jnp.int32, (N_EXPERTS, nt_max), 0).astype(F32)
    nx_ref[...] = (jnp.where(eid == te, ends, 0.0).sum(0, keepdims=True) * (1.0 / tm)).astype(jnp.int32)
    nt_ref[...] = (padded.sum(0, keepdims=True) * (1.0 / tm)).astype(jnp.int32)


def _route_pos(gate_t, rank, cnt, tt, tm, nt_max):
    T = gate_t.shape[1]
    tile = pl.BlockSpec((N_EXPERTS, tt), lambda i: (0, i))
    out = pl.BlockSpec((TOP_K, tt), lambda i: (0, i))
    return pl.pallas_call(
        functools.partial(_route_pos_kernel, tm=tm, nt_max=nt_max),
        grid=(T // tt,),
        in_specs=[tile, tile, pl.BlockSpec((N_EXPERTS, 128), lambda i: (0, 0))],
        out_specs=[out, out, pl.BlockSpec((1, nt_max), lambda i: (0, 0)), pl.BlockSpec((1, nt_max), lambda i: (0, 0)),
                   pl.BlockSpec((1, 128), lambda i: (0, 0))],
        out_shape=[jax.ShapeDtypeStruct((TOP_K, T), jnp.int32), jax.ShapeDtypeStruct((TOP_K, T), F32),
                   jax.ShapeDtypeStruct((1, nt_max), jnp.int32), jax.ShapeDtypeStruct((1, nt_max), jnp.int32),
                   jax.ShapeDtypeStruct((1, 128), jnp.int32)],
        compiler_params=_cp("arbitrary"),
        name="moe_positions",
    )(gate_t, rank, cnt)


def _gmm_kernel(te_ref, nx_ref, nt_ref, xs_ref, w1_hbm, w3_hbm, w2_hbm, *rest, l):
    ys_ref, b1_ref, b3_ref, b2_ref, f1_ref, f3_ref, f2_ref, seg_ref, sem = rest[-9:]
    j = pl.program_id(0)
    live = j < nt_ref[0]
    new_expert = (j == 0) | (te_ref[j] != te_ref[jnp.maximum(j - 1, 0)])

    def fetch(e, slot):
        return [pltpu.make_async_copy(w_hbm.at[l, e], f_ref.at[slot], sem.at[i, slot])
                for i, (w_hbm, f_ref) in enumerate(((w1_hbm, f1_ref), (w3_hbm, f3_ref), (w2_hbm, f2_ref)))]

    @pl.when(live & new_expert)
    def _():
        @pl.when(j == 0)
        def _():
            seg_ref[0] = 0
            for c in fetch(te_ref[0], 0):
                c.start()

        slot = lax.rem(seg_ref[0], 2)
        for c in fetch(te_ref[j], slot):
            c.wait()
        b1_ref[...] = f1_ref[slot].astype(BF16)
        b3_ref[...] = f3_ref[slot].astype(BF16)
        b2_ref[...] = f2_ref[slot].astype(BF16)
        nxt = nx_ref[j]

        @pl.when(nxt < nt_ref[0])
        def _():
            for c in fetch(te_ref[nxt], 1 - slot):
                c.start()

        seg_ref[0] = seg_ref[0] + 1

    @pl.when(live)
    def _():
        xa, xb = _unpack_pairs(xs_ref[...])
        xa, xb = xa.astype(BF16), xb.astype(BF16)
        half = D // 2
        a = _dot(xa, b1_ref[:half, :]) + _dot(xb, b1_ref[half:, :])
        b = _dot(xa, b3_ref[:half, :]) + _dot(xb, b3_ref[half:, :])
        hid = (a * _sigmoid(a) * b).astype(BF16)
        ys_ref[...] = _pack_pairs(_dot(hid, b2_ref[...]))


def _gmm(te, nx, nt, xs, w1, w3, w2, l, tm, after=None):
    n_slots = xs.shape[0]
    ds = D_EXPERT
    rows = pl.BlockSpec((tm, D // 2), lambda j, te, nx, nt: (jnp.minimum(j, nt[0] - 1), 0))
    hbm = pl.BlockSpec(memory_space=pl.ANY)
    dep_specs, dep_args = _after(after)
    return pl.pallas_call(
        functools.partial(_gmm_kernel, l=l),
        grid_spec=pltpu.PrefetchScalarGridSpec(
            num_scalar_prefetch=3,
            grid=(n_slots // tm,),
            in_specs=[rows, hbm, hbm, hbm] + dep_specs,
            out_specs=rows,
            scratch_shapes=[pltpu.VMEM((D, ds), BF16), pltpu.VMEM((D, ds), BF16), pltpu.VMEM((ds, D), BF16),
                            pltpu.VMEM((2, D, ds), F32), pltpu.VMEM((2, D, ds), F32), pltpu.VMEM((2, ds, D), F32),
                            pltpu.SMEM((1,), jnp.int32), pltpu.SemaphoreType.DMA((3, 2))]),
        out_shape=jax.ShapeDtypeStruct((n_slots, D // 2), jnp.int32),
        compiler_params=_cp("arbitrary"),
        name="moe_grouped_ffn",
    )(te, nx, nt, xs, w1, w3, w2, *dep_args)


def _combine_kernel(yk_ref, w_ref, hp_ref, s1_ref, s3_ref, s2_ref, x_ref, mod_ref, lg_ref, lb_ref, o_ref):
    w = w_ref[...]
    acc_a = acc_b = None
    for k in range(TOP_K):
        ya, yb = _unpack_pairs(yk_ref[k])
        wk = w[:, k:k + 1]
        acc_a = wk * ya if acc_a is None else acc_a + wk * ya
        acc_b = wk * yb if acc_b is None else acc_b + wk * yb
    ha, hb = _unpack_pairs(hp_ref[...])
    ha, hb = ha.astype(BF16), hb.astype(BF16)
    half = D // 2
    a = _dot(ha, s1_ref[:half, :]) + _dot(hb, s1_ref[half:, :])
    b = _dot(ha, s3_ref[:half, :]) + _dot(hb, s3_ref[half:, :])
    y = jnp.concatenate([acc_a, acc_b], axis=1) + _dot((a * _sigmoid(a) * b).astype(BF16), s2_ref[...])
    m = mod_ref[...]
    o_ref[...] = _layer_norm(DN_ALPHA * x_ref[...] + m[5:6, :] * y, lg_ref[...], lb_ref[...])


def _combine(yk, w, hp, s1, s3, s2, x1, mod, lg, lb, mod_row, bm):
    T = x1.shape[0]
    ds = D_EXPERT
    row = lambda n: pl.BlockSpec((bm, n), lambda i: (i, 0))
    fixed = lambda s: pl.BlockSpec(s, lambda i: (0, 0))
    return pl.pallas_call(
        _combine_kernel,
        grid=(T // bm,),
        in_specs=[pl.BlockSpec((TOP_K, bm, D // 2), lambda i: (0, i, 0)), row(TOP_K), row(D // 2),
                  fixed((D, ds)), fixed((D, ds)), fixed((ds, D)), row(D),
                  pl.BlockSpec((None, 6, D), lambda i: (mod_row(i), 0, 0)), fixed((1, D)), fixed((1, D))],
        out_specs=row(D),
        out_shape=jax.ShapeDtypeStruct((T, D), F32),
        compiler_params=_cp("arbitrary"),
        name="moe_combine_norm",
    )(yk, w, hp, s1, s3, s2, x1, mod, lg, lb)


def _sc_worker():
    return lax.axis_index("s") * SC_CORES + lax.axis_index("c")


def _sc_mesh():
    return plsc.VectorSubcoreMesh(core_axis_name="c", subcore_axis_name="s")


def _sc_copy_rows(table_hbm, idx_v, out_hbm, base, n_chunks, bufs, gsem, wsem):
    def gather(c, b):
        rows = idx_v.at[pl.ds(pl.multiple_of(c * SC_ROWS, SC_ROWS), SC_ROWS)]
        return pltpu.make_async_copy(table_hbm.at[rows], bufs.at[b], gsem.at[b])

    def write(c, b):
        rows = pl.ds(base + pl.multiple_of(c * SC_ROWS, SC_ROWS), SC_ROWS)
        return pltpu.make_async_copy(bufs.at[b], out_hbm.at[rows], wsem.at[b])

    gather(0, 0).start()

    @pl.loop(0, n_chunks // 2)
    def _(p):
        for b in range(2):
            c = 2 * p + b
            gather(c, b).wait()
            write(c, b).start()

            @pl.when(c >= 1)
            def _():
                write(c - 1, 1 - b).wait()

            @pl.when(c + 1 < n_chunks)
            def _():
                gather(c + 1, 1 - b).start()

    write(n_chunks - 1, 1).wait()


def _sc_gather(table, idx):
    N, W = idx.shape[0], table.shape[1]
    per_w = N // SC_WORKERS
    n_chunks = per_w // SC_ROWS

    def body(table_hbm, idx_hbm, out_hbm, idx_v, bufs, gsem, wsem):
        base = _sc_worker() * per_w
        pltpu.sync_copy(idx_hbm.at[pl.ds(base, per_w)], idx_v)
        _sc_copy_rows(table_hbm, idx_v, out_hbm, base, n_chunks, bufs, gsem, wsem)

    return pl.kernel(
        body, out_type=jax.ShapeDtypeStruct((N, W), table.dtype), mesh=_sc_mesh(),
        scratch_types=[pltpu.VMEM((per_w,), jnp.int32), pltpu.VMEM((2, SC_ROWS, W), table.dtype),
                       pltpu.SemaphoreType.DMA((2,)), pltpu.SemaphoreType.DMA((2,))],
        name="sc_gather",
    )(table, idx)


def _sc_dispatch(pos, table, n_slots):
    NP, (T, W) = pos.shape[0], table.shape
    per_w = n_slots // SC_WORKERS
    n_chunks = per_w // SC_ROWS
    scan = 8192

    def body(pos_hbm, table_hbm, out_hbm, pos_v, src_v, bufs, gsem, wsem):
        base = _sc_worker() * per_w
        lane = lax.iota(jnp.int32, SC_LANES)

        @pl.loop(0, per_w // SC_LANES)
        def _(j):
            o = pl.multiple_of(j * SC_LANES, SC_LANES)
            src_v[pl.ds(o, SC_LANES)] = (base + o + lane) & (T - 1)

        @pl.loop(0, NP // scan)
        def _(c):
            pltpu.sync_copy(pos_hbm.at[pl.ds(pl.multiple_of(c * scan, scan), scan)], pos_v)

            @pl.loop(0, scan // SC_LANES)
            def _(v):
                o = pl.multiple_of(v * SC_LANES, SC_LANES)
                p = pos_v[pl.ds(o, SC_LANES)] - base
                mine = (p >= 0) & (p < per_w)
                tok = (c * scan + o + lane) & (T - 1)
                plsc.store_scatter(src_v, [jnp.where(mine, p, 0)], tok, mask=mine)

        _sc_copy_rows(table_hbm, src_v, out_hbm, base, n_chunks, bufs, gsem, wsem)

    return pl.kernel(
        body, out_type=jax.ShapeDtypeStruct((n_slots, W), table.dtype), mesh=_sc_mesh(),
        scratch_types=[pltpu.VMEM((scan,), jnp.int32), pltpu.VMEM((per_w,), jnp.int32),
                       pltpu.VMEM((2, SC_ROWS, W), table.dtype), pltpu.SemaphoreType.DMA((2,)),
                       pltpu.SemaphoreType.DMA((2,))],
        compiler_params=pltpu.CompilerParams(needs_layout_passes=False),
        name="sc_dispatch",
    )(pos, table)


def _caches_kernel(*refs, nb, S):
    n_in = 6 * DEPTH
    outs = refs[n_in:]
    l = pl.program_id(0)
    for a in range(DEPTH):
        @pl.when(l == a)
        def _(a=a):
            ckv, kpe, wk, wv, nk, nv = refs[6 * a:6 * (a + 1)]
            for g in range(nb):
                rows = slice(g * S, (g + 1) * S)
                outs[0][g] = ckv[rows, :]
                outs[1][g] = kpe[rows, 64:96]
                outs[2][g] = wk[rows, :]
                outs[3][g] = wv[rows, :]
                outs[4][g] = nk[rows, :]
                outs[5][g] = nv[rows, :]


def _emit_caches(projs, ckvs, B, S):
    nb = 4
    while B % nb:
        nb //= 2

    def layer_specs(a):
        row = lambda l, b: jnp.where(l == a, b, 0)
        col = lambda w, off: pl.BlockSpec((nb * S, w), lambda l, b: (row(l, b), off // w))
        return [pl.BlockSpec((nb * S, 128), lambda l, b: (row(l, b), 0)), col(128, P_KPE), col(128, P_WK),
                col(128, P_WV), col(256, P_NK), col(256, P_NV)]

    in_specs, args = [], []
    for a in range(DEPTH):
        in_specs += layer_specs(a)
        args += [ckvs[a]] + [projs[a]] * 5
    widths = (128, 32, 128, 128, 256, 256)
    return pl.pallas_call(
        functools.partial(_caches_kernel, nb=nb, S=S),
        grid=(DEPTH, B // nb),
        in_specs=in_specs,
        out_specs=[pl.BlockSpec((nb, None, S, w), lambda l, b: (b, l, 0, 0)) for w in widths],
        out_shape=[jax.ShapeDtypeStruct((B, DEPTH, S, w), F32) for w in widths],
        compiler_params=_cp("arbitrary", "arbitrary"),
        name="context_tensors",
    )(*args)


def _rot_cols(w, q):
    a, b, c, d = w[..., :q], w[..., q:2 * q], w[..., 2 * q:3 * q], w[..., 3 * q:]
    return jnp.concatenate([-b, a, -d, c], -1)


def _prep_w_in(w):
    z = lambda n: jnp.zeros((D, n), w.dtype)
    qlat, ckv, kpe, hy = w[:, 0:256], w[:, 256:384], w[:, 384:416], w[:, 416:1184]
    wq, wk, wv = w[:, 1184:1440], w[:, 1440:1568], w[:, 1568:1696]
    nq, nk, nv, gate = w[:, 1696:1952], w[:, 1952:2208], w[:, 2208:2464], w[:, 2464:]
    wq_r = _rot_cols(wq.reshape(D, 4, 64), 16).reshape(D, 256)
    wk_r = _rot_cols(wk.reshape(D, 2, 64), 16).reshape(D, 128)
    kpe_r = _rot_cols(kpe, 8)
    cols = [qlat, ckv, z(64), kpe, z(32), hy, wq, wk, wv, nq, nk, nv, wq_r, wk_r, z(64), kpe_r, z(32), gate]
    return jnp.concatenate(cols, 1).astype(BF16)


def _prep_mla(w_uq, w_ukv):
    uq = w_uq.reshape(256, 4, 96)
    nope, pe = uq[..., :64], uq[..., 64:]
    z32 = jnp.zeros((256, 4, 32), w_uq.dtype)
    z64 = jnp.zeros((256, 4, 64), w_uq.dtype)
    wcat = jnp.concatenate([nope, pe, z32], -1).reshape(256, 512).astype(BF16)
    wrot = jnp.concatenate([z64, _rot_cols(pe, 8), z32], -1).reshape(256, 512).astype(BF16)
    ukv = w_ukv.reshape(128, 4, 128)
    wk = jnp.concatenate([ukv[..., :64], jnp.zeros((128, 4, 64), w_ukv.dtype)], -1).reshape(128, 512).astype(BF16)
    wv = ukv[..., 64:].reshape(128, 256).astype(BF16)
    return wcat, wrot, wk, wv


def _rope_tab(L, q):
    t = jnp.arange(L)
    inv = ROPE_BASE ** (-jnp.arange(q, dtype=F32) / q)
    ar = (t // GRID_W).astype(F32)[:, None] * inv[None, :]
    ac = (t % GRID_W).astype(F32)[:, None] * inv[None, :]
    cos = jnp.concatenate([jnp.cos(ar), jnp.cos(ar), jnp.cos(ac), jnp.cos(ac)], 1)
    sin = jnp.concatenate([jnp.sin(ar), jnp.sin(ar), jnp.sin(ac), jnp.sin(ac)], 1)
    return cos, sin


def _rope_tables(L):
    c8, s8 = _rope_tab(L, 8)
    c16, s16 = _rope_tab(L, 16)
    one, zero = jnp.ones((L, 64), F32), jnp.zeros((L, 64), F32)
    z32 = jnp.zeros((L, 32), F32)
    mla_q = (jnp.tile(jnp.concatenate([one, c8, z32], 1), (1, 4)), jnp.tile(jnp.concatenate([zero, s8, z32], 1), (1, 4)))
    mla_k = (jnp.concatenate([zero, c8, z32], 1), jnp.concatenate([zero, s8, z32], 1))
    win = (jnp.tile(c16, (1, 4)), jnp.tile(s16, (1, 4)), jnp.tile(c16, (1, 2)), jnp.tile(s16, (1, 2)))
    return mla_q + mla_k, win


def _hyena(proj, lp, dft, NB, Lb):
    cm, sm, smt = dft
    tm = min(Lb, 512)
    v, x1, x2 = _short_conv(proj, lp["hy_conv_w"], lp["hy_conv_b"].reshape(1, -1), NB, Lb)
    w1p = jnp.pad(lp["hy_w1"], ((0, 128 - lp["hy_w1"].shape[0]), (0, 0)))
    fs, nyq = _hy_filter(Lb, w1p, lp["hy_b1"].reshape(1, -1), lp["hy_w2"], lp["hy_b2"].reshape(1, -1), lp["hy_w3"],
                         lp["hy_sin_freq"], lp["hy_log_decay"].reshape(1, -1))
    gr, gi = _hy_gdft(cm, sm, fs, nyq, Lb, tm)
    skip = lp["hy_skip"].reshape(2, 1, HY_C)
    z = v
    for n, gate in enumerate((x1, x2)):
        yr, yi = _hy_fwd(cm, sm, z, gr, gi, n, NB, Lb, tm)
        z = _hy_inv(cm, smt, yr, yi, z, gate, skip, n, NB, Lb, tm)
    return z


def _layer_steps(x, mod, lp, l, NB, Lb, mod_row_of_batch, dft, cache=None, tabs=None, na_bias=None):
    T = NB * Lb
    latent = cache is not None
    bm = 256
    rows_of = lambda n: (lambda i: mod_row_of_batch((i * n) // Lb))
    mod_row = rows_of(bm)
    span = Lb if latent else T
    bmp = min(span, 1024)
    proj, gates = _in_proj(x, mod, lp["w_in_p"], rows_of(bmp), bmp)

    gq, gkv = lp["mla_q_norm"].reshape(1, -1), lp["mla_kv_norm"].reshape(1, -1)
    wcat, wrot, wk, wv = lp["mla_w"]
    q_all, ckv_n, kpe_r = _mla_q(proj, gq, gkv, wcat, wrot, tabs[0] if latent else None, Lb, min(span, 512))
    if latent:
        ckv_c, kpe_c, kc_c, vc_c, kd_c, vd_c = cache
        Lc = ckv_c.shape[1]
        kpe_cp = jnp.pad(kpe_c, ((0, 0), (0, 0), (64, 32)))
        ckv_all = jnp.concatenate([ckv_c, ckv_n.reshape(NB, Lb, 128)], 1).reshape(NB * (Lc + Lb), 128)
        kpe_all = jnp.concatenate([kpe_cp, kpe_r.reshape(NB, Lb, 128)], 1).reshape(NB * (Lc + Lb), 128)
        k_all, v_all = _mla_kv(ckv_all, kpe_all, wk, wv, 512)
        oc = _lat_win_attention(proj, kc_c.reshape(NB, Lc, 128), vc_c.reshape(NB, Lc, 128), tabs[1],
                                lp["win_sink"], NB, Lb)
        od = _lat_na_attention(proj, kd_c.reshape(NB, Lc, 256), vd_c.reshape(NB, Lc, 256), na_bias, NB, Lb)
        ob = _hyena(proj, lp, dft, NB, Lb)
        after = yield "projected", od
        oa = _lat_mla_attention(q_all, k_all, v_all, NB, Lb, Lc + Lb, 256, after=after)
        after = None
    else:
        k_all, v_all = _mla_kv(ckv_n, kpe_r, wk, wv, 512)
        oa, oc, od = _ctx_attention(proj, q_all, k_all, v_all, lp["win_sink"], NB, Lb)
        ob = _hyena(proj, lp, dft, NB, Lb)
        after = yield "projected", oa

    bmm = min(span, 512)
    x1, hp, logits_t = _merge((oa, ob, oc, od), gates, lp["w_branch_b"], lp["w_out_b"], x, mod,
                              lp["ln1_g"].reshape(1, -1), lp["ln1_b"].reshape(1, -1), lp["moe_router"].T,
                              rows_of(bmm), bmm, after=after)
    n_slots = T * TOP_K + N_EXPERTS * MOE_TM
    gate_t, rank, cnt = _router(logits_t, lp["moe_bias"].reshape(-1, 1), 512)
    pos, w8, te, nx, nt = _route_pos(gate_t, rank, cnt, 512, MOE_TM, n_slots // MOE_TM)
    xs = _sc_dispatch(pos.reshape(-1), hp, n_slots)
    after = yield "dispatched", None
    ys = _gmm(te.reshape(-1), nx.reshape(-1), nt.reshape(-1)[:1], xs, lp["moe_w1"], lp["moe_w3"], lp["moe_w2"], l,
              MOE_TM, after=after)
    yield "ffn", ys
    yk = _sc_gather(ys, pos.reshape(-1)).reshape(TOP_K, T, D // 2)
    x2 = _combine(yk, w8.T, hp, lp["sh_w1_b"], lp["sh_w3_b"], lp["sh_w2_b"], x1, mod,
                  lp["ln2_g"].reshape(1, -1), lp["ln2_b"].reshape(1, -1), mod_row, bm)
    yield "done", (x2, proj, ckv_n)


def kernel(x_prompt, x_sample, cache_mla_ckv, cache_mla_kpe, cache_win_k, cache_win_v, cache_na_k, cache_na_v, c, c_ctx, w_ada, b_ada, w_in, mla_q_norm, mla_kv_norm, mla_w_uq, mla_w_ukv, hy_conv_w, hy_conv_b, hy_w1, hy_b1, hy_w2, hy_b2, hy_w3, hy_sin_freq, hy_log_decay, hy_skip, win_sink, na_rpb, w_branch, w_out, ln1_g, ln1_b, ln2_g, ln2_b, moe_router, moe_bias, moe_w1, moe_w3, moe_w2, sh_w1, sh_w3, sh_w2):
    B, S, _ = x_prompt.shape
    DB, DS, _ = x_sample.shape
    xp = x_prompt.reshape(B * S, D)
    xs = x_sample.reshape(DB * DS, D)
    cvec = jnp.concatenate([c_ctx[None, :], c, jnp.zeros((8 - 1 - DB, D), F32)], 0)
    dft_ctx = _dft_mats(S)
    dft_lat = _dft_mats(DS)
    tabs = _rope_tables(DS)
    projs, ckvs = [], []

    def params(l):
        return dict(w_in_p=_prep_w_in(w_in[l]), mla_q_norm=mla_q_norm[l], mla_kv_norm=mla_kv_norm[l],
                    mla_w=_prep_mla(mla_w_uq[l], mla_w_ukv[l]), hy_conv_w=hy_conv_w[l], hy_conv_b=hy_conv_b[l],
                    hy_w1=hy_w1[l], hy_b1=hy_b1[l], hy_w2=hy_w2[l], hy_b2=hy_b2[l], hy_w3=hy_w3[l],
                    hy_sin_freq=hy_sin_freq[l], hy_log_decay=hy_log_decay[l], hy_skip=hy_skip[l],
                    win_sink=win_sink[l], w_branch_b=w_branch[l].astype(BF16), w_out_b=w_out[l].astype(BF16),
                    ln1_g=ln1_g[l], ln1_b=ln1_b[l], ln2_g=ln2_g[l], ln2_b=ln2_b[l],
                    moe_router=moe_router[l], moe_bias=moe_bias[l], moe_w1=moe_w1, moe_w3=moe_w3, moe_w2=moe_w2,
                    sh_w1_b=sh_w1[l].astype(BF16), sh_w3_b=sh_w3[l].astype(BF16), sh_w2_b=sh_w2[l].astype(BF16))

    lps = [params(l) for l in range(DEPTH)]
    mods = [_modulation(cvec, w_ada, b_ada, l) for l in range(DEPTH)]

    def ctx_layer(l, x):
        return _layer_steps(x, mods[l], lps[l], l, B, S, lambda b: 0, dft_ctx)

    def lat_layer(l, x):
        cache = (cache_mla_ckv[:, l], cache_mla_kpe[:, l], cache_win_k[:, l], cache_win_v[:, l],
                 cache_na_k[:, l], cache_na_v[:, l])
        return _layer_steps(x, mods[l], lps[l], l, DB, DS, lambda b: 1 + b, dft_lat, cache=cache, tabs=tabs,
                            na_bias=_na_bias(na_rpb[l]))

    ctx = ctx_layer(0, xp)
    next(ctx)
    ctx.send(None)
    for l in range(DEPTH):
        lat = lat_layer(l, xs)
        lat_local = next(lat)[1]
        ys_ctx = ctx.send(lat_local)[1]
        lat.send(ys_ctx)
        xp, proj, ckv_n = ctx.send(None)[1]
        projs.append(proj)
        ckvs.append(ckv_n)
        if l + 1 < DEPTH:
            ctx = ctx_layer(l + 1, xp)
            attended = next(ctx)[1]
            ys_lat = lat.send(attended)[1]
            ctx.send(ys_lat)
        else:
            ckv, kpe, wk, wv, nk, nv = _emit_caches(projs, ckvs, B, S)
            lat.send((xp, nv))
        xs = lat.send(None)[1][0]
    heads = lambda t, h: t.reshape(B, DEPTH, S, h, HEAD_DIM)
    return (xp.reshape(B, S, D), xs.reshape(DB, DS, D), ckv, kpe, heads(wk, 2), heads(wv, 2), heads(nk, 4),
            heads(nv, 4))
```

```python
import functools
import math

import jax
import jax.numpy as jnp
from jax import lax
from jax.experimental import pallas as pl
from jax.experimental.pallas import tpu as pltpu
from jax.experimental.pallas import tpu_sc as plsc

F32 = jnp.float32
BF16 = jnp.bfloat16

D = 1024
DEPTH = 2
GRID_W = 64
HEAD_DIM = 64
LOG2E = math.log2(math.e)
MLA_SCALE = 96 ** -0.5 * LOG2E
ATT_SCALE = HEAD_DIM ** -0.5 * LOG2E
HY_C = 256
HY_BANDS = 8
NA_KH = 8
NA_KW = 16
N_EXPERTS = 64
N_GROUPS = 8
TOP_K = 8
TOPK_GROUPS = 4
D_EXPERT = 256
ROUTED_SCALE = 2.5
ROPE_BASE = 10000.0
LN_EPS = 1e-5
RMS_EPS = 1e-6
NEG = -1e30
DN_ALPHA = (2 * DEPTH) ** 0.25

P_QLAT, P_CKV, P_KPE, P_HY = 0, 256, 384, 512
P_WQ, P_WK, P_WV = 1280, 1536, 1664
P_NQ, P_NK, P_NV = 1792, 2048, 2304
P_WQR, P_WKR, P_KPER, P_GATE = 2560, 2816, 2944, 3072
N_PROJ = 7168

VMEM_LIMIT = 56 * 1024 * 1024

SC_CORES = 2
SC_SUBCORES = 16
SC_LANES = 16
SC_WORKERS = SC_CORES * SC_SUBCORES
SC_ROWS = 64

MOE_TM = 512

def _cp(*sem):
    return pltpu.CompilerParams(dimension_semantics=sem, vmem_limit_bytes=VMEM_LIMIT)


def _sigmoid(x):
    return 1.0 / (1.0 + jnp.exp(-x))


def _dot(a, b):
    return jnp.dot(a, b, preferred_element_type=F32)


def _dot_nt(a, b):
    return lax.dot_general(a, b, (((1,), (1,)), ((), ())), preferred_element_type=F32)


def _dot_hi(a, b):
    return jnp.dot(a, b, preferred_element_type=F32, precision=lax.Precision.HIGHEST)


def _pack_pairs(x):
    w = x.shape[1] // 2
    hi = lax.bitcast_convert_type(x[:, :w].astype(BF16).astype(F32), jnp.int32)
    lo = lax.bitcast_convert_type(x[:, w:].astype(BF16).astype(F32), jnp.int32)
    return hi | lax.shift_right_logical(lo, 16)


def _unpack_pairs(p):
    hi = lax.bitcast_convert_type(p & jnp.int32(-65536), F32)
    lo = lax.bitcast_convert_type(lax.shift_left(p, 16), F32)
    return hi, lo


def _layer_norm(x, g, b):
    mu = jnp.mean(x, -1, keepdims=True)
    xc = x - mu
    var = jnp.mean(xc * xc, -1, keepdims=True)
    return xc * lax.rsqrt(var + LN_EPS) * g + b


def _rms_norm(x, g):
    return x * lax.rsqrt(jnp.mean(x * x, -1, keepdims=True) + RMS_EPS) * g


def _mod_kernel(c_ref, w_ref, b_ref, o_ref):
    c = c_ref[...]
    a = (c * _sigmoid(c)).astype(BF16)
    o_ref[...] = _dot(a, w_ref[...].astype(BF16)) + b_ref[...]


def _modulation(cvec, w_ada, b_ada, l):
    out = pl.pallas_call(
        _mod_kernel,
        grid=(6,),
        in_specs=[pl.BlockSpec((8, D), lambda j: (0, 0)),
                  pl.BlockSpec((None, D, D), lambda j: (l, 0, j)),
                  pl.BlockSpec((None, 1, D), lambda j: (l, 0, j))],
        out_specs=pl.BlockSpec((8, D), lambda j: (0, j)),
        out_shape=jax.ShapeDtypeStruct((8, 6 * D), F32),
        compiler_params=_cp("arbitrary"),
        name="modulation",
    )(cvec, w_ada, b_ada.reshape(DEPTH, 1, 6 * D))
    return out.reshape(8, 6, D)


def _inproj_kernel(x_ref, mod_ref, w_ref, o_ref, g_ref, h_ref, *, n_main):
    j = pl.program_id(1)

    @pl.when(j == 0)
    def _():
        m = mod_ref[...]
        h_ref[...] = (x_ref[...] * (1.0 + m[1:2, :]) + m[0:1, :]).astype(BF16)

    y = _dot(h_ref[...], w_ref[...])

    @pl.when(j < n_main)
    def _():
        o_ref[...] = y

    @pl.when(j >= n_main)
    def _():
        g_ref[...] = y.astype(BF16)


def _in_proj(x, mod, w_p, mod_row, bm, bn=1024):
    T = x.shape[0]
    n_main = P_GATE // bn
    return pl.pallas_call(
        functools.partial(_inproj_kernel, n_main=n_main),
        grid=(T // bm, N_PROJ // bn),
        in_specs=[pl.BlockSpec((bm, D), lambda i, j: (i, 0)),
                  pl.BlockSpec((None, 6, D), lambda i, j: (mod_row(i), 0, 0)),
                  pl.BlockSpec((D, bn), lambda i, j: (0, j))],
        out_specs=[pl.BlockSpec((bm, bn), lambda i, j: (i, jnp.minimum(j, n_main - 1))),
                   pl.BlockSpec((bm, bn), lambda i, j: (i, jnp.maximum(j - n_main, 0)))],
        out_shape=[jax.ShapeDtypeStruct((T, P_GATE), F32), jax.ShapeDtypeStruct((T, N_PROJ - P_GATE), BF16)],
        scratch_shapes=[pltpu.VMEM((bm, D), BF16)],
        compiler_params=_cp("arbitrary", "arbitrary"),
        name="in_proj",
    )(x, mod, w_p)


def _mla_q_kernel(*refs, rope):
    if rope:
        (ql_ref, ckv_ref, kpe_ref, kper_ref, gq_ref, gkv_ref, wc_ref, wr_ref,
         cq_ref, sq_ref, ck_ref, sk_ref, q_ref, ckvn_ref, kpeo_ref) = refs
    else:
        ql_ref, ckv_ref, kpe_ref, gq_ref, gkv_ref, wc_ref, q_ref, ckvn_ref, kpeo_ref = refs
    qn = _rms_norm(ql_ref[...], gq_ref[...]).astype(BF16)
    q = _dot(qn, wc_ref[...])
    if rope:
        q = q * cq_ref[...] + _dot(qn, wr_ref[...]) * sq_ref[...]
        kpeo_ref[...] = kpe_ref[...] * ck_ref[...] + kper_ref[...] * sk_ref[...]
    else:
        kpeo_ref[...] = kpe_ref[...]
    q_ref[...] = (q * MLA_SCALE).astype(BF16)
    ckvn_ref[...] = _rms_norm(ckv_ref[...], gkv_ref[...])


def _mla_q(proj, gq, gkv, wcat, wrot, tabs, Lb, bm):
    T = proj.shape[0]
    rope = tabs is not None
    nl = Lb // bm
    col = lambda c: (lambda i: (i, c))
    fixed = lambda i: (0, 0)
    in_specs = [pl.BlockSpec((bm, 256), col(P_QLAT // 256)),
                pl.BlockSpec((bm, 128), col(P_CKV // 128)),
                pl.BlockSpec((bm, 128), col(P_KPE // 128))]
    args = [proj, proj, proj]
    if rope:
        in_specs.append(pl.BlockSpec((bm, 128), col(P_KPER // 128)))
        args.append(proj)
    in_specs += [pl.BlockSpec((1, 256), fixed), pl.BlockSpec((1, 128), fixed), pl.BlockSpec((256, 512), fixed)]
    args += [gq, gkv, wcat]
    if rope:
        cq, sq, ck, sk = tabs
        pos = lambda i: (i % nl, 0)
        in_specs += [pl.BlockSpec((256, 512), fixed), pl.BlockSpec((bm, 512), pos), pl.BlockSpec((bm, 512), pos),
                     pl.BlockSpec((bm, 128), pos), pl.BlockSpec((bm, 128), pos)]
        args += [wrot, cq, sq, ck, sk]
    return pl.pallas_call(
        functools.partial(_mla_q_kernel, rope=rope),
        grid=(T // bm,),
        in_specs=in_specs,
        out_specs=[pl.BlockSpec((bm, 512), lambda i: (i, 0)),
                   pl.BlockSpec((bm, 128), lambda i: (i, 0)),
                   pl.BlockSpec((bm, 128), lambda i: (i, 0))],
        out_shape=[jax.ShapeDtypeStruct((T, 512), BF16),
                   jax.ShapeDtypeStruct((T, 128), F32),
                   jax.ShapeDtypeStruct((T, 128), F32)],
        compiler_params=_cp("arbitrary"),
        name="mla_q",
    )(*args)


def _mla_kv_kernel(ckv_ref, kpe_ref, wk_ref, wv_ref, k_ref, v_ref):
    c = ckv_ref[...].astype(BF16)
    kpe = kpe_ref[...]
    k_ref[...] = (_dot(c, wk_ref[...]) + jnp.concatenate([kpe] * 4, axis=1)).astype(BF16)
    v_ref[...] = _dot(c, wv_ref[...]).astype(BF16)


def _mla_kv(ckv, kpe, wk, wv, bm):
    Tk = ckv.shape[0]
    return pl.pallas_call(
        _mla_kv_kernel,
        grid=(Tk // bm,),
        in_specs=[pl.BlockSpec((bm, 128), lambda i: (i, 0)),
                  pl.BlockSpec((bm, 128), lambda i: (i, 0)),
                  pl.BlockSpec((128, 512), lambda i: (0, 0)),
                  pl.BlockSpec((128, 256), lambda i: (0, 0))],
        out_specs=[pl.BlockSpec((bm, 512), lambda i: (i, 0)),
                   pl.BlockSpec((bm, 256), lambda i: (i, 0))],
        out_shape=[jax.ShapeDtypeStruct((Tk, 512), BF16),
                   jax.ShapeDtypeStruct((Tk, 256), BF16)],
        compiler_params=_cp("arbitrary"),
        name="mla_kv",
    )(ckv, kpe, wk, wv)


def _attn_core(q, kvs, masks, sink):
    ss = []
    for (k, _), mk in zip(kvs, masks):
        s = _dot_nt(q, k)
        if mk is not None:
            s = s + mk[1] if mk[0] == "add" else jnp.where(mk[1], s, NEG)
        ss.append(s)
    m = ss[0].max(-1, keepdims=True)
    for s in ss[1:]:
        m = jnp.maximum(m, s.max(-1, keepdims=True))
    if sink is not None:
        m = jnp.maximum(m, sink)
    den = None
    acc = None
    for s, (_, v) in zip(ss, kvs):
        p = jnp.exp2(s - m)
        d = p.sum(-1, keepdims=True)
        a = _dot(p.astype(BF16), v)
        den = d if den is None else den + d
        acc = a if acc is None else acc + a
    if sink is not None:
        den = den + jnp.exp2(sink - m)
    return acc / den


def _ctx_attn_kernel(qm_ref, km_ref, vm_ref, wq_ref, wk_ref, wv_ref, nq_ref, nk_ref, nv_ref, sink_ref, *rest):
    om_ref, ow_ref, on_ref = rest[-3:]
    for h in range(4):
        q = qm_ref[:, 128 * h:128 * (h + 1)]
        k = km_ref[:, 128 * h:128 * (h + 1)]
        v = vm_ref[:, 64 * h:64 * (h + 1)]
        om_ref[:, 64 * h:64 * (h + 1)] = _attn_core(q, [(k, v)], [None], None)
    for h in range(4):
        g = h // 2
        q = (wq_ref[:, 64 * h:64 * (h + 1)] * ATT_SCALE).astype(BF16)
        k = wk_ref[:, 64 * g:64 * (g + 1)].astype(BF16)
        v = wv_ref[:, 64 * g:64 * (g + 1)].astype(BF16)
        ow_ref[:, 64 * h:64 * (h + 1)] = _attn_core(q, [(k, v)], [None], sink_ref[h] * LOG2E)
    for h in range(4):
        q = (nq_ref[:, 64 * h:64 * (h + 1)] * ATT_SCALE).astype(BF16)
        k = nk_ref[:, 64 * h:64 * (h + 1)].astype(BF16)
        v = nv_ref[:, 64 * h:64 * (h + 1)].astype(BF16)
        on_ref[:, 64 * h:64 * (h + 1)] = _attn_core(q, [(k, v)], [None], None)


def _ctx_attention(proj, q_all, k_all, v_all, sink, NB, Lb, after=None):
    T = proj.shape[0]
    pc = lambda w, off: pl.BlockSpec((Lb, w), lambda b: (b, off // w))
    row = lambda w: pl.BlockSpec((Lb, w), lambda b: (b, 0))
    dep_specs, dep_args = _after(after)
    return pl.pallas_call(
        _ctx_attn_kernel,
        grid=(NB,),
        in_specs=[row(512), row(512), row(256),
                  pc(256, P_WQ), pc(128, P_WK), pc(128, P_WV),
                  pc(256, P_NQ), pc(256, P_NK), pc(256, P_NV),
                  pl.BlockSpec(memory_space=pltpu.SMEM)] + dep_specs,
        out_specs=[row(256), row(256), row(256)],
        out_shape=[jax.ShapeDtypeStruct((T, 256), F32)] * 3,
        compiler_params=_cp("arbitrary"),
        name="ctx_attention",
    )(q_all, k_all, v_all, proj, proj, proj, proj, proj, proj, sink, *dep_args)


def _lat_mla_kernel(q_ref, k_ref, v_ref, *rest):
    o_ref = rest[-1]
    for h in range(4):
        q = q_ref[:, 128 * h:128 * (h + 1)]
        k = k_ref[:, 128 * h:128 * (h + 1)]
        v = v_ref[:, 64 * h:64 * (h + 1)]
        o_ref[:, 64 * h:64 * (h + 1)] = _attn_core(q, [(k, v)], [None], None)


def _after(after):
    deps = [] if after is None else list(after) if isinstance(after, (tuple, list)) else [after]
    return [pl.BlockSpec(memory_space=pl.ANY)] * len(deps), deps


def _lat_mla_attention(q_all, k_all, v_all, NB, Lb, Lk, tq, after=None):
    T = q_all.shape[0]
    nq = Lb // tq
    dep_specs, dep_args = _after(after)
    return pl.pallas_call(
        _lat_mla_kernel,
        grid=(NB, nq),
        in_specs=[pl.BlockSpec((tq, 512), lambda b, i: (b * nq + i, 0)),
                  pl.BlockSpec((Lk, 512), lambda b, i: (b, 0)),
                  pl.BlockSpec((Lk, 256), lambda b, i: (b, 0))] + dep_specs,
        out_specs=pl.BlockSpec((tq, 256), lambda b, i: (b * nq + i, 0)),
        out_shape=jax.ShapeDtypeStruct((T, 256), F32),
        compiler_params=_cp("arbitrary", "arbitrary"),
        name="lat_mla_attention",
    )(q_all, k_all, v_all, *dep_args)


def _attn_local_ctx(q, locs, kc, vc, sink):
    s_ctx = _dot_nt(q, kc)
    m_ctx = s_ctx.max(-1, keepdims=True)
    if sink is not None:
        m_ctx = jnp.maximum(m_ctx, sink)
    ms, dens, accs = [], [], []
    for rs, k, v, mk in locs:
        s = _dot_nt(q[rs], k)
        s = s + mk[1] if mk[0] == "add" else jnp.where(mk[1], s, NEG)
        m = jnp.maximum(s.max(-1, keepdims=True), m_ctx[rs])
        p = jnp.exp2(s - m)
        ms.append(m)
        dens.append(p.sum(-1, keepdims=True))
        accs.append(_dot(p.astype(BF16), v))
    m = jnp.concatenate(ms, axis=0)
    p = jnp.exp2(s_ctx - m)
    den = jnp.concatenate(dens, axis=0) + p.sum(-1, keepdims=True)
    if sink is not None:
        den = den + jnp.exp2(sink - m)
    return (jnp.concatenate(accs, axis=0) + _dot(p.astype(BF16), vc)) / den


def _lat_win_kernel(q_ref, qr_ref, k_ref, kr_ref, v_ref, kc_ref, vc_ref, cq_ref, sq_ref, ck_ref, sk_ref,
                    sink_ref, o_ref, *, Lb, bpt):
    t = pl.program_id(1)
    q = (q_ref[...] * cq_ref[...] + qr_ref[...] * sq_ref[...]) * ATT_SCALE
    kc = kc_ref[...].astype(BF16)
    vc = vc_ref[...].astype(BF16)
    blocks = []
    for bb in range(bpt):
        i = t * bpt + bb
        start = pl.multiple_of(jnp.clip((i - 1) * 128, 0, Lb - 384), 128)
        win = pl.ds(start, 384)
        kk = (k_ref[win, :] * ck_ref[win, :] + kr_ref[win, :] * sk_ref[win, :]).astype(BF16)
        qpos = i * 128 + lax.broadcasted_iota(jnp.int32, (128, 384), 0)
        kpos = start + lax.broadcasted_iota(jnp.int32, (128, 384), 1)
        blocks.append((kk, v_ref[win, :].astype(BF16), jnp.abs(qpos - kpos) <= 128))
    for h in range(4):
        g = h // 2
        sl = slice(64 * g, 64 * (g + 1))
        locs = [(slice(128 * bb, 128 * (bb + 1)), kk[:, sl], vv[:, sl], ("keep", valid))
                for bb, (kk, vv, valid) in enumerate(blocks)]
        qh = q[:, 64 * h:64 * (h + 1)].astype(BF16)
        o_ref[:, 64 * h:64 * (h + 1)] = _attn_local_ctx(qh, locs, kc[:, sl], vc[:, sl], sink_ref[h] * LOG2E)


def _lat_win_attention(proj, kc, vc, tabs, sink, NB, Lb):
    T = proj.shape[0]
    bpt = 1
    tq = 128 * bpt
    nt = Lb // tq
    Lc = kc.shape[1]
    cq, sq, ck, sk = tabs
    qspec = lambda off: pl.BlockSpec((tq, 256), lambda b, i: (b * nt + i, off // 256))
    kspec = lambda off: pl.BlockSpec((Lb, 128), lambda b, i: (b, off // 128))
    cspec = pl.BlockSpec((None, Lc, 128), lambda b, i: (b, 0, 0))
    return pl.pallas_call(
        functools.partial(_lat_win_kernel, Lb=Lb, bpt=bpt),
        grid=(NB, nt),
        in_specs=[qspec(P_WQ), qspec(P_WQR), kspec(P_WK), kspec(P_WKR), kspec(P_WV), cspec, cspec,
                  pl.BlockSpec((tq, 256), lambda b, i: (i, 0)), pl.BlockSpec((tq, 256), lambda b, i: (i, 0)),
                  pl.BlockSpec((Lb, 128), lambda b, i: (0, 0)), pl.BlockSpec((Lb, 128), lambda b, i: (0, 0)),
                  pl.BlockSpec(memory_space=pltpu.SMEM)],
        out_specs=pl.BlockSpec((tq, 256), lambda b, i: (b * nt + i, 0)),
        out_shape=jax.ShapeDtypeStruct((T, 256), F32),
        compiler_params=_cp("arbitrary", "arbitrary"),
        name="lat_win_attention",
    )(proj, proj, proj, proj, proj, kc, vc, cq, sq, ck, sk, sink)


def _na_bias_kernel(rpb_ref, o_ref):
    h = pl.program_id(0)
    qc = lax.broadcasted_iota(jnp.int32, (GRID_W, GRID_W), 0)
    kc = lax.broadcasted_iota(jnp.int32, (GRID_W, GRID_W), 1)
    dc = kc - qc + (NA_KW - 1)
    wstart = jnp.clip(qc - NA_KW // 2, 0, GRID_W - NA_KW)
    ok = (kc >= wstart) & (kc < wstart + NA_KW)
    n_dc = 2 * NA_KW - 1
    n_dr = 2 * NA_KH - 1
    tabs = []
    for dr in range(n_dr):
        t = jnp.zeros((GRID_W, GRID_W), F32)
        for j in range(n_dc):
            t = jnp.where(dc == j, rpb_ref[(h * n_dr + dr) * n_dc + j], t)
        tabs.append(jnp.where(ok, t * LOG2E, NEG))
    for o in range(NA_KH):
        for a in range(NA_KH):
            o_ref[o, :, GRID_W * a:GRID_W * (a + 1)] = tabs[a + NA_KH - 1 - o]


def _na_bias(rpb):
    H = rpb.shape[0]
    return pl.pallas_call(
        _na_bias_kernel,
        grid=(H,),
        in_specs=[pl.BlockSpec(memory_space=pltpu.SMEM)],
        out_specs=pl.BlockSpec((None, NA_KH, GRID_W, NA_KH * GRID_W), lambda h: (h, 0, 0, 0)),
        out_shape=jax.ShapeDtypeStruct((H, NA_KH, GRID_W, NA_KH * GRID_W), F32),
        compiler_params=_cp("arbitrary"),
        name="na_bias",
    )(rpb.reshape(-1))


def _lat_na_kernel(q_ref, k_ref, v_ref, kc_ref, vc_ref, bias_ref, o_ref, *, rows, rpt):
    t = pl.program_id(1)
    q = q_ref[...] * ATT_SCALE
    kc = kc_ref[...].astype(BF16)
    vc = vc_ref[...].astype(BF16)
    bands = []
    for rr in range(rpt):
        r = t * rpt + rr
        first = jnp.clip(r - NA_KH // 2, 0, rows - NA_KH)
        win = pl.ds(pl.multiple_of(first * GRID_W, GRID_W), NA_KH * GRID_W)
        bands.append((k_ref[win, :].astype(BF16), v_ref[win, :].astype(BF16), r - first))
    for h in range(4):
        sl = slice(64 * h, 64 * (h + 1))
        locs = [(slice(GRID_W * rr, GRID_W * (rr + 1)), kk[:, sl], vv[:, sl], ("add", bias_ref[h, off]))
                for rr, (kk, vv, off) in enumerate(bands)]
        o_ref[:, sl] = _attn_local_ctx(q[:, sl].astype(BF16), locs, kc[:, sl], vc[:, sl], None)


def _lat_na_attention(proj, kc, vc, bias, NB, Lb):
    T = proj.shape[0]
    rows = Lb // GRID_W
    rpt = 8
    tq = GRID_W * rpt
    nt = rows // rpt
    Lc = kc.shape[1]
    kspec = lambda off: pl.BlockSpec((Lb, 256), lambda b, t: (b, off // 256))
    cspec = pl.BlockSpec((None, Lc, 256), lambda b, t: (b, 0, 0))
    return pl.pallas_call(
        functools.partial(_lat_na_kernel, rows=rows, rpt=rpt),
        grid=(NB, nt),
        in_specs=[pl.BlockSpec((tq, 256), lambda b, t: (b * nt + t, P_NQ // 256)),
                  kspec(P_NK), kspec(P_NV), cspec, cspec,
                  pl.BlockSpec((4, NA_KH, GRID_W, NA_KH * GRID_W), lambda b, t: (0, 0, 0, 0))],
        out_specs=pl.BlockSpec((tq, 256), lambda b, t: (b * nt + t, 0)),
        out_shape=jax.ShapeDtypeStruct((T, 256), F32),
        compiler_params=_cp("arbitrary", "arbitrary"),
        name="lat_na_attention",
    )(proj, proj, proj, kc, vc, bias)


def _short_conv_kernel(a_ref, b_ref, c_ref, w_ref, bias_ref, oa_ref, ob_ref, oc_ref, *, L):
    t = lax.broadcasted_iota(jnp.int32, (L, HY_C), 0)
    for n, (x_ref, o_ref) in enumerate(((a_ref, oa_ref), (b_ref, ob_ref), (c_ref, oc_ref))):
        sl = slice(HY_C * n, HY_C * (n + 1))
        x = x_ref[...]
        prev = jnp.where(t == 0, 0.0, pltpu.roll(x, 1, axis=0))
        nxt = jnp.where(t == L - 1, 0.0, pltpu.roll(x, L - 1, axis=0))
        o_ref[...] = prev * w_ref[0:1, sl] + x * w_ref[1:2, sl] + nxt * w_ref[2:3, sl] + bias_ref[:, sl]


def _short_conv(proj, w, b, NB, Lb):
    T = proj.shape[0]
    spec = lambda c: pl.BlockSpec((Lb, HY_C), lambda i: (i, c))
    return pl.pallas_call(
        functools.partial(_short_conv_kernel, L=Lb),
        grid=(NB,),
        in_specs=[spec(P_HY // HY_C), spec(P_HY // HY_C + 1), spec(P_HY // HY_C + 2),
                  pl.BlockSpec((3, 3 * HY_C), lambda i: (0, 0)), pl.BlockSpec((1, 3 * HY_C), lambda i: (0, 0))],
        out_specs=[spec(0)] * 3,
        out_shape=[jax.ShapeDtypeStruct((T, HY_C), F32)] * 3,
        compiler_params=_cp("arbitrary"),
        name="hyena_short_conv",
    )(proj, proj, proj, w, b)


def _hy_filter_kernel(w1_ref, b1_ref, w2_ref, b2_ref, w3_ref, freq_ref, ld_ref, fs_ref, nyq_ref, *, L):
    ti = lax.broadcasted_iota(jnp.int32, (L, 128), 0)
    t = ti.astype(F32)
    j = lax.broadcasted_iota(jnp.int32, (L, 128), 1)
    band = jnp.where(j <= HY_BANDS, j - 1, j - 1 - HY_BANDS).astype(F32)
    ang = (2.0 * math.pi / L) * t * band
    tn = t / L
    z = jnp.where(j == 0, tn, jnp.where(j <= HY_BANDS, jnp.cos(ang),
                                        jnp.where(j <= 2 * HY_BANDS, -jnp.sin(ang), 0.0)))
    a = jnp.sin(freq_ref[0:1, :] * (_dot_hi(z, w1_ref[...]) + b1_ref[...]))
    a = jnp.sin(freq_ref[1:2, :] * (_dot_hi(a, w2_ref[...]) + b2_ref[...]))
    filt = _dot_hi(a, w3_ref[...])
    tcol = lax.broadcasted_iota(jnp.int32, (L, 4 * HY_C), 0)
    filt = filt * jnp.exp(-(tcol.astype(F32) / L) * jnp.exp(ld_ref[...]))
    t1 = lax.broadcasted_iota(jnp.int32, (L, HY_C), 0)
    sign = jnp.where(t1 % 2 == 0, 1.0, -1.0)
    for n in range(2):
        fwd = filt[:, 2 * HY_C * n:2 * HY_C * n + HY_C]
        bwd = jnp.where(t1 == 0, 0.0, filt[:, 2 * HY_C * n + HY_C:2 * HY_C * (n + 1)])
        tot = fwd + bwd
        fs_ref[:, HY_C * n:HY_C * (n + 1)] = tot
        fs_ref[:, 2 * HY_C + HY_C * n:2 * HY_C + HY_C * (n + 1)] = fwd - bwd
        nyq_ref[:, HY_C * n:HY_C * (n + 1)] = (tot * sign).sum(0, keepdims=True)


def _hy_filter(L, w1p, b1, w2, b2, w3, freq, ld):
    full = lambda s: pl.BlockSpec(s, lambda: tuple(0 for _ in s))
    return pl.pallas_call(
        functools.partial(_hy_filter_kernel, L=L),
        in_specs=[full((128, 64)), full((1, 64)), full((64, 64)), full((1, 64)), full((64, 4 * HY_C)),
                  full((2, 64)), full((1, 4 * HY_C))],
        out_specs=[full((L, 4 * HY_C)), full((1, 2 * HY_C))],
        out_shape=[jax.ShapeDtypeStruct((L, 4 * HY_C), F32), jax.ShapeDtypeStruct((1, 2 * HY_C), F32)],
        compiler_params=pltpu.CompilerParams(vmem_limit_bytes=VMEM_LIMIT),
        name="hyena_filter",
    )(w1p, b1, w2, b2, w3, freq, ld)


def _hy_gdft_kernel(cm_ref, sm_ref, fs_ref, nyq_ref, gr_ref, gi_ref, *, tm):
    m = pl.program_id(0)
    f = fs_ref[...].astype(BF16)
    gr_ref[...] = _dot(cm_ref[...], f[:, :2 * HY_C])
    gi = _dot(sm_ref[...], f[:, 2 * HY_C:])
    row = m * tm + lax.broadcasted_iota(jnp.int32, (tm, 2 * HY_C), 0)
    gi_ref[...] = jnp.where(row == 0, nyq_ref[...], gi)


def _hy_gdft(cm, sm, fs, nyq, L, tm):
    return pl.pallas_call(
        functools.partial(_hy_gdft_kernel, tm=tm),
        grid=(L // tm,),
        in_specs=[pl.BlockSpec((tm, L), lambda m: (m, 0)), pl.BlockSpec((tm, L), lambda m: (m, 0)),
                  pl.BlockSpec((L, 4 * HY_C), lambda m: (0, 0)), pl.BlockSpec((1, 2 * HY_C), lambda m: (0, 0))],
        out_specs=[pl.BlockSpec((tm, 2 * HY_C), lambda m: (m, 0))] * 2,
        out_shape=[jax.ShapeDtypeStruct((L, 2 * HY_C), F32)] * 2,
        compiler_params=_cp("arbitrary"),
        name="hyena_filter_dft",
    )(cm, sm, fs, nyq)


def _hy_fwd_kernel(cm_ref, sm_ref, z_ref, gr_ref, gi_ref, yr_ref, yi_ref, *, L, tm, ns):
    m = pl.program_id(1)
    gr = gr_ref[...]
    gi = gi_ref[...]
    row0 = (m * tm + lax.broadcasted_iota(jnp.int32, (tm, HY_C), 0)) == 0
    s = jnp.where(row0, 0.5 / L, 1.0 / L)
    for g in range(ns):
        zb = z_ref[g * L:(g + 1) * L, :].astype(BF16)
        zr = _dot(cm_ref[...], zb)
        zi = _dot(sm_ref[...], zb)
        zigi = zi * gi
        yr_ref[g * tm:(g + 1) * tm, :] = ((zr * gr - jnp.where(row0, 0.0, zigi)) * s).astype(BF16)
        yi_ref[g * tm:(g + 1) * tm, :] = (jnp.where(row0, zigi, zr * gi + zi * gr) * s).astype(BF16)


def _hy_seqs_per_step(NB, Lb, tm):
    ns = max(1, 2048 // Lb) if tm == Lb else 1
    while NB % ns:
        ns //= 2
    return ns


def _hy_fwd(cm, sm, z, gr, gi, n, NB, Lb, tm):
    T = z.shape[0]
    nm = Lb // tm
    ns = _hy_seqs_per_step(NB, Lb, tm)
    return pl.pallas_call(
        functools.partial(_hy_fwd_kernel, L=Lb, tm=tm, ns=ns),
        grid=(NB // ns, nm),
        in_specs=[pl.BlockSpec((tm, Lb), lambda b, m: (m, 0)), pl.BlockSpec((tm, Lb), lambda b, m: (m, 0)),
                  pl.BlockSpec((ns * Lb, HY_C), lambda b, m: (b, 0)),
                  pl.BlockSpec((tm, HY_C), lambda b, m: (m, n)), pl.BlockSpec((tm, HY_C), lambda b, m: (m, n))],
        out_specs=[pl.BlockSpec((ns * tm, HY_C), lambda b, m: (b * nm + m, 0))] * 2,
        out_shape=[jax.ShapeDtypeStruct((T, HY_C), BF16)] * 2,
        compiler_params=_cp("arbitrary", "arbitrary"),
        name="hyena_fwd_dft",
    )(cm, sm, z, gr, gi)


def _hy_inv_kernel(cm_ref, smt_ref, yr_ref, yi_ref, z_ref, g_ref, skip_ref, o_ref, *, L, tm, ns):
    for g in range(ns):
        seq = slice(g * L, (g + 1) * L)
        out = slice(g * tm, (g + 1) * tm)
        conv = _dot(cm_ref[...], yr_ref[seq, :]) + _dot(smt_ref[...], yi_ref[seq, :])
        o_ref[out, :] = g_ref[out, :] * (conv + skip_ref[...] * z_ref[out, :])


def _hy_inv(cm, smt, yr, yi, z, gate, skip, n, NB, Lb, tm):
    T = z.shape[0]
    nm = Lb // tm
    ns = _hy_seqs_per_step(NB, Lb, tm)
    tile = pl.BlockSpec((ns * tm, HY_C), lambda b, m: (b * nm + m, 0))
    seq = pl.BlockSpec((ns * Lb, HY_C), lambda b, m: (b, 0))
    return pl.pallas_call(
        functools.partial(_hy_inv_kernel, L=Lb, tm=tm, ns=ns),
        grid=(NB // ns, nm),
        in_specs=[pl.BlockSpec((tm, Lb), lambda b, m: (m, 0)), pl.BlockSpec((tm, Lb), lambda b, m: (m, 0)),
                  seq, seq, tile, tile, pl.BlockSpec((None, 1, HY_C), lambda b, m: (n, 0, 0))],
        out_specs=tile,
        out_shape=jax.ShapeDtypeStruct((T, HY_C), F32),
        compiler_params=_cp("arbitrary", "arbitrary"),
        name="hyena_inv_dft",
    )(cm, smt, yr, yi, z, gate, skip)


def _dft_mats(L):
    k = jnp.arange(L, dtype=jnp.int32)
    blk = 64

    def trig(mult):
        ang = ((mult[:, None] * k[None, :]) % (2 * L)).astype(F32) * (math.pi / L)
        return jnp.cos(ang), jnp.sin(ang)

    ca, sa = trig(jnp.arange(L // blk, dtype=jnp.int32) * blk)
    cb, sb = trig(jnp.arange(blk, dtype=jnp.int32))
    cm = (ca[:, None, :] * cb[None] - sa[:, None, :] * sb[None]).reshape(L, L)
    s = -(sa[:, None, :] * cb[None] + ca[:, None, :] * sb[None]).reshape(L, L)
    alt = jnp.where(k % 2 == 0, 1.0, -1.0).astype(F32)
    sm = jnp.where(k[:, None] == 0, alt[None, :], s)
    smt = jnp.where(k[None, :] == 0, alt[:, None], s)
    return cm.astype(BF16), sm.astype(BF16), smt.astype(BF16)


def _merge_kernel(oa_ref, ob_ref, oc_ref, od_ref, g0_ref, g1_ref, g2_ref, g3_ref, wb_ref, wo_ref, x_ref, mod_ref,
                  lg_ref, lb_ref, rt_ref, *rest):
    x1_ref, hp_ref, logit_ref = rest[-3:]
    acc = None
    for o_ref, g_ref, i in ((oa_ref, g0_ref, 0), (ob_ref, g1_ref, 1), (oc_ref, g2_ref, 2), (od_ref, g3_ref, 3)):
        y = _sigmoid(g_ref[...].astype(F32)) * _dot(o_ref[...].astype(BF16), wb_ref[i])
        acc = y if acc is None else acc + y
    mix = _dot(acc.astype(BF16), wo_ref[...])
    m = mod_ref[...]
    x1 = _layer_norm(DN_ALPHA * x_ref[...] + m[2:3, :] * mix, lg_ref[...], lb_ref[...])
    x1_ref[...] = x1
    h2 = x1 * (1.0 + m[4:5, :]) + m[3:4, :]
    hp_ref[...] = _pack_pairs(h2)
    hi = h2.astype(BF16)
    lo = (h2 - hi.astype(F32)).astype(BF16)
    r = rt_ref[...]
    r_hi = r.astype(BF16)
    r_lo = (r - r_hi.astype(F32)).astype(BF16)
    both = _dot_nt(jnp.concatenate([r_hi, r_lo], axis=0), hi)
    logit_ref[...] = both[:N_EXPERTS, :] + both[N_EXPERTS:, :] + _dot_nt(r_hi, lo)


def _merge(outs, gates, wb, wo, x, mod, lg, lb, router, mod_row, bm, after=None):
    T = x.shape[0]
    row = lambda w: pl.BlockSpec((bm, w), lambda i: (i, 0))
    gspec = lambda n: pl.BlockSpec((bm, D), lambda i: (i, n))
    fixed2 = lambda s: pl.BlockSpec(s, lambda i: (0, 0))
    dep_specs, dep_args = _after(after)
    return pl.pallas_call(
        _merge_kernel,
        grid=(T // bm,),
        in_specs=[row(256)] * 4 + [gspec(0), gspec(1), gspec(2), gspec(3),
                                   pl.BlockSpec((4, 256, D), lambda i: (0, 0, 0)), fixed2((D, D)), row(D),
                                   pl.BlockSpec((None, 6, D), lambda i: (mod_row(i), 0, 0)),
                                   fixed2((1, D)), fixed2((1, D)), fixed2((N_EXPERTS, D))] + dep_specs,
        out_specs=[row(D), row(D // 2), pl.BlockSpec((N_EXPERTS, bm), lambda i: (0, i))],
        out_shape=[jax.ShapeDtypeStruct((T, D), F32), jax.ShapeDtypeStruct((T, D // 2), jnp.int32),
                   jax.ShapeDtypeStruct((N_EXPERTS, T), F32)],
        compiler_params=_cp("arbitrary"),
        name="merge_norm",
    )(*outs, gates, gates, gates, gates, wb, wo, x, mod, lg, lb, router, *dep_args)


def _router_kernel(logit_ref, bias_ref, g_ref, rank_ref, cnt_ref, *, tt):
    per = N_EXPERTS // N_GROUPS
    scores = _sigmoid(logit_ref[...])
    sel = (scores + bias_ref[...]).reshape(N_GROUPS, per, tt)
    gid = lax.broadcasted_iota(jnp.int32, (N_GROUPS, per, tt), 0).astype(F32)
    jid = lax.broadcasted_iota(jnp.int32, (N_GROUPS, per, tt), 1).astype(F32)
    eid = gid * per + jid
    ninf = -jnp.inf
    m1 = sel.max(1, keepdims=True)
    i1 = jnp.where(sel == m1, jid, float(per)).min(1, keepdims=True)
    m2 = jnp.where(jid == i1, ninf, sel).max(1, keepdims=True)
    gs = m1 + m2
    g1 = lax.broadcasted_iota(jnp.int32, (N_GROUPS, 1, tt), 0).astype(F32)
    chosen = jnp.zeros((N_GROUPS, 1, tt), F32)
    for _ in range(TOPK_GROUPS):
        mx = gs.max(0, keepdims=True)
        gi = jnp.where(gs == mx, g1, float(N_GROUPS)).min(0, keepdims=True)
        pick = g1 == gi
        chosen = jnp.where(pick, 1.0, chosen)
        gs = jnp.where(pick, ninf, gs)
    cand = jnp.where(chosen > 0.0, sel, NEG)
    picked = jnp.zeros((N_GROUPS, per, tt), F32)
    for _ in range(TOP_K):
        mx = cand.max(1, keepdims=True).max(0, keepdims=True)
        ei = jnp.where(cand == mx, eid, float(N_EXPERTS)).min(1, keepdims=True).min(0, keepdims=True)
        pick = eid == ei
        picked = jnp.where(pick, 1.0, picked)
        cand = jnp.where(pick, ninf, cand)
    w = scores.reshape(N_GROUPS, per, tt) * picked
    wsum = w.sum(1, keepdims=True).sum(0, keepdims=True)
    g_ref[...] = (w / wsum * ROUTED_SCALE).reshape(N_EXPERTS, tt)
    pk = picked.reshape(N_EXPERTS, tt)
    t_in = lax.broadcasted_iota(jnp.int32, (tt, tt), 0)
    t_out = lax.broadcasted_iota(jnp.int32, (tt, tt), 1)
    upper = jnp.where(t_in <= t_out, 1.0, 0.0).astype(BF16)

    @pl.when(pl.program_id(0) == 0)
    def _():
        cnt_ref[...] = jnp.zeros_like(cnt_ref)

    before = cnt_ref[:, 0:1]
    rank_ref[...] = jnp.where(pk > 0.0, before + _dot(pk.astype(BF16), upper) - 1.0, -1.0)
    cnt_ref[...] += pk.sum(-1, keepdims=True)


def _router(logits_t, bias, tt):
    T = logits_t.shape[1]
    tile = pl.BlockSpec((N_EXPERTS, tt), lambda i: (0, i))
    return pl.pallas_call(
        functools.partial(_router_kernel, tt=tt),
        grid=(T // tt,),
        in_specs=[tile, pl.BlockSpec((N_EXPERTS, 1), lambda i: (0, 0))],
        out_specs=[tile, tile, pl.BlockSpec((N_EXPERTS, 128), lambda i: (0, 0))],
        out_shape=[jax.ShapeDtypeStruct((N_EXPERTS, T), F32), jax.ShapeDtypeStruct((N_EXPERTS, T), F32),
                   jax.ShapeDtypeStruct((N_EXPERTS, 128), F32)],
        compiler_params=_cp("arbitrary"),
        name="moe_router",
    )(logits_t, bias)


def _route_pos_kernel(gate_ref, rank_ref, cnt_ref, pos_ref, w_ref, te_ref, nx_ref, nt_ref, *, tm, nt_max):
    ei = lax.broadcasted_iota(jnp.int32, (N_EXPERTS, N_EXPERTS), 0)
    ej = lax.broadcasted_iota(jnp.int32, (N_EXPERTS, N_EXPERTS), 1)
    below = jnp.where(ej < ei, 1.0, 0.0)
    padded = jnp.ceil(cnt_ref[...] * (1.0 / tm)) * tm
    offs = _dot_hi(below, padded)
    rank = rank_ref[...]
    routed = rank >= 0.0
    pos = offs[:, 0:1] + rank
    slot = _dot(below.astype(BF16), jnp.where(routed, 1.0, 0.0).astype(BF16))
    gate = gate_ref[...]
    for k in range(TOP_K):
        mine = routed & (slot == float(k))
        pos_ref[k:k + 1, :] = jnp.where(mine, pos, 0.0).sum(0, keepdims=True).astype(jnp.int32)
        w_ref[k:k + 1, :] = jnp.where(mine, gate, 0.0).sum(0, keepdims=True)
    ends = (offs + padded)[:, 0:1]
    first = (lax.broadcasted_iota(jnp.int32, (N_EXPERTS, nt_max), 1) * tm).astype(F32)
    te = jnp.minimum(jnp.where(ends <= first, 1.0, 0.0).sum(0, keepdims=True), N_EXPERTS - 1.0)
    te_ref[...] = te.astype(jnp.int32)
    eid = lax.broadcasted_iota(jnp.int32, (N_EXPERTS, nt_max), 0).astype(F32)
    nx_ref[...] = (jnp.where(eid == te, ends, 0.0).sum(0, keepdims=True) * (1.0 / tm)).astype(jnp.int32)
    nt_ref[...] = (padded.sum(0, keepdims=True) * (1.0 / tm)).astype(jnp.int32)


def _route_pos(gate_t, rank, cnt, tt, tm, nt_max):
    T = gate_t.shape[1]
    tile = pl.BlockSpec((N_EXPERTS, tt), lambda i: (0, i))
    out = pl.BlockSpec((TOP_K, tt), lambda i: (0, i))
    return pl.pallas_call(
        functools.partial(_route_pos_kernel, tm=tm, nt_max=nt_max),
        grid=(T // tt,),
        in_specs=[tile, tile, pl.BlockSpec((N_EXPERTS, 128), lambda i: (0, 0))],
        out_specs=[out, out, pl.BlockSpec((1, nt_max), lambda i: (0, 0)), pl.BlockSpec((1, nt_max), lambda i: (0, 0)),
                   pl.BlockSpec((1, 128), lambda i: (0, 0))],
        out_shape=[jax.ShapeDtypeStruct((TOP_K, T), jnp.int32), jax.ShapeDtypeStruct((TOP_K, T), F32),
                   jax.ShapeDtypeStruct((1, nt_max), jnp.int32), jax.ShapeDtypeStruct((1, nt_max), jnp.int32),
                   jax.ShapeDtypeStruct((1, 128), jnp.int32)],
        compiler_params=_cp("arbitrary"),
        name="moe_positions",
    )(gate_t, rank, cnt)


def _gmm_kernel(te_ref, nx_ref, nt_ref, xs_ref, w1_hbm, w3_hbm, w2_hbm, *rest, l):
    ys_ref, b1_ref, b3_ref, b2_ref, f1_ref, f3_ref, f2_ref, seg_ref, sem = rest[-9:]
    j = pl.program_id(0)
    live = j < nt_ref[0]
    new_expert = (j == 0) | (te_ref[j] != te_ref[jnp.maximum(j - 1, 0)])

    def fetch(e, slot):
        return [pltpu.make_async_copy(w_hbm.at[l, e], f_ref.at[slot], sem.at[i, slot])
                for i, (w_hbm, f_ref) in enumerate(((w1_hbm, f1_ref), (w3_hbm, f3_ref), (w2_hbm, f2_ref)))]

    @pl.when(live & new_expert)
    def _():
        @pl.when(j == 0)
        def _():
            seg_ref[0] = 0
            for c in fetch(te_ref[0], 0):
                c.start()

        slot = lax.rem(seg_ref[0], 2)
        for c in fetch(te_ref[j], slot):
            c.wait()
        b1_ref[...] = f1_ref[slot].astype(BF16)
        b3_ref[...] = f3_ref[slot].astype(BF16)
        b2_ref[...] = f2_ref[slot].astype(BF16)
        nxt = nx_ref[j]

        @pl.when(nxt < nt_ref[0])
        def _():
            for c in fetch(te_ref[nxt], 1 - slot):
                c.start()

        seg_ref[0] = seg_ref[0] + 1

    @pl.when(live)
    def _():
        xa, xb = _unpack_pairs(xs_ref[...])
        xa, xb = xa.astype(BF16), xb.astype(BF16)
        half = D // 2
        a = _dot(xa, b1_ref[:half, :]) + _dot(xb, b1_ref[half:, :])
        b = _dot(xa, b3_ref[:half, :]) + _dot(xb, b3_ref[half:, :])
        hid = (a * _sigmoid(a) * b).astype(BF16)
        ys_ref[...] = _pack_pairs(_dot(hid, b2_ref[...]))


def _gmm(te, nx, nt, xs, w1, w3, w2, l, tm, after=None):
    n_slots = xs.shape[0]
    ds = D_EXPERT
    rows = pl.BlockSpec((tm, D // 2), lambda j, te, nx, nt: (jnp.minimum(j, nt[0] - 1), 0))
    hbm = pl.BlockSpec(memory_space=pl.ANY)
    dep_specs, dep_args = _after(after)
    return pl.pallas_call(
        functools.partial(_gmm_kernel, l=l),
        grid_spec=pltpu.PrefetchScalarGridSpec(
            num_scalar_prefetch=3,
            grid=(n_slots // tm,),
            in_specs=[rows, hbm, hbm, hbm] + dep_specs,
            out_specs=rows,
            scratch_shapes=[pltpu.VMEM((D, ds), BF16), pltpu.VMEM((D, ds), BF16), pltpu.VMEM((ds, D), BF16),
                            pltpu.VMEM((2, D, ds), F32), pltpu.VMEM((2, D, ds), F32), pltpu.VMEM((2, ds, D), F32),
                            pltpu.SMEM((1,), jnp.int32), pltpu.SemaphoreType.DMA((3, 2))]),
        out_shape=jax.ShapeDtypeStruct((n_slots, D // 2), jnp.int32),
        compiler_params=_cp("arbitrary"),
        name="moe_grouped_ffn",
    )(te, nx, nt, xs, w1, w3, w2, *dep_args)


def _combine_kernel(yk_ref, w_ref, hp_ref, s1_ref, s3_ref, s2_ref, x_ref, mod_ref, lg_ref, lb_ref, o_ref):
    w = w_ref[...]
    acc_a = acc_b = None
    for k in range(TOP_K):
        ya, yb = _unpack_pairs(yk_ref[k])
        wk = w[:, k:k + 1]
        acc_a = wk * ya if acc_a is None else acc_a + wk * ya
        acc_b = wk * yb if acc_b is None else acc_b + wk * yb
    ha, hb = _unpack_pairs(hp_ref[...])
    ha, hb = ha.astype(BF16), hb.astype(BF16)
    half = D // 2
    a = _dot(ha, s1_ref[:half, :]) + _dot(hb, s1_ref[half:, :])
    b = _dot(ha, s3_ref[:half, :]) + _dot(hb, s3_ref[half:, :])
    y = jnp.concatenate([acc_a, acc_b], axis=1) + _dot((a * _sigmoid(a) * b).astype(BF16), s2_ref[...])
    m = mod_ref[...]
    o_ref[...] = _layer_norm(DN_ALPHA * x_ref[...] + m[5:6, :] * y, lg_ref[...], lb_ref[...])


def _combine(yk, w, hp, s1, s3, s2, x1, mod, lg, lb, mod_row, bm):
    T = x1.shape[0]
    ds = D_EXPERT
    row = lambda n: pl.BlockSpec((bm, n), lambda i: (i, 0))
    fixed = lambda s: pl.BlockSpec(s, lambda i: (0, 0))
    return pl.pallas_call(
        _combine_kernel,
        grid=(T // bm,),
        in_specs=[pl.BlockSpec((TOP_K, bm, D // 2), lambda i: (0, i, 0)), row(TOP_K), row(D // 2),
                  fixed((D, ds)), fixed((D, ds)), fixed((ds, D)), row(D),
                  pl.BlockSpec((None, 6, D), lambda i: (mod_row(i), 0, 0)), fixed((1, D)), fixed((1, D))],
        out_specs=row(D),
        out_shape=jax.ShapeDtypeStruct((T, D), F32),
        compiler_params=_cp("arbitrary"),
        name="moe_combine_norm",
    )(yk, w, hp, s1, s3, s2, x1, mod, lg, lb)


def _sc_worker():
    return lax.axis_index("s") * SC_CORES + lax.axis_index("c")


def _sc_mesh():
    return plsc.VectorSubcoreMesh(core_axis_name="c", subcore_axis_name="s")


def _sc_copy_rows(table_hbm, idx_v, out_hbm, base, n_chunks, bufs, gsem, wsem):
    def gather(c, b):
        rows = idx_v.at[pl.ds(pl.multiple_of(c * SC_ROWS, SC_ROWS), SC_ROWS)]
        return pltpu.make_async_copy(table_hbm.at[rows], bufs.at[b], gsem.at[b])

    def write(c, b):
        rows = pl.ds(base + pl.multiple_of(c * SC_ROWS, SC_ROWS), SC_ROWS)
        return pltpu.make_async_copy(bufs.at[b], out_hbm.at[rows], wsem.at[b])

    gather(0, 0).start()

    @pl.loop(0, n_chunks // 2)
    def _(p):
        for b in range(2):
            c = 2 * p + b
            gather(c, b).wait()
            write(c, b).start()

            @pl.when(c >= 1)
            def _():
                write(c - 1, 1 - b).wait()

            @pl.when(c + 1 < n_chunks)
            def _():
                gather(c + 1, 1 - b).start()

    write(n_chunks - 1, 1).wait()


def _sc_gather(table, idx):
    N, W = idx.shape[0], table.shape[1]
    per_w = N // SC_WORKERS
    n_chunks = per_w // SC_ROWS

    def body(table_hbm, idx_hbm, out_hbm, idx_v, bufs, gsem, wsem):
        base = _sc_worker() * per_w
        pltpu.sync_copy(idx_hbm.at[pl.ds(base, per_w)], idx_v)
        _sc_copy_rows(table_hbm, idx_v, out_hbm, base, n_chunks, bufs, gsem, wsem)

    return pl.kernel(
        body, out_type=jax.ShapeDtypeStruct((N, W), table.dtype), mesh=_sc_mesh(),
        scratch_types=[pltpu.VMEM((per_w,), jnp.int32), pltpu.VMEM((2, SC_ROWS, W), table.dtype),
                       pltpu.SemaphoreType.DMA((2,)), pltpu.SemaphoreType.DMA((2,))],
        name="sc_gather",
    )(table, idx)


def _sc_dispatch(pos, table, n_slots):
    NP, (T, W) = pos.shape[0], table.shape
    per_w = n_slots // SC_WORKERS
    n_chunks = per_w // SC_ROWS
    scan = 8192

    def body(pos_hbm, table_hbm, out_hbm, pos_v, src_v, bufs, gsem, wsem):
        base = _sc_worker() * per_w
        lane = lax.iota(jnp.int32, SC_LANES)

        @pl.loop(0, per_w // SC_LANES)
        def _(j):
            o = pl.multiple_of(j * SC_LANES, SC_LANES)
            src_v[pl.ds(o, SC_LANES)] = (base + o + lane) & (T - 1)

        @pl.loop(0, NP // scan)
        def _(c):
            pltpu.sync_copy(pos_hbm.at[pl.ds(pl.multiple_of(c * scan, scan), scan)], pos_v)

            @pl.loop(0, scan // SC_LANES)
            def _(v):
                o = pl.multiple_of(v * SC_LANES, SC_LANES)
                p = pos_v[pl.ds(o, SC_LANES)] - base
                mine = (p >= 0) & (p < per_w)
                tok = (c * scan + o + lane) & (T - 1)
                plsc.store_scatter(src_v, [jnp.where(mine, p, 0)], tok, mask=mine)

        _sc_copy_rows(table_hbm, src_v, out_hbm, base, n_chunks, bufs, gsem, wsem)

    return pl.kernel(
        body, out_type=jax.ShapeDtypeStruct((n_slots, W), table.dtype), mesh=_sc_mesh(),
        scratch_types=[pltpu.VMEM((scan,), jnp.int32), pltpu.VMEM((per_w,), jnp.int32),
                       pltpu.VMEM((2, SC_ROWS, W), table.dtype), pltpu.SemaphoreType.DMA((2,)),
                       pltpu.SemaphoreType.DMA((2,))],
        compiler_params=pltpu.CompilerParams(needs_layout_passes=False),
        name="sc_dispatch",
    )(pos, table)


def _caches_kernel(*refs, nb, S):
    n_in = 6 * DEPTH
    outs = refs[n_in:]
    l = pl.program_id(0)
    for a in range(DEPTH):
        @pl.when(l == a)
        def _(a=a):
            ckv, kpe, wk, wv, nk, nv = refs[6 * a:6 * (a + 1)]
            for g in range(nb):
                rows = slice(g * S, (g + 1) * S)
                outs[0][g] = ckv[rows, :]
                outs[1][g] = kpe[rows, 64:96]
                outs[2][g] = wk[rows, :]
                outs[3][g] = wv[rows, :]
                outs[4][g] = nk[rows, :]
                outs[5][g] = nv[rows, :]


def _emit_caches(projs, ckvs, B, S):
    nb = 4
    while B % nb:
        nb //= 2

    def layer_specs(a):
        row = lambda l, b: jnp.where(l == a, b, 0)
        col = lambda w, off: pl.BlockSpec((nb * S, w), lambda l, b: (row(l, b), off // w))
        return [pl.BlockSpec((nb * S, 128), lambda l, b: (row(l, b), 0)), col(128, P_KPE), col(128, P_WK),
                col(128, P_WV), col(256, P_NK), col(256, P_NV)]

    in_specs, args = [], []
    for a in range(DEPTH):
        in_specs += layer_specs(a)
        args += [ckvs[a]] + [projs[a]] * 5
    widths = (128, 32, 128, 128, 256, 256)
    return pl.pallas_call(
        functools.partial(_caches_kernel, nb=nb, S=S),
        grid=(DEPTH, B // nb),
        in_specs=in_specs,
        out_specs=[pl.BlockSpec((nb, None, S, w), lambda l, b: (b, l, 0, 0)) for w in widths],
        out_shape=[jax.ShapeDtypeStruct((B, DEPTH, S, w), F32) for w in widths],
        compiler_params=_cp("arbitrary", "arbitrary"),
        name="context_tensors",
    )(*args)


def _rot_cols(w, q):
    a, b, c, d = w[..., :q], w[..., q:2 * q], w[..., 2 * q:3 * q], w[..., 3 * q:]
    return jnp.concatenate([-b, a, -d, c], -1)


def _prep_w_in(w):
    z = lambda n: jnp.zeros((D, n), w.dtype)
    qlat, ckv, kpe, hy = w[:, 0:256], w[:, 256:384], w[:, 384:416], w[:, 416:1184]
    wq, wk, wv = w[:, 1184:1440], w[:, 1440:1568], w[:, 1568:1696]
    nq, nk, nv, gate = w[:, 1696:1952], w[:, 1952:2208], w[:, 2208:2464], w[:, 2464:]
    wq_r = _rot_cols(wq.reshape(D, 4, 64), 16).reshape(D, 256)
    wk_r = _rot_cols(wk.reshape(D, 2, 64), 16).reshape(D, 128)
    kpe_r = _rot_cols(kpe, 8)
    cols = [qlat, ckv, z(64), kpe, z(32), hy, wq, wk, wv, nq, nk, nv, wq_r, wk_r, z(64), kpe_r, z(32), gate]
    return jnp.concatenate(cols, 1).astype(BF16)


def _prep_mla(w_uq, w_ukv):
    uq = w_uq.reshape(256, 4, 96)
    nope, pe = uq[..., :64], uq[..., 64:]
    z32 = jnp.zeros((256, 4, 32), w_uq.dtype)
    z64 = jnp.zeros((256, 4, 64), w_uq.dtype)
    wcat = jnp.concatenate([nope, pe, z32], -1).reshape(256, 512).astype(BF16)
    wrot = jnp.concatenate([z64, _rot_cols(pe, 8), z32], -1).reshape(256, 512).astype(BF16)
    ukv = w_ukv.reshape(128, 4, 128)
    wk = jnp.concatenate([ukv[..., :64], jnp.zeros((128, 4, 64), w_ukv.dtype)], -1).reshape(128, 512).astype(BF16)
    wv = ukv[..., 64:].reshape(128, 256).astype(BF16)
    return wcat, wrot, wk, wv


def _rope_tab(L, q):
    t = jnp.arange(L)
    inv = ROPE_BASE ** (-jnp.arange(q, dtype=F32) / q)
    ar = (t // GRID_W).astype(F32)[:, None] * inv[None, :]
    ac = (t % GRID_W).astype(F32)[:, None] * inv[None, :]
    cos = jnp.concatenate([jnp.cos(ar), jnp.cos(ar), jnp.cos(ac), jnp.cos(ac)], 1)
    sin = jnp.concatenate([jnp.sin(ar), jnp.sin(ar), jnp.sin(ac), jnp.sin(ac)], 1)
    return cos, sin


def _rope_tables(L):
    c8, s8 = _rope_tab(L, 8)
    c16, s16 = _rope_tab(L, 16)
    one, zero = jnp.ones((L, 64), F32), jnp.zeros((L, 64), F32)
    z32 = jnp.zeros((L, 32), F32)
    mla_q = (jnp.tile(jnp.concatenate([one, c8, z32], 1), (1, 4)), jnp.tile(jnp.concatenate([zero, s8, z32], 1), (1, 4)))
    mla_k = (jnp.concatenate([zero, c8, z32], 1), jnp.concatenate([zero, s8, z32], 1))
    win = (jnp.tile(c16, (1, 4)), jnp.tile(s16, (1, 4)), jnp.tile(c16, (1, 2)), jnp.tile(s16, (1, 2)))
    return mla_q + mla_k, win


def _hyena(proj, lp, dft, NB, Lb):
    cm, sm, smt = dft
    tm = min(Lb, 1024)
    v, x1, x2 = _short_conv(proj, lp["hy_conv_w"], lp["hy_conv_b"].reshape(1, -1), NB, Lb)
    w1p = jnp.pad(lp["hy_w1"], ((0, 128 - lp["hy_w1"].shape[0]), (0, 0)))
    fs, nyq = _hy_filter(Lb, w1p, lp["hy_b1"].reshape(1, -1), lp["hy_w2"], lp["hy_b2"].reshape(1, -1), lp["hy_w3"],
                         lp["hy_sin_freq"], lp["hy_log_decay"].reshape(1, -1))
    gr, gi = _hy_gdft(cm, sm, fs, nyq, Lb, tm)
    skip = lp["hy_skip"].reshape(2, 1, HY_C)
    z = v
    for n, gate in enumerate((x1, x2)):
        yr, yi = _hy_fwd(cm, sm, z, gr, gi, n, NB, Lb, tm)
        z = _hy_inv(cm, smt, yr, yi, z, gate, skip, n, NB, Lb, tm)
    return z


def _layer_steps(x, mod, lp, l, NB, Lb, mod_row_of_batch, dft, cache=None, tabs=None, na_bias=None):
    T = NB * Lb
    latent = cache is not None
    rows_of = lambda n: (lambda i: mod_row_of_batch((i * n) // Lb))
    span = Lb if latent else T
    bmp = min(span, 1024)
    proj, gates = _in_proj(x, mod, lp["w_in_p"], rows_of(bmp), bmp)

    gq, gkv = lp["mla_q_norm"].reshape(1, -1), lp["mla_kv_norm"].reshape(1, -1)
    wcat, wrot, wk, wv = lp["mla_w"]
    q_all, ckv_n, kpe_r = _mla_q(proj, gq, gkv, wcat, wrot, tabs[0] if latent else None, Lb, min(span, 512))
    if latent:
        ckv_c, kpe_c, kc_c, vc_c, kd_c, vd_c = cache
        Lc = ckv_c.shape[1]
        kpe_cp = jnp.pad(kpe_c, ((0, 0), (0, 0), (64, 32)))
        ckv_all = jnp.concatenate([ckv_c, ckv_n.reshape(NB, Lb, 128)], 1).reshape(NB * (Lc + Lb), 128)
        kpe_all = jnp.concatenate([kpe_cp, kpe_r.reshape(NB, Lb, 128)], 1).reshape(NB * (Lc + Lb), 128)
        k_all, v_all = _mla_kv(ckv_all, kpe_all, wk, wv, 512)
        oc = _lat_win_attention(proj, kc_c.reshape(NB, Lc, 128), vc_c.reshape(NB, Lc, 128), tabs[1],
                                lp["win_sink"], NB, Lb)
        od = _lat_na_attention(proj, kd_c.reshape(NB, Lc, 256), vd_c.reshape(NB, Lc, 256), na_bias, NB, Lb)
        ob = _hyena(proj, lp, dft, NB, Lb)
        after = yield "projected", od
        oa = _lat_mla_attention(q_all, k_all, v_all, NB, Lb, Lc + Lb, 256, after=after)
        after = None
    else:
        k_all, v_all = _mla_kv(ckv_n, kpe_r, wk, wv, 512)
        oa, oc, od = _ctx_attention(proj, q_all, k_all, v_all, lp["win_sink"], NB, Lb)
        ob = _hyena(proj, lp, dft, NB, Lb)
        after = yield "projected", oa

    bmm = min(span, 512)
    x1, hp, logits_t = _merge((oa, ob, oc, od), gates, lp["w_branch_b"], lp["w_out_b"], x, mod,
                              lp["ln1_g"].reshape(1, -1), lp["ln1_b"].reshape(1, -1), lp["moe_router"].T,
                              rows_of(bmm), bmm, after=after)
    n_slots = T * TOP_K + N_EXPERTS * MOE_TM
    gate_t, rank, cnt = _router(logits_t, lp["moe_bias"].reshape(-1, 1), 512)
    pos, w8, te, nx, nt = _route_pos(gate_t, rank, cnt, 512, MOE_TM, n_slots // MOE_TM)
    xs = _sc_dispatch(pos.reshape(-1), hp, n_slots)
    after = yield "dispatched", None
    ys = _gmm(te.reshape(-1), nx.reshape(-1), nt.reshape(-1)[:1], xs, lp["moe_w1"], lp["moe_w3"], lp["moe_w2"], l,
              MOE_TM, after=after)
    yield "ffn", ys
    yk = _sc_gather(ys, pos.reshape(-1)).reshape(TOP_K, T, D // 2)
    x2 = _combine(yk, w8.T, hp, lp["sh_w1_b"], lp["sh_w3_b"], lp["sh_w2_b"], x1, mod,
                  lp["ln2_g"].reshape(1, -1), lp["ln2_b"].reshape(1, -1), rows_of(bmm), bmm)
    yield "done", (x2, proj, ckv_n)


def kernel(x_prompt, x_sample, cache_mla_ckv, cache_mla_kpe, cache_win_k, cache_win_v, cache_na_k, cache_na_v, c, c_ctx, w_ada, b_ada, w_in, mla_q_norm, mla_kv_norm, mla_w_uq, mla_w_ukv, hy_conv_w, hy_conv_b, hy_w1, hy_b1, hy_w2, hy_b2, hy_w3, hy_sin_freq, hy_log_decay, hy_skip, win_sink, na_rpb, w_branch, w_out, ln1_g, ln1_b, ln2_g, ln2_b, moe_router, moe_bias, moe_w1, moe_w3, moe_w2, sh_w1, sh_w3, sh_w2):
    B, S, _ = x_prompt.shape
    DB, DS, _ = x_sample.shape
    xp = x_prompt.reshape(B * S, D)
    xs = x_sample.reshape(DB * DS, D)
    cvec = jnp.concatenate([c_ctx[None, :], c, jnp.zeros((8 - 1 - DB, D), F32)], 0)
    dft_ctx = _dft_mats(S)
    dft_lat = _dft_mats(DS)
    tabs = _rope_tables(DS)
    projs, ckvs = [], []

    def params(l):
        return dict(w_in_p=_prep_w_in(w_in[l]), mla_q_norm=mla_q_norm[l], mla_kv_norm=mla_kv_norm[l],
                    mla_w=_prep_mla(mla_w_uq[l], mla_w_ukv[l]), hy_conv_w=hy_conv_w[l], hy_conv_b=hy_conv_b[l],
                    hy_w1=hy_w1[l], hy_b1=hy_b1[l], hy_w2=hy_w2[l], hy_b2=hy_b2[l], hy_w3=hy_w3[l],
                    hy_sin_freq=hy_sin_freq[l], hy_log_decay=hy_log_decay[l], hy_skip=hy_skip[l],
                    win_sink=win_sink[l], w_branch_b=w_branch[l].astype(BF16), w_out_b=w_out[l].astype(BF16),
                    ln1_g=ln1_g[l], ln1_b=ln1_b[l], ln2_g=ln2_g[l], ln2_b=ln2_b[l],
                    moe_router=moe_router[l], moe_bias=moe_bias[l], moe_w1=moe_w1, moe_w3=moe_w3, moe_w2=moe_w2,
                    sh_w1_b=sh_w1[l].astype(BF16), sh_w3_b=sh_w3[l].astype(BF16), sh_w2_b=sh_w2[l].astype(BF16))

    lps = [params(l) for l in range(DEPTH)]
    mods = [_modulation(cvec, w_ada, b_ada, l) for l in range(DEPTH)]

    def ctx_layer(l, x):
        return _layer_steps(x, mods[l], lps[l], l, B, S, lambda b: 0, dft_ctx)

    def lat_layer(l, x):
        cache = (cache_mla_ckv[:, l], cache_mla_kpe[:, l], cache_win_k[:, l], cache_win_v[:, l],
                 cache_na_k[:, l], cache_na_v[:, l])
        return _layer_steps(x, mods[l], lps[l], l, DB, DS, lambda b: 1 + b, dft_lat, cache=cache, tabs=tabs,
                            na_bias=_na_bias(na_rpb[l]))

    ctx = ctx_layer(0, xp)
    next(ctx)
    ctx.send(None)
    for l in range(DEPTH):
        lat = lat_layer(l, xs)
        lat_local = next(lat)[1]
        ys_ctx = ctx.send(lat_local)[1]
        lat.send(ys_ctx)
        xp, proj, ckv_n = ctx.send(None)[1]
        projs.append(proj)
        ckvs.append(ckv_n)
        if l + 1 < DEPTH:
            ctx = ctx_layer(l + 1, xp)
            attended = next(ctx)[1]
            ys_lat = lat.send(attended)[1]
            ctx.send(ys_lat)
        else:
            ckv, kpe, wk, wv, nk, nv = _emit_caches(projs, ckvs, B, S)
            lat.send((xp, nv))
        xs = lat.send(None)[1][0]
    heads = lambda t, h: t.reshape(B, DEPTH, S, h, HEAD_DIM)
    return (xp.reshape(B, S, D), xs.reshape(DB, DS, D), ckv, kpe, heads(wk, 2), heads(wv, 2), heads(nk, 4),
            heads(nv, 4))
```
